```python
import math
import jax, jax.numpy as jnp
from jax import lax
import numpy as np

D_MODEL = 1024
BATCH = 16
SEQ = 4096
DEPTH = 2

EPS = 1e-6
SSD_HEADS = 16
SSD_HEAD_DIM = 64
SSD_WIDTH = SSD_HEADS * SSD_HEAD_DIM
SSD_GROUPS = 2
SSD_STATE = 128
SSD_CONV = 4
SSD_CHUNK = 128
SSD_CONV_CH = SSD_WIDTH + 2 * SSD_GROUPS * SSD_STATE
DT_MIN = 1e-3
DT_MAX = 1e-1
POOL_GROUPS = 4
POOL_GROUP_DIM = 128
POOL_WIDTH = POOL_GROUPS * POOL_GROUP_DIM
POOL_WINDOWS = (2, 4, 8, 16)
MLA_HEADS = 8
MLA_Q_RANK = 384
MLA_KV_RANK = 256
MLA_NOPE = 64
MLA_ROPE = 32
MLA_V = 64
MLA_QK = MLA_NOPE + MLA_ROPE
MLA_WIDTH = MLA_HEADS * MLA_V
ROPE_THETA = 10000.0
Q_BLOCK = 128
MIX_WIDTH = SSD_WIDTH + POOL_WIDTH + MLA_WIDTH
IN_SIZES = (SSD_WIDTH, SSD_CONV_CH, SSD_HEADS, POOL_WIDTH, MLA_Q_RANK, MLA_KV_RANK, MLA_ROPE)
IN_COLS = SSD_WIDTH + SSD_CONV_CH + SSD_HEADS + POOL_WIDTH + MLA_Q_RANK + MLA_KV_RANK + MLA_ROPE
D_FF = 2816
FFN_CONV = 3

kernel_name = "hybrid_ssd_pool_mla_convffn"


def rmsnorm(x, w):
    xf = x.astype(jnp.float32)
    var = jnp.mean(xf * xf, axis=-1, keepdims=True)
    return (xf * lax.rsqrt(var + EPS)).astype(x.dtype) * w


def causal_dwconv(x, w, b):
    K = w.shape[0]
    S = x.shape[1]
    xp = jnp.pad(x, ((0, 0), (K - 1, 0), (0, 0)))
    acc = xp[:, 0:S] * w[0] + b
    for k in range(1, K):
        acc = acc + xp[:, k:k + S] * w[k]
    return acc


def rope(x, cos, sin):
    x1, x2 = jnp.split(x, 2, axis=-1)
    return jnp.concatenate([x1 * cos - x2 * sin, x1 * sin + x2 * cos], axis=-1)


def rope_tables(positions):
    inv_freq = ROPE_THETA ** (-jnp.arange(0, MLA_ROPE, 2, dtype=jnp.float32) / MLA_ROPE)
    ang = positions.astype(jnp.float32)[..., None] * inv_freq
    return jnp.cos(ang), jnp.sin(ang)


def ssd_mixer(z, xbc, dt_raw, conv_w, conv_b, dt_bias, a_log, d_skip, norm_w):
    Bsz, S, _ = xbc.shape
    G, E, P, N, L = SSD_GROUPS, SSD_HEADS // SSD_GROUPS, SSD_HEAD_DIM, SSD_STATE, SSD_CHUNK
    nc = S // L
    xbc = jax.nn.silu(causal_dwconv(xbc, conv_w, conv_b))
    xs, bs, cs = jnp.split(xbc, [SSD_WIDTH, SSD_WIDTH + G * N], axis=-1)
    xs = xs.reshape(Bsz, nc, L, G, E, P)
    bs = bs.reshape(Bsz, nc, L, G, N)
    cs = cs.reshape(Bsz, nc, L, G, N)
    dt = jax.nn.softplus((dt_raw + dt_bias).astype(jnp.float32)).reshape(Bsz, nc, L, G, E)
    a = -jnp.exp(a_log.astype(jnp.float32)).reshape(G, E)
    da = dt * a
    xdt = xs * dt[..., None]
    da_cum = jnp.cumsum(da, axis=2)
    causal = jnp.tril(jnp.ones((L, L), dtype=bool))
    seg = da_cum[:, :, :, None] - da_cum[:, :, None, :]
    decay = jnp.exp(jnp.where(causal[None, None, :, :, None, None], seg, -jnp.inf))
    cb = jnp.einsum("bclgn,bcsgn->bclsg", cs, bs)
    y_diag = jnp.einsum("bclsg,bclsge,bcsgep->bclgep", cb, decay, xdt)
    decay_to_end = jnp.exp(da_cum[:, :, -1:] - da_cum)
    chunk_states = jnp.einsum("bclgn,bclge,bclgep->bcgepn", bs, decay_to_end, xdt)
    chunk_decay = jnp.exp(da_cum[:, :, -1])

    def step(h, inp):
        dec, st = inp
        return h * dec[..., None, None] + st, h

    h0 = jnp.zeros((Bsz, G, E, P, N), dtype=chunk_states.dtype)
    _, h_in = lax.scan(step, h0, (jnp.moveaxis(chunk_decay, 1, 0), jnp.moveaxis(chunk_states, 1, 0)))
    h_in = jnp.moveaxis(h_in, 0, 1)
    y_off = jnp.einsum("bclgn,bcgepn,bclge->bclgep", cs, h_in, jnp.exp(da_cum))
    y = y_diag + y_off + xs * d_skip.reshape(G, E)[:, :, None]
    y = y.reshape(Bsz, S, SSD_WIDTH)
    return rmsnorm(y * jax.nn.silu(z), norm_w)


def pool_mixer(u, pool_w, pool_scale):
    Bsz, S, _ = u.shape
    uf = u.astype(jnp.float32)
    csum = jnp.pad(jnp.cumsum(uf, axis=1), ((0, 0), (1, 0), (0, 0)))
    count = jnp.arange(1, S + 1, dtype=jnp.float32)[:, None]
    means = []
    for gi, w in enumerate(POOL_WINDOWS):
        c = csum[:, :, gi * POOL_GROUP_DIM:(gi + 1) * POOL_GROUP_DIM]
        lag = jnp.pad(c, ((0, 0), (w - 1, 0), (0, 0)))[:, :S]
        means.append((c[:, 1:] - lag) / jnp.minimum(count, float(w)))
    pooled = (jnp.concatenate(means, axis=-1) - uf).astype(u.dtype)
    pooled = pooled.reshape(Bsz, S, POOL_GROUPS, POOL_GROUP_DIM)
    y = jnp.einsum("bsgc,gcd->bsgd", pooled, pool_w).reshape(Bsz, S, POOL_WIDTH)
    return y * pool_scale


def mla_mixer(c_q, c_kv, k_pe, cos, sin, q_norm, w_uq, kv_norm, w_ukv):
    Bsz, S, _ = c_q.shape
    H = MLA_HEADS
    q = (rmsnorm(c_q, q_norm) @ w_uq).reshape(Bsz, S, H, MLA_QK)
    kv = (rmsnorm(c_kv, kv_norm) @ w_ukv).reshape(Bsz, S, H, MLA_NOPE + MLA_V)
    q_nope, q_pe = jnp.split(q, [MLA_NOPE], axis=-1)
    k_nope, v = jnp.split(kv, [MLA_NOPE], axis=-1)
    q_pe = rope(q_pe, cos[:, :, None, :], sin[:, :, None, :])
    k_pe = rope(k_pe, cos, sin)
    q = jnp.concatenate([q_nope, q_pe], axis=-1)
    k = jnp.concatenate([k_nope, jnp.broadcast_to(k_pe[:, :, None, :], (Bsz, S, H, MLA_ROPE))], axis=-1)
    scale = 1.0 / math.sqrt(MLA_QK)
    nb = S // Q_BLOCK
    qb = jnp.moveaxis(q.reshape(Bsz, nb, Q_BLOCK, H, MLA_QK), 1, 0)
    key_pos = jnp.arange(S)

    def attend(args):
        q_blk, i = args
        s = jnp.einsum("bqhd,bkhd->bhqk", q_blk, k).astype(jnp.float32) * scale
        q_pos = i * Q_BLOCK + jnp.arange(Q_BLOCK)
        s = jnp.where(key_pos[None, :] <= q_pos[:, None], s, -jnp.inf)
        p = jax.nn.softmax(s, axis=-1).astype(v.dtype)
        return jnp.einsum("bhqk,bkhd->bqhd", p, v)

    o = lax.map(attend, (qb, jnp.arange(nb)))
    return jnp.moveaxis(o, 0, 1).reshape(Bsz, S, MLA_WIDTH)


def conv_ffn(h, w_up, conv_w, conv_b, w_down):
    up = causal_dwconv(h @ w_up, conv_w, conv_b)
    gate, val = jnp.split(up, 2, axis=-1)
    return (jax.nn.silu(gate) * val) @ w_down


def _fwd_setup_inputs(seed: int = 0) -> dict:
    key = jax.random.key(seed)
    ks = jax.random.split(key, 24)
    f32 = jnp.float32

    def nrm(k, shape, scale):
        return jax.random.normal(k, shape, f32) * scale

    def gain(k, shape):
        return 1.0 + 0.02 * jax.random.normal(k, shape, f32)

    x = jax.random.normal(ks[0], (BATCH, SEQ, D_MODEL), f32)
    offsets = jax.random.randint(ks[1], (BATCH, 1), 0, 1024, dtype=jnp.int32)
    positions = (offsets + jnp.arange(SEQ, dtype=jnp.int32)[None, :]).astype(jnp.int32)
    u_dt = jax.random.uniform(ks[2], (DEPTH, SSD_HEADS), f32)
    dt0 = jnp.exp(u_dt * (math.log(DT_MAX) - math.log(DT_MIN)) + math.log(DT_MIN))
    dt_bias = dt0 + jnp.log(-jnp.expm1(-dt0))
    a_log = jnp.log(jax.random.uniform(ks[3], (DEPTH, SSD_HEADS), f32, 1.0, 16.0))
    return {
        "x": x,
        "positions": positions,
        "attn_norm": gain(ks[4], (DEPTH, D_MODEL)),
        "w_in": nrm(ks[5], (DEPTH, D_MODEL, IN_COLS), D_MODEL ** -0.5),
        "ssd_conv_w": nrm(ks[6], (DEPTH, SSD_CONV, SSD_CONV_CH), SSD_CONV ** -0.5),
        "ssd_conv_b": nrm(ks[7], (DEPTH, SSD_CONV_CH), 0.02),
        "ssd_dt_bias": dt_bias,
        "ssd_a_log": a_log,
        "ssd_d": 1.0 + 0.1 * jax.random.normal(ks[8], (DEPTH, SSD_HEADS), f32),
        "ssd_norm": gain(ks[9], (DEPTH, SSD_WIDTH)),
        "pool_w": nrm(ks[10], (DEPTH, POOL_GROUPS, POOL_GROUP_DIM, POOL_GROUP_DIM), POOL_GROUP_DIM ** -0.5),
        "pool_scale": gain(ks[11], (DEPTH, POOL_WIDTH)),
        "mla_q_norm": gain(ks[12], (DEPTH, MLA_Q_RANK)),
        "mla_w_uq": nrm(ks[13], (DEPTH, MLA_Q_RANK, MLA_HEADS * MLA_QK), MLA_Q_RANK ** -0.5),
        "mla_kv_norm": gain(ks[14], (DEPTH, MLA_KV_RANK)),
        "mla_w_ukv": nrm(ks[15], (DEPTH, MLA_KV_RANK, MLA_HEADS * (MLA_NOPE + MLA_V)), MLA_KV_RANK ** -0.5),
        "w_out": nrm(ks[16], (DEPTH, MIX_WIDTH, D_MODEL), MIX_WIDTH ** -0.5),
        "ffn_norm": gain(ks[17], (DEPTH, D_MODEL)),
        "ffn_w_up": nrm(ks[18], (DEPTH, D_MODEL, 2 * D_FF), D_MODEL ** -0.5),
        "ffn_conv_w": nrm(ks[19], (DEPTH, FFN_CONV, 2 * D_FF), FFN_CONV ** -0.5),
        "ffn_conv_b": nrm(ks[20], (DEPTH, 2 * D_FF), 0.02),
        "ffn_w_down": nrm(ks[21], (DEPTH, D_FF, D_MODEL), D_FF ** -0.5),
        "final_norm": gain(ks[22], (D_MODEL,)),
    }


def _fwd_reference(x, positions, attn_norm, w_in, ssd_conv_w, ssd_conv_b, ssd_dt_bias, ssd_a_log,
              ssd_d, ssd_norm, pool_w, pool_scale, mla_q_norm, mla_w_uq, mla_kv_norm, mla_w_ukv,
              w_out, ffn_norm, ffn_w_up, ffn_conv_w, ffn_conv_b, ffn_w_down, final_norm):
    cos, sin = rope_tables(positions)
    splits = [int(s) for s in np.cumsum(IN_SIZES)[:-1]]
    for l in range(DEPTH):
        h = rmsnorm(x, attn_norm[l])
        proj = h @ w_in[l]
        z, xbc, dt_raw, u, c_q, c_kv, k_pe = jnp.split(proj, splits, axis=-1)
        y_ssd = ssd_mixer(z, xbc, dt_raw, ssd_conv_w[l], ssd_conv_b[l], ssd_dt_bias[l],
                          ssd_a_log[l], ssd_d[l], ssd_norm[l])
        y_pool = pool_mixer(u, pool_w[l], pool_scale[l])
        y_mla = mla_mixer(c_q, c_kv, k_pe, cos, sin, mla_q_norm[l], mla_w_uq[l],
                          mla_kv_norm[l], mla_w_ukv[l])
        x = x + jnp.concatenate([y_ssd, y_pool, y_mla], axis=-1) @ w_out[l]
        h = rmsnorm(x, ffn_norm[l])
        x = x + conv_ffn(h, ffn_w_up[l], ffn_conv_w[l], ffn_conv_b[l], ffn_w_down[l])
    return rmsnorm(x, final_norm)


import jax as _jax
import jax.numpy as _jnp

TWIN_FORMAT = 'train_step'
FWD_PARAMS = ['x', 'positions', 'attn_norm', 'w_in', 'ssd_conv_w', 'ssd_conv_b', 'ssd_dt_bias', 'ssd_a_log', 'ssd_d', 'ssd_norm', 'pool_w', 'pool_scale', 'mla_q_norm', 'mla_w_uq', 'mla_kv_norm', 'mla_w_ukv', 'w_out', 'ffn_norm', 'ffn_w_up', 'ffn_conv_w', 'ffn_conv_b', 'ffn_w_down', 'final_norm']
TWIN_WEIGHTS = ['attn_norm', 'w_in', 'ssd_conv_w', 'ssd_conv_b', 'ssd_dt_bias', 'ssd_a_log', 'ssd_d', 'ssd_norm', 'pool_w', 'pool_scale', 'mla_q_norm', 'mla_w_uq', 'mla_kv_norm', 'mla_w_ukv', 'w_out', 'ffn_norm', 'ffn_w_up', 'ffn_conv_w', 'ffn_conv_b', 'ffn_w_down', 'final_norm']
TWIN_DIFF_INPUT = 'x'
TWIN_INPUTS = ['x', 'positions', 'attn_norm', 'w_in', 'ssd_conv_w', 'ssd_conv_b', 'ssd_dt_bias', 'ssd_a_log', 'ssd_d', 'ssd_norm', 'pool_w', 'pool_scale', 'mla_q_norm', 'mla_w_uq', 'mla_kv_norm', 'mla_w_ukv', 'w_out', 'ffn_norm', 'ffn_w_up', 'ffn_conv_w', 'ffn_conv_b', 'ffn_w_down', 'final_norm', 'loss_target', 'm_attn_norm', 'm_w_in', 'm_ssd_conv_w', 'm_ssd_conv_b', 'm_ssd_dt_bias', 'm_ssd_a_log', 'm_ssd_d', 'm_ssd_norm', 'm_pool_w', 'm_pool_scale', 'm_mla_q_norm', 'm_mla_w_uq', 'm_mla_kv_norm', 'm_mla_w_ukv', 'm_w_out', 'm_ffn_norm', 'm_ffn_w_up', 'm_ffn_conv_w', 'm_ffn_conv_b', 'm_ffn_w_down', 'm_final_norm', 'v_attn_norm', 'v_w_in', 'v_ssd_conv_w', 'v_ssd_conv_b', 'v_ssd_dt_bias', 'v_ssd_a_log', 'v_ssd_d', 'v_ssd_norm', 'v_pool_w', 'v_pool_scale', 'v_mla_q_norm', 'v_mla_w_uq', 'v_mla_kv_norm', 'v_mla_w_ukv', 'v_w_out', 'v_ffn_norm', 'v_ffn_w_up', 'v_ffn_conv_w', 'v_ffn_conv_b', 'v_ffn_w_down', 'v_final_norm']
TWIN_OUTPUTS = ['loss', 'grad_x', 'grad_attn_norm', 'grad_w_in', 'grad_ssd_conv_w', 'grad_ssd_conv_b', 'grad_ssd_dt_bias', 'grad_ssd_a_log', 'grad_ssd_d', 'grad_ssd_norm', 'grad_pool_w', 'grad_pool_scale', 'grad_mla_q_norm', 'grad_mla_w_uq', 'grad_mla_kv_norm', 'grad_mla_w_ukv', 'grad_w_out', 'grad_ffn_norm', 'grad_ffn_w_up', 'grad_ffn_conv_w', 'grad_ffn_conv_b', 'grad_ffn_w_down', 'grad_final_norm', 'delta_attn_norm', 'delta_w_in', 'delta_ssd_conv_w', 'delta_ssd_conv_b', 'delta_ssd_dt_bias', 'delta_ssd_a_log', 'delta_ssd_d', 'delta_ssd_norm', 'delta_pool_w', 'delta_pool_scale', 'delta_mla_q_norm', 'delta_mla_w_uq', 'delta_mla_kv_norm', 'delta_mla_w_ukv', 'delta_w_out', 'delta_ffn_norm', 'delta_ffn_w_up', 'delta_ffn_conv_w', 'delta_ffn_conv_b', 'delta_ffn_w_down', 'delta_final_norm', 'new_m_attn_norm', 'new_m_w_in', 'new_m_ssd_conv_w', 'new_m_ssd_conv_b', 'new_m_ssd_dt_bias', 'new_m_ssd_a_log', 'new_m_ssd_d', 'new_m_ssd_norm', 'new_m_pool_w', 'new_m_pool_scale', 'new_m_mla_q_norm', 'new_m_mla_w_uq', 'new_m_mla_kv_norm', 'new_m_mla_w_ukv', 'new_m_w_out', 'new_m_ffn_norm', 'new_m_ffn_w_up', 'new_m_ffn_conv_w', 'new_m_ffn_conv_b', 'new_m_ffn_w_down', 'new_m_final_norm', 'new_v_attn_norm', 'new_v_w_in', 'new_v_ssd_conv_w', 'new_v_ssd_conv_b', 'new_v_ssd_dt_bias', 'new_v_ssd_a_log', 'new_v_ssd_d', 'new_v_ssd_norm', 'new_v_pool_w', 'new_v_pool_scale', 'new_v_mla_q_norm', 'new_v_mla_w_uq', 'new_v_mla_kv_norm', 'new_v_mla_w_ukv', 'new_v_w_out', 'new_v_ffn_norm', 'new_v_ffn_w_up', 'new_v_ffn_conv_w', 'new_v_ffn_conv_b', 'new_v_ffn_w_down', 'new_v_final_norm']
TWIN_LEAF_KINDS = {'loss': 'loss', 'grad_x': 'grad_x', 'grad_attn_norm': 'grad_w', 'grad_w_in': 'grad_w', 'grad_ssd_conv_w': 'grad_w', 'grad_ssd_conv_b': 'grad_w', 'grad_ssd_dt_bias': 'grad_w', 'grad_ssd_a_log': 'grad_w', 'grad_ssd_d': 'grad_w', 'grad_ssd_norm': 'grad_w', 'grad_pool_w': 'grad_w', 'grad_pool_scale': 'grad_w', 'grad_mla_q_norm': 'grad_w', 'grad_mla_w_uq': 'grad_w', 'grad_mla_kv_norm': 'grad_w', 'grad_mla_w_ukv': 'grad_w', 'grad_w_out': 'grad_w', 'grad_ffn_norm': 'grad_w', 'grad_ffn_w_up': 'grad_w', 'grad_ffn_conv_w': 'grad_w', 'grad_ffn_conv_b': 'grad_w', 'grad_ffn_w_down': 'grad_w', 'grad_final_norm': 'grad_w', 'delta_attn_norm': 'delta_w', 'delta_w_in': 'delta_w', 'delta_ssd_conv_w': 'delta_w', 'delta_ssd_conv_b': 'delta_w', 'delta_ssd_dt_bias': 'delta_w', 'delta_ssd_a_log': 'delta_w', 'delta_ssd_d': 'delta_w', 'delta_ssd_norm': 'delta_w', 'delta_pool_w': 'delta_w', 'delta_pool_scale': 'delta_w', 'delta_mla_q_norm': 'delta_w', 'delta_mla_w_uq': 'delta_w', 'delta_mla_kv_norm': 'delta_w', 'delta_mla_w_ukv': 'delta_w', 'delta_w_out': 'delta_w', 'delta_ffn_norm': 'delta_w', 'delta_ffn_w_up': 'delta_w', 'delta_ffn_conv_w': 'delta_w', 'delta_ffn_conv_b': 'delta_w', 'delta_ffn_w_down': 'delta_w', 'delta_final_norm': 'delta_w', 'new_m_attn_norm': 'new_m', 'new_m_w_in': 'new_m', 'new_m_ssd_conv_w': 'new_m', 'new_m_ssd_conv_b': 'new_m', 'new_m_ssd_dt_bias': 'new_m', 'new_m_ssd_a_log': 'new_m', 'new_m_ssd_d': 'new_m', 'new_m_ssd_norm': 'new_m', 'new_m_pool_w': 'new_m', 'new_m_pool_scale': 'new_m', 'new_m_mla_q_norm': 'new_m', 'new_m_mla_w_uq': 'new_m', 'new_m_mla_kv_norm': 'new_m', 'new_m_mla_w_ukv': 'new_m', 'new_m_w_out': 'new_m', 'new_m_ffn_norm': 'new_m', 'new_m_ffn_w_up': 'new_m', 'new_m_ffn_conv_w': 'new_m', 'new_m_ffn_conv_b': 'new_m', 'new_m_ffn_w_down': 'new_m', 'new_m_final_norm': 'new_m', 'new_v_attn_norm': 'new_v', 'new_v_w_in': 'new_v', 'new_v_ssd_conv_w': 'new_v', 'new_v_ssd_conv_b': 'new_v', 'new_v_ssd_dt_bias': 'new_v', 'new_v_ssd_a_log': 'new_v', 'new_v_ssd_d': 'new_v', 'new_v_ssd_norm': 'new_v', 'new_v_pool_w': 'new_v', 'new_v_pool_scale': 'new_v', 'new_v_mla_q_norm': 'new_v', 'new_v_mla_w_uq': 'new_v', 'new_v_mla_kv_norm': 'new_v', 'new_v_mla_w_ukv': 'new_v', 'new_v_w_out': 'new_v', 'new_v_ffn_norm': 'new_v', 'new_v_ffn_w_up': 'new_v', 'new_v_ffn_conv_w': 'new_v', 'new_v_ffn_conv_b': 'new_v', 'new_v_ffn_w_down': 'new_v', 'new_v_final_norm': 'new_v'}


def _forward(args):
    return _fwd_reference(*[args[k] for k in FWD_PARAMS])


def _output_shape():
    out = _jax.eval_shape(lambda: _forward(_fwd_setup_inputs(0)))
    return out.shape, out.dtype

N_MICROBATCH = 1
ADAM_LR = 0.001
ADAM_B1 = 0.9
ADAM_B2 = 0.999
ADAM_EPS = 1e-08
ADAM_WD = 0.01
ADAM_STEP = 10
PER_EXAMPLE_BATCH_AXIS = {'x': 0, 'positions': 0, 'loss_target': 0}
SHARED_INPUTS = []
_WEIGHT_DTYPES = {'attn_norm': _jnp.float32, 'w_in': _jnp.float32, 'ssd_conv_w': _jnp.float32, 'ssd_conv_b': _jnp.float32, 'ssd_dt_bias': _jnp.float32, 'ssd_a_log': _jnp.float32, 'ssd_d': _jnp.float32, 'ssd_norm': _jnp.float32, 'pool_w': _jnp.float32, 'pool_scale': _jnp.float32, 'mla_q_norm': _jnp.float32, 'mla_w_uq': _jnp.float32, 'mla_kv_norm': _jnp.float32, 'mla_w_ukv': _jnp.float32, 'w_out': _jnp.float32, 'ffn_norm': _jnp.float32, 'ffn_w_up': _jnp.float32, 'ffn_conv_w': _jnp.float32, 'ffn_conv_b': _jnp.float32, 'ffn_w_down': _jnp.float32, 'final_norm': _jnp.float32}
MOMENT_SCALE = {'attn_norm': 2.400376e-01, 'w_in': 1.268497e-01, 'ssd_conv_w': 1.298669e-01, 'ssd_conv_b': 1.990375e-01, 'ssd_dt_bias': 3.804062e-01, 'ssd_a_log': 1.269558e+00, 'ssd_d': 7.725423e-01, 'ssd_norm': 1.553342e-01, 'pool_w': 1.280927e-01, 'pool_scale': 1.232647e-01, 'mla_q_norm': 3.187556e-02, 'mla_w_uq': 2.176084e-02, 'mla_kv_norm': 5.866096e-02, 'mla_w_ukv': 2.904966e-02, 'w_out': 1.744932e-01, 'ffn_norm': 1.713855e-01, 'ffn_w_up': 6.552749e-02, 'ffn_conv_w': 6.522694e-02, 'ffn_conv_b': 6.772893e-02, 'ffn_w_down': 1.074874e-01, 'final_norm': 6.376006e+01}


def _to_microbatches(a, axis):
    t = _jnp.moveaxis(a, axis, 0)
    t = t.reshape((N_MICROBATCH, t.shape[0] // N_MICROBATCH) + t.shape[1:])
    return _jnp.moveaxis(t, 1, axis + 1)


def setup_inputs(seed: int = 0) -> dict:
    inp = _fwd_setup_inputs(seed)
    key = _jax.random.fold_in(_jax.random.key(seed), 7919)
    shape, _ = _output_shape()
    out = dict(inp)
    out["loss_target"] = _jax.random.normal(_jax.random.fold_in(key, 0), shape, _jnp.float32)
    for i, name in enumerate(TWIN_WEIGHTS):
        w = inp[name].astype(_jnp.float32)
        if MOMENT_SCALE is None:
            s = _jnp.sqrt(_jnp.mean(_jnp.square(w)) + 1e-30)
        else:
            s = MOMENT_SCALE[name]
        km, kv = _jax.random.split(_jax.random.fold_in(key, i + 1))
        out[name] = w
        out["m_" + name] = s * _jax.random.normal(km, w.shape, _jnp.float32)
        out["v_" + name] = (s * s) * _jax.random.uniform(kv, w.shape, _jnp.float32, 0.5, 1.5)
    if N_MICROBATCH > 1:
        for name, axis in PER_EXAMPLE_BATCH_AXIS.items():
            out[name] = _to_microbatches(out[name], axis)
    return {'x': out['x'], 'positions': out['positions'], 'attn_norm': out['attn_norm'], 'w_in': out['w_in'], 'ssd_conv_w': out['ssd_conv_w'], 'ssd_conv_b': out['ssd_conv_b'], 'ssd_dt_bias': out['ssd_dt_bias'], 'ssd_a_log': out['ssd_a_log'], 'ssd_d': out['ssd_d'], 'ssd_norm': out['ssd_norm'], 'pool_w': out['pool_w'], 'pool_scale': out['pool_scale'], 'mla_q_norm': out['mla_q_norm'], 'mla_w_uq': out['mla_w_uq'], 'mla_kv_norm': out['mla_kv_norm'], 'mla_w_ukv': out['mla_w_ukv'], 'w_out': out['w_out'], 'ffn_norm': out['ffn_norm'], 'ffn_w_up': out['ffn_w_up'], 'ffn_conv_w': out['ffn_conv_w'], 'ffn_conv_b': out['ffn_conv_b'], 'ffn_w_down': out['ffn_w_down'], 'final_norm': out['final_norm'], 'loss_target': out['loss_target'], 'm_attn_norm': out['m_attn_norm'], 'm_w_in': out['m_w_in'], 'm_ssd_conv_w': out['m_ssd_conv_w'], 'm_ssd_conv_b': out['m_ssd_conv_b'], 'm_ssd_dt_bias': out['m_ssd_dt_bias'], 'm_ssd_a_log': out['m_ssd_a_log'], 'm_ssd_d': out['m_ssd_d'], 'm_ssd_norm': out['m_ssd_norm'], 'm_pool_w': out['m_pool_w'], 'm_pool_scale': out['m_pool_scale'], 'm_mla_q_norm': out['m_mla_q_norm'], 'm_mla_w_uq': out['m_mla_w_uq'], 'm_mla_kv_norm': out['m_mla_kv_norm'], 'm_mla_w_ukv': out['m_mla_w_ukv'], 'm_w_out': out['m_w_out'], 'm_ffn_norm': out['m_ffn_norm'], 'm_ffn_w_up': out['m_ffn_w_up'], 'm_ffn_conv_w': out['m_ffn_conv_w'], 'm_ffn_conv_b': out['m_ffn_conv_b'], 'm_ffn_w_down': out['m_ffn_w_down'], 'm_final_norm': out['m_final_norm'], 'v_attn_norm': out['v_attn_norm'], 'v_w_in': out['v_w_in'], 'v_ssd_conv_w': out['v_ssd_conv_w'], 'v_ssd_conv_b': out['v_ssd_conv_b'], 'v_ssd_dt_bias': out['v_ssd_dt_bias'], 'v_ssd_a_log': out['v_ssd_a_log'], 'v_ssd_d': out['v_ssd_d'], 'v_ssd_norm': out['v_ssd_norm'], 'v_pool_w': out['v_pool_w'], 'v_pool_scale': out['v_pool_scale'], 'v_mla_q_norm': out['v_mla_q_norm'], 'v_mla_w_uq': out['v_mla_w_uq'], 'v_mla_kv_norm': out['v_mla_kv_norm'], 'v_mla_w_ukv': out['v_mla_w_ukv'], 'v_w_out': out['v_w_out'], 'v_ffn_norm': out['v_ffn_norm'], 'v_ffn_w_up': out['v_ffn_w_up'], 'v_ffn_conv_w': out['v_ffn_conv_w'], 'v_ffn_conv_b': out['v_ffn_conv_b'], 'v_ffn_w_down': out['v_ffn_w_down'], 'v_final_norm': out['v_final_norm']}


def _loss(weights, diff, rest, loss_target):
    with _jax.named_scope("forward"):
        args = {**rest, TWIN_DIFF_INPUT: diff, **{k: w.astype(_WEIGHT_DTYPES[k]) for k, w in weights.items()}}
        y = _forward(args)
    with _jax.named_scope("loss_head"):
        err = _jnp.square(y.astype(_jnp.float32) - loss_target)
        return 0.5 * _jnp.sum(_jnp.mean(err, axis=-1)) if err.ndim else 0.5 * err


def _adamw(w, g, m, v):
    m = ADAM_B1 * m + (1.0 - ADAM_B1) * g
    v = ADAM_B2 * v + (1.0 - ADAM_B2) * _jnp.square(g)
    m_hat = m / (1.0 - ADAM_B1 ** ADAM_STEP)
    v_hat = v / (1.0 - ADAM_B2 ** ADAM_STEP)
    delta = -ADAM_LR * (m_hat / (_jnp.sqrt(v_hat) + ADAM_EPS) + ADAM_WD * w)
    return delta, m, v


def reference(x, positions, attn_norm, w_in, ssd_conv_w, ssd_conv_b, ssd_dt_bias, ssd_a_log, ssd_d, ssd_norm, pool_w, pool_scale, mla_q_norm, mla_w_uq, mla_kv_norm, mla_w_ukv, w_out, ffn_norm, ffn_w_up, ffn_conv_w, ffn_conv_b, ffn_w_down, final_norm, loss_target, m_attn_norm, m_w_in, m_ssd_conv_w, m_ssd_conv_b, m_ssd_dt_bias, m_ssd_a_log, m_ssd_d, m_ssd_norm, m_pool_w, m_pool_scale, m_mla_q_norm, m_mla_w_uq, m_mla_kv_norm, m_mla_w_ukv, m_w_out, m_ffn_norm, m_ffn_w_up, m_ffn_conv_w, m_ffn_conv_b, m_ffn_w_down, m_final_norm, v_attn_norm, v_w_in, v_ssd_conv_w, v_ssd_conv_b, v_ssd_dt_bias, v_ssd_a_log, v_ssd_d, v_ssd_norm, v_pool_w, v_pool_scale, v_mla_q_norm, v_mla_w_uq, v_mla_kv_norm, v_mla_w_ukv, v_w_out, v_ffn_norm, v_ffn_w_up, v_ffn_conv_w, v_ffn_conv_b, v_ffn_w_down, v_final_norm):
    given = dict(x=x, positions=positions, attn_norm=attn_norm, w_in=w_in, ssd_conv_w=ssd_conv_w, ssd_conv_b=ssd_conv_b, ssd_dt_bias=ssd_dt_bias, ssd_a_log=ssd_a_log, ssd_d=ssd_d, ssd_norm=ssd_norm, pool_w=pool_w, pool_scale=pool_scale, mla_q_norm=mla_q_norm, mla_w_uq=mla_w_uq, mla_kv_norm=mla_kv_norm, mla_w_ukv=mla_w_ukv, w_out=w_out, ffn_norm=ffn_norm, ffn_w_up=ffn_w_up, ffn_conv_w=ffn_conv_w, ffn_conv_b=ffn_conv_b, ffn_w_down=ffn_w_down, final_norm=final_norm, loss_target=loss_target, m_attn_norm=m_attn_norm, m_w_in=m_w_in, m_ssd_conv_w=m_ssd_conv_w, m_ssd_conv_b=m_ssd_conv_b, m_ssd_dt_bias=m_ssd_dt_bias, m_ssd_a_log=m_ssd_a_log, m_ssd_d=m_ssd_d, m_ssd_norm=m_ssd_norm, m_pool_w=m_pool_w, m_pool_scale=m_pool_scale, m_mla_q_norm=m_mla_q_norm, m_mla_w_uq=m_mla_w_uq, m_mla_kv_norm=m_mla_kv_norm, m_mla_w_ukv=m_mla_w_ukv, m_w_out=m_w_out, m_ffn_norm=m_ffn_norm, m_ffn_w_up=m_ffn_w_up, m_ffn_conv_w=m_ffn_conv_w, m_ffn_conv_b=m_ffn_conv_b, m_ffn_w_down=m_ffn_w_down, m_final_norm=m_final_norm, v_attn_norm=v_attn_norm, v_w_in=v_w_in, v_ssd_conv_w=v_ssd_conv_w, v_ssd_conv_b=v_ssd_conv_b, v_ssd_dt_bias=v_ssd_dt_bias, v_ssd_a_log=v_ssd_a_log, v_ssd_d=v_ssd_d, v_ssd_norm=v_ssd_norm, v_pool_w=v_pool_w, v_pool_scale=v_pool_scale, v_mla_q_norm=v_mla_q_norm, v_mla_w_uq=v_mla_w_uq, v_mla_kv_norm=v_mla_kv_norm, v_mla_w_ukv=v_mla_w_ukv, v_w_out=v_w_out, v_ffn_norm=v_ffn_norm, v_ffn_w_up=v_ffn_w_up, v_ffn_conv_w=v_ffn_conv_w, v_ffn_conv_b=v_ffn_conv_b, v_ffn_w_down=v_ffn_w_down, v_final_norm=v_final_norm)
    weights = {n: given[n] for n in TWIN_WEIGHTS}
    shared = {n: given[n] for n in SHARED_INPUTS}
    per_example = {n: given[n] for n in ['x', 'positions']}
    grad_fn = _jax.value_and_grad(_loss, argnums=(0, 1))

    def one_microbatch(ex, loss_target):
        ex = dict(ex)
        diff = ex.pop(TWIN_DIFF_INPUT)
        return grad_fn(weights, diff, {**shared, **ex}, loss_target)

    if N_MICROBATCH == 1:
        loss, (grad_w, grad_x) = one_microbatch(per_example, given["loss_target"])
    else:
        def body(carry, xs):
            loss_sum, grad_sum = carry
            l_k, (gw_k, gx_k) = one_microbatch(xs[0], xs[1])
            with _jax.named_scope("update"):
                return (loss_sum + l_k, _jax.tree.map(_jnp.add, grad_sum, gw_k)), gx_k

        init = (_jnp.zeros((), _jnp.float32), _jax.tree.map(_jnp.zeros_like, weights))
        (loss, grad_w), grad_x = _jax.lax.scan(body, init, (per_example, given["loss_target"]))
    with _jax.named_scope("update"):
        delta_w, new_m, new_v = {}, {}, {}
        for n in TWIN_WEIGHTS:
            delta_w[n], new_m[n], new_v[n] = _adamw(weights[n], grad_w[n], given["m_" + n], given["v_" + n])
    return (loss, grad_x, *[grad_w[n] for n in TWIN_WEIGHTS], *[delta_w[n] for n in TWIN_WEIGHTS],
            *[new_m[n] for n in TWIN_WEIGHTS], *[new_v[n] for n in TWIN_WEIGHTS])
```

```python
import functools
import math

import jax
import jax.numpy as jnp
from jax import lax
from jax.experimental import pallas as pl
from jax.experimental.pallas import tpu as pltpu

F32 = jnp.float32
BF16 = jnp.bfloat16
MXU_DTYPE = jnp.bfloat16
HI = lax.Precision.HIGHEST

D_MODEL = 1024
DEPTH = 2
EPS = 1e-6
SSD_HEADS = 16
SSD_HEAD_DIM = 64
SSD_WIDTH = 1024
SSD_GROUPS = 2
SSD_STATE = 128
SSD_CONV = 4
SSD_CHUNK = 128
SSD_CONV_CH = 1536
POOL_GROUPS = 4
POOL_GROUP_DIM = 128
POOL_WIDTH = 512
POOL_WINDOWS = (2, 4, 8, 16)
MLA_HEADS = 8
MLA_Q_RANK = 384
MLA_KV_RANK = 256
MLA_NOPE = 64
MLA_ROPE = 32
MLA_V = 64
MLA_QK = 96
MLA_WIDTH = 512
ROPE_THETA = 10000.0
MIX_WIDTH = 2048
IN_COLS = 3760
D_FF = 2816
FFN_CONV = 3
ADAM_LR = 0.001
ADAM_B1 = 0.9
ADAM_B2 = 0.999
ADAM_EPS = 1e-08
ADAM_WD = 0.01
ADAM_STEP = 10

LANE = 128
HALO = 8
POOL_HALO = 16
PZ0 = 0
PXBC0 = 1024
PU0 = 2560
PCQ0 = 3072
PDT0 = 3456
PCKV0 = 3584
PKPE0 = 3840
PROJ_W = 4096
HEAD_W = 128
MLA_PAD = MLA_HEADS * HEAD_W
MIXP = SSD_WIDTH + POOL_WIDTH + MLA_PAD
N_CHIPS = 4
N_DEV = 8
VMEM_LIMIT = 56 * 1024 * 1024


def _cparams(dims, vmem=None):
    return pltpu.CompilerParams(dimension_semantics=dims, vmem_limit_bytes=vmem or VMEM_LIMIT)


def _sds(shape, dtype):
    return jax.ShapeDtypeStruct(tuple(shape), dtype)


def _mx(v):
    return v.astype(MXU_DTYPE)


def _dot(a, b):
    return jnp.dot(_mx(a), _mx(b), preferred_element_type=F32)


def _dot_nt(a, b):
    return lax.dot_general(_mx(a), _mx(b), (((1,), (1,)), ((), ())), preferred_element_type=F32)


def _dot_tn(a, b):
    return lax.dot_general(_mx(a), _mx(b), (((0,), (0,)), ((), ())), preferred_element_type=F32)


def _dot_hi(a, b):
    return jnp.dot(a, b, preferred_element_type=F32, precision=HI)


def _sigmoid(v):
    return 1.0 / (1.0 + jnp.exp(-v))


def _pick(n, prefs):
    for p in prefs:
        if n % p == 0:
            return p
    return n


def matmul(a, b, *, res=None, out_dtype=F32, name, tm=None, tn=None, tk=None):
    M, K = a.shape
    K2, N = b.shape
    assert K == K2
    tm = tm or _pick(M, (1024, 512, 256, 128))
    tn = tn or _pick(N, (512, 256, 128))
    tk = tk or _pick(K, (1024, 512, 256, 128))
    nk = K // tk

    def body(*refs):
        if res is None:
            a_ref, b_ref, o_ref, acc = refs
            r_ref = None
        else:
            a_ref, b_ref, r_ref, o_ref, acc = refs
        k = pl.program_id(2)
        part = _dot(a_ref[...], b_ref[...])

        @pl.when(k == 0)
        def _():
            acc[...] = part

        @pl.when(k > 0)
        def _():
            acc[...] += part

        @pl.when(k == nk - 1)
        def _():
            out = acc[...]
            if r_ref is not None:
                out = out + r_ref[...]
            o_ref[...] = out.astype(out_dtype)

    in_specs = [pl.BlockSpec((tm, tk), lambda i, j, k: (i, k)), pl.BlockSpec((tk, tn), lambda i, j, k: (k, j))]
    args = [a, b]
    if res is not None:
        in_specs.append(pl.BlockSpec((tm, tn), lambda i, j, k: (i, j)))
        args.append(res)
    return pl.pallas_call(
        body, name=name, grid=(M // tm, N // tn, nk), in_specs=in_specs,
        out_specs=pl.BlockSpec((tm, tn), lambda i, j, k: (i, j)), out_shape=_sds((M, N), out_dtype),
        scratch_shapes=[pltpu.VMEM((tm, tn), F32)],
        compiler_params=_cparams(("parallel", "parallel", "arbitrary")),
    )(*args)


def matmul_tn(a, g, *, name, tm=None, tn=None, tk=None):
    T, M = a.shape
    T2, N = g.shape
    assert T == T2
    tm = tm or _pick(M, (512, 384, 256, 128))
    tn = tn or _pick(N, (512, 256, 128))
    tk = tk or _pick(T, (1024, 512, 256, 128))
    nk = T // tk

    def body(a_ref, g_ref, o_ref):
        k = pl.program_id(2)
        part = _dot_tn(a_ref[...], g_ref[...])

        @pl.when(k == 0)
        def _():
            o_ref[...] = part

        @pl.when(k > 0)
        def _():
            o_ref[...] += part

    return pl.pallas_call(
        body, name=name, grid=(M // tm, N // tn, nk),
        in_specs=[pl.BlockSpec((tk, tm), lambda i, j, k: (k, i)), pl.BlockSpec((tk, tn), lambda i, j, k: (k, j))],
        out_specs=pl.BlockSpec((tm, tn), lambda i, j, k: (i, j)), out_shape=_sds((M, N), F32),
        compiler_params=_cparams(("parallel", "parallel", "arbitrary")),
    )(a, g)


def rmsnorm_fwd(x, gamma, *, name, tm=512):
    T, D = x.shape
    tm = _pick(T, (tm, 256, 128))

    def body(x_ref, g_ref, o_ref):
        xv = x_ref[...]
        r = lax.rsqrt(jnp.mean(xv * xv, axis=-1, keepdims=True) + EPS)
        o_ref[...] = ((xv * r) * g_ref[...]).astype(MXU_DTYPE)

    return pl.pallas_call(
        body, name=name, grid=(T // tm,),
        in_specs=[pl.BlockSpec((tm, D), lambda i: (i, 0)), pl.BlockSpec((1, D), lambda i: (0, 0))],
        out_specs=pl.BlockSpec((tm, D), lambda i: (i, 0)), out_shape=_sds((T, D), MXU_DTYPE),
        compiler_params=_cparams(("parallel",)),
    )(x, gamma.reshape(1, D))


def _rms_bwd_tile(xv, gamma, dh):
    r = lax.rsqrt(jnp.mean(xv * xv, axis=-1, keepdims=True) + EPS)
    xh = xv * r
    dg = jnp.sum(dh * xh, axis=0, keepdims=True)
    dn = dh * gamma
    dx = r * (dn - xh * jnp.mean(dn * xh, axis=-1, keepdims=True))
    return dx, dg


def rmsnorm_bwd(x, gamma, dh, dres, *, name, tm=256):
    T, D = x.shape
    tm = _pick(T, (tm, 128))

    def body(x_ref, g_ref, dh_ref, dr_ref, dx_ref, dg_ref):
        dx, dg = _rms_bwd_tile(x_ref[...], g_ref[...], dh_ref[...].astype(F32))
        dx_ref[...] = dx + dr_ref[...]

        @pl.when(pl.program_id(0) == 0)
        def _():
            dg_ref[...] = dg

        @pl.when(pl.program_id(0) > 0)
        def _():
            dg_ref[...] += dg

    row = pl.BlockSpec((tm, D), lambda i: (i, 0))
    vec = pl.BlockSpec((1, D), lambda i: (0, 0))
    return pl.pallas_call(
        body, name=name, grid=(T // tm,), in_specs=[row, vec, row, row], out_specs=[row, vec],
        out_shape=[_sds((T, D), F32), _sds((1, D), F32)], compiler_params=_cparams(("arbitrary",)),
    )(x, gamma.reshape(1, D), dh, dres)


def final_loss(x, gamma, target, *, name="final_loss", tm=256):
    T, D = x.shape
    tm = _pick(T, (tm, 128))

    def body(x_ref, g_ref, t_ref, l_ref, dx_ref, dg_ref):
        xv = x_ref[...]
        gam = g_ref[...]
        r = lax.rsqrt(jnp.mean(xv * xv, axis=-1, keepdims=True) + EPS)
        y = (xv * r) * gam
        err = y - t_ref[...]
        part = 0.5 * jnp.sum(jnp.sum(err * err, axis=-1, keepdims=True) / D, axis=0, keepdims=True)
        dx, dg = _rms_bwd_tile(xv, gam, err / D)
        dx_ref[...] = dx

        @pl.when(pl.program_id(0) == 0)
        def _():
            dg_ref[...] = dg
            l_ref[...] = jnp.broadcast_to(part, l_ref.shape)

        @pl.when(pl.program_id(0) > 0)
        def _():
            dg_ref[...] += dg
            l_ref[...] += jnp.broadcast_to(part, l_ref.shape)

    row = pl.BlockSpec((tm, D), lambda i: (i, 0))
    vec = pl.BlockSpec((1, D), lambda i: (0, 0))
    return pl.pallas_call(
        body, name=name, grid=(T // tm,), in_specs=[row, vec, row],
        out_specs=[pl.BlockSpec((1, LANE), lambda i: (0, 0)), row, vec],
        out_shape=[_sds((1, LANE), F32), _sds((T, D), F32), _sds((1, D), F32)],
        compiler_params=_cparams(("arbitrary",)),
    )(x, gamma.reshape(1, D), target)


def _halo_prev(ts):
    return lambda i, j, off=0: (jnp.maximum(i * (ts // HALO) - 1, 0), j + off)


def _cat_prev(cur, halo, first):
    return jnp.concatenate([jnp.where(first, 0.0, halo), cur], axis=0)


def _cat_next(cur, halo, last):
    return jnp.concatenate([cur, jnp.where(last, 0.0, halo)], axis=0)


def _delayed(cat, r):
    if r == 0:
        return cat[HALO:]
    return pltpu.roll(cat, r, axis=0)[HALO:]


def _advanced(cat, r):
    n = cat.shape[0]
    if r == 0:
        return cat[:n - HALO]
    return pltpu.roll(cat, n - r, axis=0)[:n - HALO]


def _conv_pre(cat, w, b, K):
    acc = _delayed(cat, K - 1) * w[0:1, :] + b
    for k in range(1, K):
        acc = acc + _delayed(cat, K - 1 - k) * w[k:k + 1, :]
    return acc


def _pad_rows8(w):
    return jnp.pad(w, ((0, 8 - w.shape[0]), (0, 0)))


def ssd_conv_fwd(proj, w, b, S, *, name, ts=512, tc=512):
    T = proj.shape[0]
    C, K = SSD_CONV_CH, SSD_CONV
    ts = _pick(S, (ts, 256, 128))
    off = PXBC0 // tc
    ns = S // ts

    def body(x_ref, h_ref, w_ref, b_ref, o_ref):
        first = (pl.program_id(0) % ns) == 0
        pre = _conv_pre(_cat_prev(x_ref[...], h_ref[...], first), w_ref[...], b_ref[...], K)
        o_ref[...] = pre * _sigmoid(pre)

    return pl.pallas_call(
        body, name=name, grid=(T // ts, C // tc),
        in_specs=[pl.BlockSpec((ts, tc), lambda i, j: (i, j + off)),
                  pl.BlockSpec((HALO, tc), functools.partial(_halo_prev(ts), off=off)),
                  pl.BlockSpec((8, tc), lambda i, j: (0, j)), pl.BlockSpec((1, tc), lambda i, j: (0, j))],
        out_specs=pl.BlockSpec((ts, tc), lambda i, j: (i, j)), out_shape=_sds((T, C), F32),
        compiler_params=_cparams(("parallel", "parallel")),
    )(proj, proj, _pad_rows8(w), b.reshape(1, C))


def ssd_conv_bwd_pre(proj, w, b, dxc, S, *, name, ts=512, tc=512):
    T = proj.shape[0]
    C, K = SSD_CONV_CH, SSD_CONV
    ts = _pick(S, (ts, 256, 128))
    off = PXBC0 // tc
    ns = S // ts

    def body(x_ref, h_ref, w_ref, b_ref, d_ref, o_ref, acc_ref):
        i = pl.program_id(1)
        first = (i % ns) == 0
        cat = _cat_prev(x_ref[...], h_ref[...], first)
        pre = _conv_pre(cat, w_ref[...], b_ref[...], K)
        sg = _sigmoid(pre)
        dpre = d_ref[...] * (sg * (1.0 + pre * (1.0 - sg)))
        o_ref[...] = dpre
        rows = [jnp.sum(dpre * _delayed(cat, K - 1 - k), axis=0, keepdims=True) for k in range(K)]
        rows.append(jnp.sum(dpre, axis=0, keepdims=True))
        rows.append(jnp.zeros((8 - len(rows), dpre.shape[1]), F32))
        part = jnp.concatenate(rows, axis=0)

        @pl.when(i == 0)
        def _():
            acc_ref[...] = part

        @pl.when(i > 0)
        def _():
            acc_ref[...] += part

    hp = _halo_prev(ts)
    return pl.pallas_call(
        body, name=name, grid=(C // tc, T // ts),
        in_specs=[pl.BlockSpec((ts, tc), lambda j, i: (i, j + off)),
                  pl.BlockSpec((HALO, tc), lambda j, i: hp(i, j, off)),
                  pl.BlockSpec((8, tc), lambda j, i: (0, j)), pl.BlockSpec((1, tc), lambda j, i: (0, j)),
                  pl.BlockSpec((ts, tc), lambda j, i: (i, j))],
        out_specs=[pl.BlockSpec((ts, tc), lambda j, i: (i, j)), pl.BlockSpec((8, tc), lambda j, i: (0, j))],
        out_shape=[_sds((T, C), F32), _sds((8, C), F32)],
        compiler_params=_cparams(("parallel", "arbitrary")),
    )(proj, proj, _pad_rows8(w), b.reshape(1, C), dxc)


def conv_bwd_x(dpre, w, S, K, *, name, ts=512, tc=None):
    T, C = dpre.shape
    out_dtype = MXU_DTYPE
    ts = _pick(S, (ts, 256, 128))
    tc = tc or _pick(C, (512, 256, 128))
    ns = S // ts
    nblk = T // HALO

    def body(d_ref, h_ref, w_ref, o_ref):
        last = (pl.program_id(0) % ns) == ns - 1
        cat = _cat_next(d_ref[...], h_ref[...], last)
        wv = w_ref[...]
        acc = _advanced(cat, K - 1) * wv[0:1, :]
        for k in range(1, K):
            acc = acc + _advanced(cat, K - 1 - k) * wv[k:k + 1, :]
        o_ref[...] = acc.astype(out_dtype)

    return pl.pallas_call(
        body, name=name, grid=(T // ts, C // tc),
        in_specs=[pl.BlockSpec((ts, tc), lambda i, j: (i, j)),
                  pl.BlockSpec((HALO, tc), lambda i, j: (jnp.minimum((i + 1) * (ts // HALO), nblk - 1), j)),
                  pl.BlockSpec((8, tc), lambda i, j: (0, j))],
        out_specs=pl.BlockSpec((ts, tc), lambda i, j: (i, j)), out_shape=_sds((T, C), out_dtype),
        compiler_params=_cparams(("parallel", "parallel")),
    )(dpre, dpre, _pad_rows8(w))


def ffn_conv_gate_fwd(up, w, b, S, *, name, ts=512, tc=256):
    T, C2 = up.shape
    C, K = C2 // 2, FFN_CONV
    ts = _pick(S, (ts, 256, 128))
    nj = C // tc
    ns = S // ts
    w8 = _pad_rows8(w)
    b2 = b.reshape(1, C2)

    def body(g_ref, gh_ref, v_ref, vh_ref, wg_ref, wv_ref, bg_ref, bv_ref, o_ref):
        first = (pl.program_id(0) % ns) == 0
        g = _conv_pre(_cat_prev(g_ref[...], gh_ref[...], first), wg_ref[...], bg_ref[...], K)
        v = _conv_pre(_cat_prev(v_ref[...], vh_ref[...], first), wv_ref[...], bv_ref[...], K)
        o_ref[...] = (g * _sigmoid(g) * v).astype(o_ref.dtype)

    hp = _halo_prev(ts)
    return pl.pallas_call(
        body, name=name, grid=(T // ts, nj),
        in_specs=[pl.BlockSpec((ts, tc), lambda i, j: (i, j)), pl.BlockSpec((HALO, tc), lambda i, j: hp(i, j)),
                  pl.BlockSpec((ts, tc), lambda i, j: (i, j + nj)), pl.BlockSpec((HALO, tc), lambda i, j: hp(i, j, nj)),
                  pl.BlockSpec((8, tc), lambda i, j: (0, j)), pl.BlockSpec((8, tc), lambda i, j: (0, j + nj)),
                  pl.BlockSpec((1, tc), lambda i, j: (0, j)), pl.BlockSpec((1, tc), lambda i, j: (0, j + nj))],
        out_specs=pl.BlockSpec((ts, tc), lambda i, j: (i, j)), out_shape=_sds((T, C), MXU_DTYPE),
        compiler_params=_cparams(("parallel", "parallel")),
    )(up, up, up, up, w8, w8, b2, b2)


def ffn_conv_gate_bwd(up, w, b, dact, S, *, name, ts=512, tc=256):
    T, C2 = up.shape
    C, K = C2 // 2, FFN_CONV
    ts = _pick(S, (ts, 256, 128))
    nj = C // tc
    ns = S // ts
    w8 = _pad_rows8(w)
    b2 = b.reshape(1, C2)

    def stats(dpre, cat):
        rows = [jnp.sum(dpre * _delayed(cat, K - 1 - k), axis=0, keepdims=True) for k in range(K)]
        rows.append(jnp.sum(dpre, axis=0, keepdims=True))
        rows.append(jnp.zeros((8 - len(rows), dpre.shape[1]), F32))
        return jnp.concatenate(rows, axis=0)

    def body(g_ref, gh_ref, v_ref, vh_ref, wg_ref, wv_ref, bg_ref, bv_ref, d_ref, dg_ref, dv_ref, ag_ref, av_ref):
        i = pl.program_id(1)
        first = (i % ns) == 0
        gcat = _cat_prev(g_ref[...], gh_ref[...], first)
        vcat = _cat_prev(v_ref[...], vh_ref[...], first)
        g = _conv_pre(gcat, wg_ref[...], bg_ref[...], K)
        v = _conv_pre(vcat, wv_ref[...], bv_ref[...], K)
        d = d_ref[...]
        sg = _sigmoid(g)
        dg = d * v * (sg * (1.0 + g * (1.0 - sg)))
        dv = d * (g * sg)
        dg_ref[...] = dg
        dv_ref[...] = dv
        sgp, svp = stats(dg, gcat), stats(dv, vcat)

        @pl.when(i == 0)
        def _():
            ag_ref[...] = sgp
            av_ref[...] = svp

        @pl.when(i > 0)
        def _():
            ag_ref[...] += sgp
            av_ref[...] += svp

    hp = _halo_prev(ts)
    dg, dv, ag, av = pl.pallas_call(
        body, name=name, grid=(nj, T // ts),
        in_specs=[pl.BlockSpec((ts, tc), lambda j, i: (i, j)), pl.BlockSpec((HALO, tc), lambda j, i: hp(i, j)),
                  pl.BlockSpec((ts, tc), lambda j, i: (i, j + nj)), pl.BlockSpec((HALO, tc), lambda j, i: hp(i, j, nj)),
                  pl.BlockSpec((8, tc), lambda j, i: (0, j)), pl.BlockSpec((8, tc), lambda j, i: (0, j + nj)),
                  pl.BlockSpec((1, tc), lambda j, i: (0, j)), pl.BlockSpec((1, tc), lambda j, i: (0, j + nj)),
                  pl.BlockSpec((ts, tc), lambda j, i: (i, j))],
        out_specs=[pl.BlockSpec((ts, tc), lambda j, i: (i, j)), pl.BlockSpec((ts, tc), lambda j, i: (i, j)),
                   pl.BlockSpec((8, tc), lambda j, i: (0, j)), pl.BlockSpec((8, tc), lambda j, i: (0, j))],
        out_shape=[_sds((T, C), F32), _sds((T, C), F32), _sds((8, C), F32), _sds((8, C), F32)],
        compiler_params=_cparams(("parallel", "arbitrary")),
    )(up, up, up, up, w8, w8, b2, b2, dact)
    return dg, dv, jnp.concatenate([ag, av], axis=1)


def _pool_counts(pos, w):
    return jnp.minimum(pos + 1.0, float(w))


def pool_fwd(proj, pool_w, pool_scale, S, *, name, ts=512):
    T = proj.shape[0]
    C, G, GD, H = POOL_WIDTH, POOL_GROUPS, POOL_GROUP_DIM, POOL_HALO
    ts = _pick(S, (ts, 256, 128))
    ns = S // ts
    off = PU0 // C

    def body(u_ref, h_ref, w_ref, s_ref, y_ref, p_ref):
        i = pl.program_id(0)
        first = (i % ns) == 0
        cat = jnp.concatenate([jnp.where(first, 0.0, h_ref[...]), u_ref[...]], axis=0)
        pos = ((i % ns) * ts + lax.broadcasted_iota(jnp.int32, (ts, 1), 0)).astype(F32)
        sums = cat
        win = 1
        for g, wlen in enumerate(POOL_WINDOWS):
            while win < wlen:
                sums = sums + pltpu.roll(sums, win, axis=0)
                win *= 2
            sl = slice(g * GD, (g + 1) * GD)
            pooled = sums[H:, sl] / _pool_counts(pos, wlen) - cat[H:, sl]
            p_ref[:, sl] = pooled.astype(p_ref.dtype)
            y_ref[:, sl] = (_dot(pooled, w_ref[g]) * s_ref[:, sl]).astype(y_ref.dtype)

    return pl.pallas_call(
        body, name=name, grid=(T // ts,),
        in_specs=[pl.BlockSpec((ts, C), lambda i: (i, off)),
                  pl.BlockSpec((H, C), lambda i: (jnp.maximum(i * (ts // H) - 1, 0), off)),
                  pl.BlockSpec((G, GD, GD), lambda i: (0, 0, 0)), pl.BlockSpec((1, C), lambda i: (0, 0))],
        out_specs=[pl.BlockSpec((ts, C), lambda i: (i, 0)), pl.BlockSpec((ts, C), lambda i: (i, 0))],
        out_shape=[_sds((T, C), MXU_DTYPE), _sds((T, C), MXU_DTYPE)],
        compiler_params=_cparams(("parallel",)),
    )(proj, proj, _mx(pool_w), pool_scale.reshape(1, C))


def pool_bwd(dmix, pooled, pool_w, pool_scale, S, *, name, ts=512):
    T = dmix.shape[0]
    C, G, GD, H = POOL_WIDTH, POOL_GROUPS, POOL_GROUP_DIM, POOL_HALO
    ts = _pick(S, (ts, 256, 128))
    ns = S // ts
    off = SSD_WIDTH // C
    nblk = T // H

    def body(d_ref, dh_ref, p_ref, w_ref, s_ref, du_ref, dw_ref, ds_ref):
        i = pl.program_id(0)
        last = (i % ns) == ns - 1
        dcat = jnp.concatenate([d_ref[...], jnp.where(last, 0.0, dh_ref[...])], axis=0)
        n = ts + H
        pos = ((i % ns) * ts + lax.broadcasted_iota(jnp.int32, (n, 1), 0)).astype(F32)
        dws, dss = [], []
        for g, wlen in enumerate(POOL_WINDOWS):
            sl = slice(g * GD, (g + 1) * GD)
            wg = w_ref[g]
            pg = p_ref[:, sl]
            dys = dcat[:, sl] * s_ref[:, sl]
            dss.append(jnp.sum(dcat[:ts, sl] * _dot(pg, wg), axis=0, keepdims=True))
            dws.append(_dot_tn(pg, dys[:ts]))
            dp = _dot_nt(dys, wg)
            q = dp / _pool_counts(pos, wlen)
            win = 1
            while win < wlen:
                q = q + pltpu.roll(q, n - win, axis=0)
                win *= 2
            du_ref[:, sl] = (q[:ts] - dp[:ts]).astype(du_ref.dtype)
        dsp = jnp.concatenate(dss, axis=1)

        @pl.when(i == 0)
        def _():
            for g in range(G):
                dw_ref[g] = dws[g]
            ds_ref[...] = dsp

        @pl.when(i > 0)
        def _():
            for g in range(G):
                dw_ref[g] += dws[g]
            ds_ref[...] += dsp

    return pl.pallas_call(
        body, name=name, grid=(T // ts,),
        in_specs=[pl.BlockSpec((ts, C), lambda i: (i, off)),
                  pl.BlockSpec((H, C), lambda i: (jnp.minimum((i + 1) * (ts // H), nblk - 1), off)),
                  pl.BlockSpec((ts, C), lambda i: (i, 0)),
                  pl.BlockSpec((G, GD, GD), lambda i: (0, 0, 0)), pl.BlockSpec((1, C), lambda i: (0, 0))],
        out_specs=[pl.BlockSpec((ts, C), lambda i: (i, 0)), pl.BlockSpec((G, GD, GD), lambda i: (0, 0, 0)),
                   pl.BlockSpec((1, C), lambda i: (0, 0))],
        out_shape=[_sds((T, C), MXU_DTYPE), _sds((G, GD, GD), F32), _sds((1, C), F32)],
        compiler_params=_cparams(("arbitrary",)),
    )(dmix, dmix, pooled, _mx(pool_w), pool_scale.reshape(1, C))


ROPE0 = MLA_NOPE
ROPE_HALF = MLA_ROPE // 2


def _rope_tables(pos, invf):
    lane = lax.broadcasted_iota(jnp.int32, (1, HEAD_W), 1)
    ang = pos * invf
    cs, sn = jnp.cos(ang), jnp.sin(ang)
    in_a = (lane >= ROPE0) & (lane < ROPE0 + ROPE_HALF)
    in_b = (lane >= ROPE0 + ROPE_HALF) & (lane < ROPE0 + MLA_ROPE)
    return jnp.where(in_a | in_b, cs, 1.0), jnp.where(in_a, -sn, 0.0), jnp.where(in_b, sn, 0.0), in_a | in_b


def _rope(v, cosf, sin_a, sin_b):
    return (v * cosf + pltpu.roll(v, HEAD_W - ROPE_HALF, axis=1) * sin_a + pltpu.roll(v, ROPE_HALF, axis=1) * sin_b)


def _unrope(d, cosf, sin_a, sin_b):
    return (d * cosf + pltpu.roll(d * sin_a, ROPE_HALF, axis=1) + pltpu.roll(d * sin_b, HEAD_W - ROPE_HALF, axis=1))


def _rms_tile(xv, gamma):
    return (xv * lax.rsqrt(jnp.mean(xv * xv, axis=-1, keepdims=True) + EPS)) * gamma


def mla_prep_fwd(proj, pos, invf, q_norm, w_uq_p, kv_norm, w_ukv_p, *, name, tm=256):
    T = proj.shape[0]
    tm = _pick(T, (tm, 128))
    QR, KR, P = MLA_Q_RANK, MLA_KV_RANK, MLA_PAD

    def body(cq_ref, ckv_ref, kpe_ref, pos_ref, invf_ref, qn_ref, wq_ref, kn_ref, wkv_ref,
             q_ref, k_ref, v_ref, cqn_ref, ckvn_ref):
        cosf, sin_a, sin_b, _ = _rope_tables(pos_ref[...], invf_ref[...])
        cqn = _rms_tile(cq_ref[...], qn_ref[...]).astype(MXU_DTYPE)
        ckvn = _rms_tile(ckv_ref[...], kn_ref[...]).astype(MXU_DTYPE)
        cqn_ref[...] = cqn
        ckvn_ref[...] = ckvn
        qp = _dot(cqn, wq_ref[...])
        kvp = _dot(ckvn, wkv_ref[...])
        kpe = _rope(kpe_ref[...], cosf, sin_a, sin_b)
        for h in range(MLA_HEADS):
            sl = slice(h * HEAD_W, (h + 1) * HEAD_W)
            q_ref[:, sl] = _rope(qp[:, sl], cosf, sin_a, sin_b).astype(q_ref.dtype)
            k_ref[:, sl] = (kvp[:, sl] + kpe).astype(k_ref.dtype)
            v_ref[:, sl] = kvp[:, P + h * HEAD_W:P + (h + 1) * HEAD_W].astype(v_ref.dtype)

    row = lambda w: pl.BlockSpec((tm, w), lambda i: (i, 0))
    full = lambda a, b: pl.BlockSpec((a, b), lambda i: (0, 0))
    return pl.pallas_call(
        body, name=name, grid=(T // tm,),
        in_specs=[pl.BlockSpec((tm, QR), lambda i: (i, PCQ0 // QR)), pl.BlockSpec((tm, KR), lambda i: (i, PCKV0 // KR)),
                  pl.BlockSpec((tm, LANE), lambda i: (i, PKPE0 // LANE)), row(1), full(1, LANE),
                  full(1, QR), full(QR, P), full(1, KR), full(KR, 2 * P)],
        out_specs=[row(P), row(P), row(P), row(QR), row(KR)],
        out_shape=[_sds((T, P), MXU_DTYPE)] * 3 + [_sds((T, QR), MXU_DTYPE), _sds((T, KR), MXU_DTYPE)],
        compiler_params=_cparams(("parallel",)),
    )(proj, proj, proj, pos, invf, q_norm.reshape(1, QR), w_uq_p, kv_norm.reshape(1, KR), w_ukv_p)


def mla_prep_bwd(proj, pos, invf, q_norm, w_uq_pt, kv_norm, w_ukv_pt, dq, dk, dv, *, name, tm=256):
    T = proj.shape[0]
    tm = _pick(T, (tm, 128))
    QR, KR, P = MLA_Q_RANK, MLA_KV_RANK, MLA_PAD

    def body(cq_ref, ckv_ref, pos_ref, invf_ref, qn_ref, wqt_ref, kn_ref, wkvt_ref, dq_ref, dk_ref, dv_ref,
             dqp_ref, dkvp_ref, dcq_ref, dckv_ref, dkpe_ref, dqn_ref, dkn_ref):
        cosf, sin_a, sin_b, rot = _rope_tables(pos_ref[...], invf_ref[...])
        dkpe = jnp.zeros((tm, HEAD_W), F32)
        for h in range(MLA_HEADS):
            sl = slice(h * HEAD_W, (h + 1) * HEAD_W)
            dqp_ref[:, sl] = _unrope(dq_ref[:, sl], cosf, sin_a, sin_b).astype(dqp_ref.dtype)
            dkh = dk_ref[:, sl]
            dkpe = dkpe + dkh
            dkvp_ref[:, sl] = dkh.astype(dkvp_ref.dtype)
            dkvp_ref[:, P + h * HEAD_W:P + (h + 1) * HEAD_W] = dv_ref[:, sl].astype(dkvp_ref.dtype)
        dkpe_ref[...] = jnp.where(rot, _unrope(dkpe, cosf, sin_a, sin_b), 0.0).astype(dkpe_ref.dtype)
        dcq, dqn = _rms_bwd_tile(cq_ref[...], qn_ref[...], _dot(dqp_ref[...], wqt_ref[...]))
        dckv, dkn = _rms_bwd_tile(ckv_ref[...], kn_ref[...], _dot(dkvp_ref[...], wkvt_ref[...]))
        dcq_ref[...] = dcq.astype(dcq_ref.dtype)
        dckv_ref[...] = dckv.astype(dckv_ref.dtype)

        @pl.when(pl.program_id(0) == 0)
        def _():
            dqn_ref[...] = dqn
            dkn_ref[...] = dkn

        @pl.when(pl.program_id(0) > 0)
        def _():
            dqn_ref[...] += dqn
            dkn_ref[...] += dkn

    row = lambda w: pl.BlockSpec((tm, w), lambda i: (i, 0))
    full = lambda a, b: pl.BlockSpec((a, b), lambda i: (0, 0))
    return pl.pallas_call(
        body, name=name, grid=(T // tm,),
        in_specs=[pl.BlockSpec((tm, QR), lambda i: (i, PCQ0 // QR)), pl.BlockSpec((tm, KR), lambda i: (i, PCKV0 // KR)),
                  row(1), full(1, LANE), full(1, QR), full(P, QR), full(1, KR), full(2 * P, KR), row(P), row(P), row(P)],
        out_specs=[row(P), row(2 * P), row(QR), row(KR), row(LANE), full(1, QR), full(1, KR)],
        out_shape=[_sds((T, P), MXU_DTYPE), _sds((T, 2 * P), MXU_DTYPE), _sds((T, QR), MXU_DTYPE),
                   _sds((T, KR), MXU_DTYPE), _sds((T, LANE), MXU_DTYPE), _sds((1, QR), F32), _sds((1, KR), F32)],
        compiler_params=_cparams(("arbitrary",)),
    )(proj, proj, pos, invf, q_norm.reshape(1, QR), w_uq_pt, kv_norm.reshape(1, KR), w_ukv_pt, dq, dk, dv)


ATTN_SCALE = 1.0 / math.sqrt(MLA_QK)


def _causal_mask(i, j, blk):
    row = lax.broadcasted_iota(jnp.int32, (blk, blk), 0)
    col = lax.broadcasted_iota(jnp.int32, (blk, blk), 1)
    return col <= row + (i - j) * blk


def flash_fwd(q, k, v, S, *, name, blk=512):
    T, P = q.shape
    blk = _pick(S, (blk, 256, 128))
    B, nq, H, W = T // S, S // blk, MLA_HEADS, HEAD_W

    def body(q_ref, k_ref, v_ref, o_ref, lse_ref, m_s, l_s, acc_s):
        i, j = pl.program_id(2), pl.program_id(3)

        @pl.when(j == 0)
        def _():
            m_s[...] = jnp.full(m_s.shape, -jnp.inf, F32)
            l_s[...] = jnp.zeros(l_s.shape, F32)
            acc_s[...] = jnp.zeros(acc_s.shape, F32)

        @pl.when(j <= i)
        def _():
            s = _dot_nt(q_ref[...], k_ref[...]) * ATTN_SCALE
            s = jnp.where(_causal_mask(i, j, blk), s, -jnp.inf)
            m_prev = m_s[...]
            m_new = jnp.maximum(m_prev, jnp.max(s, axis=1, keepdims=True))
            p = jnp.exp(s - m_new[:, 0:1])
            alpha = jnp.exp(m_prev - m_new)
            l_s[...] = alpha * l_s[...] + jnp.sum(p, axis=1, keepdims=True)
            acc_s[...] = alpha * acc_s[...] + _dot(p, v_ref[...])
            m_s[...] = m_new

        @pl.when(j == i)
        def _():
            o_ref[...] = acc_s[...] / l_s[...]
            lse_ref[...] = m_s[...] + jnp.log(l_s[...])

    qmap = lambda b, h, i, j: (b * nq + i, h)
    kmap = lambda b, h, i, j: (b * nq + jnp.minimum(j, i), h)
    return pl.pallas_call(
        body, name=name, grid=(B, H, nq, nq),
        in_specs=[pl.BlockSpec((blk, W), qmap), pl.BlockSpec((blk, W), kmap), pl.BlockSpec((blk, W), kmap)],
        out_specs=[pl.BlockSpec((blk, W), qmap), pl.BlockSpec((blk, W), qmap)],
        out_shape=[_sds((T, P), F32), _sds((T, P), F32)],
        scratch_shapes=[pltpu.VMEM((blk, W), F32)] * 3,
        compiler_params=_cparams(("parallel", "parallel", "parallel", "arbitrary")),
    )(q, k, v)


def flash_bwd(q, k, v, o, lse, dmix, S, *, name, blk=512):
    T, P = q.shape
    blk = _pick(S, (blk, 256, 128))
    B, nq, H, W = T // S, S // blk, MLA_HEADS, HEAD_W
    off = (SSD_WIDTH + POOL_WIDTH) // W

    def body(q_ref, k_ref, v_ref, o_ref, lse_ref, do_ref, dq_ref, dk_ref, dv_ref):
        j, i = pl.program_id(2), pl.program_id(3)

        @pl.when(i >= j)
        def _():
            qv, kv, do = q_ref[...], k_ref[...], do_ref[...]
            s = _dot_nt(qv, kv) * ATTN_SCALE
            p = jnp.where(_causal_mask(i, j, blk), jnp.exp(s - lse_ref[:, 0:1]), 0.0)
            delta = jnp.sum(do * o_ref[...], axis=1, keepdims=True)
            dv_part = _dot_tn(p, do)
            ds = p * (_dot_nt(do, v_ref[...]) - delta) * ATTN_SCALE
            dk_part = _dot_tn(ds, qv)
            dq_part = _dot(ds, kv)
            rows = pl.ds(pl.multiple_of(i * blk, blk), blk)

            @pl.when(i == j)
            def _():
                dk_ref[...] = dk_part
                dv_ref[...] = dv_part

            @pl.when(i > j)
            def _():
                dk_ref[...] += dk_part
                dv_ref[...] += dv_part

            @pl.when(j == 0)
            def _():
                dq_ref[rows, :] = dq_part

            @pl.when(j > 0)
            def _():
                dq_ref[rows, :] += dq_part

    qmap = lambda b, h, j, i: (b * nq + jnp.maximum(i, j), h)
    kmap = lambda b, h, j, i: (b * nq + j, h)
    return pl.pallas_call(
        body, name=name, grid=(B, H, nq, nq),
        in_specs=[pl.BlockSpec((blk, W), qmap), pl.BlockSpec((blk, W), kmap), pl.BlockSpec((blk, W), kmap),
                  pl.BlockSpec((blk, W), qmap), pl.BlockSpec((blk, W), qmap),
                  pl.BlockSpec((blk, W), lambda b, h, j, i: (b * nq + jnp.maximum(i, j), off + h))],
        out_specs=[pl.BlockSpec((S, W), lambda b, h, j, i: (b, h)), pl.BlockSpec((blk, W), kmap), pl.BlockSpec((blk, W), kmap)],
        out_shape=[_sds((T, P), F32)] * 3,
        compiler_params=_cparams(("parallel", "parallel", "arbitrary", "arbitrary")),
    )(q, k, v, o, lse, dmix)


SSD_PAIRS = SSD_HEADS // 2
PAIRS_PER_GROUP = SSD_PAIRS // SSD_GROUPS
GN = SSD_GROUPS * SSD_STATE


def _log1p_small(e):
    return jnp.where(e < 1e-3, e * (1.0 - e * (0.5 - e / 3.0)), jnp.log(1.0 + e))


def _softplus(v):
    return jnp.maximum(v, 0.0) + _log1p_small(jnp.exp(-jnp.abs(v)))


def _ssd_decay(dt_raw, dtb, alog):
    L = dt_raw.shape[0]
    pre = dt_raw + dtb
    dt = _softplus(pre)
    a = -jnp.exp(alog)
    row = lax.broadcasted_iota(jnp.int32, (L, L), 0)
    col = lax.broadcasted_iota(jnp.int32, (L, L), 1)
    tri = row >= col
    cum = _dot_hi(tri.astype(F32), dt * a)
    return pre, dt, a, tri, cum, cum.T


def _col(m, h):
    return m[:, h:h + 1]


def _pair_sel(m, k, lo):
    return jnp.where(lo, _col(m, 2 * k), _col(m, 2 * k + 1))


def _ssd_specs(S):
    L = SSD_CHUNK
    nc = S // L
    return L, nc


def ssd_fwd(proj, xc, dtb, alog, dchan, normw, S, *, name):
    T = proj.shape[0]
    L, nc = _ssd_specs(S)
    B, W, N = T // S, SSD_WIDTH, SSD_STATE

    def body(xs_ref, bs_ref, cs_ref, dt_ref, z_ref, dtb_ref, alog_ref, dch_ref, nw_ref, y_ref, ys_ref, hin_ref, st):
        @pl.when(pl.program_id(1) == 0)
        def _():
            st[...] = jnp.zeros(st.shape, F32)

        hin_ref[...] = st[...]
        _, dt, a, tri, cum, cum_t = _ssd_decay(dt_ref[...], dtb_ref[...], alog_ref[...])
        e_cum = jnp.exp(cum)
        last = cum[L - 1:L, :]
        w_end = jnp.exp(last - cum)
        e_last = jnp.exp(last)
        lo = lax.broadcasted_iota(jnp.int32, (1, LANE), 1) < SSD_HEAD_DIM
        slo = lax.broadcasted_iota(jnp.int32, (LANE, 1), 0) < SSD_HEAD_DIM
        for g in range(SSD_GROUPS):
            bm = bs_ref[:, g * N:(g + 1) * N]
            cm = cs_ref[:, g * N:(g + 1) * N]
            gmat = _dot_nt(cm, bm)
            for kk in range(PAIRS_PER_GROUP):
                k = g * PAIRS_PER_GROUP + kk
                sl = slice(k * LANE, (k + 1) * LANE)
                xv = xs_ref[:, sl]
                xdt = xv * _pair_sel(dt, k, lo)
                yd = []
                for h in (2 * k, 2 * k + 1):
                    gam = jnp.exp(jnp.where(tri, _col(cum, h) - cum_t[h:h + 1, :], -jnp.inf))
                    yd.append(_dot(gmat * gam, xdt))
                hp = st[sl, :]
                y_off = _dot_nt(cm, hp) * _pair_sel(e_cum, k, lo)
                y_ref[:, sl] = jnp.where(lo, yd[0], yd[1]) + y_off + xv * dch_ref[:, sl]
                zmat = xdt * _pair_sel(w_end, k, lo)
                e_rows = jnp.where(slo, _col(e_last, 2 * k), _col(e_last, 2 * k + 1))
                st[sl, :] = hp * e_rows + _dot_tn(zmat, bm)
        y = y_ref[...]
        z = z_ref[...]
        yz = y * (z * _sigmoid(z))
        ys_ref[...] = _rms_tile(yz, nw_ref[...]).astype(ys_ref.dtype)

    r = lambda b, c: b * nc + c
    vec = lambda w: pl.BlockSpec((1, w), lambda b, c: (0, 0))
    return pl.pallas_call(
        body, name=name, grid=(B, nc),
        in_specs=[pl.BlockSpec((L, W), lambda b, c: (r(b, c), 0)),
                  pl.BlockSpec((L, GN), lambda b, c: (r(b, c), W // GN)),
                  pl.BlockSpec((L, GN), lambda b, c: (r(b, c), W // GN + 1)),
                  pl.BlockSpec((L, LANE), lambda b, c: (r(b, c), PDT0 // LANE)),
                  pl.BlockSpec((L, W), lambda b, c: (r(b, c), PZ0 // W)),
                  vec(LANE), vec(LANE), vec(W), vec(W)],
        out_specs=[pl.BlockSpec((L, W), lambda b, c: (r(b, c), 0)), pl.BlockSpec((L, W), lambda b, c: (r(b, c), 0)),
                   pl.BlockSpec((W, N), lambda b, c: (r(b, c), 0))],
        out_shape=[_sds((T, W), F32), _sds((T, W), MXU_DTYPE), _sds((T // L * W, N), F32)],
        scratch_shapes=[pltpu.VMEM((W, N), F32)],
        compiler_params=_cparams(("parallel", "arbitrary")),
    )(xc, xc, xc, proj, proj, dtb, alog, dchan, normw)


def ssd_bwd(proj, xc, ypre, hin, dmix, dtb, alog, dchan, normw, S, *, name):
    T = proj.shape[0]
    L, nc = _ssd_specs(S)
    B, W, N = T // S, SSD_WIDTH, SSD_STATE

    def body(xs_ref, bs_ref, cs_ref, dt_ref, z_ref, y_ref, hin_ref, dys_ref, dtb_ref, alog_ref, dch_ref, nw_ref,
             dxc_ref, ddt_ref, dz_ref, sm_ref, dnw_ref, dst):
        step = pl.program_id(0) * nc + pl.program_id(1)

        @pl.when(pl.program_id(1) == 0)
        def _():
            dst[...] = jnp.zeros(dst.shape, F32)

        pre, dt, a, tri, cum, cum_t = _ssd_decay(dt_ref[...], dtb_ref[...], alog_ref[...])
        e_cum = jnp.exp(cum)
        last = cum[L - 1:L, :]
        w_end = jnp.exp(last - cum)
        e_last = jnp.exp(last)
        lane = lax.broadcasted_iota(jnp.int32, (1, LANE), 1)
        sub = lax.broadcasted_iota(jnp.int32, (LANE, 1), 0)
        lo, slo = lane < SSD_HEAD_DIM, sub < SSD_HEAD_DIM
        is_last_row = sub == L - 1

        y, z, nw = y_ref[...], z_ref[...], nw_ref[...]
        sg = _sigmoid(z)
        gate = z * sg
        dyz, dnw = _rms_bwd_tile(y * gate, nw, dys_ref[...])
        dy_all = dyz * gate
        dz_ref[...] = (dyz * y * (sg * (1.0 + z * (1.0 - sg)))).astype(dz_ref.dtype)

        d_cum = jnp.zeros((L, LANE), F32)
        d_cum_t = jnp.zeros((LANE, L), F32)
        d_dt = jnp.zeros((L, LANE), F32)
        d_dskip = jnp.zeros((1, LANE), F32)
        for g in range(SSD_GROUPS):
            bm = bs_ref[:, g * N:(g + 1) * N]
            cm = cs_ref[:, g * N:(g + 1) * N]
            gmat = _dot_nt(cm, bm)
            d_g = jnp.zeros((L, L), F32)
            d_bm = jnp.zeros((L, N), F32)
            d_cm = jnp.zeros((L, N), F32)
            for kk in range(PAIRS_PER_GROUP):
                k = g * PAIRS_PER_GROUP + kk
                sl = slice(k * LANE, (k + 1) * LANE)
                xv = xs_ref[:, sl]
                dyv = dy_all[:, sl]
                dt_sel = _pair_sel(dt, k, lo)
                xdt = xv * dt_sel
                hp = hin_ref[sl, :]
                dh_out = dst[sl, :]
                e_sel = _pair_sel(e_cum, k, lo)
                w_sel = _pair_sel(w_end, k, lo)
                y_off = _dot_nt(cm, hp) * e_sel
                zmat = xdt * w_sel
                d_z = _dot_nt(bm, dh_out)
                d_bm = d_bm + _dot(zmat, dh_out)
                d_xdt = d_z * w_sel
                dw_full = d_z * zmat
                e_rows = jnp.where(slo, _col(e_last, 2 * k), _col(e_last, 2 * k + 1))
                hh = jnp.sum(dh_out * hp, axis=1, keepdims=True) * e_rows
                d_r = dyv * e_sel
                d_cm = d_cm + _dot(d_r, hp)
                dst[sl, :] = dh_out * e_rows + _dot_tn(d_r, cm)
                dyoff_full = dyv * y_off
                for j, h in enumerate((2 * k, 2 * k + 1)):
                    mine = lo if j == 0 else jnp.logical_not(lo)
                    smine = slo if j == 0 else jnp.logical_not(slo)
                    hot = lane == h
                    dyh = jnp.where(mine, dyv, 0.0)
                    gam = jnp.exp(jnp.where(tri, _col(cum, h) - cum_t[h:h + 1, :], -jnp.inf))
                    mx = gmat * gam
                    d_xdt = d_xdt + _dot_tn(mx, dyh)
                    d_mx = jnp.where(tri, _dot_nt(dyh, xdt), 0.0)
                    d_g = d_g + d_mx * gam
                    d_seg = d_mx * mx
                    row_l = (jnp.sum(d_seg, axis=1, keepdims=True)
                             + jnp.sum(jnp.where(mine, dyoff_full - dw_full, 0.0), axis=1, keepdims=True))
                    at_end = (jnp.sum(jnp.where(mine, dw_full, 0.0), keepdims=True)
                              + jnp.sum(jnp.where(smine, hh, 0.0), keepdims=True))
                    d_cum = d_cum + jnp.where(hot, row_l + jnp.where(is_last_row, at_end, 0.0), 0.0)
                    d_cum_t = d_cum_t - jnp.where(sub == h, jnp.sum(d_seg, axis=0, keepdims=True), 0.0)
                    d_dskip = d_dskip + jnp.where(hot, jnp.sum(jnp.where(mine, dyv * xv, 0.0), keepdims=True), 0.0)
                for j, h in enumerate((2 * k, 2 * k + 1)):
                    mine = lo if j == 0 else jnp.logical_not(lo)
                    d_dt = d_dt + jnp.where(lane == h, jnp.sum(jnp.where(mine, d_xdt * xv, 0.0), axis=1, keepdims=True), 0.0)
                dxc_ref[:, sl] = d_xdt * dt_sel + dyv * dch_ref[:, sl]
            dxc_ref[:, W + g * N:W + (g + 1) * N] = d_bm + _dot_tn(d_g, cm)
            dxc_ref[:, W + GN + g * N:W + GN + (g + 1) * N] = d_cm + _dot(d_g, bm)

        d_cum = d_cum + d_cum_t.T
        d_da = _dot_hi(jnp.logical_not(tri).astype(F32) + (lax.broadcasted_iota(jnp.int32, (L, L), 0)
                                                              == lax.broadcasted_iota(jnp.int32, (L, L), 1)).astype(F32), d_cum)
        d_dt = d_dt + d_da * a
        heads = lane < SSD_HEADS
        d_pre = jnp.where(heads, d_dt * _sigmoid(pre), 0.0)
        ddt_ref[...] = d_pre.astype(ddt_ref.dtype)
        d_alog = jnp.sum(d_da * dt, axis=0, keepdims=True) * a
        part = jnp.concatenate([jnp.where(heads, d_alog, 0.0), jnp.sum(d_pre, axis=0, keepdims=True), d_dskip,
                                jnp.zeros((5, LANE), F32)], axis=0)

        @pl.when(step == 0)
        def _():
            sm_ref[...] = part
            dnw_ref[...] = dnw

        @pl.when(step > 0)
        def _():
            sm_ref[...] += part
            dnw_ref[...] += dnw

    r = lambda b, c: b * nc + (nc - 1 - c)
    vec = lambda w: pl.BlockSpec((1, w), lambda b, c: (0, 0))
    blk = lambda w, j: pl.BlockSpec((L, w), lambda b, c: (r(b, c), j))
    return pl.pallas_call(
        body, name=name, grid=(B, nc),
        in_specs=[blk(W, 0), blk(GN, W // GN), blk(GN, W // GN + 1), blk(LANE, PDT0 // LANE), blk(W, PZ0 // W),
                  blk(W, 0), pl.BlockSpec((W, N), lambda b, c: (r(b, c), 0)), blk(W, 0),
                  vec(LANE), vec(LANE), vec(W), vec(W)],
        out_specs=[blk(SSD_CONV_CH, 0), blk(LANE, 0), blk(W, 0), pl.BlockSpec((8, LANE), lambda b, c: (0, 0)), vec(W)],
        out_shape=[_sds((T, SSD_CONV_CH), F32), _sds((T, LANE), MXU_DTYPE), _sds((T, W), MXU_DTYPE),
                   _sds((8, LANE), F32), _sds((1, W), F32)],
        scratch_shapes=[pltpu.VMEM((W, N), F32)],
        compiler_params=_cparams(("arbitrary", "arbitrary")),
    )(xc, xc, xc, proj, proj, ypre, hin, dmix, dtb, alog, dchan, normw)


def _adamw_math(w, g, m, v):
    m = ADAM_B1 * m + (1.0 - ADAM_B1) * g
    v = ADAM_B2 * v + (1.0 - ADAM_B2) * (g * g)
    m_hat = m / (1.0 - ADAM_B1 ** ADAM_STEP)
    v_hat = v / (1.0 - ADAM_B2 ** ADAM_STEP)
    delta = -ADAM_LR * (m_hat / (jnp.sqrt(v_hat) + ADAM_EPS) + ADAM_WD * w)
    return delta, m, v


def adamw(w, g_parts, m, v, *, name, tr=256):
    R, C = w.shape
    tr = _pick(R, (tr, 128, 64, 32, 16, 8))
    n = len(g_parts)

    def body(*refs):
        w_ref, m_ref, v_ref = refs[0], refs[1 + n], refs[2 + n]
        g_ref, d_ref, nm_ref, nv_ref = refs[3 + n:]
        g = refs[1][...]
        for p in refs[2:1 + n]:
            g = g + p[...]
        g_ref[...] = g
        d_ref[...], nm_ref[...], nv_ref[...] = _adamw_math(w_ref[...], g, m_ref[...], v_ref[...])

    spec = pl.BlockSpec((tr, C), lambda i: (i, 0))
    return pl.pallas_call(
        body, name=name, grid=(R // tr,), in_specs=[spec] * (3 + n), out_specs=[spec] * 4,
        out_shape=[_sds((R, C), F32)] * 4, compiler_params=_cparams(("parallel",)),
    )(w, *g_parts, m, v)


def _my_place():
    return lax.axis_index("x"), lax.axis_index("y"), lax.axis_index("c")


def chip_exchange(src, *, per_dest, name):
    R, C = src.shape[-2:]

    def body(src_ref, out_ref, send_sems, recv_sems, local_sem):
        x, y, c = _my_place()
        me = 2 * x + y
        peers = [(1 - x, y), (x, 1 - y), (1 - x, 1 - y)]
        own = pltpu.make_async_copy(src_ref.at[me] if per_dest else src_ref, out_ref.at[me], local_sem)
        own.start()
        copies = []
        for k, (px, py) in enumerate(peers):
            dest = 2 * px + py
            copies.append(pltpu.make_async_remote_copy(
                src_ref=src_ref.at[dest] if per_dest else src_ref, dst_ref=out_ref.at[me],
                send_sem=send_sems.at[k], recv_sem=recv_sems.at[k],
                device_id=(px, py, c), device_id_type=pl.DeviceIdType.MESH))
        for cp in copies:
            cp.start()
        for k, (px, py) in enumerate(peers):
            pltpu.make_async_remote_copy(
                src_ref=src_ref.at[0] if per_dest else src_ref, dst_ref=out_ref.at[2 * px + py],
                send_sem=send_sems.at[k], recv_sem=recv_sems.at[k],
                device_id=(px, py, c), device_id_type=pl.DeviceIdType.MESH).wait_recv()
        for cp in copies:
            cp.wait_send()
        own.wait()

    return pl.pallas_call(
        body, name=name, in_specs=[pl.BlockSpec(memory_space=pl.ANY)], out_specs=pl.BlockSpec(memory_space=pl.ANY),
        out_shape=_sds((N_CHIPS, R, C), src.dtype),
        scratch_shapes=[pltpu.SemaphoreType.DMA((3,)), pltpu.SemaphoreType.DMA((3,)), pltpu.SemaphoreType.DMA],
        compiler_params=pltpu.CompilerParams(has_side_effects=True),
    )(src)


def sibling_exchange(src, *, name):
    R, C = src.shape

    def body(src_ref, out_ref, send_sem, recv_sem, local_sem):
        x, y, c = _my_place()
        own = pltpu.make_async_copy(src_ref, out_ref.at[c], local_sem)
        own.start()
        cp = pltpu.make_async_remote_copy(
            src_ref=src_ref, dst_ref=out_ref.at[c], send_sem=send_sem, recv_sem=recv_sem,
            device_id=(x, y, 1 - c), device_id_type=pl.DeviceIdType.MESH)
        cp.start()
        pltpu.make_async_remote_copy(
            src_ref=src_ref, dst_ref=out_ref.at[1 - c], send_sem=send_sem, recv_sem=recv_sem,
            device_id=(x, y, 1 - c), device_id_type=pl.DeviceIdType.MESH).wait_recv()
        cp.wait_send()
        own.wait()

    return pl.pallas_call(
        body, name=name, in_specs=[pl.BlockSpec(memory_space=pl.ANY)], out_specs=pl.BlockSpec(memory_space=pl.ANY),
        out_shape=_sds((2, R, C), src.dtype),
        scratch_shapes=[pltpu.SemaphoreType.DMA, pltpu.SemaphoreType.DMA, pltpu.SemaphoreType.DMA],
        compiler_params=pltpu.CompilerParams(has_side_effects=True),
    )(src)


def all_sum_small(vec, *, name):
    R, C = vec.shape

    def body(v_ref, out_ref, buf, send_sems, recv_sems):
        x, y, c = _my_place()
        me = 4 * x + 2 * y + c
        buf[me] = v_ref[...]
        copies = []
        for k in range(1, N_DEV):
            px, py, pc = x ^ (k >> 2), y ^ ((k >> 1) & 1), c ^ (k & 1)
            copies.append(pltpu.make_async_remote_copy(
                src_ref=v_ref, dst_ref=buf.at[me], send_sem=send_sems.at[k - 1], recv_sem=recv_sems.at[k - 1],
                device_id=(px, py, pc), device_id_type=pl.DeviceIdType.MESH))
        for cp in copies:
            cp.start()
        for k in range(1, N_DEV):
            px, py, pc = x ^ (k >> 2), y ^ ((k >> 1) & 1), c ^ (k & 1)
            pltpu.make_async_remote_copy(
                src_ref=v_ref, dst_ref=buf.at[4 * px + 2 * py + pc], send_sem=send_sems.at[k - 1],
                recv_sem=recv_sems.at[k - 1], device_id=(px, py, pc), device_id_type=pl.DeviceIdType.MESH).wait_recv()
        for cp in copies:
            cp.wait_send()
        acc = buf[0]
        for d in range(1, N_DEV):
            acc = acc + buf[d]
        out_ref[...] = acc

    return pl.pallas_call(
        body, name=name, in_specs=[pl.BlockSpec(memory_space=pltpu.VMEM)], out_specs=pl.BlockSpec(memory_space=pltpu.VMEM),
        out_shape=_sds((R, C), F32),
        scratch_shapes=[pltpu.VMEM((N_DEV, R, C), F32), pltpu.SemaphoreType.DMA((N_DEV - 1,)),
                        pltpu.SemaphoreType.DMA((N_DEV - 1,))],
        compiler_params=pltpu.CompilerParams(has_side_effects=True, vmem_limit_bytes=VMEM_LIMIT),
    )(vec)


def sum4(parts, *, name, tr=400):
    _, R, C = parts.shape
    tr = _pick(R, (tr, 200, 100, 50, 25, 8))

    def body(p_ref, o_ref):
        acc = p_ref[0].astype(F32)
        for j in range(1, N_CHIPS):
            acc = acc + p_ref[j].astype(F32)
        o_ref[...] = acc

    return pl.pallas_call(
        body, name=name, grid=(R // tr,), in_specs=[pl.BlockSpec((N_CHIPS, tr, C), lambda i: (0, i, 0))],
        out_specs=pl.BlockSpec((tr, C), lambda i: (i, 0)), out_shape=_sds((R, C), F32),
        compiler_params=_cparams(("parallel",)),
    )(parts)


WEIGHTS = ['attn_norm', 'w_in', 'ssd_conv_w', 'ssd_conv_b', 'ssd_dt_bias', 'ssd_a_log', 'ssd_d', 'ssd_norm', 'pool_w',
           'pool_scale', 'mla_q_norm', 'mla_w_uq', 'mla_kv_norm', 'mla_w_ukv', 'w_out', 'ffn_norm', 'ffn_w_up',
           'ffn_conv_w', 'ffn_conv_b', 'ffn_w_down', 'final_norm']
BIG = {'w_in': 2, 'mla_w_uq': 2, 'mla_w_ukv': 2, 'w_out': 1, 'ffn_w_up': 2, 'ffn_w_down': 1}
CONV_SHARDED = ('ssd_conv_w', 'ffn_conv_w')
PACK_COLS = 512


def _zeros_cols(w, n):
    return jnp.zeros((w.shape[0], n), w.dtype)


def _w_in_to_padded(w):
    return jnp.concatenate([w[:, 0:2560], w[:, 2576:3088], w[:, 3088:3472], w[:, 2560:2576], _zeros_cols(w, 112),
                            w[:, 3472:3728], _zeros_cols(w, 64), w[:, 3728:3760], _zeros_cols(w, 32 + 128)], axis=1)


def _w_in_from_padded(g):
    return jnp.concatenate([g[:, 0:2560], g[:, PDT0:PDT0 + SSD_HEADS], g[:, PU0:PU0 + POOL_WIDTH],
                            g[:, PCQ0:PCQ0 + MLA_Q_RANK], g[:, PCKV0:PCKV0 + MLA_KV_RANK],
                            g[:, PKPE0 + ROPE0:PKPE0 + ROPE0 + MLA_ROPE]], axis=1)


def _w_uq_to_padded(w):
    r = w.reshape(MLA_Q_RANK, MLA_HEADS, MLA_QK)
    return jnp.pad(r, ((0, 0), (0, 0), (0, HEAD_W - MLA_QK))).reshape(MLA_Q_RANK, MLA_PAD)


def _w_uq_from_padded(g):
    return g.reshape(MLA_Q_RANK, MLA_HEADS, HEAD_W)[:, :, :MLA_QK].reshape(MLA_Q_RANK, MLA_HEADS * MLA_QK)


def _w_ukv_to_padded(w):
    r = w.reshape(MLA_KV_RANK, MLA_HEADS, MLA_NOPE + MLA_V)
    pad = lambda t: jnp.pad(t, ((0, 0), (0, 0), (0, HEAD_W - t.shape[2]))).reshape(MLA_KV_RANK, MLA_PAD)
    return jnp.concatenate([pad(r[:, :, :MLA_NOPE]), pad(r[:, :, MLA_NOPE:])], axis=1)


def _w_ukv_from_padded(g):
    kk = g[:, :MLA_PAD].reshape(MLA_KV_RANK, MLA_HEADS, HEAD_W)[:, :, :MLA_NOPE]
    vv = g[:, MLA_PAD:].reshape(MLA_KV_RANK, MLA_HEADS, HEAD_W)[:, :, :MLA_V]
    return jnp.concatenate([kk, vv], axis=2).reshape(MLA_KV_RANK, MLA_HEADS * (MLA_NOPE + MLA_V))


def _w_out_to_padded(w):
    att = w[SSD_WIDTH + POOL_WIDTH:].reshape(MLA_HEADS, MLA_V, D_MODEL)
    att = jnp.pad(att, ((0, 0), (0, HEAD_W - MLA_V), (0, 0))).reshape(MLA_PAD, D_MODEL)
    return jnp.concatenate([w[:SSD_WIDTH + POOL_WIDTH], att], axis=0)


def _w_out_from_padded(g):
    att = g[SSD_WIDTH + POOL_WIDTH:].reshape(MLA_HEADS, HEAD_W, D_MODEL)[:, :MLA_V].reshape(MLA_WIDTH, D_MODEL)
    return jnp.concatenate([g[:SSD_WIDTH + POOL_WIDTH], att], axis=0)


def _pad_lanes(v, n=LANE):
    return jnp.pad(v.reshape(1, -1), ((0, 0), (0, n - v.size)))


def _pack_rows(parts, cols, dtype):
    flat = jnp.concatenate([p.astype(dtype).reshape(-1) for p in parts])
    rows = -(-flat.size // (cols * 16)) * 16
    return jnp.pad(flat, (0, rows * cols - flat.size)).reshape(rows, cols)


def _unpack_rows(packed, shapes):
    flat = packed.reshape(-1)
    out, at = [], 0
    for s in shapes:
        n = math.prod(s)
        out.append(flat[at:at + n].reshape(s))
        at += n
    return out


def _shard_shape(full_shape, axis):
    s = list(full_shape)
    s[axis] //= N_CHIPS
    return tuple(s)


def _take_shard(a, axis, j):
    n = a.shape[axis] // N_CHIPS
    return lax.slice_in_dim(a, j * n, (j + 1) * n, axis=axis)


def _layer_weights(full, small, l):
    w = {}
    w['w_in_p'] = _w_in_to_padded(full['w_in'][l])
    w['w_uq_p'] = _w_uq_to_padded(full['mla_w_uq'][l])
    w['w_ukv_p'] = _w_ukv_to_padded(full['mla_w_ukv'][l])
    w['w_out_p'] = _w_out_to_padded(full['w_out'][l])
    w['w_up'] = full['ffn_w_up'][l]
    w['w_down'] = full['ffn_w_down'][l]
    for k in ('w_in_p', 'w_uq_p', 'w_ukv_p', 'w_out_p', 'w_up', 'w_down'):
        w[k + 't'] = w[k].T
    for k in ('attn_norm', 'ssd_conv_w', 'ssd_conv_b', 'ssd_norm', 'pool_w', 'pool_scale', 'mla_q_norm', 'mla_kv_norm',
              'ffn_norm', 'ffn_conv_w', 'ffn_conv_b'):
        w[k] = small[k][l]
    w['dtb'] = _pad_lanes(small['ssd_dt_bias'][l])
    w['alog'] = _pad_lanes(small['ssd_a_log'][l])
    w['dchan'] = jnp.repeat(small['ssd_d'][l], SSD_HEAD_DIM).reshape(1, SSD_WIDTH)
    w['ssd_norm'] = w['ssd_norm'].reshape(1, SSD_WIDTH)
    return w


def _layer_fwd(x, pos, invf, w, S, l):
    n = lambda s: f"{s}_l{l}"
    h1 = rmsnorm_fwd(x, w['attn_norm'], name=n("attn_norm"))
    proj = matmul(h1, w['w_in_p'], name=n("w_in"))
    xc = ssd_conv_fwd(proj, w['ssd_conv_w'], w['ssd_conv_b'], S, name=n("ssd_conv"))
    ypre, yssd, hin = ssd_fwd(proj, xc, w['dtb'], w['alog'], w['dchan'], w['ssd_norm'], S, name=n("ssd_scan"))
    ypool, pooled = pool_fwd(proj, w['pool_w'], w['pool_scale'], S, name=n("pool"))
    q, k, v, cqn, ckvn = mla_prep_fwd(proj, pos, invf, w['mla_q_norm'], w['w_uq_p'], w['mla_kv_norm'], w['w_ukv_p'],
                                      name=n("mla_prep"))
    o, lse = flash_fwd(q, k, v, S, name=n("attention"))
    mix = jnp.concatenate([yssd, ypool, o.astype(MXU_DTYPE)], axis=1)
    x2 = matmul(mix, w['w_out_p'], res=x, name=n("w_out"))
    h2 = rmsnorm_fwd(x2, w['ffn_norm'], name=n("ffn_norm"))
    up = matmul(h2, w['w_up'], name=n("ffn_up"))
    act = ffn_conv_gate_fwd(up, w['ffn_conv_w'], w['ffn_conv_b'], S, name=n("ffn_conv_gate"))
    x3 = matmul(act, w['w_down'], res=x2, name=n("ffn_down"))
    saved = dict(x=x, h1=h1, proj=proj, xc=xc, ypre=ypre, hin=hin, pooled=pooled, q=q, k=k, v=v, cqn=cqn, ckvn=ckvn,
                 o=o, lse=lse, mix=mix, x2=x2, h2=h2, up=up, act=act)
    return x3, saved


def _layer_bwd(dx3, pos, invf, w, s, S, l):
    n = lambda t: f"{t}_l{l}"
    g = {}
    dact = matmul(dx3, w['w_downt'], name=n("d_ffn_down"))
    g['ffn_w_down'] = matmul_tn(s['act'], dx3, name=n("g_ffn_down"))
    dgate, dval, st = ffn_conv_gate_bwd(s['up'], w['ffn_conv_w'], w['ffn_conv_b'], dact, S, name=n("d_ffn_conv_gate"))
    g['ffn_conv_w'], g['ffn_conv_b'] = st[:FFN_CONV], st[FFN_CONV]
    dup_g = conv_bwd_x(dgate, w['ffn_conv_w'][:, :D_FF], S, FFN_CONV, name=n("d_ffn_conv_g"))
    dup_v = conv_bwd_x(dval, w['ffn_conv_w'][:, D_FF:], S, FFN_CONV, name=n("d_ffn_conv_v"))
    dh2 = matmul(dup_v, w['w_upt'][D_FF:], res=matmul(dup_g, w['w_upt'][:D_FF], name=n("d_ffn_up_g")), name=n("d_ffn_up_v"))
    g['ffn_w_up'] = jnp.concatenate([matmul_tn(s['h2'], dup_g, name=n("g_ffn_up_g")),
                                     matmul_tn(s['h2'], dup_v, name=n("g_ffn_up_v"))], axis=1)
    dx2, gn = rmsnorm_bwd(s['x2'], w['ffn_norm'], dh2, dx3, name=n("d_ffn_norm"))
    g['ffn_norm'] = gn[0]
    dmix = matmul(dx2, w['w_out_pt'], name=n("d_w_out"))
    g['w_out'] = _w_out_from_padded(matmul_tn(s['mix'], dx2, name=n("g_w_out")))
    dxc, ddt, dz, sm, gsn = ssd_bwd(s['proj'], s['xc'], s['ypre'], s['hin'], dmix, w['dtb'], w['alog'], w['dchan'],
                                    w['ssd_norm'], S, name=n("d_ssd_scan"))
    g['ssd_a_log'], g['ssd_dt_bias'], g['ssd_d'] = sm[0, :SSD_HEADS], sm[1, :SSD_HEADS], sm[2, :SSD_HEADS]
    g['ssd_norm'] = gsn[0]
    dpre, st = ssd_conv_bwd_pre(s['proj'], w['ssd_conv_w'], w['ssd_conv_b'], dxc, S, name=n("d_ssd_conv_act"))
    g['ssd_conv_w'], g['ssd_conv_b'] = st[:SSD_CONV], st[SSD_CONV]
    dxbc = conv_bwd_x(dpre, w['ssd_conv_w'], S, SSD_CONV, name=n("d_ssd_conv"))
    du, g['pool_w'], gps = pool_bwd(dmix, s['pooled'], w['pool_w'], w['pool_scale'], S, name=n("d_pool"))
    g['pool_scale'] = gps[0]
    dq, dk, dv = flash_bwd(s['q'], s['k'], s['v'], s['o'], s['lse'], dmix, S, name=n("d_attention"))
    dqp, dkvp, dcq, dckv, dkpe, gqn, gkn = mla_prep_bwd(s['proj'], pos, invf, w['mla_q_norm'], w['w_uq_pt'],
                                                        w['mla_kv_norm'], w['w_ukv_pt'], dq, dk, dv, name=n("d_mla_prep"))
    g['mla_q_norm'], g['mla_kv_norm'] = gqn[0], gkn[0]
    g['mla_w_uq'] = _w_uq_from_padded(matmul_tn(s['cqn'], dqp, name=n("g_w_uq")))
    g['mla_w_ukv'] = _w_ukv_from_padded(matmul_tn(s['ckvn'], dkvp, name=n("g_w_ukv")))
    dproj = jnp.concatenate([dz, dxbc, du, dcq, ddt, dckv, dkpe, jnp.zeros_like(dkpe)], axis=1)
    dh1 = matmul(dproj, w['w_in_pt'], name=n("d_w_in"))
    g['w_in'] = _w_in_from_padded(matmul_tn(s['h1'], dproj, name=n("g_w_in")))
    dx, gn = rmsnorm_bwd(s['x'], w['attn_norm'], dh1, dx2, name=n("d_attn_norm"))
    g['attn_norm'] = gn[0]
    return dx, g


def _local_step(x, positions, target, full, small, S):
    pos = positions.reshape(-1, 1).astype(F32)
    inv_freq = ROPE_THETA ** (-jnp.arange(0, MLA_ROPE, 2, dtype=F32) / MLA_ROPE)
    invf = jnp.concatenate([jnp.zeros((ROPE0,), F32), inv_freq, inv_freq,
                            jnp.zeros((HEAD_W - ROPE0 - MLA_ROPE,), F32)]).reshape(1, HEAD_W)
    ws = [_layer_weights(full, small, l) for l in range(DEPTH)]
    saved = []
    h = x
    for l in range(DEPTH):
        h, s = _layer_fwd(h, pos, invf, ws[l], S, l)
        saved.append(s)
    loss, dh, gfn = final_loss(h, small['final_norm'], target)
    layer_grads = [None] * DEPTH
    for l in reversed(range(DEPTH)):
        dh, layer_grads[l] = _layer_bwd(dh, pos, invf, ws[l], saved[l], S, l)
    grads = {k: jnp.stack([layer_grads[l][k] for l in range(DEPTH)]) for k in layer_grads[0]}
    grads['final_norm'] = gfn[0]
    return loss, dh, grads


def kernel(x, positions, attn_norm, w_in, ssd_conv_w, ssd_conv_b, ssd_dt_bias, ssd_a_log, ssd_d, ssd_norm, pool_w, pool_scale, mla_q_norm, mla_w_uq, mla_kv_norm, mla_w_ukv, w_out, ffn_norm, ffn_w_up, ffn_conv_w, ffn_conv_b, ffn_w_down, final_norm, loss_target, m_attn_norm, m_w_in, m_ssd_conv_w, m_ssd_conv_b, m_ssd_dt_bias, m_ssd_a_log, m_ssd_d, m_ssd_norm, m_pool_w, m_pool_scale, m_mla_q_norm, m_mla_w_uq, m_mla_kv_norm, m_mla_w_ukv, m_w_out, m_ffn_norm, m_ffn_w_up, m_ffn_conv_w, m_ffn_conv_b, m_ffn_w_down, m_final_norm, v_attn_norm, v_w_in, v_ssd_conv_w, v_ssd_conv_b, v_ssd_dt_bias, v_ssd_a_log, v_ssd_d, v_ssd_norm, v_pool_w, v_pool_scale, v_mla_q_norm, v_mla_w_uq, v_mla_kv_norm, v_mla_w_ukv, v_w_out, v_ffn_norm, v_ffn_w_up, v_ffn_conv_w, v_ffn_conv_b, v_ffn_w_down, v_final_norm):
    wv = dict(zip(WEIGHTS, (attn_norm, w_in, ssd_conv_w, ssd_conv_b, ssd_dt_bias, ssd_a_log, ssd_d, ssd_norm, pool_w,
                            pool_scale, mla_q_norm, mla_w_uq, mla_kv_norm, mla_w_ukv, w_out, ffn_norm, ffn_w_up,
                            ffn_conv_w, ffn_conv_b, ffn_w_down, final_norm)))
    mv = dict(zip(WEIGHTS, (m_attn_norm, m_w_in, m_ssd_conv_w, m_ssd_conv_b, m_ssd_dt_bias, m_ssd_a_log, m_ssd_d,
                            m_ssd_norm, m_pool_w, m_pool_scale, m_mla_q_norm, m_mla_w_uq, m_mla_kv_norm, m_mla_w_ukv,
                            m_w_out, m_ffn_norm, m_ffn_w_up, m_ffn_conv_w, m_ffn_conv_b, m_ffn_w_down, m_final_norm)))
    vv = dict(zip(WEIGHTS, (v_attn_norm, v_w_in, v_ssd_conv_w, v_ssd_conv_b, v_ssd_dt_bias, v_ssd_a_log, v_ssd_d,
                            v_ssd_norm, v_pool_w, v_pool_scale, v_mla_q_norm, v_mla_w_uq, v_mla_kv_norm, v_mla_w_ukv,
                            v_w_out, v_ffn_norm, v_ffn_w_up, v_ffn_conv_w, v_ffn_conv_b, v_ffn_w_down, v_final_norm)))
    Bl, S, D = x.shape
    chip = 2 * lax.axis_index("x") + lax.axis_index("y")
    core = lax.axis_index("c")

    big_names = list(BIG)
    packed = _pack_rows([wv[k] for k in big_names], PACK_COLS, MXU_DTYPE)
    gathered = chip_exchange(packed, per_dest=False, name="gather_weights")
    shard_shapes = [wv[k].shape for k in big_names]
    per_chip = [_unpack_rows(gathered[j], shard_shapes) for j in range(N_CHIPS)]
    full = {k: jnp.concatenate([per_chip[j][i] for j in range(N_CHIPS)], axis=BIG[k]) for i, k in enumerate(big_names)}
    placed = []
    for k in CONV_SHARDED:
        sh = wv[k]
        whole = jnp.zeros(sh.shape[:-1] + (sh.shape[-1] * N_CHIPS,), F32)
        whole = lax.dynamic_update_slice_in_dim(whole, sh, chip * sh.shape[-1], axis=sh.ndim - 1)
        placed.append(jnp.where(core == 1, whole, 0.0))
    conv_full = _unpack_rows(all_sum_small(_pack_rows(placed, LANE, F32), name="gather_conv_weights"),
                             [p.shape for p in placed])
    small = {k: wv[k] for k in WEIGHTS if k not in BIG}
    small.update(dict(zip(CONV_SHARDED, conv_full)))

    loss, dx, grads = _local_step(x.reshape(Bl * S, D), positions, loss_target.reshape(Bl * S, D), full, small, S)

    send = jnp.stack([_pack_rows([_take_shard(grads[k], BIG[k], j) for k in big_names], PACK_COLS, MXU_DTYPE)
                      for j in range(N_CHIPS)])
    recv = chip_exchange(send, per_dest=True, name="scatter_grads")
    both = sibling_exchange(sum4(recv, name="sum_chips"), name="swap_core_sums")
    parts = [_unpack_rows(both[c], shard_shapes) for c in range(2)]
    small_names = [k for k in WEIGHTS if k not in BIG]
    small_sum = all_sum_small(_pack_rows([grads[k] for k in small_names] + [loss[0, :1]], LANE, F32), name="sum_small_grads")
    summed = _unpack_rows(small_sum, [grads[k].shape for k in small_names] + [(1,)])
    loss_total = summed[-1].reshape(())
    g_small = dict(zip(small_names, summed[:-1]))
    for k in CONV_SHARDED:
        n = wv[k].shape[-1]
        g_small[k] = lax.dynamic_slice_in_dim(g_small[k], chip * n, n, axis=g_small[k].ndim - 1)

    out_g, out_d, out_m, out_v = {}, {}, {}, {}
    for i, k in enumerate(big_names):
        shp = wv[k].shape
        two_d = lambda a: a.reshape(-1, shp[-1])
        res = adamw(two_d(wv[k]), [two_d(parts[0][i]), two_d(parts[1][i])], two_d(mv[k]), two_d(vv[k]), name=f"adamw_{k}")
        out_g[k], out_d[k], out_m[k], out_v[k] = (r.reshape(shp) for r in res)
    shapes = [wv[k].shape for k in small_names]
    pk = lambda d: _pack_rows([d[k] for k in small_names], LANE, F32)
    res = adamw(pk(wv), [pk(g_small)], pk(mv), pk(vv), name="adamw_small")
    for dst, r in zip((out_g, out_d, out_m, out_v), res):
        dst.update(dict(zip(small_names, _unpack_rows(r, shapes))))
    return (loss_total, dx.reshape(Bl, S, D), *[out_g[k] for k in WEIGHTS], *[out_d[k] for k in WEIGHTS],
            *[out_m[k] for k in WEIGHTS], *[out_v[k] for k in WEIGHTS])
```

```python
import functools
import math

import jax
import jax.numpy as jnp
from jax import lax
from jax.experimental import pallas as pl
from jax.experimental.pallas import tpu as pltpu

F32 = jnp.float32
BF16 = jnp.bfloat16
MXU_DTYPE = jnp.bfloat16
HI = lax.Precision.HIGHEST

D_MODEL = 1024
DEPTH = 2
EPS = 1e-6
SSD_HEADS = 16
SSD_HEAD_DIM = 64
SSD_WIDTH = 1024
SSD_GROUPS = 2
SSD_STATE = 128
SSD_CONV = 4
SSD_CHUNK = 128
SSD_CONV_CH = 1536
POOL_GROUPS = 4
POOL_GROUP_DIM = 128
POOL_WIDTH = 512
POOL_WINDOWS = (2, 4, 8, 16)
MLA_HEADS = 8
MLA_Q_RANK = 384
MLA_KV_RANK = 256
MLA_NOPE = 64
MLA_ROPE = 32
MLA_V = 64
MLA_QK = 96
MLA_WIDTH = 512
ROPE_THETA = 10000.0
MIX_WIDTH = 2048
IN_COLS = 3760
D_FF = 2816
FFN_CONV = 3
ADAM_LR = 0.001
ADAM_B1 = 0.9
ADAM_B2 = 0.999
ADAM_EPS = 1e-08
ADAM_WD = 0.01
ADAM_STEP = 10

LANE = 128
HALO = 8
POOL_HALO = 16
PZ0 = 0
PXBC0 = 1024
PU0 = 2560
PCQ0 = 3072
PDT0 = 3456
PCKV0 = 3584
PKPE0 = 3840
PROJ_W = 4096
HEAD_W = 128
MLA_PAD = MLA_HEADS * HEAD_W
MIXP = SSD_WIDTH + POOL_WIDTH + MLA_PAD
N_CHIPS = 4
N_DEV = 8
VMEM_LIMIT = 56 * 1024 * 1024


def _cparams(dims, vmem=None):
    return pltpu.CompilerParams(dimension_semantics=dims, vmem_limit_bytes=vmem or VMEM_LIMIT)


def _sds(shape, dtype):
    return jax.ShapeDtypeStruct(tuple(shape), dtype)


def _mx(v):
    return v.astype(MXU_DTYPE)


def _dot(a, b):
    return jnp.dot(_mx(a), _mx(b), preferred_element_type=F32)


def _dot_nt(a, b):
    return lax.dot_general(_mx(a), _mx(b), (((1,), (1,)), ((), ())), preferred_element_type=F32)


def _dot_tn(a, b):
    return lax.dot_general(_mx(a), _mx(b), (((0,), (0,)), ((), ())), preferred_element_type=F32)


def _dot_hi(a, b):
    return jnp.dot(a, b, preferred_element_type=F32, precision=HI)


def _sigmoid(v):
    return 1.0 / (1.0 + jnp.exp(-v))


def _pick(n, prefs):
    for p in prefs:
        if n % p == 0:
            return p
    return n


def matmul(a, b, *, res=None, out_dtype=F32, name, nt=False, kblock=0, tm=None, tn=None):
    M, K = a.shape
    N = b.shape[0] if nt else b.shape[1]
    assert (b.shape[1] % K == 0) if nt else (K == b.shape[0] and kblock == 0)
    tm = tm or _pick(M, (1024, 512, 256, 128))
    tn = tn or _pick(N, (512, 1408, 256, 128))

    def body(*refs):
        a_ref, b_ref = refs[:2]
        o_ref = refs[-1]
        out = (_dot_nt if nt else _dot)(a_ref[...], b_ref[...])
        if res is not None:
            out = out + refs[2][...]
        o_ref[...] = out.astype(out_dtype)

    b_spec = pl.BlockSpec((tn, K), lambda i, j: (j, kblock)) if nt else pl.BlockSpec((K, tn), lambda i, j: (0, j))
    in_specs = [pl.BlockSpec((tm, K), lambda i, j: (i, 0)), b_spec]
    args = [a, b]
    if res is not None:
        in_specs.append(pl.BlockSpec((tm, tn), lambda i, j: (i, j)))
        args.append(res)
    return pl.pallas_call(
        body, name=name, grid=(M // tm, N // tn), in_specs=in_specs,
        out_specs=pl.BlockSpec((tm, tn), lambda i, j: (i, j)), out_shape=_sds((M, N), out_dtype),
        compiler_params=_cparams(("parallel", "parallel")),
    )(*args)


def matmul_tn(a, g, *, name, tm=None, tn=None, tk=None):
    T, M = a.shape
    T2, N = g.shape
    assert T == T2
    tm = tm or _pick(M, (512, 384, 256, 128))
    tn = tn or _pick(N, (2048, 1408, 1024, 512, 256, 128))
    tk = tk or _pick(T, (512, 256, 128))
    nk = T // tk

    def body(a_ref, g_ref, o_ref):
        k = pl.program_id(2)
        part = _dot_tn(a_ref[...], g_ref[...])

        @pl.when(k == 0)
        def _():
            o_ref[...] = part

        @pl.when(k > 0)
        def _():
            o_ref[...] += part

    return pl.pallas_call(
        body, name=name, grid=(M // tm, N // tn, nk),
        in_specs=[pl.BlockSpec((tk, tm), lambda i, j, k: (k, i)), pl.BlockSpec((tk, tn), lambda i, j, k: (k, j))],
        out_specs=pl.BlockSpec((tm, tn), lambda i, j, k: (i, j)), out_shape=_sds((M, N), F32),
        compiler_params=_cparams(("parallel", "parallel", "arbitrary")),
    )(a, g)


def rmsnorm_fwd(x, gamma, *, name, tm=512):
    T, D = x.shape
    tm = _pick(T, (tm, 256, 128))

    def body(x_ref, g_ref, o_ref):
        xv = x_ref[...]
        r = lax.rsqrt(jnp.mean(xv * xv, axis=-1, keepdims=True) + EPS)
        o_ref[...] = ((xv * r) * g_ref[...]).astype(MXU_DTYPE)

    return pl.pallas_call(
        body, name=name, grid=(T // tm,),
        in_specs=[pl.BlockSpec((tm, D), lambda i: (i, 0)), pl.BlockSpec((1, D), lambda i: (0, 0))],
        out_specs=pl.BlockSpec((tm, D), lambda i: (i, 0)), out_shape=_sds((T, D), MXU_DTYPE),
        compiler_params=_cparams(("parallel",)),
    )(x, gamma.reshape(1, D))


def _rms_bwd_tile(xv, gamma, dh):
    r = lax.rsqrt(jnp.mean(xv * xv, axis=-1, keepdims=True) + EPS)
    xh = xv * r
    dg = jnp.sum(dh * xh, axis=0, keepdims=True)
    dn = dh * gamma
    dx = r * (dn - xh * jnp.mean(dn * xh, axis=-1, keepdims=True))
    return dx, dg


def rmsnorm_bwd(x, gamma, dh, dres, *, name, tm=256):
    T, D = x.shape
    tm = _pick(T, (tm, 128))

    def body(x_ref, g_ref, dh_ref, dr_ref, dx_ref, dg_ref):
        dx, dg = _rms_bwd_tile(x_ref[...], g_ref[...], dh_ref[...].astype(F32))
        dx_ref[...] = dx + dr_ref[...]

        @pl.when(pl.program_id(0) == 0)
        def _():
            dg_ref[...] = dg

        @pl.when(pl.program_id(0) > 0)
        def _():
            dg_ref[...] += dg

    row = pl.BlockSpec((tm, D), lambda i: (i, 0))
    vec = pl.BlockSpec((1, D), lambda i: (0, 0))
    return pl.pallas_call(
        body, name=name, grid=(T // tm,), in_specs=[row, vec, row, row], out_specs=[row, vec],
        out_shape=[_sds((T, D), F32), _sds((1, D), F32)], compiler_params=_cparams(("arbitrary",)),
    )(x, gamma.reshape(1, D), dh, dres)


def final_loss(x, gamma, target, *, name="final_loss", tm=256):
    T, D = x.shape
    tm = _pick(T, (tm, 128))

    def body(x_ref, g_ref, t_ref, l_ref, dx_ref, dg_ref):
        xv = x_ref[...]
        gam = g_ref[...]
        r = lax.rsqrt(jnp.mean(xv * xv, axis=-1, keepdims=True) + EPS)
        y = (xv * r) * gam
        err = y - t_ref[...]
        part = 0.5 * jnp.sum(jnp.sum(err * err, axis=-1, keepdims=True) / D, axis=0, keepdims=True)
        dx, dg = _rms_bwd_tile(xv, gam, err / D)
        dx_ref[...] = dx

        @pl.when(pl.program_id(0) == 0)
        def _():
            dg_ref[...] = dg
            l_ref[...] = jnp.broadcast_to(part, l_ref.shape)

        @pl.when(pl.program_id(0) > 0)
        def _():
            dg_ref[...] += dg
            l_ref[...] += jnp.broadcast_to(part, l_ref.shape)

    row = pl.BlockSpec((tm, D), lambda i: (i, 0))
    vec = pl.BlockSpec((1, D), lambda i: (0, 0))
    return pl.pallas_call(
        body, name=name, grid=(T // tm,), in_specs=[row, vec, row],
        out_specs=[pl.BlockSpec((1, LANE), lambda i: (0, 0)), row, vec],
        out_shape=[_sds((1, LANE), F32), _sds((T, D), F32), _sds((1, D), F32)],
        compiler_params=_cparams(("arbitrary",)),
    )(x, gamma.reshape(1, D), target)


def _halo_prev(ts):
    return lambda i, j, off=0: (jnp.maximum(i * (ts // HALO) - 1, 0), j + off)


def _cat_prev(cur, halo, first):
    return jnp.concatenate([jnp.where(first, 0.0, halo), cur], axis=0)


def _cat_next(cur, halo, last):
    return jnp.concatenate([cur, jnp.where(last, 0.0, halo)], axis=0)


def _delayed(cat, r):
    if r == 0:
        return cat[HALO:]
    return pltpu.roll(cat, r, axis=0)[HALO:]


def _advanced(cat, r):
    n = cat.shape[0]
    if r == 0:
        return cat[:n - HALO]
    return pltpu.roll(cat, n - r, axis=0)[:n - HALO]


def _conv_pre(cat, w, b, K):
    acc = _delayed(cat, K - 1) * w[0:1, :] + b
    for k in range(1, K):
        acc = acc + _delayed(cat, K - 1 - k) * w[k:k + 1, :]
    return acc


def _pad_rows8(w):
    return jnp.pad(w, ((0, 8 - w.shape[0]), (0, 0)))


def ssd_conv_fwd(proj, w, b, S, *, name, ts=1024, tc=512):
    T = proj.shape[0]
    C, K = SSD_CONV_CH, SSD_CONV
    ts = _pick(S, (ts, 256, 128))
    off = PXBC0 // tc
    ns = S // ts

    def body(x_ref, h_ref, w_ref, b_ref, o_ref):
        first = (pl.program_id(0) % ns) == 0
        pre = _conv_pre(_cat_prev(x_ref[...], h_ref[...], first), w_ref[...], b_ref[...], K)
        o_ref[...] = pre * _sigmoid(pre)

    return pl.pallas_call(
        body, name=name, grid=(T // ts, C // tc),
        in_specs=[pl.BlockSpec((ts, tc), lambda i, j: (i, j + off)),
                  pl.BlockSpec((HALO, tc), functools.partial(_halo_prev(ts), off=off)),
                  pl.BlockSpec((8, tc), lambda i, j: (0, j)), pl.BlockSpec((1, tc), lambda i, j: (0, j))],
        out_specs=pl.BlockSpec((ts, tc), lambda i, j: (i, j)), out_shape=_sds((T, C), F32),
        compiler_params=_cparams(("parallel", "parallel")),
    )(proj, proj, _pad_rows8(w), b.reshape(1, C))


def ssd_conv_bwd_pre(proj, w, b, dxc, S, *, name, ts=1024, tc=512):
    T = proj.shape[0]
    C, K = SSD_CONV_CH, SSD_CONV
    ts = _pick(S, (ts, 256, 128))
    off = PXBC0 // tc
    ns = S // ts

    def body(x_ref, h_ref, w_ref, b_ref, d_ref, o_ref, acc_ref):
        i = pl.program_id(1)
        first = (i % ns) == 0
        cat = _cat_prev(x_ref[...], h_ref[...], first)
        pre = _conv_pre(cat, w_ref[...], b_ref[...], K)
        sg = _sigmoid(pre)
        dpre = d_ref[...] * (sg * (1.0 + pre * (1.0 - sg)))
        o_ref[...] = dpre
        rows = [jnp.sum(dpre * _delayed(cat, K - 1 - k), axis=0, keepdims=True) for k in range(K)]
        rows.append(jnp.sum(dpre, axis=0, keepdims=True))
        rows.append(jnp.zeros((8 - len(rows), dpre.shape[1]), F32))
        part = jnp.concatenate(rows, axis=0)

        @pl.when(i == 0)
        def _():
            acc_ref[...] = part

        @pl.when(i > 0)
        def _():
            acc_ref[...] += part

    hp = _halo_prev(ts)
    return pl.pallas_call(
        body, name=name, grid=(C // tc, T // ts),
        in_specs=[pl.BlockSpec((ts, tc), lambda j, i: (i, j + off)),
                  pl.BlockSpec((HALO, tc), lambda j, i: hp(i, j, off)),
                  pl.BlockSpec((8, tc), lambda j, i: (0, j)), pl.BlockSpec((1, tc), lambda j, i: (0, j)),
                  pl.BlockSpec((ts, tc), lambda j, i: (i, j))],
        out_specs=[pl.BlockSpec((ts, tc), lambda j, i: (i, j)), pl.BlockSpec((8, tc), lambda j, i: (0, j))],
        out_shape=[_sds((T, C), F32), _sds((8, C), F32)],
        compiler_params=_cparams(("parallel", "arbitrary")),
    )(proj, proj, _pad_rows8(w), b.reshape(1, C), dxc)


def conv_bwd_x(dpre, w, S, K, *, name, ts=512, tc=None):
    T, C = dpre.shape
    out_dtype = MXU_DTYPE
    ts = _pick(S, (ts, 256, 128))
    tc = tc or _pick(C, (1408, 512, 256, 128))
    ns = S // ts
    nblk = T // HALO

    def body(d_ref, h_ref, w_ref, o_ref):
        last = (pl.program_id(0) % ns) == ns - 1
        cat = _cat_next(d_ref[...], h_ref[...], last)
        wv = w_ref[...]
        acc = _advanced(cat, K - 1) * wv[0:1, :]
        for k in range(1, K):
            acc = acc + _advanced(cat, K - 1 - k) * wv[k:k + 1, :]
        o_ref[...] = acc.astype(out_dtype)

    return pl.pallas_call(
        body, name=name, grid=(T // ts, C // tc),
        in_specs=[pl.BlockSpec((ts, tc), lambda i, j: (i, j)),
                  pl.BlockSpec((HALO, tc), lambda i, j: (jnp.minimum((i + 1) * (ts // HALO), nblk - 1), j)),
                  pl.BlockSpec((8, tc), lambda i, j: (0, j))],
        out_specs=pl.BlockSpec((ts, tc), lambda i, j: (i, j)), out_shape=_sds((T, C), out_dtype),
        compiler_params=_cparams(("parallel", "parallel")),
    )(dpre, dpre, _pad_rows8(w))


def ffn_conv_gate_fwd(up, w, b, S, *, name, ts=512, tc=1408):
    T, C2 = up.shape
    C, K = C2 // 2, FFN_CONV
    ts = _pick(S, (ts, 256, 128))
    nj = C // tc
    ns = S // ts
    w8 = _pad_rows8(w)
    b2 = b.reshape(1, C2)

    def body(g_ref, gh_ref, v_ref, vh_ref, wg_ref, wv_ref, bg_ref, bv_ref, o_ref):
        first = (pl.program_id(0) % ns) == 0
        g = _conv_pre(_cat_prev(g_ref[...], gh_ref[...], first), wg_ref[...], bg_ref[...], K)
        v = _conv_pre(_cat_prev(v_ref[...], vh_ref[...], first), wv_ref[...], bv_ref[...], K)
        o_ref[...] = (g * _sigmoid(g) * v).astype(o_ref.dtype)

    hp = _halo_prev(ts)
    return pl.pallas_call(
        body, name=name, grid=(T // ts, nj),
        in_specs=[pl.BlockSpec((ts, tc), lambda i, j: (i, j)), pl.BlockSpec((HALO, tc), lambda i, j: hp(i, j)),
                  pl.BlockSpec((ts, tc), lambda i, j: (i, j + nj)), pl.BlockSpec((HALO, tc), lambda i, j: hp(i, j, nj)),
                  pl.BlockSpec((8, tc), lambda i, j: (0, j)), pl.BlockSpec((8, tc), lambda i, j: (0, j + nj)),
                  pl.BlockSpec((1, tc), lambda i, j: (0, j)), pl.BlockSpec((1, tc), lambda i, j: (0, j + nj))],
        out_specs=pl.BlockSpec((ts, tc), lambda i, j: (i, j)), out_shape=_sds((T, C), MXU_DTYPE),
        compiler_params=_cparams(("parallel", "parallel")),
    )(up, up, up, up, w8, w8, b2, b2)


def ffn_conv_gate_bwd(up, w, b, dact, S, *, name, ts=256, tc=1408):
    T, C2 = up.shape
    C, K = C2 // 2, FFN_CONV
    ts = _pick(S, (ts, 256, 128))
    nj = C // tc
    ns = S // ts
    w8 = _pad_rows8(w)
    b2 = b.reshape(1, C2)

    def stats(dpre, cat):
        rows = [jnp.sum(dpre * _delayed(cat, K - 1 - k), axis=0, keepdims=True) for k in range(K)]
        rows.append(jnp.sum(dpre, axis=0, keepdims=True))
        rows.append(jnp.zeros((8 - len(rows), dpre.shape[1]), F32))
        return jnp.concatenate(rows, axis=0)

    def body(g_ref, gh_ref, v_ref, vh_ref, wg_ref, wv_ref, bg_ref, bv_ref, d_ref, dg_ref, dv_ref, ag_ref, av_ref):
        i = pl.program_id(1)
        first = (i % ns) == 0
        gcat = _cat_prev(g_ref[...], gh_ref[...], first)
        vcat = _cat_prev(v_ref[...], vh_ref[...], first)
        g = _conv_pre(gcat, wg_ref[...], bg_ref[...], K)
        v = _conv_pre(vcat, wv_ref[...], bv_ref[...], K)
        d = d_ref[...]
        sg = _sigmoid(g)
        dg = d * v * (sg * (1.0 + g * (1.0 - sg)))
        dv = d * (g * sg)
        dg_ref[...] = dg
        dv_ref[...] = dv
        sgp, svp = stats(dg, gcat), stats(dv, vcat)

        @pl.when(i == 0)
        def _():
            ag_ref[...] = sgp
            av_ref[...] = svp

        @pl.when(i > 0)
        def _():
            ag_ref[...] += sgp
            av_ref[...] += svp

    hp = _halo_prev(ts)
    dg, dv, ag, av = pl.pallas_call(
        body, name=name, grid=(nj, T // ts),
        in_specs=[pl.BlockSpec((ts, tc), lambda j, i: (i, j)), pl.BlockSpec((HALO, tc), lambda j, i: hp(i, j)),
                  pl.BlockSpec((ts, tc), lambda j, i: (i, j + nj)), pl.BlockSpec((HALO, tc), lambda j, i: hp(i, j, nj)),
                  pl.BlockSpec((8, tc), lambda j, i: (0, j)), pl.BlockSpec((8, tc), lambda j, i: (0, j + nj)),
                  pl.BlockSpec((1, tc), lambda j, i: (0, j)), pl.BlockSpec((1, tc), lambda j, i: (0, j + nj)),
                  pl.BlockSpec((ts, tc), lambda j, i: (i, j))],
        out_specs=[pl.BlockSpec((ts, tc), lambda j, i: (i, j)), pl.BlockSpec((ts, tc), lambda j, i: (i, j)),
                   pl.BlockSpec((8, tc), lambda j, i: (0, j)), pl.BlockSpec((8, tc), lambda j, i: (0, j))],
        out_shape=[_sds((T, C), F32), _sds((T, C), F32), _sds((8, C), F32), _sds((8, C), F32)],
        compiler_params=_cparams(("parallel", "arbitrary")),
    )(up, up, up, up, w8, w8, b2, b2, dact)
    return dg, dv, jnp.concatenate([ag, av], axis=1)


def _pool_counts(pos, w):
    return jnp.minimum(pos + 1.0, float(w))


def pool_fwd(proj, pool_w, pool_scale, S, *, name, ts=512):
    T = proj.shape[0]
    C, G, GD, H = POOL_WIDTH, POOL_GROUPS, POOL_GROUP_DIM, POOL_HALO
    ts = _pick(S, (ts, 256, 128))
    ns = S // ts
    off = PU0 // C

    def body(u_ref, h_ref, w_ref, s_ref, y_ref, p_ref):
        i = pl.program_id(0)
        first = (i % ns) == 0
        cat = jnp.concatenate([jnp.where(first, 0.0, h_ref[...]), u_ref[...]], axis=0)
        pos = ((i % ns) * ts + lax.broadcasted_iota(jnp.int32, (ts, 1), 0)).astype(F32)
        sums = cat
        win = 1
        for g, wlen in enumerate(POOL_WINDOWS):
            while win < wlen:
                sums = sums + pltpu.roll(sums, win, axis=0)
                win *= 2
            sl = slice(g * GD, (g + 1) * GD)
            pooled = sums[H:, sl] / _pool_counts(pos, wlen) - cat[H:, sl]
            p_ref[:, sl] = pooled.astype(p_ref.dtype)
            y_ref[:, sl] = (_dot(pooled, w_ref[g]) * s_ref[:, sl]).astype(y_ref.dtype)

    return pl.pallas_call(
        body, name=name, grid=(T // ts,),
        in_specs=[pl.BlockSpec((ts, C), lambda i: (i, off)),
                  pl.BlockSpec((H, C), lambda i: (jnp.maximum(i * (ts // H) - 1, 0), off)),
                  pl.BlockSpec((G, GD, GD), lambda i: (0, 0, 0)), pl.BlockSpec((1, C), lambda i: (0, 0))],
        out_specs=[pl.BlockSpec((ts, C), lambda i: (i, 0)), pl.BlockSpec((ts, C), lambda i: (i, 0))],
        out_shape=[_sds((T, C), MXU_DTYPE), _sds((T, C), MXU_DTYPE)],
        compiler_params=_cparams(("parallel",)),
    )(proj, proj, _mx(pool_w), pool_scale.reshape(1, C))


def pool_bwd(dmix, pooled, pool_w, pool_scale, S, *, name, ts=512):
    T = dmix.shape[0]
    C, G, GD, H = POOL_WIDTH, POOL_GROUPS, POOL_GROUP_DIM, POOL_HALO
    ts = _pick(S, (ts, 256, 128))
    ns = S // ts
    off = SSD_WIDTH // C
    nblk = T // H

    def body(d_ref, dh_ref, p_ref, w_ref, s_ref, du_ref, dw_ref, ds_ref):
        i = pl.program_id(0)
        last = (i % ns) == ns - 1
        dcat = jnp.concatenate([d_ref[...], jnp.where(last, 0.0, dh_ref[...])], axis=0)
        n = ts + H
        pos = ((i % ns) * ts + lax.broadcasted_iota(jnp.int32, (n, 1), 0)).astype(F32)
        dws, dss = [], []
        for g, wlen in enumerate(POOL_WINDOWS):
            sl = slice(g * GD, (g + 1) * GD)
            wg = w_ref[g]
            pg = p_ref[:, sl]
            dys = dcat[:, sl] * s_ref[:, sl]
            dss.append(jnp.sum(dcat[:ts, sl] * _dot(pg, wg), axis=0, keepdims=True))
            dws.append(_dot_tn(pg, dys[:ts]))
            dp = _dot_nt(dys, wg)
            q = dp / _pool_counts(pos, wlen)
            win = 1
            while win < wlen:
                q = q + pltpu.roll(q, n - win, axis=0)
                win *= 2
            du_ref[:, sl] = (q[:ts] - dp[:ts]).astype(du_ref.dtype)
        dsp = jnp.concatenate(dss, axis=1)

        @pl.when(i == 0)
        def _():
            for g in range(G):
                dw_ref[g] = dws[g]
            ds_ref[...] = dsp

        @pl.when(i > 0)
        def _():
            for g in range(G):
                dw_ref[g] += dws[g]
            ds_ref[...] += dsp

    return pl.pallas_call(
        body, name=name, grid=(T // ts,),
        in_specs=[pl.BlockSpec((ts, C), lambda i: (i, off)),
                  pl.BlockSpec((H, C), lambda i: (jnp.minimum((i + 1) * (ts // H), nblk - 1), off)),
                  pl.BlockSpec((ts, C), lambda i: (i, 0)),
                  pl.BlockSpec((G, GD, GD), lambda i: (0, 0, 0)), pl.BlockSpec((1, C), lambda i: (0, 0))],
        out_specs=[pl.BlockSpec((ts, C), lambda i: (i, 0)), pl.BlockSpec((G, GD, GD), lambda i: (0, 0, 0)),
                   pl.BlockSpec((1, C), lambda i: (0, 0))],
        out_shape=[_sds((T, C), MXU_DTYPE), _sds((G, GD, GD), F32), _sds((1, C), F32)],
        compiler_params=_cparams(("arbitrary",)),
    )(dmix, dmix, pooled, _mx(pool_w), pool_scale.reshape(1, C))


ROPE0 = MLA_NOPE
ROPE_HALF = MLA_ROPE // 2


def _rope_tables(pos, invf):
    lane = lax.broadcasted_iota(jnp.int32, (1, HEAD_W), 1)
    ang = pos * invf
    cs, sn = jnp.cos(ang), jnp.sin(ang)
    in_a = (lane >= ROPE0) & (lane < ROPE0 + ROPE_HALF)
    in_b = (lane >= ROPE0 + ROPE_HALF) & (lane < ROPE0 + MLA_ROPE)
    return jnp.where(in_a | in_b, cs, 1.0), jnp.where(in_a, -sn, 0.0), jnp.where(in_b, sn, 0.0), in_a | in_b


def _rope(v, cosf, sin_a, sin_b):
    return (v * cosf + pltpu.roll(v, HEAD_W - ROPE_HALF, axis=1) * sin_a + pltpu.roll(v, ROPE_HALF, axis=1) * sin_b)


def _unrope(d, cosf, sin_a, sin_b):
    return (d * cosf + pltpu.roll(d * sin_a, ROPE_HALF, axis=1) + pltpu.roll(d * sin_b, HEAD_W - ROPE_HALF, axis=1))


def _rms_tile(xv, gamma):
    return (xv * lax.rsqrt(jnp.mean(xv * xv, axis=-1, keepdims=True) + EPS)) * gamma


def mla_prep_fwd(proj, pos, invf, q_norm, w_uq_p, kv_norm, w_ukv_p, *, name, tm=256):
    T = proj.shape[0]
    tm = _pick(T, (tm, 128))
    QR, KR, P = MLA_Q_RANK, MLA_KV_RANK, MLA_PAD

    def body(cq_ref, ckv_ref, kpe_ref, pos_ref, invf_ref, qn_ref, wq_ref, kn_ref, wkv_ref,
             q_ref, k_ref, v_ref, cqn_ref, ckvn_ref):
        cosf, sin_a, sin_b, _ = _rope_tables(pos_ref[...], invf_ref[...])
        cqn = _rms_tile(cq_ref[...], qn_ref[...]).astype(MXU_DTYPE)
        ckvn = _rms_tile(ckv_ref[...], kn_ref[...]).astype(MXU_DTYPE)
        cqn_ref[...] = cqn
        ckvn_ref[...] = ckvn
        qp = _dot(cqn, wq_ref[...])
        kvp = _dot(ckvn, wkv_ref[...])
        kpe = _rope(kpe_ref[...], cosf, sin_a, sin_b)
        for h in range(MLA_HEADS):
            sl = slice(h * HEAD_W, (h + 1) * HEAD_W)
            q_ref[:, sl] = (_rope(qp[:, sl], cosf, sin_a, sin_b) * ATTN_SCALE).astype(q_ref.dtype)
            k_ref[:, sl] = (kvp[:, sl] + kpe).astype(k_ref.dtype)
            v_ref[:, sl] = kvp[:, P + h * HEAD_W:P + (h + 1) * HEAD_W].astype(v_ref.dtype)

    row = lambda w: pl.BlockSpec((tm, w), lambda i: (i, 0))
    full = lambda a, b: pl.BlockSpec((a, b), lambda i: (0, 0))
    return pl.pallas_call(
        body, name=name, grid=(T // tm,),
        in_specs=[pl.BlockSpec((tm, QR), lambda i: (i, PCQ0 // QR)), pl.BlockSpec((tm, KR), lambda i: (i, PCKV0 // KR)),
                  pl.BlockSpec((tm, LANE), lambda i: (i, PKPE0 // LANE)), row(1), full(1, LANE),
                  full(1, QR), full(QR, P), full(1, KR), full(KR, 2 * P)],
        out_specs=[row(P), row(P), row(P), row(QR), row(KR)],
        out_shape=[_sds((T, P), MXU_DTYPE)] * 3 + [_sds((T, QR), MXU_DTYPE), _sds((T, KR), MXU_DTYPE)],
        compiler_params=_cparams(("parallel",)),
    )(proj, proj, proj, pos, invf, q_norm.reshape(1, QR), w_uq_p, kv_norm.reshape(1, KR), w_ukv_p)


def mla_prep_bwd(proj, pos, invf, q_norm, w_uq_p, kv_norm, w_ukv_p, dq, dk, dv, *, name, tm=256):
    T = proj.shape[0]
    tm = _pick(T, (tm, 128))
    QR, KR, P = MLA_Q_RANK, MLA_KV_RANK, MLA_PAD

    def body(cq_ref, ckv_ref, pos_ref, invf_ref, qn_ref, wq_ref, kn_ref, wkv_ref, dq_ref, dk_ref, dv_ref,
             dqp_ref, dkvp_ref, dcq_ref, dckv_ref, dkpe_ref, dqn_ref, dkn_ref):
        cosf, sin_a, sin_b, rot = _rope_tables(pos_ref[...], invf_ref[...])
        dkpe = jnp.zeros((tm, HEAD_W), F32)
        for h in range(MLA_HEADS):
            sl = slice(h * HEAD_W, (h + 1) * HEAD_W)
            dqp_ref[:, sl] = _unrope(dq_ref[:, sl] * ATTN_SCALE, cosf, sin_a, sin_b).astype(dqp_ref.dtype)
            dkh = dk_ref[:, sl]
            dkpe = dkpe + dkh
            dkvp_ref[:, sl] = dkh.astype(dkvp_ref.dtype)
            dkvp_ref[:, P + h * HEAD_W:P + (h + 1) * HEAD_W] = dv_ref[:, sl].astype(dkvp_ref.dtype)
        dkpe_ref[...] = jnp.where(rot, _unrope(dkpe, cosf, sin_a, sin_b), 0.0).astype(dkpe_ref.dtype)
        dcq, dqn = _rms_bwd_tile(cq_ref[...], qn_ref[...], _dot_nt(dqp_ref[...], wq_ref[...]))
        dckv, dkn = _rms_bwd_tile(ckv_ref[...], kn_ref[...], _dot_nt(dkvp_ref[...], wkv_ref[...]))
        dcq_ref[...] = dcq.astype(dcq_ref.dtype)
        dckv_ref[...] = dckv.astype(dckv_ref.dtype)

        @pl.when(pl.program_id(0) == 0)
        def _():
            dqn_ref[...] = dqn
            dkn_ref[...] = dkn

        @pl.when(pl.program_id(0) > 0)
        def _():
            dqn_ref[...] += dqn
            dkn_ref[...] += dkn

    row = lambda w: pl.BlockSpec((tm, w), lambda i: (i, 0))
    full = lambda a, b: pl.BlockSpec((a, b), lambda i: (0, 0))
    return pl.pallas_call(
        body, name=name, grid=(T // tm,),
        in_specs=[pl.BlockSpec((tm, QR), lambda i: (i, PCQ0 // QR)), pl.BlockSpec((tm, KR), lambda i: (i, PCKV0 // KR)),
                  row(1), full(1, LANE), full(1, QR), full(QR, P), full(1, KR), full(KR, 2 * P), row(P), row(P), row(P)],
        out_specs=[row(P), row(2 * P), row(QR), row(KR), row(LANE), full(1, QR), full(1, KR)],
        out_shape=[_sds((T, P), MXU_DTYPE), _sds((T, 2 * P), MXU_DTYPE), _sds((T, QR), MXU_DTYPE),
                   _sds((T, KR), MXU_DTYPE), _sds((T, LANE), MXU_DTYPE), _sds((1, QR), F32), _sds((1, KR), F32)],
        compiler_params=_cparams(("arbitrary",)),
    )(proj, proj, pos, invf, q_norm.reshape(1, QR), w_uq_p, kv_norm.reshape(1, KR), w_ukv_p, dq, dk, dv)


ATTN_SCALE = 1.0 / math.sqrt(MLA_QK)


def _causal_mask(i, j, blk):
    row = lax.broadcasted_iota(jnp.int32, (blk, blk), 0)
    col = lax.broadcasted_iota(jnp.int32, (blk, blk), 1)
    return col <= row + (i - j) * blk


def flash_fwd(q, k, v, S, *, name, blk=512):
    T, P = q.shape
    blk = _pick(S, (blk, 256, 128))
    B, nq, H, W = T // S, S // blk, MLA_HEADS, HEAD_W

    def body(q_ref, k_ref, v_ref, o_ref, lse_ref):
        i = pl.program_id(2)
        qv = q_ref[...]

        def online(j, carry, masked):
            m_prev, l_prev, acc = carry
            rows = pl.ds(pl.multiple_of(j * blk, blk), blk)
            s = _dot_nt(qv, k_ref[rows, :])
            if masked:
                s = jnp.where(_causal_mask(0, 0, blk), s, -jnp.inf)
            m_new = jnp.maximum(m_prev, jnp.max(s, axis=1, keepdims=True))
            p = jnp.exp(s - m_new)
            alpha = jnp.exp(m_prev - m_new)
            return (m_new, alpha * l_prev + jnp.sum(p, axis=1, keepdims=True), alpha * acc + _dot(p, v_ref[rows, :]))

        init = (jnp.full((blk, 1), -jnp.inf, F32), jnp.zeros((blk, 1), F32), jnp.zeros((blk, W), F32))
        carry = lax.fori_loop(0, i, lambda j, c: online(j, c, False), init)
        m, l, acc = online(i, carry, True)
        o_ref[...] = acc / l
        lse_ref[...] = jnp.broadcast_to(m + jnp.log(l), (blk, W))

    qmap = lambda b, h, i: (b * nq + i, h)
    kmap = lambda b, h, i: (b, h)
    return pl.pallas_call(
        body, name=name, grid=(B, H, nq),
        in_specs=[pl.BlockSpec((blk, W), qmap), pl.BlockSpec((S, W), kmap), pl.BlockSpec((S, W), kmap)],
        out_specs=[pl.BlockSpec((blk, W), qmap), pl.BlockSpec((blk, W), qmap)],
        out_shape=[_sds((T, P), F32), _sds((T, P), F32)],
        compiler_params=_cparams(("parallel", "parallel", "arbitrary")),
    )(q, k, v)


def flash_bwd(q, k, v, o, lse, dmix, S, *, name, blk=512):
    T, P = q.shape
    blk = _pick(S, (blk, 256, 128))
    B, nq, H, W = T // S, S // blk, MLA_HEADS, HEAD_W
    off = (SSD_WIDTH + POOL_WIDTH) // W

    def body(q_ref, k_ref, v_ref, o_ref, lse_ref, do_ref, dq_ref, dk_ref, dv_ref, delta_s):
        j = pl.program_id(2)

        @pl.when(j == 0)
        def _():
            for i in range(nq):
                rows = slice(i * blk, (i + 1) * blk)
                delta_s[rows, :] = jnp.sum(do_ref[rows, :] * o_ref[rows, :], axis=1, keepdims=True)
                dq_ref[rows, :] = jnp.zeros((blk, W), F32)

        kv, vv = k_ref[...], v_ref[...]

        def step(i, carry, masked):
            dk, dv = carry
            rows = pl.ds(pl.multiple_of(i * blk, blk), blk)
            qv, do = q_ref[rows, :], do_ref[rows, :]
            p = jnp.exp(_dot_nt(qv, kv) - lse_ref[rows, 0:1])
            if masked:
                p = jnp.where(_causal_mask(0, 0, blk), p, 0.0)
            ds = p * (_dot_nt(do, vv) - delta_s[rows, :])
            dq_ref[rows, :] += _dot(ds, kv)
            return dk + _dot_tn(ds, qv), dv + _dot_tn(p, do)

        zero = jnp.zeros((blk, W), F32)
        carry = step(j, (zero, zero), True)
        dk, dv = lax.fori_loop(j + 1, nq, lambda i, c: step(i, c, False), carry)
        dk_ref[...] = dk
        dv_ref[...] = dv

    full = lambda b, h, j: (b, h)
    kmap = lambda b, h, j: (b * nq + j, h)
    return pl.pallas_call(
        body, name=name, grid=(B, H, nq),
        in_specs=[pl.BlockSpec((S, W), full), pl.BlockSpec((blk, W), kmap), pl.BlockSpec((blk, W), kmap),
                  pl.BlockSpec((S, W), full), pl.BlockSpec((S, W), full),
                  pl.BlockSpec((S, W), lambda b, h, j: (b, off + h))],
        out_specs=[pl.BlockSpec((S, W), full), pl.BlockSpec((blk, W), kmap), pl.BlockSpec((blk, W), kmap)],
        out_shape=[_sds((T, P), F32)] * 3,
        scratch_shapes=[pltpu.VMEM((S, 1), F32)],
        compiler_params=_cparams(("parallel", "parallel", "arbitrary")),
    )(q, k, v, o, lse, dmix)


SSD_PAIRS = SSD_HEADS // 2
PAIRS_PER_GROUP = SSD_PAIRS // SSD_GROUPS
GN = SSD_GROUPS * SSD_STATE


def _log1p_small(e):
    return jnp.where(e < 1e-3, e * (1.0 - e * (0.5 - e / 3.0)), jnp.log(1.0 + e))


def _softplus(v):
    return jnp.maximum(v, 0.0) + _log1p_small(jnp.exp(-jnp.abs(v)))


def _ssd_decay(dt_raw, dtb, alog):
    L = dt_raw.shape[0]
    pre = dt_raw + dtb
    dt = _softplus(pre)
    a = -jnp.exp(alog)
    row = lax.broadcasted_iota(jnp.int32, (L, L), 0)
    col = lax.broadcasted_iota(jnp.int32, (L, L), 1)
    tri = row >= col
    cum = _dot_hi(tri.astype(F32), dt * a)
    return pre, dt, a, tri, cum, cum.T


def _col(m, h):
    return m[:, h:h + 1]


def _pair_sel(m, k, lo):
    return jnp.where(lo, _col(m, 2 * k), _col(m, 2 * k + 1))


def _ssd_specs(S):
    L = SSD_CHUNK
    nc = S // L
    return L, nc


def ssd_fwd(proj, xc, dtb, alog, dchan, normw, S, *, name):
    T = proj.shape[0]
    L, nc = _ssd_specs(S)
    B, W, N = T // S, SSD_WIDTH, SSD_STATE

    def body(xs_ref, bs_ref, cs_ref, dt_ref, z_ref, dtb_ref, alog_ref, dch_ref, nw_ref, y_ref, ys_ref, hin_ref, st):
        @pl.when(pl.program_id(1) == 0)
        def _():
            st[...] = jnp.zeros(st.shape, F32)

        hin_ref[...] = st[...]
        _, dt, a, tri, cum, cum_t = _ssd_decay(dt_ref[...], dtb_ref[...], alog_ref[...])
        e_cum = jnp.exp(cum)
        last = cum[L - 1:L, :]
        w_end = jnp.exp(last - cum)
        e_last = jnp.exp(last)
        lo = lax.broadcasted_iota(jnp.int32, (1, LANE), 1) < SSD_HEAD_DIM
        slo = lax.broadcasted_iota(jnp.int32, (LANE, 1), 0) < SSD_HEAD_DIM
        for g in range(SSD_GROUPS):
            bm = bs_ref[:, g * N:(g + 1) * N]
            cm = cs_ref[:, g * N:(g + 1) * N]
            gmat = _dot_nt(cm, bm)
            for kk in range(PAIRS_PER_GROUP):
                k = g * PAIRS_PER_GROUP + kk
                sl = slice(k * LANE, (k + 1) * LANE)
                xv = xs_ref[:, sl]
                xdt = xv * _pair_sel(dt, k, lo)
                yd = []
                for h in (2 * k, 2 * k + 1):
                    gam = jnp.exp(jnp.where(tri, _col(cum, h) - cum_t[h:h + 1, :], -jnp.inf))
                    yd.append(_dot(gmat * gam, xdt))
                hp = st[sl, :]
                y_off = _dot_nt(cm, hp) * _pair_sel(e_cum, k, lo)
                y_ref[:, sl] = jnp.where(lo, yd[0], yd[1]) + y_off + xv * dch_ref[:, sl]
                zmat = xdt * _pair_sel(w_end, k, lo)
                e_rows = jnp.where(slo, _col(e_last, 2 * k), _col(e_last, 2 * k + 1))
                st[sl, :] = hp * e_rows + _dot_tn(zmat, bm)
        y = y_ref[...]
        z = z_ref[...]
        yz = y * (z * _sigmoid(z))
        ys_ref[...] = _rms_tile(yz, nw_ref[...]).astype(ys_ref.dtype)

    r = lambda b, c: b * nc + c
    vec = lambda w: pl.BlockSpec((1, w), lambda b, c: (0, 0))
    return pl.pallas_call(
        body, name=name, grid=(B, nc),
        in_specs=[pl.BlockSpec((L, W), lambda b, c: (r(b, c), 0)),
                  pl.BlockSpec((L, GN), lambda b, c: (r(b, c), W // GN)),
                  pl.BlockSpec((L, GN), lambda b, c: (r(b, c), W // GN + 1)),
                  pl.BlockSpec((L, LANE), lambda b, c: (r(b, c), PDT0 // LANE)),
                  pl.BlockSpec((L, W), lambda b, c: (r(b, c), PZ0 // W)),
                  vec(LANE), vec(LANE), vec(W), vec(W)],
        out_specs=[pl.BlockSpec((L, W), lambda b, c: (r(b, c), 0)), pl.BlockSpec((L, W), lambda b, c: (r(b, c), 0)),
                   pl.BlockSpec((W, N), lambda b, c: (r(b, c), 0))],
        out_shape=[_sds((T, W), F32), _sds((T, W), MXU_DTYPE), _sds((T // L * W, N), F32)],
        scratch_shapes=[pltpu.VMEM((W, N), F32)],
        compiler_params=_cparams(("parallel", "arbitrary")),
    )(xc, xc, xc, proj, proj, dtb, alog, dchan, normw)


def ssd_bwd(proj, xc, ypre, hin, dmix, dtb, alog, dchan, normw, S, *, name):
    T = proj.shape[0]
    L, nc = _ssd_specs(S)
    B, W, N = T // S, SSD_WIDTH, SSD_STATE

    def body(xs_ref, bs_ref, cs_ref, dt_ref, z_ref, y_ref, hin_ref, dys_ref, dtb_ref, alog_ref, dch_ref, nw_ref,
             dxc_ref, ddt_ref, dz_ref, sm_ref, dnw_ref, dst):
        step = pl.program_id(0) * nc + pl.program_id(1)

        @pl.when(pl.program_id(1) == 0)
        def _():
            dst[...] = jnp.zeros(dst.shape, F32)

        pre, dt, a, tri, cum, cum_t = _ssd_decay(dt_ref[...], dtb_ref[...], alog_ref[...])
        e_cum = jnp.exp(cum)
        last = cum[L - 1:L, :]
        w_end = jnp.exp(last - cum)
        e_last = jnp.exp(last)
        lane = lax.broadcasted_iota(jnp.int32, (1, LANE), 1)
        sub = lax.broadcasted_iota(jnp.int32, (LANE, 1), 0)
        lo, slo = lane < SSD_HEAD_DIM, sub < SSD_HEAD_DIM
        is_last_row = sub == L - 1

        y, z, nw = y_ref[...], z_ref[...], nw_ref[...]
        sg = _sigmoid(z)
        gate = z * sg
        dyz, dnw = _rms_bwd_tile(y * gate, nw, dys_ref[...])
        dy_all = dyz * gate
        dz_ref[...] = (dyz * y * (sg * (1.0 + z * (1.0 - sg)))).astype(dz_ref.dtype)

        d_cum = jnp.zeros((L, LANE), F32)
        d_cum_t = jnp.zeros((LANE, L), F32)
        d_dt = jnp.zeros((L, LANE), F32)
        d_dskip = jnp.zeros((1, LANE), F32)
        for g in range(SSD_GROUPS):
            bm = bs_ref[:, g * N:(g + 1) * N]
            cm = cs_ref[:, g * N:(g + 1) * N]
            gmat = _dot_nt(cm, bm)
            d_g = jnp.zeros((L, L), F32)
            d_bm = jnp.zeros((L, N), F32)
            d_cm = jnp.zeros((L, N), F32)
            for kk in range(PAIRS_PER_GROUP):
                k = g * PAIRS_PER_GROUP + kk
                sl = slice(k * LANE, (k + 1) * LANE)
                xv = xs_ref[:, sl]
                dyv = dy_all[:, sl]
                dt_sel = _pair_sel(dt, k, lo)
                xdt = xv * dt_sel
                hp = hin_ref[sl, :]
                dh_out = dst[sl, :]
                e_sel = _pair_sel(e_cum, k, lo)
                w_sel = _pair_sel(w_end, k, lo)
                y_off = _dot_nt(cm, hp) * e_sel
                zmat = xdt * w_sel
                d_z = _dot_nt(bm, dh_out)
                d_bm = d_bm + _dot(zmat, dh_out)
                d_xdt = d_z * w_sel
                dw_full = d_z * zmat
                e_rows = jnp.where(slo, _col(e_last, 2 * k), _col(e_last, 2 * k + 1))
                hh = jnp.sum(dh_out * hp, axis=1, keepdims=True) * e_rows
                d_r = dyv * e_sel
                d_cm = d_cm + _dot(d_r, hp)
                dst[sl, :] = dh_out * e_rows + _dot_tn(d_r, cm)
                dyoff_full = dyv * y_off
                for j, h in enumerate((2 * k, 2 * k + 1)):
                    mine = lo if j == 0 else jnp.logical_not(lo)
                    smine = slo if j == 0 else jnp.logical_not(slo)
                    hot = lane == h
                    dyh = jnp.where(mine, dyv, 0.0)
                    gam = jnp.exp(jnp.where(tri, _col(cum, h) - cum_t[h:h + 1, :], -jnp.inf))
                    mx = gmat * gam
                    d_xdt = d_xdt + _dot_tn(mx, dyh)
                    d_mx = jnp.where(tri, _dot_nt(dyh, xdt), 0.0)
                    d_g = d_g + d_mx * gam
                    d_seg = d_mx * mx
                    row_l = (jnp.sum(d_seg, axis=1, keepdims=True)
                             + jnp.sum(jnp.where(mine, dyoff_full - dw_full, 0.0), axis=1, keepdims=True))
                    at_end = (jnp.sum(jnp.where(mine, dw_full, 0.0), keepdims=True)
                              + jnp.sum(jnp.where(smine, hh, 0.0), keepdims=True))
                    d_cum = d_cum + jnp.where(hot, row_l + jnp.where(is_last_row, at_end, 0.0), 0.0)
                    d_cum_t = d_cum_t - jnp.where(sub == h, jnp.sum(d_seg, axis=0, keepdims=True), 0.0)
                    d_dskip = d_dskip + jnp.where(hot, jnp.sum(jnp.where(mine, dyv * xv, 0.0), keepdims=True), 0.0)
                for j, h in enumerate((2 * k, 2 * k + 1)):
                    mine = lo if j == 0 else jnp.logical_not(lo)
                    d_dt = d_dt + jnp.where(lane == h, jnp.sum(jnp.where(mine, d_xdt * xv, 0.0), axis=1, keepdims=True), 0.0)
                dxc_ref[:, sl] = d_xdt * dt_sel + dyv * dch_ref[:, sl]
            dxc_ref[:, W + g * N:W + (g + 1) * N] = d_bm + _dot_tn(d_g, cm)
            dxc_ref[:, W + GN + g * N:W + GN + (g + 1) * N] = d_cm + _dot(d_g, bm)

        d_cum = d_cum + d_cum_t.T
        d_da = _dot_hi(jnp.logical_not(tri).astype(F32) + (lax.broadcasted_iota(jnp.int32, (L, L), 0)
                                                              == lax.broadcasted_iota(jnp.int32, (L, L), 1)).astype(F32), d_cum)
        d_dt = d_dt + d_da * a
        heads = lane < SSD_HEADS
        d_pre = jnp.where(heads, d_dt * _sigmoid(pre), 0.0)
        ddt_ref[...] = d_pre.astype(ddt_ref.dtype)
        d_alog = jnp.sum(d_da * dt, axis=0, keepdims=True) * a
        part = jnp.concatenate([jnp.where(heads, d_alog, 0.0), jnp.sum(d_pre, axis=0, keepdims=True), d_dskip,
                                jnp.zeros((5, LANE), F32)], axis=0)

        @pl.when(step == 0)
        def _():
            sm_ref[...] = part
            dnw_ref[...] = dnw

        @pl.when(step > 0)
        def _():
            sm_ref[...] += part
            dnw_ref[...] += dnw

    r = lambda b, c: b * nc + (nc - 1 - c)
    vec = lambda w: pl.BlockSpec((1, w), lambda b, c: (0, 0))
    blk = lambda w, j: pl.BlockSpec((L, w), lambda b, c: (r(b, c), j))
    return pl.pallas_call(
        body, name=name, grid=(B, nc),
        in_specs=[blk(W, 0), blk(GN, W // GN), blk(GN, W // GN + 1), blk(LANE, PDT0 // LANE), blk(W, PZ0 // W),
                  blk(W, 0), pl.BlockSpec((W, N), lambda b, c: (r(b, c), 0)), blk(W, 0),
                  vec(LANE), vec(LANE), vec(W), vec(W)],
        out_specs=[blk(SSD_CONV_CH, 0), blk(LANE, 0), blk(W, 0), pl.BlockSpec((8, LANE), lambda b, c: (0, 0)), vec(W)],
        out_shape=[_sds((T, SSD_CONV_CH), F32), _sds((T, LANE), MXU_DTYPE), _sds((T, W), MXU_DTYPE),
                   _sds((8, LANE), F32), _sds((1, W), F32)],
        scratch_shapes=[pltpu.VMEM((W, N), F32)],
        compiler_params=_cparams(("arbitrary", "arbitrary")),
    )(xc, xc, xc, proj, proj, ypre, hin, dmix, dtb, alog, dchan, normw)


def _adamw_math(w, g, m, v):
    m = ADAM_B1 * m + (1.0 - ADAM_B1) * g
    v = ADAM_B2 * v + (1.0 - ADAM_B2) * (g * g)
    m_hat = m / (1.0 - ADAM_B1 ** ADAM_STEP)
    v_hat = v / (1.0 - ADAM_B2 ** ADAM_STEP)
    delta = -ADAM_LR * (m_hat / (jnp.sqrt(v_hat) + ADAM_EPS) + ADAM_WD * w)
    return delta, m, v


def adamw(w, g_parts, m, v, *, name, tr=256):
    R, C = w.shape
    tr = _pick(R, (tr, 128, 64, 32, 16, 8))
    n = len(g_parts)

    def body(*refs):
        w_ref, m_ref, v_ref = refs[0], refs[1 + n], refs[2 + n]
        g_ref, d_ref, nm_ref, nv_ref = refs[3 + n:]
        g = refs[1][...]
        for p in refs[2:1 + n]:
            g = g + p[...]
        g_ref[...] = g
        d_ref[...], nm_ref[...], nv_ref[...] = _adamw_math(w_ref[...], g, m_ref[...], v_ref[...])

    spec = pl.BlockSpec((tr, C), lambda i: (i, 0))
    return pl.pallas_call(
        body, name=name, grid=(R // tr,), in_specs=[spec] * (3 + n), out_specs=[spec] * 4,
        out_shape=[_sds((R, C), F32)] * 4, compiler_params=_cparams(("parallel",)),
    )(w, *g_parts, m, v)


def _my_place():
    return lax.axis_index("x"), lax.axis_index("y"), lax.axis_index("c")


def chip_exchange(src, *, per_dest, name):
    R, C = src.shape[-2:]

    def body(src_ref, out_ref, send_sems, recv_sems, local_sem):
        x, y, c = _my_place()
        me = 2 * x + y
        peers = [(1 - x, y), (x, 1 - y), (1 - x, 1 - y)]
        own = pltpu.make_async_copy(src_ref.at[me] if per_dest else src_ref, out_ref.at[me], local_sem)
        own.start()
        copies = []
        for k, (px, py) in enumerate(peers):
            dest = 2 * px + py
            copies.append(pltpu.make_async_remote_copy(
                src_ref=src_ref.at[dest] if per_dest else src_ref, dst_ref=out_ref.at[me],
                send_sem=send_sems.at[k], recv_sem=recv_sems.at[k],
                device_id=(px, py, c), device_id_type=pl.DeviceIdType.MESH))
        for cp in copies:
            cp.start()
        for k, (px, py) in enumerate(peers):
            pltpu.make_async_remote_copy(
                src_ref=src_ref.at[0] if per_dest else src_ref, dst_ref=out_ref.at[2 * px + py],
                send_sem=send_sems.at[k], recv_sem=recv_sems.at[k],
                device_id=(px, py, c), device_id_type=pl.DeviceIdType.MESH).wait_recv()
        for cp in copies:
            cp.wait_send()
        own.wait()

    return pl.pallas_call(
        body, name=name, in_specs=[pl.BlockSpec(memory_space=pl.ANY)], out_specs=pl.BlockSpec(memory_space=pl.ANY),
        out_shape=_sds((N_CHIPS, R, C), src.dtype),
        scratch_shapes=[pltpu.SemaphoreType.DMA((3,)), pltpu.SemaphoreType.DMA((3,)), pltpu.SemaphoreType.DMA],
        compiler_params=pltpu.CompilerParams(has_side_effects=True),
    )(src)


D2D_CHUNKS = 8
ICI_CHUNKS = 4


def sibling_exchange(src, *, name, nch=D2D_CHUNKS):
    R, C = src.shape
    ch = R // nch
    assert ch * nch == R

    def body(src_ref, out_ref, send_sems, recv_sems, local_sem):
        x, y, c = _my_place()
        own = pltpu.make_async_copy(src_ref, out_ref.at[c], local_sem)
        own.start()

        def copy(q, block):
            rows = pl.ds(q * ch, ch)
            return pltpu.make_async_remote_copy(
                src_ref=src_ref.at[rows], dst_ref=out_ref.at[block, rows], send_sem=send_sems.at[q],
                recv_sem=recv_sems.at[q], device_id=(x, y, 1 - c), device_id_type=pl.DeviceIdType.MESH)

        sends = [copy(q, c) for q in range(nch)]
        for cp in sends:
            cp.start()
        for q in range(nch):
            copy(q, 1 - c).wait_recv()
        for cp in sends:
            cp.wait_send()
        own.wait()

    return pl.pallas_call(
        body, name=name, in_specs=[pl.BlockSpec(memory_space=pl.ANY)], out_specs=pl.BlockSpec(memory_space=pl.ANY),
        out_shape=_sds((2, R, C), src.dtype),
        scratch_shapes=[pltpu.SemaphoreType.DMA((nch,)), pltpu.SemaphoreType.DMA((nch,)), pltpu.SemaphoreType.DMA],
        compiler_params=pltpu.CompilerParams(has_side_effects=True),
    )(src)


def gather_shards(src, *, name, nch=ICI_CHUNKS):
    R, C = src.shape
    half = R // 2
    ch = half // nch
    assert 2 * nch * ch == R

    def body(src_ref, out_ref, ici_send, ici_recv, d2d_send, d2d_recv, local_sem):
        x, y, c = _my_place()
        me = 2 * x + y
        sibling = (x, y, 1 - c)
        peers = [(1 - x, y), (x, 1 - y), (1 - x, 1 - y)]
        own = pltpu.make_async_copy(src_ref, out_ref.at[me], local_sem)
        own.start()

        def rows(core, q):
            return pl.ds(core * half + q * ch, ch)

        def ici(k, q, block):
            px, py = peers[k]
            return pltpu.make_async_remote_copy(
                src_ref=src_ref.at[rows(c, q)], dst_ref=out_ref.at[block, rows(c, q)], send_sem=ici_send.at[k, q],
                recv_sem=ici_recv.at[k, q], device_id=(px, py, c), device_id_type=pl.DeviceIdType.MESH)

        def d2d(k, q, core):
            px, py = peers[k]
            block = 2 * px + py
            return pltpu.make_async_remote_copy(
                src_ref=out_ref.at[block, rows(core, q)], dst_ref=out_ref.at[block, rows(core, q)],
                send_sem=d2d_send.at[k, q], recv_sem=d2d_recv.at[k, q], device_id=sibling,
                device_id_type=pl.DeviceIdType.MESH)

        sends = [ici(k, q, me) for k in range(3) for q in range(nch)]
        for cp in sends:
            cp.start()
        passed = []
        for k, (px, py) in enumerate(peers):
            for q in range(nch):
                ici(k, q, 2 * px + py).wait_recv()
                passed.append(d2d(k, q, c))
                passed[-1].start()
        for k in range(3):
            for q in range(nch):
                d2d(k, q, 1 - c).wait_recv()
        for cp in sends + passed:
            cp.wait_send()
        own.wait()

    return pl.pallas_call(
        body, name=name, in_specs=[pl.BlockSpec(memory_space=pl.ANY)], out_specs=pl.BlockSpec(memory_space=pl.ANY),
        out_shape=_sds((N_CHIPS, R, C), src.dtype),
        scratch_shapes=[pltpu.SemaphoreType.DMA((3, nch))] * 4 + [pltpu.SemaphoreType.DMA],
        compiler_params=pltpu.CompilerParams(has_side_effects=True),
    )(src)


def all_sum_small(vec, *, name):
    R, C = vec.shape

    def body(v_ref, out_ref, buf, send_sems, recv_sems):
        x, y, c = _my_place()
        me = 4 * x + 2 * y + c
        buf[me] = v_ref[...]
        copies = []
        for k in range(1, N_DEV):
            px, py, pc = x ^ (k >> 2), y ^ ((k >> 1) & 1), c ^ (k & 1)
            copies.append(pltpu.make_async_remote_copy(
                src_ref=v_ref, dst_ref=buf.at[me], send_sem=send_sems.at[k - 1], recv_sem=recv_sems.at[k - 1],
                device_id=(px, py, pc), device_id_type=pl.DeviceIdType.MESH))
        for cp in copies:
            cp.start()
        for k in range(1, N_DEV):
            px, py, pc = x ^ (k >> 2), y ^ ((k >> 1) & 1), c ^ (k & 1)
            pltpu.make_async_remote_copy(
                src_ref=v_ref, dst_ref=buf.at[4 * px + 2 * py + pc], send_sem=send_sems.at[k - 1],
                recv_sem=recv_sems.at[k - 1], device_id=(px, py, pc), device_id_type=pl.DeviceIdType.MESH).wait_recv()
        for cp in copies:
            cp.wait_send()
        acc = buf[0]
        for d in range(1, N_DEV):
            acc = acc + buf[d]
        out_ref[...] = acc

    return pl.pallas_call(
        body, name=name, in_specs=[pl.BlockSpec(memory_space=pltpu.VMEM)], out_specs=pl.BlockSpec(memory_space=pltpu.VMEM),
        out_shape=_sds((R, C), F32),
        scratch_shapes=[pltpu.VMEM((N_DEV, R, C), F32), pltpu.SemaphoreType.DMA((N_DEV - 1,)),
                        pltpu.SemaphoreType.DMA((N_DEV - 1,))],
        compiler_params=pltpu.CompilerParams(has_side_effects=True, vmem_limit_bytes=VMEM_LIMIT),
    )(vec)


def sum4(parts, *, name, tr=512):
    _, R, C = parts.shape
    tr = _pick(R, (tr, 256, 128, 64, 32, 16))

    def body(p_ref, o_ref):
        acc = p_ref[0].astype(F32)
        for j in range(1, N_CHIPS):
            acc = acc + p_ref[j].astype(F32)
        o_ref[...] = acc

    return pl.pallas_call(
        body, name=name, grid=(R // tr,), in_specs=[pl.BlockSpec((N_CHIPS, tr, C), lambda i: (0, i, 0))],
        out_specs=pl.BlockSpec((tr, C), lambda i: (i, 0)), out_shape=_sds((R, C), F32),
        compiler_params=_cparams(("parallel",)),
    )(parts)


WEIGHTS = ['attn_norm', 'w_in', 'ssd_conv_w', 'ssd_conv_b', 'ssd_dt_bias', 'ssd_a_log', 'ssd_d', 'ssd_norm', 'pool_w',
           'pool_scale', 'mla_q_norm', 'mla_w_uq', 'mla_kv_norm', 'mla_w_ukv', 'w_out', 'ffn_norm', 'ffn_w_up',
           'ffn_conv_w', 'ffn_conv_b', 'ffn_w_down', 'final_norm']
BIG = {'w_in': 2, 'mla_w_uq': 2, 'mla_w_ukv': 2, 'w_out': 1, 'ffn_w_up': 2, 'ffn_w_down': 1}
CONV_SHARDED = ('ssd_conv_w', 'ffn_conv_w')
PACK_COLS = 512
PACK_ROW_MULTIPLE = 1024


def _zeros_cols(w, n):
    return jnp.zeros((w.shape[0], n), w.dtype)


def _w_in_to_padded(w):
    return jnp.concatenate([w[:, 0:2560], w[:, 2576:3088], w[:, 3088:3472], w[:, 2560:2576], _zeros_cols(w, 112),
                            w[:, 3472:3728], _zeros_cols(w, 64), w[:, 3728:3760], _zeros_cols(w, 32 + 128)], axis=1)


def _w_in_from_padded(g):
    return jnp.concatenate([g[:, 0:2560], g[:, PDT0:PDT0 + SSD_HEADS], g[:, PU0:PU0 + POOL_WIDTH],
                            g[:, PCQ0:PCQ0 + MLA_Q_RANK], g[:, PCKV0:PCKV0 + MLA_KV_RANK],
                            g[:, PKPE0 + ROPE0:PKPE0 + ROPE0 + MLA_ROPE]], axis=1)


def _w_uq_to_padded(w):
    r = w.reshape(MLA_Q_RANK, MLA_HEADS, MLA_QK)
    return jnp.pad(r, ((0, 0), (0, 0), (0, HEAD_W - MLA_QK))).reshape(MLA_Q_RANK, MLA_PAD)


def _w_uq_from_padded(g):
    return g.reshape(MLA_Q_RANK, MLA_HEADS, HEAD_W)[:, :, :MLA_QK].reshape(MLA_Q_RANK, MLA_HEADS * MLA_QK)


def _w_ukv_to_padded(w):
    r = w.reshape(MLA_KV_RANK, MLA_HEADS, MLA_NOPE + MLA_V)
    pad = lambda t: jnp.pad(t, ((0, 0), (0, 0), (0, HEAD_W - t.shape[2]))).reshape(MLA_KV_RANK, MLA_PAD)
    return jnp.concatenate([pad(r[:, :, :MLA_NOPE]), pad(r[:, :, MLA_NOPE:])], axis=1)


def _w_ukv_from_padded(g):
    kk = g[:, :MLA_PAD].reshape(MLA_KV_RANK, MLA_HEADS, HEAD_W)[:, :, :MLA_NOPE]
    vv = g[:, MLA_PAD:].reshape(MLA_KV_RANK, MLA_HEADS, HEAD_W)[:, :, :MLA_V]
    return jnp.concatenate([kk, vv], axis=2).reshape(MLA_KV_RANK, MLA_HEADS * (MLA_NOPE + MLA_V))


def _w_out_to_padded(w):
    att = w[SSD_WIDTH + POOL_WIDTH:].reshape(MLA_HEADS, MLA_V, D_MODEL)
    att = jnp.pad(att, ((0, 0), (0, HEAD_W - MLA_V), (0, 0))).reshape(MLA_PAD, D_MODEL)
    return jnp.concatenate([w[:SSD_WIDTH + POOL_WIDTH], att], axis=0)


def _w_out_from_padded(g):
    att = g[SSD_WIDTH + POOL_WIDTH:].reshape(MLA_HEADS, HEAD_W, D_MODEL)[:, :MLA_V].reshape(MLA_WIDTH, D_MODEL)
    return jnp.concatenate([g[:SSD_WIDTH + POOL_WIDTH], att], axis=0)


def _pad_lanes(v, n=LANE):
    return jnp.pad(v.reshape(1, -1), ((0, 0), (0, n - v.size)))


def _pack_rows(parts, cols, dtype, row_multiple=16):
    flat = jnp.concatenate([p.astype(dtype).reshape(-1) for p in parts])
    rows = -(-flat.size // (cols * row_multiple)) * row_multiple
    return jnp.pad(flat, (0, rows * cols - flat.size)).reshape(rows, cols)


def _unpack_rows(packed, shapes):
    flat = packed.reshape(-1)
    out, at = [], 0
    for s in shapes:
        n = math.prod(s)
        out.append(flat[at:at + n].reshape(s))
        at += n
    return out


def _shard_shape(full_shape, axis):
    s = list(full_shape)
    s[axis] //= N_CHIPS
    return tuple(s)


def _take_shard(a, axis, j):
    n = a.shape[axis] // N_CHIPS
    return lax.slice_in_dim(a, j * n, (j + 1) * n, axis=axis)


def _layer_weights(full, small, l):
    w = {}
    w['w_in_p'] = _w_in_to_padded(full['w_in'][l])
    w['w_uq_p'] = _w_uq_to_padded(full['mla_w_uq'][l])
    w['w_ukv_p'] = _w_ukv_to_padded(full['mla_w_ukv'][l])
    w['w_out_p'] = _w_out_to_padded(full['w_out'][l])
    w['w_up'] = full['ffn_w_up'][l]
    w['w_down'] = full['ffn_w_down'][l]
    for k in ('attn_norm', 'ssd_conv_w', 'ssd_conv_b', 'ssd_norm', 'pool_w', 'pool_scale', 'mla_q_norm', 'mla_kv_norm',
              'ffn_norm', 'ffn_conv_w', 'ffn_conv_b'):
        w[k] = small[k][l]
    w['dtb'] = _pad_lanes(small['ssd_dt_bias'][l])
    w['alog'] = _pad_lanes(small['ssd_a_log'][l])
    w['dchan'] = jnp.repeat(small['ssd_d'][l], SSD_HEAD_DIM).reshape(1, SSD_WIDTH)
    w['ssd_norm'] = w['ssd_norm'].reshape(1, SSD_WIDTH)
    return w


def _layer_fwd(x, pos, invf, w, S, l):
    n = lambda s: f"{s}_l{l}"
    h1 = rmsnorm_fwd(x, w['attn_norm'], name=n("attn_norm"))
    proj = matmul(h1, w['w_in_p'], name=n("w_in"))
    xc = ssd_conv_fwd(proj, w['ssd_conv_w'], w['ssd_conv_b'], S, name=n("ssd_conv"))
    ypre, yssd, hin = ssd_fwd(proj, xc, w['dtb'], w['alog'], w['dchan'], w['ssd_norm'], S, name=n("ssd_scan"))
    ypool, pooled = pool_fwd(proj, w['pool_w'], w['pool_scale'], S, name=n("pool"))
    q, k, v, cqn, ckvn = mla_prep_fwd(proj, pos, invf, w['mla_q_norm'], w['w_uq_p'], w['mla_kv_norm'], w['w_ukv_p'],
                                      name=n("mla_prep"))
    o, lse = flash_fwd(q, k, v, S, name=n("attention"))
    mix = jnp.concatenate([yssd, ypool, o.astype(MXU_DTYPE)], axis=1)
    x2 = matmul(mix, w['w_out_p'], res=x, name=n("w_out"))
    h2 = rmsnorm_fwd(x2, w['ffn_norm'], name=n("ffn_norm"))
    up = matmul(h2, w['w_up'], name=n("ffn_up"))
    act = ffn_conv_gate_fwd(up, w['ffn_conv_w'], w['ffn_conv_b'], S, name=n("ffn_conv_gate"))
    x3 = matmul(act, w['w_down'], res=x2, name=n("ffn_down"))
    saved = dict(x=x, h1=h1, proj=proj, xc=xc, ypre=ypre, hin=hin, pooled=pooled, q=q, k=k, v=v, cqn=cqn, ckvn=ckvn,
                 o=o, lse=lse, mix=mix, x2=x2, h2=h2, up=up, act=act)
    return x3, saved


def _layer_bwd(dx3, pos, invf, w, s, S, l):
    n = lambda t: f"{t}_l{l}"
    g = {}
    dact = matmul(dx3, w['w_down'], nt=True, name=n("d_ffn_down"))
    g['ffn_w_down'] = matmul_tn(s['act'], dx3, name=n("g_ffn_down"))
    dgate, dval, st = ffn_conv_gate_bwd(s['up'], w['ffn_conv_w'], w['ffn_conv_b'], dact, S, name=n("d_ffn_conv_gate"))
    g['ffn_conv_w'], g['ffn_conv_b'] = st[:FFN_CONV], st[FFN_CONV]
    dup_g = conv_bwd_x(dgate, w['ffn_conv_w'][:, :D_FF], S, FFN_CONV, name=n("d_ffn_conv_g"))
    dup_v = conv_bwd_x(dval, w['ffn_conv_w'][:, D_FF:], S, FFN_CONV, name=n("d_ffn_conv_v"))
    dh2 = matmul(dup_g, w['w_up'], nt=True, kblock=0, name=n("d_ffn_up_g"))
    dh2 = matmul(dup_v, w['w_up'], nt=True, kblock=1, res=dh2, name=n("d_ffn_up_v"))
    g['ffn_w_up'] = jnp.concatenate([matmul_tn(s['h2'], dup_g, name=n("g_ffn_up_g")),
                                     matmul_tn(s['h2'], dup_v, name=n("g_ffn_up_v"))], axis=1)
    dx2, gn = rmsnorm_bwd(s['x2'], w['ffn_norm'], dh2, dx3, name=n("d_ffn_norm"))
    g['ffn_norm'] = gn[0]
    dmix = matmul(dx2, w['w_out_p'], nt=True, name=n("d_w_out"))
    g['w_out'] = _w_out_from_padded(matmul_tn(s['mix'], dx2, name=n("g_w_out")))
    dxc, ddt, dz, sm, gsn = ssd_bwd(s['proj'], s['xc'], s['ypre'], s['hin'], dmix, w['dtb'], w['alog'], w['dchan'],
                                    w['ssd_norm'], S, name=n("d_ssd_scan"))
    g['ssd_a_log'], g['ssd_dt_bias'], g['ssd_d'] = sm[0, :SSD_HEADS], sm[1, :SSD_HEADS], sm[2, :SSD_HEADS]
    g['ssd_norm'] = gsn[0]
    dpre, st = ssd_conv_bwd_pre(s['proj'], w['ssd_conv_w'], w['ssd_conv_b'], dxc, S, name=n("d_ssd_conv_act"))
    g['ssd_conv_w'], g['ssd_conv_b'] = st[:SSD_CONV], st[SSD_CONV]
    dxbc = conv_bwd_x(dpre, w['ssd_conv_w'], S, SSD_CONV, name=n("d_ssd_conv"))
    du, g['pool_w'], gps = pool_bwd(dmix, s['pooled'], w['pool_w'], w['pool_scale'], S, name=n("d_pool"))
    g['pool_scale'] = gps[0]
    dq, dk, dv = flash_bwd(s['q'], s['k'], s['v'], s['o'], s['lse'], dmix, S, name=n("d_attention"))
    dqp, dkvp, dcq, dckv, dkpe, gqn, gkn = mla_prep_bwd(s['proj'], pos, invf, w['mla_q_norm'], w['w_uq_p'],
                                                        w['mla_kv_norm'], w['w_ukv_p'], dq, dk, dv, name=n("d_mla_prep"))
    g['mla_q_norm'], g['mla_kv_norm'] = gqn[0], gkn[0]
    g['mla_w_uq'] = _w_uq_from_padded(matmul_tn(s['cqn'], dqp, name=n("g_w_uq")))
    g['mla_w_ukv'] = _w_ukv_from_padded(matmul_tn(s['ckvn'], dkvp, name=n("g_w_ukv")))
    dproj = jnp.concatenate([dz, dxbc, du, dcq, ddt, dckv, dkpe, jnp.zeros_like(dkpe)], axis=1)
    dh1 = matmul(dproj, w['w_in_p'], nt=True, name=n("d_w_in"))
    g['w_in'] = _w_in_from_padded(matmul_tn(s['h1'], dproj, name=n("g_w_in")))
    dx, gn = rmsnorm_bwd(s['x'], w['attn_norm'], dh1, dx2, name=n("d_attn_norm"))
    g['attn_norm'] = gn[0]
    return dx, g


def _local_step(x, positions, target, full, small, S):
    pos = positions.reshape(-1, 1).astype(F32)
    inv_freq = ROPE_THETA ** (-jnp.arange(0, MLA_ROPE, 2, dtype=F32) / MLA_ROPE)
    invf = jnp.concatenate([jnp.zeros((ROPE0,), F32), inv_freq, inv_freq,
                            jnp.zeros((HEAD_W - ROPE0 - MLA_ROPE,), F32)]).reshape(1, HEAD_W)
    ws = [_layer_weights(full, small, l) for l in range(DEPTH)]
    saved = []
    h = x
    for l in range(DEPTH):
        h, s = _layer_fwd(h, pos, invf, ws[l], S, l)
        saved.append(s)
    loss, dh, gfn = final_loss(h, small['final_norm'], target)
    layer_grads = [None] * DEPTH
    for l in reversed(range(DEPTH)):
        dh, layer_grads[l] = _layer_bwd(dh, pos, invf, ws[l], saved[l], S, l)
    grads = {k: jnp.stack([layer_grads[l][k] for l in range(DEPTH)]) for k in layer_grads[0]}
    grads['final_norm'] = gfn[0]
    return loss, dh, grads


def kernel(x, positions, attn_norm, w_in, ssd_conv_w, ssd_conv_b, ssd_dt_bias, ssd_a_log, ssd_d, ssd_norm, pool_w, pool_scale, mla_q_norm, mla_w_uq, mla_kv_norm, mla_w_ukv, w_out, ffn_norm, ffn_w_up, ffn_conv_w, ffn_conv_b, ffn_w_down, final_norm, loss_target, m_attn_norm, m_w_in, m_ssd_conv_w, m_ssd_conv_b, m_ssd_dt_bias, m_ssd_a_log, m_ssd_d, m_ssd_norm, m_pool_w, m_pool_scale, m_mla_q_norm, m_mla_w_uq, m_mla_kv_norm, m_mla_w_ukv, m_w_out, m_ffn_norm, m_ffn_w_up, m_ffn_conv_w, m_ffn_conv_b, m_ffn_w_down, m_final_norm, v_attn_norm, v_w_in, v_ssd_conv_w, v_ssd_conv_b, v_ssd_dt_bias, v_ssd_a_log, v_ssd_d, v_ssd_norm, v_pool_w, v_pool_scale, v_mla_q_norm, v_mla_w_uq, v_mla_kv_norm, v_mla_w_ukv, v_w_out, v_ffn_norm, v_ffn_w_up, v_ffn_conv_w, v_ffn_conv_b, v_ffn_w_down, v_final_norm):
    wv = dict(zip(WEIGHTS, (attn_norm, w_in, ssd_conv_w, ssd_conv_b, ssd_dt_bias, ssd_a_log, ssd_d, ssd_norm, pool_w,
                            pool_scale, mla_q_norm, mla_w_uq, mla_kv_norm, mla_w_ukv, w_out, ffn_norm, ffn_w_up,
                            ffn_conv_w, ffn_conv_b, ffn_w_down, final_norm)))
    mv = dict(zip(WEIGHTS, (m_attn_norm, m_w_in, m_ssd_conv_w, m_ssd_conv_b, m_ssd_dt_bias, m_ssd_a_log, m_ssd_d,
                            m_ssd_norm, m_pool_w, m_pool_scale, m_mla_q_norm, m_mla_w_uq, m_mla_kv_norm, m_mla_w_ukv,
                            m_w_out, m_ffn_norm, m_ffn_w_up, m_ffn_conv_w, m_ffn_conv_b, m_ffn_w_down, m_final_norm)))
    vv = dict(zip(WEIGHTS, (v_attn_norm, v_w_in, v_ssd_conv_w, v_ssd_conv_b, v_ssd_dt_bias, v_ssd_a_log, v_ssd_d,
                            v_ssd_norm, v_pool_w, v_pool_scale, v_mla_q_norm, v_mla_w_uq, v_mla_kv_norm, v_mla_w_ukv,
                            v_w_out, v_ffn_norm, v_ffn_w_up, v_ffn_conv_w, v_ffn_conv_b, v_ffn_w_down, v_final_norm)))
    Bl, S, D = x.shape
    chip = 2 * lax.axis_index("x") + lax.axis_index("y")
    core = lax.axis_index("c")

    big_names = list(BIG)
    packed = _pack_rows([wv[k] for k in big_names], PACK_COLS, MXU_DTYPE, PACK_ROW_MULTIPLE)
    gathered = gather_shards(packed, name="gather_weights")
    shard_shapes = [wv[k].shape for k in big_names]
    per_chip = [_unpack_rows(gathered[j], shard_shapes) for j in range(N_CHIPS)]
    full = {k: jnp.concatenate([per_chip[j][i] for j in range(N_CHIPS)], axis=BIG[k]) for i, k in enumerate(big_names)}
    placed = []
    for k in CONV_SHARDED:
        sh = wv[k]
        whole = jnp.zeros(sh.shape[:-1] + (sh.shape[-1] * N_CHIPS,), F32)
        whole = lax.dynamic_update_slice_in_dim(whole, sh, chip * sh.shape[-1], axis=sh.ndim - 1)
        placed.append(jnp.where(core == 1, whole, 0.0))
    conv_full = _unpack_rows(all_sum_small(_pack_rows(placed, LANE, F32), name="gather_conv_weights"),
                             [p.shape for p in placed])
    small = {k: wv[k] for k in WEIGHTS if k not in BIG}
    small.update(dict(zip(CONV_SHARDED, conv_full)))

    loss, dx, grads = _local_step(x.reshape(Bl * S, D), positions, loss_target.reshape(Bl * S, D), full, small, S)

    send = jnp.stack([_pack_rows([_take_shard(grads[k], BIG[k], j) for k in big_names], PACK_COLS, MXU_DTYPE,
                                 PACK_ROW_MULTIPLE) for j in range(N_CHIPS)])
    recv = chip_exchange(send, per_dest=True, name="scatter_grads")
    both = sibling_exchange(sum4(recv, name="sum_chips"), name="swap_core_sums")
    parts = [_unpack_rows(both[c], shard_shapes) for c in range(2)]
    small_names = [k for k in WEIGHTS if k not in BIG]
    small_sum = all_sum_small(_pack_rows([grads[k] for k in small_names] + [loss[0, :1]], LANE, F32), name="sum_small_grads")
    summed = _unpack_rows(small_sum, [grads[k].shape for k in small_names] + [(1,)])
    loss_total = summed[-1].reshape(())
    g_small = dict(zip(small_names, summed[:-1]))
    for k in CONV_SHARDED:
        n = wv[k].shape[-1]
        g_small[k] = lax.dynamic_slice_in_dim(g_small[k], chip * n, n, axis=g_small[k].ndim - 1)

    out_g, out_d, out_m, out_v = {}, {}, {}, {}
    for i, k in enumerate(big_names):
        shp = wv[k].shape
        two_d = lambda a: a.reshape(-1, shp[-1])
        res = adamw(two_d(wv[k]), [two_d(parts[0][i]), two_d(parts[1][i])], two_d(mv[k]), two_d(vv[k]), name=f"adamw_{k}")
        out_g[k], out_d[k], out_m[k], out_v[k] = (r.reshape(shp) for r in res)
    shapes = [wv[k].shape for k in small_names]
    pk = lambda d: _pack_rows([d[k] for k in small_names], LANE, F32)
    res = adamw(pk(wv), [pk(g_small)], pk(mv), pk(vv), name="adamw_small")
    for dst, r in zip((out_g, out_d, out_m, out_v), res):
        dst.update(dict(zip(small_names, _unpack_rows(r, shapes))))
    return (loss_total, dx.reshape(Bl, S, D), *[out_g[k] for k in WEIGHTS], *[out_d[k] for k in WEIGHTS],
            *[out_m[k] for k in WEIGHTS], *[out_v[k] for k in WEIGHTS])
```

```python
import functools
import math

import jax
import jax.numpy as jnp
from jax import lax
from jax.experimental import pallas as pl
from jax.experimental.pallas import tpu as pltpu

F32 = jnp.float32
BF16 = jnp.bfloat16
MXU_DTYPE = jnp.bfloat16
HI = lax.Precision.HIGHEST

D_MODEL = 1024
DEPTH = 2
EPS = 1e-6
SSD_HEADS = 16
SSD_HEAD_DIM = 64
SSD_WIDTH = 1024
SSD_GROUPS = 2
SSD_STATE = 128
SSD_CONV = 4
SSD_CHUNK = 128
SSD_CONV_CH = 1536
POOL_GROUPS = 4
POOL_GROUP_DIM = 128
POOL_WIDTH = 512
POOL_WINDOWS = (2, 4, 8, 16)
MLA_HEADS = 8
MLA_Q_RANK = 384
MLA_KV_RANK = 256
MLA_NOPE = 64
MLA_ROPE = 32
MLA_V = 64
MLA_QK = 96
MLA_WIDTH = 512
ROPE_THETA = 10000.0
MIX_WIDTH = 2048
IN_COLS = 3760
D_FF = 2816
FFN_CONV = 3
ADAM_LR = 0.001
ADAM_B1 = 0.9
ADAM_B2 = 0.999
ADAM_EPS = 1e-08
ADAM_WD = 0.01
ADAM_STEP = 10

LANE = 128
HALO = 8
POOL_HALO = 16
PZ0 = 0
PXBC0 = 1024
PU0 = 2560
PCQ0 = 3072
PDT0 = 3456
PCKV0 = 3584
PKPE0 = 3840
PROJ_W = 4096
HEAD_W = 128
MLA_PAD = MLA_HEADS * HEAD_W
MIXP = SSD_WIDTH + POOL_WIDTH + MLA_PAD
N_CHIPS = 4
N_DEV = 8
VMEM_LIMIT = 56 * 1024 * 1024


def _cparams(dims, vmem=None):
    return pltpu.CompilerParams(dimension_semantics=dims, vmem_limit_bytes=vmem or VMEM_LIMIT)


def _sds(shape, dtype):
    return jax.ShapeDtypeStruct(tuple(shape), dtype)


def _mx(v):
    return v.astype(MXU_DTYPE)


def _dot(a, b):
    return jnp.dot(_mx(a), _mx(b), preferred_element_type=F32)


def _dot_nt(a, b):
    return lax.dot_general(_mx(a), _mx(b), (((1,), (1,)), ((), ())), preferred_element_type=F32)


def _dot_tn(a, b):
    return lax.dot_general(_mx(a), _mx(b), (((0,), (0,)), ((), ())), preferred_element_type=F32)


def _dot_hi(a, b):
    return jnp.dot(a, b, preferred_element_type=F32, precision=HI)


def _sigmoid(v):
    return 1.0 / (1.0 + jnp.exp(-v))


def _pick(n, prefs):
    for p in prefs:
        if n % p == 0:
            return p
    return n


def matmul(a, b, *, res=None, out_dtype=F32, name, nt=False, kblock=0, tm=None, tn=None):
    M, K = a.shape
    N = b.shape[0] if nt else b.shape[1]
    assert (b.shape[1] % K == 0) if nt else (K == b.shape[0] and kblock == 0)
    tm = tm or _pick(M, (1024, 512, 256, 128))
    tn = tn or _pick(N, (512, 1408, 256, 128))

    def body(*refs):
        a_ref, b_ref = refs[:2]
        o_ref = refs[-1]
        out = (_dot_nt if nt else _dot)(a_ref[...], b_ref[...])
        if res is not None:
            out = out + refs[2][...]
        o_ref[...] = out.astype(out_dtype)

    b_spec = pl.BlockSpec((tn, K), lambda i, j: (j, kblock)) if nt else pl.BlockSpec((K, tn), lambda i, j: (0, j))
    in_specs = [pl.BlockSpec((tm, K), lambda i, j: (i, 0)), b_spec]
    args = [a, b]
    if res is not None:
        in_specs.append(pl.BlockSpec((tm, tn), lambda i, j: (i, j)))
        args.append(res)
    return pl.pallas_call(
        body, name=name, grid=(M // tm, N // tn), in_specs=in_specs,
        out_specs=pl.BlockSpec((tm, tn), lambda i, j: (i, j)), out_shape=_sds((M, N), out_dtype),
        compiler_params=_cparams(("parallel", "parallel")),
    )(*args)


def matmul_tn(a, g, *, name, tm=None, tn=None, tk=None):
    T, M = a.shape
    T2, N = g.shape
    assert T == T2
    tm = tm or _pick(M, (1408, 1280, 1024, 512, 384, 256, 128))
    tn = tn or _pick(N, (1024, 1408, 512, 256, 128))
    tk = tk or _pick(T, (512, 256, 128))
    nk = T // tk

    def body(a_ref, g_ref, o_ref, acc):
        k = pl.program_id(2)
        part = _dot_tn(a_ref[...], g_ref[...])

        @pl.when(k == 0)
        def _():
            acc[...] = part

        @pl.when(k > 0)
        def _():
            acc[...] += part

        @pl.when(k == nk - 1)
        def _():
            o_ref[...] = acc[...].astype(o_ref.dtype)

    return pl.pallas_call(
        body, name=name, grid=(M // tm, N // tn, nk),
        in_specs=[pl.BlockSpec((tk, tm), lambda i, j, k: (k, i)), pl.BlockSpec((tk, tn), lambda i, j, k: (k, j))],
        out_specs=pl.BlockSpec((tm, tn), lambda i, j, k: (i, j)), out_shape=_sds((M, N), MXU_DTYPE),
        scratch_shapes=[pltpu.VMEM((tm, tn), F32)],
        compiler_params=_cparams(("parallel", "parallel", "arbitrary")),
    )(a, g)


def rmsnorm_fwd(x, gamma, *, name, tm=512):
    T, D = x.shape
    tm = _pick(T, (tm, 256, 128))

    def body(x_ref, g_ref, o_ref):
        xv = x_ref[...]
        r = lax.rsqrt(jnp.mean(xv * xv, axis=-1, keepdims=True) + EPS)
        o_ref[...] = ((xv * r) * g_ref[...]).astype(MXU_DTYPE)

    return pl.pallas_call(
        body, name=name, grid=(T // tm,),
        in_specs=[pl.BlockSpec((tm, D), lambda i: (i, 0)), pl.BlockSpec((1, D), lambda i: (0, 0))],
        out_specs=pl.BlockSpec((tm, D), lambda i: (i, 0)), out_shape=_sds((T, D), MXU_DTYPE),
        compiler_params=_cparams(("parallel",)),
    )(x, gamma.reshape(1, D))


def _rms_bwd_tile(xv, gamma, dh):
    r = lax.rsqrt(jnp.mean(xv * xv, axis=-1, keepdims=True) + EPS)
    xh = xv * r
    dg = jnp.sum(dh * xh, axis=0, keepdims=True)
    dn = dh * gamma
    dx = r * (dn - xh * jnp.mean(dn * xh, axis=-1, keepdims=True))
    return dx, dg


def rmsnorm_bwd(x, gamma, dh, dres, *, name, tm=256):
    T, D = x.shape
    tm = _pick(T, (tm, 128))

    def body(x_ref, g_ref, dh_ref, dr_ref, dx_ref, dg_ref):
        dx, dg = _rms_bwd_tile(x_ref[...], g_ref[...], dh_ref[...].astype(F32))
        dx_ref[...] = dx + dr_ref[...]

        @pl.when(pl.program_id(0) == 0)
        def _():
            dg_ref[...] = dg

        @pl.when(pl.program_id(0) > 0)
        def _():
            dg_ref[...] += dg

    row = pl.BlockSpec((tm, D), lambda i: (i, 0))
    vec = pl.BlockSpec((1, D), lambda i: (0, 0))
    return pl.pallas_call(
        body, name=name, grid=(T // tm,), in_specs=[row, vec, row, row], out_specs=[row, vec],
        out_shape=[_sds((T, D), F32), _sds((1, D), F32)], compiler_params=_cparams(("arbitrary",)),
    )(x, gamma.reshape(1, D), dh, dres)


def final_loss(x, gamma, target, *, name="final_loss", tm=256):
    T, D = x.shape
    tm = _pick(T, (tm, 128))

    def body(x_ref, g_ref, t_ref, l_ref, dx_ref, dg_ref):
        xv = x_ref[...]
        gam = g_ref[...]
        r = lax.rsqrt(jnp.mean(xv * xv, axis=-1, keepdims=True) + EPS)
        y = (xv * r) * gam
        err = y - t_ref[...]
        part = 0.5 * jnp.sum(jnp.sum(err * err, axis=-1, keepdims=True) / D, axis=0, keepdims=True)
        dx, dg = _rms_bwd_tile(xv, gam, err / D)
        dx_ref[...] = dx

        @pl.when(pl.program_id(0) == 0)
        def _():
            dg_ref[...] = dg
            l_ref[...] = jnp.broadcast_to(part, l_ref.shape)

        @pl.when(pl.program_id(0) > 0)
        def _():
            dg_ref[...] += dg
            l_ref[...] += jnp.broadcast_to(part, l_ref.shape)

    row = pl.BlockSpec((tm, D), lambda i: (i, 0))
    vec = pl.BlockSpec((1, D), lambda i: (0, 0))
    return pl.pallas_call(
        body, name=name, grid=(T // tm,), in_specs=[row, vec, row],
        out_specs=[pl.BlockSpec((1, LANE), lambda i: (0, 0)), row, vec],
        out_shape=[_sds((1, LANE), F32), _sds((T, D), F32), _sds((1, D), F32)],
        compiler_params=_cparams(("arbitrary",)),
    )(x, gamma.reshape(1, D), target)


def _halo_prev(ts):
    return lambda i, j, off=0: (jnp.maximum(i * (ts // HALO) - 1, 0), j + off)


def _cat_prev(cur, halo, first):
    return jnp.concatenate([jnp.where(first, 0.0, halo), cur], axis=0)


def _cat_next(cur, halo, last):
    return jnp.concatenate([cur, jnp.where(last, 0.0, halo)], axis=0)


def _delayed(cat, r):
    if r == 0:
        return cat[HALO:]
    return pltpu.roll(cat, r, axis=0)[HALO:]


def _advanced(cat, r):
    n = cat.shape[0]
    if r == 0:
        return cat[:n - HALO]
    return pltpu.roll(cat, n - r, axis=0)[:n - HALO]


def _conv_pre(cat, w, b, K):
    acc = _delayed(cat, K - 1) * w[0:1, :] + b
    for k in range(1, K):
        acc = acc + _delayed(cat, K - 1 - k) * w[k:k + 1, :]
    return acc


def _pad_rows8(w):
    return jnp.pad(w, ((0, 8 - w.shape[0]), (0, 0)))


def ssd_conv_fwd(proj, w, b, S, *, name, ts=1024, tc=512):
    T = proj.shape[0]
    C, K = SSD_CONV_CH, SSD_CONV
    ts = _pick(S, (ts, 256, 128))
    off = PXBC0 // tc
    ns = S // ts

    def body(x_ref, h_ref, w_ref, b_ref, o_ref):
        first = (pl.program_id(0) % ns) == 0
        pre = _conv_pre(_cat_prev(x_ref[...], h_ref[...], first), w_ref[...], b_ref[...], K)
        o_ref[...] = pre * _sigmoid(pre)

    return pl.pallas_call(
        body, name=name, grid=(T // ts, C // tc),
        in_specs=[pl.BlockSpec((ts, tc), lambda i, j: (i, j + off)),
                  pl.BlockSpec((HALO, tc), functools.partial(_halo_prev(ts), off=off)),
                  pl.BlockSpec((8, tc), lambda i, j: (0, j)), pl.BlockSpec((1, tc), lambda i, j: (0, j))],
        out_specs=pl.BlockSpec((ts, tc), lambda i, j: (i, j)), out_shape=_sds((T, C), F32),
        compiler_params=_cparams(("parallel", "parallel")),
    )(proj, proj, _pad_rows8(w), b.reshape(1, C))


def ssd_conv_bwd_pre(proj, w, b, dxc, S, *, name, ts=1024, tc=512):
    T = proj.shape[0]
    C, K = SSD_CONV_CH, SSD_CONV
    ts = _pick(S, (ts, 256, 128))
    off = PXBC0 // tc
    ns = S // ts

    def body(x_ref, h_ref, w_ref, b_ref, d_ref, o_ref, acc_ref):
        i = pl.program_id(1)
        first = (i % ns) == 0
        cat = _cat_prev(x_ref[...], h_ref[...], first)
        pre = _conv_pre(cat, w_ref[...], b_ref[...], K)
        sg = _sigmoid(pre)
        dpre = d_ref[...] * (sg * (1.0 + pre * (1.0 - sg)))
        o_ref[...] = dpre
        rows = [jnp.sum(dpre * _delayed(cat, K - 1 - k), axis=0, keepdims=True) for k in range(K)]
        rows.append(jnp.sum(dpre, axis=0, keepdims=True))
        rows.append(jnp.zeros((8 - len(rows), dpre.shape[1]), F32))
        part = jnp.concatenate(rows, axis=0)

        @pl.when(i == 0)
        def _():
            acc_ref[...] = part

        @pl.when(i > 0)
        def _():
            acc_ref[...] += part

    hp = _halo_prev(ts)
    return pl.pallas_call(
        body, name=name, grid=(C // tc, T // ts),
        in_specs=[pl.BlockSpec((ts, tc), lambda j, i: (i, j + off)),
                  pl.BlockSpec((HALO, tc), lambda j, i: hp(i, j, off)),
                  pl.BlockSpec((8, tc), lambda j, i: (0, j)), pl.BlockSpec((1, tc), lambda j, i: (0, j)),
                  pl.BlockSpec((ts, tc), lambda j, i: (i, j))],
        out_specs=[pl.BlockSpec((ts, tc), lambda j, i: (i, j)), pl.BlockSpec((8, tc), lambda j, i: (0, j))],
        out_shape=[_sds((T, C), F32), _sds((8, C), F32)],
        compiler_params=_cparams(("parallel", "arbitrary")),
    )(proj, proj, _pad_rows8(w), b.reshape(1, C), dxc)


def conv_bwd_x(dpre, w, S, K, *, name, ts=512, tc=None):
    T, C = dpre.shape
    out_dtype = MXU_DTYPE
    ts = _pick(S, (ts, 256, 128))
    tc = tc or _pick(C, (1408, 512, 256, 128))
    ns = S // ts
    nblk = T // HALO

    def body(d_ref, h_ref, w_ref, o_ref):
        last = (pl.program_id(0) % ns) == ns - 1
        cat = _cat_next(d_ref[...], h_ref[...], last)
        wv = w_ref[...]
        acc = _advanced(cat, K - 1) * wv[0:1, :]
        for k in range(1, K):
            acc = acc + _advanced(cat, K - 1 - k) * wv[k:k + 1, :]
        o_ref[...] = acc.astype(out_dtype)

    return pl.pallas_call(
        body, name=name, grid=(T // ts, C // tc),
        in_specs=[pl.BlockSpec((ts, tc), lambda i, j: (i, j)),
                  pl.BlockSpec((HALO, tc), lambda i, j: (jnp.minimum((i + 1) * (ts // HALO), nblk - 1), j)),
                  pl.BlockSpec((8, tc), lambda i, j: (0, j))],
        out_specs=pl.BlockSpec((ts, tc), lambda i, j: (i, j)), out_shape=_sds((T, C), out_dtype),
        compiler_params=_cparams(("parallel", "parallel")),
    )(dpre, dpre, _pad_rows8(w))


def ffn_conv_gate_fwd(up, w, b, S, *, name, ts=512, tc=1408):
    T, C2 = up.shape
    C, K = C2 // 2, FFN_CONV
    ts = _pick(S, (ts, 256, 128))
    nj = C // tc
    ns = S // ts
    w8 = _pad_rows8(w)
    b2 = b.reshape(1, C2)

    def body(g_ref, gh_ref, v_ref, vh_ref, wg_ref, wv_ref, bg_ref, bv_ref, o_ref):
        first = (pl.program_id(0) % ns) == 0
        g = _conv_pre(_cat_prev(g_ref[...], gh_ref[...], first), wg_ref[...], bg_ref[...], K)
        v = _conv_pre(_cat_prev(v_ref[...], vh_ref[...], first), wv_ref[...], bv_ref[...], K)
        o_ref[...] = (g * _sigmoid(g) * v).astype(o_ref.dtype)

    hp = _halo_prev(ts)
    return pl.pallas_call(
        body, name=name, grid=(T // ts, nj),
        in_specs=[pl.BlockSpec((ts, tc), lambda i, j: (i, j)), pl.BlockSpec((HALO, tc), lambda i, j: hp(i, j)),
                  pl.BlockSpec((ts, tc), lambda i, j: (i, j + nj)), pl.BlockSpec((HALO, tc), lambda i, j: hp(i, j, nj)),
                  pl.BlockSpec((8, tc), lambda i, j: (0, j)), pl.BlockSpec((8, tc), lambda i, j: (0, j + nj)),
                  pl.BlockSpec((1, tc), lambda i, j: (0, j)), pl.BlockSpec((1, tc), lambda i, j: (0, j + nj))],
        out_specs=pl.BlockSpec((ts, tc), lambda i, j: (i, j)), out_shape=_sds((T, C), MXU_DTYPE),
        compiler_params=_cparams(("parallel", "parallel")),
    )(up, up, up, up, w8, w8, b2, b2)


def ffn_conv_gate_bwd(up, w, b, dact, S, *, name, ts=256, tc=1408):
    T, C2 = up.shape
    C, K = C2 // 2, FFN_CONV
    ts = _pick(S, (ts, 256, 128))
    nj = C // tc
    ns = S // ts
    w8 = _pad_rows8(w)
    b2 = b.reshape(1, C2)

    def stats(dpre, cat):
        rows = [jnp.sum(dpre * _delayed(cat, K - 1 - k), axis=0, keepdims=True) for k in range(K)]
        rows.append(jnp.sum(dpre, axis=0, keepdims=True))
        rows.append(jnp.zeros((8 - len(rows), dpre.shape[1]), F32))
        return jnp.concatenate(rows, axis=0)

    def body(g_ref, gh_ref, v_ref, vh_ref, wg_ref, wv_ref, bg_ref, bv_ref, d_ref, dg_ref, dv_ref, ag_ref, av_ref):
        i = pl.program_id(1)
        first = (i % ns) == 0
        gcat = _cat_prev(g_ref[...], gh_ref[...], first)
        vcat = _cat_prev(v_ref[...], vh_ref[...], first)
        g = _conv_pre(gcat, wg_ref[...], bg_ref[...], K)
        v = _conv_pre(vcat, wv_ref[...], bv_ref[...], K)
        d = d_ref[...]
        sg = _sigmoid(g)
        dg = d * v * (sg * (1.0 + g * (1.0 - sg)))
        dv = d * (g * sg)
        dg_ref[...] = dg
        dv_ref[...] = dv
        sgp, svp = stats(dg, gcat), stats(dv, vcat)

        @pl.when(i == 0)
        def _():
            ag_ref[...] = sgp
            av_ref[...] = svp

        @pl.when(i > 0)
        def _():
            ag_ref[...] += sgp
            av_ref[...] += svp

    hp = _halo_prev(ts)
    dg, dv, ag, av = pl.pallas_call(
        body, name=name, grid=(nj, T // ts),
        in_specs=[pl.BlockSpec((ts, tc), lambda j, i: (i, j)), pl.BlockSpec((HALO, tc), lambda j, i: hp(i, j)),
                  pl.BlockSpec((ts, tc), lambda j, i: (i, j + nj)), pl.BlockSpec((HALO, tc), lambda j, i: hp(i, j, nj)),
                  pl.BlockSpec((8, tc), lambda j, i: (0, j)), pl.BlockSpec((8, tc), lambda j, i: (0, j + nj)),
                  pl.BlockSpec((1, tc), lambda j, i: (0, j)), pl.BlockSpec((1, tc), lambda j, i: (0, j + nj)),
                  pl.BlockSpec((ts, tc), lambda j, i: (i, j))],
        out_specs=[pl.BlockSpec((ts, tc), lambda j, i: (i, j)), pl.BlockSpec((ts, tc), lambda j, i: (i, j)),
                   pl.BlockSpec((8, tc), lambda j, i: (0, j)), pl.BlockSpec((8, tc), lambda j, i: (0, j))],
        out_shape=[_sds((T, C), F32), _sds((T, C), F32), _sds((8, C), F32), _sds((8, C), F32)],
        compiler_params=_cparams(("parallel", "arbitrary")),
    )(up, up, up, up, w8, w8, b2, b2, dact)
    return dg, dv, jnp.concatenate([ag, av], axis=1)


def _pool_counts(pos, w):
    return jnp.minimum(pos + 1.0, float(w))


def pool_fwd(proj, pool_w, pool_scale, S, *, name, ts=512):
    T = proj.shape[0]
    C, G, GD, H = POOL_WIDTH, POOL_GROUPS, POOL_GROUP_DIM, POOL_HALO
    ts = _pick(S, (ts, 256, 128))
    ns = S // ts
    off = PU0 // C

    def body(u_ref, h_ref, w_ref, s_ref, y_ref, p_ref):
        i = pl.program_id(0)
        first = (i % ns) == 0
        cat = jnp.concatenate([jnp.where(first, 0.0, h_ref[...]), u_ref[...]], axis=0)
        pos = ((i % ns) * ts + lax.broadcasted_iota(jnp.int32, (ts, 1), 0)).astype(F32)
        sums = cat
        win = 1
        for g, wlen in enumerate(POOL_WINDOWS):
            while win < wlen:
                sums = sums + pltpu.roll(sums, win, axis=0)
                win *= 2
            sl = slice(g * GD, (g + 1) * GD)
            pooled = sums[H:, sl] / _pool_counts(pos, wlen) - cat[H:, sl]
            p_ref[:, sl] = pooled.astype(p_ref.dtype)
            y_ref[:, sl] = (_dot(pooled, w_ref[g]) * s_ref[:, sl]).astype(y_ref.dtype)

    return pl.pallas_call(
        body, name=name, grid=(T // ts,),
        in_specs=[pl.BlockSpec((ts, C), lambda i: (i, off)),
                  pl.BlockSpec((H, C), lambda i: (jnp.maximum(i * (ts // H) - 1, 0), off)),
                  pl.BlockSpec((G, GD, GD), lambda i: (0, 0, 0)), pl.BlockSpec((1, C), lambda i: (0, 0))],
        out_specs=[pl.BlockSpec((ts, C), lambda i: (i, 0)), pl.BlockSpec((ts, C), lambda i: (i, 0))],
        out_shape=[_sds((T, C), MXU_DTYPE), _sds((T, C), MXU_DTYPE)],
        compiler_params=_cparams(("parallel",)),
    )(proj, proj, _mx(pool_w), pool_scale.reshape(1, C))


def pool_bwd(dmix, pooled, pool_w, pool_scale, S, *, name, ts=512):
    T = dmix.shape[0]
    C, G, GD, H = POOL_WIDTH, POOL_GROUPS, POOL_GROUP_DIM, POOL_HALO
    ts = _pick(S, (ts, 256, 128))
    ns = S // ts
    off = SSD_WIDTH // C
    nblk = T // H

    def body(d_ref, dh_ref, p_ref, w_ref, s_ref, du_ref, dw_ref, ds_ref):
        i = pl.program_id(0)
        last = (i % ns) == ns - 1
        dcat = jnp.concatenate([d_ref[...], jnp.where(last, 0.0, dh_ref[...])], axis=0)
        n = ts + H
        pos = ((i % ns) * ts + lax.broadcasted_iota(jnp.int32, (n, 1), 0)).astype(F32)
        dws, dss = [], []
        for g, wlen in enumerate(POOL_WINDOWS):
            sl = slice(g * GD, (g + 1) * GD)
            wg = w_ref[g]
            pg = p_ref[:, sl]
            dys = dcat[:, sl] * s_ref[:, sl]
            dss.append(jnp.sum(dcat[:ts, sl] * _dot(pg, wg), axis=0, keepdims=True))
            dws.append(_dot_tn(pg, dys[:ts]))
            dp = _dot_nt(dys, wg)
            q = dp / _pool_counts(pos, wlen)
            win = 1
            while win < wlen:
                q = q + pltpu.roll(q, n - win, axis=0)
                win *= 2
            du_ref[:, sl] = (q[:ts] - dp[:ts]).astype(du_ref.dtype)
        dsp = jnp.concatenate(dss, axis=1)

        @pl.when(i == 0)
        def _():
            for g in range(G):
                dw_ref[g] = dws[g]
            ds_ref[...] = dsp

        @pl.when(i > 0)
        def _():
            for g in range(G):
                dw_ref[g] += dws[g]
            ds_ref[...] += dsp

    return pl.pallas_call(
        body, name=name, grid=(T // ts,),
        in_specs=[pl.BlockSpec((ts, C), lambda i: (i, off)),
                  pl.BlockSpec((H, C), lambda i: (jnp.minimum((i + 1) * (ts // H), nblk - 1), off)),
                  pl.BlockSpec((ts, C), lambda i: (i, 0)),
                  pl.BlockSpec((G, GD, GD), lambda i: (0, 0, 0)), pl.BlockSpec((1, C), lambda i: (0, 0))],
        out_specs=[pl.BlockSpec((ts, C), lambda i: (i, 0)), pl.BlockSpec((G, GD, GD), lambda i: (0, 0, 0)),
                   pl.BlockSpec((1, C), lambda i: (0, 0))],
        out_shape=[_sds((T, C), MXU_DTYPE), _sds((G, GD, GD), F32), _sds((1, C), F32)],
        compiler_params=_cparams(("arbitrary",)),
    )(dmix, dmix, pooled, _mx(pool_w), pool_scale.reshape(1, C))


ROPE0 = MLA_NOPE
ROPE_HALF = MLA_ROPE // 2


def _rope_tables(pos, invf):
    lane = lax.broadcasted_iota(jnp.int32, (1, HEAD_W), 1)
    ang = pos * invf
    cs, sn = jnp.cos(ang), jnp.sin(ang)
    in_a = (lane >= ROPE0) & (lane < ROPE0 + ROPE_HALF)
    in_b = (lane >= ROPE0 + ROPE_HALF) & (lane < ROPE0 + MLA_ROPE)
    return jnp.where(in_a | in_b, cs, 1.0), jnp.where(in_a, -sn, 0.0), jnp.where(in_b, sn, 0.0), in_a | in_b


def _rope(v, cosf, sin_a, sin_b):
    return (v * cosf + pltpu.roll(v, HEAD_W - ROPE_HALF, axis=1) * sin_a + pltpu.roll(v, ROPE_HALF, axis=1) * sin_b)


def _unrope(d, cosf, sin_a, sin_b):
    return (d * cosf + pltpu.roll(d * sin_a, ROPE_HALF, axis=1) + pltpu.roll(d * sin_b, HEAD_W - ROPE_HALF, axis=1))


def _rms_tile(xv, gamma):
    return (xv * lax.rsqrt(jnp.mean(xv * xv, axis=-1, keepdims=True) + EPS)) * gamma


def mla_prep_fwd(proj, pos, invf, q_norm, w_uq_p, kv_norm, w_ukv_p, *, name, tm=256):
    T = proj.shape[0]
    tm = _pick(T, (tm, 128))
    QR, KR, P = MLA_Q_RANK, MLA_KV_RANK, MLA_PAD

    def body(cq_ref, ckv_ref, kpe_ref, pos_ref, invf_ref, qn_ref, wq_ref, kn_ref, wkv_ref,
             q_ref, k_ref, v_ref, cqn_ref, ckvn_ref):
        cosf, sin_a, sin_b, _ = _rope_tables(pos_ref[...], invf_ref[...])
        cqn = _rms_tile(cq_ref[...], qn_ref[...]).astype(MXU_DTYPE)
        ckvn = _rms_tile(ckv_ref[...], kn_ref[...]).astype(MXU_DTYPE)
        cqn_ref[...] = cqn
        ckvn_ref[...] = ckvn
        qp = _dot(cqn, wq_ref[...])
        kvp = _dot(ckvn, wkv_ref[...])
        kpe = _rope(kpe_ref[...], cosf, sin_a, sin_b)
        for h in range(MLA_HEADS):
            sl = slice(h * HEAD_W, (h + 1) * HEAD_W)
            q_ref[:, sl] = (_rope(qp[:, sl], cosf, sin_a, sin_b) * ATTN_SCALE).astype(q_ref.dtype)
            k_ref[:, sl] = (kvp[:, sl] + kpe).astype(k_ref.dtype)
            v_ref[:, sl] = kvp[:, P + h * HEAD_W:P + (h + 1) * HEAD_W].astype(v_ref.dtype)

    row = lambda w: pl.BlockSpec((tm, w), lambda i: (i, 0))
    full = lambda a, b: pl.BlockSpec((a, b), lambda i: (0, 0))
    return pl.pallas_call(
        body, name=name, grid=(T // tm,),
        in_specs=[pl.BlockSpec((tm, QR), lambda i: (i, PCQ0 // QR)), pl.BlockSpec((tm, KR), lambda i: (i, PCKV0 // KR)),
                  pl.BlockSpec((tm, LANE), lambda i: (i, PKPE0 // LANE)), row(1), full(1, LANE),
                  full(1, QR), full(QR, P), full(1, KR), full(KR, 2 * P)],
        out_specs=[row(P), row(P), row(P), row(QR), row(KR)],
        out_shape=[_sds((T, P), MXU_DTYPE)] * 3 + [_sds((T, QR), MXU_DTYPE), _sds((T, KR), MXU_DTYPE)],
        compiler_params=_cparams(("parallel",)),
    )(proj, proj, proj, pos, invf, q_norm.reshape(1, QR), w_uq_p, kv_norm.reshape(1, KR), w_ukv_p)


def mla_prep_bwd(proj, pos, invf, q_norm, w_uq_p, kv_norm, w_ukv_p, dq, dk, dv, *, name, tm=256):
    T = proj.shape[0]
    tm = _pick(T, (tm, 128))
    QR, KR, P = MLA_Q_RANK, MLA_KV_RANK, MLA_PAD

    def body(cq_ref, ckv_ref, pos_ref, invf_ref, qn_ref, wq_ref, kn_ref, wkv_ref, dq_ref, dk_ref, dv_ref,
             dqp_ref, dkvp_ref, dcq_ref, dckv_ref, dkpe_ref, dqn_ref, dkn_ref):
        cosf, sin_a, sin_b, rot = _rope_tables(pos_ref[...], invf_ref[...])
        dkpe = jnp.zeros((tm, HEAD_W), F32)
        for h in range(MLA_HEADS):
            sl = slice(h * HEAD_W, (h + 1) * HEAD_W)
            dqp_ref[:, sl] = _unrope(dq_ref[:, sl] * ATTN_SCALE, cosf, sin_a, sin_b).astype(dqp_ref.dtype)
            dkh = dk_ref[:, sl]
            dkpe = dkpe + dkh
            dkvp_ref[:, sl] = dkh.astype(dkvp_ref.dtype)
            dkvp_ref[:, P + h * HEAD_W:P + (h + 1) * HEAD_W] = dv_ref[:, sl].astype(dkvp_ref.dtype)
        dkpe_ref[...] = jnp.where(rot, _unrope(dkpe, cosf, sin_a, sin_b), 0.0).astype(dkpe_ref.dtype)
        dcq, dqn = _rms_bwd_tile(cq_ref[...], qn_ref[...], _dot_nt(dqp_ref[...], wq_ref[...]))
        dckv, dkn = _rms_bwd_tile(ckv_ref[...], kn_ref[...], _dot_nt(dkvp_ref[...], wkv_ref[...]))
        dcq_ref[...] = dcq.astype(dcq_ref.dtype)
        dckv_ref[...] = dckv.astype(dckv_ref.dtype)

        @pl.when(pl.program_id(0) == 0)
        def _():
            dqn_ref[...] = dqn
            dkn_ref[...] = dkn

        @pl.when(pl.program_id(0) > 0)
        def _():
            dqn_ref[...] += dqn
            dkn_ref[...] += dkn

    row = lambda w: pl.BlockSpec((tm, w), lambda i: (i, 0))
    full = lambda a, b: pl.BlockSpec((a, b), lambda i: (0, 0))
    return pl.pallas_call(
        body, name=name, grid=(T // tm,),
        in_specs=[pl.BlockSpec((tm, QR), lambda i: (i, PCQ0 // QR)), pl.BlockSpec((tm, KR), lambda i: (i, PCKV0 // KR)),
                  row(1), full(1, LANE), full(1, QR), full(QR, P), full(1, KR), full(KR, 2 * P), row(P), row(P), row(P)],
        out_specs=[row(P), row(2 * P), row(QR), row(KR), row(LANE), full(1, QR), full(1, KR)],
        out_shape=[_sds((T, P), MXU_DTYPE), _sds((T, 2 * P), MXU_DTYPE), _sds((T, QR), MXU_DTYPE),
                   _sds((T, KR), MXU_DTYPE), _sds((T, LANE), MXU_DTYPE), _sds((1, QR), F32), _sds((1, KR), F32)],
        compiler_params=_cparams(("arbitrary",)),
    )(proj, proj, pos, invf, q_norm.reshape(1, QR), w_uq_p, kv_norm.reshape(1, KR), w_ukv_p, dq, dk, dv)


ATTN_SCALE = 1.0 / math.sqrt(MLA_QK)


def _causal_mask(i, j, blk):
    row = lax.broadcasted_iota(jnp.int32, (blk, blk), 0)
    col = lax.broadcasted_iota(jnp.int32, (blk, blk), 1)
    return col <= row + (i - j) * blk


def flash_fwd(q, k, v, S, *, name, blk=512):
    T, P = q.shape
    blk = _pick(S, (blk, 256, 128))
    B, nq, H, W = T // S, S // blk, MLA_HEADS, HEAD_W

    def body(q_ref, k_ref, v_ref, o_ref, lse_ref):
        i = pl.program_id(2)
        qv = q_ref[...]

        def online(j, carry, masked):
            m_prev, l_prev, acc = carry
            rows = pl.ds(pl.multiple_of(j * blk, blk), blk)
            s = _dot_nt(qv, k_ref[rows, :])
            if masked:
                s = jnp.where(_causal_mask(0, 0, blk), s, -jnp.inf)
            m_new = jnp.maximum(m_prev, jnp.max(s, axis=1, keepdims=True))
            p = jnp.exp(s - m_new)
            alpha = jnp.exp(m_prev - m_new)
            return (m_new, alpha * l_prev + jnp.sum(p, axis=1, keepdims=True), alpha * acc + _dot(p, v_ref[rows, :]))

        init = (jnp.full((blk, 1), -jnp.inf, F32), jnp.zeros((blk, 1), F32), jnp.zeros((blk, W), F32))
        carry = lax.fori_loop(0, i, lambda j, c: online(j, c, False), init)
        m, l, acc = online(i, carry, True)
        o_ref[...] = acc / l
        lse_ref[...] = jnp.broadcast_to(m + jnp.log(l), (blk, W))

    qmap = lambda b, h, i: (b * nq + i, h)
    kmap = lambda b, h, i: (b, h)
    return pl.pallas_call(
        body, name=name, grid=(B, H, nq),
        in_specs=[pl.BlockSpec((blk, W), qmap), pl.BlockSpec((S, W), kmap), pl.BlockSpec((S, W), kmap)],
        out_specs=[pl.BlockSpec((blk, W), qmap), pl.BlockSpec((blk, W), qmap)],
        out_shape=[_sds((T, P), F32), _sds((T, P), F32)],
        compiler_params=_cparams(("parallel", "parallel", "arbitrary")),
    )(q, k, v)


def flash_bwd(q, k, v, o, lse, dmix, S, *, name, blk=512):
    T, P = q.shape
    blk = _pick(S, (blk, 256, 128))
    B, nq, H, W = T // S, S // blk, MLA_HEADS, HEAD_W
    off = (SSD_WIDTH + POOL_WIDTH) // W

    def body(q_ref, k_ref, v_ref, o_ref, lse_ref, do_ref, dq_ref, dk_ref, dv_ref, delta_s):
        j = pl.program_id(2)

        @pl.when(j == 0)
        def _():
            for i in range(nq):
                rows = slice(i * blk, (i + 1) * blk)
                delta_s[rows, :] = jnp.sum(do_ref[rows, :] * o_ref[rows, :], axis=1, keepdims=True)
                dq_ref[rows, :] = jnp.zeros((blk, W), F32)

        kv, vv = k_ref[...], v_ref[...]

        def step(i, carry, masked):
            dk, dv = carry
            rows = pl.ds(pl.multiple_of(i * blk, blk), blk)
            qv, do = q_ref[rows, :], do_ref[rows, :]
            p = jnp.exp(_dot_nt(qv, kv) - lse_ref[rows, 0:1])
            if masked:
                p = jnp.where(_causal_mask(0, 0, blk), p, 0.0)
            ds = p * (_dot_nt(do, vv) - delta_s[rows, :])
            dq_ref[rows, :] += _dot(ds, kv)
            return dk + _dot_tn(ds, qv), dv + _dot_tn(p, do)

        zero = jnp.zeros((blk, W), F32)
        carry = step(j, (zero, zero), True)
        dk, dv = lax.fori_loop(j + 1, nq, lambda i, c: step(i, c, False), carry)
        dk_ref[...] = dk
        dv_ref[...] = dv

    full = lambda b, h, j: (b, h)
    kmap = lambda b, h, j: (b * nq + j, h)
    return pl.pallas_call(
        body, name=name, grid=(B, H, nq),
        in_specs=[pl.BlockSpec((S, W), full), pl.BlockSpec((blk, W), kmap), pl.BlockSpec((blk, W), kmap),
                  pl.BlockSpec((S, W), full), pl.BlockSpec((S, W), full),
                  pl.BlockSpec((S, W), lambda b, h, j: (b, off + h))],
        out_specs=[pl.BlockSpec((S, W), full), pl.BlockSpec((blk, W), kmap), pl.BlockSpec((blk, W), kmap)],
        out_shape=[_sds((T, P), F32)] * 3,
        scratch_shapes=[pltpu.VMEM((S, 1), F32)],
        compiler_params=_cparams(("parallel", "parallel", "arbitrary")),
    )(q, k, v, o, lse, dmix)


SSD_PAIRS = SSD_HEADS // 2
PAIRS_PER_GROUP = SSD_PAIRS // SSD_GROUPS
GN = SSD_GROUPS * SSD_STATE


def _log1p_small(e):
    return jnp.where(e < 1e-3, e * (1.0 - e * (0.5 - e / 3.0)), jnp.log(1.0 + e))


def _softplus(v):
    return jnp.maximum(v, 0.0) + _log1p_small(jnp.exp(-jnp.abs(v)))


def _ssd_decay(dt_raw, dtb, alog):
    L = dt_raw.shape[0]
    pre = dt_raw + dtb
    dt = _softplus(pre)
    a = -jnp.exp(alog)
    row = lax.broadcasted_iota(jnp.int32, (L, L), 0)
    col = lax.broadcasted_iota(jnp.int32, (L, L), 1)
    tri = row >= col
    cum = _dot_hi(tri.astype(F32), dt * a)
    return pre, dt, a, tri, cum, cum.T


def _col(m, h):
    return m[:, h:h + 1]


def _pair_sel(m, k, lo):
    return jnp.where(lo, _col(m, 2 * k), _col(m, 2 * k + 1))


def _ssd_specs(S):
    L = SSD_CHUNK
    nc = S // L
    return L, nc


def ssd_fwd(proj, xc, dtb, alog, dchan, normw, S, *, name):
    T = proj.shape[0]
    L, nc = _ssd_specs(S)
    B, W, N = T // S, SSD_WIDTH, SSD_STATE

    def body(xs_ref, bs_ref, cs_ref, dt_ref, z_ref, dtb_ref, alog_ref, dch_ref, nw_ref, y_ref, ys_ref, hin_ref, st):
        @pl.when(pl.program_id(1) == 0)
        def _():
            st[...] = jnp.zeros(st.shape, F32)

        hin_ref[...] = st[...]
        _, dt, a, tri, cum, cum_t = _ssd_decay(dt_ref[...], dtb_ref[...], alog_ref[...])
        e_cum = jnp.exp(cum)
        last = cum[L - 1:L, :]
        w_end = jnp.exp(last - cum)
        e_last = jnp.exp(last)
        lo = lax.broadcasted_iota(jnp.int32, (1, LANE), 1) < SSD_HEAD_DIM
        slo = lax.broadcasted_iota(jnp.int32, (LANE, 1), 0) < SSD_HEAD_DIM
        for g in range(SSD_GROUPS):
            bm = bs_ref[:, g * N:(g + 1) * N]
            cm = cs_ref[:, g * N:(g + 1) * N]
            gmat = _dot_nt(cm, bm)
            for kk in range(PAIRS_PER_GROUP):
                k = g * PAIRS_PER_GROUP + kk
                sl = slice(k * LANE, (k + 1) * LANE)
                xv = xs_ref[:, sl]
                xdt = xv * _pair_sel(dt, k, lo)
                yd = []
                for h in (2 * k, 2 * k + 1):
                    gam = jnp.exp(jnp.where(tri, _col(cum, h) - cum_t[h:h + 1, :], -jnp.inf))
                    yd.append(_dot(gmat * gam, xdt))
                hp = st[sl, :]
                y_off = _dot_nt(cm, hp) * _pair_sel(e_cum, k, lo)
                y_ref[:, sl] = jnp.where(lo, yd[0], yd[1]) + y_off + xv * dch_ref[:, sl]
                zmat = xdt * _pair_sel(w_end, k, lo)
                e_rows = jnp.where(slo, _col(e_last, 2 * k), _col(e_last, 2 * k + 1))
                st[sl, :] = hp * e_rows + _dot_tn(zmat, bm)
        y = y_ref[...]
        z = z_ref[...]
        yz = y * (z * _sigmoid(z))
        ys_ref[...] = _rms_tile(yz, nw_ref[...]).astype(ys_ref.dtype)

    r = lambda b, c: b * nc + c
    vec = lambda w: pl.BlockSpec((1, w), lambda b, c: (0, 0))
    return pl.pallas_call(
        body, name=name, grid=(B, nc),
        in_specs=[pl.BlockSpec((L, W), lambda b, c: (r(b, c), 0)),
                  pl.BlockSpec((L, GN), lambda b, c: (r(b, c), W // GN)),
                  pl.BlockSpec((L, GN), lambda b, c: (r(b, c), W // GN + 1)),
                  pl.BlockSpec((L, LANE), lambda b, c: (r(b, c), PDT0 // LANE)),
                  pl.BlockSpec((L, W), lambda b, c: (r(b, c), PZ0 // W)),
                  vec(LANE), vec(LANE), vec(W), vec(W)],
        out_specs=[pl.BlockSpec((L, W), lambda b, c: (r(b, c), 0)), pl.BlockSpec((L, W), lambda b, c: (r(b, c), 0)),
                   pl.BlockSpec((W, N), lambda b, c: (r(b, c), 0))],
        out_shape=[_sds((T, W), F32), _sds((T, W), MXU_DTYPE), _sds((T // L * W, N), F32)],
        scratch_shapes=[pltpu.VMEM((W, N), F32)],
        compiler_params=_cparams(("parallel", "arbitrary")),
    )(xc, xc, xc, proj, proj, dtb, alog, dchan, normw)


def ssd_bwd(proj, xc, ypre, hin, dmix, dtb, alog, dchan, normw, S, *, name):
    T = proj.shape[0]
    L, nc = _ssd_specs(S)
    B, W, N = T // S, SSD_WIDTH, SSD_STATE

    def body(xs_ref, bs_ref, cs_ref, dt_ref, z_ref, y_ref, hin_ref, dys_ref, dtb_ref, alog_ref, dch_ref, nw_ref,
             dxc_ref, ddt_ref, dz_ref, sm_ref, dnw_ref, dst):
        step = pl.program_id(0) * nc + pl.program_id(1)

        @pl.when(pl.program_id(1) == 0)
        def _():
            dst[...] = jnp.zeros(dst.shape, F32)

        pre, dt, a, tri, cum, cum_t = _ssd_decay(dt_ref[...], dtb_ref[...], alog_ref[...])
        e_cum = jnp.exp(cum)
        last = cum[L - 1:L, :]
        w_end = jnp.exp(last - cum)
        e_last = jnp.exp(last)
        lane = lax.broadcasted_iota(jnp.int32, (1, LANE), 1)
        sub = lax.broadcasted_iota(jnp.int32, (LANE, 1), 0)
        lo, slo = lane < SSD_HEAD_DIM, sub < SSD_HEAD_DIM
        is_last_row = sub == L - 1

        y, z, nw = y_ref[...], z_ref[...], nw_ref[...]
        sg = _sigmoid(z)
        gate = z * sg
        dyz, dnw = _rms_bwd_tile(y * gate, nw, dys_ref[...])
        dy_all = dyz * gate
        dz_ref[...] = (dyz * y * (sg * (1.0 + z * (1.0 - sg)))).astype(dz_ref.dtype)

        d_cum = jnp.zeros((L, LANE), F32)
        d_cum_t = jnp.zeros((LANE, L), F32)
        d_dt = jnp.zeros((L, LANE), F32)
        d_dskip = jnp.zeros((1, LANE), F32)
        for g in range(SSD_GROUPS):
            bm = bs_ref[:, g * N:(g + 1) * N]
            cm = cs_ref[:, g * N:(g + 1) * N]
            gmat = _dot_nt(cm, bm)
            d_g = jnp.zeros((L, L), F32)
            d_bm = jnp.zeros((L, N), F32)
            d_cm = jnp.zeros((L, N), F32)
            for kk in range(PAIRS_PER_GROUP):
                k = g * PAIRS_PER_GROUP + kk
                sl = slice(k * LANE, (k + 1) * LANE)
                xv = xs_ref[:, sl]
                dyv = dy_all[:, sl]
                dt_sel = _pair_sel(dt, k, lo)
                xdt = xv * dt_sel
                hp = hin_ref[sl, :]
                dh_out = dst[sl, :]
                e_sel = _pair_sel(e_cum, k, lo)
                w_sel = _pair_sel(w_end, k, lo)
                y_off = _dot_nt(cm, hp) * e_sel
                zmat = xdt * w_sel
                d_z = _dot_nt(bm, dh_out)
                d_bm = d_bm + _dot(zmat, dh_out)
                d_xdt = d_z * w_sel
                dw_full = d_z * zmat
                e_rows = jnp.where(slo, _col(e_last, 2 * k), _col(e_last, 2 * k + 1))
                hh = jnp.sum(dh_out * hp, axis=1, keepdims=True) * e_rows
                d_r = dyv * e_sel
                d_cm = d_cm + _dot(d_r, hp)
                dst[sl, :] = dh_out * e_rows + _dot_tn(d_r, cm)
                dyoff_full = dyv * y_off
                for j, h in enumerate((2 * k, 2 * k + 1)):
                    mine = lo if j == 0 else jnp.logical_not(lo)
                    smine = slo if j == 0 else jnp.logical_not(slo)
                    hot = lane == h
                    dyh = jnp.where(mine, dyv, 0.0)
                    gam = jnp.exp(jnp.where(tri, _col(cum, h) - cum_t[h:h + 1, :], -jnp.inf))
                    mx = gmat * gam
                    d_xdt = d_xdt + _dot_tn(mx, dyh)
                    d_mx = jnp.where(tri, _dot_nt(dyh, xdt), 0.0)
                    d_g = d_g + d_mx * gam
                    d_seg = d_mx * mx
                    row_l = (jnp.sum(d_seg, axis=1, keepdims=True)
                             + jnp.sum(jnp.where(mine, dyoff_full - dw_full, 0.0), axis=1, keepdims=True))
                    at_end = (jnp.sum(jnp.where(mine, dw_full, 0.0), keepdims=True)
                              + jnp.sum(jnp.where(smine, hh, 0.0), keepdims=True))
                    d_cum = d_cum + jnp.where(hot, row_l + jnp.where(is_last_row, at_end, 0.0), 0.0)
                    d_cum_t = d_cum_t - jnp.where(sub == h, jnp.sum(d_seg, axis=0, keepdims=True), 0.0)
                    d_dskip = d_dskip + jnp.where(hot, jnp.sum(jnp.where(mine, dyv * xv, 0.0), keepdims=True), 0.0)
                for j, h in enumerate((2 * k, 2 * k + 1)):
                    mine = lo if j == 0 else jnp.logical_not(lo)
                    d_dt = d_dt + jnp.where(lane == h, jnp.sum(jnp.where(mine, d_xdt * xv, 0.0), axis=1, keepdims=True), 0.0)
                dxc_ref[:, sl] = d_xdt * dt_sel + dyv * dch_ref[:, sl]
            dxc_ref[:, W + g * N:W + (g + 1) * N] = d_bm + _dot_tn(d_g, cm)
            dxc_ref[:, W + GN + g * N:W + GN + (g + 1) * N] = d_cm + _dot(d_g, bm)

        d_cum = d_cum + d_cum_t.T
        d_da = _dot_hi(jnp.logical_not(tri).astype(F32) + (lax.broadcasted_iota(jnp.int32, (L, L), 0)
                                                              == lax.broadcasted_iota(jnp.int32, (L, L), 1)).astype(F32), d_cum)
        d_dt = d_dt + d_da * a
        heads = lane < SSD_HEADS
        d_pre = jnp.where(heads, d_dt * _sigmoid(pre), 0.0)
        ddt_ref[...] = d_pre.astype(ddt_ref.dtype)
        d_alog = jnp.sum(d_da * dt, axis=0, keepdims=True) * a
        part = jnp.concatenate([jnp.where(heads, d_alog, 0.0), jnp.sum(d_pre, axis=0, keepdims=True), d_dskip,
                                jnp.zeros((5, LANE), F32)], axis=0)

        @pl.when(step == 0)
        def _():
            sm_ref[...] = part
            dnw_ref[...] = dnw

        @pl.when(step > 0)
        def _():
            sm_ref[...] += part
            dnw_ref[...] += dnw

    r = lambda b, c: b * nc + (nc - 1 - c)
    vec = lambda w: pl.BlockSpec((1, w), lambda b, c: (0, 0))
    blk = lambda w, j: pl.BlockSpec((L, w), lambda b, c: (r(b, c), j))
    return pl.pallas_call(
        body, name=name, grid=(B, nc),
        in_specs=[blk(W, 0), blk(GN, W // GN), blk(GN, W // GN + 1), blk(LANE, PDT0 // LANE), blk(W, PZ0 // W),
                  blk(W, 0), pl.BlockSpec((W, N), lambda b, c: (r(b, c), 0)), blk(W, 0),
                  vec(LANE), vec(LANE), vec(W), vec(W)],
        out_specs=[blk(SSD_CONV_CH, 0), blk(LANE, 0), blk(W, 0), pl.BlockSpec((8, LANE), lambda b, c: (0, 0)), vec(W)],
        out_shape=[_sds((T, SSD_CONV_CH), F32), _sds((T, LANE), MXU_DTYPE), _sds((T, W), MXU_DTYPE),
                   _sds((8, LANE), F32), _sds((1, W), F32)],
        scratch_shapes=[pltpu.VMEM((W, N), F32)],
        compiler_params=_cparams(("arbitrary", "arbitrary")),
    )(xc, xc, xc, proj, proj, ypre, hin, dmix, dtb, alog, dchan, normw)


def _adamw_math(w, g, m, v):
    m = ADAM_B1 * m + (1.0 - ADAM_B1) * g
    v = ADAM_B2 * v + (1.0 - ADAM_B2) * (g * g)
    m_hat = m / (1.0 - ADAM_B1 ** ADAM_STEP)
    v_hat = v / (1.0 - ADAM_B2 ** ADAM_STEP)
    delta = -ADAM_LR * (m_hat / (jnp.sqrt(v_hat) + ADAM_EPS) + ADAM_WD * w)
    return delta, m, v


def adamw(w, g_parts, m, v, *, name, tr=256):
    R, C = w.shape
    tr = _pick(R, (tr, 128, 64, 32, 16, 8))
    n = len(g_parts)

    def body(*refs):
        w_ref, m_ref, v_ref = refs[0], refs[1 + n], refs[2 + n]
        g_ref, d_ref, nm_ref, nv_ref = refs[3 + n:]
        g = refs[1][...]
        for p in refs[2:1 + n]:
            g = g + p[...]
        g_ref[...] = g
        d_ref[...], nm_ref[...], nv_ref[...] = _adamw_math(w_ref[...], g, m_ref[...], v_ref[...])

    spec = pl.BlockSpec((tr, C), lambda i: (i, 0))
    return pl.pallas_call(
        body, name=name, grid=(R // tr,), in_specs=[spec] * (3 + n), out_specs=[spec] * 4,
        out_shape=[_sds((R, C), F32)] * 4, compiler_params=_cparams(("parallel",)),
    )(w, *g_parts, m, v)


def _my_place():
    return lax.axis_index("x"), lax.axis_index("y"), lax.axis_index("c")


def _other_chips(x, y):
    return [(1 - x, y), (x, 1 - y), (1 - x, 1 - y)]


def relation_of(chip, me):
    d = chip ^ me
    return jnp.where(d == 2, 0, jnp.where(d == 1, 1, jnp.where(d == 3, 2, -1)))


def scatter_blocks(src, *, name):
    _, R, C = src.shape

    def body(src_ref, out_ref, send_sems, recv_sems):
        x, y, c = _my_place()
        peers = _other_chips(x, y)

        def copy(k):
            px, py = peers[k]
            return pltpu.make_async_remote_copy(
                src_ref=src_ref.at[2 * px + py], dst_ref=out_ref.at[k], send_sem=send_sems.at[k],
                recv_sem=recv_sems.at[k], device_id=(px, py, c), device_id_type=pl.DeviceIdType.MESH)

        copies = [copy(k) for k in range(3)]
        for cp in copies:
            cp.start()
        for cp in copies:
            cp.wait_recv()
        for cp in copies:
            cp.wait_send()

    return pl.pallas_call(
        body, name=name, in_specs=[pl.BlockSpec(memory_space=pl.ANY)], out_specs=pl.BlockSpec(memory_space=pl.ANY),
        out_shape=_sds((3, R, C), src.dtype),
        scratch_shapes=[pltpu.SemaphoreType.DMA((3,)), pltpu.SemaphoreType.DMA((3,))],
        compiler_params=pltpu.CompilerParams(has_side_effects=True),
    )(src)


D2D_CHUNKS = 8
ICI_CHUNKS = 4


def sibling_swap(src, *, name, nch=D2D_CHUNKS):
    R, C = src.shape
    ch = R // nch
    assert ch * nch == R

    def body(src_ref, out_ref, send_sems, recv_sems):
        x, y, c = _my_place()

        def copy(q):
            rows = pl.ds(q * ch, ch)
            return pltpu.make_async_remote_copy(
                src_ref=src_ref.at[rows], dst_ref=out_ref.at[rows], send_sem=send_sems.at[q],
                recv_sem=recv_sems.at[q], device_id=(x, y, 1 - c), device_id_type=pl.DeviceIdType.MESH)

        copies = [copy(q) for q in range(nch)]
        for cp in copies:
            cp.start()
        for cp in copies:
            cp.wait_recv()
        for cp in copies:
            cp.wait_send()

    return pl.pallas_call(
        body, name=name, in_specs=[pl.BlockSpec(memory_space=pl.ANY)], out_specs=pl.BlockSpec(memory_space=pl.ANY),
        out_shape=_sds((R, C), src.dtype),
        scratch_shapes=[pltpu.SemaphoreType.DMA((nch,)), pltpu.SemaphoreType.DMA((nch,))],
        compiler_params=pltpu.CompilerParams(has_side_effects=True),
    )(src)


def gather_shards(src, *, name, nch=ICI_CHUNKS):
    R, C = src.shape
    half = R // 2
    ch = half // nch
    assert 2 * nch * ch == R

    def body(src_ref, out_ref, ici_send, ici_recv, d2d_send, d2d_recv):
        x, y, c = _my_place()
        sibling = (x, y, 1 - c)
        peers = _other_chips(x, y)

        def rows(core, q):
            return pl.ds(core * half + q * ch, ch)

        def ici(k, q):
            px, py = peers[k]
            return pltpu.make_async_remote_copy(
                src_ref=src_ref.at[rows(c, q)], dst_ref=out_ref.at[k, rows(c, q)], send_sem=ici_send.at[k, q],
                recv_sem=ici_recv.at[k, q], device_id=(px, py, c), device_id_type=pl.DeviceIdType.MESH)

        def d2d(k, q, core):
            return pltpu.make_async_remote_copy(
                src_ref=out_ref.at[k, rows(core, q)], dst_ref=out_ref.at[k, rows(core, q)],
                send_sem=d2d_send.at[k, q], recv_sem=d2d_recv.at[k, q], device_id=sibling,
                device_id_type=pl.DeviceIdType.MESH)

        sends = [ici(k, q) for k in range(3) for q in range(nch)]
        for cp in sends:
            cp.start()
        passed = []
        for k in range(3):
            for q in range(nch):
                ici(k, q).wait_recv()
                passed.append(d2d(k, q, c))
                passed[-1].start()
        for k in range(3):
            for q in range(nch):
                d2d(k, q, 1 - c).wait_recv()
        for cp in sends + passed:
            cp.wait_send()

    return pl.pallas_call(
        body, name=name, in_specs=[pl.BlockSpec(memory_space=pl.ANY)], out_specs=pl.BlockSpec(memory_space=pl.ANY),
        out_shape=_sds((3, R, C), src.dtype),
        scratch_shapes=[pltpu.SemaphoreType.DMA((3, nch))] * 4,
        compiler_params=pltpu.CompilerParams(has_side_effects=True),
    )(src)


def all_sum_small(vec, *, name):
    R, C = vec.shape

    def body(v_ref, out_ref, buf, send_sems, recv_sems):
        x, y, c = _my_place()
        me = 4 * x + 2 * y + c
        buf[me] = v_ref[...]
        copies = []
        for k in range(1, N_DEV):
            px, py, pc = x ^ (k >> 2), y ^ ((k >> 1) & 1), c ^ (k & 1)
            copies.append(pltpu.make_async_remote_copy(
                src_ref=v_ref, dst_ref=buf.at[me], send_sem=send_sems.at[k - 1], recv_sem=recv_sems.at[k - 1],
                device_id=(px, py, pc), device_id_type=pl.DeviceIdType.MESH))
        for cp in copies:
            cp.start()
        for k in range(1, N_DEV):
            px, py, pc = x ^ (k >> 2), y ^ ((k >> 1) & 1), c ^ (k & 1)
            pltpu.make_async_remote_copy(
                src_ref=v_ref, dst_ref=buf.at[4 * px + 2 * py + pc], send_sem=send_sems.at[k - 1],
                recv_sem=recv_sems.at[k - 1], device_id=(px, py, pc), device_id_type=pl.DeviceIdType.MESH).wait_recv()
        for cp in copies:
            cp.wait_send()
        acc = buf[0]
        for d in range(1, N_DEV):
            acc = acc + buf[d]
        out_ref[...] = acc

    return pl.pallas_call(
        body, name=name, in_specs=[pl.BlockSpec(memory_space=pltpu.VMEM)], out_specs=pl.BlockSpec(memory_space=pltpu.VMEM),
        out_shape=_sds((R, C), F32),
        scratch_shapes=[pltpu.VMEM((N_DEV, R, C), F32), pltpu.SemaphoreType.DMA((N_DEV - 1,)),
                        pltpu.SemaphoreType.DMA((N_DEV - 1,))],
        compiler_params=pltpu.CompilerParams(has_side_effects=True, vmem_limit_bytes=VMEM_LIMIT),
    )(vec)


def sum_chips(own, others, *, name, tr=512):
    R, C = own.shape
    tr = _pick(R, (tr, 256, 128, 64, 32, 16))

    def body(o_ref, p_ref, s_ref):
        acc = o_ref[...].astype(F32)
        for k in range(3):
            acc = acc + p_ref[k].astype(F32)
        s_ref[...] = acc

    return pl.pallas_call(
        body, name=name, grid=(R // tr,),
        in_specs=[pl.BlockSpec((tr, C), lambda i: (i, 0)), pl.BlockSpec((3, tr, C), lambda i: (0, i, 0))],
        out_specs=pl.BlockSpec((tr, C), lambda i: (i, 0)), out_shape=_sds((R, C), F32),
        compiler_params=_cparams(("parallel",)),
    )(own, others)


WEIGHTS = ['attn_norm', 'w_in', 'ssd_conv_w', 'ssd_conv_b', 'ssd_dt_bias', 'ssd_a_log', 'ssd_d', 'ssd_norm', 'pool_w',
           'pool_scale', 'mla_q_norm', 'mla_w_uq', 'mla_kv_norm', 'mla_w_ukv', 'w_out', 'ffn_norm', 'ffn_w_up',
           'ffn_conv_w', 'ffn_conv_b', 'ffn_w_down', 'final_norm']
BIG = {'w_in': 2, 'mla_w_uq': 2, 'mla_w_ukv': 2, 'w_out': 1, 'ffn_w_up': 2, 'ffn_w_down': 1}
CONV_SHARDED = ('ssd_conv_w', 'ffn_conv_w')
PACK_COLS = 512
PACK_ROW_MULTIPLE = 1024


def _zeros_cols(w, n):
    return jnp.zeros((w.shape[0], n), w.dtype)


def _w_in_to_padded(w):
    return jnp.concatenate([w[:, 0:2560], w[:, 2576:3088], w[:, 3088:3472], w[:, 2560:2576], _zeros_cols(w, 112),
                            w[:, 3472:3728], _zeros_cols(w, 64), w[:, 3728:3760], _zeros_cols(w, 32 + 128)], axis=1)


def _w_in_from_padded(g):
    return jnp.concatenate([g[:, 0:2560], g[:, PDT0:PDT0 + SSD_HEADS], g[:, PU0:PU0 + POOL_WIDTH],
                            g[:, PCQ0:PCQ0 + MLA_Q_RANK], g[:, PCKV0:PCKV0 + MLA_KV_RANK],
                            g[:, PKPE0 + ROPE0:PKPE0 + ROPE0 + MLA_ROPE]], axis=1)


def _w_uq_to_padded(w):
    r = w.reshape(MLA_Q_RANK, MLA_HEADS, MLA_QK)
    return jnp.pad(r, ((0, 0), (0, 0), (0, HEAD_W - MLA_QK))).reshape(MLA_Q_RANK, MLA_PAD)


def _w_uq_from_padded(g):
    return g.reshape(MLA_Q_RANK, MLA_HEADS, HEAD_W)[:, :, :MLA_QK].reshape(MLA_Q_RANK, MLA_HEADS * MLA_QK)


def _w_ukv_to_padded(w):
    r = w.reshape(MLA_KV_RANK, MLA_HEADS, MLA_NOPE + MLA_V)
    pad = lambda t: jnp.pad(t, ((0, 0), (0, 0), (0, HEAD_W - t.shape[2]))).reshape(MLA_KV_RANK, MLA_PAD)
    return jnp.concatenate([pad(r[:, :, :MLA_NOPE]), pad(r[:, :, MLA_NOPE:])], axis=1)


def _w_ukv_from_padded(g):
    kk = g[:, :MLA_PAD].reshape(MLA_KV_RANK, MLA_HEADS, HEAD_W)[:, :, :MLA_NOPE]
    vv = g[:, MLA_PAD:].reshape(MLA_KV_RANK, MLA_HEADS, HEAD_W)[:, :, :MLA_V]
    return jnp.concatenate([kk, vv], axis=2).reshape(MLA_KV_RANK, MLA_HEADS * (MLA_NOPE + MLA_V))


def _w_out_to_padded(w):
    att = w[SSD_WIDTH + POOL_WIDTH:].reshape(MLA_HEADS, MLA_V, D_MODEL)
    att = jnp.pad(att, ((0, 0), (0, HEAD_W - MLA_V), (0, 0))).reshape(MLA_PAD, D_MODEL)
    return jnp.concatenate([w[:SSD_WIDTH + POOL_WIDTH], att], axis=0)


def _w_out_from_padded(g):
    att = g[SSD_WIDTH + POOL_WIDTH:].reshape(MLA_HEADS, HEAD_W, D_MODEL)[:, :MLA_V].reshape(MLA_WIDTH, D_MODEL)
    return jnp.concatenate([g[:SSD_WIDTH + POOL_WIDTH], att], axis=0)


def _pad_lanes(v, n=LANE):
    return jnp.pad(v.reshape(1, -1), ((0, 0), (0, n - v.size)))


def _pack_rows(parts, cols, dtype, row_multiple=16):
    flat = jnp.concatenate([p.astype(dtype).reshape(-1) for p in parts])
    rows = -(-flat.size // (cols * row_multiple)) * row_multiple
    return jnp.pad(flat, (0, rows * cols - flat.size)).reshape(rows, cols)


def _unpack_rows(packed, shapes):
    flat = packed.reshape(-1)
    out, at = [], 0
    for s in shapes:
        n = math.prod(s)
        out.append(flat[at:at + n].reshape(s))
        at += n
    return out


def _shard_shape(full_shape, axis):
    s = list(full_shape)
    s[axis] //= N_CHIPS
    return tuple(s)


def _take_shard(a, axis, j):
    n = a.shape[axis] // N_CHIPS
    return lax.slice_in_dim(a, j * n, (j + 1) * n, axis=axis)


def _layer_weights(full, small, l):
    w = {}
    w['w_in_p'] = _w_in_to_padded(full['w_in'][l])
    w['w_uq_p'] = _w_uq_to_padded(full['mla_w_uq'][l])
    w['w_ukv_p'] = _w_ukv_to_padded(full['mla_w_ukv'][l])
    w['w_out_p'] = _w_out_to_padded(full['w_out'][l])
    w['w_up'] = full['ffn_w_up'][l]
    w['w_down'] = full['ffn_w_down'][l]
    for k in ('attn_norm', 'ssd_conv_w', 'ssd_conv_b', 'ssd_norm', 'pool_w', 'pool_scale', 'mla_q_norm', 'mla_kv_norm',
              'ffn_norm', 'ffn_conv_w', 'ffn_conv_b'):
        w[k] = small[k][l]
    w['dtb'] = _pad_lanes(small['ssd_dt_bias'][l])
    w['alog'] = _pad_lanes(small['ssd_a_log'][l])
    w['dchan'] = jnp.repeat(small['ssd_d'][l], SSD_HEAD_DIM).reshape(1, SSD_WIDTH)
    w['ssd_norm'] = w['ssd_norm'].reshape(1, SSD_WIDTH)
    return w


def _layer_fwd(x, pos, invf, w, S, l):
    n = lambda s: f"{s}_l{l}"
    h1 = rmsnorm_fwd(x, w['attn_norm'], name=n("attn_norm"))
    proj = matmul(h1, w['w_in_p'], name=n("w_in"))
    xc = ssd_conv_fwd(proj, w['ssd_conv_w'], w['ssd_conv_b'], S, name=n("ssd_conv"))
    ypre, yssd, hin = ssd_fwd(proj, xc, w['dtb'], w['alog'], w['dchan'], w['ssd_norm'], S, name=n("ssd_scan"))
    ypool, pooled = pool_fwd(proj, w['pool_w'], w['pool_scale'], S, name=n("pool"))
    q, k, v, cqn, ckvn = mla_prep_fwd(proj, pos, invf, w['mla_q_norm'], w['w_uq_p'], w['mla_kv_norm'], w['w_ukv_p'],
                                      name=n("mla_prep"))
    o, lse = flash_fwd(q, k, v, S, name=n("attention"))
    mix = jnp.concatenate([yssd, ypool, o.astype(MXU_DTYPE)], axis=1)
    x2 = matmul(mix, w['w_out_p'], res=x, name=n("w_out"))
    h2 = rmsnorm_fwd(x2, w['ffn_norm'], name=n("ffn_norm"))
    up = matmul(h2, w['w_up'], name=n("ffn_up"))
    act = ffn_conv_gate_fwd(up, w['ffn_conv_w'], w['ffn_conv_b'], S, name=n("ffn_conv_gate"))
    x3 = matmul(act, w['w_down'], res=x2, name=n("ffn_down"))
    saved = dict(x=x, h1=h1, proj=proj, xc=xc, ypre=ypre, hin=hin, pooled=pooled, q=q, k=k, v=v, cqn=cqn, ckvn=ckvn,
                 o=o, lse=lse, mix=mix, x2=x2, h2=h2, up=up, act=act)
    return x3, saved


def _layer_bwd(dx3, pos, invf, w, s, S, l):
    n = lambda t: f"{t}_l{l}"
    g = {}
    dact = matmul(dx3, w['w_down'], nt=True, name=n("d_ffn_down"))
    g['ffn_w_down'] = matmul_tn(s['act'], dx3, name=n("g_ffn_down"))
    dgate, dval, st = ffn_conv_gate_bwd(s['up'], w['ffn_conv_w'], w['ffn_conv_b'], dact, S, name=n("d_ffn_conv_gate"))
    g['ffn_conv_w'], g['ffn_conv_b'] = st[:FFN_CONV], st[FFN_CONV]
    dup_g = conv_bwd_x(dgate, w['ffn_conv_w'][:, :D_FF], S, FFN_CONV, name=n("d_ffn_conv_g"))
    dup_v = conv_bwd_x(dval, w['ffn_conv_w'][:, D_FF:], S, FFN_CONV, name=n("d_ffn_conv_v"))
    dh2 = matmul(dup_g, w['w_up'], nt=True, kblock=0, name=n("d_ffn_up_g"))
    dh2 = matmul(dup_v, w['w_up'], nt=True, kblock=1, res=dh2, name=n("d_ffn_up_v"))
    g['ffn_w_up'] = jnp.concatenate([matmul_tn(s['h2'], dup_g, name=n("g_ffn_up_g")),
                                     matmul_tn(s['h2'], dup_v, name=n("g_ffn_up_v"))], axis=1)
    dx2, gn = rmsnorm_bwd(s['x2'], w['ffn_norm'], dh2, dx3, name=n("d_ffn_norm"))
    g['ffn_norm'] = gn[0]
    dmix = matmul(dx2, w['w_out_p'], nt=True, name=n("d_w_out"))
    g['w_out'] = _w_out_from_padded(matmul_tn(s['mix'], dx2, name=n("g_w_out")))
    dxc, ddt, dz, sm, gsn = ssd_bwd(s['proj'], s['xc'], s['ypre'], s['hin'], dmix, w['dtb'], w['alog'], w['dchan'],
                                    w['ssd_norm'], S, name=n("d_ssd_scan"))
    g['ssd_a_log'], g['ssd_dt_bias'], g['ssd_d'] = sm[0, :SSD_HEADS], sm[1, :SSD_HEADS], sm[2, :SSD_HEADS]
    g['ssd_norm'] = gsn[0]
    dpre, st = ssd_conv_bwd_pre(s['proj'], w['ssd_conv_w'], w['ssd_conv_b'], dxc, S, name=n("d_ssd_conv_act"))
    g['ssd_conv_w'], g['ssd_conv_b'] = st[:SSD_CONV], st[SSD_CONV]
    dxbc = conv_bwd_x(dpre, w['ssd_conv_w'], S, SSD_CONV, name=n("d_ssd_conv"))
    du, g['pool_w'], gps = pool_bwd(dmix, s['pooled'], w['pool_w'], w['pool_scale'], S, name=n("d_pool"))
    g['pool_scale'] = gps[0]
    dq, dk, dv = flash_bwd(s['q'], s['k'], s['v'], s['o'], s['lse'], dmix, S, name=n("d_attention"))
    dqp, dkvp, dcq, dckv, dkpe, gqn, gkn = mla_prep_bwd(s['proj'], pos, invf, w['mla_q_norm'], w['w_uq_p'],
                                                        w['mla_kv_norm'], w['w_ukv_p'], dq, dk, dv, name=n("d_mla_prep"))
    g['mla_q_norm'], g['mla_kv_norm'] = gqn[0], gkn[0]
    g['mla_w_uq'] = _w_uq_from_padded(matmul_tn(s['cqn'], dqp, name=n("g_w_uq")))
    g['mla_w_ukv'] = _w_ukv_from_padded(matmul_tn(s['ckvn'], dkvp, name=n("g_w_ukv")))
    dproj = jnp.concatenate([dz, dxbc, du, dcq, ddt, dckv, dkpe, jnp.zeros_like(dkpe)], axis=1)
    dh1 = matmul(dproj, w['w_in_p'], nt=True, name=n("d_w_in"))
    g['w_in'] = _w_in_from_padded(matmul_tn(s['h1'], dproj, name=n("g_w_in")))
    dx, gn = rmsnorm_bwd(s['x'], w['attn_norm'], dh1, dx2, name=n("d_attn_norm"))
    g['attn_norm'] = gn[0]
    return dx, g


def _local_step(x, positions, target, full, small, S):
    pos = positions.reshape(-1, 1).astype(F32)
    inv_freq = ROPE_THETA ** (-jnp.arange(0, MLA_ROPE, 2, dtype=F32) / MLA_ROPE)
    invf = jnp.concatenate([jnp.zeros((ROPE0,), F32), inv_freq, inv_freq,
                            jnp.zeros((HEAD_W - ROPE0 - MLA_ROPE,), F32)]).reshape(1, HEAD_W)
    ws = [_layer_weights(full, small, l) for l in range(DEPTH)]
    saved = []
    h = x
    for l in range(DEPTH):
        h, s = _layer_fwd(h, pos, invf, ws[l], S, l)
        saved.append(s)
    loss, dh, gfn = final_loss(h, small['final_norm'], target)
    layer_grads = [None] * DEPTH
    for l in reversed(range(DEPTH)):
        dh, layer_grads[l] = _layer_bwd(dh, pos, invf, ws[l], saved[l], S, l)
    return loss, dh, layer_grads, gfn[0]


def kernel(x, positions, attn_norm, w_in, ssd_conv_w, ssd_conv_b, ssd_dt_bias, ssd_a_log, ssd_d, ssd_norm, pool_w, pool_scale, mla_q_norm, mla_w_uq, mla_kv_norm, mla_w_ukv, w_out, ffn_norm, ffn_w_up, ffn_conv_w, ffn_conv_b, ffn_w_down, final_norm, loss_target, m_attn_norm, m_w_in, m_ssd_conv_w, m_ssd_conv_b, m_ssd_dt_bias, m_ssd_a_log, m_ssd_d, m_ssd_norm, m_pool_w, m_pool_scale, m_mla_q_norm, m_mla_w_uq, m_mla_kv_norm, m_mla_w_ukv, m_w_out, m_ffn_norm, m_ffn_w_up, m_ffn_conv_w, m_ffn_conv_b, m_ffn_w_down, m_final_norm, v_attn_norm, v_w_in, v_ssd_conv_w, v_ssd_conv_b, v_ssd_dt_bias, v_ssd_a_log, v_ssd_d, v_ssd_norm, v_pool_w, v_pool_scale, v_mla_q_norm, v_mla_w_uq, v_mla_kv_norm, v_mla_w_ukv, v_w_out, v_ffn_norm, v_ffn_w_up, v_ffn_conv_w, v_ffn_conv_b, v_ffn_w_down, v_final_norm):
    wv = dict(zip(WEIGHTS, (attn_norm, w_in, ssd_conv_w, ssd_conv_b, ssd_dt_bias, ssd_a_log, ssd_d, ssd_norm, pool_w,
                            pool_scale, mla_q_norm, mla_w_uq, mla_kv_norm, mla_w_ukv, w_out, ffn_norm, ffn_w_up,
                            ffn_conv_w, ffn_conv_b, ffn_w_down, final_norm)))
    mv = dict(zip(WEIGHTS, (m_attn_norm, m_w_in, m_ssd_conv_w, m_ssd_conv_b, m_ssd_dt_bias, m_ssd_a_log, m_ssd_d,
                            m_ssd_norm, m_pool_w, m_pool_scale, m_mla_q_norm, m_mla_w_uq, m_mla_kv_norm, m_mla_w_ukv,
                            m_w_out, m_ffn_norm, m_ffn_w_up, m_ffn_conv_w, m_ffn_conv_b, m_ffn_w_down, m_final_norm)))
    vv = dict(zip(WEIGHTS, (v_attn_norm, v_w_in, v_ssd_conv_w, v_ssd_conv_b, v_ssd_dt_bias, v_ssd_a_log, v_ssd_d,
                            v_ssd_norm, v_pool_w, v_pool_scale, v_mla_q_norm, v_mla_w_uq, v_mla_kv_norm, v_mla_w_ukv,
                            v_w_out, v_ffn_norm, v_ffn_w_up, v_ffn_conv_w, v_ffn_conv_b, v_ffn_w_down, v_final_norm)))
    Bl, S, D = x.shape
    chip = 2 * lax.axis_index("x") + lax.axis_index("y")
    core = lax.axis_index("c")

    big_names = list(BIG)
    packed = _pack_rows([wv[k] for k in big_names], PACK_COLS, MXU_DTYPE, PACK_ROW_MULTIPLE)
    others = gather_shards(packed, name="gather_weights")
    shard_shapes = [wv[k].shape for k in big_names]
    per_chip = []
    for j in range(N_CHIPS):
        r = relation_of(j, chip)
        block = jnp.where(r < 0, packed, jnp.where(r == 0, others[0], jnp.where(r == 1, others[1], others[2])))
        per_chip.append(_unpack_rows(block, shard_shapes))
    full = {k: jnp.concatenate([per_chip[j][i] for j in range(N_CHIPS)], axis=BIG[k]) for i, k in enumerate(big_names)}
    placed = []
    for k in CONV_SHARDED:
        sh = wv[k]
        whole = jnp.zeros(sh.shape[:-1] + (sh.shape[-1] * N_CHIPS,), F32)
        whole = lax.dynamic_update_slice_in_dim(whole, sh, chip * sh.shape[-1], axis=sh.ndim - 1)
        placed.append(jnp.where(core == 1, whole, 0.0))
    conv_full = _unpack_rows(all_sum_small(_pack_rows(placed, LANE, F32), name="gather_conv_weights"),
                             [p.shape for p in placed])
    small = {k: wv[k] for k in WEIGHTS if k not in BIG}
    small.update(dict(zip(CONV_SHARDED, conv_full)))

    loss, dx, layer_grads, g_final_norm = _local_step(x.reshape(Bl * S, D), positions, loss_target.reshape(Bl * S, D),
                                                      full, small, S)
    small_names = [k for k in WEIGHTS if k not in BIG]
    grads = {k: jnp.stack([layer_grads[l][k] for l in range(DEPTH)]) for k in small_names if k != 'final_norm'}
    grads['final_norm'] = g_final_norm

    send = jnp.stack([_pack_rows([_take_shard(layer_grads[l][k], BIG[k] - 1, j) for k in big_names for l in range(DEPTH)],
                                 PACK_COLS, MXU_DTYPE, PACK_ROW_MULTIPLE) for j in range(N_CHIPS)])
    others = scatter_blocks(send, name="scatter_grads")
    mine = sum_chips(lax.dynamic_index_in_dim(send, chip, 0, keepdims=False), others, name="sum_chips")
    theirs = sibling_swap(mine, name="swap_core_sums")
    parts = [_unpack_rows(mine, shard_shapes), _unpack_rows(theirs, shard_shapes)]
    small_sum = all_sum_small(_pack_rows([grads[k] for k in small_names] + [loss[0, :1]], LANE, F32), name="sum_small_grads")
    summed = _unpack_rows(small_sum, [grads[k].shape for k in small_names] + [(1,)])
    loss_total = summed[-1].reshape(())
    g_small = dict(zip(small_names, summed[:-1]))
    for k in CONV_SHARDED:
        n = wv[k].shape[-1]
        g_small[k] = lax.dynamic_slice_in_dim(g_small[k], chip * n, n, axis=g_small[k].ndim - 1)

    out_g, out_d, out_m, out_v = {}, {}, {}, {}
    for i, k in enumerate(big_names):
        shp = wv[k].shape
        two_d = lambda a: a.reshape(-1, shp[-1])
        res = adamw(two_d(wv[k]), [two_d(parts[0][i]), two_d(parts[1][i])], two_d(mv[k]), two_d(vv[k]), name=f"adamw_{k}")
        out_g[k], out_d[k], out_m[k], out_v[k] = (r.reshape(shp) for r in res)
    shapes = [wv[k].shape for k in small_names]
    pk = lambda d: _pack_rows([d[k] for k in small_names], LANE, F32)
    res = adamw(pk(wv), [pk(g_small)], pk(mv), pk(vv), name="adamw_small")
    for dst, r in zip((out_g, out_d, out_m, out_v), res):
        dst.update(dict(zip(small_names, _unpack_rows(r, shapes))))
    return (loss_total, dx.reshape(Bl, S, D), *[out_g[k] for k in WEIGHTS], *[out_d[k] for k in WEIGHTS],
            *[out_m[k] for k in WEIGHTS], *[out_v[k] for k in WEIGHTS])
```

```python
import functools
import math
from typing import Callable, NamedTuple

import jax
import jax.numpy as jnp
from jax import lax
from jax.experimental import pallas as pl
from jax.experimental.pallas import tpu as pltpu

F32 = jnp.float32
BF16 = jnp.bfloat16
MXU_DTYPE = jnp.bfloat16
HI = lax.Precision.HIGHEST

D_MODEL = 1024
DEPTH = 2
EPS = 1e-6
SSD_HEADS = 16
SSD_HEAD_DIM = 64
SSD_WIDTH = 1024
SSD_GROUPS = 2
SSD_STATE = 128
SSD_CONV = 4
SSD_CHUNK = 128
SSD_CONV_CH = 1536
POOL_GROUPS = 4
POOL_GROUP_DIM = 128
POOL_WIDTH = 512
POOL_WINDOWS = (2, 4, 8, 16)
MLA_HEADS = 8
MLA_Q_RANK = 384
MLA_KV_RANK = 256
MLA_NOPE = 64
MLA_ROPE = 32
MLA_V = 64
MLA_QK = 96
MLA_WIDTH = 512
ROPE_THETA = 10000.0
MIX_WIDTH = 2048
IN_COLS = 3760
D_FF = 2816
FFN_CONV = 3
ADAM_LR = 0.001
ADAM_B1 = 0.9
ADAM_B2 = 0.999
ADAM_EPS = 1e-08
ADAM_WD = 0.01
ADAM_STEP = 10

LANE = 128
HALO = 8
POOL_HALO = 16
PZ0 = 0
PXBC0 = 1024
PU0 = 2560
PCQ0 = 3072
PDT0 = 3456
PCKV0 = 3584
PKPE0 = 3840
PROJ_W = 4096
HEAD_W = 128
MLA_PAD = MLA_HEADS * HEAD_W
MIXP = SSD_WIDTH + POOL_WIDTH + MLA_PAD
N_CHIPS = 4
N_DEV = 8
VMEM_LIMIT = 56 * 1024 * 1024


def _cparams(dims, vmem=None):
    return pltpu.CompilerParams(dimension_semantics=dims, vmem_limit_bytes=vmem or VMEM_LIMIT)


def _sds(shape, dtype):
    return jax.ShapeDtypeStruct(tuple(shape), dtype)


def _mx(v):
    return v.astype(MXU_DTYPE)


def _dot(a, b):
    return jnp.dot(_mx(a), _mx(b), preferred_element_type=F32)


def _dot_nt(a, b):
    return lax.dot_general(_mx(a), _mx(b), (((1,), (1,)), ((), ())), preferred_element_type=F32)


def _dot_tn(a, b):
    return lax.dot_general(_mx(a), _mx(b), (((0,), (0,)), ((), ())), preferred_element_type=F32)


def _dot_hi(a, b):
    return jnp.dot(a, b, preferred_element_type=F32, precision=HI)


def _sigmoid(v):
    return 1.0 / (1.0 + jnp.exp(-v))


def _pick(n, prefs):
    for p in prefs:
        if n % p == 0:
            return p
    return n


def matmul(a, b, *, res=None, out_dtype=F32, name, nt=False, kblock=0, tm=None, tn=None):
    M, K = a.shape
    N = b.shape[0] if nt else b.shape[1]
    assert (b.shape[1] % K == 0) if nt else (K == b.shape[0] and kblock == 0)
    tm = tm or _pick(M, (1024, 512, 256, 128))
    tn = tn or _pick(N, (512, 1408, 256, 128))

    def body(*refs):
        a_ref, b_ref = refs[:2]
        o_ref = refs[-1]
        out = (_dot_nt if nt else _dot)(a_ref[...], b_ref[...])
        if res is not None:
            out = out + refs[2][...]
        o_ref[...] = out.astype(out_dtype)

    b_spec = pl.BlockSpec((tn, K), lambda i, j: (j, kblock)) if nt else pl.BlockSpec((K, tn), lambda i, j: (0, j))
    in_specs = [pl.BlockSpec((tm, K), lambda i, j: (i, 0)), b_spec]
    args = [a, b]
    if res is not None:
        in_specs.append(pl.BlockSpec((tm, tn), lambda i, j: (i, j)))
        args.append(res)
    return pl.pallas_call(
        body, name=name, grid=(M // tm, N // tn), in_specs=in_specs,
        out_specs=pl.BlockSpec((tm, tn), lambda i, j: (i, j)), out_shape=_sds((M, N), out_dtype),
        compiler_params=_cparams(("parallel", "parallel")),
    )(*args)


def matmul_tn(a, g, *, name, tm=None, tn=None, tk=None):
    T, M = a.shape
    T2, N = g.shape
    assert T == T2
    tm = tm or _pick(M, (1408, 1280, 1024, 512, 384, 256, 128))
    tn = tn or _pick(N, (1024, 1408, 512, 256, 128))
    tk = tk or _pick(T, (512, 256, 128))
    nk = T // tk

    def body(a_ref, g_ref, o_ref, acc):
        k = pl.program_id(2)
        part = _dot_tn(a_ref[...], g_ref[...])

        @pl.when(k == 0)
        def _():
            acc[...] = part

        @pl.when(k > 0)
        def _():
            acc[...] += part

        @pl.when(k == nk - 1)
        def _():
            o_ref[...] = acc[...].astype(o_ref.dtype)

    return pl.pallas_call(
        body, name=name, grid=(M // tm, N // tn, nk),
        in_specs=[pl.BlockSpec((tk, tm), lambda i, j, k: (k, i)), pl.BlockSpec((tk, tn), lambda i, j, k: (k, j))],
        out_specs=pl.BlockSpec((tm, tn), lambda i, j, k: (i, j)), out_shape=_sds((M, N), MXU_DTYPE),
        scratch_shapes=[pltpu.VMEM((tm, tn), F32)],
        compiler_params=_cparams(("parallel", "parallel", "arbitrary")),
    )(a, g)


def rmsnorm_fwd(x, gamma, *, name, tm=512):
    T, D = x.shape
    tm = _pick(T, (tm, 256, 128))

    def body(x_ref, g_ref, o_ref):
        xv = x_ref[...]
        r = lax.rsqrt(jnp.mean(xv * xv, axis=-1, keepdims=True) + EPS)
        o_ref[...] = ((xv * r) * g_ref[...]).astype(MXU_DTYPE)

    return pl.pallas_call(
        body, name=name, grid=(T // tm,),
        in_specs=[pl.BlockSpec((tm, D), lambda i: (i, 0)), pl.BlockSpec((1, D), lambda i: (0, 0))],
        out_specs=pl.BlockSpec((tm, D), lambda i: (i, 0)), out_shape=_sds((T, D), MXU_DTYPE),
        compiler_params=_cparams(("parallel",)),
    )(x, gamma.reshape(1, D))


def _rms_bwd_tile(xv, gamma, dh):
    r = lax.rsqrt(jnp.mean(xv * xv, axis=-1, keepdims=True) + EPS)
    xh = xv * r
    dg = jnp.sum(dh * xh, axis=0, keepdims=True)
    dn = dh * gamma
    dx = r * (dn - xh * jnp.mean(dn * xh, axis=-1, keepdims=True))
    return dx, dg


def rmsnorm_bwd(x, gamma, dh, dres, *, name, tm=256):
    T, D = x.shape
    tm = _pick(T, (tm, 128))

    def body(x_ref, g_ref, dh_ref, dr_ref, dx_ref, dg_ref):
        dx, dg = _rms_bwd_tile(x_ref[...], g_ref[...], dh_ref[...].astype(F32))
        dx_ref[...] = dx + dr_ref[...]

        @pl.when(pl.program_id(0) == 0)
        def _():
            dg_ref[...] = dg

        @pl.when(pl.program_id(0) > 0)
        def _():
            dg_ref[...] += dg

    row = pl.BlockSpec((tm, D), lambda i: (i, 0))
    vec = pl.BlockSpec((1, D), lambda i: (0, 0))
    return pl.pallas_call(
        body, name=name, grid=(T // tm,), in_specs=[row, vec, row, row], out_specs=[row, vec],
        out_shape=[_sds((T, D), F32), _sds((1, D), F32)], compiler_params=_cparams(("arbitrary",)),
    )(x, gamma.reshape(1, D), dh, dres)


def final_loss(x, gamma, target, *, name="final_loss", tm=256):
    T, D = x.shape
    tm = _pick(T, (tm, 128))

    def body(x_ref, g_ref, t_ref, l_ref, dx_ref, dg_ref):
        xv = x_ref[...]
        gam = g_ref[...]
        r = lax.rsqrt(jnp.mean(xv * xv, axis=-1, keepdims=True) + EPS)
        y = (xv * r) * gam
        err = y - t_ref[...]
        part = 0.5 * jnp.sum(jnp.sum(err * err, axis=-1, keepdims=True) / D, axis=0, keepdims=True)
        dx, dg = _rms_bwd_tile(xv, gam, err / D)
        dx_ref[...] = dx

        @pl.when(pl.program_id(0) == 0)
        def _():
            dg_ref[...] = dg
            l_ref[...] = jnp.broadcast_to(part, l_ref.shape)

        @pl.when(pl.program_id(0) > 0)
        def _():
            dg_ref[...] += dg
            l_ref[...] += jnp.broadcast_to(part, l_ref.shape)

    row = pl.BlockSpec((tm, D), lambda i: (i, 0))
    vec = pl.BlockSpec((1, D), lambda i: (0, 0))
    return pl.pallas_call(
        body, name=name, grid=(T // tm,), in_specs=[row, vec, row],
        out_specs=[pl.BlockSpec((1, LANE), lambda i: (0, 0)), row, vec],
        out_shape=[_sds((1, LANE), F32), _sds((T, D), F32), _sds((1, D), F32)],
        compiler_params=_cparams(("arbitrary",)),
    )(x, gamma.reshape(1, D), target)


def _halo_prev(ts):
    return lambda i, j, off=0: (jnp.maximum(i * (ts // HALO) - 1, 0), j + off)


def _cat_prev(cur, halo, first):
    return jnp.concatenate([jnp.where(first, 0.0, halo), cur], axis=0)


def _cat_next(cur, halo, last):
    return jnp.concatenate([cur, jnp.where(last, 0.0, halo)], axis=0)


def _delayed(cat, r):
    if r == 0:
        return cat[HALO:]
    return pltpu.roll(cat, r, axis=0)[HALO:]


def _advanced(cat, r):
    n = cat.shape[0]
    if r == 0:
        return cat[:n - HALO]
    return pltpu.roll(cat, n - r, axis=0)[:n - HALO]


def _conv_pre(cat, w, b, K):
    acc = _delayed(cat, K - 1) * w[0:1, :] + b
    for k in range(1, K):
        acc = acc + _delayed(cat, K - 1 - k) * w[k:k + 1, :]
    return acc


def _pad_rows8(w):
    return jnp.pad(w, ((0, 8 - w.shape[0]), (0, 0)))


def ssd_conv_fwd(proj, w, b, S, *, name, ts=1024, tc=512):
    T = proj.shape[0]
    C, K = SSD_CONV_CH, SSD_CONV
    ts = _pick(S, (ts, 256, 128))
    off = PXBC0 // tc
    ns = S // ts

    def body(x_ref, h_ref, w_ref, b_ref, o_ref):
        first = (pl.program_id(0) % ns) == 0
        pre = _conv_pre(_cat_prev(x_ref[...], h_ref[...], first), w_ref[...], b_ref[...], K)
        o_ref[...] = pre * _sigmoid(pre)

    return pl.pallas_call(
        body, name=name, grid=(T // ts, C // tc),
        in_specs=[pl.BlockSpec((ts, tc), lambda i, j: (i, j + off)),
                  pl.BlockSpec((HALO, tc), functools.partial(_halo_prev(ts), off=off)),
                  pl.BlockSpec((8, tc), lambda i, j: (0, j)), pl.BlockSpec((1, tc), lambda i, j: (0, j))],
        out_specs=pl.BlockSpec((ts, tc), lambda i, j: (i, j)), out_shape=_sds((T, C), F32),
        compiler_params=_cparams(("parallel", "parallel")),
    )(proj, proj, _pad_rows8(w), b.reshape(1, C))


def ssd_conv_bwd_pre(proj, w, b, dxc, S, *, name, ts=1024, tc=512):
    T = proj.shape[0]
    C, K = SSD_CONV_CH, SSD_CONV
    ts = _pick(S, (ts, 256, 128))
    off = PXBC0 // tc
    ns = S // ts

    def body(x_ref, h_ref, w_ref, b_ref, d_ref, o_ref, acc_ref):
        i = pl.program_id(1)
        first = (i % ns) == 0
        cat = _cat_prev(x_ref[...], h_ref[...], first)
        pre = _conv_pre(cat, w_ref[...], b_ref[...], K)
        sg = _sigmoid(pre)
        dpre = d_ref[...] * (sg * (1.0 + pre * (1.0 - sg)))
        o_ref[...] = dpre
        rows = [jnp.sum(dpre * _delayed(cat, K - 1 - k), axis=0, keepdims=True) for k in range(K)]
        rows.append(jnp.sum(dpre, axis=0, keepdims=True))
        rows.append(jnp.zeros((8 - len(rows), dpre.shape[1]), F32))
        part = jnp.concatenate(rows, axis=0)

        @pl.when(i == 0)
        def _():
            acc_ref[...] = part

        @pl.when(i > 0)
        def _():
            acc_ref[...] += part

    hp = _halo_prev(ts)
    return pl.pallas_call(
        body, name=name, grid=(C // tc, T // ts),
        in_specs=[pl.BlockSpec((ts, tc), lambda j, i: (i, j + off)),
                  pl.BlockSpec((HALO, tc), lambda j, i: hp(i, j, off)),
                  pl.BlockSpec((8, tc), lambda j, i: (0, j)), pl.BlockSpec((1, tc), lambda j, i: (0, j)),
                  pl.BlockSpec((ts, tc), lambda j, i: (i, j))],
        out_specs=[pl.BlockSpec((ts, tc), lambda j, i: (i, j)), pl.BlockSpec((8, tc), lambda j, i: (0, j))],
        out_shape=[_sds((T, C), F32), _sds((8, C), F32)],
        compiler_params=_cparams(("parallel", "arbitrary")),
    )(proj, proj, _pad_rows8(w), b.reshape(1, C), dxc)


def conv_bwd_x(dpre, w, S, K, *, name, ts=512, tc=None):
    T, C = dpre.shape
    out_dtype = MXU_DTYPE
    ts = _pick(S, (ts, 256, 128))
    tc = tc or _pick(C, (1408, 512, 256, 128))
    ns = S // ts
    nblk = T // HALO

    def body(d_ref, h_ref, w_ref, o_ref):
        last = (pl.program_id(0) % ns) == ns - 1
        cat = _cat_next(d_ref[...], h_ref[...], last)
        wv = w_ref[...]
        acc = _advanced(cat, K - 1) * wv[0:1, :]
        for k in range(1, K):
            acc = acc + _advanced(cat, K - 1 - k) * wv[k:k + 1, :]
        o_ref[...] = acc.astype(out_dtype)

    return pl.pallas_call(
        body, name=name, grid=(T // ts, C // tc),
        in_specs=[pl.BlockSpec((ts, tc), lambda i, j: (i, j)),
                  pl.BlockSpec((HALO, tc), lambda i, j: (jnp.minimum((i + 1) * (ts // HALO), nblk - 1), j)),
                  pl.BlockSpec((8, tc), lambda i, j: (0, j))],
        out_specs=pl.BlockSpec((ts, tc), lambda i, j: (i, j)), out_shape=_sds((T, C), out_dtype),
        compiler_params=_cparams(("parallel", "parallel")),
    )(dpre, dpre, _pad_rows8(w))


def ffn_conv_gate_fwd(up, w, b, S, *, name, ts=512, tc=1408):
    T, C2 = up.shape
    C, K = C2 // 2, FFN_CONV
    ts = _pick(S, (ts, 256, 128))
    nj = C // tc
    ns = S // ts
    w8 = _pad_rows8(w)
    b2 = b.reshape(1, C2)

    def body(g_ref, gh_ref, v_ref, vh_ref, wg_ref, wv_ref, bg_ref, bv_ref, o_ref):
        first = (pl.program_id(0) % ns) == 0
        g = _conv_pre(_cat_prev(g_ref[...], gh_ref[...], first), wg_ref[...], bg_ref[...], K)
        v = _conv_pre(_cat_prev(v_ref[...], vh_ref[...], first), wv_ref[...], bv_ref[...], K)
        o_ref[...] = (g * _sigmoid(g) * v).astype(o_ref.dtype)

    hp = _halo_prev(ts)
    return pl.pallas_call(
        body, name=name, grid=(T // ts, nj),
        in_specs=[pl.BlockSpec((ts, tc), lambda i, j: (i, j)), pl.BlockSpec((HALO, tc), lambda i, j: hp(i, j)),
                  pl.BlockSpec((ts, tc), lambda i, j: (i, j + nj)), pl.BlockSpec((HALO, tc), lambda i, j: hp(i, j, nj)),
                  pl.BlockSpec((8, tc), lambda i, j: (0, j)), pl.BlockSpec((8, tc), lambda i, j: (0, j + nj)),
                  pl.BlockSpec((1, tc), lambda i, j: (0, j)), pl.BlockSpec((1, tc), lambda i, j: (0, j + nj))],
        out_specs=pl.BlockSpec((ts, tc), lambda i, j: (i, j)), out_shape=_sds((T, C), MXU_DTYPE),
        compiler_params=_cparams(("parallel", "parallel")),
    )(up, up, up, up, w8, w8, b2, b2)


def ffn_conv_gate_bwd(up, w, b, dact, S, *, name, ts=256, tc=1408):
    T, C2 = up.shape
    C, K = C2 // 2, FFN_CONV
    ts = _pick(S, (ts, 256, 128))
    nj = C // tc
    ns = S // ts
    w8 = _pad_rows8(w)
    b2 = b.reshape(1, C2)

    def stats(dpre, cat):
        rows = [jnp.sum(dpre * _delayed(cat, K - 1 - k), axis=0, keepdims=True) for k in range(K)]
        rows.append(jnp.sum(dpre, axis=0, keepdims=True))
        rows.append(jnp.zeros((8 - len(rows), dpre.shape[1]), F32))
        return jnp.concatenate(rows, axis=0)

    def body(g_ref, gh_ref, v_ref, vh_ref, wg_ref, wv_ref, bg_ref, bv_ref, d_ref, dg_ref, dv_ref, ag_ref, av_ref):
        i = pl.program_id(1)
        first = (i % ns) == 0
        gcat = _cat_prev(g_ref[...], gh_ref[...], first)
        vcat = _cat_prev(v_ref[...], vh_ref[...], first)
        g = _conv_pre(gcat, wg_ref[...], bg_ref[...], K)
        v = _conv_pre(vcat, wv_ref[...], bv_ref[...], K)
        d = d_ref[...]
        sg = _sigmoid(g)
        dg = d * v * (sg * (1.0 + g * (1.0 - sg)))
        dv = d * (g * sg)
        dg_ref[...] = dg
        dv_ref[...] = dv
        sgp, svp = stats(dg, gcat), stats(dv, vcat)

        @pl.when(i == 0)
        def _():
            ag_ref[...] = sgp
            av_ref[...] = svp

        @pl.when(i > 0)
        def _():
            ag_ref[...] += sgp
            av_ref[...] += svp

    hp = _halo_prev(ts)
    dg, dv, ag, av = pl.pallas_call(
        body, name=name, grid=(nj, T // ts),
        in_specs=[pl.BlockSpec((ts, tc), lambda j, i: (i, j)), pl.BlockSpec((HALO, tc), lambda j, i: hp(i, j)),
                  pl.BlockSpec((ts, tc), lambda j, i: (i, j + nj)), pl.BlockSpec((HALO, tc), lambda j, i: hp(i, j, nj)),
                  pl.BlockSpec((8, tc), lambda j, i: (0, j)), pl.BlockSpec((8, tc), lambda j, i: (0, j + nj)),
                  pl.BlockSpec((1, tc), lambda j, i: (0, j)), pl.BlockSpec((1, tc), lambda j, i: (0, j + nj)),
                  pl.BlockSpec((ts, tc), lambda j, i: (i, j))],
        out_specs=[pl.BlockSpec((ts, tc), lambda j, i: (i, j)), pl.BlockSpec((ts, tc), lambda j, i: (i, j)),
                   pl.BlockSpec((8, tc), lambda j, i: (0, j)), pl.BlockSpec((8, tc), lambda j, i: (0, j))],
        out_shape=[_sds((T, C), F32), _sds((T, C), F32), _sds((8, C), F32), _sds((8, C), F32)],
        compiler_params=_cparams(("parallel", "arbitrary")),
    )(up, up, up, up, w8, w8, b2, b2, dact)
    return dg, dv, jnp.concatenate([ag, av], axis=1)


def _pool_counts(pos, w):
    return jnp.minimum(pos + 1.0, float(w))


def pool_fwd(proj, pool_w, pool_scale, S, *, name, ts=512):
    T = proj.shape[0]
    C, G, GD, H = POOL_WIDTH, POOL_GROUPS, POOL_GROUP_DIM, POOL_HALO
    ts = _pick(S, (ts, 256, 128))
    ns = S // ts
    off = PU0 // C

    def body(u_ref, h_ref, w_ref, s_ref, y_ref, p_ref):
        i = pl.program_id(0)
        first = (i % ns) == 0
        cat = jnp.concatenate([jnp.where(first, 0.0, h_ref[...]), u_ref[...]], axis=0)
        pos = ((i % ns) * ts + lax.broadcasted_iota(jnp.int32, (ts, 1), 0)).astype(F32)
        sums = cat
        win = 1
        for g, wlen in enumerate(POOL_WINDOWS):
            while win < wlen:
                sums = sums + pltpu.roll(sums, win, axis=0)
                win *= 2
            sl = slice(g * GD, (g + 1) * GD)
            pooled = sums[H:, sl] / _pool_counts(pos, wlen) - cat[H:, sl]
            p_ref[:, sl] = pooled.astype(p_ref.dtype)
            y_ref[:, sl] = (_dot(pooled, w_ref[g]) * s_ref[:, sl]).astype(y_ref.dtype)

    return pl.pallas_call(
        body, name=name, grid=(T // ts,),
        in_specs=[pl.BlockSpec((ts, C), lambda i: (i, off)),
                  pl.BlockSpec((H, C), lambda i: (jnp.maximum(i * (ts // H) - 1, 0), off)),
                  pl.BlockSpec((G, GD, GD), lambda i: (0, 0, 0)), pl.BlockSpec((1, C), lambda i: (0, 0))],
        out_specs=[pl.BlockSpec((ts, C), lambda i: (i, 0)), pl.BlockSpec((ts, C), lambda i: (i, 0))],
        out_shape=[_sds((T, C), MXU_DTYPE), _sds((T, C), MXU_DTYPE)],
        compiler_params=_cparams(("parallel",)),
    )(proj, proj, _mx(pool_w), pool_scale.reshape(1, C))


def pool_bwd(dmix, pooled, pool_w, pool_scale, S, *, name, ts=512):
    T = dmix.shape[0]
    C, G, GD, H = POOL_WIDTH, POOL_GROUPS, POOL_GROUP_DIM, POOL_HALO
    ts = _pick(S, (ts, 256, 128))
    ns = S // ts
    off = SSD_WIDTH // C
    nblk = T // H

    def body(d_ref, dh_ref, p_ref, w_ref, s_ref, du_ref, dw_ref, ds_ref):
        i = pl.program_id(0)
        last = (i % ns) == ns - 1
        dcat = jnp.concatenate([d_ref[...], jnp.where(last, 0.0, dh_ref[...])], axis=0)
        n = ts + H
        pos = ((i % ns) * ts + lax.broadcasted_iota(jnp.int32, (n, 1), 0)).astype(F32)
        dws, dss = [], []
        for g, wlen in enumerate(POOL_WINDOWS):
            sl = slice(g * GD, (g + 1) * GD)
            wg = w_ref[g]
            pg = p_ref[:, sl]
            dys = dcat[:, sl] * s_ref[:, sl]
            dss.append(jnp.sum(dcat[:ts, sl] * _dot(pg, wg), axis=0, keepdims=True))
            dws.append(_dot_tn(pg, dys[:ts]))
            dp = _dot_nt(dys, wg)
            q = dp / _pool_counts(pos, wlen)
            win = 1
            while win < wlen:
                q = q + pltpu.roll(q, n - win, axis=0)
                win *= 2
            du_ref[:, sl] = (q[:ts] - dp[:ts]).astype(du_ref.dtype)
        dsp = jnp.concatenate(dss, axis=1)

        @pl.when(i == 0)
        def _():
            for g in range(G):
                dw_ref[g] = dws[g]
            ds_ref[...] = dsp

        @pl.when(i > 0)
        def _():
            for g in range(G):
                dw_ref[g] += dws[g]
            ds_ref[...] += dsp

    return pl.pallas_call(
        body, name=name, grid=(T // ts,),
        in_specs=[pl.BlockSpec((ts, C), lambda i: (i, off)),
                  pl.BlockSpec((H, C), lambda i: (jnp.minimum((i + 1) * (ts // H), nblk - 1), off)),
                  pl.BlockSpec((ts, C), lambda i: (i, 0)),
                  pl.BlockSpec((G, GD, GD), lambda i: (0, 0, 0)), pl.BlockSpec((1, C), lambda i: (0, 0))],
        out_specs=[pl.BlockSpec((ts, C), lambda i: (i, 0)), pl.BlockSpec((G, GD, GD), lambda i: (0, 0, 0)),
                   pl.BlockSpec((1, C), lambda i: (0, 0))],
        out_shape=[_sds((T, C), MXU_DTYPE), _sds((G, GD, GD), F32), _sds((1, C), F32)],
        compiler_params=_cparams(("arbitrary",)),
    )(dmix, dmix, pooled, _mx(pool_w), pool_scale.reshape(1, C))


ROPE0 = MLA_NOPE
ROPE_HALF = MLA_ROPE // 2


def _rope_tables(pos, invf):
    lane = lax.broadcasted_iota(jnp.int32, (1, HEAD_W), 1)
    ang = pos * invf
    cs, sn = jnp.cos(ang), jnp.sin(ang)
    in_a = (lane >= ROPE0) & (lane < ROPE0 + ROPE_HALF)
    in_b = (lane >= ROPE0 + ROPE_HALF) & (lane < ROPE0 + MLA_ROPE)
    return jnp.where(in_a | in_b, cs, 1.0), jnp.where(in_a, -sn, 0.0), jnp.where(in_b, sn, 0.0), in_a | in_b


def _rope(v, cosf, sin_a, sin_b):
    return (v * cosf + pltpu.roll(v, HEAD_W - ROPE_HALF, axis=1) * sin_a + pltpu.roll(v, ROPE_HALF, axis=1) * sin_b)


def _unrope(d, cosf, sin_a, sin_b):
    return (d * cosf + pltpu.roll(d * sin_a, ROPE_HALF, axis=1) + pltpu.roll(d * sin_b, HEAD_W - ROPE_HALF, axis=1))


def _rms_tile(xv, gamma):
    return (xv * lax.rsqrt(jnp.mean(xv * xv, axis=-1, keepdims=True) + EPS)) * gamma


def mla_prep_fwd(proj, pos, invf, q_norm, w_uq_p, kv_norm, w_ukv_p, *, name, tm=256):
    T = proj.shape[0]
    tm = _pick(T, (tm, 128))
    QR, KR, P = MLA_Q_RANK, MLA_KV_RANK, MLA_PAD

    def body(cq_ref, ckv_ref, kpe_ref, pos_ref, invf_ref, qn_ref, wq_ref, kn_ref, wkv_ref,
             q_ref, k_ref, v_ref, cqn_ref, ckvn_ref):
        cosf, sin_a, sin_b, _ = _rope_tables(pos_ref[...], invf_ref[...])
        cqn = _rms_tile(cq_ref[...], qn_ref[...]).astype(MXU_DTYPE)
        ckvn = _rms_tile(ckv_ref[...], kn_ref[...]).astype(MXU_DTYPE)
        cqn_ref[...] = cqn
        ckvn_ref[...] = ckvn
        qp = _dot(cqn, wq_ref[...])
        kvp = _dot(ckvn, wkv_ref[...])
        kpe = _rope(kpe_ref[...], cosf, sin_a, sin_b)
        for h in range(MLA_HEADS):
            sl = slice(h * HEAD_W, (h + 1) * HEAD_W)
            q_ref[:, sl] = (_rope(qp[:, sl], cosf, sin_a, sin_b) * ATTN_SCALE).astype(q_ref.dtype)
            k_ref[:, sl] = (kvp[:, sl] + kpe).astype(k_ref.dtype)
            v_ref[:, sl] = kvp[:, P + h * HEAD_W:P + (h + 1) * HEAD_W].astype(v_ref.dtype)

    row = lambda w: pl.BlockSpec((tm, w), lambda i: (i, 0))
    full = lambda a, b: pl.BlockSpec((a, b), lambda i: (0, 0))
    return pl.pallas_call(
        body, name=name, grid=(T // tm,),
        in_specs=[pl.BlockSpec((tm, QR), lambda i: (i, PCQ0 // QR)), pl.BlockSpec((tm, KR), lambda i: (i, PCKV0 // KR)),
                  pl.BlockSpec((tm, LANE), lambda i: (i, PKPE0 // LANE)), row(1), full(1, LANE),
                  full(1, QR), full(QR, P), full(1, KR), full(KR, 2 * P)],
        out_specs=[row(P), row(P), row(P), row(QR), row(KR)],
        out_shape=[_sds((T, P), MXU_DTYPE)] * 3 + [_sds((T, QR), MXU_DTYPE), _sds((T, KR), MXU_DTYPE)],
        compiler_params=_cparams(("parallel",)),
    )(proj, proj, proj, pos, invf, q_norm.reshape(1, QR), w_uq_p, kv_norm.reshape(1, KR), w_ukv_p)


def mla_prep_bwd(proj, pos, invf, q_norm, w_uq_p, kv_norm, w_ukv_p, dq, dk, dv, *, name, tm=256):
    T = proj.shape[0]
    tm = _pick(T, (tm, 128))
    QR, KR, P = MLA_Q_RANK, MLA_KV_RANK, MLA_PAD

    def body(cq_ref, ckv_ref, pos_ref, invf_ref, qn_ref, wq_ref, kn_ref, wkv_ref, dq_ref, dk_ref, dv_ref,
             dqp_ref, dkvp_ref, dcq_ref, dckv_ref, dkpe_ref, dqn_ref, dkn_ref):
        cosf, sin_a, sin_b, rot = _rope_tables(pos_ref[...], invf_ref[...])
        dkpe = jnp.zeros((tm, HEAD_W), F32)
        for h in range(MLA_HEADS):
            sl = slice(h * HEAD_W, (h + 1) * HEAD_W)
            dqp_ref[:, sl] = _unrope(dq_ref[:, sl] * ATTN_SCALE, cosf, sin_a, sin_b).astype(dqp_ref.dtype)
            dkh = dk_ref[:, sl]
            dkpe = dkpe + dkh
            dkvp_ref[:, sl] = dkh.astype(dkvp_ref.dtype)
            dkvp_ref[:, P + h * HEAD_W:P + (h + 1) * HEAD_W] = dv_ref[:, sl].astype(dkvp_ref.dtype)
        dkpe_ref[...] = jnp.where(rot, _unrope(dkpe, cosf, sin_a, sin_b), 0.0).astype(dkpe_ref.dtype)
        dcq, dqn = _rms_bwd_tile(cq_ref[...], qn_ref[...], _dot_nt(dqp_ref[...], wq_ref[...]))
        dckv, dkn = _rms_bwd_tile(ckv_ref[...], kn_ref[...], _dot_nt(dkvp_ref[...], wkv_ref[...]))
        dcq_ref[...] = dcq.astype(dcq_ref.dtype)
        dckv_ref[...] = dckv.astype(dckv_ref.dtype)

        @pl.when(pl.program_id(0) == 0)
        def _():
            dqn_ref[...] = dqn
            dkn_ref[...] = dkn

        @pl.when(pl.program_id(0) > 0)
        def _():
            dqn_ref[...] += dqn
            dkn_ref[...] += dkn

    row = lambda w: pl.BlockSpec((tm, w), lambda i: (i, 0))
    full = lambda a, b: pl.BlockSpec((a, b), lambda i: (0, 0))
    return pl.pallas_call(
        body, name=name, grid=(T // tm,),
        in_specs=[pl.BlockSpec((tm, QR), lambda i: (i, PCQ0 // QR)), pl.BlockSpec((tm, KR), lambda i: (i, PCKV0 // KR)),
                  row(1), full(1, LANE), full(1, QR), full(QR, P), full(1, KR), full(KR, 2 * P), row(P), row(P), row(P)],
        out_specs=[row(P), row(2 * P), row(QR), row(KR), row(LANE), full(1, QR), full(1, KR)],
        out_shape=[_sds((T, P), MXU_DTYPE), _sds((T, 2 * P), MXU_DTYPE), _sds((T, QR), MXU_DTYPE),
                   _sds((T, KR), MXU_DTYPE), _sds((T, LANE), MXU_DTYPE), _sds((1, QR), F32), _sds((1, KR), F32)],
        compiler_params=_cparams(("arbitrary",)),
    )(proj, proj, pos, invf, q_norm.reshape(1, QR), w_uq_p, kv_norm.reshape(1, KR), w_ukv_p, dq, dk, dv)


ATTN_SCALE = 1.0 / math.sqrt(MLA_QK)


def _causal_mask(i, j, blk):
    row = lax.broadcasted_iota(jnp.int32, (blk, blk), 0)
    col = lax.broadcasted_iota(jnp.int32, (blk, blk), 1)
    return col <= row + (i - j) * blk


def _hosting(hosted, grid, n_in, n_out, n_scratch):
    if hosted is None:
        return (lambda body: body), (), [], [], []
    hi, ho = len(hosted.inputs), len(hosted.out_shapes)

    def wrap(body):
        def full(*refs):
            ins, rest = refs[:n_in + hi], refs[n_in + hi:]
            outs, scr = rest[:n_out + ho], rest[n_out + ho:]
            parts = ins[n_in:], outs[n_out:], scr[n_scratch:]
            ids = [pl.program_id(d) for d in range(len(grid))]
            step = ids[0]
            for d in range(1, len(grid)):
                step = step * grid[d] + ids[d]
            total = math.prod(grid)

            @pl.when(step == 0)
            def _():
                hosted.start(*parts)

            body(*ins[:n_in], *outs[:n_out], *scr[:n_scratch])

            @pl.when(step == total // 2)
            def _():
                hosted.relay(*parts)

            @pl.when(step == total - 1)
            def _():
                hosted.finish(*parts)

        return full

    hbm = pl.BlockSpec(memory_space=pl.ANY)
    return wrap, tuple(hosted.inputs), [hbm] * ho, list(hosted.out_shapes), list(hosted.sems)


def flash_fwd(q, k, v, S, *, name, blk=512, hosted=None):
    T, P = q.shape
    blk = _pick(S, (blk, 256, 128))
    B, nq, H, W = T // S, S // blk, MLA_HEADS, HEAD_W
    grid = (B, H, nq)
    wrap, h_in, h_ospecs, h_oshapes, h_scratch = _hosting(hosted, grid, 3, 2, 0)

    def body(q_ref, k_ref, v_ref, o_ref, lse_ref):
        i = pl.program_id(2)
        qv = q_ref[...]

        def online(j, carry, masked):
            m_prev, l_prev, acc = carry
            rows = pl.ds(pl.multiple_of(j * blk, blk), blk)
            s = _dot_nt(qv, k_ref[rows, :])
            if masked:
                s = jnp.where(_causal_mask(0, 0, blk), s, -jnp.inf)
            m_new = jnp.maximum(m_prev, jnp.max(s, axis=1, keepdims=True))
            p = jnp.exp(s - m_new)
            alpha = jnp.exp(m_prev - m_new)
            return (m_new, alpha * l_prev + jnp.sum(p, axis=1, keepdims=True), alpha * acc + _dot(p, v_ref[rows, :]))

        init = (jnp.full((blk, 1), -jnp.inf, F32), jnp.zeros((blk, 1), F32), jnp.zeros((blk, W), F32))
        carry = lax.fori_loop(0, i, lambda j, c: online(j, c, False), init)
        m, l, acc = online(i, carry, True)
        o_ref[...] = acc / l
        lse_ref[...] = jnp.broadcast_to(m + jnp.log(l), (blk, W))

    qmap = lambda b, h, i: (b * nq + i, h)
    kmap = lambda b, h, i: (b, h)
    hbm = pl.BlockSpec(memory_space=pl.ANY)
    return pl.pallas_call(
        wrap(body), name=name, grid=grid,
        in_specs=[pl.BlockSpec((blk, W), qmap), pl.BlockSpec((S, W), kmap), pl.BlockSpec((S, W), kmap)] + [hbm] * len(h_in),
        out_specs=[pl.BlockSpec((blk, W), qmap), pl.BlockSpec((blk, W), qmap)] + h_ospecs,
        out_shape=[_sds((T, P), F32), _sds((T, P), F32)] + h_oshapes,
        scratch_shapes=h_scratch,
        compiler_params=_cparams(("arbitrary",) * 3 if hosted else ("parallel", "parallel", "arbitrary")),
    )(q, k, v, *h_in)


def flash_bwd(q, k, v, o, lse, dmix, S, *, name, blk=512, hosted=None):
    T, P = q.shape
    blk = _pick(S, (blk, 256, 128))
    B, nq, H, W = T // S, S // blk, MLA_HEADS, HEAD_W
    off = (SSD_WIDTH + POOL_WIDTH) // W
    grid = (B, H, nq)
    wrap, h_in, h_ospecs, h_oshapes, h_scratch = _hosting(hosted, grid, 6, 3, 1)

    def body(q_ref, k_ref, v_ref, o_ref, lse_ref, do_ref, dq_ref, dk_ref, dv_ref, delta_s):
        j = pl.program_id(2)

        @pl.when(j == 0)
        def _():
            for i in range(nq):
                rows = slice(i * blk, (i + 1) * blk)
                delta_s[rows, :] = jnp.sum(do_ref[rows, :] * o_ref[rows, :], axis=1, keepdims=True)
                dq_ref[rows, :] = jnp.zeros((blk, W), F32)

        kv, vv = k_ref[...], v_ref[...]

        def step(i, carry, masked):
            dk, dv = carry
            rows = pl.ds(pl.multiple_of(i * blk, blk), blk)
            qv, do = q_ref[rows, :], do_ref[rows, :]
            p = jnp.exp(_dot_nt(qv, kv) - lse_ref[rows, 0:1])
            if masked:
                p = jnp.where(_causal_mask(0, 0, blk), p, 0.0)
            ds = p * (_dot_nt(do, vv) - delta_s[rows, :])
            dq_ref[rows, :] += _dot(ds, kv)
            return dk + _dot_tn(ds, qv), dv + _dot_tn(p, do)

        zero = jnp.zeros((blk, W), F32)
        carry = step(j, (zero, zero), True)
        dk, dv = lax.fori_loop(j + 1, nq, lambda i, c: step(i, c, False), carry)
        dk_ref[...] = dk
        dv_ref[...] = dv

    full = lambda b, h, j: (b, h)
    kmap = lambda b, h, j: (b * nq + j, h)
    hbm = pl.BlockSpec(memory_space=pl.ANY)
    return pl.pallas_call(
        wrap(body), name=name, grid=grid,
        in_specs=[pl.BlockSpec((S, W), full), pl.BlockSpec((blk, W), kmap), pl.BlockSpec((blk, W), kmap),
                  pl.BlockSpec((S, W), full), pl.BlockSpec((S, W), full),
                  pl.BlockSpec((S, W), lambda b, h, j: (b, off + h))] + [hbm] * len(h_in),
        out_specs=[pl.BlockSpec((S, W), full), pl.BlockSpec((blk, W), kmap), pl.BlockSpec((blk, W), kmap)] + h_ospecs,
        out_shape=[_sds((T, P), F32)] * 3 + h_oshapes,
        scratch_shapes=[pltpu.VMEM((S, 1), F32)] + h_scratch,
        compiler_params=_cparams(("arbitrary",) * 3 if hosted else ("parallel", "parallel", "arbitrary")),
    )(q, k, v, o, lse, dmix, *h_in)


SSD_PAIRS = SSD_HEADS // 2
PAIRS_PER_GROUP = SSD_PAIRS // SSD_GROUPS
GN = SSD_GROUPS * SSD_STATE


def _log1p_small(e):
    return jnp.where(e < 1e-3, e * (1.0 - e * (0.5 - e / 3.0)), jnp.log(1.0 + e))


def _softplus(v):
    return jnp.maximum(v, 0.0) + _log1p_small(jnp.exp(-jnp.abs(v)))


def _ssd_decay(dt_raw, dtb, alog):
    L = dt_raw.shape[0]
    pre = dt_raw + dtb
    dt = _softplus(pre)
    a = -jnp.exp(alog)
    row = lax.broadcasted_iota(jnp.int32, (L, L), 0)
    col = lax.broadcasted_iota(jnp.int32, (L, L), 1)
    tri = row >= col
    cum = _dot_hi(tri.astype(F32), dt * a)
    return pre, dt, a, tri, cum, cum.T


def _col(m, h):
    return m[:, h:h + 1]


def _pair_sel(m, k, lo):
    return jnp.where(lo, _col(m, 2 * k), _col(m, 2 * k + 1))


def _ssd_specs(S):
    L = SSD_CHUNK
    nc = S // L
    return L, nc


def ssd_fwd(proj, xc, dtb, alog, dchan, normw, S, *, name):
    T = proj.shape[0]
    L, nc = _ssd_specs(S)
    B, W, N = T // S, SSD_WIDTH, SSD_STATE

    def body(xs_ref, bs_ref, cs_ref, dt_ref, z_ref, dtb_ref, alog_ref, dch_ref, nw_ref, y_ref, ys_ref, hin_ref, st):
        @pl.when(pl.program_id(1) == 0)
        def _():
            st[...] = jnp.zeros(st.shape, F32)

        hin_ref[...] = st[...]
        _, dt, a, tri, cum, cum_t = _ssd_decay(dt_ref[...], dtb_ref[...], alog_ref[...])
        e_cum = jnp.exp(cum)
        last = cum[L - 1:L, :]
        w_end = jnp.exp(last - cum)
        e_last = jnp.exp(last)
        lo = lax.broadcasted_iota(jnp.int32, (1, LANE), 1) < SSD_HEAD_DIM
        slo = lax.broadcasted_iota(jnp.int32, (LANE, 1), 0) < SSD_HEAD_DIM
        for g in range(SSD_GROUPS):
            bm = bs_ref[:, g * N:(g + 1) * N]
            cm = cs_ref[:, g * N:(g + 1) * N]
            gmat = _dot_nt(cm, bm)
            for kk in range(PAIRS_PER_GROUP):
                k = g * PAIRS_PER_GROUP + kk
                sl = slice(k * LANE, (k + 1) * LANE)
                xv = xs_ref[:, sl]
                xdt = xv * _pair_sel(dt, k, lo)
                yd = []
                for h in (2 * k, 2 * k + 1):
                    gam = jnp.exp(jnp.where(tri, _col(cum, h) - cum_t[h:h + 1, :], -jnp.inf))
                    yd.append(_dot(gmat * gam, xdt))
                hp = st[sl, :]
                y_off = _dot_nt(cm, hp) * _pair_sel(e_cum, k, lo)
                y_ref[:, sl] = jnp.where(lo, yd[0], yd[1]) + y_off + xv * dch_ref[:, sl]
                zmat = xdt * _pair_sel(w_end, k, lo)
                e_rows = jnp.where(slo, _col(e_last, 2 * k), _col(e_last, 2 * k + 1))
                st[sl, :] = hp * e_rows + _dot_tn(zmat, bm)
        y = y_ref[...]
        z = z_ref[...]
        yz = y * (z * _sigmoid(z))
        ys_ref[...] = _rms_tile(yz, nw_ref[...]).astype(ys_ref.dtype)

    r = lambda b, c: b * nc + c
    vec = lambda w: pl.BlockSpec((1, w), lambda b, c: (0, 0))
    return pl.pallas_call(
        body, name=name, grid=(B, nc),
        in_specs=[pl.BlockSpec((L, W), lambda b, c: (r(b, c), 0)),
                  pl.BlockSpec((L, GN), lambda b, c: (r(b, c), W // GN)),
                  pl.BlockSpec((L, GN), lambda b, c: (r(b, c), W // GN + 1)),
                  pl.BlockSpec((L, LANE), lambda b, c: (r(b, c), PDT0 // LANE)),
                  pl.BlockSpec((L, W), lambda b, c: (r(b, c), PZ0 // W)),
                  vec(LANE), vec(LANE), vec(W), vec(W)],
        out_specs=[pl.BlockSpec((L, W), lambda b, c: (r(b, c), 0)), pl.BlockSpec((L, W), lambda b, c: (r(b, c), 0)),
                   pl.BlockSpec((W, N), lambda b, c: (r(b, c), 0))],
        out_shape=[_sds((T, W), F32), _sds((T, W), MXU_DTYPE), _sds((T // L * W, N), F32)],
        scratch_shapes=[pltpu.VMEM((W, N), F32)],
        compiler_params=_cparams(("parallel", "arbitrary")),
    )(xc, xc, xc, proj, proj, dtb, alog, dchan, normw)


def ssd_bwd(proj, xc, ypre, hin, dmix, dtb, alog, dchan, normw, S, *, name):
    T = proj.shape[0]
    L, nc = _ssd_specs(S)
    B, W, N = T // S, SSD_WIDTH, SSD_STATE

    def body(xs_ref, bs_ref, cs_ref, dt_ref, z_ref, y_ref, hin_ref, dys_ref, dtb_ref, alog_ref, dch_ref, nw_ref,
             dxc_ref, ddt_ref, dz_ref, sm_ref, dnw_ref, dst):
        step = pl.program_id(0) * nc + pl.program_id(1)

        @pl.when(pl.program_id(1) == 0)
        def _():
            dst[...] = jnp.zeros(dst.shape, F32)

        pre, dt, a, tri, cum, cum_t = _ssd_decay(dt_ref[...], dtb_ref[...], alog_ref[...])
        e_cum = jnp.exp(cum)
        last = cum[L - 1:L, :]
        w_end = jnp.exp(last - cum)
        e_last = jnp.exp(last)
        lane = lax.broadcasted_iota(jnp.int32, (1, LANE), 1)
        sub = lax.broadcasted_iota(jnp.int32, (LANE, 1), 0)
        lo, slo = lane < SSD_HEAD_DIM, sub < SSD_HEAD_DIM
        is_last_row = sub == L - 1

        y, z, nw = y_ref[...], z_ref[...], nw_ref[...]
        sg = _sigmoid(z)
        gate = z * sg
        dyz, dnw = _rms_bwd_tile(y * gate, nw, dys_ref[...])
        dy_all = dyz * gate
        dz_ref[...] = (dyz * y * (sg * (1.0 + z * (1.0 - sg)))).astype(dz_ref.dtype)

        d_cum = jnp.zeros((L, LANE), F32)
        d_cum_t = jnp.zeros((LANE, L), F32)
        d_dt = jnp.zeros((L, LANE), F32)
        d_dskip = jnp.zeros((1, LANE), F32)
        for g in range(SSD_GROUPS):
            bm = bs_ref[:, g * N:(g + 1) * N]
            cm = cs_ref[:, g * N:(g + 1) * N]
            gmat = _dot_nt(cm, bm)
            d_g = jnp.zeros((L, L), F32)
            d_bm = jnp.zeros((L, N), F32)
            d_cm = jnp.zeros((L, N), F32)
            for kk in range(PAIRS_PER_GROUP):
                k = g * PAIRS_PER_GROUP + kk
                sl = slice(k * LANE, (k + 1) * LANE)
                xv = xs_ref[:, sl]
                dyv = dy_all[:, sl]
                dt_sel = _pair_sel(dt, k, lo)
                xdt = xv * dt_sel
                hp = hin_ref[sl, :]
                dh_out = dst[sl, :]
                e_sel = _pair_sel(e_cum, k, lo)
                w_sel = _pair_sel(w_end, k, lo)
                y_off = _dot_nt(cm, hp) * e_sel
                zmat = xdt * w_sel
                d_z = _dot_nt(bm, dh_out)
                d_bm = d_bm + _dot(zmat, dh_out)
                d_xdt = d_z * w_sel
                dw_full = d_z * zmat
                e_rows = jnp.where(slo, _col(e_last, 2 * k), _col(e_last, 2 * k + 1))
                hh = jnp.sum(dh_out * hp, axis=1, keepdims=True) * e_rows
                d_r = dyv * e_sel
                d_cm = d_cm + _dot(d_r, hp)
                dst[sl, :] = dh_out * e_rows + _dot_tn(d_r, cm)
                dyoff_full = dyv * y_off
                for j, h in enumerate((2 * k, 2 * k + 1)):
                    mine = lo if j == 0 else jnp.logical_not(lo)
                    smine = slo if j == 0 else jnp.logical_not(slo)
                    hot = lane == h
                    dyh = jnp.where(mine, dyv, 0.0)
                    gam = jnp.exp(jnp.where(tri, _col(cum, h) - cum_t[h:h + 1, :], -jnp.inf))
                    mx = gmat * gam
                    d_xdt = d_xdt + _dot_tn(mx, dyh)
                    d_mx = jnp.where(tri, _dot_nt(dyh, xdt), 0.0)
                    d_g = d_g + d_mx * gam
                    d_seg = d_mx * mx
                    row_l = (jnp.sum(d_seg, axis=1, keepdims=True)
                             + jnp.sum(jnp.where(mine, dyoff_full - dw_full, 0.0), axis=1, keepdims=True))
                    at_end = (jnp.sum(jnp.where(mine, dw_full, 0.0), keepdims=True)
                              + jnp.sum(jnp.where(smine, hh, 0.0), keepdims=True))
                    d_cum = d_cum + jnp.where(hot, row_l + jnp.where(is_last_row, at_end, 0.0), 0.0)
                    d_cum_t = d_cum_t - jnp.where(sub == h, jnp.sum(d_seg, axis=0, keepdims=True), 0.0)
                    d_dskip = d_dskip + jnp.where(hot, jnp.sum(jnp.where(mine, dyv * xv, 0.0), keepdims=True), 0.0)
                for j, h in enumerate((2 * k, 2 * k + 1)):
                    mine = lo if j == 0 else jnp.logical_not(lo)
                    d_dt = d_dt + jnp.where(lane == h, jnp.sum(jnp.where(mine, d_xdt * xv, 0.0), axis=1, keepdims=True), 0.0)
                dxc_ref[:, sl] = d_xdt * dt_sel + dyv * dch_ref[:, sl]
            dxc_ref[:, W + g * N:W + (g + 1) * N] = d_bm + _dot_tn(d_g, cm)
            dxc_ref[:, W + GN + g * N:W + GN + (g + 1) * N] = d_cm + _dot(d_g, bm)

        d_cum = d_cum + d_cum_t.T
        d_da = _dot_hi(jnp.logical_not(tri).astype(F32) + (lax.broadcasted_iota(jnp.int32, (L, L), 0)
                                                              == lax.broadcasted_iota(jnp.int32, (L, L), 1)).astype(F32), d_cum)
        d_dt = d_dt + d_da * a
        heads = lane < SSD_HEADS
        d_pre = jnp.where(heads, d_dt * _sigmoid(pre), 0.0)
        ddt_ref[...] = d_pre.astype(ddt_ref.dtype)
        d_alog = jnp.sum(d_da * dt, axis=0, keepdims=True) * a
        part = jnp.concatenate([jnp.where(heads, d_alog, 0.0), jnp.sum(d_pre, axis=0, keepdims=True), d_dskip,
                                jnp.zeros((5, LANE), F32)], axis=0)

        @pl.when(step == 0)
        def _():
            sm_ref[...] = part
            dnw_ref[...] = dnw

        @pl.when(step > 0)
        def _():
            sm_ref[...] += part
            dnw_ref[...] += dnw

    r = lambda b, c: b * nc + (nc - 1 - c)
    vec = lambda w: pl.BlockSpec((1, w), lambda b, c: (0, 0))
    blk = lambda w, j: pl.BlockSpec((L, w), lambda b, c: (r(b, c), j))
    return pl.pallas_call(
        body, name=name, grid=(B, nc),
        in_specs=[blk(W, 0), blk(GN, W // GN), blk(GN, W // GN + 1), blk(LANE, PDT0 // LANE), blk(W, PZ0 // W),
                  blk(W, 0), pl.BlockSpec((W, N), lambda b, c: (r(b, c), 0)), blk(W, 0),
                  vec(LANE), vec(LANE), vec(W), vec(W)],
        out_specs=[blk(SSD_CONV_CH, 0), blk(LANE, 0), blk(W, 0), pl.BlockSpec((8, LANE), lambda b, c: (0, 0)), vec(W)],
        out_shape=[_sds((T, SSD_CONV_CH), F32), _sds((T, LANE), MXU_DTYPE), _sds((T, W), MXU_DTYPE),
                   _sds((8, LANE), F32), _sds((1, W), F32)],
        scratch_shapes=[pltpu.VMEM((W, N), F32)],
        compiler_params=_cparams(("arbitrary", "arbitrary")),
    )(xc, xc, xc, proj, proj, ypre, hin, dmix, dtb, alog, dchan, normw)


def _adamw_math(w, g, m, v):
    m = ADAM_B1 * m + (1.0 - ADAM_B1) * g
    v = ADAM_B2 * v + (1.0 - ADAM_B2) * (g * g)
    m_hat = m / (1.0 - ADAM_B1 ** ADAM_STEP)
    v_hat = v / (1.0 - ADAM_B2 ** ADAM_STEP)
    delta = -ADAM_LR * (m_hat / (jnp.sqrt(v_hat) + ADAM_EPS) + ADAM_WD * w)
    return delta, m, v


def adamw(w, g_parts, m, v, *, name, tr=256):
    R, C = w.shape
    tr = _pick(R, (tr, 128, 64, 32, 16, 8))
    n = len(g_parts)

    def body(*refs):
        w_ref, m_ref, v_ref = refs[0], refs[1 + n], refs[2 + n]
        g_ref, d_ref, nm_ref, nv_ref = refs[3 + n:]
        g = refs[1][...]
        for p in refs[2:1 + n]:
            g = g + p[...]
        g_ref[...] = g
        d_ref[...], nm_ref[...], nv_ref[...] = _adamw_math(w_ref[...], g, m_ref[...], v_ref[...])

    spec = pl.BlockSpec((tr, C), lambda i: (i, 0))
    return pl.pallas_call(
        body, name=name, grid=(R // tr,), in_specs=[spec] * (3 + n), out_specs=[spec] * 4,
        out_shape=[_sds((R, C), F32)] * 4, compiler_params=_cparams(("parallel",)),
    )(w, *g_parts, m, v)


def _my_place():
    return lax.axis_index("x"), lax.axis_index("y"), lax.axis_index("c")


def _other_chips(x, y):
    return [(1 - x, y), (x, 1 - y), (1 - x, 1 - y)]


def relation_of(chip, me):
    d = chip ^ me
    return jnp.where(d == 2, 0, jnp.where(d == 1, 1, jnp.where(d == 3, 2, -1)))


class Exchange(NamedTuple):
    inputs: tuple
    out_shapes: tuple
    sems: tuple
    start: Callable
    relay: Callable
    finish: Callable


def scatter_exchange(src):
    _, R, C = src.shape

    def copies(ins, outs, sems):
        x, y, c = _my_place()
        out = []
        for k, (px, py) in enumerate(_other_chips(x, y)):
            out.append(pltpu.make_async_remote_copy(
                src_ref=ins[0].at[2 * px + py], dst_ref=outs[0].at[k], send_sem=sems[0].at[k],
                recv_sem=sems[1].at[k], device_id=(px, py, c), device_id_type=pl.DeviceIdType.MESH))
        return out

    def start(ins, outs, sems):
        for cp in copies(ins, outs, sems):
            cp.start()

    def finish(ins, outs, sems):
        cps = copies(ins, outs, sems)
        for cp in cps:
            cp.wait_recv()
        for cp in cps:
            cp.wait_send()

    return Exchange((src,), (_sds((3, R, C), src.dtype),), (pltpu.SemaphoreType.DMA((3,)),) * 2,
                    start, lambda *a: None, finish)


def run_exchange(ex, *, name):
    n_in, n_out = len(ex.inputs), len(ex.out_shapes)

    def body(*refs):
        parts = refs[:n_in], refs[n_in:n_in + n_out], refs[n_in + n_out:]
        ex.start(*parts)
        ex.relay(*parts)
        ex.finish(*parts)

    hbm = pl.BlockSpec(memory_space=pl.ANY)
    return pl.pallas_call(
        body, name=name, in_specs=[hbm] * n_in, out_specs=[hbm] * n_out, out_shape=list(ex.out_shapes),
        scratch_shapes=list(ex.sems), compiler_params=pltpu.CompilerParams(has_side_effects=True),
    )(*ex.inputs)


D2D_CHUNKS = 8
ICI_CHUNKS = 4


def sibling_swap(src, *, name, nch=D2D_CHUNKS):
    R, C = src.shape
    ch = R // nch
    assert ch * nch == R

    def body(src_ref, out_ref, send_sems, recv_sems):
        x, y, c = _my_place()

        def copy(q):
            rows = pl.ds(q * ch, ch)
            return pltpu.make_async_remote_copy(
                src_ref=src_ref.at[rows], dst_ref=out_ref.at[rows], send_sem=send_sems.at[q],
                recv_sem=recv_sems.at[q], device_id=(x, y, 1 - c), device_id_type=pl.DeviceIdType.MESH)

        copies = [copy(q) for q in range(nch)]
        for cp in copies:
            cp.start()
        for cp in copies:
            cp.wait_recv()
        for cp in copies:
            cp.wait_send()

    return pl.pallas_call(
        body, name=name, in_specs=[pl.BlockSpec(memory_space=pl.ANY)], out_specs=pl.BlockSpec(memory_space=pl.ANY),
        out_shape=_sds((R, C), src.dtype),
        scratch_shapes=[pltpu.SemaphoreType.DMA((nch,)), pltpu.SemaphoreType.DMA((nch,))],
        compiler_params=pltpu.CompilerParams(has_side_effects=True),
    )(src)


def gather_exchange(src, nch=ICI_CHUNKS):
    R, C = src.shape
    half = R // 2
    ch = half // nch
    assert 2 * nch * ch == R
    pieces = [(k, q) for k in range(3) for q in range(nch)]

    def makers(ins, outs, sems):
        src_ref, out_ref = ins[0], outs[0]
        ici_send, ici_recv, d2d_send, d2d_recv = sems
        x, y, c = _my_place()
        peers = _other_chips(x, y)

        def rows(core, q):
            return pl.ds(core * half + q * ch, ch)

        def ici(k, q):
            px, py = peers[k]
            return pltpu.make_async_remote_copy(
                src_ref=src_ref.at[rows(c, q)], dst_ref=out_ref.at[k, rows(c, q)], send_sem=ici_send.at[k, q],
                recv_sem=ici_recv.at[k, q], device_id=(px, py, c), device_id_type=pl.DeviceIdType.MESH)

        def d2d(k, q, core):
            return pltpu.make_async_remote_copy(
                src_ref=out_ref.at[k, rows(core, q)], dst_ref=out_ref.at[k, rows(core, q)],
                send_sem=d2d_send.at[k, q], recv_sem=d2d_recv.at[k, q], device_id=(x, y, 1 - c),
                device_id_type=pl.DeviceIdType.MESH)

        return ici, d2d, c

    def start(*refs):
        ici, _, _ = makers(*refs)
        for k, q in pieces:
            ici(k, q).start()

    def relay(*refs):
        ici, d2d, c = makers(*refs)
        for k, q in pieces:
            ici(k, q).wait_recv()
            d2d(k, q, c).start()

    def finish(*refs):
        ici, d2d, c = makers(*refs)
        for k, q in pieces:
            d2d(k, q, 1 - c).wait_recv()
        for k, q in pieces:
            ici(k, q).wait_send()
            d2d(k, q, c).wait_send()

    return Exchange((src,), (_sds((3, R, C), src.dtype),), (pltpu.SemaphoreType.DMA((3, nch)),) * 4, start, relay, finish)


def all_sum_small(vec, *, name):
    R, C = vec.shape

    def body(v_ref, out_ref, buf, send_sems, recv_sems):
        x, y, c = _my_place()
        me = 4 * x + 2 * y + c
        buf[me] = v_ref[...]
        copies = []
        for k in range(1, N_DEV):
            px, py, pc = x ^ (k >> 2), y ^ ((k >> 1) & 1), c ^ (k & 1)
            copies.append(pltpu.make_async_remote_copy(
                src_ref=v_ref, dst_ref=buf.at[me], send_sem=send_sems.at[k - 1], recv_sem=recv_sems.at[k - 1],
                device_id=(px, py, pc), device_id_type=pl.DeviceIdType.MESH))
        for cp in copies:
            cp.start()
        for k in range(1, N_DEV):
            px, py, pc = x ^ (k >> 2), y ^ ((k >> 1) & 1), c ^ (k & 1)
            pltpu.make_async_remote_copy(
                src_ref=v_ref, dst_ref=buf.at[4 * px + 2 * py + pc], send_sem=send_sems.at[k - 1],
                recv_sem=recv_sems.at[k - 1], device_id=(px, py, pc), device_id_type=pl.DeviceIdType.MESH).wait_recv()
        for cp in copies:
            cp.wait_send()
        acc = buf[0]
        for d in range(1, N_DEV):
            acc = acc + buf[d]
        out_ref[...] = acc

    return pl.pallas_call(
        body, name=name, in_specs=[pl.BlockSpec(memory_space=pltpu.VMEM)], out_specs=pl.BlockSpec(memory_space=pltpu.VMEM),
        out_shape=_sds((R, C), F32),
        scratch_shapes=[pltpu.VMEM((N_DEV, R, C), F32), pltpu.SemaphoreType.DMA((N_DEV - 1,)),
                        pltpu.SemaphoreType.DMA((N_DEV - 1,))],
        compiler_params=pltpu.CompilerParams(has_side_effects=True, vmem_limit_bytes=VMEM_LIMIT),
    )(vec)


def sum_chips(own, others, *, name, tr=512):
    R, C = own.shape
    tr = _pick(R, (tr, 384, 256, 128, 64, 32, 16))

    def body(o_ref, p_ref, s_ref):
        acc = o_ref[...].astype(F32)
        for k in range(3):
            acc = acc + p_ref[k].astype(F32)
        s_ref[...] = acc

    return pl.pallas_call(
        body, name=name, grid=(R // tr,),
        in_specs=[pl.BlockSpec((tr, C), lambda i: (i, 0)), pl.BlockSpec((3, tr, C), lambda i: (0, i, 0))],
        out_specs=pl.BlockSpec((tr, C), lambda i: (i, 0)), out_shape=_sds((R, C), F32),
        compiler_params=_cparams(("parallel",)),
    )(own, others)


WEIGHTS = ['attn_norm', 'w_in', 'ssd_conv_w', 'ssd_conv_b', 'ssd_dt_bias', 'ssd_a_log', 'ssd_d', 'ssd_norm', 'pool_w',
           'pool_scale', 'mla_q_norm', 'mla_w_uq', 'mla_kv_norm', 'mla_w_ukv', 'w_out', 'ffn_norm', 'ffn_w_up',
           'ffn_conv_w', 'ffn_conv_b', 'ffn_w_down', 'final_norm']
BIG = {'w_in': 2, 'mla_w_uq': 2, 'mla_w_ukv': 2, 'w_out': 1, 'ffn_w_up': 2, 'ffn_w_down': 1}
CONV_SHARDED = ('ssd_conv_w', 'ffn_conv_w')
PACK_COLS = 512
PACK_ROW_MULTIPLE = 128


def _zeros_cols(w, n):
    return jnp.zeros((w.shape[0], n), w.dtype)


def _w_in_to_padded(w):
    return jnp.concatenate([w[:, 0:2560], w[:, 2576:3088], w[:, 3088:3472], w[:, 2560:2576], _zeros_cols(w, 112),
                            w[:, 3472:3728], _zeros_cols(w, 64), w[:, 3728:3760], _zeros_cols(w, 32 + 128)], axis=1)


def _w_in_from_padded(g):
    return jnp.concatenate([g[:, 0:2560], g[:, PDT0:PDT0 + SSD_HEADS], g[:, PU0:PU0 + POOL_WIDTH],
                            g[:, PCQ0:PCQ0 + MLA_Q_RANK], g[:, PCKV0:PCKV0 + MLA_KV_RANK],
                            g[:, PKPE0 + ROPE0:PKPE0 + ROPE0 + MLA_ROPE]], axis=1)


def _w_uq_to_padded(w):
    r = w.reshape(MLA_Q_RANK, MLA_HEADS, MLA_QK)
    return jnp.pad(r, ((0, 0), (0, 0), (0, HEAD_W - MLA_QK))).reshape(MLA_Q_RANK, MLA_PAD)


def _w_uq_from_padded(g):
    return g.reshape(MLA_Q_RANK, MLA_HEADS, HEAD_W)[:, :, :MLA_QK].reshape(MLA_Q_RANK, MLA_HEADS * MLA_QK)


def _w_ukv_to_padded(w):
    r = w.reshape(MLA_KV_RANK, MLA_HEADS, MLA_NOPE + MLA_V)
    pad = lambda t: jnp.pad(t, ((0, 0), (0, 0), (0, HEAD_W - t.shape[2]))).reshape(MLA_KV_RANK, MLA_PAD)
    return jnp.concatenate([pad(r[:, :, :MLA_NOPE]), pad(r[:, :, MLA_NOPE:])], axis=1)


def _w_ukv_from_padded(g):
    kk = g[:, :MLA_PAD].reshape(MLA_KV_RANK, MLA_HEADS, HEAD_W)[:, :, :MLA_NOPE]
    vv = g[:, MLA_PAD:].reshape(MLA_KV_RANK, MLA_HEADS, HEAD_W)[:, :, :MLA_V]
    return jnp.concatenate([kk, vv], axis=2).reshape(MLA_KV_RANK, MLA_HEADS * (MLA_NOPE + MLA_V))


def _w_out_to_padded(w):
    att = w[SSD_WIDTH + POOL_WIDTH:].reshape(MLA_HEADS, MLA_V, D_MODEL)
    att = jnp.pad(att, ((0, 0), (0, HEAD_W - MLA_V), (0, 0))).reshape(MLA_PAD, D_MODEL)
    return jnp.concatenate([w[:SSD_WIDTH + POOL_WIDTH], att], axis=0)


def _w_out_from_padded(g):
    att = g[SSD_WIDTH + POOL_WIDTH:].reshape(MLA_HEADS, HEAD_W, D_MODEL)[:, :MLA_V].reshape(MLA_WIDTH, D_MODEL)
    return jnp.concatenate([g[:SSD_WIDTH + POOL_WIDTH], att], axis=0)


def _pad_lanes(v, n=LANE):
    return jnp.pad(v.reshape(1, -1), ((0, 0), (0, n - v.size)))


def _pack_rows(parts, cols, dtype, row_multiple=16):
    flat = jnp.concatenate([p.astype(dtype).reshape(-1) for p in parts])
    rows = -(-flat.size // (cols * row_multiple)) * row_multiple
    return jnp.pad(flat, (0, rows * cols - flat.size)).reshape(rows, cols)


def _unpack_rows(packed, shapes):
    flat = packed.reshape(-1)
    out, at = [], 0
    for s in shapes:
        n = math.prod(s)
        out.append(flat[at:at + n].reshape(s))
        at += n
    return out


def _shard_shape(full_shape, axis):
    s = list(full_shape)
    s[axis] //= N_CHIPS
    return tuple(s)


def _take_shard(a, axis, j):
    n = a.shape[axis] // N_CHIPS
    return lax.slice_in_dim(a, j * n, (j + 1) * n, axis=axis)


def _layer_weights(full, small, l):
    w = {}
    w['w_in_p'] = _w_in_to_padded(full['w_in'])
    w['w_uq_p'] = _w_uq_to_padded(full['mla_w_uq'])
    w['w_ukv_p'] = _w_ukv_to_padded(full['mla_w_ukv'])
    w['w_out_p'] = _w_out_to_padded(full['w_out'])
    w['w_up'] = full['ffn_w_up']
    w['w_down'] = full['ffn_w_down']
    for k in ('attn_norm', 'ssd_conv_w', 'ssd_conv_b', 'ssd_norm', 'pool_w', 'pool_scale', 'mla_q_norm', 'mla_kv_norm',
              'ffn_norm', 'ffn_conv_w', 'ffn_conv_b'):
        w[k] = small[k][l]
    w['dtb'] = _pad_lanes(small['ssd_dt_bias'][l])
    w['alog'] = _pad_lanes(small['ssd_a_log'][l])
    w['dchan'] = jnp.repeat(small['ssd_d'][l], SSD_HEAD_DIM).reshape(1, SSD_WIDTH)
    w['ssd_norm'] = w['ssd_norm'].reshape(1, SSD_WIDTH)
    return w


def _layer_fwd(x, pos, invf, w, S, l, hosted=None):
    n = lambda s: f"{s}_l{l}"
    h1 = rmsnorm_fwd(x, w['attn_norm'], name=n("attn_norm"))
    proj = matmul(h1, w['w_in_p'], name=n("w_in"))
    xc = ssd_conv_fwd(proj, w['ssd_conv_w'], w['ssd_conv_b'], S, name=n("ssd_conv"))
    ypre, yssd, hin = ssd_fwd(proj, xc, w['dtb'], w['alog'], w['dchan'], w['ssd_norm'], S, name=n("ssd_scan"))
    ypool, pooled = pool_fwd(proj, w['pool_w'], w['pool_scale'], S, name=n("pool"))
    q, k, v, cqn, ckvn = mla_prep_fwd(proj, pos, invf, w['mla_q_norm'], w['w_uq_p'], w['mla_kv_norm'], w['w_ukv_p'],
                                      name=n("mla_prep"))
    o, lse, *exchanged = flash_fwd(q, k, v, S, name=n("attention"), hosted=hosted)
    mix = jnp.concatenate([yssd, ypool, o.astype(MXU_DTYPE)], axis=1)
    x2 = matmul(mix, w['w_out_p'], res=x, name=n("w_out"))
    h2 = rmsnorm_fwd(x2, w['ffn_norm'], name=n("ffn_norm"))
    up = matmul(h2, w['w_up'], name=n("ffn_up"))
    act = ffn_conv_gate_fwd(up, w['ffn_conv_w'], w['ffn_conv_b'], S, name=n("ffn_conv_gate"))
    x3 = matmul(act, w['w_down'], res=x2, name=n("ffn_down"))
    saved = dict(x=x, h1=h1, proj=proj, xc=xc, ypre=ypre, hin=hin, pooled=pooled, q=q, k=k, v=v, cqn=cqn, ckvn=ckvn,
                 o=o, lse=lse, mix=mix, x2=x2, h2=h2, up=up, act=act)
    return x3, saved, exchanged


def _layer_bwd(dx3, pos, invf, w, s, S, l, host=None):
    n = lambda t: f"{t}_l{l}"
    g = {}
    dact = matmul(dx3, w['w_down'], nt=True, name=n("d_ffn_down"))
    g['ffn_w_down'] = matmul_tn(s['act'], dx3, name=n("g_ffn_down"))
    dgate, dval, st = ffn_conv_gate_bwd(s['up'], w['ffn_conv_w'], w['ffn_conv_b'], dact, S, name=n("d_ffn_conv_gate"))
    g['ffn_conv_w'], g['ffn_conv_b'] = st[:FFN_CONV], st[FFN_CONV]
    dup_g = conv_bwd_x(dgate, w['ffn_conv_w'][:, :D_FF], S, FFN_CONV, name=n("d_ffn_conv_g"))
    dup_v = conv_bwd_x(dval, w['ffn_conv_w'][:, D_FF:], S, FFN_CONV, name=n("d_ffn_conv_v"))
    dh2 = matmul(dup_g, w['w_up'], nt=True, kblock=0, name=n("d_ffn_up_g"))
    dh2 = matmul(dup_v, w['w_up'], nt=True, kblock=1, res=dh2, name=n("d_ffn_up_v"))
    g['ffn_w_up'] = jnp.concatenate([matmul_tn(s['h2'], dup_g, name=n("g_ffn_up_g")),
                                     matmul_tn(s['h2'], dup_v, name=n("g_ffn_up_v"))], axis=1)
    dx2, gn = rmsnorm_bwd(s['x2'], w['ffn_norm'], dh2, dx3, name=n("d_ffn_norm"))
    g['ffn_norm'] = gn[0]
    dmix = matmul(dx2, w['w_out_p'], nt=True, name=n("d_w_out"))
    g['w_out'] = _w_out_from_padded(matmul_tn(s['mix'], dx2, name=n("g_w_out")))
    dxc, ddt, dz, sm, gsn = ssd_bwd(s['proj'], s['xc'], s['ypre'], s['hin'], dmix, w['dtb'], w['alog'], w['dchan'],
                                    w['ssd_norm'], S, name=n("d_ssd_scan"))
    g['ssd_a_log'], g['ssd_dt_bias'], g['ssd_d'] = sm[0, :SSD_HEADS], sm[1, :SSD_HEADS], sm[2, :SSD_HEADS]
    g['ssd_norm'] = gsn[0]
    dpre, st = ssd_conv_bwd_pre(s['proj'], w['ssd_conv_w'], w['ssd_conv_b'], dxc, S, name=n("d_ssd_conv_act"))
    g['ssd_conv_w'], g['ssd_conv_b'] = st[:SSD_CONV], st[SSD_CONV]
    dxbc = conv_bwd_x(dpre, w['ssd_conv_w'], S, SSD_CONV, name=n("d_ssd_conv"))
    du, g['pool_w'], gps = pool_bwd(dmix, s['pooled'], w['pool_w'], w['pool_scale'], S, name=n("d_pool"))
    g['pool_scale'] = gps[0]
    dq, dk, dv, *exchanged = flash_bwd(s['q'], s['k'], s['v'], s['o'], s['lse'], dmix, S, name=n("d_attention"),
                                       hosted=host(g) if host else None)
    dqp, dkvp, dcq, dckv, dkpe, gqn, gkn = mla_prep_bwd(s['proj'], pos, invf, w['mla_q_norm'], w['w_uq_p'],
                                                        w['mla_kv_norm'], w['w_ukv_p'], dq, dk, dv, name=n("d_mla_prep"))
    g['mla_q_norm'], g['mla_kv_norm'] = gqn[0], gkn[0]
    g['mla_w_uq'] = _w_uq_from_padded(matmul_tn(s['cqn'], dqp, name=n("g_w_uq")))
    g['mla_w_ukv'] = _w_ukv_from_padded(matmul_tn(s['ckvn'], dkvp, name=n("g_w_ukv")))
    dproj = jnp.concatenate([dz, dxbc, du, dcq, ddt, dckv, dkpe, jnp.zeros_like(dkpe)], axis=1)
    dh1 = matmul(dproj, w['w_in_p'], nt=True, name=n("d_w_in"))
    g['w_in'] = _w_in_from_padded(matmul_tn(s['h1'], dproj, name=n("g_w_in")))
    dx, gn = rmsnorm_bwd(s['x'], w['attn_norm'], dh1, dx2, name=n("d_attn_norm"))
    g['attn_norm'] = gn[0]
    return dx, g, exchanged


def _rope_inputs(positions):
    pos = positions.reshape(-1, 1).astype(F32)
    inv_freq = ROPE_THETA ** (-jnp.arange(0, MLA_ROPE, 2, dtype=F32) / MLA_ROPE)
    invf = jnp.concatenate([jnp.zeros((ROPE0,), F32), inv_freq, inv_freq,
                            jnp.zeros((HEAD_W - ROPE0 - MLA_ROPE,), F32)]).reshape(1, HEAD_W)
    return pos, invf


EARLY_GRADS = ('w_out', 'ffn_w_up', 'ffn_w_down')


def kernel(x, positions, attn_norm, w_in, ssd_conv_w, ssd_conv_b, ssd_dt_bias, ssd_a_log, ssd_d, ssd_norm, pool_w, pool_scale, mla_q_norm, mla_w_uq, mla_kv_norm, mla_w_ukv, w_out, ffn_norm, ffn_w_up, ffn_conv_w, ffn_conv_b, ffn_w_down, final_norm, loss_target, m_attn_norm, m_w_in, m_ssd_conv_w, m_ssd_conv_b, m_ssd_dt_bias, m_ssd_a_log, m_ssd_d, m_ssd_norm, m_pool_w, m_pool_scale, m_mla_q_norm, m_mla_w_uq, m_mla_kv_norm, m_mla_w_ukv, m_w_out, m_ffn_norm, m_ffn_w_up, m_ffn_conv_w, m_ffn_conv_b, m_ffn_w_down, m_final_norm, v_attn_norm, v_w_in, v_ssd_conv_w, v_ssd_conv_b, v_ssd_dt_bias, v_ssd_a_log, v_ssd_d, v_ssd_norm, v_pool_w, v_pool_scale, v_mla_q_norm, v_mla_w_uq, v_mla_kv_norm, v_mla_w_ukv, v_w_out, v_ffn_norm, v_ffn_w_up, v_ffn_conv_w, v_ffn_conv_b, v_ffn_w_down, v_final_norm):
    wv = dict(zip(WEIGHTS, (attn_norm, w_in, ssd_conv_w, ssd_conv_b, ssd_dt_bias, ssd_a_log, ssd_d, ssd_norm, pool_w,
                            pool_scale, mla_q_norm, mla_w_uq, mla_kv_norm, mla_w_ukv, w_out, ffn_norm, ffn_w_up,
                            ffn_conv_w, ffn_conv_b, ffn_w_down, final_norm)))
    mv = dict(zip(WEIGHTS, (m_attn_norm, m_w_in, m_ssd_conv_w, m_ssd_conv_b, m_ssd_dt_bias, m_ssd_a_log, m_ssd_d,
                            m_ssd_norm, m_pool_w, m_pool_scale, m_mla_q_norm, m_mla_w_uq, m_mla_kv_norm, m_mla_w_ukv,
                            m_w_out, m_ffn_norm, m_ffn_w_up, m_ffn_conv_w, m_ffn_conv_b, m_ffn_w_down, m_final_norm)))
    vv = dict(zip(WEIGHTS, (v_attn_norm, v_w_in, v_ssd_conv_w, v_ssd_conv_b, v_ssd_dt_bias, v_ssd_a_log, v_ssd_d,
                            v_ssd_norm, v_pool_w, v_pool_scale, v_mla_q_norm, v_mla_w_uq, v_mla_kv_norm, v_mla_w_ukv,
                            v_w_out, v_ffn_norm, v_ffn_w_up, v_ffn_conv_w, v_ffn_conv_b, v_ffn_w_down, v_final_norm)))
    Bl, S, D = x.shape
    chip = 2 * lax.axis_index("x") + lax.axis_index("y")
    core = lax.axis_index("c")

    big_names = list(BIG)
    layer_shard_shapes = [wv[k].shape[1:] for k in big_names]

    def pack_layer(l):
        return _pack_rows([wv[k][l] for k in big_names], PACK_COLS, MXU_DTYPE, PACK_ROW_MULTIPLE)

    def whole_weights(own, others):
        per_chip = []
        for j in range(N_CHIPS):
            r = relation_of(j, chip)
            block = jnp.where(r < 0, own, jnp.where(r == 0, others[0], jnp.where(r == 1, others[1], others[2])))
            per_chip.append(_unpack_rows(block, layer_shard_shapes))
        return {k: jnp.concatenate([per_chip[j][i] for j in range(N_CHIPS)], axis=BIG[k] - 1)
                for i, k in enumerate(big_names)}

    packed = [pack_layer(l) for l in range(DEPTH)]
    others0, = run_exchange(gather_exchange(packed[0]), name="gather_weights_l0")
    placed = []
    for k in CONV_SHARDED:
        sh = wv[k]
        whole = jnp.zeros(sh.shape[:-1] + (sh.shape[-1] * N_CHIPS,), F32)
        whole = lax.dynamic_update_slice_in_dim(whole, sh, chip * sh.shape[-1], axis=sh.ndim - 1)
        placed.append(jnp.where(core == 1, whole, 0.0))
    conv_full = _unpack_rows(all_sum_small(_pack_rows(placed, LANE, F32), name="gather_conv_weights"),
                             [p.shape for p in placed])
    small = {k: wv[k] for k in WEIGHTS if k not in BIG}
    small.update(dict(zip(CONV_SHARDED, conv_full)))

    T = Bl * S
    pos, invf = _rope_inputs(positions)
    group_a = [(k, 1) for k in big_names] + [(k, 0) for k in EARLY_GRADS]
    group_b = [(k, 0) for k in big_names if k not in EARLY_GRADS]

    def scatter_of(group, layer_grads):
        send = jnp.stack([_pack_rows([_take_shard(layer_grads[l][k], BIG[k] - 1, j) for k, l in group],
                                     PACK_COLS, MXU_DTYPE, PACK_ROW_MULTIPLE) for j in range(N_CHIPS)])
        return send, scatter_exchange(send)

    layer_grads = [None] * DEPTH
    sent = {}
    w0 = _layer_weights(whole_weights(packed[0], others0), small, 0)
    h, saved0, (others1,) = _layer_fwd(x.reshape(T, D), pos, invf, w0, S, 0, hosted=gather_exchange(packed[1]))
    w1 = _layer_weights(whole_weights(packed[1], others1), small, 1)
    h, saved1, _ = _layer_fwd(h, pos, invf, w1, S, 1)
    loss, dh, g_final_norm = final_loss(h, small['final_norm'], loss_target.reshape(T, D))
    dh, layer_grads[1], _ = _layer_bwd(dh, pos, invf, w1, saved1, S, 1)

    def host_a(early):
        layer_grads[0] = early
        sent['a'], ex = scatter_of(group_a, layer_grads)
        return ex

    dx, layer_grads[0], (others_a,) = _layer_bwd(dh, pos, invf, w0, saved0, S, 0, host=host_a)
    sent['b'], ex_b = scatter_of(group_b, layer_grads)
    others_b, = run_exchange(ex_b, name="scatter_grads_b")
    small_names = [k for k in WEIGHTS if k not in BIG]
    grads = {k: jnp.stack([layer_grads[l][k] for l in range(DEPTH)]) for k in small_names if k != 'final_norm'}
    grads['final_norm'] = g_final_norm[0]

    pieces = [{}, {}]
    for tag, group, others in (('a', group_a, others_a), ('b', group_b, others_b)):
        mine = sum_chips(lax.dynamic_index_in_dim(sent[tag], chip, 0, keepdims=False), others, name=f"sum_chips_{tag}")
        theirs = sibling_swap(mine, name=f"swap_core_sums_{tag}")
        shapes = [wv[k].shape[1:] for k, _ in group]
        for dst, packed_sum in zip(pieces, (mine, theirs)):
            dst.update(dict(zip(group, _unpack_rows(packed_sum, shapes))))
    parts = [[jnp.stack([p[(k, l)] for l in range(DEPTH)]) for k in big_names] for p in pieces]
    small_sum = all_sum_small(_pack_rows([grads[k] for k in small_names] + [loss[0, :1]], LANE, F32), name="sum_small_grads")
    summed = _unpack_rows(small_sum, [grads[k].shape for k in small_names] + [(1,)])
    loss_total = summed[-1].reshape(())
    g_small = dict(zip(small_names, summed[:-1]))
    for k in CONV_SHARDED:
        n = wv[k].shape[-1]
        g_small[k] = lax.dynamic_slice_in_dim(g_small[k], chip * n, n, axis=g_small[k].ndim - 1)

    out_g, out_d, out_m, out_v = {}, {}, {}, {}
    for i, k in enumerate(big_names):
        shp = wv[k].shape
        two_d = lambda a, shp=shp: a.reshape(-1, shp[-1])
        res = adamw(two_d(wv[k]), [two_d(parts[0][i]), two_d(parts[1][i])], two_d(mv[k]), two_d(vv[k]), name=f"adamw_{k}")
        out_g[k], out_d[k], out_m[k], out_v[k] = (r.reshape(shp) for r in res)
    shapes = [wv[k].shape for k in small_names]
    pk = lambda d: _pack_rows([d[k] for k in small_names], LANE, F32)
    res = adamw(pk(wv), [pk(g_small)], pk(mv), pk(vv), name="adamw_small")
    for dst, r in zip((out_g, out_d, out_m, out_v), res):
        dst.update(dict(zip(small_names, _unpack_rows(r, shapes))))
    return (loss_total, dx.reshape(Bl, S, D), *[out_g[k] for k in WEIGHTS], *[out_d[k] for k in WEIGHTS],
            *[out_m[k] for k in WEIGHTS], *[out_v[k] for k in WEIGHTS])
```

```python
import functools
import math
from typing import Callable, NamedTuple

import jax
import jax.numpy as jnp
from jax import lax
from jax.experimental import pallas as pl
from jax.experimental.pallas import tpu as pltpu

F32 = jnp.float32
MXU_DTYPE = jnp.bfloat16
HI = lax.Precision.HIGHEST

D_MODEL = 1024
DEPTH = 2
EPS = 1e-6
SSD_HEADS = 16
SSD_HEAD_DIM = 64
SSD_WIDTH = 1024
SSD_GROUPS = 2
SSD_STATE = 128
SSD_CONV = 4
SSD_CHUNK = 128
SSD_CONV_CH = 1536
POOL_GROUPS = 4
POOL_GROUP_DIM = 128
POOL_WIDTH = 512
POOL_WINDOWS = (2, 4, 8, 16)
MLA_HEADS = 8
MLA_Q_RANK = 384
MLA_KV_RANK = 256
MLA_NOPE = 64
MLA_ROPE = 32
MLA_V = 64
MLA_QK = 96
MLA_WIDTH = 512
ROPE_THETA = 10000.0
MIX_WIDTH = 2048
IN_COLS = 3760
D_FF = 2816
FFN_CONV = 3
ADAM_LR = 0.001
ADAM_B1 = 0.9
ADAM_B2 = 0.999
ADAM_EPS = 1e-08
ADAM_WD = 0.01
ADAM_STEP = 10

LANE = 128
HALO = 8
POOL_HALO = 16
PZ0 = 0
PXBC0 = 1024
PU0 = 2560
PCQ0 = 3072
PDT0 = 3456
PCKV0 = 3584
PKPE0 = 3840
PROJ_W = 4096
HEAD_W = 128
MLA_PAD = MLA_HEADS * HEAD_W
MIXP = SSD_WIDTH + POOL_WIDTH + MLA_PAD
N_CHIPS = 4
N_DEV = 8
VMEM_LIMIT = 56 * 1024 * 1024


def _cparams(dims, vmem=None):
    return pltpu.CompilerParams(dimension_semantics=dims, vmem_limit_bytes=vmem or VMEM_LIMIT)


def _sds(shape, dtype):
    return jax.ShapeDtypeStruct(tuple(shape), dtype)


def _mx(v):
    return v.astype(MXU_DTYPE)


def _dot(a, b):
    return jnp.dot(_mx(a), _mx(b), preferred_element_type=F32)


def _dot_nt(a, b):
    return lax.dot_general(_mx(a), _mx(b), (((1,), (1,)), ((), ())), preferred_element_type=F32)


def _dot_tn(a, b):
    return lax.dot_general(_mx(a), _mx(b), (((0,), (0,)), ((), ())), preferred_element_type=F32)


def _dot_hi(a, b):
    return jnp.dot(a, b, preferred_element_type=F32, precision=HI)


def _sigmoid(v):
    return 1.0 / (1.0 + jnp.exp(-v))


def _pick(n, prefs):
    for p in prefs:
        if n % p == 0:
            return p
    return n


def matmul(a, b, *, res=None, out_dtype=F32, name, nt=False, kblock=0, tm=None, tn=None):
    M, K = a.shape
    N = b.shape[0] if nt else b.shape[1]
    assert (b.shape[1] % K == 0) if nt else (K == b.shape[0] and kblock == 0)
    tm = tm or _pick(M, (1024, 512, 256, 128))
    tn = tn or _pick(N, (512, 1408, 256, 128))

    def body(*refs):
        a_ref, b_ref = refs[:2]
        o_ref = refs[-1]
        out = (_dot_nt if nt else _dot)(a_ref[...], b_ref[...])
        if res is not None:
            out = out + refs[2][...]
        o_ref[...] = out.astype(out_dtype)

    b_spec = pl.BlockSpec((tn, K), lambda i, j: (j, kblock)) if nt else pl.BlockSpec((K, tn), lambda i, j: (0, j))
    in_specs = [pl.BlockSpec((tm, K), lambda i, j: (i, 0)), b_spec]
    args = [a, b]
    if res is not None:
        in_specs.append(pl.BlockSpec((tm, tn), lambda i, j: (i, j)))
        args.append(res)
    return pl.pallas_call(
        body, name=name, grid=(M // tm, N // tn), in_specs=in_specs,
        out_specs=pl.BlockSpec((tm, tn), lambda i, j: (i, j)), out_shape=_sds((M, N), out_dtype),
        compiler_params=_cparams(("parallel", "parallel")),
    )(*args)


def matmul_tn(a, g, *, name, tm=None, tn=None, tk=None):
    T, M = a.shape
    T2, N = g.shape
    assert T == T2
    tm = tm or _pick(M, (1408, 1280, 1024, 512, 384, 256, 128))
    tn = tn or _pick(N, (1024, 1408, 512, 256, 128))
    tk = tk or _pick(T, (512, 256, 128))
    nk = T // tk

    def body(a_ref, g_ref, o_ref, acc):
        k = pl.program_id(2)
        part = _dot_tn(a_ref[...], g_ref[...])

        @pl.when(k == 0)
        def _():
            acc[...] = part

        @pl.when(k > 0)
        def _():
            acc[...] += part

        @pl.when(k == nk - 1)
        def _():
            o_ref[...] = acc[...].astype(o_ref.dtype)

    return pl.pallas_call(
        body, name=name, grid=(M // tm, N // tn, nk),
        in_specs=[pl.BlockSpec((tk, tm), lambda i, j, k: (k, i)), pl.BlockSpec((tk, tn), lambda i, j, k: (k, j))],
        out_specs=pl.BlockSpec((tm, tn), lambda i, j, k: (i, j)), out_shape=_sds((M, N), MXU_DTYPE),
        scratch_shapes=[pltpu.VMEM((tm, tn), F32)],
        compiler_params=_cparams(("parallel", "parallel", "arbitrary")),
    )(a, g)


def rmsnorm_fwd(x, gamma, *, name, tm=512):
    T, D = x.shape
    tm = _pick(T, (tm, 256, 128))

    def body(x_ref, g_ref, o_ref):
        xv = x_ref[...]
        r = lax.rsqrt(jnp.mean(xv * xv, axis=-1, keepdims=True) + EPS)
        o_ref[...] = ((xv * r) * g_ref[...]).astype(MXU_DTYPE)

    return pl.pallas_call(
        body, name=name, grid=(T // tm,),
        in_specs=[pl.BlockSpec((tm, D), lambda i: (i, 0)), pl.BlockSpec((1, D), lambda i: (0, 0))],
        out_specs=pl.BlockSpec((tm, D), lambda i: (i, 0)), out_shape=_sds((T, D), MXU_DTYPE),
        compiler_params=_cparams(("parallel",)),
    )(x, gamma.reshape(1, D))


def _rms_bwd_tile(xv, gamma, dh):
    r = lax.rsqrt(jnp.mean(xv * xv, axis=-1, keepdims=True) + EPS)
    xh = xv * r
    dg = jnp.sum(dh * xh, axis=0, keepdims=True)
    dn = dh * gamma
    dx = r * (dn - xh * jnp.mean(dn * xh, axis=-1, keepdims=True))
    return dx, dg


def rmsnorm_bwd(x, gamma, dh, dres, *, name, tm=256):
    T, D = x.shape
    tm = _pick(T, (tm, 128))

    def body(x_ref, g_ref, dh_ref, dr_ref, dx_ref, dg_ref):
        dx, dg = _rms_bwd_tile(x_ref[...], g_ref[...], dh_ref[...].astype(F32))
        dx_ref[...] = dx + dr_ref[...]

        @pl.when(pl.program_id(0) == 0)
        def _():
            dg_ref[...] = dg

        @pl.when(pl.program_id(0) > 0)
        def _():
            dg_ref[...] += dg

    row = pl.BlockSpec((tm, D), lambda i: (i, 0))
    vec = pl.BlockSpec((1, D), lambda i: (0, 0))
    return pl.pallas_call(
        body, name=name, grid=(T // tm,), in_specs=[row, vec, row, row], out_specs=[row, vec],
        out_shape=[_sds((T, D), F32), _sds((1, D), F32)], compiler_params=_cparams(("arbitrary",)),
    )(x, gamma.reshape(1, D), dh, dres)


def final_loss(x, gamma, target, *, name="final_loss", tm=256):
    T, D = x.shape
    tm = _pick(T, (tm, 128))

    def body(x_ref, g_ref, t_ref, l_ref, dx_ref, dg_ref):
        xv = x_ref[...]
        gam = g_ref[...]
        r = lax.rsqrt(jnp.mean(xv * xv, axis=-1, keepdims=True) + EPS)
        y = (xv * r) * gam
        err = y - t_ref[...]
        part = 0.5 * jnp.sum(jnp.sum(err * err, axis=-1, keepdims=True) / D, axis=0, keepdims=True)
        dx, dg = _rms_bwd_tile(xv, gam, err / D)
        dx_ref[...] = dx

        @pl.when(pl.program_id(0) == 0)
        def _():
            dg_ref[...] = dg
            l_ref[...] = jnp.broadcast_to(part, l_ref.shape)

        @pl.when(pl.program_id(0) > 0)
        def _():
            dg_ref[...] += dg
            l_ref[...] += jnp.broadcast_to(part, l_ref.shape)

    row = pl.BlockSpec((tm, D), lambda i: (i, 0))
    vec = pl.BlockSpec((1, D), lambda i: (0, 0))
    return pl.pallas_call(
        body, name=name, grid=(T // tm,), in_specs=[row, vec, row],
        out_specs=[pl.BlockSpec((1, LANE), lambda i: (0, 0)), row, vec],
        out_shape=[_sds((1, LANE), F32), _sds((T, D), F32), _sds((1, D), F32)],
        compiler_params=_cparams(("arbitrary",)),
    )(x, gamma.reshape(1, D), target)


def _halo_prev(ts):
    return lambda i, j, off=0: (jnp.maximum(i * (ts // HALO) - 1, 0), j + off)


def _cat_prev(cur, halo, first):
    return jnp.concatenate([jnp.where(first, 0.0, halo), cur], axis=0)


def _cat_next(cur, halo, last):
    return jnp.concatenate([cur, jnp.where(last, 0.0, halo)], axis=0)


def _delayed(cat, r):
    if r == 0:
        return cat[HALO:]
    return pltpu.roll(cat, r, axis=0)[HALO:]


def _advanced(cat, r):
    n = cat.shape[0]
    if r == 0:
        return cat[:n - HALO]
    return pltpu.roll(cat, n - r, axis=0)[:n - HALO]


def _conv_pre(cat, w, b, K):
    acc = _delayed(cat, K - 1) * w[0:1, :] + b
    for k in range(1, K):
        acc = acc + _delayed(cat, K - 1 - k) * w[k:k + 1, :]
    return acc


def _pad_rows8(w):
    return jnp.pad(w, ((0, 8 - w.shape[0]), (0, 0)))


def ssd_conv_fwd(proj, w, b, S, *, name, ts=1024, tc=512):
    T = proj.shape[0]
    C, K = SSD_CONV_CH, SSD_CONV
    ts = _pick(S, (ts, 256, 128))
    off = PXBC0 // tc
    ns = S // ts

    def body(x_ref, h_ref, w_ref, b_ref, o_ref):
        first = (pl.program_id(0) % ns) == 0
        pre = _conv_pre(_cat_prev(x_ref[...], h_ref[...], first), w_ref[...], b_ref[...], K)
        o_ref[...] = pre * _sigmoid(pre)

    return pl.pallas_call(
        body, name=name, grid=(T // ts, C // tc),
        in_specs=[pl.BlockSpec((ts, tc), lambda i, j: (i, j + off)),
                  pl.BlockSpec((HALO, tc), functools.partial(_halo_prev(ts), off=off)),
                  pl.BlockSpec((8, tc), lambda i, j: (0, j)), pl.BlockSpec((1, tc), lambda i, j: (0, j))],
        out_specs=pl.BlockSpec((ts, tc), lambda i, j: (i, j)), out_shape=_sds((T, C), F32),
        compiler_params=_cparams(("parallel", "parallel")),
    )(proj, proj, _pad_rows8(w), b.reshape(1, C))


def ssd_conv_bwd_pre(proj, w, b, dxc, S, *, name, ts=1024, tc=512):
    T = proj.shape[0]
    C, K = SSD_CONV_CH, SSD_CONV
    ts = _pick(S, (ts, 256, 128))
    off = PXBC0 // tc
    ns = S // ts

    def body(x_ref, h_ref, w_ref, b_ref, d_ref, o_ref, acc_ref):
        i = pl.program_id(1)
        first = (i % ns) == 0
        cat = _cat_prev(x_ref[...], h_ref[...], first)
        pre = _conv_pre(cat, w_ref[...], b_ref[...], K)
        sg = _sigmoid(pre)
        dpre = d_ref[...] * (sg * (1.0 + pre * (1.0 - sg)))
        o_ref[...] = dpre
        rows = [jnp.sum(dpre * _delayed(cat, K - 1 - k), axis=0, keepdims=True) for k in range(K)]
        rows.append(jnp.sum(dpre, axis=0, keepdims=True))
        rows.append(jnp.zeros((8 - len(rows), dpre.shape[1]), F32))
        part = jnp.concatenate(rows, axis=0)

        @pl.when(i == 0)
        def _():
            acc_ref[...] = part

        @pl.when(i > 0)
        def _():
            acc_ref[...] += part

    hp = _halo_prev(ts)
    return pl.pallas_call(
        body, name=name, grid=(C // tc, T // ts),
        in_specs=[pl.BlockSpec((ts, tc), lambda j, i: (i, j + off)),
                  pl.BlockSpec((HALO, tc), lambda j, i: hp(i, j, off)),
                  pl.BlockSpec((8, tc), lambda j, i: (0, j)), pl.BlockSpec((1, tc), lambda j, i: (0, j)),
                  pl.BlockSpec((ts, tc), lambda j, i: (i, j))],
        out_specs=[pl.BlockSpec((ts, tc), lambda j, i: (i, j)), pl.BlockSpec((8, tc), lambda j, i: (0, j))],
        out_shape=[_sds((T, C), F32), _sds((8, C), F32)],
        compiler_params=_cparams(("parallel", "arbitrary")),
    )(proj, proj, _pad_rows8(w), b.reshape(1, C), dxc)


def conv_bwd_x(dpre, w, S, K, *, name, ts=512, tc=None):
    T, C = dpre.shape
    out_dtype = MXU_DTYPE
    ts = _pick(S, (ts, 256, 128))
    tc = tc or _pick(C, (1408, 512, 256, 128))
    ns = S // ts
    nblk = T // HALO

    def body(d_ref, h_ref, w_ref, o_ref):
        last = (pl.program_id(0) % ns) == ns - 1
        cat = _cat_next(d_ref[...], h_ref[...], last)
        wv = w_ref[...]
        acc = _advanced(cat, K - 1) * wv[0:1, :]
        for k in range(1, K):
            acc = acc + _advanced(cat, K - 1 - k) * wv[k:k + 1, :]
        o_ref[...] = acc.astype(out_dtype)

    return pl.pallas_call(
        body, name=name, grid=(T // ts, C // tc),
        in_specs=[pl.BlockSpec((ts, tc), lambda i, j: (i, j)),
                  pl.BlockSpec((HALO, tc), lambda i, j: (jnp.minimum((i + 1) * (ts // HALO), nblk - 1), j)),
                  pl.BlockSpec((8, tc), lambda i, j: (0, j))],
        out_specs=pl.BlockSpec((ts, tc), lambda i, j: (i, j)), out_shape=_sds((T, C), out_dtype),
        compiler_params=_cparams(("parallel", "parallel")),
    )(dpre, dpre, _pad_rows8(w))


def ffn_conv_gate_fwd(up, w, b, S, *, name, ts=512, tc=1408):
    T, C2 = up.shape
    C, K = C2 // 2, FFN_CONV
    ts = _pick(S, (ts, 256, 128))
    nj = C // tc
    ns = S // ts
    w8 = _pad_rows8(w)
    b2 = b.reshape(1, C2)

    def body(g_ref, gh_ref, v_ref, vh_ref, wg_ref, wv_ref, bg_ref, bv_ref, o_ref):
        first = (pl.program_id(0) % ns) == 0
        g = _conv_pre(_cat_prev(g_ref[...], gh_ref[...], first), wg_ref[...], bg_ref[...], K)
        v = _conv_pre(_cat_prev(v_ref[...], vh_ref[...], first), wv_ref[...], bv_ref[...], K)
        o_ref[...] = (g * _sigmoid(g) * v).astype(o_ref.dtype)

    hp = _halo_prev(ts)
    return pl.pallas_call(
        body, name=name, grid=(T // ts, nj),
        in_specs=[pl.BlockSpec((ts, tc), lambda i, j: (i, j)), pl.BlockSpec((HALO, tc), lambda i, j: hp(i, j)),
                  pl.BlockSpec((ts, tc), lambda i, j: (i, j + nj)), pl.BlockSpec((HALO, tc), lambda i, j: hp(i, j, nj)),
                  pl.BlockSpec((8, tc), lambda i, j: (0, j)), pl.BlockSpec((8, tc), lambda i, j: (0, j + nj)),
                  pl.BlockSpec((1, tc), lambda i, j: (0, j)), pl.BlockSpec((1, tc), lambda i, j: (0, j + nj))],
        out_specs=pl.BlockSpec((ts, tc), lambda i, j: (i, j)), out_shape=_sds((T, C), MXU_DTYPE),
        compiler_params=_cparams(("parallel", "parallel")),
    )(up, up, up, up, w8, w8, b2, b2)


def ffn_conv_gate_bwd(up, w, b, dact, S, *, name, ts=256, tc=1408):
    T, C2 = up.shape
    C, K = C2 // 2, FFN_CONV
    ts = _pick(S, (ts, 256, 128))
    nj = C // tc
    ns = S // ts
    w8 = _pad_rows8(w)
    b2 = b.reshape(1, C2)

    def stats(dpre, cat):
        rows = [jnp.sum(dpre * _delayed(cat, K - 1 - k), axis=0, keepdims=True) for k in range(K)]
        rows.append(jnp.sum(dpre, axis=0, keepdims=True))
        rows.append(jnp.zeros((8 - len(rows), dpre.shape[1]), F32))
        return jnp.concatenate(rows, axis=0)

    def body(g_ref, gh_ref, v_ref, vh_ref, wg_ref, wv_ref, bg_ref, bv_ref, d_ref, dg_ref, dv_ref, ag_ref, av_ref):
        i = pl.program_id(1)
        first = (i % ns) == 0
        gcat = _cat_prev(g_ref[...], gh_ref[...], first)
        vcat = _cat_prev(v_ref[...], vh_ref[...], first)
        g = _conv_pre(gcat, wg_ref[...], bg_ref[...], K)
        v = _conv_pre(vcat, wv_ref[...], bv_ref[...], K)
        d = d_ref[...]
        sg = _sigmoid(g)
        dg = d * v * (sg * (1.0 + g * (1.0 - sg)))
        dv = d * (g * sg)
        dg_ref[...] = dg
        dv_ref[...] = dv
        sgp, svp = stats(dg, gcat), stats(dv, vcat)

        @pl.when(i == 0)
        def _():
            ag_ref[...] = sgp
            av_ref[...] = svp

        @pl.when(i > 0)
        def _():
            ag_ref[...] += sgp
            av_ref[...] += svp

    hp = _halo_prev(ts)
    dg, dv, ag, av = pl.pallas_call(
        body, name=name, grid=(nj, T // ts),
        in_specs=[pl.BlockSpec((ts, tc), lambda j, i: (i, j)), pl.BlockSpec((HALO, tc), lambda j, i: hp(i, j)),
                  pl.BlockSpec((ts, tc), lambda j, i: (i, j + nj)), pl.BlockSpec((HALO, tc), lambda j, i: hp(i, j, nj)),
                  pl.BlockSpec((8, tc), lambda j, i: (0, j)), pl.BlockSpec((8, tc), lambda j, i: (0, j + nj)),
                  pl.BlockSpec((1, tc), lambda j, i: (0, j)), pl.BlockSpec((1, tc), lambda j, i: (0, j + nj)),
                  pl.BlockSpec((ts, tc), lambda j, i: (i, j))],
        out_specs=[pl.BlockSpec((ts, tc), lambda j, i: (i, j)), pl.BlockSpec((ts, tc), lambda j, i: (i, j)),
                   pl.BlockSpec((8, tc), lambda j, i: (0, j)), pl.BlockSpec((8, tc), lambda j, i: (0, j))],
        out_shape=[_sds((T, C), F32), _sds((T, C), F32), _sds((8, C), F32), _sds((8, C), F32)],
        compiler_params=_cparams(("parallel", "arbitrary")),
    )(up, up, up, up, w8, w8, b2, b2, dact)
    return dg, dv, jnp.concatenate([ag, av], axis=1)


def _pool_counts(pos, w):
    return jnp.minimum(pos + 1.0, float(w))


def pool_fwd(proj, pool_w, pool_scale, S, *, name, ts=512):
    T = proj.shape[0]
    C, G, GD, H = POOL_WIDTH, POOL_GROUPS, POOL_GROUP_DIM, POOL_HALO
    ts = _pick(S, (ts, 256, 128))
    ns = S // ts
    off = PU0 // C

    def body(u_ref, h_ref, w_ref, s_ref, y_ref, p_ref):
        i = pl.program_id(0)
        first = (i % ns) == 0
        cat = jnp.concatenate([jnp.where(first, 0.0, h_ref[...]), u_ref[...]], axis=0)
        pos = ((i % ns) * ts + lax.broadcasted_iota(jnp.int32, (ts, 1), 0)).astype(F32)
        sums = cat
        win = 1
        for g, wlen in enumerate(POOL_WINDOWS):
            while win < wlen:
                sums = sums + pltpu.roll(sums, win, axis=0)
                win *= 2
            sl = slice(g * GD, (g + 1) * GD)
            pooled = sums[H:, sl] / _pool_counts(pos, wlen) - cat[H:, sl]
            p_ref[:, sl] = pooled.astype(p_ref.dtype)
            y_ref[:, sl] = (_dot(pooled, w_ref[g]) * s_ref[:, sl]).astype(y_ref.dtype)

    return pl.pallas_call(
        body, name=name, grid=(T // ts,),
        in_specs=[pl.BlockSpec((ts, C), lambda i: (i, off)),
                  pl.BlockSpec((H, C), lambda i: (jnp.maximum(i * (ts // H) - 1, 0), off)),
                  pl.BlockSpec((G, GD, GD), lambda i: (0, 0, 0)), pl.BlockSpec((1, C), lambda i: (0, 0))],
        out_specs=[pl.BlockSpec((ts, C), lambda i: (i, 0)), pl.BlockSpec((ts, C), lambda i: (i, 0))],
        out_shape=[_sds((T, C), MXU_DTYPE), _sds((T, C), MXU_DTYPE)],
        compiler_params=_cparams(("parallel",)),
    )(proj, proj, _mx(pool_w), pool_scale.reshape(1, C))


def pool_bwd(dmix, pooled, pool_w, pool_scale, S, *, name, ts=512):
    T = dmix.shape[0]
    C, G, GD, H = POOL_WIDTH, POOL_GROUPS, POOL_GROUP_DIM, POOL_HALO
    ts = _pick(S, (ts, 256, 128))
    ns = S // ts
    off = SSD_WIDTH // C
    nblk = T // H

    def body(d_ref, dh_ref, p_ref, w_ref, s_ref, du_ref, dw_ref, ds_ref):
        i = pl.program_id(0)
        last = (i % ns) == ns - 1
        dcat = jnp.concatenate([d_ref[...], jnp.where(last, 0.0, dh_ref[...])], axis=0)
        n = ts + H
        pos = ((i % ns) * ts + lax.broadcasted_iota(jnp.int32, (n, 1), 0)).astype(F32)
        dws, dss = [], []
        for g, wlen in enumerate(POOL_WINDOWS):
            sl = slice(g * GD, (g + 1) * GD)
            wg = w_ref[g]
            pg = p_ref[:, sl]
            dys = dcat[:, sl] * s_ref[:, sl]
            dss.append(jnp.sum(dcat[:ts, sl] * _dot(pg, wg), axis=0, keepdims=True))
            dws.append(_dot_tn(pg, dys[:ts]))
            dp = _dot_nt(dys, wg)
            q = dp / _pool_counts(pos, wlen)
            win = 1
            while win < wlen:
                q = q + pltpu.roll(q, n - win, axis=0)
                win *= 2
            du_ref[:, sl] = (q[:ts] - dp[:ts]).astype(du_ref.dtype)
        dsp = jnp.concatenate(dss, axis=1)

        @pl.when(i == 0)
        def _():
            for g in range(G):
                dw_ref[g] = dws[g]
            ds_ref[...] = dsp

        @pl.when(i > 0)
        def _():
            for g in range(G):
                dw_ref[g] += dws[g]
            ds_ref[...] += dsp

    return pl.pallas_call(
        body, name=name, grid=(T // ts,),
        in_specs=[pl.BlockSpec((ts, C), lambda i: (i, off)),
                  pl.BlockSpec((H, C), lambda i: (jnp.minimum((i + 1) * (ts // H), nblk - 1), off)),
                  pl.BlockSpec((ts, C), lambda i: (i, 0)),
                  pl.BlockSpec((G, GD, GD), lambda i: (0, 0, 0)), pl.BlockSpec((1, C), lambda i: (0, 0))],
        out_specs=[pl.BlockSpec((ts, C), lambda i: (i, 0)), pl.BlockSpec((G, GD, GD), lambda i: (0, 0, 0)),
                   pl.BlockSpec((1, C), lambda i: (0, 0))],
        out_shape=[_sds((T, C), MXU_DTYPE), _sds((G, GD, GD), F32), _sds((1, C), F32)],
        compiler_params=_cparams(("arbitrary",)),
    )(dmix, dmix, pooled, _mx(pool_w), pool_scale.reshape(1, C))


ROPE0 = MLA_NOPE
ROPE_HALF = MLA_ROPE // 2


def _rope_tables(pos, invf):
    lane = lax.broadcasted_iota(jnp.int32, (1, HEAD_W), 1)
    ang = pos * invf
    cs, sn = jnp.cos(ang), jnp.sin(ang)
    in_a = (lane >= ROPE0) & (lane < ROPE0 + ROPE_HALF)
    in_b = (lane >= ROPE0 + ROPE_HALF) & (lane < ROPE0 + MLA_ROPE)
    return jnp.where(in_a | in_b, cs, 1.0), jnp.where(in_a, -sn, 0.0), jnp.where(in_b, sn, 0.0), in_a | in_b


def _rope(v, cosf, sin_a, sin_b):
    return (v * cosf + pltpu.roll(v, HEAD_W - ROPE_HALF, axis=1) * sin_a + pltpu.roll(v, ROPE_HALF, axis=1) * sin_b)


def _unrope(d, cosf, sin_a, sin_b):
    return (d * cosf + pltpu.roll(d * sin_a, ROPE_HALF, axis=1) + pltpu.roll(d * sin_b, HEAD_W - ROPE_HALF, axis=1))


def _rms_tile(xv, gamma):
    return (xv * lax.rsqrt(jnp.mean(xv * xv, axis=-1, keepdims=True) + EPS)) * gamma


def mla_prep_fwd(proj, pos, invf, q_norm, w_uq_p, kv_norm, w_ukv_p, *, name, tm=256):
    T = proj.shape[0]
    tm = _pick(T, (tm, 128))
    QR, KR, P = MLA_Q_RANK, MLA_KV_RANK, MLA_PAD

    def body(cq_ref, ckv_ref, kpe_ref, pos_ref, invf_ref, qn_ref, wq_ref, kn_ref, wkv_ref,
             q_ref, k_ref, v_ref, cqn_ref, ckvn_ref):
        cosf, sin_a, sin_b, _ = _rope_tables(pos_ref[...], invf_ref[...])
        cqn = _rms_tile(cq_ref[...], qn_ref[...]).astype(MXU_DTYPE)
        ckvn = _rms_tile(ckv_ref[...], kn_ref[...]).astype(MXU_DTYPE)
        cqn_ref[...] = cqn
        ckvn_ref[...] = ckvn
        qp = _dot(cqn, wq_ref[...])
        kvp = _dot(ckvn, wkv_ref[...])
        kpe = _rope(kpe_ref[...], cosf, sin_a, sin_b)
        for h in range(MLA_HEADS):
            sl = slice(h * HEAD_W, (h + 1) * HEAD_W)
            q_ref[:, sl] = (_rope(qp[:, sl], cosf, sin_a, sin_b) * ATTN_SCALE).astype(q_ref.dtype)
            k_ref[:, sl] = (kvp[:, sl] + kpe).astype(k_ref.dtype)
            v_ref[:, sl] = kvp[:, P + h * HEAD_W:P + (h + 1) * HEAD_W].astype(v_ref.dtype)

    row = lambda w: pl.BlockSpec((tm, w), lambda i: (i, 0))
    full = lambda a, b: pl.BlockSpec((a, b), lambda i: (0, 0))
    return pl.pallas_call(
        body, name=name, grid=(T // tm,),
        in_specs=[pl.BlockSpec((tm, QR), lambda i: (i, PCQ0 // QR)), pl.BlockSpec((tm, KR), lambda i: (i, PCKV0 // KR)),
                  pl.BlockSpec((tm, LANE), lambda i: (i, PKPE0 // LANE)), row(1), full(1, LANE),
                  full(1, QR), full(QR, P), full(1, KR), full(KR, 2 * P)],
        out_specs=[row(P), row(P), row(P), row(QR), row(KR)],
        out_shape=[_sds((T, P), MXU_DTYPE)] * 3 + [_sds((T, QR), MXU_DTYPE), _sds((T, KR), MXU_DTYPE)],
        compiler_params=_cparams(("parallel",)),
    )(proj, proj, proj, pos, invf, q_norm.reshape(1, QR), w_uq_p, kv_norm.reshape(1, KR), w_ukv_p)


def mla_prep_bwd(proj, pos, invf, q_norm, w_uq_p, kv_norm, w_ukv_p, dq, dk, dv, *, name, tm=256):
    T = proj.shape[0]
    tm = _pick(T, (tm, 128))
    QR, KR, P = MLA_Q_RANK, MLA_KV_RANK, MLA_PAD

    def body(cq_ref, ckv_ref, pos_ref, invf_ref, qn_ref, wq_ref, kn_ref, wkv_ref, dq_ref, dk_ref, dv_ref,
             dqp_ref, dkvp_ref, dcq_ref, dckv_ref, dkpe_ref, dqn_ref, dkn_ref):
        cosf, sin_a, sin_b, rot = _rope_tables(pos_ref[...], invf_ref[...])
        dkpe = jnp.zeros((tm, HEAD_W), F32)
        for h in range(MLA_HEADS):
            sl = slice(h * HEAD_W, (h + 1) * HEAD_W)
            dqp_ref[:, sl] = _unrope(dq_ref[:, sl] * ATTN_SCALE, cosf, sin_a, sin_b).astype(dqp_ref.dtype)
            dkh = dk_ref[:, sl]
            dkpe = dkpe + dkh
            dkvp_ref[:, sl] = dkh.astype(dkvp_ref.dtype)
            dkvp_ref[:, P + h * HEAD_W:P + (h + 1) * HEAD_W] = dv_ref[:, sl].astype(dkvp_ref.dtype)
        dkpe_ref[...] = jnp.where(rot, _unrope(dkpe, cosf, sin_a, sin_b), 0.0).astype(dkpe_ref.dtype)
        dcq, dqn = _rms_bwd_tile(cq_ref[...], qn_ref[...], _dot_nt(dqp_ref[...], wq_ref[...]))
        dckv, dkn = _rms_bwd_tile(ckv_ref[...], kn_ref[...], _dot_nt(dkvp_ref[...], wkv_ref[...]))
        dcq_ref[...] = dcq.astype(dcq_ref.dtype)
        dckv_ref[...] = dckv.astype(dckv_ref.dtype)

        @pl.when(pl.program_id(0) == 0)
        def _():
            dqn_ref[...] = dqn
            dkn_ref[...] = dkn

        @pl.when(pl.program_id(0) > 0)
        def _():
            dqn_ref[...] += dqn
            dkn_ref[...] += dkn

    row = lambda w: pl.BlockSpec((tm, w), lambda i: (i, 0))
    full = lambda a, b: pl.BlockSpec((a, b), lambda i: (0, 0))
    return pl.pallas_call(
        body, name=name, grid=(T // tm,),
        in_specs=[pl.BlockSpec((tm, QR), lambda i: (i, PCQ0 // QR)), pl.BlockSpec((tm, KR), lambda i: (i, PCKV0 // KR)),
                  row(1), full(1, LANE), full(1, QR), full(QR, P), full(1, KR), full(KR, 2 * P), row(P), row(P), row(P)],
        out_specs=[row(P), row(2 * P), row(QR), row(KR), row(LANE), full(1, QR), full(1, KR)],
        out_shape=[_sds((T, P), MXU_DTYPE), _sds((T, 2 * P), MXU_DTYPE), _sds((T, QR), MXU_DTYPE),
                   _sds((T, KR), MXU_DTYPE), _sds((T, LANE), MXU_DTYPE), _sds((1, QR), F32), _sds((1, KR), F32)],
        compiler_params=_cparams(("arbitrary",)),
    )(proj, proj, pos, invf, q_norm.reshape(1, QR), w_uq_p, kv_norm.reshape(1, KR), w_ukv_p, dq, dk, dv)


ATTN_SCALE = 1.0 / math.sqrt(MLA_QK)


def _causal_mask(i, j, blk):
    row = lax.broadcasted_iota(jnp.int32, (blk, blk), 0)
    col = lax.broadcasted_iota(jnp.int32, (blk, blk), 1)
    return col <= row + (i - j) * blk


def _hosting(hosted, grid, n_in, n_out, n_scratch):
    if hosted is None:
        return (lambda body: body), (), [], [], []
    hi, ho = len(hosted.inputs), len(hosted.out_shapes)

    def wrap(body):
        def full(*refs):
            ins, rest = refs[:n_in + hi], refs[n_in + hi:]
            outs, scr = rest[:n_out + ho], rest[n_out + ho:]
            parts = ins[n_in:], outs[n_out:], scr[n_scratch:]
            ids = [pl.program_id(d) for d in range(len(grid))]
            step = ids[0]
            for d in range(1, len(grid)):
                step = step * grid[d] + ids[d]
            total = math.prod(grid)

            @pl.when(step == 0)
            def _():
                hosted.start(*parts)

            body(*ins[:n_in], *outs[:n_out], *scr[:n_scratch])

            @pl.when(step == total // 2)
            def _():
                hosted.relay(*parts)

            @pl.when(step == total - 1)
            def _():
                hosted.finish(*parts)

        return full

    hbm = pl.BlockSpec(memory_space=pl.ANY)
    return wrap, tuple(hosted.inputs), [hbm] * ho, list(hosted.out_shapes), list(hosted.sems)


def flash_fwd(q, k, v, S, *, name, blk=512, hosted=None):
    T, P = q.shape
    blk = _pick(S, (blk, 256, 128))
    B, nq, H, W = T // S, S // blk, MLA_HEADS, HEAD_W
    grid = (B, H, nq)
    wrap, h_in, h_ospecs, h_oshapes, h_scratch = _hosting(hosted, grid, 3, 2, 0)

    def body(q_ref, k_ref, v_ref, o_ref, lse_ref):
        i = pl.program_id(2)
        qv = q_ref[...]

        def online(j, carry, masked):
            m_prev, l_prev, acc = carry
            rows = pl.ds(pl.multiple_of(j * blk, blk), blk)
            s = _dot_nt(qv, k_ref[rows, :])
            if masked:
                s = jnp.where(_causal_mask(0, 0, blk), s, -jnp.inf)
            m_new = jnp.maximum(m_prev, jnp.max(s, axis=1, keepdims=True))
            p = jnp.exp(s - m_new)
            alpha = jnp.exp(m_prev - m_new)
            return (m_new, alpha * l_prev + jnp.sum(p, axis=1, keepdims=True), alpha * acc + _dot(p, v_ref[rows, :]))

        init = (jnp.full((blk, 1), -jnp.inf, F32), jnp.zeros((blk, 1), F32), jnp.zeros((blk, W), F32))
        carry = lax.fori_loop(0, i, lambda j, c: online(j, c, False), init)
        m, l, acc = online(i, carry, True)
        o_ref[...] = acc / l
        lse_ref[...] = jnp.broadcast_to(m + jnp.log(l), (blk, W))

    qmap = lambda b, h, i: (b * nq + i, h)
    kmap = lambda b, h, i: (b, h)
    hbm = pl.BlockSpec(memory_space=pl.ANY)
    return pl.pallas_call(
        wrap(body), name=name, grid=grid,
        in_specs=[pl.BlockSpec((blk, W), qmap), pl.BlockSpec((S, W), kmap), pl.BlockSpec((S, W), kmap)] + [hbm] * len(h_in),
        out_specs=[pl.BlockSpec((blk, W), qmap), pl.BlockSpec((blk, W), qmap)] + h_ospecs,
        out_shape=[_sds((T, P), F32), _sds((T, P), F32)] + h_oshapes,
        scratch_shapes=h_scratch,
        compiler_params=_cparams(("arbitrary",) * 3 if hosted else ("parallel", "parallel", "arbitrary")),
    )(q, k, v, *h_in)


def flash_bwd(q, k, v, o, lse, dmix, S, *, name, blk=512, hosted=None):
    T, P = q.shape
    blk = _pick(S, (blk, 256, 128))
    B, nq, H, W = T // S, S // blk, MLA_HEADS, HEAD_W
    off = (SSD_WIDTH + POOL_WIDTH) // W
    grid = (B, H, nq)
    wrap, h_in, h_ospecs, h_oshapes, h_scratch = _hosting(hosted, grid, 6, 3, 1)

    def body(q_ref, k_ref, v_ref, o_ref, lse_ref, do_ref, dq_ref, dk_ref, dv_ref, delta_s):
        j = pl.program_id(2)

        @pl.when(j == 0)
        def _():
            for i in range(nq):
                rows = slice(i * blk, (i + 1) * blk)
                delta_s[rows, :] = jnp.sum(do_ref[rows, :] * o_ref[rows, :], axis=1, keepdims=True)
                dq_ref[rows, :] = jnp.zeros((blk, W), F32)

        kv, vv = k_ref[...], v_ref[...]

        def step(i, carry, masked):
            dk, dv = carry
            rows = pl.ds(pl.multiple_of(i * blk, blk), blk)
            qv, do = q_ref[rows, :], do_ref[rows, :]
            p = jnp.exp(_dot_nt(qv, kv) - lse_ref[rows, 0:1])
            if masked:
                p = jnp.where(_causal_mask(0, 0, blk), p, 0.0)
            ds = p * (_dot_nt(do, vv) - delta_s[rows, :])
            dq_ref[rows, :] += _dot(ds, kv)
            return dk + _dot_tn(ds, qv), dv + _dot_tn(p, do)

        zero = jnp.zeros((blk, W), F32)
        carry = step(j, (zero, zero), True)
        dk, dv = lax.fori_loop(j + 1, nq, lambda i, c: step(i, c, False), carry)
        dk_ref[...] = dk
        dv_ref[...] = dv

    full = lambda b, h, j: (b, h)
    kmap = lambda b, h, j: (b * nq + j, h)
    hbm = pl.BlockSpec(memory_space=pl.ANY)
    return pl.pallas_call(
        wrap(body), name=name, grid=grid,
        in_specs=[pl.BlockSpec((S, W), full), pl.BlockSpec((blk, W), kmap), pl.BlockSpec((blk, W), kmap),
                  pl.BlockSpec((S, W), full), pl.BlockSpec((S, W), full),
                  pl.BlockSpec((S, W), lambda b, h, j: (b, off + h))] + [hbm] * len(h_in),
        out_specs=[pl.BlockSpec((S, W), full), pl.BlockSpec((blk, W), kmap), pl.BlockSpec((blk, W), kmap)] + h_ospecs,
        out_shape=[_sds((T, P), F32)] * 3 + h_oshapes,
        scratch_shapes=[pltpu.VMEM((S, 1), F32)] + h_scratch,
        compiler_params=_cparams(("arbitrary",) * 3 if hosted else ("parallel", "parallel", "arbitrary")),
    )(q, k, v, o, lse, dmix, *h_in)


SSD_PAIRS = SSD_HEADS // 2
PAIRS_PER_GROUP = SSD_PAIRS // SSD_GROUPS
GN = SSD_GROUPS * SSD_STATE


def _log1p_small(e):
    return jnp.where(e < 1e-3, e * (1.0 - e * (0.5 - e / 3.0)), jnp.log(1.0 + e))


def _softplus(v):
    return jnp.maximum(v, 0.0) + _log1p_small(jnp.exp(-jnp.abs(v)))


def _ssd_decay(dt_raw, dtb, alog):
    L = dt_raw.shape[0]
    pre = dt_raw + dtb
    dt = _softplus(pre)
    a = -jnp.exp(alog)
    row = lax.broadcasted_iota(jnp.int32, (L, L), 0)
    col = lax.broadcasted_iota(jnp.int32, (L, L), 1)
    tri = row >= col
    cum = _dot_hi(tri.astype(F32), dt * a)
    return pre, dt, a, tri, cum, cum.T


def _col(m, h):
    return m[:, h:h + 1]


def _pair_sel(m, k, lo):
    return jnp.where(lo, _col(m, 2 * k), _col(m, 2 * k + 1))


def _ssd_specs(S):
    L = SSD_CHUNK
    nc = S // L
    return L, nc


def ssd_fwd(proj, xc, dtb, alog, dchan, normw, S, *, name):
    T = proj.shape[0]
    L, nc = _ssd_specs(S)
    B, W, N = T // S, SSD_WIDTH, SSD_STATE

    def body(xs_ref, bs_ref, cs_ref, dt_ref, z_ref, dtb_ref, alog_ref, dch_ref, nw_ref, y_ref, ys_ref, hin_ref, st):
        @pl.when(pl.program_id(1) == 0)
        def _():
            st[...] = jnp.zeros(st.shape, F32)

        hin_ref[...] = st[...]
        _, dt, a, tri, cum, cum_t = _ssd_decay(dt_ref[...], dtb_ref[...], alog_ref[...])
        e_cum = jnp.exp(cum)
        last = cum[L - 1:L, :]
        w_end = jnp.exp(last - cum)
        e_last = jnp.exp(last)
        lo = lax.broadcasted_iota(jnp.int32, (1, LANE), 1) < SSD_HEAD_DIM
        slo = lax.broadcasted_iota(jnp.int32, (LANE, 1), 0) < SSD_HEAD_DIM
        for g in range(SSD_GROUPS):
            bm = bs_ref[:, g * N:(g + 1) * N]
            cm = cs_ref[:, g * N:(g + 1) * N]
            gmat = _dot_nt(cm, bm)
            for kk in range(PAIRS_PER_GROUP):
                k = g * PAIRS_PER_GROUP + kk
                sl = slice(k * LANE, (k + 1) * LANE)
                xv = xs_ref[:, sl]
                xdt = xv * _pair_sel(dt, k, lo)
                yd = []
                for h in (2 * k, 2 * k + 1):
                    gam = jnp.exp(jnp.where(tri, _col(cum, h) - cum_t[h:h + 1, :], -jnp.inf))
                    yd.append(_dot(gmat * gam, xdt))
                hp = st[sl, :]
                y_off = _dot_nt(cm, hp) * _pair_sel(e_cum, k, lo)
                y_ref[:, sl] = jnp.where(lo, yd[0], yd[1]) + y_off + xv * dch_ref[:, sl]
                zmat = xdt * _pair_sel(w_end, k, lo)
                e_rows = jnp.where(slo, _col(e_last, 2 * k), _col(e_last, 2 * k + 1))
                st[sl, :] = hp * e_rows + _dot_tn(zmat, bm)
        y = y_ref[...]
        z = z_ref[...]
        yz = y * (z * _sigmoid(z))
        ys_ref[...] = _rms_tile(yz, nw_ref[...]).astype(ys_ref.dtype)

    r = lambda b, c: b * nc + c
    vec = lambda w: pl.BlockSpec((1, w), lambda b, c: (0, 0))
    return pl.pallas_call(
        body, name=name, grid=(B, nc),
        in_specs=[pl.BlockSpec((L, W), lambda b, c: (r(b, c), 0)),
                  pl.BlockSpec((L, GN), lambda b, c: (r(b, c), W // GN)),
                  pl.BlockSpec((L, GN), lambda b, c: (r(b, c), W // GN + 1)),
                  pl.BlockSpec((L, LANE), lambda b, c: (r(b, c), PDT0 // LANE)),
                  pl.BlockSpec((L, W), lambda b, c: (r(b, c), PZ0 // W)),
                  vec(LANE), vec(LANE), vec(W), vec(W)],
        out_specs=[pl.BlockSpec((L, W), lambda b, c: (r(b, c), 0)), pl.BlockSpec((L, W), lambda b, c: (r(b, c), 0)),
                   pl.BlockSpec((W, N), lambda b, c: (r(b, c), 0))],
        out_shape=[_sds((T, W), F32), _sds((T, W), MXU_DTYPE), _sds((T // L * W, N), F32)],
        scratch_shapes=[pltpu.VMEM((W, N), F32)],
        compiler_params=_cparams(("parallel", "arbitrary")),
    )(xc, xc, xc, proj, proj, dtb, alog, dchan, normw)


def ssd_bwd(proj, xc, ypre, hin, dmix, dtb, alog, dchan, normw, S, *, name):
    T = proj.shape[0]
    L, nc = _ssd_specs(S)
    B, W, N = T // S, SSD_WIDTH, SSD_STATE

    def body(xs_ref, bs_ref, cs_ref, dt_ref, z_ref, y_ref, hin_ref, dys_ref, dtb_ref, alog_ref, dch_ref, nw_ref,
             dxc_ref, ddt_ref, dz_ref, sm_ref, dnw_ref, dst):
        step = pl.program_id(0) * nc + pl.program_id(1)

        @pl.when(pl.program_id(1) == 0)
        def _():
            dst[...] = jnp.zeros(dst.shape, F32)

        pre, dt, a, tri, cum, cum_t = _ssd_decay(dt_ref[...], dtb_ref[...], alog_ref[...])
        e_cum = jnp.exp(cum)
        last = cum[L - 1:L, :]
        w_end = jnp.exp(last - cum)
        e_last = jnp.exp(last)
        lane = lax.broadcasted_iota(jnp.int32, (1, LANE), 1)
        sub = lax.broadcasted_iota(jnp.int32, (LANE, 1), 0)
        lo, slo = lane < SSD_HEAD_DIM, sub < SSD_HEAD_DIM
        is_last_row = sub == L - 1

        y, z, nw = y_ref[...], z_ref[...], nw_ref[...]
        sg = _sigmoid(z)
        gate = z * sg
        dyz, dnw = _rms_bwd_tile(y * gate, nw, dys_ref[...])
        dy_all = dyz * gate
        dz_ref[...] = (dyz * y * (sg * (1.0 + z * (1.0 - sg)))).astype(dz_ref.dtype)

        d_cum = jnp.zeros((L, LANE), F32)
        d_cum_t = jnp.zeros((LANE, L), F32)
        d_dt = jnp.zeros((L, LANE), F32)
        d_dskip = jnp.zeros((1, LANE), F32)
        for g in range(SSD_GROUPS):
            bm = bs_ref[:, g * N:(g + 1) * N]
            cm = cs_ref[:, g * N:(g + 1) * N]
            gmat = _dot_nt(cm, bm)
            d_g = jnp.zeros((L, L), F32)
            d_bm = jnp.zeros((L, N), F32)
            d_cm = jnp.zeros((L, N), F32)
            for kk in range(PAIRS_PER_GROUP):
                k = g * PAIRS_PER_GROUP + kk
                sl = slice(k * LANE, (k + 1) * LANE)
                xv = xs_ref[:, sl]
                dyv = dy_all[:, sl]
                dt_sel = _pair_sel(dt, k, lo)
                xdt = xv * dt_sel
                hp = hin_ref[sl, :]
                dh_out = dst[sl, :]
                e_sel = _pair_sel(e_cum, k, lo)
                w_sel = _pair_sel(w_end, k, lo)
                y_off = _dot_nt(cm, hp) * e_sel
                zmat = xdt * w_sel
                d_z = _dot_nt(bm, dh_out)
                d_bm = d_bm + _dot(zmat, dh_out)
                d_xdt = d_z * w_sel
                dw_full = d_z * zmat
                e_rows = jnp.where(slo, _col(e_last, 2 * k), _col(e_last, 2 * k + 1))
                hh = jnp.sum(dh_out * hp, axis=1, keepdims=True) * e_rows
                d_r = dyv * e_sel
                d_cm = d_cm + _dot(d_r, hp)
                dst[sl, :] = dh_out * e_rows + _dot_tn(d_r, cm)
                dyoff_full = dyv * y_off
                for j, h in enumerate((2 * k, 2 * k + 1)):
                    mine = lo if j == 0 else jnp.logical_not(lo)
                    smine = slo if j == 0 else jnp.logical_not(slo)
                    hot = lane == h
                    dyh = jnp.where(mine, dyv, 0.0)
                    gam = jnp.exp(jnp.where(tri, _col(cum, h) - cum_t[h:h + 1, :], -jnp.inf))
                    mx = gmat * gam
                    d_xdt = d_xdt + _dot_tn(mx, dyh)
                    d_mx = jnp.where(tri, _dot_nt(dyh, xdt), 0.0)
                    d_g = d_g + d_mx * gam
                    d_seg = d_mx * mx
                    row_l = (jnp.sum(d_seg, axis=1, keepdims=True)
                             + jnp.sum(jnp.where(mine, dyoff_full - dw_full, 0.0), axis=1, keepdims=True))
                    at_end = (jnp.sum(jnp.where(mine, dw_full, 0.0), keepdims=True)
                              + jnp.sum(jnp.where(smine, hh, 0.0), keepdims=True))
                    d_cum = d_cum + jnp.where(hot, row_l + jnp.where(is_last_row, at_end, 0.0), 0.0)
                    d_cum_t = d_cum_t - jnp.where(sub == h, jnp.sum(d_seg, axis=0, keepdims=True), 0.0)
                    d_dskip = d_dskip + jnp.where(hot, jnp.sum(jnp.where(mine, dyv * xv, 0.0), keepdims=True), 0.0)
                for j, h in enumerate((2 * k, 2 * k + 1)):
                    mine = lo if j == 0 else jnp.logical_not(lo)
                    d_dt = d_dt + jnp.where(lane == h, jnp.sum(jnp.where(mine, d_xdt * xv, 0.0), axis=1, keepdims=True), 0.0)
                dxc_ref[:, sl] = d_xdt * dt_sel + dyv * dch_ref[:, sl]
            dxc_ref[:, W + g * N:W + (g + 1) * N] = d_bm + _dot_tn(d_g, cm)
            dxc_ref[:, W + GN + g * N:W + GN + (g + 1) * N] = d_cm + _dot(d_g, bm)

        d_cum = d_cum + d_cum_t.T
        d_da = _dot_hi(jnp.logical_not(tri).astype(F32) + (lax.broadcasted_iota(jnp.int32, (L, L), 0)
                                                              == lax.broadcasted_iota(jnp.int32, (L, L), 1)).astype(F32), d_cum)
        d_dt = d_dt + d_da * a
        heads = lane < SSD_HEADS
        d_pre = jnp.where(heads, d_dt * _sigmoid(pre), 0.0)
        ddt_ref[...] = d_pre.astype(ddt_ref.dtype)
        d_alog = jnp.sum(d_da * dt, axis=0, keepdims=True) * a
        part = jnp.concatenate([jnp.where(heads, d_alog, 0.0), jnp.sum(d_pre, axis=0, keepdims=True), d_dskip,
                                jnp.zeros((5, LANE), F32)], axis=0)

        @pl.when(step == 0)
        def _():
            sm_ref[...] = part
            dnw_ref[...] = dnw

        @pl.when(step > 0)
        def _():
            sm_ref[...] += part
            dnw_ref[...] += dnw

    r = lambda b, c: b * nc + (nc - 1 - c)
    vec = lambda w: pl.BlockSpec((1, w), lambda b, c: (0, 0))
    blk = lambda w, j: pl.BlockSpec((L, w), lambda b, c: (r(b, c), j))
    return pl.pallas_call(
        body, name=name, grid=(B, nc),
        in_specs=[blk(W, 0), blk(GN, W // GN), blk(GN, W // GN + 1), blk(LANE, PDT0 // LANE), blk(W, PZ0 // W),
                  blk(W, 0), pl.BlockSpec((W, N), lambda b, c: (r(b, c), 0)), blk(W, 0),
                  vec(LANE), vec(LANE), vec(W), vec(W)],
        out_specs=[blk(SSD_CONV_CH, 0), blk(LANE, 0), blk(W, 0), pl.BlockSpec((8, LANE), lambda b, c: (0, 0)), vec(W)],
        out_shape=[_sds((T, SSD_CONV_CH), F32), _sds((T, LANE), MXU_DTYPE), _sds((T, W), MXU_DTYPE),
                   _sds((8, LANE), F32), _sds((1, W), F32)],
        scratch_shapes=[pltpu.VMEM((W, N), F32)],
        compiler_params=_cparams(("arbitrary", "arbitrary")),
    )(xc, xc, xc, proj, proj, ypre, hin, dmix, dtb, alog, dchan, normw)


def _adamw_math(w, g, m, v):
    m = ADAM_B1 * m + (1.0 - ADAM_B1) * g
    v = ADAM_B2 * v + (1.0 - ADAM_B2) * (g * g)
    m_hat = m / (1.0 - ADAM_B1 ** ADAM_STEP)
    v_hat = v / (1.0 - ADAM_B2 ** ADAM_STEP)
    delta = -ADAM_LR * (m_hat / (jnp.sqrt(v_hat) + ADAM_EPS) + ADAM_WD * w)
    return delta, m, v


def adamw_layers(w, g_layers, m, v, *, name, tr=256):
    L, A, B = w.shape
    tr = _pick(A, (tr, 192, 176, 128, 64, 32, 16, 8))
    na = A // tr
    n = len(g_layers[0])

    def body(*refs):
        w_ref, m_ref, v_ref = refs[0], refs[1 + L * n], refs[2 + L * n]
        g_ref, d_ref, nm_ref, nv_ref = refs[3 + L * n:]
        layer = pl.program_id(0)
        g = None
        for l in range(L):
            parts = refs[1 + l * n:1 + (l + 1) * n]
            gl = parts[0][...]
            for p in parts[1:]:
                gl = gl + p[...]
            g = gl if g is None else jnp.where(layer == l, gl, g)
        g_ref[...] = g
        d_ref[...], nm_ref[...], nv_ref[...] = _adamw_math(w_ref[...], g, m_ref[...], v_ref[...])

    def g_spec(l):
        return pl.BlockSpec((tr, B), lambda layer, i: (jnp.where(layer == l, i, jnp.where(layer < l, 0, na - 1)), 0))

    spec = pl.BlockSpec((None, tr, B), lambda layer, i: (layer, i, 0))
    return pl.pallas_call(
        body, name=name, grid=(L, na), in_specs=[spec] + [g_spec(l) for l in range(L) for _ in range(n)] + [spec] * 2,
        out_specs=[spec] * 4, out_shape=[_sds((L, A, B), F32)] * 4, compiler_params=_cparams(("arbitrary", "arbitrary")),
    )(w, *[p for parts in g_layers for p in parts], m, v)


def adamw_small(ws, gs, ms, vs, *, name):
    n = len(ws)

    def body(*refs):
        w_refs, g_refs, m_refs, v_refs = (refs[i * n:(i + 1) * n] for i in range(4))
        d_refs, nm_refs, nv_refs = (refs[(4 + i) * n:(5 + i) * n] for i in range(3))
        for a in range(n):
            d_refs[a][...], nm_refs[a][...], nv_refs[a][...] = _adamw_math(
                w_refs[a][...], g_refs[a][...], m_refs[a][...], v_refs[a][...])

    vm = pl.BlockSpec(memory_space=pltpu.VMEM)
    out = pl.pallas_call(
        body, name=name, in_specs=[vm] * (4 * n), out_specs=[vm] * (3 * n),
        out_shape=[_sds(w.shape, F32) for w in ws] * 3, compiler_params=pltpu.CompilerParams(vmem_limit_bytes=VMEM_LIMIT),
    )(*ws, *gs, *ms, *vs)
    return out[:n], out[n:2 * n], out[2 * n:]


def _my_place():
    return lax.axis_index("x"), lax.axis_index("y"), lax.axis_index("c")


def _other_chips(x, y):
    return [(1 - x, y), (x, 1 - y), (1 - x, 1 - y)]


def relation_of(chip, me):
    d = chip ^ me
    return jnp.where(d == 2, 0, jnp.where(d == 1, 1, jnp.where(d == 3, 2, -1)))


class Exchange(NamedTuple):
    inputs: tuple
    out_shapes: tuple
    sems: tuple
    start: Callable
    relay: Callable
    finish: Callable


def scatter_exchange(srcs):
    n = len(srcs)

    def copies(ins, outs, sems):
        x, y, c = _my_place()
        out = []
        for k, (px, py) in enumerate(_other_chips(x, y)):
            for a in range(n):
                out.append(pltpu.make_async_remote_copy(
                    src_ref=ins[a].at[2 * px + py], dst_ref=outs[a].at[k], send_sem=sems[0].at[k, a],
                    recv_sem=sems[1].at[k, a], device_id=(px, py, c), device_id_type=pl.DeviceIdType.MESH))
        return out

    def start(ins, outs, sems):
        for cp in copies(ins, outs, sems):
            cp.start()

    def finish(ins, outs, sems):
        cps = copies(ins, outs, sems)
        for cp in cps:
            cp.wait_recv()
        for cp in cps:
            cp.wait_send()

    return Exchange(tuple(srcs), tuple(_sds((3,) + s.shape[1:], s.dtype) for s in srcs),
                    (pltpu.SemaphoreType.DMA((3, n)),) * 2, start, lambda *a: None, finish)


def run_exchange(ex, *, name):
    n_in, n_out = len(ex.inputs), len(ex.out_shapes)

    def body(*refs):
        parts = refs[:n_in], refs[n_in:n_in + n_out], refs[n_in + n_out:]
        ex.start(*parts)
        ex.relay(*parts)
        ex.finish(*parts)

    hbm = pl.BlockSpec(memory_space=pl.ANY)
    return pl.pallas_call(
        body, name=name, in_specs=[hbm] * n_in, out_specs=[hbm] * n_out, out_shape=list(ex.out_shapes),
        scratch_shapes=list(ex.sems), compiler_params=pltpu.CompilerParams(has_side_effects=True),
    )(*ex.inputs)


def sibling_swap(srcs, *, name):
    n = len(srcs)

    def body(*refs):
        src_refs, out_refs, (send_sems, recv_sems) = refs[:n], refs[n:2 * n], refs[2 * n:]
        x, y, c = _my_place()
        copies = [pltpu.make_async_remote_copy(
            src_ref=src_refs[a], dst_ref=out_refs[a], send_sem=send_sems.at[a], recv_sem=recv_sems.at[a],
            device_id=(x, y, 1 - c), device_id_type=pl.DeviceIdType.MESH) for a in range(n)]
        for cp in copies:
            cp.start()
        for cp in copies:
            cp.wait_recv()
        for cp in copies:
            cp.wait_send()

    hbm = pl.BlockSpec(memory_space=pl.ANY)
    return pl.pallas_call(
        body, name=name, in_specs=[hbm] * n, out_specs=[hbm] * n, out_shape=[_sds(s.shape, s.dtype) for s in srcs],
        scratch_shapes=[pltpu.SemaphoreType.DMA((n,)), pltpu.SemaphoreType.DMA((n,))],
        compiler_params=pltpu.CompilerParams(has_side_effects=True),
    )(*srcs)


def gather_exchange(srcs):
    nch = len(srcs)
    halves = [s.shape[0] // 2 for s in srcs]
    assert all(2 * h == s.shape[0] and h % 16 == 0 for h, s in zip(halves, srcs))
    pieces = [(k, q) for k in range(3) for q in range(nch)]

    def makers(ins, outs, sems):
        ici_send, ici_recv, d2d_send, d2d_recv = sems
        x, y, c = _my_place()
        peers = _other_chips(x, y)

        def rows(core, q):
            return pl.ds(core * halves[q], halves[q])

        def ici(k, q):
            px, py = peers[k]
            return pltpu.make_async_remote_copy(
                src_ref=ins[q].at[rows(c, q)], dst_ref=outs[q].at[k, rows(c, q)], send_sem=ici_send.at[k, q],
                recv_sem=ici_recv.at[k, q], device_id=(px, py, c), device_id_type=pl.DeviceIdType.MESH)

        def d2d(k, q, core):
            return pltpu.make_async_remote_copy(
                src_ref=outs[q].at[k, rows(core, q)], dst_ref=outs[q].at[k, rows(core, q)],
                send_sem=d2d_send.at[k, q], recv_sem=d2d_recv.at[k, q], device_id=(x, y, 1 - c),
                device_id_type=pl.DeviceIdType.MESH)

        return ici, d2d, c

    def start(*refs):
        ici, _, _ = makers(*refs)
        for k, q in pieces:
            ici(k, q).start()

    def relay(*refs):
        ici, d2d, c = makers(*refs)
        for k, q in pieces:
            ici(k, q).wait_recv()
            d2d(k, q, c).start()

    def finish(*refs):
        ici, d2d, c = makers(*refs)
        for k, q in pieces:
            d2d(k, q, 1 - c).wait_recv()
        for k, q in pieces:
            ici(k, q).wait_send()
            d2d(k, q, c).wait_send()

    return Exchange(tuple(srcs), tuple(_sds((3,) + s.shape, s.dtype) for s in srcs),
                    (pltpu.SemaphoreType.DMA((3, nch)),) * 4, start, relay, finish)


def all_sum_small(vec, *, name):
    R, C = vec.shape

    def body(v_ref, out_ref, buf, send_sems, recv_sems):
        x, y, c = _my_place()
        me = 4 * x + 2 * y + c
        buf[me] = v_ref[...]
        copies = []
        for k in range(1, N_DEV):
            px, py, pc = x ^ (k >> 2), y ^ ((k >> 1) & 1), c ^ (k & 1)
            copies.append(pltpu.make_async_remote_copy(
                src_ref=v_ref, dst_ref=buf.at[me], send_sem=send_sems.at[k - 1], recv_sem=recv_sems.at[k - 1],
                device_id=(px, py, pc), device_id_type=pl.DeviceIdType.MESH))
        for cp in copies:
            cp.start()
        for k in range(1, N_DEV):
            px, py, pc = x ^ (k >> 2), y ^ ((k >> 1) & 1), c ^ (k & 1)
            pltpu.make_async_remote_copy(
                src_ref=v_ref, dst_ref=buf.at[4 * px + 2 * py + pc], send_sem=send_sems.at[k - 1],
                recv_sem=recv_sems.at[k - 1], device_id=(px, py, pc), device_id_type=pl.DeviceIdType.MESH).wait_recv()
        for cp in copies:
            cp.wait_send()
        acc = buf[0]
        for d in range(1, N_DEV):
            acc = acc + buf[d]
        out_ref[...] = acc

    return pl.pallas_call(
        body, name=name, in_specs=[pl.BlockSpec(memory_space=pltpu.VMEM)], out_specs=pl.BlockSpec(memory_space=pltpu.VMEM),
        out_shape=_sds((R, C), F32),
        scratch_shapes=[pltpu.VMEM((N_DEV, R, C), F32), pltpu.SemaphoreType.DMA((N_DEV - 1,)),
                        pltpu.SemaphoreType.DMA((N_DEV - 1,))],
        compiler_params=pltpu.CompilerParams(has_side_effects=True, vmem_limit_bytes=VMEM_LIMIT),
    )(vec)


def sum_chips(own, others, *, name, tr=512):
    R, C = own.shape
    tr = _pick(R, (tr, 384, 352, 256, 128, 64, 32, 16))

    def body(o_ref, p_ref, s_ref):
        acc = o_ref[...].astype(F32)
        for k in range(3):
            acc = acc + p_ref[k].astype(F32)
        s_ref[...] = acc

    return pl.pallas_call(
        body, name=name, grid=(R // tr,),
        in_specs=[pl.BlockSpec((tr, C), lambda i: (i, 0)), pl.BlockSpec((3, tr, C), lambda i: (0, i, 0))],
        out_specs=pl.BlockSpec((tr, C), lambda i: (i, 0)), out_shape=_sds((R, C), F32),
        compiler_params=_cparams(("parallel",)),
    )(own, others)


WEIGHTS = ['attn_norm', 'w_in', 'ssd_conv_w', 'ssd_conv_b', 'ssd_dt_bias', 'ssd_a_log', 'ssd_d', 'ssd_norm', 'pool_w',
           'pool_scale', 'mla_q_norm', 'mla_w_uq', 'mla_kv_norm', 'mla_w_ukv', 'w_out', 'ffn_norm', 'ffn_w_up',
           'ffn_conv_w', 'ffn_conv_b', 'ffn_w_down', 'final_norm']
BIG = {'w_in': 2, 'mla_w_uq': 2, 'mla_w_ukv': 2, 'w_out': 1, 'ffn_w_up': 2, 'ffn_w_down': 1}
CONV_SHARDED = ('ssd_conv_w', 'ffn_conv_w')


def _zeros_cols(w, n):
    return jnp.zeros((w.shape[0], n), w.dtype)


def _w_in_to_padded(w):
    return jnp.concatenate([w[:, 0:2560], w[:, 2576:3088], w[:, 3088:3472], w[:, 2560:2576], _zeros_cols(w, 112),
                            w[:, 3472:3728], _zeros_cols(w, 64), w[:, 3728:3760], _zeros_cols(w, 32 + 128)], axis=1)


def _w_in_from_padded(g):
    return jnp.concatenate([g[:, 0:2560], g[:, PDT0:PDT0 + SSD_HEADS], g[:, PU0:PU0 + POOL_WIDTH],
                            g[:, PCQ0:PCQ0 + MLA_Q_RANK], g[:, PCKV0:PCKV0 + MLA_KV_RANK],
                            g[:, PKPE0 + ROPE0:PKPE0 + ROPE0 + MLA_ROPE]], axis=1)


def _w_uq_to_padded(w):
    r = w.reshape(MLA_Q_RANK, MLA_HEADS, MLA_QK)
    return jnp.pad(r, ((0, 0), (0, 0), (0, HEAD_W - MLA_QK))).reshape(MLA_Q_RANK, MLA_PAD)


def _w_uq_from_padded(g):
    return g.reshape(MLA_Q_RANK, MLA_HEADS, HEAD_W)[:, :, :MLA_QK].reshape(MLA_Q_RANK, MLA_HEADS * MLA_QK)


def _w_ukv_to_padded(w):
    r = w.reshape(MLA_KV_RANK, MLA_HEADS, MLA_NOPE + MLA_V)
    pad = lambda t: jnp.pad(t, ((0, 0), (0, 0), (0, HEAD_W - t.shape[2]))).reshape(MLA_KV_RANK, MLA_PAD)
    return jnp.concatenate([pad(r[:, :, :MLA_NOPE]), pad(r[:, :, MLA_NOPE:])], axis=1)


def _w_ukv_from_padded(g):
    kk = g[:, :MLA_PAD].reshape(MLA_KV_RANK, MLA_HEADS, HEAD_W)[:, :, :MLA_NOPE]
    vv = g[:, MLA_PAD:].reshape(MLA_KV_RANK, MLA_HEADS, HEAD_W)[:, :, :MLA_V]
    return jnp.concatenate([kk, vv], axis=2).reshape(MLA_KV_RANK, MLA_HEADS * (MLA_NOPE + MLA_V))


def _w_out_to_padded(w):
    att = w[SSD_WIDTH + POOL_WIDTH:].reshape(MLA_HEADS, MLA_V, D_MODEL)
    att = jnp.pad(att, ((0, 0), (0, HEAD_W - MLA_V), (0, 0))).reshape(MLA_PAD, D_MODEL)
    return jnp.concatenate([w[:SSD_WIDTH + POOL_WIDTH], att], axis=0)


def _w_out_from_padded(g):
    att = g[SSD_WIDTH + POOL_WIDTH:].reshape(MLA_HEADS, HEAD_W, D_MODEL)[:, :MLA_V].reshape(MLA_WIDTH, D_MODEL)
    return jnp.concatenate([g[:SSD_WIDTH + POOL_WIDTH], att], axis=0)


def _pad_lanes(v, n=LANE):
    return jnp.pad(v.reshape(1, -1), ((0, 0), (0, n - v.size)))


def _pack_rows(parts, cols, dtype, row_multiple=16):
    flat = jnp.concatenate([p.astype(dtype).reshape(-1) for p in parts])
    rows = -(-flat.size // (cols * row_multiple)) * row_multiple
    return jnp.pad(flat, (0, rows * cols - flat.size)).reshape(rows, cols)


def _unpack_rows(packed, shapes):
    flat = packed.reshape(-1)
    out, at = [], 0
    for s in shapes:
        n = math.prod(s)
        out.append(flat[at:at + n].reshape(s))
        at += n
    return out


def _split_for_chips(g, axis):
    a, b = g.shape
    if axis == 0:
        return g.reshape(N_CHIPS, a // N_CHIPS, b)
    return g.reshape(a, N_CHIPS, b // N_CHIPS).transpose(1, 0, 2)


def _layer_weights(full, small, l):
    w = {}
    w['w_in_p'] = _w_in_to_padded(full['w_in'])
    w['w_uq_p'] = _w_uq_to_padded(full['mla_w_uq'])
    w['w_ukv_p'] = _w_ukv_to_padded(full['mla_w_ukv'])
    w['w_out_p'] = _w_out_to_padded(full['w_out'])
    w['w_up'] = full['ffn_w_up']
    w['w_down'] = full['ffn_w_down']
    for k in ('attn_norm', 'ssd_conv_w', 'ssd_conv_b', 'ssd_norm', 'pool_w', 'pool_scale', 'mla_q_norm', 'mla_kv_norm',
              'ffn_norm', 'ffn_conv_w', 'ffn_conv_b'):
        w[k] = small[k][l]
    w['dtb'] = _pad_lanes(small['ssd_dt_bias'][l])
    w['alog'] = _pad_lanes(small['ssd_a_log'][l])
    w['dchan'] = jnp.repeat(small['ssd_d'][l], SSD_HEAD_DIM).reshape(1, SSD_WIDTH)
    w['ssd_norm'] = w['ssd_norm'].reshape(1, SSD_WIDTH)
    return w


def _layer_fwd(x, pos, invf, w, S, l, hosted=None):
    n = lambda s: f"{s}_l{l}"
    h1 = rmsnorm_fwd(x, w['attn_norm'], name=n("attn_norm"))
    proj = matmul(h1, w['w_in_p'], name=n("w_in"))
    xc = ssd_conv_fwd(proj, w['ssd_conv_w'], w['ssd_conv_b'], S, name=n("ssd_conv"))
    ypre, yssd, hin = ssd_fwd(proj, xc, w['dtb'], w['alog'], w['dchan'], w['ssd_norm'], S, name=n("ssd_scan"))
    ypool, pooled = pool_fwd(proj, w['pool_w'], w['pool_scale'], S, name=n("pool"))
    q, k, v, cqn, ckvn = mla_prep_fwd(proj, pos, invf, w['mla_q_norm'], w['w_uq_p'], w['mla_kv_norm'], w['w_ukv_p'],
                                      name=n("mla_prep"))
    o, lse, *exchanged = flash_fwd(q, k, v, S, name=n("attention"), hosted=hosted)
    mix = jnp.concatenate([yssd, ypool, o.astype(MXU_DTYPE)], axis=1)
    x2 = matmul(mix, w['w_out_p'], res=x, name=n("w_out"))
    h2 = rmsnorm_fwd(x2, w['ffn_norm'], name=n("ffn_norm"))
    up = matmul(h2, w['w_up'], name=n("ffn_up"))
    act = ffn_conv_gate_fwd(up, w['ffn_conv_w'], w['ffn_conv_b'], S, name=n("ffn_conv_gate"))
    x3 = matmul(act, w['w_down'], res=x2, name=n("ffn_down"))
    saved = dict(x=x, h1=h1, proj=proj, xc=xc, ypre=ypre, hin=hin, pooled=pooled, q=q, k=k, v=v, cqn=cqn, ckvn=ckvn,
                 o=o, lse=lse, mix=mix, x2=x2, h2=h2, up=up, act=act)
    return x3, saved, exchanged


def _layer_bwd(dx3, pos, invf, w, s, S, l, host=None):
    n = lambda t: f"{t}_l{l}"
    g = {}
    dact = matmul(dx3, w['w_down'], nt=True, name=n("d_ffn_down"))
    g['ffn_w_down'] = matmul_tn(s['act'], dx3, name=n("g_ffn_down"))
    dgate, dval, st = ffn_conv_gate_bwd(s['up'], w['ffn_conv_w'], w['ffn_conv_b'], dact, S, name=n("d_ffn_conv_gate"))
    g['ffn_conv_w'], g['ffn_conv_b'] = st[:FFN_CONV], st[FFN_CONV]
    dup_g = conv_bwd_x(dgate, w['ffn_conv_w'][:, :D_FF], S, FFN_CONV, name=n("d_ffn_conv_g"))
    dup_v = conv_bwd_x(dval, w['ffn_conv_w'][:, D_FF:], S, FFN_CONV, name=n("d_ffn_conv_v"))
    dh2 = matmul(dup_g, w['w_up'], nt=True, kblock=0, name=n("d_ffn_up_g"))
    dh2 = matmul(dup_v, w['w_up'], nt=True, kblock=1, res=dh2, name=n("d_ffn_up_v"))
    g['ffn_w_up'] = jnp.concatenate([matmul_tn(s['h2'], dup_g, name=n("g_ffn_up_g")),
                                     matmul_tn(s['h2'], dup_v, name=n("g_ffn_up_v"))], axis=1)
    dx2, gn = rmsnorm_bwd(s['x2'], w['ffn_norm'], dh2, dx3, name=n("d_ffn_norm"))
    g['ffn_norm'] = gn[0]
    dmix = matmul(dx2, w['w_out_p'], nt=True, name=n("d_w_out"))
    g['w_out'] = _w_out_from_padded(matmul_tn(s['mix'], dx2, name=n("g_w_out")))
    dxc, ddt, dz, sm, gsn = ssd_bwd(s['proj'], s['xc'], s['ypre'], s['hin'], dmix, w['dtb'], w['alog'], w['dchan'],
                                    w['ssd_norm'], S, name=n("d_ssd_scan"))
    g['ssd_a_log'], g['ssd_dt_bias'], g['ssd_d'] = sm[0, :SSD_HEADS], sm[1, :SSD_HEADS], sm[2, :SSD_HEADS]
    g['ssd_norm'] = gsn[0]
    dpre, st = ssd_conv_bwd_pre(s['proj'], w['ssd_conv_w'], w['ssd_conv_b'], dxc, S, name=n("d_ssd_conv_act"))
    g['ssd_conv_w'], g['ssd_conv_b'] = st[:SSD_CONV], st[SSD_CONV]
    dxbc = conv_bwd_x(dpre, w['ssd_conv_w'], S, SSD_CONV, name=n("d_ssd_conv"))
    du, g['pool_w'], gps = pool_bwd(dmix, s['pooled'], w['pool_w'], w['pool_scale'], S, name=n("d_pool"))
    g['pool_scale'] = gps[0]
    dq, dk, dv, *exchanged = flash_bwd(s['q'], s['k'], s['v'], s['o'], s['lse'], dmix, S, name=n("d_attention"),
                                       hosted=host(g) if host else None)
    dqp, dkvp, dcq, dckv, dkpe, gqn, gkn = mla_prep_bwd(s['proj'], pos, invf, w['mla_q_norm'], w['w_uq_p'],
                                                        w['mla_kv_norm'], w['w_ukv_p'], dq, dk, dv, name=n("d_mla_prep"))
    g['mla_q_norm'], g['mla_kv_norm'] = gqn[0], gkn[0]
    g['mla_w_uq'] = _w_uq_from_padded(matmul_tn(s['cqn'], dqp, name=n("g_w_uq")))
    g['mla_w_ukv'] = _w_ukv_from_padded(matmul_tn(s['ckvn'], dkvp, name=n("g_w_ukv")))
    dproj = jnp.concatenate([dz, dxbc, du, dcq, ddt, dckv, dkpe, jnp.zeros_like(dkpe)], axis=1)
    dh1 = matmul(dproj, w['w_in_p'], nt=True, name=n("d_w_in"))
    g['w_in'] = _w_in_from_padded(matmul_tn(s['h1'], dproj, name=n("g_w_in")))
    dx, gn = rmsnorm_bwd(s['x'], w['attn_norm'], dh1, dx2, name=n("d_attn_norm"))
    g['attn_norm'] = gn[0]
    return dx, g, exchanged


def _rope_inputs(positions):
    pos = positions.reshape(-1, 1).astype(F32)
    inv_freq = ROPE_THETA ** (-jnp.arange(0, MLA_ROPE, 2, dtype=F32) / MLA_ROPE)
    invf = jnp.concatenate([jnp.zeros((ROPE0,), F32), inv_freq, inv_freq,
                            jnp.zeros((HEAD_W - ROPE0 - MLA_ROPE,), F32)]).reshape(1, HEAD_W)
    return pos, invf


EARLY_GRADS = ('w_out', 'ffn_w_up', 'ffn_w_down')


def kernel(x, positions, attn_norm, w_in, ssd_conv_w, ssd_conv_b, ssd_dt_bias, ssd_a_log, ssd_d, ssd_norm, pool_w, pool_scale, mla_q_norm, mla_w_uq, mla_kv_norm, mla_w_ukv, w_out, ffn_norm, ffn_w_up, ffn_conv_w, ffn_conv_b, ffn_w_down, final_norm, loss_target, m_attn_norm, m_w_in, m_ssd_conv_w, m_ssd_conv_b, m_ssd_dt_bias, m_ssd_a_log, m_ssd_d, m_ssd_norm, m_pool_w, m_pool_scale, m_mla_q_norm, m_mla_w_uq, m_mla_kv_norm, m_mla_w_ukv, m_w_out, m_ffn_norm, m_ffn_w_up, m_ffn_conv_w, m_ffn_conv_b, m_ffn_w_down, m_final_norm, v_attn_norm, v_w_in, v_ssd_conv_w, v_ssd_conv_b, v_ssd_dt_bias, v_ssd_a_log, v_ssd_d, v_ssd_norm, v_pool_w, v_pool_scale, v_mla_q_norm, v_mla_w_uq, v_mla_kv_norm, v_mla_w_ukv, v_w_out, v_ffn_norm, v_ffn_w_up, v_ffn_conv_w, v_ffn_conv_b, v_ffn_w_down, v_final_norm):
    wv = dict(zip(WEIGHTS, (attn_norm, w_in, ssd_conv_w, ssd_conv_b, ssd_dt_bias, ssd_a_log, ssd_d, ssd_norm, pool_w,
                            pool_scale, mla_q_norm, mla_w_uq, mla_kv_norm, mla_w_ukv, w_out, ffn_norm, ffn_w_up,
                            ffn_conv_w, ffn_conv_b, ffn_w_down, final_norm)))
    mv = dict(zip(WEIGHTS, (m_attn_norm, m_w_in, m_ssd_conv_w, m_ssd_conv_b, m_ssd_dt_bias, m_ssd_a_log, m_ssd_d,
                            m_ssd_norm, m_pool_w, m_pool_scale, m_mla_q_norm, m_mla_w_uq, m_mla_kv_norm, m_mla_w_ukv,
                            m_w_out, m_ffn_norm, m_ffn_w_up, m_ffn_conv_w, m_ffn_conv_b, m_ffn_w_down, m_final_norm)))
    vv = dict(zip(WEIGHTS, (v_attn_norm, v_w_in, v_ssd_conv_w, v_ssd_conv_b, v_ssd_dt_bias, v_ssd_a_log, v_ssd_d,
                            v_ssd_norm, v_pool_w, v_pool_scale, v_mla_q_norm, v_mla_w_uq, v_mla_kv_norm, v_mla_w_ukv,
                            v_w_out, v_ffn_norm, v_ffn_w_up, v_ffn_conv_w, v_ffn_conv_b, v_ffn_w_down, v_final_norm)))
    Bl, S, D = x.shape
    chip = 2 * lax.axis_index("x") + lax.axis_index("y")
    core = lax.axis_index("c")

    big_names = list(BIG)
    shards = [[wv[k][l].astype(MXU_DTYPE) for k in big_names] for l in range(DEPTH)]

    def whole_weights(own, others):
        rel = [relation_of(j, chip) for j in range(N_CHIPS)]
        return {k: jnp.concatenate(
            [jnp.where(r < 0, mine, jnp.where(r == 0, theirs[0], jnp.where(r == 1, theirs[1], theirs[2]))) for r in rel],
            axis=BIG[k] - 1) for k, mine, theirs in zip(big_names, own, others)}

    others0 = run_exchange(gather_exchange(shards[0]), name="gather_weights_l0")
    placed = []
    for k in CONV_SHARDED:
        sh = wv[k]
        whole = jnp.zeros(sh.shape[:-1] + (sh.shape[-1] * N_CHIPS,), F32)
        whole = lax.dynamic_update_slice_in_dim(whole, sh, chip * sh.shape[-1], axis=sh.ndim - 1)
        placed.append(jnp.where(core == 1, whole, 0.0))
    conv_full = _unpack_rows(all_sum_small(_pack_rows(placed, LANE, F32), name="gather_conv_weights"),
                             [p.shape for p in placed])
    small = {k: wv[k] for k in WEIGHTS if k not in BIG}
    small.update(dict(zip(CONV_SHARDED, conv_full)))

    T = Bl * S
    pos, invf = _rope_inputs(positions)
    group_a = [(k, 1) for k in big_names] + [(k, 0) for k in EARLY_GRADS]
    group_b = [(k, 0) for k in big_names if k not in EARLY_GRADS]

    def scatter_of(group, layer_grads):
        send = [_split_for_chips(layer_grads[l][k], BIG[k] - 1) for k, l in group]
        return send, scatter_exchange(send)

    layer_grads = [None] * DEPTH
    sent = {}
    w0 = _layer_weights(whole_weights(shards[0], others0), small, 0)
    h, saved0, others1 = _layer_fwd(x.reshape(T, D), pos, invf, w0, S, 0, hosted=gather_exchange(shards[1]))
    w1 = _layer_weights(whole_weights(shards[1], others1), small, 1)
    h, saved1, _ = _layer_fwd(h, pos, invf, w1, S, 1)
    loss, dh, g_final_norm = final_loss(h, small['final_norm'], loss_target.reshape(T, D))
    dh, layer_grads[1], _ = _layer_bwd(dh, pos, invf, w1, saved1, S, 1)

    def host_a(early):
        layer_grads[0] = early
        sent['a'], ex = scatter_of(group_a, layer_grads)
        return ex

    dx, layer_grads[0], others_a = _layer_bwd(dh, pos, invf, w0, saved0, S, 0, host=host_a)
    sent['b'], ex_b = scatter_of(group_b, layer_grads)
    others_b = run_exchange(ex_b, name="scatter_grads_b")
    small_names = [k for k in WEIGHTS if k not in BIG]
    grads = {k: jnp.stack([layer_grads[l][k] for l in range(DEPTH)]) for k in small_names if k != 'final_norm'}
    grads['final_norm'] = g_final_norm[0]

    pieces = [{}, {}]
    for tag, group, others in (('a', group_a, others_a), ('b', group_b, others_b)):
        mine = [sum_chips(lax.dynamic_index_in_dim(s, chip, 0, keepdims=False), o, name=f"sum_chips_{k}_l{l}")
                for s, o, (k, l) in zip(sent[tag], others, group)]
        theirs = sibling_swap(mine, name=f"swap_core_sums_{tag}")
        pieces[0].update(dict(zip(group, mine)))
        pieces[1].update(dict(zip(group, theirs)))
    small_sum = all_sum_small(_pack_rows([grads[k] for k in small_names] + [loss[0, :1]], LANE, F32), name="sum_small_grads")
    summed = _unpack_rows(small_sum, [grads[k].shape for k in small_names] + [(1,)])
    loss_total = summed[-1].reshape(())
    g_small = dict(zip(small_names, summed[:-1]))
    for k in CONV_SHARDED:
        n = wv[k].shape[-1]
        g_small[k] = lax.dynamic_slice_in_dim(g_small[k], chip * n, n, axis=g_small[k].ndim - 1)

    out_g, out_d, out_m, out_v = {}, {}, {}, {}
    for k in big_names:
        g_layers = [[pieces[0][(k, l)], pieces[1][(k, l)]] for l in range(DEPTH)]
        out_g[k], out_d[k], out_m[k], out_v[k] = adamw_layers(wv[k], g_layers, mv[k], vv[k], name=f"adamw_{k}")
    at_least_2d = lambda a: a.reshape(1, -1) if a.ndim == 1 else a
    res = adamw_small(*[[at_least_2d(d[k]) for k in small_names] for d in (wv, g_small, mv, vv)], name="adamw_small")
    out_g.update(g_small)
    for dst, r in zip((out_d, out_m, out_v), res):
        dst.update({k: a.reshape(wv[k].shape) for k, a in zip(small_names, r)})
    return (loss_total, dx.reshape(Bl, S, D), *[out_g[k] for k in WEIGHTS], *[out_d[k] for k in WEIGHTS],
            *[out_m[k] for k in WEIGHTS], *[out_v[k] for k in WEIGHTS])
```

```python
import functools
import math
from typing import Callable, NamedTuple

import jax
import jax.numpy as jnp
from jax import lax
from jax.experimental import pallas as pl
from jax.experimental.pallas import tpu as pltpu

F32 = jnp.float32
MXU_DTYPE = jnp.bfloat16
HI = lax.Precision.HIGHEST

D_MODEL = 1024
DEPTH = 2
EPS = 1e-6
SSD_HEADS = 16
SSD_HEAD_DIM = 64
SSD_WIDTH = 1024
SSD_GROUPS = 2
SSD_STATE = 128
SSD_CONV = 4
SSD_CHUNK = 128
SSD_CONV_CH = 1536
POOL_GROUPS = 4
POOL_GROUP_DIM = 128
POOL_WIDTH = 512
POOL_WINDOWS = (2, 4, 8, 16)
MLA_HEADS = 8
MLA_Q_RANK = 384
MLA_KV_RANK = 256
MLA_NOPE = 64
MLA_ROPE = 32
MLA_V = 64
MLA_QK = 96
MLA_WIDTH = 512
ROPE_THETA = 10000.0
MIX_WIDTH = 2048
IN_COLS = 3760
D_FF = 2816
FFN_CONV = 3
ADAM_LR = 0.001
ADAM_B1 = 0.9
ADAM_B2 = 0.999
ADAM_EPS = 1e-08
ADAM_WD = 0.01
ADAM_STEP = 10

LANE = 128
HALO = 8
POOL_HALO = 16
PZ0 = 0
PXBC0 = 1024
PU0 = 2560
PCQ0 = 3072
PDT0 = 3456
PCKV0 = 3584
PKPE0 = 3840
PROJ_W = 4096
HEAD_W = 128
MLA_PAD = MLA_HEADS * HEAD_W
MIXP = SSD_WIDTH + POOL_WIDTH + MLA_PAD
N_CHIPS = 4
N_DEV = 8
VMEM_LIMIT = 56 * 1024 * 1024


def _cparams(dims, vmem=None):
    return pltpu.CompilerParams(dimension_semantics=dims, vmem_limit_bytes=vmem or VMEM_LIMIT)


def _sds(shape, dtype):
    return jax.ShapeDtypeStruct(tuple(shape), dtype)


def _mx(v):
    return v.astype(MXU_DTYPE)


def _dot(a, b):
    return jnp.dot(_mx(a), _mx(b), preferred_element_type=F32)


def _dot_nt(a, b):
    return lax.dot_general(_mx(a), _mx(b), (((1,), (1,)), ((), ())), preferred_element_type=F32)


def _dot_tn(a, b):
    return lax.dot_general(_mx(a), _mx(b), (((0,), (0,)), ((), ())), preferred_element_type=F32)


def _dot_hi(a, b):
    return jnp.dot(a, b, preferred_element_type=F32, precision=HI)


def _sigmoid(v):
    return 1.0 / (1.0 + jnp.exp(-v))


def _pick(n, prefs):
    for p in prefs:
        if n % p == 0:
            return p
    return n


def matmul(a, b, *, res=None, out_dtype=F32, name, nt=False, kblock=0, tm=None, tn=None):
    M, K = a.shape
    N = b.shape[0] if nt else b.shape[1]
    assert (b.shape[1] % K == 0) if nt else (K == b.shape[0] and kblock == 0)
    tm = tm or _pick(M, (1024, 512, 256, 128))
    tn = tn or _pick(N, (512, 1408, 256, 128))

    def body(*refs):
        a_ref, b_ref = refs[:2]
        o_ref = refs[-1]
        out = (_dot_nt if nt else _dot)(a_ref[...], b_ref[...])
        if res is not None:
            out = out + refs[2][...]
        o_ref[...] = out.astype(out_dtype)

    b_spec = pl.BlockSpec((tn, K), lambda i, j: (j, kblock)) if nt else pl.BlockSpec((K, tn), lambda i, j: (0, j))
    in_specs = [pl.BlockSpec((tm, K), lambda i, j: (i, 0)), b_spec]
    args = [a, b]
    if res is not None:
        in_specs.append(pl.BlockSpec((tm, tn), lambda i, j: (i, j)))
        args.append(res)
    return pl.pallas_call(
        body, name=name, grid=(M // tm, N // tn), in_specs=in_specs,
        out_specs=pl.BlockSpec((tm, tn), lambda i, j: (i, j)), out_shape=_sds((M, N), out_dtype),
        compiler_params=_cparams(("parallel", "parallel")),
    )(*args)


def matmul_tn(a, g, *, name, tm=None, tn=None, tk=None):
    T, M = a.shape
    T2, N = g.shape
    assert T == T2
    tm = tm or _pick(M, (1408, 1280, 1024, 512, 384, 256, 128))
    tn = tn or _pick(N, (1024, 1408, 512, 256, 128))
    tk = tk or _pick(T, (512, 256, 128))
    nk = T // tk

    def body(a_ref, g_ref, o_ref, acc):
        k = pl.program_id(2)
        part = _dot_tn(a_ref[...], g_ref[...])

        @pl.when(k == 0)
        def _():
            acc[...] = part

        @pl.when(k > 0)
        def _():
            acc[...] += part

        @pl.when(k == nk - 1)
        def _():
            o_ref[...] = acc[...].astype(o_ref.dtype)

    return pl.pallas_call(
        body, name=name, grid=(M // tm, N // tn, nk),
        in_specs=[pl.BlockSpec((tk, tm), lambda i, j, k: (k, i)), pl.BlockSpec((tk, tn), lambda i, j, k: (k, j))],
        out_specs=pl.BlockSpec((tm, tn), lambda i, j, k: (i, j)), out_shape=_sds((M, N), MXU_DTYPE),
        scratch_shapes=[pltpu.VMEM((tm, tn), F32)],
        compiler_params=_cparams(("parallel", "parallel", "arbitrary")),
    )(a, g)


def rmsnorm_fwd(x, gamma, *, name, tm=512):
    T, D = x.shape
    tm = _pick(T, (tm, 256, 128))

    def body(x_ref, g_ref, o_ref):
        xv = x_ref[...]
        r = lax.rsqrt(jnp.mean(xv * xv, axis=-1, keepdims=True) + EPS)
        o_ref[...] = ((xv * r) * g_ref[...]).astype(MXU_DTYPE)

    return pl.pallas_call(
        body, name=name, grid=(T // tm,),
        in_specs=[pl.BlockSpec((tm, D), lambda i: (i, 0)), pl.BlockSpec((1, D), lambda i: (0, 0))],
        out_specs=pl.BlockSpec((tm, D), lambda i: (i, 0)), out_shape=_sds((T, D), MXU_DTYPE),
        compiler_params=_cparams(("parallel",)),
    )(x, gamma.reshape(1, D))


def _rms_bwd_tile(xv, gamma, dh):
    r = lax.rsqrt(jnp.mean(xv * xv, axis=-1, keepdims=True) + EPS)
    xh = xv * r
    dg = jnp.sum(dh * xh, axis=0, keepdims=True)
    dn = dh * gamma
    dx = r * (dn - xh * jnp.mean(dn * xh, axis=-1, keepdims=True))
    return dx, dg


def rmsnorm_bwd(x, gamma, dh, dres, *, name, tm=256):
    T, D = x.shape
    tm = _pick(T, (tm, 128))

    def body(x_ref, g_ref, dh_ref, dr_ref, dx_ref, dg_ref):
        dx, dg = _rms_bwd_tile(x_ref[...], g_ref[...], dh_ref[...].astype(F32))
        dx_ref[...] = dx + dr_ref[...]

        @pl.when(pl.program_id(0) == 0)
        def _():
            dg_ref[...] = dg

        @pl.when(pl.program_id(0) > 0)
        def _():
            dg_ref[...] += dg

    row = pl.BlockSpec((tm, D), lambda i: (i, 0))
    vec = pl.BlockSpec((1, D), lambda i: (0, 0))
    return pl.pallas_call(
        body, name=name, grid=(T // tm,), in_specs=[row, vec, row, row], out_specs=[row, vec],
        out_shape=[_sds((T, D), F32), _sds((1, D), F32)], compiler_params=_cparams(("arbitrary",)),
    )(x, gamma.reshape(1, D), dh, dres)


def final_loss(x, gamma, target, *, name="final_loss", tm=256):
    T, D = x.shape
    tm = _pick(T, (tm, 128))

    def body(x_ref, g_ref, t_ref, l_ref, dx_ref, dg_ref):
        xv = x_ref[...]
        gam = g_ref[...]
        r = lax.rsqrt(jnp.mean(xv * xv, axis=-1, keepdims=True) + EPS)
        y = (xv * r) * gam
        err = y - t_ref[...]
        part = 0.5 * jnp.sum(jnp.sum(err * err, axis=-1, keepdims=True) / D, axis=0, keepdims=True)
        dx, dg = _rms_bwd_tile(xv, gam, err / D)
        dx_ref[...] = dx

        @pl.when(pl.program_id(0) == 0)
        def _():
            dg_ref[...] = dg
            l_ref[...] = jnp.broadcast_to(part, l_ref.shape)

        @pl.when(pl.program_id(0) > 0)
        def _():
            dg_ref[...] += dg
            l_ref[...] += jnp.broadcast_to(part, l_ref.shape)

    row = pl.BlockSpec((tm, D), lambda i: (i, 0))
    vec = pl.BlockSpec((1, D), lambda i: (0, 0))
    return pl.pallas_call(
        body, name=name, grid=(T // tm,), in_specs=[row, vec, row],
        out_specs=[pl.BlockSpec((1, LANE), lambda i: (0, 0)), row, vec],
        out_shape=[_sds((1, LANE), F32), _sds((T, D), F32), _sds((1, D), F32)],
        compiler_params=_cparams(("arbitrary",)),
    )(x, gamma.reshape(1, D), target)


def _halo_prev(ts):
    return lambda i, j, off=0: (jnp.maximum(i * (ts // HALO) - 1, 0), j + off)


def _cat_prev(cur, halo, first):
    return jnp.concatenate([jnp.where(first, 0.0, halo), cur], axis=0)


def _cat_next(cur, halo, last):
    return jnp.concatenate([cur, jnp.where(last, 0.0, halo)], axis=0)


def _delayed(cat, r):
    if r == 0:
        return cat[HALO:]
    return pltpu.roll(cat, r, axis=0)[HALO:]


def _advanced(cat, r):
    n = cat.shape[0]
    if r == 0:
        return cat[:n - HALO]
    return pltpu.roll(cat, n - r, axis=0)[:n - HALO]


def _conv_pre(cat, w, b, K):
    acc = _delayed(cat, K - 1) * w[0:1, :] + b
    for k in range(1, K):
        acc = acc + _delayed(cat, K - 1 - k) * w[k:k + 1, :]
    return acc


def _pad_rows8(w):
    return jnp.pad(w, ((0, 8 - w.shape[0]), (0, 0)))


def ssd_conv_fwd(proj, w, b, S, *, name, ts=1024, tc=512):
    T = proj.shape[0]
    C, K = SSD_CONV_CH, SSD_CONV
    ts = _pick(S, (ts, 256, 128))
    off = PXBC0 // tc
    ns = S // ts

    def body(x_ref, h_ref, w_ref, b_ref, o_ref):
        first = (pl.program_id(0) % ns) == 0
        pre = _conv_pre(_cat_prev(x_ref[...], h_ref[...], first), w_ref[...], b_ref[...], K)
        o_ref[...] = pre * _sigmoid(pre)

    return pl.pallas_call(
        body, name=name, grid=(T // ts, C // tc),
        in_specs=[pl.BlockSpec((ts, tc), lambda i, j: (i, j + off)),
                  pl.BlockSpec((HALO, tc), functools.partial(_halo_prev(ts), off=off)),
                  pl.BlockSpec((8, tc), lambda i, j: (0, j)), pl.BlockSpec((1, tc), lambda i, j: (0, j))],
        out_specs=pl.BlockSpec((ts, tc), lambda i, j: (i, j)), out_shape=_sds((T, C), F32),
        compiler_params=_cparams(("parallel", "parallel")),
    )(proj, proj, _pad_rows8(w), b.reshape(1, C))


def ssd_conv_bwd_pre(proj, w, b, dxc, S, *, name, ts=1024, tc=512):
    T = proj.shape[0]
    C, K = SSD_CONV_CH, SSD_CONV
    ts = _pick(S, (ts, 256, 128))
    off = PXBC0 // tc
    ns = S // ts

    def body(x_ref, h_ref, w_ref, b_ref, d_ref, o_ref, acc_ref):
        i = pl.program_id(1)
        first = (i % ns) == 0
        cat = _cat_prev(x_ref[...], h_ref[...], first)
        pre = _conv_pre(cat, w_ref[...], b_ref[...], K)
        sg = _sigmoid(pre)
        dpre = d_ref[...] * (sg * (1.0 + pre * (1.0 - sg)))
        o_ref[...] = dpre
        rows = [jnp.sum(dpre * _delayed(cat, K - 1 - k), axis=0, keepdims=True) for k in range(K)]
        rows.append(jnp.sum(dpre, axis=0, keepdims=True))
        rows.append(jnp.zeros((8 - len(rows), dpre.shape[1]), F32))
        part = jnp.concatenate(rows, axis=0)

        @pl.when(i == 0)
        def _():
            acc_ref[...] = part

        @pl.when(i > 0)
        def _():
            acc_ref[...] += part

    hp = _halo_prev(ts)
    return pl.pallas_call(
        body, name=name, grid=(C // tc, T // ts),
        in_specs=[pl.BlockSpec((ts, tc), lambda j, i: (i, j + off)),
                  pl.BlockSpec((HALO, tc), lambda j, i: hp(i, j, off)),
                  pl.BlockSpec((8, tc), lambda j, i: (0, j)), pl.BlockSpec((1, tc), lambda j, i: (0, j)),
                  pl.BlockSpec((ts, tc), lambda j, i: (i, j))],
        out_specs=[pl.BlockSpec((ts, tc), lambda j, i: (i, j)), pl.BlockSpec((8, tc), lambda j, i: (0, j))],
        out_shape=[_sds((T, C), F32), _sds((8, C), F32)],
        compiler_params=_cparams(("parallel", "arbitrary")),
    )(proj, proj, _pad_rows8(w), b.reshape(1, C), dxc)


def conv_bwd_x(dpre, w, S, K, *, name, ts=512, tc=None):
    T, C = dpre.shape
    out_dtype = MXU_DTYPE
    ts = _pick(S, (ts, 256, 128))
    tc = tc or _pick(C, (1408, 512, 256, 128))
    ns = S // ts
    nblk = T // HALO

    def body(d_ref, h_ref, w_ref, o_ref):
        last = (pl.program_id(0) % ns) == ns - 1
        cat = _cat_next(d_ref[...], h_ref[...], last)
        wv = w_ref[...]
        acc = _advanced(cat, K - 1) * wv[0:1, :]
        for k in range(1, K):
            acc = acc + _advanced(cat, K - 1 - k) * wv[k:k + 1, :]
        o_ref[...] = acc.astype(out_dtype)

    return pl.pallas_call(
        body, name=name, grid=(T // ts, C // tc),
        in_specs=[pl.BlockSpec((ts, tc), lambda i, j: (i, j)),
                  pl.BlockSpec((HALO, tc), lambda i, j: (jnp.minimum((i + 1) * (ts // HALO), nblk - 1), j)),
                  pl.BlockSpec((8, tc), lambda i, j: (0, j))],
        out_specs=pl.BlockSpec((ts, tc), lambda i, j: (i, j)), out_shape=_sds((T, C), out_dtype),
        compiler_params=_cparams(("parallel", "parallel")),
    )(dpre, dpre, _pad_rows8(w))


def ffn_conv_gate_fwd(up, w, b, S, *, name, ts=512, tc=1408):
    T, C2 = up.shape
    C, K = C2 // 2, FFN_CONV
    ts = _pick(S, (ts, 256, 128))
    nj = C // tc
    ns = S // ts
    w8 = _pad_rows8(w)
    b2 = b.reshape(1, C2)

    def body(g_ref, gh_ref, v_ref, vh_ref, wg_ref, wv_ref, bg_ref, bv_ref, o_ref):
        first = (pl.program_id(0) % ns) == 0
        g = _conv_pre(_cat_prev(g_ref[...], gh_ref[...], first), wg_ref[...], bg_ref[...], K)
        v = _conv_pre(_cat_prev(v_ref[...], vh_ref[...], first), wv_ref[...], bv_ref[...], K)
        o_ref[...] = (g * _sigmoid(g) * v).astype(o_ref.dtype)

    hp = _halo_prev(ts)
    return pl.pallas_call(
        body, name=name, grid=(T // ts, nj),
        in_specs=[pl.BlockSpec((ts, tc), lambda i, j: (i, j)), pl.BlockSpec((HALO, tc), lambda i, j: hp(i, j)),
                  pl.BlockSpec((ts, tc), lambda i, j: (i, j + nj)), pl.BlockSpec((HALO, tc), lambda i, j: hp(i, j, nj)),
                  pl.BlockSpec((8, tc), lambda i, j: (0, j)), pl.BlockSpec((8, tc), lambda i, j: (0, j + nj)),
                  pl.BlockSpec((1, tc), lambda i, j: (0, j)), pl.BlockSpec((1, tc), lambda i, j: (0, j + nj))],
        out_specs=pl.BlockSpec((ts, tc), lambda i, j: (i, j)), out_shape=_sds((T, C), MXU_DTYPE),
        compiler_params=_cparams(("parallel", "parallel")),
    )(up, up, up, up, w8, w8, b2, b2)


def ffn_conv_gate_bwd(up, w, b, dact, S, *, name, ts=256, tc=1408):
    T, C2 = up.shape
    C, K = C2 // 2, FFN_CONV
    ts = _pick(S, (ts, 256, 128))
    nj = C // tc
    ns = S // ts
    nblk = T // HALO
    w8 = _pad_rows8(w)
    b2 = b.reshape(1, C2)

    def stats(dpre, cat):
        rows = [jnp.sum(dpre[:ts] * _delayed(cat, K - 1 - k)[:ts], axis=0, keepdims=True) for k in range(K)]
        rows.append(jnp.sum(dpre[:ts], axis=0, keepdims=True))
        rows.append(jnp.zeros((8 - len(rows), dpre.shape[1]), F32))
        return jnp.concatenate(rows, axis=0)

    def conv_t(dpre, wv):
        acc = _advanced(dpre, K - 1) * wv[0:1, :]
        for k in range(1, K):
            acc = acc + _advanced(dpre, K - 1 - k) * wv[k:k + 1, :]
        return acc

    def body(g_ref, gp_ref, gn_ref, v_ref, vp_ref, vn_ref, wg_ref, wv_ref, bg_ref, bv_ref, d_ref, dn_ref,
             dug_ref, duv_ref, ag_ref, av_ref):
        i = pl.program_id(1)
        first = (i % ns) == 0
        last = (i % ns) == ns - 1
        gcat = jnp.concatenate([jnp.where(first, 0.0, gp_ref[...]), g_ref[...], gn_ref[...]], axis=0)
        vcat = jnp.concatenate([jnp.where(first, 0.0, vp_ref[...]), v_ref[...], vn_ref[...]], axis=0)
        wg, wv = wg_ref[...], wv_ref[...]
        g = _conv_pre(gcat, wg, bg_ref[...], K)
        v = _conv_pre(vcat, wv, bv_ref[...], K)
        d = _cat_next(d_ref[...], dn_ref[...], last)
        sg = _sigmoid(g)
        dg = d * v * (sg * (1.0 + g * (1.0 - sg)))
        dv = d * (g * sg)
        dug_ref[...] = conv_t(dg, wg).astype(dug_ref.dtype)
        duv_ref[...] = conv_t(dv, wv).astype(duv_ref.dtype)
        sgp, svp = stats(dg, gcat), stats(dv, vcat)

        @pl.when(i == 0)
        def _():
            ag_ref[...] = sgp
            av_ref[...] = svp

        @pl.when(i > 0)
        def _():
            ag_ref[...] += sgp
            av_ref[...] += svp

    hp = _halo_prev(ts)
    hn = lambda i: jnp.minimum((i + 1) * (ts // HALO), nblk - 1)
    cur = lambda off: pl.BlockSpec((ts, tc), lambda j, i: (i, j + off))
    prv = lambda off: pl.BlockSpec((HALO, tc), lambda j, i: hp(i, j, off))
    nxt = lambda off: pl.BlockSpec((HALO, tc), lambda j, i: (hn(i), j + off))
    row = lambda r, off: pl.BlockSpec((r, tc), lambda j, i: (0, j + off))
    dug, duv, ag, av = pl.pallas_call(
        body, name=name, grid=(nj, T // ts),
        in_specs=[cur(0), prv(0), nxt(0), cur(nj), prv(nj), nxt(nj), row(8, 0), row(8, nj), row(1, 0), row(1, nj),
                  cur(0), nxt(0)],
        out_specs=[cur(0), cur(0), row(8, 0), row(8, 0)],
        out_shape=[_sds((T, C), MXU_DTYPE), _sds((T, C), MXU_DTYPE), _sds((8, C), F32), _sds((8, C), F32)],
        compiler_params=_cparams(("parallel", "arbitrary")),
    )(up, up, up, up, up, up, w8, w8, b2, b2, dact, dact)
    return dug, duv, jnp.concatenate([ag, av], axis=1)


def _pool_counts(pos, w):
    return jnp.minimum(pos + 1.0, float(w))


def pool_fwd(proj, pool_w, pool_scale, S, *, name, ts=512):
    T = proj.shape[0]
    C, G, GD, H = POOL_WIDTH, POOL_GROUPS, POOL_GROUP_DIM, POOL_HALO
    ts = _pick(S, (ts, 256, 128))
    ns = S // ts
    off = PU0 // C

    def body(u_ref, h_ref, w_ref, s_ref, y_ref, p_ref):
        i = pl.program_id(0)
        first = (i % ns) == 0
        cat = jnp.concatenate([jnp.where(first, 0.0, h_ref[...]), u_ref[...]], axis=0)
        pos = ((i % ns) * ts + lax.broadcasted_iota(jnp.int32, (ts, 1), 0)).astype(F32)
        sums = cat
        win = 1
        for g, wlen in enumerate(POOL_WINDOWS):
            while win < wlen:
                sums = sums + pltpu.roll(sums, win, axis=0)
                win *= 2
            sl = slice(g * GD, (g + 1) * GD)
            pooled = sums[H:, sl] / _pool_counts(pos, wlen) - cat[H:, sl]
            p_ref[:, sl] = pooled.astype(p_ref.dtype)
            y_ref[:, sl] = (_dot(pooled, w_ref[g]) * s_ref[:, sl]).astype(y_ref.dtype)

    return pl.pallas_call(
        body, name=name, grid=(T // ts,),
        in_specs=[pl.BlockSpec((ts, C), lambda i: (i, off)),
                  pl.BlockSpec((H, C), lambda i: (jnp.maximum(i * (ts // H) - 1, 0), off)),
                  pl.BlockSpec((G, GD, GD), lambda i: (0, 0, 0)), pl.BlockSpec((1, C), lambda i: (0, 0))],
        out_specs=[pl.BlockSpec((ts, C), lambda i: (i, 0)), pl.BlockSpec((ts, C), lambda i: (i, 0))],
        out_shape=[_sds((T, C), MXU_DTYPE), _sds((T, C), MXU_DTYPE)],
        compiler_params=_cparams(("parallel",)),
    )(proj, proj, _mx(pool_w), pool_scale.reshape(1, C))


def pool_bwd(dmix, pooled, pool_w, pool_scale, S, *, name, ts=512):
    T = dmix.shape[0]
    C, G, GD, H = POOL_WIDTH, POOL_GROUPS, POOL_GROUP_DIM, POOL_HALO
    ts = _pick(S, (ts, 256, 128))
    ns = S // ts
    off = SSD_WIDTH // C
    nblk = T // H

    def body(d_ref, dh_ref, p_ref, w_ref, s_ref, du_ref, dw_ref, ds_ref):
        i = pl.program_id(0)
        last = (i % ns) == ns - 1
        dcat = jnp.concatenate([d_ref[...], jnp.where(last, 0.0, dh_ref[...])], axis=0)
        n = ts + H
        pos = ((i % ns) * ts + lax.broadcasted_iota(jnp.int32, (n, 1), 0)).astype(F32)
        dws, dss = [], []
        for g, wlen in enumerate(POOL_WINDOWS):
            sl = slice(g * GD, (g + 1) * GD)
            wg = w_ref[g]
            pg = p_ref[:, sl]
            dys = dcat[:, sl] * s_ref[:, sl]
            dss.append(jnp.sum(dcat[:ts, sl] * _dot(pg, wg), axis=0, keepdims=True))
            dws.append(_dot_tn(pg, dys[:ts]))
            dp = _dot_nt(dys, wg)
            q = dp / _pool_counts(pos, wlen)
            win = 1
            while win < wlen:
                q = q + pltpu.roll(q, n - win, axis=0)
                win *= 2
            du_ref[:, sl] = (q[:ts] - dp[:ts]).astype(du_ref.dtype)
        dsp = jnp.concatenate(dss, axis=1)

        @pl.when(i == 0)
        def _():
            for g in range(G):
                dw_ref[g] = dws[g]
            ds_ref[...] = dsp

        @pl.when(i > 0)
        def _():
            for g in range(G):
                dw_ref[g] += dws[g]
            ds_ref[...] += dsp

    return pl.pallas_call(
        body, name=name, grid=(T // ts,),
        in_specs=[pl.BlockSpec((ts, C), lambda i: (i, off)),
                  pl.BlockSpec((H, C), lambda i: (jnp.minimum((i + 1) * (ts // H), nblk - 1), off)),
                  pl.BlockSpec((ts, C), lambda i: (i, 0)),
                  pl.BlockSpec((G, GD, GD), lambda i: (0, 0, 0)), pl.BlockSpec((1, C), lambda i: (0, 0))],
        out_specs=[pl.BlockSpec((ts, C), lambda i: (i, 0)), pl.BlockSpec((G, GD, GD), lambda i: (0, 0, 0)),
                   pl.BlockSpec((1, C), lambda i: (0, 0))],
        out_shape=[_sds((T, C), MXU_DTYPE), _sds((G, GD, GD), F32), _sds((1, C), F32)],
        compiler_params=_cparams(("arbitrary",)),
    )(dmix, dmix, pooled, _mx(pool_w), pool_scale.reshape(1, C))


ROPE0 = MLA_NOPE
ROPE_HALF = MLA_ROPE // 2


def _rope_tables(pos, invf):
    lane = lax.broadcasted_iota(jnp.int32, (1, HEAD_W), 1)
    ang = pos * invf
    cs, sn = jnp.cos(ang), jnp.sin(ang)
    in_a = (lane >= ROPE0) & (lane < ROPE0 + ROPE_HALF)
    in_b = (lane >= ROPE0 + ROPE_HALF) & (lane < ROPE0 + MLA_ROPE)
    return jnp.where(in_a | in_b, cs, 1.0), jnp.where(in_a, -sn, 0.0), jnp.where(in_b, sn, 0.0), in_a | in_b


def _rope(v, cosf, sin_a, sin_b):
    return (v * cosf + pltpu.roll(v, HEAD_W - ROPE_HALF, axis=1) * sin_a + pltpu.roll(v, ROPE_HALF, axis=1) * sin_b)


def _unrope(d, cosf, sin_a, sin_b):
    return (d * cosf + pltpu.roll(d * sin_a, ROPE_HALF, axis=1) + pltpu.roll(d * sin_b, HEAD_W - ROPE_HALF, axis=1))


def _rms_tile(xv, gamma):
    return (xv * lax.rsqrt(jnp.mean(xv * xv, axis=-1, keepdims=True) + EPS)) * gamma


def mla_prep_fwd(proj, pos, invf, q_norm, w_uq_p, kv_norm, w_ukv_p, *, name, tm=256):
    T = proj.shape[0]
    tm = _pick(T, (tm, 128))
    QR, KR, P = MLA_Q_RANK, MLA_KV_RANK, MLA_PAD

    def body(cq_ref, ckv_ref, kpe_ref, pos_ref, invf_ref, qn_ref, wq_ref, kn_ref, wkv_ref,
             q_ref, k_ref, v_ref, cqn_ref, ckvn_ref):
        cosf, sin_a, sin_b, _ = _rope_tables(pos_ref[...], invf_ref[...])
        cqn = _rms_tile(cq_ref[...], qn_ref[...]).astype(MXU_DTYPE)
        ckvn = _rms_tile(ckv_ref[...], kn_ref[...]).astype(MXU_DTYPE)
        cqn_ref[...] = cqn
        ckvn_ref[...] = ckvn
        qp = _dot(cqn, wq_ref[...])
        kvp = _dot(ckvn, wkv_ref[...])
        kpe = _rope(kpe_ref[...], cosf, sin_a, sin_b)
        for h in range(MLA_HEADS):
            sl = slice(h * HEAD_W, (h + 1) * HEAD_W)
            q_ref[:, sl] = (_rope(qp[:, sl], cosf, sin_a, sin_b) * ATTN_SCALE).astype(q_ref.dtype)
            k_ref[:, sl] = (kvp[:, sl] + kpe).astype(k_ref.dtype)
            v_ref[:, sl] = kvp[:, P + h * HEAD_W:P + (h + 1) * HEAD_W].astype(v_ref.dtype)

    row = lambda w: pl.BlockSpec((tm, w), lambda i: (i, 0))
    full = lambda a, b: pl.BlockSpec((a, b), lambda i: (0, 0))
    return pl.pallas_call(
        body, name=name, grid=(T // tm,),
        in_specs=[pl.BlockSpec((tm, QR), lambda i: (i, PCQ0 // QR)), pl.BlockSpec((tm, KR), lambda i: (i, PCKV0 // KR)),
                  pl.BlockSpec((tm, LANE), lambda i: (i, PKPE0 // LANE)), row(1), full(1, LANE),
                  full(1, QR), full(QR, P), full(1, KR), full(KR, 2 * P)],
        out_specs=[row(P), row(P), row(P), row(QR), row(KR)],
        out_shape=[_sds((T, P), MXU_DTYPE)] * 3 + [_sds((T, QR), MXU_DTYPE), _sds((T, KR), MXU_DTYPE)],
        compiler_params=_cparams(("parallel",)),
    )(proj, proj, proj, pos, invf, q_norm.reshape(1, QR), w_uq_p, kv_norm.reshape(1, KR), w_ukv_p)


def mla_prep_bwd(proj, pos, invf, q_norm, w_uq_p, kv_norm, w_ukv_p, dq, dk, dv, *, name, tm=256):
    T = proj.shape[0]
    tm = _pick(T, (tm, 128))
    QR, KR, P = MLA_Q_RANK, MLA_KV_RANK, MLA_PAD

    def body(cq_ref, ckv_ref, pos_ref, invf_ref, qn_ref, wq_ref, kn_ref, wkv_ref, dq_ref, dk_ref, dv_ref,
             dqp_ref, dkvp_ref, dcq_ref, dckv_ref, dkpe_ref, dqn_ref, dkn_ref):
        cosf, sin_a, sin_b, rot = _rope_tables(pos_ref[...], invf_ref[...])
        dkpe = jnp.zeros((tm, HEAD_W), F32)
        for h in range(MLA_HEADS):
            sl = slice(h * HEAD_W, (h + 1) * HEAD_W)
            dqp_ref[:, sl] = _unrope(dq_ref[:, sl] * ATTN_SCALE, cosf, sin_a, sin_b).astype(dqp_ref.dtype)
            dkh = dk_ref[:, sl]
            dkpe = dkpe + dkh
            dkvp_ref[:, sl] = dkh.astype(dkvp_ref.dtype)
            dkvp_ref[:, P + h * HEAD_W:P + (h + 1) * HEAD_W] = dv_ref[:, sl].astype(dkvp_ref.dtype)
        dkpe_ref[...] = jnp.where(rot, _unrope(dkpe, cosf, sin_a, sin_b), 0.0).astype(dkpe_ref.dtype)
        dcq, dqn = _rms_bwd_tile(cq_ref[...], qn_ref[...], _dot_nt(dqp_ref[...], wq_ref[...]))
        dckv, dkn = _rms_bwd_tile(ckv_ref[...], kn_ref[...], _dot_nt(dkvp_ref[...], wkv_ref[...]))
        dcq_ref[...] = dcq.astype(dcq_ref.dtype)
        dckv_ref[...] = dckv.astype(dckv_ref.dtype)

        @pl.when(pl.program_id(0) == 0)
        def _():
            dqn_ref[...] = dqn
            dkn_ref[...] = dkn

        @pl.when(pl.program_id(0) > 0)
        def _():
            dqn_ref[...] += dqn
            dkn_ref[...] += dkn

    row = lambda w: pl.BlockSpec((tm, w), lambda i: (i, 0))
    full = lambda a, b: pl.BlockSpec((a, b), lambda i: (0, 0))
    return pl.pallas_call(
        body, name=name, grid=(T // tm,),
        in_specs=[pl.BlockSpec((tm, QR), lambda i: (i, PCQ0 // QR)), pl.BlockSpec((tm, KR), lambda i: (i, PCKV0 // KR)),
                  row(1), full(1, LANE), full(1, QR), full(QR, P), full(1, KR), full(KR, 2 * P), row(P), row(P), row(P)],
        out_specs=[row(P), row(2 * P), row(QR), row(KR), row(LANE), full(1, QR), full(1, KR)],
        out_shape=[_sds((T, P), MXU_DTYPE), _sds((T, 2 * P), MXU_DTYPE), _sds((T, QR), MXU_DTYPE),
                   _sds((T, KR), MXU_DTYPE), _sds((T, LANE), MXU_DTYPE), _sds((1, QR), F32), _sds((1, KR), F32)],
        compiler_params=_cparams(("arbitrary",)),
    )(proj, proj, pos, invf, q_norm.reshape(1, QR), w_uq_p, kv_norm.reshape(1, KR), w_ukv_p, dq, dk, dv)


ATTN_SCALE = 1.0 / math.sqrt(MLA_QK)


def _causal_mask(i, j, blk):
    row = lax.broadcasted_iota(jnp.int32, (blk, blk), 0)
    col = lax.broadcasted_iota(jnp.int32, (blk, blk), 1)
    return col <= row + (i - j) * blk


def _hosting(hosted, grid, n_in, n_out, n_scratch):
    if hosted is None:
        return (lambda body: body), (), [], [], []
    hi, ho = len(hosted.inputs), len(hosted.out_shapes)

    def wrap(body):
        def full(*refs):
            ins, rest = refs[:n_in + hi], refs[n_in + hi:]
            outs, scr = rest[:n_out + ho], rest[n_out + ho:]
            parts = ins[n_in:], outs[n_out:], scr[n_scratch:]
            ids = [pl.program_id(d) for d in range(len(grid))]
            step = ids[0]
            for d in range(1, len(grid)):
                step = step * grid[d] + ids[d]
            total = math.prod(grid)

            @pl.when(step == 0)
            def _():
                hosted.start(*parts)

            body(*ins[:n_in], *outs[:n_out], *scr[:n_scratch])

            @pl.when(step == total // 2)
            def _():
                hosted.relay(*parts)

            @pl.when(step == total - 1)
            def _():
                hosted.finish(*parts)

        return full

    hbm = pl.BlockSpec(memory_space=pl.ANY)
    return wrap, tuple(hosted.inputs), [hbm] * ho, list(hosted.out_shapes), list(hosted.sems)


def flash_fwd(q, k, v, S, *, name, blk=512, hosted=None):
    T, P = q.shape
    blk = _pick(S, (blk, 256, 128))
    B, nq, H, W = T // S, S // blk, MLA_HEADS, HEAD_W
    grid = (B, H, nq)
    wrap, h_in, h_ospecs, h_oshapes, h_scratch = _hosting(hosted, grid, 3, 2, 0)

    def body(q_ref, k_ref, v_ref, o_ref, lse_ref):
        i = pl.program_id(2)
        qv = q_ref[...]

        def online(j, carry, masked):
            m_prev, l_prev, acc = carry
            rows = pl.ds(pl.multiple_of(j * blk, blk), blk)
            s = _dot_nt(qv, k_ref[rows, :])
            if masked:
                s = jnp.where(_causal_mask(0, 0, blk), s, -jnp.inf)
            m_new = jnp.maximum(m_prev, jnp.max(s, axis=1, keepdims=True))
            p = jnp.exp(s - m_new)
            alpha = jnp.exp(m_prev - m_new)
            return (m_new, alpha * l_prev + jnp.sum(p, axis=1, keepdims=True), alpha * acc + _dot(p, v_ref[rows, :]))

        init = (jnp.full((blk, 1), -jnp.inf, F32), jnp.zeros((blk, 1), F32), jnp.zeros((blk, W), F32))
        carry = lax.fori_loop(0, i, lambda j, c: online(j, c, False), init)
        m, l, acc = online(i, carry, True)
        o_ref[...] = acc / l
        lse_ref[...] = jnp.broadcast_to(m + jnp.log(l), (blk, W))

    qmap = lambda b, h, i: (b * nq + i, h)
    kmap = lambda b, h, i: (b, h)
    hbm = pl.BlockSpec(memory_space=pl.ANY)
    return pl.pallas_call(
        wrap(body), name=name, grid=grid,
        in_specs=[pl.BlockSpec((blk, W), qmap), pl.BlockSpec((S, W), kmap), pl.BlockSpec((S, W), kmap)] + [hbm] * len(h_in),
        out_specs=[pl.BlockSpec((blk, W), qmap), pl.BlockSpec((blk, W), qmap)] + h_ospecs,
        out_shape=[_sds((T, P), F32), _sds((T, P), F32)] + h_oshapes,
        scratch_shapes=h_scratch,
        compiler_params=_cparams(("arbitrary",) * 3 if hosted else ("parallel", "parallel", "arbitrary")),
    )(q, k, v, *h_in)


def flash_bwd(q, k, v, o, lse, dmix, S, *, name, blk=512, hosted=None):
    T, P = q.shape
    blk = _pick(S, (blk, 256, 128))
    B, nq, H, W = T // S, S // blk, MLA_HEADS, HEAD_W
    off = (SSD_WIDTH + POOL_WIDTH) // W
    grid = (B, H, nq)
    wrap, h_in, h_ospecs, h_oshapes, h_scratch = _hosting(hosted, grid, 6, 3, 1)

    def body(q_ref, k_ref, v_ref, o_ref, lse_ref, do_ref, dq_ref, dk_ref, dv_ref, delta_s):
        j = pl.program_id(2)

        @pl.when(j == 0)
        def _():
            for i in range(nq):
                rows = slice(i * blk, (i + 1) * blk)
                delta_s[rows, :] = jnp.sum(do_ref[rows, :] * o_ref[rows, :], axis=1, keepdims=True)
                dq_ref[rows, :] = jnp.zeros((blk, W), F32)

        kv, vv = k_ref[...], v_ref[...]

        def step(i, carry, masked):
            dk, dv = carry
            rows = pl.ds(pl.multiple_of(i * blk, blk), blk)
            qv, do = q_ref[rows, :], do_ref[rows, :]
            p = jnp.exp(_dot_nt(qv, kv) - lse_ref[rows, 0:1])
            if masked:
                p = jnp.where(_causal_mask(0, 0, blk), p, 0.0)
            ds = p * (_dot_nt(do, vv) - delta_s[rows, :])
            dq_ref[rows, :] += _dot(ds, kv)
            return dk + _dot_tn(ds, qv), dv + _dot_tn(p, do)

        zero = jnp.zeros((blk, W), F32)
        carry = step(j, (zero, zero), True)
        dk, dv = lax.fori_loop(j + 1, nq, lambda i, c: step(i, c, False), carry)
        dk_ref[...] = dk
        dv_ref[...] = dv

    full = lambda b, h, j: (b, h)
    kmap = lambda b, h, j: (b * nq + j, h)
    hbm = pl.BlockSpec(memory_space=pl.ANY)
    return pl.pallas_call(
        wrap(body), name=name, grid=grid,
        in_specs=[pl.BlockSpec((S, W), full), pl.BlockSpec((blk, W), kmap), pl.BlockSpec((blk, W), kmap),
                  pl.BlockSpec((S, W), full), pl.BlockSpec((S, W), full),
                  pl.BlockSpec((S, W), lambda b, h, j: (b, off + h))] + [hbm] * len(h_in),
        out_specs=[pl.BlockSpec((S, W), full), pl.BlockSpec((blk, W), kmap), pl.BlockSpec((blk, W), kmap)] + h_ospecs,
        out_shape=[_sds((T, P), F32)] * 3 + h_oshapes,
        scratch_shapes=[pltpu.VMEM((S, 1), F32)] + h_scratch,
        compiler_params=_cparams(("arbitrary",) * 3 if hosted else ("parallel", "parallel", "arbitrary")),
    )(q, k, v, o, lse, dmix, *h_in)


SSD_PAIRS = SSD_HEADS // 2
PAIRS_PER_GROUP = SSD_PAIRS // SSD_GROUPS
GN = SSD_GROUPS * SSD_STATE


def _log1p_small(e):
    return jnp.where(e < 1e-3, e * (1.0 - e * (0.5 - e / 3.0)), jnp.log(1.0 + e))


def _softplus(v):
    return jnp.maximum(v, 0.0) + _log1p_small(jnp.exp(-jnp.abs(v)))


def _ssd_decay(dt_raw, dtb, alog):
    L = dt_raw.shape[0]
    pre = dt_raw + dtb
    dt = _softplus(pre)
    a = -jnp.exp(alog)
    row = lax.broadcasted_iota(jnp.int32, (L, L), 0)
    col = lax.broadcasted_iota(jnp.int32, (L, L), 1)
    tri = row >= col
    cum = _dot_hi(tri.astype(F32), dt * a)
    return pre, dt, a, tri, cum, cum.T


def _col(m, h):
    return m[:, h:h + 1]


def _pair_sel(m, k, lo):
    return jnp.where(lo, _col(m, 2 * k), _col(m, 2 * k + 1))


def _ssd_specs(S):
    L = SSD_CHUNK
    nc = S // L
    return L, nc


def ssd_fwd(proj, xc, dtb, alog, dchan, normw, S, *, name):
    T = proj.shape[0]
    L, nc = _ssd_specs(S)
    B, W, N = T // S, SSD_WIDTH, SSD_STATE

    def body(xs_ref, bs_ref, cs_ref, dt_ref, z_ref, dtb_ref, alog_ref, dch_ref, nw_ref, y_ref, ys_ref, hin_ref, st):
        @pl.when(pl.program_id(1) == 0)
        def _():
            st[...] = jnp.zeros(st.shape, F32)

        hin_ref[...] = st[...]
        _, dt, a, tri, cum, cum_t = _ssd_decay(dt_ref[...], dtb_ref[...], alog_ref[...])
        e_cum = jnp.exp(cum)
        last = cum[L - 1:L, :]
        w_end = jnp.exp(last - cum)
        e_last = jnp.exp(last)
        lo = lax.broadcasted_iota(jnp.int32, (1, LANE), 1) < SSD_HEAD_DIM
        for g in range(SSD_GROUPS):
            bm = bs_ref[:, g * N:(g + 1) * N]
            cm = cs_ref[:, g * N:(g + 1) * N]
            bm_t = bm.T
            gmat = _dot_nt(cm, bm)
            for kk in range(PAIRS_PER_GROUP):
                k = g * PAIRS_PER_GROUP + kk
                sl = slice(k * LANE, (k + 1) * LANE)
                xv = xs_ref[:, sl]
                xdt = xv * _pair_sel(dt, k, lo)
                yd = []
                for h in (2 * k, 2 * k + 1):
                    gam = jnp.exp(jnp.where(tri, _col(cum, h) - cum_t[h:h + 1, :], -jnp.inf))
                    yd.append(_dot(gmat * gam, xdt))
                hp = st[:, sl]
                y_off = _dot(cm, hp) * _pair_sel(e_cum, k, lo)
                y_ref[:, sl] = jnp.where(lo, yd[0], yd[1]) + y_off + xv * dch_ref[:, sl]
                zmat = xdt * _pair_sel(w_end, k, lo)
                st[:, sl] = hp * _pair_sel(e_last, k, lo) + _dot(bm_t, zmat)
        y = y_ref[...]
        z = z_ref[...]
        yz = y * (z * _sigmoid(z))
        ys_ref[...] = _rms_tile(yz, nw_ref[...]).astype(ys_ref.dtype)

    r = lambda b, c: b * nc + c
    vec = lambda w: pl.BlockSpec((1, w), lambda b, c: (0, 0))
    return pl.pallas_call(
        body, name=name, grid=(B, nc),
        in_specs=[pl.BlockSpec((L, W), lambda b, c: (r(b, c), 0)),
                  pl.BlockSpec((L, GN), lambda b, c: (r(b, c), W // GN)),
                  pl.BlockSpec((L, GN), lambda b, c: (r(b, c), W // GN + 1)),
                  pl.BlockSpec((L, LANE), lambda b, c: (r(b, c), PDT0 // LANE)),
                  pl.BlockSpec((L, W), lambda b, c: (r(b, c), PZ0 // W)),
                  vec(LANE), vec(LANE), vec(W), vec(W)],
        out_specs=[pl.BlockSpec((L, W), lambda b, c: (r(b, c), 0)), pl.BlockSpec((L, W), lambda b, c: (r(b, c), 0)),
                   pl.BlockSpec((N, W), lambda b, c: (r(b, c), 0))],
        out_shape=[_sds((T, W), F32), _sds((T, W), MXU_DTYPE), _sds((T // L * N, W), F32)],
        scratch_shapes=[pltpu.VMEM((N, W), F32)],
        compiler_params=_cparams(("parallel", "arbitrary")),
    )(xc, xc, xc, proj, proj, dtb, alog, dchan, normw)


def ssd_bwd(proj, xc, ypre, hin, dmix, dtb, alog, dchan, normw, S, *, name):
    T = proj.shape[0]
    L, nc = _ssd_specs(S)
    B, W, N = T // S, SSD_WIDTH, SSD_STATE

    def body(xs_ref, bs_ref, cs_ref, dt_ref, z_ref, y_ref, hin_ref, dys_ref, dtb_ref, alog_ref, dch_ref, nw_ref,
             dxc_ref, ddt_ref, dz_ref, sm_ref, dnw_ref, dst):
        step = pl.program_id(0) * nc + pl.program_id(1)

        @pl.when(pl.program_id(1) == 0)
        def _():
            dst[...] = jnp.zeros(dst.shape, F32)

        pre, dt, a, tri, cum, cum_t = _ssd_decay(dt_ref[...], dtb_ref[...], alog_ref[...])
        e_cum = jnp.exp(cum)
        last = cum[L - 1:L, :]
        w_end = jnp.exp(last - cum)
        e_last = jnp.exp(last)
        lane = lax.broadcasted_iota(jnp.int32, (1, LANE), 1)
        sub = lax.broadcasted_iota(jnp.int32, (LANE, 1), 0)
        lo = lane < SSD_HEAD_DIM
        is_last_row = sub == L - 1
        tri_t = (lax.broadcasted_iota(jnp.int32, (L, L), 0) <= lax.broadcasted_iota(jnp.int32, (L, L), 1))

        y, z, nw = y_ref[...], z_ref[...], nw_ref[...]
        sg = _sigmoid(z)
        gate = z * sg
        dyz, dnw = _rms_bwd_tile(y * gate, nw, dys_ref[...])
        dy_all = dyz * gate
        dz_ref[...] = (dyz * y * (sg * (1.0 + z * (1.0 - sg)))).astype(dz_ref.dtype)

        d_cum = jnp.zeros((L, LANE), F32)
        d_cum_t = jnp.zeros((LANE, L), F32)
        d_dt = jnp.zeros((L, LANE), F32)
        d_dskip = jnp.zeros((1, LANE), F32)
        for g in range(SSD_GROUPS):
            bm = bs_ref[:, g * N:(g + 1) * N]
            cm = cs_ref[:, g * N:(g + 1) * N]
            cm_t = cm.T
            gmat = _dot_nt(cm, bm)
            gmat_t = _dot_nt(bm, cm)
            d_g = jnp.zeros((L, L), F32)
            d_bm = jnp.zeros((L, N), F32)
            d_cm = jnp.zeros((L, N), F32)
            for kk in range(PAIRS_PER_GROUP):
                k = g * PAIRS_PER_GROUP + kk
                sl = slice(k * LANE, (k + 1) * LANE)
                xv = xs_ref[:, sl]
                dyv = dy_all[:, sl]
                dt_sel = _pair_sel(dt, k, lo)
                xdt = xv * dt_sel
                hp = hin_ref[:, sl]
                dh_out = dst[:, sl]
                e_sel = _pair_sel(e_cum, k, lo)
                w_sel = _pair_sel(w_end, k, lo)
                e_lane = _pair_sel(e_last, k, lo)
                y_off = _dot(cm, hp) * e_sel
                zmat = xdt * w_sel
                d_z = _dot(bm, dh_out)
                d_bm = d_bm + _dot_nt(zmat, dh_out)
                d_xdt = d_z * w_sel
                dw_full = d_z * zmat
                hh = dh_out * hp
                d_r = dyv * e_sel
                d_cm = d_cm + _dot_nt(d_r, hp)
                dst[:, sl] = dh_out * e_lane + _dot(cm_t, d_r)
                dyoff_full = dyv * y_off
                for j, h in enumerate((2 * k, 2 * k + 1)):
                    mine = lo if j == 0 else jnp.logical_not(lo)
                    hot = lane == h
                    dyh = jnp.where(mine, dyv, 0.0)
                    gam = jnp.exp(jnp.where(tri, _col(cum, h) - cum_t[h:h + 1, :], -jnp.inf))
                    gam_t = jnp.exp(jnp.where(tri_t, cum_t[h:h + 1, :] - _col(cum, h), -jnp.inf))
                    mx = gmat * gam
                    d_xdt = d_xdt + _dot(gmat_t * gam_t, dyh)
                    d_mx = jnp.where(tri, _dot_nt(dyh, xdt), 0.0)
                    d_g = d_g + d_mx * gam
                    d_seg = d_mx * mx
                    row_l = (jnp.sum(d_seg, axis=1, keepdims=True)
                             + jnp.sum(jnp.where(mine, dyoff_full - dw_full, 0.0), axis=1, keepdims=True))
                    at_end = (jnp.sum(jnp.where(mine, dw_full, 0.0), keepdims=True)
                              + jnp.sum(jnp.where(mine, hh, 0.0), keepdims=True) * _col(e_last, h))
                    d_cum = d_cum + jnp.where(hot, row_l + jnp.where(is_last_row, at_end, 0.0), 0.0)
                    d_cum_t = d_cum_t - jnp.where(sub == h, jnp.sum(d_seg, axis=0, keepdims=True), 0.0)
                    d_dskip = d_dskip + jnp.where(hot, jnp.sum(jnp.where(mine, dyv * xv, 0.0), keepdims=True), 0.0)
                for j, h in enumerate((2 * k, 2 * k + 1)):
                    mine = lo if j == 0 else jnp.logical_not(lo)
                    d_dt = d_dt + jnp.where(lane == h, jnp.sum(jnp.where(mine, d_xdt * xv, 0.0), axis=1, keepdims=True), 0.0)
                dxc_ref[:, sl] = d_xdt * dt_sel + dyv * dch_ref[:, sl]
            dxc_ref[:, W + g * N:W + (g + 1) * N] = d_bm + _dot_tn(d_g, cm)
            dxc_ref[:, W + GN + g * N:W + GN + (g + 1) * N] = d_cm + _dot(d_g, bm)

        d_cum = d_cum + d_cum_t.T
        d_da = _dot_hi(jnp.logical_not(tri).astype(F32) + (lax.broadcasted_iota(jnp.int32, (L, L), 0)
                                                              == lax.broadcasted_iota(jnp.int32, (L, L), 1)).astype(F32), d_cum)
        d_dt = d_dt + d_da * a
        heads = lane < SSD_HEADS
        d_pre = jnp.where(heads, d_dt * _sigmoid(pre), 0.0)
        ddt_ref[...] = d_pre.astype(ddt_ref.dtype)
        d_alog = jnp.sum(d_da * dt, axis=0, keepdims=True) * a
        part = jnp.concatenate([jnp.where(heads, d_alog, 0.0), jnp.sum(d_pre, axis=0, keepdims=True), d_dskip,
                                jnp.zeros((5, LANE), F32)], axis=0)

        @pl.when(step == 0)
        def _():
            sm_ref[...] = part
            dnw_ref[...] = dnw

        @pl.when(step > 0)
        def _():
            sm_ref[...] += part
            dnw_ref[...] += dnw

    r = lambda b, c: b * nc + (nc - 1 - c)
    vec = lambda w: pl.BlockSpec((1, w), lambda b, c: (0, 0))
    blk = lambda w, j: pl.BlockSpec((L, w), lambda b, c: (r(b, c), j))
    return pl.pallas_call(
        body, name=name, grid=(B, nc),
        in_specs=[blk(W, 0), blk(GN, W // GN), blk(GN, W // GN + 1), blk(LANE, PDT0 // LANE), blk(W, PZ0 // W),
                  blk(W, 0), pl.BlockSpec((N, W), lambda b, c: (r(b, c), 0)), blk(W, 0),
                  vec(LANE), vec(LANE), vec(W), vec(W)],
        out_specs=[blk(SSD_CONV_CH, 0), blk(LANE, 0), blk(W, 0), pl.BlockSpec((8, LANE), lambda b, c: (0, 0)), vec(W)],
        out_shape=[_sds((T, SSD_CONV_CH), F32), _sds((T, LANE), MXU_DTYPE), _sds((T, W), MXU_DTYPE),
                   _sds((8, LANE), F32), _sds((1, W), F32)],
        scratch_shapes=[pltpu.VMEM((N, W), F32)],
        compiler_params=_cparams(("arbitrary", "arbitrary")),
    )(xc, xc, xc, proj, proj, ypre, hin, dmix, dtb, alog, dchan, normw)


def _adamw_math(w, g, m, v):
    m = ADAM_B1 * m + (1.0 - ADAM_B1) * g
    v = ADAM_B2 * v + (1.0 - ADAM_B2) * (g * g)
    m_hat = m / (1.0 - ADAM_B1 ** ADAM_STEP)
    v_hat = v / (1.0 - ADAM_B2 ** ADAM_STEP)
    delta = -ADAM_LR * (m_hat / (jnp.sqrt(v_hat) + ADAM_EPS) + ADAM_WD * w)
    return delta, m, v


def adamw_layers(w, g_layers, m, v, *, name, tr=256):
    L, A, B = w.shape
    tr = _pick(A, (tr, 192, 176, 128, 64, 32, 16, 8))
    na = A // tr
    n = len(g_layers[0])

    def body(*refs):
        w_ref, m_ref, v_ref = refs[0], refs[1 + L * n], refs[2 + L * n]
        g_ref, d_ref, nm_ref, nv_ref = refs[3 + L * n:]
        layer = pl.program_id(0)
        g = None
        for l in range(L):
            parts = refs[1 + l * n:1 + (l + 1) * n]
            gl = parts[0][...]
            for p in parts[1:]:
                gl = gl + p[...]
            g = gl if g is None else jnp.where(layer == l, gl, g)
        g_ref[...] = g
        d_ref[...], nm_ref[...], nv_ref[...] = _adamw_math(w_ref[...], g, m_ref[...], v_ref[...])

    def g_spec(l):
        return pl.BlockSpec((tr, B), lambda layer, i: (jnp.where(layer == l, i, jnp.where(layer < l, 0, na - 1)), 0))

    spec = pl.BlockSpec((None, tr, B), lambda layer, i: (layer, i, 0))
    return pl.pallas_call(
        body, name=name, grid=(L, na), in_specs=[spec] + [g_spec(l) for l in range(L) for _ in range(n)] + [spec] * 2,
        out_specs=[spec] * 4, out_shape=[_sds((L, A, B), F32)] * 4, compiler_params=_cparams(("arbitrary", "arbitrary")),
    )(w, *[p for parts in g_layers for p in parts], m, v)


def adamw_small(ws, gs, ms, vs, *, name):
    n = len(ws)

    def body(*refs):
        w_refs, g_refs, m_refs, v_refs = (refs[i * n:(i + 1) * n] for i in range(4))
        d_refs, nm_refs, nv_refs = (refs[(4 + i) * n:(5 + i) * n] for i in range(3))
        for a in range(n):
            d_refs[a][...], nm_refs[a][...], nv_refs[a][...] = _adamw_math(
                w_refs[a][...], g_refs[a][...], m_refs[a][...], v_refs[a][...])

    vm = pl.BlockSpec(memory_space=pltpu.VMEM)
    out = pl.pallas_call(
        body, name=name, in_specs=[vm] * (4 * n), out_specs=[vm] * (3 * n),
        out_shape=[_sds(w.shape, F32) for w in ws] * 3, compiler_params=pltpu.CompilerParams(vmem_limit_bytes=VMEM_LIMIT),
    )(*ws, *gs, *ms, *vs)
    return out[:n], out[n:2 * n], out[2 * n:]


def _my_place():
    return lax.axis_index("x"), lax.axis_index("y"), lax.axis_index("c")


def _other_chips(x, y):
    return [(1 - x, y), (x, 1 - y), (1 - x, 1 - y)]


def relation_of(chip, me):
    d = chip ^ me
    return jnp.where(d == 2, 0, jnp.where(d == 1, 1, jnp.where(d == 3, 2, -1)))


class Exchange(NamedTuple):
    inputs: tuple
    out_shapes: tuple
    sems: tuple
    start: Callable
    relay: Callable
    finish: Callable


def scatter_exchange(srcs):
    n = len(srcs)

    def copies(ins, outs, sems):
        x, y, c = _my_place()
        out = []
        for k, (px, py) in enumerate(_other_chips(x, y)):
            for a in range(n):
                out.append(pltpu.make_async_remote_copy(
                    src_ref=ins[a].at[2 * px + py], dst_ref=outs[a].at[k], send_sem=sems[0].at[k, a],
                    recv_sem=sems[1].at[k, a], device_id=(px, py, c), device_id_type=pl.DeviceIdType.MESH))
        return out

    def start(ins, outs, sems):
        for cp in copies(ins, outs, sems):
            cp.start()

    def finish(ins, outs, sems):
        cps = copies(ins, outs, sems)
        for cp in cps:
            cp.wait_recv()
        for cp in cps:
            cp.wait_send()

    return Exchange(tuple(srcs), tuple(_sds((3,) + s.shape[1:], s.dtype) for s in srcs),
                    (pltpu.SemaphoreType.DMA((3, n)),) * 2, start, lambda *a: None, finish)


def run_exchange(ex, *, name):
    n_in, n_out = len(ex.inputs), len(ex.out_shapes)

    def body(*refs):
        parts = refs[:n_in], refs[n_in:n_in + n_out], refs[n_in + n_out:]
        ex.start(*parts)
        ex.relay(*parts)
        ex.finish(*parts)

    hbm = pl.BlockSpec(memory_space=pl.ANY)
    return pl.pallas_call(
        body, name=name, in_specs=[hbm] * n_in, out_specs=[hbm] * n_out, out_shape=list(ex.out_shapes),
        scratch_shapes=list(ex.sems), compiler_params=pltpu.CompilerParams(has_side_effects=True),
    )(*ex.inputs)


def sibling_swap(srcs, *, name):
    n = len(srcs)

    def body(*refs):
        src_refs, out_refs, (send_sems, recv_sems) = refs[:n], refs[n:2 * n], refs[2 * n:]
        x, y, c = _my_place()
        copies = [pltpu.make_async_remote_copy(
            src_ref=src_refs[a], dst_ref=out_refs[a], send_sem=send_sems.at[a], recv_sem=recv_sems.at[a],
            device_id=(x, y, 1 - c), device_id_type=pl.DeviceIdType.MESH) for a in range(n)]
        for cp in copies:
            cp.start()
        for cp in copies:
            cp.wait_recv()
        for cp in copies:
            cp.wait_send()

    hbm = pl.BlockSpec(memory_space=pl.ANY)
    return pl.pallas_call(
        body, name=name, in_specs=[hbm] * n, out_specs=[hbm] * n, out_shape=[_sds(s.shape, s.dtype) for s in srcs],
        scratch_shapes=[pltpu.SemaphoreType.DMA((n,)), pltpu.SemaphoreType.DMA((n,))],
        compiler_params=pltpu.CompilerParams(has_side_effects=True),
    )(*srcs)


def gather_exchange(srcs):
    nch = len(srcs)
    halves = [s.shape[0] // 2 for s in srcs]
    assert all(2 * h == s.shape[0] and h % 16 == 0 for h, s in zip(halves, srcs))
    pieces = [(k, q) for k in range(3) for q in range(nch)]

    def makers(ins, outs, sems):
        ici_send, ici_recv, d2d_send, d2d_recv = sems
        x, y, c = _my_place()
        peers = _other_chips(x, y)

        def rows(core, q):
            return pl.ds(core * halves[q], halves[q])

        def ici(k, q):
            px, py = peers[k]
            return pltpu.make_async_remote_copy(
                src_ref=ins[q].at[rows(c, q)], dst_ref=outs[q].at[k, rows(c, q)], send_sem=ici_send.at[k, q],
                recv_sem=ici_recv.at[k, q], device_id=(px, py, c), device_id_type=pl.DeviceIdType.MESH)

        def d2d(k, q, core):
            return pltpu.make_async_remote_copy(
                src_ref=outs[q].at[k, rows(core, q)], dst_ref=outs[q].at[k, rows(core, q)],
                send_sem=d2d_send.at[k, q], recv_sem=d2d_recv.at[k, q], device_id=(x, y, 1 - c),
                device_id_type=pl.DeviceIdType.MESH)

        return ici, d2d, c

    def start(*refs):
        ici, _, _ = makers(*refs)
        for k, q in pieces:
            ici(k, q).start()

    def relay(*refs):
        ici, d2d, c = makers(*refs)
        for k, q in pieces:
            ici(k, q).wait_recv()
            d2d(k, q, c).start()

    def finish(*refs):
        ici, d2d, c = makers(*refs)
        for k, q in pieces:
            d2d(k, q, 1 - c).wait_recv()
        for k, q in pieces:
            ici(k, q).wait_send()
            d2d(k, q, c).wait_send()

    return Exchange(tuple(srcs), tuple(_sds((3,) + s.shape, s.dtype) for s in srcs),
                    (pltpu.SemaphoreType.DMA((3, nch)),) * 4, start, relay, finish)


def all_sum_small(vec, *, name):
    R, C = vec.shape

    def body(v_ref, out_ref, buf, send_sems, recv_sems):
        x, y, c = _my_place()
        me = 4 * x + 2 * y + c
        buf[me] = v_ref[...]
        copies = []
        for k in range(1, N_DEV):
            px, py, pc = x ^ (k >> 2), y ^ ((k >> 1) & 1), c ^ (k & 1)
            copies.append(pltpu.make_async_remote_copy(
                src_ref=v_ref, dst_ref=buf.at[me], send_sem=send_sems.at[k - 1], recv_sem=recv_sems.at[k - 1],
                device_id=(px, py, pc), device_id_type=pl.DeviceIdType.MESH))
        for cp in copies:
            cp.start()
        for k in range(1, N_DEV):
            px, py, pc = x ^ (k >> 2), y ^ ((k >> 1) & 1), c ^ (k & 1)
            pltpu.make_async_remote_copy(
                src_ref=v_ref, dst_ref=buf.at[4 * px + 2 * py + pc], send_sem=send_sems.at[k - 1],
                recv_sem=recv_sems.at[k - 1], device_id=(px, py, pc), device_id_type=pl.DeviceIdType.MESH).wait_recv()
        for cp in copies:
            cp.wait_send()
        acc = buf[0]
        for d in range(1, N_DEV):
            acc = acc + buf[d]
        out_ref[...] = acc

    return pl.pallas_call(
        body, name=name, in_specs=[pl.BlockSpec(memory_space=pltpu.VMEM)], out_specs=pl.BlockSpec(memory_space=pltpu.VMEM),
        out_shape=_sds((R, C), F32),
        scratch_shapes=[pltpu.VMEM((N_DEV, R, C), F32), pltpu.SemaphoreType.DMA((N_DEV - 1,)),
                        pltpu.SemaphoreType.DMA((N_DEV - 1,))],
        compiler_params=pltpu.CompilerParams(has_side_effects=True, vmem_limit_bytes=VMEM_LIMIT),
    )(vec)


def sum_chips(own, others, *, name, tr=512):
    R, C = own.shape
    tr = _pick(R, (tr, 384, 352, 256, 128, 64, 32, 16))

    def body(o_ref, p_ref, s_ref):
        acc = o_ref[...].astype(F32)
        for k in range(3):
            acc = acc + p_ref[k].astype(F32)
        s_ref[...] = acc

    return pl.pallas_call(
        body, name=name, grid=(R // tr,),
        in_specs=[pl.BlockSpec((tr, C), lambda i: (i, 0)), pl.BlockSpec((3, tr, C), lambda i: (0, i, 0))],
        out_specs=pl.BlockSpec((tr, C), lambda i: (i, 0)), out_shape=_sds((R, C), F32),
        compiler_params=_cparams(("parallel",)),
    )(own, others)


WEIGHTS = ['attn_norm', 'w_in', 'ssd_conv_w', 'ssd_conv_b', 'ssd_dt_bias', 'ssd_a_log', 'ssd_d', 'ssd_norm', 'pool_w',
           'pool_scale', 'mla_q_norm', 'mla_w_uq', 'mla_kv_norm', 'mla_w_ukv', 'w_out', 'ffn_norm', 'ffn_w_up',
           'ffn_conv_w', 'ffn_conv_b', 'ffn_w_down', 'final_norm']
BIG = {'w_in': 2, 'mla_w_uq': 2, 'mla_w_ukv': 2, 'w_out': 1, 'ffn_w_up': 2, 'ffn_w_down': 1}
CONV_SHARDED = ('ssd_conv_w', 'ffn_conv_w')


def _zeros_cols(w, n):
    return jnp.zeros((w.shape[0], n), w.dtype)


def _w_in_to_padded(w):
    return jnp.concatenate([w[:, 0:2560], w[:, 2576:3088], w[:, 3088:3472], w[:, 2560:2576], _zeros_cols(w, 112),
                            w[:, 3472:3728], _zeros_cols(w, 64), w[:, 3728:3760], _zeros_cols(w, 32 + 128)], axis=1)


def _w_in_from_padded(g):
    return jnp.concatenate([g[:, 0:2560], g[:, PDT0:PDT0 + SSD_HEADS], g[:, PU0:PU0 + POOL_WIDTH],
                            g[:, PCQ0:PCQ0 + MLA_Q_RANK], g[:, PCKV0:PCKV0 + MLA_KV_RANK],
                            g[:, PKPE0 + ROPE0:PKPE0 + ROPE0 + MLA_ROPE]], axis=1)


def _w_uq_to_padded(w):
    r = w.reshape(MLA_Q_RANK, MLA_HEADS, MLA_QK)
    return jnp.pad(r, ((0, 0), (0, 0), (0, HEAD_W - MLA_QK))).reshape(MLA_Q_RANK, MLA_PAD)


def _w_uq_from_padded(g):
    return g.reshape(MLA_Q_RANK, MLA_HEADS, HEAD_W)[:, :, :MLA_QK].reshape(MLA_Q_RANK, MLA_HEADS * MLA_QK)


def _w_ukv_to_padded(w):
    r = w.reshape(MLA_KV_RANK, MLA_HEADS, MLA_NOPE + MLA_V)
    pad = lambda t: jnp.pad(t, ((0, 0), (0, 0), (0, HEAD_W - t.shape[2]))).reshape(MLA_KV_RANK, MLA_PAD)
    return jnp.concatenate([pad(r[:, :, :MLA_NOPE]), pad(r[:, :, MLA_NOPE:])], axis=1)


def _w_ukv_from_padded(g):
    kk = g[:, :MLA_PAD].reshape(MLA_KV_RANK, MLA_HEADS, HEAD_W)[:, :, :MLA_NOPE]
    vv = g[:, MLA_PAD:].reshape(MLA_KV_RANK, MLA_HEADS, HEAD_W)[:, :, :MLA_V]
    return jnp.concatenate([kk, vv], axis=2).reshape(MLA_KV_RANK, MLA_HEADS * (MLA_NOPE + MLA_V))


def _w_out_to_padded(w):
    att = w[SSD_WIDTH + POOL_WIDTH:].reshape(MLA_HEADS, MLA_V, D_MODEL)
    att = jnp.pad(att, ((0, 0), (0, HEAD_W - MLA_V), (0, 0))).reshape(MLA_PAD, D_MODEL)
    return jnp.concatenate([w[:SSD_WIDTH + POOL_WIDTH], att], axis=0)


def _w_out_from_padded(g):
    att = g[SSD_WIDTH + POOL_WIDTH:].reshape(MLA_HEADS, HEAD_W, D_MODEL)[:, :MLA_V].reshape(MLA_WIDTH, D_MODEL)
    return jnp.concatenate([g[:SSD_WIDTH + POOL_WIDTH], att], axis=0)


def _pad_lanes(v, n=LANE):
    return jnp.pad(v.reshape(1, -1), ((0, 0), (0, n - v.size)))


def _pack_rows(parts, cols, dtype, row_multiple=16):
    flat = jnp.concatenate([p.astype(dtype).reshape(-1) for p in parts])
    rows = -(-flat.size // (cols * row_multiple)) * row_multiple
    return jnp.pad(flat, (0, rows * cols - flat.size)).reshape(rows, cols)


def _unpack_rows(packed, shapes):
    flat = packed.reshape(-1)
    out, at = [], 0
    for s in shapes:
        n = math.prod(s)
        out.append(flat[at:at + n].reshape(s))
        at += n
    return out


def _split_for_chips(g, axis):
    a, b = g.shape
    if axis == 0:
        return g.reshape(N_CHIPS, a // N_CHIPS, b)
    return g.reshape(a, N_CHIPS, b // N_CHIPS).transpose(1, 0, 2)


def _layer_weights(full, small, l):
    w = {}
    w['w_in_p'] = _w_in_to_padded(full['w_in'])
    w['w_uq_p'] = _w_uq_to_padded(full['mla_w_uq'])
    w['w_ukv_p'] = _w_ukv_to_padded(full['mla_w_ukv'])
    w['w_out_p'] = _w_out_to_padded(full['w_out'])
    w['w_up'] = full['ffn_w_up']
    w['w_down'] = full['ffn_w_down']
    for k in ('attn_norm', 'ssd_conv_w', 'ssd_conv_b', 'ssd_norm', 'pool_w', 'pool_scale', 'mla_q_norm', 'mla_kv_norm',
              'ffn_norm', 'ffn_conv_w', 'ffn_conv_b'):
        w[k] = small[k][l]
    w['dtb'] = _pad_lanes(small['ssd_dt_bias'][l])
    w['alog'] = _pad_lanes(small['ssd_a_log'][l])
    w['dchan'] = jnp.repeat(small['ssd_d'][l], SSD_HEAD_DIM).reshape(1, SSD_WIDTH)
    w['ssd_norm'] = w['ssd_norm'].reshape(1, SSD_WIDTH)
    return w


def _layer_fwd(x, pos, invf, w, S, l, hosted=None):
    n = lambda s: f"{s}_l{l}"
    h1 = rmsnorm_fwd(x, w['attn_norm'], name=n("attn_norm"))
    proj = matmul(h1, w['w_in_p'], name=n("w_in"))
    xc = ssd_conv_fwd(proj, w['ssd_conv_w'], w['ssd_conv_b'], S, name=n("ssd_conv"))
    ypre, yssd, hin = ssd_fwd(proj, xc, w['dtb'], w['alog'], w['dchan'], w['ssd_norm'], S, name=n("ssd_scan"))
    ypool, pooled = pool_fwd(proj, w['pool_w'], w['pool_scale'], S, name=n("pool"))
    q, k, v, cqn, ckvn = mla_prep_fwd(proj, pos, invf, w['mla_q_norm'], w['w_uq_p'], w['mla_kv_norm'], w['w_ukv_p'],
                                      name=n("mla_prep"))
    o, lse, *exchanged = flash_fwd(q, k, v, S, name=n("attention"), hosted=hosted)
    mix = jnp.concatenate([yssd, ypool, o.astype(MXU_DTYPE)], axis=1)
    x2 = matmul(mix, w['w_out_p'], res=x, name=n("w_out"))
    h2 = rmsnorm_fwd(x2, w['ffn_norm'], name=n("ffn_norm"))
    up = matmul(h2, w['w_up'], name=n("ffn_up"))
    act = ffn_conv_gate_fwd(up, w['ffn_conv_w'], w['ffn_conv_b'], S, name=n("ffn_conv_gate"))
    x3 = matmul(act, w['w_down'], res=x2, name=n("ffn_down"))
    saved = dict(x=x, h1=h1, proj=proj, xc=xc, ypre=ypre, hin=hin, pooled=pooled, q=q, k=k, v=v, cqn=cqn, ckvn=ckvn,
                 o=o, lse=lse, mix=mix, x2=x2, h2=h2, up=up, act=act)
    return x3, saved, exchanged


def _layer_bwd(dx3, pos, invf, w, s, S, l, host=None):
    n = lambda t: f"{t}_l{l}"
    g = {}
    dact = matmul(dx3, w['w_down'], nt=True, name=n("d_ffn_down"))
    g['ffn_w_down'] = matmul_tn(s['act'], dx3, name=n("g_ffn_down"))
    dup_g, dup_v, st = ffn_conv_gate_bwd(s['up'], w['ffn_conv_w'], w['ffn_conv_b'], dact, S, name=n("d_ffn_conv_gate"))
    g['ffn_conv_w'], g['ffn_conv_b'] = st[:FFN_CONV], st[FFN_CONV]
    dh2 = matmul(dup_g, w['w_up'], nt=True, kblock=0, name=n("d_ffn_up_g"))
    dh2 = matmul(dup_v, w['w_up'], nt=True, kblock=1, res=dh2, name=n("d_ffn_up_v"))
    g['ffn_w_up'] = jnp.concatenate([matmul_tn(s['h2'], dup_g, name=n("g_ffn_up_g")),
                                     matmul_tn(s['h2'], dup_v, name=n("g_ffn_up_v"))], axis=1)
    dx2, gn = rmsnorm_bwd(s['x2'], w['ffn_norm'], dh2, dx3, name=n("d_ffn_norm"))
    g['ffn_norm'] = gn[0]
    dmix = matmul(dx2, w['w_out_p'], nt=True, name=n("d_w_out"))
    g['w_out'] = _w_out_from_padded(matmul_tn(s['mix'], dx2, name=n("g_w_out")))
    dxc, ddt, dz, sm, gsn = ssd_bwd(s['proj'], s['xc'], s['ypre'], s['hin'], dmix, w['dtb'], w['alog'], w['dchan'],
                                    w['ssd_norm'], S, name=n("d_ssd_scan"))
    g['ssd_a_log'], g['ssd_dt_bias'], g['ssd_d'] = sm[0, :SSD_HEADS], sm[1, :SSD_HEADS], sm[2, :SSD_HEADS]
    g['ssd_norm'] = gsn[0]
    dpre, st = ssd_conv_bwd_pre(s['proj'], w['ssd_conv_w'], w['ssd_conv_b'], dxc, S, name=n("d_ssd_conv_act"))
    g['ssd_conv_w'], g['ssd_conv_b'] = st[:SSD_CONV], st[SSD_CONV]
    dxbc = conv_bwd_x(dpre, w['ssd_conv_w'], S, SSD_CONV, name=n("d_ssd_conv"))
    du, g['pool_w'], gps = pool_bwd(dmix, s['pooled'], w['pool_w'], w['pool_scale'], S, name=n("d_pool"))
    g['pool_scale'] = gps[0]
    dq, dk, dv, *exchanged = flash_bwd(s['q'], s['k'], s['v'], s['o'], s['lse'], dmix, S, name=n("d_attention"),
                                       hosted=host(g) if host else None)
    dqp, dkvp, dcq, dckv, dkpe, gqn, gkn = mla_prep_bwd(s['proj'], pos, invf, w['mla_q_norm'], w['w_uq_p'],
                                                        w['mla_kv_norm'], w['w_ukv_p'], dq, dk, dv, name=n("d_mla_prep"))
    g['mla_q_norm'], g['mla_kv_norm'] = gqn[0], gkn[0]
    g['mla_w_uq'] = _w_uq_from_padded(matmul_tn(s['cqn'], dqp, name=n("g_w_uq")))
    g['mla_w_ukv'] = _w_ukv_from_padded(matmul_tn(s['ckvn'], dkvp, name=n("g_w_ukv")))
    dproj = jnp.concatenate([dz, dxbc, du, dcq, ddt, dckv, dkpe, jnp.zeros_like(dkpe)], axis=1)
    dh1 = matmul(dproj, w['w_in_p'], nt=True, name=n("d_w_in"))
    g['w_in'] = _w_in_from_padded(matmul_tn(s['h1'], dproj, name=n("g_w_in")))
    dx, gn = rmsnorm_bwd(s['x'], w['attn_norm'], dh1, dx2, name=n("d_attn_norm"))
    g['attn_norm'] = gn[0]
    return dx, g, exchanged


def _rope_inputs(positions):
    pos = positions.reshape(-1, 1).astype(F32)
    inv_freq = ROPE_THETA ** (-jnp.arange(0, MLA_ROPE, 2, dtype=F32) / MLA_ROPE)
    invf = jnp.concatenate([jnp.zeros((ROPE0,), F32), inv_freq, inv_freq,
                            jnp.zeros((HEAD_W - ROPE0 - MLA_ROPE,), F32)]).reshape(1, HEAD_W)
    return pos, invf


EARLY_GRADS = ('w_out', 'ffn_w_up', 'ffn_w_down')


def kernel(x, positions, attn_norm, w_in, ssd_conv_w, ssd_conv_b, ssd_dt_bias, ssd_a_log, ssd_d, ssd_norm, pool_w, pool_scale, mla_q_norm, mla_w_uq, mla_kv_norm, mla_w_ukv, w_out, ffn_norm, ffn_w_up, ffn_conv_w, ffn_conv_b, ffn_w_down, final_norm, loss_target, m_attn_norm, m_w_in, m_ssd_conv_w, m_ssd_conv_b, m_ssd_dt_bias, m_ssd_a_log, m_ssd_d, m_ssd_norm, m_pool_w, m_pool_scale, m_mla_q_norm, m_mla_w_uq, m_mla_kv_norm, m_mla_w_ukv, m_w_out, m_ffn_norm, m_ffn_w_up, m_ffn_conv_w, m_ffn_conv_b, m_ffn_w_down, m_final_norm, v_attn_norm, v_w_in, v_ssd_conv_w, v_ssd_conv_b, v_ssd_dt_bias, v_ssd_a_log, v_ssd_d, v_ssd_norm, v_pool_w, v_pool_scale, v_mla_q_norm, v_mla_w_uq, v_mla_kv_norm, v_mla_w_ukv, v_w_out, v_ffn_norm, v_ffn_w_up, v_ffn_conv_w, v_ffn_conv_b, v_ffn_w_down, v_final_norm):
    wv = dict(zip(WEIGHTS, (attn_norm, w_in, ssd_conv_w, ssd_conv_b, ssd_dt_bias, ssd_a_log, ssd_d, ssd_norm, pool_w,
                            pool_scale, mla_q_norm, mla_w_uq, mla_kv_norm, mla_w_ukv, w_out, ffn_norm, ffn_w_up,
                            ffn_conv_w, ffn_conv_b, ffn_w_down, final_norm)))
    mv = dict(zip(WEIGHTS, (m_attn_norm, m_w_in, m_ssd_conv_w, m_ssd_conv_b, m_ssd_dt_bias, m_ssd_a_log, m_ssd_d,
                            m_ssd_norm, m_pool_w, m_pool_scale, m_mla_q_norm, m_mla_w_uq, m_mla_kv_norm, m_mla_w_ukv,
                            m_w_out, m_ffn_norm, m_ffn_w_up, m_ffn_conv_w, m_ffn_conv_b, m_ffn_w_down, m_final_norm)))
    vv = dict(zip(WEIGHTS, (v_attn_norm, v_w_in, v_ssd_conv_w, v_ssd_conv_b, v_ssd_dt_bias, v_ssd_a_log, v_ssd_d,
                            v_ssd_norm, v_pool_w, v_pool_scale, v_mla_q_norm, v_mla_w_uq, v_mla_kv_norm, v_mla_w_ukv,
                            v_w_out, v_ffn_norm, v_ffn_w_up, v_ffn_conv_w, v_ffn_conv_b, v_ffn_w_down, v_final_norm)))
    Bl, S, D = x.shape
    chip = 2 * lax.axis_index("x") + lax.axis_index("y")
    core = lax.axis_index("c")

    big_names = list(BIG)
    shards = [[wv[k][l].astype(MXU_DTYPE) for k in big_names] for l in range(DEPTH)]

    def whole_weights(own, others):
        rel = [relation_of(j, chip) for j in range(N_CHIPS)]
        return {k: jnp.concatenate(
            [jnp.where(r < 0, mine, jnp.where(r == 0, theirs[0], jnp.where(r == 1, theirs[1], theirs[2]))) for r in rel],
            axis=BIG[k] - 1) for k, mine, theirs in zip(big_names, own, others)}

    others0 = run_exchange(gather_exchange(shards[0]), name="gather_weights_l0")
    placed = []
    for k in CONV_SHARDED:
        sh = wv[k]
        whole = jnp.zeros(sh.shape[:-1] + (sh.shape[-1] * N_CHIPS,), F32)
        whole = lax.dynamic_update_slice_in_dim(whole, sh, chip * sh.shape[-1], axis=sh.ndim - 1)
        placed.append(jnp.where(core == 1, whole, 0.0))
    conv_full = _unpack_rows(all_sum_small(_pack_rows(placed, LANE, F32), name="gather_conv_weights"),
                             [p.shape for p in placed])
    small = {k: wv[k] for k in WEIGHTS if k not in BIG}
    small.update(dict(zip(CONV_SHARDED, conv_full)))

    T = Bl * S
    pos, invf = _rope_inputs(positions)
    group_a = [(k, 1) for k in big_names] + [(k, 0) for k in EARLY_GRADS]
    group_b = [(k, 0) for k in big_names if k not in EARLY_GRADS]

    def scatter_of(group, layer_grads):
        send = [_split_for_chips(layer_grads[l][k], BIG[k] - 1) for k, l in group]
        return send, scatter_exchange(send)

    layer_grads = [None] * DEPTH
    sent = {}
    w0 = _layer_weights(whole_weights(shards[0], others0), small, 0)
    h, saved0, others1 = _layer_fwd(x.reshape(T, D), pos, invf, w0, S, 0, hosted=gather_exchange(shards[1]))
    w1 = _layer_weights(whole_weights(shards[1], others1), small, 1)
    h, saved1, _ = _layer_fwd(h, pos, invf, w1, S, 1)
    loss, dh, g_final_norm = final_loss(h, small['final_norm'], loss_target.reshape(T, D))
    dh, layer_grads[1], _ = _layer_bwd(dh, pos, invf, w1, saved1, S, 1)

    def host_a(early):
        layer_grads[0] = early
        sent['a'], ex = scatter_of(group_a, layer_grads)
        return ex

    dx, layer_grads[0], others_a = _layer_bwd(dh, pos, invf, w0, saved0, S, 0, host=host_a)
    sent['b'], ex_b = scatter_of(group_b, layer_grads)
    others_b = run_exchange(ex_b, name="scatter_grads_b")
    small_names = [k for k in WEIGHTS if k not in BIG]
    grads = {k: jnp.stack([layer_grads[l][k] for l in range(DEPTH)]) for k in small_names if k != 'final_norm'}
    grads['final_norm'] = g_final_norm[0]

    pieces = [{}, {}]
    for tag, group, others in (('a', group_a, others_a), ('b', group_b, others_b)):
        mine = [sum_chips(lax.dynamic_index_in_dim(s, chip, 0, keepdims=False), o, name=f"sum_chips_{k}_l{l}")
                for s, o, (k, l) in zip(sent[tag], others, group)]
        theirs = sibling_swap(mine, name=f"swap_core_sums_{tag}")
        pieces[0].update(dict(zip(group, mine)))
        pieces[1].update(dict(zip(group, theirs)))
    small_sum = all_sum_small(_pack_rows([grads[k] for k in small_names] + [loss[0, :1]], LANE, F32), name="sum_small_grads")
    summed = _unpack_rows(small_sum, [grads[k].shape for k in small_names] + [(1,)])
    loss_total = summed[-1].reshape(())
    g_small = dict(zip(small_names, summed[:-1]))
    for k in CONV_SHARDED:
        n = wv[k].shape[-1]
        g_small[k] = lax.dynamic_slice_in_dim(g_small[k], chip * n, n, axis=g_small[k].ndim - 1)

    out_g, out_d, out_m, out_v = {}, {}, {}, {}
    for k in big_names:
        g_layers = [[pieces[0][(k, l)], pieces[1][(k, l)]] for l in range(DEPTH)]
        out_g[k], out_d[k], out_m[k], out_v[k] = adamw_layers(wv[k], g_layers, mv[k], vv[k], name=f"adamw_{k}")
    at_least_2d = lambda a: a.reshape(1, -1) if a.ndim == 1 else a
    res = adamw_small(*[[at_least_2d(d[k]) for k in small_names] for d in (wv, g_small, mv, vv)], name="adamw_small")
    out_g.update(g_small)
    for dst, r in zip((out_d, out_m, out_v), res):
        dst.update({k: a.reshape(wv[k].shape) for k, a in zip(small_names, r)})
    return (loss_total, dx.reshape(Bl, S, D), *[out_g[k] for k in WEIGHTS], *[out_d[k] for k in WEIGHTS],
            *[out_m[k] for k in WEIGHTS], *[out_v[k] for k in WEIGHTS])
```

```python
import functools
import math
from typing import Callable, NamedTuple

import jax
import jax.numpy as jnp
from jax import lax
from jax.experimental import pallas as pl
from jax.experimental.pallas import tpu as pltpu

F32 = jnp.float32
MXU_DTYPE = jnp.bfloat16
HI = lax.Precision.HIGHEST

D_MODEL = 1024
DEPTH = 2
EPS = 1e-6
SSD_HEADS = 16
SSD_HEAD_DIM = 64
SSD_WIDTH = 1024
SSD_GROUPS = 2
SSD_STATE = 128
SSD_CONV = 4
SSD_CHUNK = 128
SSD_CONV_CH = 1536
POOL_GROUPS = 4
POOL_GROUP_DIM = 128
POOL_WIDTH = 512
POOL_WINDOWS = (2, 4, 8, 16)
MLA_HEADS = 8
MLA_Q_RANK = 384
MLA_KV_RANK = 256
MLA_NOPE = 64
MLA_ROPE = 32
MLA_V = 64
MLA_QK = 96
MLA_WIDTH = 512
ROPE_THETA = 10000.0
MIX_WIDTH = 2048
IN_COLS = 3760
D_FF = 2816
FFN_CONV = 3
ADAM_LR = 0.001
ADAM_B1 = 0.9
ADAM_B2 = 0.999
ADAM_EPS = 1e-08
ADAM_WD = 0.01
ADAM_STEP = 10

LANE = 128
HALO = 8
POOL_HALO = 16
PZ0 = 0
PXBC0 = 1024
PU0 = 2560
PCQ0 = 3072
PDT0 = 3456
PCKV0 = 3584
PKPE0 = 3840
PROJ_W = 4096
HEAD_W = 128
MLA_PAD = MLA_HEADS * HEAD_W
MIXP = SSD_WIDTH + POOL_WIDTH + MLA_PAD
N_CHIPS = 4
N_DEV = 8
VMEM_LIMIT = 56 * 1024 * 1024


def _cparams(dims, vmem=None):
    return pltpu.CompilerParams(dimension_semantics=dims, vmem_limit_bytes=vmem or VMEM_LIMIT)


def _sds(shape, dtype):
    return jax.ShapeDtypeStruct(tuple(shape), dtype)


def _mx(v):
    return v.astype(MXU_DTYPE)


def _dot(a, b):
    return jnp.dot(_mx(a), _mx(b), preferred_element_type=F32)


def _dot_nt(a, b):
    return lax.dot_general(_mx(a), _mx(b), (((1,), (1,)), ((), ())), preferred_element_type=F32)


def _dot_tn(a, b):
    return lax.dot_general(_mx(a), _mx(b), (((0,), (0,)), ((), ())), preferred_element_type=F32)


def _dot_hi(a, b):
    return jnp.dot(a, b, preferred_element_type=F32, precision=HI)


def _sigmoid(v):
    return 1.0 / (1.0 + jnp.exp(-v))


def _pick(n, prefs):
    for p in prefs:
        if n % p == 0:
            return p
    return n


def matmul(a, b, *, res=None, out_dtype=F32, name, nt=False, kblock=0, tm=None, tn=None):
    M, K = a.shape
    N = b.shape[0] if nt else b.shape[1]
    assert (b.shape[1] % K == 0) if nt else (K == b.shape[0] and kblock == 0)
    tm = tm or _pick(M, (1024, 512, 256, 128))
    tn = tn or _pick(N, (1024, 1408, 1280, 512, 256, 128))

    def body(*refs):
        a_ref, b_ref = refs[:2]
        o_ref = refs[-1]
        out = (_dot_nt if nt else _dot)(a_ref[...], b_ref[...])
        if res is not None:
            out = out + refs[2][...]
        o_ref[...] = out.astype(out_dtype)

    b_spec = pl.BlockSpec((tn, K), lambda i, j: (j, kblock)) if nt else pl.BlockSpec((K, tn), lambda i, j: (0, j))
    in_specs = [pl.BlockSpec((tm, K), lambda i, j: (i, 0)), b_spec]
    args = [a, b]
    if res is not None:
        in_specs.append(pl.BlockSpec((tm, tn), lambda i, j: (i, j)))
        args.append(res)
    return pl.pallas_call(
        body, name=name, grid=(M // tm, N // tn), in_specs=in_specs,
        out_specs=pl.BlockSpec((tm, tn), lambda i, j: (i, j)), out_shape=_sds((M, N), out_dtype),
        compiler_params=_cparams(("parallel", "parallel")),
    )(*args)


def matmul_tn(a, g, *, name, tm=None, tn=None, tk=None):
    T, M = a.shape
    T2, N = g.shape
    assert T == T2
    tm = tm or _pick(M, (1408, 1280, 1024, 512, 384, 256, 128))
    tn = tn or _pick(N, (1024, 1408, 512, 256, 128))
    tk = tk or _pick(T, (1024, 512, 256, 128))
    nk = T // tk

    def body(a_ref, g_ref, o_ref, acc):
        k = pl.program_id(2)
        part = _dot_tn(a_ref[...], g_ref[...])

        @pl.when(k == 0)
        def _():
            acc[...] = part

        @pl.when(k > 0)
        def _():
            acc[...] += part

        @pl.when(k == nk - 1)
        def _():
            o_ref[...] = acc[...].astype(o_ref.dtype)

    return pl.pallas_call(
        body, name=name, grid=(M // tm, N // tn, nk),
        in_specs=[pl.BlockSpec((tk, tm), lambda i, j, k: (k, i)), pl.BlockSpec((tk, tn), lambda i, j, k: (k, j))],
        out_specs=pl.BlockSpec((tm, tn), lambda i, j, k: (i, j)), out_shape=_sds((M, N), MXU_DTYPE),
        scratch_shapes=[pltpu.VMEM((tm, tn), F32)],
        compiler_params=_cparams(("parallel", "parallel", "arbitrary")),
    )(a, g)


def rmsnorm_fwd(x, gamma, *, name, tm=512):
    T, D = x.shape
    tm = _pick(T, (tm, 256, 128))

    def body(x_ref, g_ref, o_ref):
        xv = x_ref[...]
        r = lax.rsqrt(jnp.mean(xv * xv, axis=-1, keepdims=True) + EPS)
        o_ref[...] = ((xv * r) * g_ref[...]).astype(MXU_DTYPE)

    return pl.pallas_call(
        body, name=name, grid=(T // tm,),
        in_specs=[pl.BlockSpec((tm, D), lambda i: (i, 0)), pl.BlockSpec((1, D), lambda i: (0, 0))],
        out_specs=pl.BlockSpec((tm, D), lambda i: (i, 0)), out_shape=_sds((T, D), MXU_DTYPE),
        compiler_params=_cparams(("parallel",)),
    )(x, gamma.reshape(1, D))


def _rms_bwd_tile(xv, gamma, dh):
    r = lax.rsqrt(jnp.mean(xv * xv, axis=-1, keepdims=True) + EPS)
    xh = xv * r
    dg = jnp.sum(dh * xh, axis=0, keepdims=True)
    dn = dh * gamma
    dx = r * (dn - xh * jnp.mean(dn * xh, axis=-1, keepdims=True))
    return dx, dg


def rmsnorm_bwd(x, gamma, dh, dres, *, name, tm=256):
    T, D = x.shape
    tm = _pick(T, (tm, 128))

    def body(x_ref, g_ref, dh_ref, dr_ref, dx_ref, dg_ref):
        dx, dg = _rms_bwd_tile(x_ref[...], g_ref[...], dh_ref[...].astype(F32))
        dx_ref[...] = dx + dr_ref[...]

        @pl.when(pl.program_id(0) == 0)
        def _():
            dg_ref[...] = dg

        @pl.when(pl.program_id(0) > 0)
        def _():
            dg_ref[...] += dg

    row = pl.BlockSpec((tm, D), lambda i: (i, 0))
    vec = pl.BlockSpec((1, D), lambda i: (0, 0))
    return pl.pallas_call(
        body, name=name, grid=(T // tm,), in_specs=[row, vec, row, row], out_specs=[row, vec],
        out_shape=[_sds((T, D), F32), _sds((1, D), F32)], compiler_params=_cparams(("arbitrary",)),
    )(x, gamma.reshape(1, D), dh, dres)


def final_loss(x, gamma, target, *, name="final_loss", tm=256):
    T, D = x.shape
    tm = _pick(T, (tm, 128))

    def body(x_ref, g_ref, t_ref, l_ref, dx_ref, dg_ref):
        xv = x_ref[...]
        gam = g_ref[...]
        r = lax.rsqrt(jnp.mean(xv * xv, axis=-1, keepdims=True) + EPS)
        y = (xv * r) * gam
        err = y - t_ref[...]
        part = 0.5 * jnp.sum(jnp.sum(err * err, axis=-1, keepdims=True) / D, axis=0, keepdims=True)
        dx, dg = _rms_bwd_tile(xv, gam, err / D)
        dx_ref[...] = dx

        @pl.when(pl.program_id(0) == 0)
        def _():
            dg_ref[...] = dg
            l_ref[...] = jnp.broadcast_to(part, l_ref.shape)

        @pl.when(pl.program_id(0) > 0)
        def _():
            dg_ref[...] += dg
            l_ref[...] += jnp.broadcast_to(part, l_ref.shape)

    row = pl.BlockSpec((tm, D), lambda i: (i, 0))
    vec = pl.BlockSpec((1, D), lambda i: (0, 0))
    return pl.pallas_call(
        body, name=name, grid=(T // tm,), in_specs=[row, vec, row],
        out_specs=[pl.BlockSpec((1, LANE), lambda i: (0, 0)), row, vec],
        out_shape=[_sds((1, LANE), F32), _sds((T, D), F32), _sds((1, D), F32)],
        compiler_params=_cparams(("arbitrary",)),
    )(x, gamma.reshape(1, D), target)


def _halo_prev(ts):
    return lambda i, j, off=0: (jnp.maximum(i * (ts // HALO) - 1, 0), j + off)


def _cat_prev(cur, halo, first):
    return jnp.concatenate([jnp.where(first, 0.0, halo), cur], axis=0)


def _cat_next(cur, halo, last):
    return jnp.concatenate([cur, jnp.where(last, 0.0, halo)], axis=0)


def _delayed(cat, r):
    if r == 0:
        return cat[HALO:]
    return pltpu.roll(cat, r, axis=0)[HALO:]


def _advanced(cat, r):
    n = cat.shape[0]
    if r == 0:
        return cat[:n - HALO]
    return pltpu.roll(cat, n - r, axis=0)[:n - HALO]


def _conv_pre(cat, w, b, K):
    acc = _delayed(cat, K - 1) * w[0:1, :] + b
    for k in range(1, K):
        acc = acc + _delayed(cat, K - 1 - k) * w[k:k + 1, :]
    return acc


def _pad_rows8(w):
    return jnp.pad(w, ((0, 8 - w.shape[0]), (0, 0)))


def ssd_conv_fwd(proj, w, b, S, *, name, ts=1024, tc=512):
    T = proj.shape[0]
    C, K = SSD_CONV_CH, SSD_CONV
    ts = _pick(S, (ts, 256, 128))
    off = PXBC0 // tc
    ns = S // ts

    def body(x_ref, h_ref, w_ref, b_ref, o_ref):
        first = (pl.program_id(0) % ns) == 0
        pre = _conv_pre(_cat_prev(x_ref[...], h_ref[...], first), w_ref[...], b_ref[...], K)
        o_ref[...] = pre * _sigmoid(pre)

    return pl.pallas_call(
        body, name=name, grid=(T // ts, C // tc),
        in_specs=[pl.BlockSpec((ts, tc), lambda i, j: (i, j + off)),
                  pl.BlockSpec((HALO, tc), functools.partial(_halo_prev(ts), off=off)),
                  pl.BlockSpec((8, tc), lambda i, j: (0, j)), pl.BlockSpec((1, tc), lambda i, j: (0, j))],
        out_specs=pl.BlockSpec((ts, tc), lambda i, j: (i, j)), out_shape=_sds((T, C), F32),
        compiler_params=_cparams(("parallel", "parallel")),
    )(proj, proj, _pad_rows8(w), b.reshape(1, C))


def ssd_conv_bwd_pre(proj, w, b, dxc, S, *, name, ts=1024, tc=512):
    T = proj.shape[0]
    C, K = SSD_CONV_CH, SSD_CONV
    ts = _pick(S, (ts, 256, 128))
    off = PXBC0 // tc
    ns = S // ts

    def body(x_ref, h_ref, w_ref, b_ref, d_ref, o_ref, acc_ref):
        i = pl.program_id(1)
        first = (i % ns) == 0
        cat = _cat_prev(x_ref[...], h_ref[...], first)
        pre = _conv_pre(cat, w_ref[...], b_ref[...], K)
        sg = _sigmoid(pre)
        dpre = d_ref[...] * (sg * (1.0 + pre * (1.0 - sg)))
        o_ref[...] = dpre
        rows = [jnp.sum(dpre * _delayed(cat, K - 1 - k), axis=0, keepdims=True) for k in range(K)]
        rows.append(jnp.sum(dpre, axis=0, keepdims=True))
        rows.append(jnp.zeros((8 - len(rows), dpre.shape[1]), F32))
        part = jnp.concatenate(rows, axis=0)

        @pl.when(i == 0)
        def _():
            acc_ref[...] = part

        @pl.when(i > 0)
        def _():
            acc_ref[...] += part

    hp = _halo_prev(ts)
    return pl.pallas_call(
        body, name=name, grid=(C // tc, T // ts),
        in_specs=[pl.BlockSpec((ts, tc), lambda j, i: (i, j + off)),
                  pl.BlockSpec((HALO, tc), lambda j, i: hp(i, j, off)),
                  pl.BlockSpec((8, tc), lambda j, i: (0, j)), pl.BlockSpec((1, tc), lambda j, i: (0, j)),
                  pl.BlockSpec((ts, tc), lambda j, i: (i, j))],
        out_specs=[pl.BlockSpec((ts, tc), lambda j, i: (i, j)), pl.BlockSpec((8, tc), lambda j, i: (0, j))],
        out_shape=[_sds((T, C), F32), _sds((8, C), F32)],
        compiler_params=_cparams(("parallel", "arbitrary")),
    )(proj, proj, _pad_rows8(w), b.reshape(1, C), dxc)


def conv_bwd_x(dpre, w, S, K, *, name, ts=512, tc=None):
    T, C = dpre.shape
    out_dtype = MXU_DTYPE
    ts = _pick(S, (ts, 256, 128))
    tc = tc or _pick(C, (1408, 512, 256, 128))
    ns = S // ts
    nblk = T // HALO

    def body(d_ref, h_ref, w_ref, o_ref):
        last = (pl.program_id(0) % ns) == ns - 1
        cat = _cat_next(d_ref[...], h_ref[...], last)
        wv = w_ref[...]
        acc = _advanced(cat, K - 1) * wv[0:1, :]
        for k in range(1, K):
            acc = acc + _advanced(cat, K - 1 - k) * wv[k:k + 1, :]
        o_ref[...] = acc.astype(out_dtype)

    return pl.pallas_call(
        body, name=name, grid=(T // ts, C // tc),
        in_specs=[pl.BlockSpec((ts, tc), lambda i, j: (i, j)),
                  pl.BlockSpec((HALO, tc), lambda i, j: (jnp.minimum((i + 1) * (ts // HALO), nblk - 1), j)),
                  pl.BlockSpec((8, tc), lambda i, j: (0, j))],
        out_specs=pl.BlockSpec((ts, tc), lambda i, j: (i, j)), out_shape=_sds((T, C), out_dtype),
        compiler_params=_cparams(("parallel", "parallel")),
    )(dpre, dpre, _pad_rows8(w))


def ffn_conv_gate_fwd(up, w, b, S, *, name, ts=512, tc=1408):
    T, C2 = up.shape
    C, K = C2 // 2, FFN_CONV
    ts = _pick(S, (ts, 256, 128))
    nj = C // tc
    ns = S // ts
    w8 = _pad_rows8(w)
    b2 = b.reshape(1, C2)

    def body(g_ref, gh_ref, v_ref, vh_ref, wg_ref, wv_ref, bg_ref, bv_ref, o_ref):
        first = (pl.program_id(0) % ns) == 0
        g = _conv_pre(_cat_prev(g_ref[...], gh_ref[...], first), wg_ref[...], bg_ref[...], K)
        v = _conv_pre(_cat_prev(v_ref[...], vh_ref[...], first), wv_ref[...], bv_ref[...], K)
        o_ref[...] = (g * _sigmoid(g) * v).astype(o_ref.dtype)

    hp = _halo_prev(ts)
    return pl.pallas_call(
        body, name=name, grid=(T // ts, nj),
        in_specs=[pl.BlockSpec((ts, tc), lambda i, j: (i, j)), pl.BlockSpec((HALO, tc), lambda i, j: hp(i, j)),
                  pl.BlockSpec((ts, tc), lambda i, j: (i, j + nj)), pl.BlockSpec((HALO, tc), lambda i, j: hp(i, j, nj)),
                  pl.BlockSpec((8, tc), lambda i, j: (0, j)), pl.BlockSpec((8, tc), lambda i, j: (0, j + nj)),
                  pl.BlockSpec((1, tc), lambda i, j: (0, j)), pl.BlockSpec((1, tc), lambda i, j: (0, j + nj))],
        out_specs=pl.BlockSpec((ts, tc), lambda i, j: (i, j)), out_shape=_sds((T, C), MXU_DTYPE),
        compiler_params=_cparams(("parallel", "parallel")),
    )(up, up, up, up, w8, w8, b2, b2)


def ffn_conv_gate_bwd(up, w, b, dact, S, *, name, ts=256, tc=1408):
    T, C2 = up.shape
    C, K = C2 // 2, FFN_CONV
    ts = _pick(S, (ts, 256, 128))
    nj = C // tc
    ns = S // ts
    nblk = T // HALO
    w8 = _pad_rows8(w)
    b2 = b.reshape(1, C2)

    def stats(dpre, cat):
        rows = [jnp.sum(dpre[:ts] * _delayed(cat, K - 1 - k)[:ts], axis=0, keepdims=True) for k in range(K)]
        rows.append(jnp.sum(dpre[:ts], axis=0, keepdims=True))
        rows.append(jnp.zeros((8 - len(rows), dpre.shape[1]), F32))
        return jnp.concatenate(rows, axis=0)

    def conv_t(dpre, wv):
        acc = _advanced(dpre, K - 1) * wv[0:1, :]
        for k in range(1, K):
            acc = acc + _advanced(dpre, K - 1 - k) * wv[k:k + 1, :]
        return acc

    def body(g_ref, gp_ref, gn_ref, v_ref, vp_ref, vn_ref, wg_ref, wv_ref, bg_ref, bv_ref, d_ref, dn_ref,
             dug_ref, duv_ref, ag_ref, av_ref):
        i = pl.program_id(1)
        first = (i % ns) == 0
        last = (i % ns) == ns - 1
        gcat = jnp.concatenate([jnp.where(first, 0.0, gp_ref[...]), g_ref[...], gn_ref[...]], axis=0)
        vcat = jnp.concatenate([jnp.where(first, 0.0, vp_ref[...]), v_ref[...], vn_ref[...]], axis=0)
        wg, wv = wg_ref[...], wv_ref[...]
        g = _conv_pre(gcat, wg, bg_ref[...], K)
        v = _conv_pre(vcat, wv, bv_ref[...], K)
        d = _cat_next(d_ref[...], dn_ref[...], last)
        sg = _sigmoid(g)
        dg = d * v * (sg * (1.0 + g * (1.0 - sg)))
        dv = d * (g * sg)
        dug_ref[...] = conv_t(dg, wg).astype(dug_ref.dtype)
        duv_ref[...] = conv_t(dv, wv).astype(duv_ref.dtype)
        sgp, svp = stats(dg, gcat), stats(dv, vcat)

        @pl.when(i == 0)
        def _():
            ag_ref[...] = sgp
            av_ref[...] = svp

        @pl.when(i > 0)
        def _():
            ag_ref[...] += sgp
            av_ref[...] += svp

    hp = _halo_prev(ts)
    hn = lambda i: jnp.minimum((i + 1) * (ts // HALO), nblk - 1)
    cur = lambda off: pl.BlockSpec((ts, tc), lambda j, i: (i, j + off))
    prv = lambda off: pl.BlockSpec((HALO, tc), lambda j, i: hp(i, j, off))
    nxt = lambda off: pl.BlockSpec((HALO, tc), lambda j, i: (hn(i), j + off))
    row = lambda r, off: pl.BlockSpec((r, tc), lambda j, i: (0, j + off))
    dug, duv, ag, av = pl.pallas_call(
        body, name=name, grid=(nj, T // ts),
        in_specs=[cur(0), prv(0), nxt(0), cur(nj), prv(nj), nxt(nj), row(8, 0), row(8, nj), row(1, 0), row(1, nj),
                  cur(0), nxt(0)],
        out_specs=[cur(0), cur(0), row(8, 0), row(8, 0)],
        out_shape=[_sds((T, C), MXU_DTYPE), _sds((T, C), MXU_DTYPE), _sds((8, C), F32), _sds((8, C), F32)],
        compiler_params=_cparams(("parallel", "arbitrary")),
    )(up, up, up, up, up, up, w8, w8, b2, b2, dact, dact)
    return dug, duv, jnp.concatenate([ag, av], axis=1)


def _pool_counts(pos, w):
    return jnp.minimum(pos + 1.0, float(w))


def pool_fwd(proj, pool_w, pool_scale, S, *, name, ts=512):
    T = proj.shape[0]
    C, G, GD, H = POOL_WIDTH, POOL_GROUPS, POOL_GROUP_DIM, POOL_HALO
    ts = _pick(S, (ts, 256, 128))
    ns = S // ts
    off = PU0 // C

    def body(u_ref, h_ref, w_ref, s_ref, y_ref, p_ref):
        i = pl.program_id(0)
        first = (i % ns) == 0
        cat = jnp.concatenate([jnp.where(first, 0.0, h_ref[...]), u_ref[...]], axis=0)
        pos = ((i % ns) * ts + lax.broadcasted_iota(jnp.int32, (ts, 1), 0)).astype(F32)
        sums = cat
        win = 1
        for g, wlen in enumerate(POOL_WINDOWS):
            while win < wlen:
                sums = sums + pltpu.roll(sums, win, axis=0)
                win *= 2
            sl = slice(g * GD, (g + 1) * GD)
            pooled = sums[H:, sl] / _pool_counts(pos, wlen) - cat[H:, sl]
            p_ref[:, sl] = pooled.astype(p_ref.dtype)
            y_ref[:, sl] = (_dot(pooled, w_ref[g]) * s_ref[:, sl]).astype(y_ref.dtype)

    return pl.pallas_call(
        body, name=name, grid=(T // ts,),
        in_specs=[pl.BlockSpec((ts, C), lambda i: (i, off)),
                  pl.BlockSpec((H, C), lambda i: (jnp.maximum(i * (ts // H) - 1, 0), off)),
                  pl.BlockSpec((G, GD, GD), lambda i: (0, 0, 0)), pl.BlockSpec((1, C), lambda i: (0, 0))],
        out_specs=[pl.BlockSpec((ts, C), lambda i: (i, 0)), pl.BlockSpec((ts, C), lambda i: (i, 0))],
        out_shape=[_sds((T, C), MXU_DTYPE), _sds((T, C), MXU_DTYPE)],
        compiler_params=_cparams(("parallel",)),
    )(proj, proj, _mx(pool_w), pool_scale.reshape(1, C))


def pool_bwd(dmix, pooled, pool_w, pool_scale, S, *, name, ts=512):
    T = dmix.shape[0]
    C, G, GD, H = POOL_WIDTH, POOL_GROUPS, POOL_GROUP_DIM, POOL_HALO
    ts = _pick(S, (ts, 256, 128))
    ns = S // ts
    off = SSD_WIDTH // C
    nblk = T // H

    def body(d_ref, dh_ref, p_ref, w_ref, s_ref, du_ref, dw_ref, ds_ref):
        i = pl.program_id(0)
        last = (i % ns) == ns - 1
        dcat = jnp.concatenate([d_ref[...], jnp.where(last, 0.0, dh_ref[...])], axis=0)
        n = ts + H
        pos = ((i % ns) * ts + lax.broadcasted_iota(jnp.int32, (n, 1), 0)).astype(F32)
        dws, dss = [], []
        for g, wlen in enumerate(POOL_WINDOWS):
            sl = slice(g * GD, (g + 1) * GD)
            wg = w_ref[g]
            pg = p_ref[:, sl]
            dys = dcat[:, sl] * s_ref[:, sl]
            dss.append(jnp.sum(dcat[:ts, sl] * _dot(pg, wg), axis=0, keepdims=True))
            dws.append(_dot_tn(pg, dys[:ts]))
            dp = _dot_nt(dys, wg)
            q = dp / _pool_counts(pos, wlen)
            win = 1
            while win < wlen:
                q = q + pltpu.roll(q, n - win, axis=0)
                win *= 2
            du_ref[:, sl] = (q[:ts] - dp[:ts]).astype(du_ref.dtype)
        dsp = jnp.concatenate(dss, axis=1)

        @pl.when(i == 0)
        def _():
            for g in range(G):
                dw_ref[g] = dws[g]
            ds_ref[...] = dsp

        @pl.when(i > 0)
        def _():
            for g in range(G):
                dw_ref[g] += dws[g]
            ds_ref[...] += dsp

    return pl.pallas_call(
        body, name=name, grid=(T // ts,),
        in_specs=[pl.BlockSpec((ts, C), lambda i: (i, off)),
                  pl.BlockSpec((H, C), lambda i: (jnp.minimum((i + 1) * (ts // H), nblk - 1), off)),
                  pl.BlockSpec((ts, C), lambda i: (i, 0)),
                  pl.BlockSpec((G, GD, GD), lambda i: (0, 0, 0)), pl.BlockSpec((1, C), lambda i: (0, 0))],
        out_specs=[pl.BlockSpec((ts, C), lambda i: (i, 0)), pl.BlockSpec((G, GD, GD), lambda i: (0, 0, 0)),
                   pl.BlockSpec((1, C), lambda i: (0, 0))],
        out_shape=[_sds((T, C), MXU_DTYPE), _sds((G, GD, GD), F32), _sds((1, C), F32)],
        compiler_params=_cparams(("arbitrary",)),
    )(dmix, dmix, pooled, _mx(pool_w), pool_scale.reshape(1, C))


ROPE0 = MLA_NOPE
ROPE_HALF = MLA_ROPE // 2


def _rope_tables(pos, invf):
    lane = lax.broadcasted_iota(jnp.int32, (1, HEAD_W), 1)
    ang = pos * invf
    cs, sn = jnp.cos(ang), jnp.sin(ang)
    in_a = (lane >= ROPE0) & (lane < ROPE0 + ROPE_HALF)
    in_b = (lane >= ROPE0 + ROPE_HALF) & (lane < ROPE0 + MLA_ROPE)
    return jnp.where(in_a | in_b, cs, 1.0), jnp.where(in_a, -sn, 0.0), jnp.where(in_b, sn, 0.0), in_a | in_b


def _rope(v, cosf, sin_a, sin_b):
    return (v * cosf + pltpu.roll(v, HEAD_W - ROPE_HALF, axis=1) * sin_a + pltpu.roll(v, ROPE_HALF, axis=1) * sin_b)


def _unrope(d, cosf, sin_a, sin_b):
    return (d * cosf + pltpu.roll(d * sin_a, ROPE_HALF, axis=1) + pltpu.roll(d * sin_b, HEAD_W - ROPE_HALF, axis=1))


def _rms_tile(xv, gamma):
    return (xv * lax.rsqrt(jnp.mean(xv * xv, axis=-1, keepdims=True) + EPS)) * gamma


def mla_prep_fwd(proj, pos, invf, q_norm, w_uq_p, kv_norm, w_ukv_p, *, name, tm=256):
    T = proj.shape[0]
    tm = _pick(T, (tm, 128))
    QR, KR, P = MLA_Q_RANK, MLA_KV_RANK, MLA_PAD

    def body(cq_ref, ckv_ref, kpe_ref, pos_ref, invf_ref, qn_ref, wq_ref, kn_ref, wkv_ref,
             q_ref, k_ref, v_ref, cqn_ref, ckvn_ref):
        cosf, sin_a, sin_b, _ = _rope_tables(pos_ref[...], invf_ref[...])
        cqn = _rms_tile(cq_ref[...], qn_ref[...]).astype(MXU_DTYPE)
        ckvn = _rms_tile(ckv_ref[...], kn_ref[...]).astype(MXU_DTYPE)
        cqn_ref[...] = cqn
        ckvn_ref[...] = ckvn
        qp = _dot(cqn, wq_ref[...])
        kvp = _dot(ckvn, wkv_ref[...])
        kpe = _rope(kpe_ref[...], cosf, sin_a, sin_b)
        for h in range(MLA_HEADS):
            sl = slice(h * HEAD_W, (h + 1) * HEAD_W)
            q_ref[:, sl] = (_rope(qp[:, sl], cosf, sin_a, sin_b) * ATTN_SCALE).astype(q_ref.dtype)
            k_ref[:, sl] = (kvp[:, sl] + kpe).astype(k_ref.dtype)
            v_ref[:, sl] = kvp[:, P + h * HEAD_W:P + (h + 1) * HEAD_W].astype(v_ref.dtype)

    row = lambda w: pl.BlockSpec((tm, w), lambda i: (i, 0))
    full = lambda a, b: pl.BlockSpec((a, b), lambda i: (0, 0))
    return pl.pallas_call(
        body, name=name, grid=(T // tm,),
        in_specs=[pl.BlockSpec((tm, QR), lambda i: (i, PCQ0 // QR)), pl.BlockSpec((tm, KR), lambda i: (i, PCKV0 // KR)),
                  pl.BlockSpec((tm, LANE), lambda i: (i, PKPE0 // LANE)), row(1), full(1, LANE),
                  full(1, QR), full(QR, P), full(1, KR), full(KR, 2 * P)],
        out_specs=[row(P), row(P), row(P), row(QR), row(KR)],
        out_shape=[_sds((T, P), MXU_DTYPE)] * 3 + [_sds((T, QR), MXU_DTYPE), _sds((T, KR), MXU_DTYPE)],
        compiler_params=_cparams(("parallel",)),
    )(proj, proj, proj, pos, invf, q_norm.reshape(1, QR), w_uq_p, kv_norm.reshape(1, KR), w_ukv_p)


def mla_prep_bwd(proj, pos, invf, q_norm, w_uq_p, kv_norm, w_ukv_p, dq, dk, dv, *, name, tm=256):
    T = proj.shape[0]
    tm = _pick(T, (tm, 128))
    QR, KR, P = MLA_Q_RANK, MLA_KV_RANK, MLA_PAD

    def body(cq_ref, ckv_ref, pos_ref, invf_ref, qn_ref, wq_ref, kn_ref, wkv_ref, dq_ref, dk_ref, dv_ref,
             dqp_ref, dkvp_ref, dcq_ref, dckv_ref, dkpe_ref, dqn_ref, dkn_ref):
        cosf, sin_a, sin_b, rot = _rope_tables(pos_ref[...], invf_ref[...])
        dkpe = jnp.zeros((tm, HEAD_W), F32)
        for h in range(MLA_HEADS):
            sl = slice(h * HEAD_W, (h + 1) * HEAD_W)
            dqp_ref[:, sl] = _unrope(dq_ref[:, sl] * ATTN_SCALE, cosf, sin_a, sin_b).astype(dqp_ref.dtype)
            dkh = dk_ref[:, sl]
            dkpe = dkpe + dkh
            dkvp_ref[:, sl] = dkh.astype(dkvp_ref.dtype)
            dkvp_ref[:, P + h * HEAD_W:P + (h + 1) * HEAD_W] = dv_ref[:, sl].astype(dkvp_ref.dtype)
        dkpe_ref[...] = jnp.where(rot, _unrope(dkpe, cosf, sin_a, sin_b), 0.0).astype(dkpe_ref.dtype)
        dcq, dqn = _rms_bwd_tile(cq_ref[...], qn_ref[...], _dot_nt(dqp_ref[...], wq_ref[...]))
        dckv, dkn = _rms_bwd_tile(ckv_ref[...], kn_ref[...], _dot_nt(dkvp_ref[...], wkv_ref[...]))
        dcq_ref[...] = dcq.astype(dcq_ref.dtype)
        dckv_ref[...] = dckv.astype(dckv_ref.dtype)

        @pl.when(pl.program_id(0) == 0)
        def _():
            dqn_ref[...] = dqn
            dkn_ref[...] = dkn

        @pl.when(pl.program_id(0) > 0)
        def _():
            dqn_ref[...] += dqn
            dkn_ref[...] += dkn

    row = lambda w: pl.BlockSpec((tm, w), lambda i: (i, 0))
    full = lambda a, b: pl.BlockSpec((a, b), lambda i: (0, 0))
    return pl.pallas_call(
        body, name=name, grid=(T // tm,),
        in_specs=[pl.BlockSpec((tm, QR), lambda i: (i, PCQ0 // QR)), pl.BlockSpec((tm, KR), lambda i: (i, PCKV0 // KR)),
                  row(1), full(1, LANE), full(1, QR), full(QR, P), full(1, KR), full(KR, 2 * P), row(P), row(P), row(P)],
        out_specs=[row(P), row(2 * P), row(QR), row(KR), row(LANE), full(1, QR), full(1, KR)],
        out_shape=[_sds((T, P), MXU_DTYPE), _sds((T, 2 * P), MXU_DTYPE), _sds((T, QR), MXU_DTYPE),
                   _sds((T, KR), MXU_DTYPE), _sds((T, LANE), MXU_DTYPE), _sds((1, QR), F32), _sds((1, KR), F32)],
        compiler_params=_cparams(("arbitrary",)),
    )(proj, proj, pos, invf, q_norm.reshape(1, QR), w_uq_p, kv_norm.reshape(1, KR), w_ukv_p, dq, dk, dv)


ATTN_SCALE = 1.0 / math.sqrt(MLA_QK)


def _causal_mask(i, j, blk):
    row = lax.broadcasted_iota(jnp.int32, (blk, blk), 0)
    col = lax.broadcasted_iota(jnp.int32, (blk, blk), 1)
    return col <= row + (i - j) * blk


def _hosting(hosted, grid, n_in, n_out, n_scratch):
    if hosted is None:
        return (lambda body: body), (), [], [], []
    hi, ho = len(hosted.inputs), len(hosted.out_shapes)

    def wrap(body):
        def full(*refs):
            ins, rest = refs[:n_in + hi], refs[n_in + hi:]
            outs, scr = rest[:n_out + ho], rest[n_out + ho:]
            parts = ins[n_in:], outs[n_out:], scr[n_scratch:]
            ids = [pl.program_id(d) for d in range(len(grid))]
            step = ids[0]
            for d in range(1, len(grid)):
                step = step * grid[d] + ids[d]
            total = math.prod(grid)

            @pl.when(step == 0)
            def _():
                hosted.start(*parts)

            body(*ins[:n_in], *outs[:n_out], *scr[:n_scratch])

            @pl.when(step == total // 2)
            def _():
                hosted.relay(*parts)

            @pl.when(step == total - 1)
            def _():
                hosted.finish(*parts)

        return full

    hbm = pl.BlockSpec(memory_space=pl.ANY)
    return wrap, tuple(hosted.inputs), [hbm] * ho, list(hosted.out_shapes), list(hosted.sems)


def flash_fwd(q, k, v, S, *, name, blk=512, hosted=None):
    T, P = q.shape
    blk = _pick(S, (blk, 256, 128))
    B, nq, H, W = T // S, S // blk, MLA_HEADS, HEAD_W
    grid = (B, H, nq)
    wrap, h_in, h_ospecs, h_oshapes, h_scratch = _hosting(hosted, grid, 3, 2, 0)

    def body(q_ref, k_ref, v_ref, o_ref, lse_ref):
        i = pl.program_id(2)
        qv = q_ref[...]

        def online(j, carry, masked):
            m_prev, l_prev, acc = carry
            rows = pl.ds(pl.multiple_of(j * blk, blk), blk)
            s = _dot_nt(qv, k_ref[rows, :])
            if masked:
                s = jnp.where(_causal_mask(0, 0, blk), s, -jnp.inf)
            m_new = jnp.maximum(m_prev, jnp.max(s, axis=1, keepdims=True))
            p = jnp.exp(s - m_new)
            alpha = jnp.exp(m_prev - m_new)
            return (m_new, alpha * l_prev + jnp.sum(p, axis=1, keepdims=True), alpha * acc + _dot(p, v_ref[rows, :]))

        init = (jnp.full((blk, 1), -jnp.inf, F32), jnp.zeros((blk, 1), F32), jnp.zeros((blk, W), F32))
        carry = lax.fori_loop(0, i, lambda j, c: online(j, c, False), init)
        m, l, acc = online(i, carry, True)
        o_ref[...] = acc / l
        lse_ref[...] = jnp.broadcast_to(m + jnp.log(l), (blk, W))

    qmap = lambda b, h, i: (b * nq + i, h)
    kmap = lambda b, h, i: (b, h)
    hbm = pl.BlockSpec(memory_space=pl.ANY)
    return pl.pallas_call(
        wrap(body), name=name, grid=grid,
        in_specs=[pl.BlockSpec((blk, W), qmap), pl.BlockSpec((S, W), kmap), pl.BlockSpec((S, W), kmap)] + [hbm] * len(h_in),
        out_specs=[pl.BlockSpec((blk, W), qmap), pl.BlockSpec((blk, W), qmap)] + h_ospecs,
        out_shape=[_sds((T, P), F32), _sds((T, P), F32)] + h_oshapes,
        scratch_shapes=h_scratch,
        compiler_params=_cparams(("arbitrary",) * 3 if hosted else ("parallel", "parallel", "arbitrary")),
    )(q, k, v, *h_in)


def flash_bwd(q, k, v, o, lse, dmix, S, *, name, blk=512, hosted=None):
    T, P = q.shape
    blk = _pick(S, (blk, 256, 128))
    B, nq, H, W = T // S, S // blk, MLA_HEADS, HEAD_W
    off = (SSD_WIDTH + POOL_WIDTH) // W
    grid = (B, H, nq)
    wrap, h_in, h_ospecs, h_oshapes, h_scratch = _hosting(hosted, grid, 6, 3, 1)

    def body(q_ref, k_ref, v_ref, o_ref, lse_ref, do_ref, dq_ref, dk_ref, dv_ref, delta_s):
        j = pl.program_id(2)

        @pl.when(j == 0)
        def _():
            for i in range(nq):
                rows = slice(i * blk, (i + 1) * blk)
                delta_s[rows, :] = jnp.sum(do_ref[rows, :] * o_ref[rows, :], axis=1, keepdims=True)
                dq_ref[rows, :] = jnp.zeros((blk, W), F32)

        kv, vv = k_ref[...], v_ref[...]

        def step(i, carry, masked):
            dk, dv = carry
            rows = pl.ds(pl.multiple_of(i * blk, blk), blk)
            qv, do = q_ref[rows, :], do_ref[rows, :]
            p = jnp.exp(_dot_nt(qv, kv) - lse_ref[rows, 0:1])
            if masked:
                p = jnp.where(_causal_mask(0, 0, blk), p, 0.0)
            ds = p * (_dot_nt(do, vv) - delta_s[rows, :])
            dq_ref[rows, :] += _dot(ds, kv)
            return dk + _dot_tn(ds, qv), dv + _dot_tn(p, do)

        zero = jnp.zeros((blk, W), F32)
        carry = step(j, (zero, zero), True)
        dk, dv = lax.fori_loop(j + 1, nq, lambda i, c: step(i, c, False), carry)
        dk_ref[...] = dk
        dv_ref[...] = dv

    full = lambda b, h, j: (b, h)
    kmap = lambda b, h, j: (b * nq + j, h)
    hbm = pl.BlockSpec(memory_space=pl.ANY)
    return pl.pallas_call(
        wrap(body), name=name, grid=grid,
        in_specs=[pl.BlockSpec((S, W), full), pl.BlockSpec((blk, W), kmap), pl.BlockSpec((blk, W), kmap),
                  pl.BlockSpec((S, W), full), pl.BlockSpec((S, W), full),
                  pl.BlockSpec((S, W), lambda b, h, j: (b, off + h))] + [hbm] * len(h_in),
        out_specs=[pl.BlockSpec((S, W), full), pl.BlockSpec((blk, W), kmap), pl.BlockSpec((blk, W), kmap)] + h_ospecs,
        out_shape=[_sds((T, P), F32)] * 3 + h_oshapes,
        scratch_shapes=[pltpu.VMEM((S, 1), F32)] + h_scratch,
        compiler_params=_cparams(("arbitrary",) * 3 if hosted else ("parallel", "parallel", "arbitrary")),
    )(q, k, v, o, lse, dmix, *h_in)


SSD_PAIRS = SSD_HEADS // 2
PAIRS_PER_GROUP = SSD_PAIRS // SSD_GROUPS
GN = SSD_GROUPS * SSD_STATE


def _log1p_small(e):
    return jnp.where(e < 1e-3, e * (1.0 - e * (0.5 - e / 3.0)), jnp.log(1.0 + e))


def _softplus(v):
    return jnp.maximum(v, 0.0) + _log1p_small(jnp.exp(-jnp.abs(v)))


def _ssd_decay(dt_raw, dtb, alog):
    L = dt_raw.shape[0]
    pre = dt_raw + dtb
    dt = _softplus(pre)
    a = -jnp.exp(alog)
    row = lax.broadcasted_iota(jnp.int32, (L, L), 0)
    col = lax.broadcasted_iota(jnp.int32, (L, L), 1)
    tri = row >= col
    cum = _dot_hi(tri.astype(F32), dt * a)
    return pre, dt, a, tri, cum, cum.T


def _col(m, h):
    return m[:, h:h + 1]


def _pair_sel(m, k, lo):
    return jnp.where(lo, _col(m, 2 * k), _col(m, 2 * k + 1))


def _ssd_specs(S):
    L = SSD_CHUNK
    nc = S // L
    return L, nc


def ssd_fwd(proj, xc, dtb, alog, dchan, normw, S, *, name, hosted=None):
    T = proj.shape[0]
    L, nc = _ssd_specs(S)
    B, W, N = T // S, SSD_WIDTH, SSD_STATE
    wrap, h_in, h_ospecs, h_oshapes, h_scratch = _hosting(hosted, (B, nc), 9, 3, 1)

    def body(xs_ref, bs_ref, cs_ref, dt_ref, z_ref, dtb_ref, alog_ref, dch_ref, nw_ref, y_ref, ys_ref, hin_ref, st):
        @pl.when(pl.program_id(1) == 0)
        def _():
            st[...] = jnp.zeros(st.shape, F32)

        hin_ref[...] = st[...]
        _, dt, a, tri, cum, cum_t = _ssd_decay(dt_ref[...], dtb_ref[...], alog_ref[...])
        e_cum = jnp.exp(cum)
        last = cum[L - 1:L, :]
        w_end = jnp.exp(last - cum)
        e_last = jnp.exp(last)
        lo = lax.broadcasted_iota(jnp.int32, (1, LANE), 1) < SSD_HEAD_DIM
        for g in range(SSD_GROUPS):
            bm = bs_ref[:, g * N:(g + 1) * N]
            cm = cs_ref[:, g * N:(g + 1) * N]
            bm_t = bm.T
            gmat = _dot_nt(cm, bm)
            for kk in range(PAIRS_PER_GROUP):
                k = g * PAIRS_PER_GROUP + kk
                sl = slice(k * LANE, (k + 1) * LANE)
                xv = xs_ref[:, sl]
                xdt = xv * _pair_sel(dt, k, lo)
                yd = []
                for h in (2 * k, 2 * k + 1):
                    gam = jnp.exp(jnp.where(tri, _col(cum, h) - cum_t[h:h + 1, :], -jnp.inf))
                    yd.append(_dot(gmat * gam, xdt))
                hp = st[:, sl]
                y_off = _dot(cm, hp) * _pair_sel(e_cum, k, lo)
                y_ref[:, sl] = jnp.where(lo, yd[0], yd[1]) + y_off + xv * dch_ref[:, sl]
                zmat = xdt * _pair_sel(w_end, k, lo)
                st[:, sl] = hp * _pair_sel(e_last, k, lo) + _dot(bm_t, zmat)
        y = y_ref[...]
        z = z_ref[...]
        yz = y * (z * _sigmoid(z))
        ys_ref[...] = _rms_tile(yz, nw_ref[...]).astype(ys_ref.dtype)

    r = lambda b, c: b * nc + c
    vec = lambda w: pl.BlockSpec((1, w), lambda b, c: (0, 0))
    hbm = pl.BlockSpec(memory_space=pl.ANY)
    return pl.pallas_call(
        wrap(body), name=name, grid=(B, nc),
        in_specs=[pl.BlockSpec((L, W), lambda b, c: (r(b, c), 0)),
                  pl.BlockSpec((L, GN), lambda b, c: (r(b, c), W // GN)),
                  pl.BlockSpec((L, GN), lambda b, c: (r(b, c), W // GN + 1)),
                  pl.BlockSpec((L, LANE), lambda b, c: (r(b, c), PDT0 // LANE)),
                  pl.BlockSpec((L, W), lambda b, c: (r(b, c), PZ0 // W)),
                  vec(LANE), vec(LANE), vec(W), vec(W)] + [hbm] * len(h_in),
        out_specs=[pl.BlockSpec((L, W), lambda b, c: (r(b, c), 0)), pl.BlockSpec((L, W), lambda b, c: (r(b, c), 0)),
                   pl.BlockSpec((N, W), lambda b, c: (r(b, c), 0))] + h_ospecs,
        out_shape=[_sds((T, W), F32), _sds((T, W), MXU_DTYPE), _sds((T // L * N, W), F32)] + h_oshapes,
        scratch_shapes=[pltpu.VMEM((N, W), F32)] + h_scratch,
        compiler_params=_cparams(("arbitrary", "arbitrary") if hosted else ("parallel", "arbitrary")),
    )(xc, xc, xc, proj, proj, dtb, alog, dchan, normw, *h_in)


def ssd_bwd(proj, xc, ypre, hin, dmix, dtb, alog, dchan, normw, S, *, name):
    T = proj.shape[0]
    L, nc = _ssd_specs(S)
    B, W, N = T // S, SSD_WIDTH, SSD_STATE

    def body(xs_ref, bs_ref, cs_ref, dt_ref, z_ref, y_ref, hin_ref, dys_ref, dtb_ref, alog_ref, dch_ref, nw_ref,
             dxc_ref, ddt_ref, dz_ref, sm_ref, dnw_ref, dst):
        step = pl.program_id(0) * nc + pl.program_id(1)

        @pl.when(pl.program_id(1) == 0)
        def _():
            dst[...] = jnp.zeros(dst.shape, F32)

        pre, dt, a, tri, cum, cum_t = _ssd_decay(dt_ref[...], dtb_ref[...], alog_ref[...])
        e_cum = jnp.exp(cum)
        last = cum[L - 1:L, :]
        w_end = jnp.exp(last - cum)
        e_last = jnp.exp(last)
        lane = lax.broadcasted_iota(jnp.int32, (1, LANE), 1)
        sub = lax.broadcasted_iota(jnp.int32, (LANE, 1), 0)
        lo = lane < SSD_HEAD_DIM
        is_last_row = sub == L - 1
        tri_t = (lax.broadcasted_iota(jnp.int32, (L, L), 0) <= lax.broadcasted_iota(jnp.int32, (L, L), 1))

        y, z, nw = y_ref[...], z_ref[...], nw_ref[...]
        sg = _sigmoid(z)
        gate = z * sg
        dyz, dnw = _rms_bwd_tile(y * gate, nw, dys_ref[...])
        dy_all = dyz * gate
        dz_ref[...] = (dyz * y * (sg * (1.0 + z * (1.0 - sg)))).astype(dz_ref.dtype)

        d_cum = jnp.zeros((L, LANE), F32)
        d_cum_t = jnp.zeros((LANE, L), F32)
        d_dt = jnp.zeros((L, LANE), F32)
        d_dskip = jnp.zeros((1, LANE), F32)
        for g in range(SSD_GROUPS):
            bm = bs_ref[:, g * N:(g + 1) * N]
            cm = cs_ref[:, g * N:(g + 1) * N]
            cm_t = cm.T
            gmat = _dot_nt(cm, bm)
            gmat_t = _dot_nt(bm, cm)
            d_g = jnp.zeros((L, L), F32)
            d_bm = jnp.zeros((L, N), F32)
            d_cm = jnp.zeros((L, N), F32)
            for kk in range(PAIRS_PER_GROUP):
                k = g * PAIRS_PER_GROUP + kk
                sl = slice(k * LANE, (k + 1) * LANE)
                xv = xs_ref[:, sl]
                dyv = dy_all[:, sl]
                dt_sel = _pair_sel(dt, k, lo)
                xdt = xv * dt_sel
                hp = hin_ref[:, sl]
                dh_out = dst[:, sl]
                e_sel = _pair_sel(e_cum, k, lo)
                w_sel = _pair_sel(w_end, k, lo)
                e_lane = _pair_sel(e_last, k, lo)
                y_off = _dot(cm, hp) * e_sel
                zmat = xdt * w_sel
                d_z = _dot(bm, dh_out)
                d_bm = d_bm + _dot_nt(zmat, dh_out)
                d_xdt = d_z * w_sel
                dw_full = d_z * zmat
                hh = dh_out * hp
                d_r = dyv * e_sel
                d_cm = d_cm + _dot_nt(d_r, hp)
                dst[:, sl] = dh_out * e_lane + _dot(cm_t, d_r)
                dyoff_full = dyv * y_off
                for j, h in enumerate((2 * k, 2 * k + 1)):
                    mine = lo if j == 0 else jnp.logical_not(lo)
                    hot = lane == h
                    dyh = jnp.where(mine, dyv, 0.0)
                    gam = jnp.exp(jnp.where(tri, _col(cum, h) - cum_t[h:h + 1, :], -jnp.inf))
                    gam_t = jnp.exp(jnp.where(tri_t, cum_t[h:h + 1, :] - _col(cum, h), -jnp.inf))
                    mx = gmat * gam
                    d_xdt = d_xdt + _dot(gmat_t * gam_t, dyh)
                    d_mx = jnp.where(tri, _dot_nt(dyh, xdt), 0.0)
                    d_g = d_g + d_mx * gam
                    d_seg = d_mx * mx
                    row_l = (jnp.sum(d_seg, axis=1, keepdims=True)
                             + jnp.sum(jnp.where(mine, dyoff_full - dw_full, 0.0), axis=1, keepdims=True))
                    at_end = (jnp.sum(jnp.where(mine, dw_full, 0.0), keepdims=True)
                              + jnp.sum(jnp.where(mine, hh, 0.0), keepdims=True) * _col(e_last, h))
                    d_cum = d_cum + jnp.where(hot, row_l + jnp.where(is_last_row, at_end, 0.0), 0.0)
                    d_cum_t = d_cum_t - jnp.where(sub == h, jnp.sum(d_seg, axis=0, keepdims=True), 0.0)
                    d_dskip = d_dskip + jnp.where(hot, jnp.sum(jnp.where(mine, dyv * xv, 0.0), keepdims=True), 0.0)
                for j, h in enumerate((2 * k, 2 * k + 1)):
                    mine = lo if j == 0 else jnp.logical_not(lo)
                    d_dt = d_dt + jnp.where(lane == h, jnp.sum(jnp.where(mine, d_xdt * xv, 0.0), axis=1, keepdims=True), 0.0)
                dxc_ref[:, sl] = d_xdt * dt_sel + dyv * dch_ref[:, sl]
            dxc_ref[:, W + g * N:W + (g + 1) * N] = d_bm + _dot_tn(d_g, cm)
            dxc_ref[:, W + GN + g * N:W + GN + (g + 1) * N] = d_cm + _dot(d_g, bm)

        d_cum = d_cum + d_cum_t.T
        d_da = _dot_hi(jnp.logical_not(tri).astype(F32) + (lax.broadcasted_iota(jnp.int32, (L, L), 0)
                                                              == lax.broadcasted_iota(jnp.int32, (L, L), 1)).astype(F32), d_cum)
        d_dt = d_dt + d_da * a
        heads = lane < SSD_HEADS
        d_pre = jnp.where(heads, d_dt * _sigmoid(pre), 0.0)
        ddt_ref[...] = d_pre.astype(ddt_ref.dtype)
        d_alog = jnp.sum(d_da * dt, axis=0, keepdims=True) * a
        part = jnp.concatenate([jnp.where(heads, d_alog, 0.0), jnp.sum(d_pre, axis=0, keepdims=True), d_dskip,
                                jnp.zeros((5, LANE), F32)], axis=0)

        @pl.when(step == 0)
        def _():
            sm_ref[...] = part
            dnw_ref[...] = dnw

        @pl.when(step > 0)
        def _():
            sm_ref[...] += part
            dnw_ref[...] += dnw

    r = lambda b, c: b * nc + (nc - 1 - c)
    vec = lambda w: pl.BlockSpec((1, w), lambda b, c: (0, 0))
    blk = lambda w, j: pl.BlockSpec((L, w), lambda b, c: (r(b, c), j))
    return pl.pallas_call(
        body, name=name, grid=(B, nc),
        in_specs=[blk(W, 0), blk(GN, W // GN), blk(GN, W // GN + 1), blk(LANE, PDT0 // LANE), blk(W, PZ0 // W),
                  blk(W, 0), pl.BlockSpec((N, W), lambda b, c: (r(b, c), 0)), blk(W, 0),
                  vec(LANE), vec(LANE), vec(W), vec(W)],
        out_specs=[blk(SSD_CONV_CH, 0), blk(LANE, 0), blk(W, 0), pl.BlockSpec((8, LANE), lambda b, c: (0, 0)), vec(W)],
        out_shape=[_sds((T, SSD_CONV_CH), F32), _sds((T, LANE), MXU_DTYPE), _sds((T, W), MXU_DTYPE),
                   _sds((8, LANE), F32), _sds((1, W), F32)],
        scratch_shapes=[pltpu.VMEM((N, W), F32)],
        compiler_params=_cparams(("arbitrary", "arbitrary")),
    )(xc, xc, xc, proj, proj, ypre, hin, dmix, dtb, alog, dchan, normw)


def _adamw_math(w, g, m, v):
    m = ADAM_B1 * m + (1.0 - ADAM_B1) * g
    v = ADAM_B2 * v + (1.0 - ADAM_B2) * (g * g)
    m_hat = m / (1.0 - ADAM_B1 ** ADAM_STEP)
    v_hat = v / (1.0 - ADAM_B2 ** ADAM_STEP)
    delta = -ADAM_LR * (m_hat / (jnp.sqrt(v_hat) + ADAM_EPS) + ADAM_WD * w)
    return delta, m, v


def adamw_layers(w, g_layers, m, v, *, name, tr=256):
    L, A, B = w.shape
    tr = _pick(A, (tr, 192, 176, 128, 64, 32, 16, 8))
    na = A // tr
    n = len(g_layers[0])

    def body(*refs):
        w_ref, m_ref, v_ref = refs[0], refs[1 + L * n], refs[2 + L * n]
        g_ref, d_ref, nm_ref, nv_ref = refs[3 + L * n:]
        layer = pl.program_id(0)
        g = None
        for l in range(L):
            parts = refs[1 + l * n:1 + (l + 1) * n]
            gl = parts[0][...]
            for p in parts[1:]:
                gl = gl + p[...]
            g = gl if g is None else jnp.where(layer == l, gl, g)
        g_ref[...] = g
        d_ref[...], nm_ref[...], nv_ref[...] = _adamw_math(w_ref[...], g, m_ref[...], v_ref[...])

    def g_spec(l):
        return pl.BlockSpec((tr, B), lambda layer, i: (jnp.where(layer == l, i, jnp.where(layer < l, 0, na - 1)), 0))

    spec = pl.BlockSpec((None, tr, B), lambda layer, i: (layer, i, 0))
    return pl.pallas_call(
        body, name=name, grid=(L, na), in_specs=[spec] + [g_spec(l) for l in range(L) for _ in range(n)] + [spec] * 2,
        out_specs=[spec] * 4, out_shape=[_sds((L, A, B), F32)] * 4, compiler_params=_cparams(("arbitrary", "arbitrary")),
    )(w, *[p for parts in g_layers for p in parts], m, v)


def adamw_small(ws, gs, ms, vs, *, name):
    n = len(ws)

    def body(*refs):
        w_refs, g_refs, m_refs, v_refs = (refs[i * n:(i + 1) * n] for i in range(4))
        d_refs, nm_refs, nv_refs = (refs[(4 + i) * n:(5 + i) * n] for i in range(3))
        for a in range(n):
            d_refs[a][...], nm_refs[a][...], nv_refs[a][...] = _adamw_math(
                w_refs[a][...], g_refs[a][...], m_refs[a][...], v_refs[a][...])

    vm = pl.BlockSpec(memory_space=pltpu.VMEM)
    out = pl.pallas_call(
        body, name=name, in_specs=[vm] * (4 * n), out_specs=[vm] * (3 * n),
        out_shape=[_sds(w.shape, F32) for w in ws] * 3, compiler_params=pltpu.CompilerParams(vmem_limit_bytes=VMEM_LIMIT),
    )(*ws, *gs, *ms, *vs)
    return out[:n], out[n:2 * n], out[2 * n:]


def _my_place():
    return lax.axis_index("x"), lax.axis_index("y"), lax.axis_index("c")


def _other_chips(x, y):
    return [(1 - x, y), (x, 1 - y), (1 - x, 1 - y)]


def relation_of(chip, me):
    d = chip ^ me
    return jnp.where(d == 2, 0, jnp.where(d == 1, 1, jnp.where(d == 3, 2, -1)))


class Exchange(NamedTuple):
    inputs: tuple
    out_shapes: tuple
    sems: tuple
    start: Callable
    relay: Callable
    finish: Callable


def scatter_exchange(srcs):
    n = len(srcs)

    def copies(ins, outs, sems):
        x, y, c = _my_place()
        out = []
        for k, (px, py) in enumerate(_other_chips(x, y)):
            for a in range(n):
                out.append(pltpu.make_async_remote_copy(
                    src_ref=ins[a].at[2 * px + py], dst_ref=outs[a].at[k], send_sem=sems[0].at[k, a],
                    recv_sem=sems[1].at[k, a], device_id=(px, py, c), device_id_type=pl.DeviceIdType.MESH))
        return out

    def start(ins, outs, sems):
        for cp in copies(ins, outs, sems):
            cp.start()

    def finish(ins, outs, sems):
        cps = copies(ins, outs, sems)
        for cp in cps:
            cp.wait_recv()
        for cp in cps:
            cp.wait_send()

    return Exchange(tuple(srcs), tuple(_sds((3,) + s.shape[1:], s.dtype) for s in srcs),
                    (pltpu.SemaphoreType.DMA((3, n)),) * 2, start, lambda *a: None, finish)


def run_exchange(ex, *, name):
    n_in, n_out = len(ex.inputs), len(ex.out_shapes)

    def body(*refs):
        parts = refs[:n_in], refs[n_in:n_in + n_out], refs[n_in + n_out:]
        ex.start(*parts)
        ex.relay(*parts)
        ex.finish(*parts)

    hbm = pl.BlockSpec(memory_space=pl.ANY)
    return pl.pallas_call(
        body, name=name, in_specs=[hbm] * n_in, out_specs=[hbm] * n_out, out_shape=list(ex.out_shapes),
        scratch_shapes=list(ex.sems), compiler_params=pltpu.CompilerParams(has_side_effects=True),
    )(*ex.inputs)


def sibling_swap(srcs, *, name):
    n = len(srcs)

    def body(*refs):
        src_refs, out_refs, (send_sems, recv_sems) = refs[:n], refs[n:2 * n], refs[2 * n:]
        x, y, c = _my_place()
        copies = [pltpu.make_async_remote_copy(
            src_ref=src_refs[a], dst_ref=out_refs[a], send_sem=send_sems.at[a], recv_sem=recv_sems.at[a],
            device_id=(x, y, 1 - c), device_id_type=pl.DeviceIdType.MESH) for a in range(n)]
        for cp in copies:
            cp.start()
        for cp in copies:
            cp.wait_recv()
        for cp in copies:
            cp.wait_send()

    hbm = pl.BlockSpec(memory_space=pl.ANY)
    return pl.pallas_call(
        body, name=name, in_specs=[hbm] * n, out_specs=[hbm] * n, out_shape=[_sds(s.shape, s.dtype) for s in srcs],
        scratch_shapes=[pltpu.SemaphoreType.DMA((n,)), pltpu.SemaphoreType.DMA((n,))],
        compiler_params=pltpu.CompilerParams(has_side_effects=True),
    )(*srcs)


def gather_exchange(srcs):
    nch = len(srcs)
    halves = [s.shape[0] // 2 for s in srcs]
    assert all(2 * h == s.shape[0] and h % 16 == 0 for h, s in zip(halves, srcs))
    pieces = [(k, q) for k in range(3) for q in range(nch)]

    def makers(ins, outs, sems):
        ici_send, ici_recv, d2d_send, d2d_recv = sems
        x, y, c = _my_place()
        peers = _other_chips(x, y)

        def rows(core, q):
            return pl.ds(core * halves[q], halves[q])

        def ici(k, q):
            px, py = peers[k]
            return pltpu.make_async_remote_copy(
                src_ref=ins[q].at[rows(c, q)], dst_ref=outs[q].at[k, rows(c, q)], send_sem=ici_send.at[k, q],
                recv_sem=ici_recv.at[k, q], device_id=(px, py, c), device_id_type=pl.DeviceIdType.MESH)

        def d2d(k, q, core):
            return pltpu.make_async_remote_copy(
                src_ref=outs[q].at[k, rows(core, q)], dst_ref=outs[q].at[k, rows(core, q)],
                send_sem=d2d_send.at[k, q], recv_sem=d2d_recv.at[k, q], device_id=(x, y, 1 - c),
                device_id_type=pl.DeviceIdType.MESH)

        return ici, d2d, c

    def start(*refs):
        ici, _, _ = makers(*refs)
        for k, q in pieces:
            ici(k, q).start()

    def relay(*refs):
        ici, d2d, c = makers(*refs)
        for k, q in pieces:
            ici(k, q).wait_recv()
            d2d(k, q, c).start()

    def finish(*refs):
        ici, d2d, c = makers(*refs)
        for k, q in pieces:
            d2d(k, q, 1 - c).wait_recv()
        for k, q in pieces:
            ici(k, q).wait_send()
            d2d(k, q, c).wait_send()

    return Exchange(tuple(srcs), tuple(_sds((3,) + s.shape, s.dtype) for s in srcs),
                    (pltpu.SemaphoreType.DMA((3, nch)),) * 4, start, relay, finish)


def all_sum_small(vec, *, name):
    R, C = vec.shape

    def body(v_ref, out_ref, buf, send_sems, recv_sems):
        x, y, c = _my_place()
        me = 4 * x + 2 * y + c
        buf[me] = v_ref[...]
        copies = []
        for k in range(1, N_DEV):
            px, py, pc = x ^ (k >> 2), y ^ ((k >> 1) & 1), c ^ (k & 1)
            copies.append(pltpu.make_async_remote_copy(
                src_ref=v_ref, dst_ref=buf.at[me], send_sem=send_sems.at[k - 1], recv_sem=recv_sems.at[k - 1],
                device_id=(px, py, pc), device_id_type=pl.DeviceIdType.MESH))
        for cp in copies:
            cp.start()
        for k in range(1, N_DEV):
            px, py, pc = x ^ (k >> 2), y ^ ((k >> 1) & 1), c ^ (k & 1)
            pltpu.make_async_remote_copy(
                src_ref=v_ref, dst_ref=buf.at[4 * px + 2 * py + pc], send_sem=send_sems.at[k - 1],
                recv_sem=recv_sems.at[k - 1], device_id=(px, py, pc), device_id_type=pl.DeviceIdType.MESH).wait_recv()
        for cp in copies:
            cp.wait_send()
        acc = buf[0]
        for d in range(1, N_DEV):
            acc = acc + buf[d]
        out_ref[...] = acc

    return pl.pallas_call(
        body, name=name, in_specs=[pl.BlockSpec(memory_space=pltpu.VMEM)], out_specs=pl.BlockSpec(memory_space=pltpu.VMEM),
        out_shape=_sds((R, C), F32),
        scratch_shapes=[pltpu.VMEM((N_DEV, R, C), F32), pltpu.SemaphoreType.DMA((N_DEV - 1,)),
                        pltpu.SemaphoreType.DMA((N_DEV - 1,))],
        compiler_params=pltpu.CompilerParams(has_side_effects=True, vmem_limit_bytes=VMEM_LIMIT),
    )(vec)


def sum_chips(own, others, *, name, tr=512):
    R, C = own.shape
    tr = _pick(R, (tr, 384, 352, 256, 128, 64, 32, 16))

    def body(o_ref, p_ref, s_ref):
        acc = o_ref[...].astype(F32)
        for k in range(3):
            acc = acc + p_ref[k].astype(F32)
        s_ref[...] = acc

    return pl.pallas_call(
        body, name=name, grid=(R // tr,),
        in_specs=[pl.BlockSpec((tr, C), lambda i: (i, 0)), pl.BlockSpec((3, tr, C), lambda i: (0, i, 0))],
        out_specs=pl.BlockSpec((tr, C), lambda i: (i, 0)), out_shape=_sds((R, C), F32),
        compiler_params=_cparams(("parallel",)),
    )(own, others)


WEIGHTS = ['attn_norm', 'w_in', 'ssd_conv_w', 'ssd_conv_b', 'ssd_dt_bias', 'ssd_a_log', 'ssd_d', 'ssd_norm', 'pool_w',
           'pool_scale', 'mla_q_norm', 'mla_w_uq', 'mla_kv_norm', 'mla_w_ukv', 'w_out', 'ffn_norm', 'ffn_w_up',
           'ffn_conv_w', 'ffn_conv_b', 'ffn_w_down', 'final_norm']
BIG = {'w_in': 2, 'mla_w_uq': 2, 'mla_w_ukv': 2, 'w_out': 1, 'ffn_w_up': 2, 'ffn_w_down': 1}
CONV_SHARDED = ('ssd_conv_w', 'ffn_conv_w')


def _zeros_cols(w, n):
    return jnp.zeros((w.shape[0], n), w.dtype)


def _w_in_to_padded(w):
    return jnp.concatenate([w[:, 0:2560], w[:, 2576:3088], w[:, 3088:3472], w[:, 2560:2576], _zeros_cols(w, 112),
                            w[:, 3472:3728], _zeros_cols(w, 64), w[:, 3728:3760], _zeros_cols(w, 32 + 128)], axis=1)


def _w_in_from_padded(g):
    return jnp.concatenate([g[:, 0:2560], g[:, PDT0:PDT0 + SSD_HEADS], g[:, PU0:PU0 + POOL_WIDTH],
                            g[:, PCQ0:PCQ0 + MLA_Q_RANK], g[:, PCKV0:PCKV0 + MLA_KV_RANK],
                            g[:, PKPE0 + ROPE0:PKPE0 + ROPE0 + MLA_ROPE]], axis=1)


def _w_uq_to_padded(w):
    r = w.reshape(MLA_Q_RANK, MLA_HEADS, MLA_QK)
    return jnp.pad(r, ((0, 0), (0, 0), (0, HEAD_W - MLA_QK))).reshape(MLA_Q_RANK, MLA_PAD)


def _w_uq_from_padded(g):
    return g.reshape(MLA_Q_RANK, MLA_HEADS, HEAD_W)[:, :, :MLA_QK].reshape(MLA_Q_RANK, MLA_HEADS * MLA_QK)


def _w_ukv_to_padded(w):
    r = w.reshape(MLA_KV_RANK, MLA_HEADS, MLA_NOPE + MLA_V)
    pad = lambda t: jnp.pad(t, ((0, 0), (0, 0), (0, HEAD_W - t.shape[2]))).reshape(MLA_KV_RANK, MLA_PAD)
    return jnp.concatenate([pad(r[:, :, :MLA_NOPE]), pad(r[:, :, MLA_NOPE:])], axis=1)


def _w_ukv_from_padded(g):
    kk = g[:, :MLA_PAD].reshape(MLA_KV_RANK, MLA_HEADS, HEAD_W)[:, :, :MLA_NOPE]
    vv = g[:, MLA_PAD:].reshape(MLA_KV_RANK, MLA_HEADS, HEAD_W)[:, :, :MLA_V]
    return jnp.concatenate([kk, vv], axis=2).reshape(MLA_KV_RANK, MLA_HEADS * (MLA_NOPE + MLA_V))


def _w_out_to_padded(w):
    att = w[SSD_WIDTH + POOL_WIDTH:].reshape(MLA_HEADS, MLA_V, D_MODEL)
    att = jnp.pad(att, ((0, 0), (0, HEAD_W - MLA_V), (0, 0))).reshape(MLA_PAD, D_MODEL)
    return jnp.concatenate([w[:SSD_WIDTH + POOL_WIDTH], att], axis=0)


def _w_out_from_padded(g):
    att = g[SSD_WIDTH + POOL_WIDTH:].reshape(MLA_HEADS, HEAD_W, D_MODEL)[:, :MLA_V].reshape(MLA_WIDTH, D_MODEL)
    return jnp.concatenate([g[:SSD_WIDTH + POOL_WIDTH], att], axis=0)


def _pad_lanes(v, n=LANE):
    return jnp.pad(v.reshape(1, -1), ((0, 0), (0, n - v.size)))


def _pack_rows(parts, cols, dtype, row_multiple=16):
    flat = jnp.concatenate([p.astype(dtype).reshape(-1) for p in parts])
    rows = -(-flat.size // (cols * row_multiple)) * row_multiple
    return jnp.pad(flat, (0, rows * cols - flat.size)).reshape(rows, cols)


def _unpack_rows(packed, shapes):
    flat = packed.reshape(-1)
    out, at = [], 0
    for s in shapes:
        n = math.prod(s)
        out.append(flat[at:at + n].reshape(s))
        at += n
    return out


def _split_for_chips(g, axis):
    a, b = g.shape
    if axis == 0:
        return g.reshape(N_CHIPS, a // N_CHIPS, b)
    return g.reshape(a, N_CHIPS, b // N_CHIPS).transpose(1, 0, 2)


_MATMUL_OPERANDS = {'w_in': ('w_in_p', _w_in_to_padded), 'mla_w_uq': ('w_uq_p', _w_uq_to_padded),
                    'mla_w_ukv': ('w_ukv_p', _w_ukv_to_padded), 'w_out': ('w_out_p', _w_out_to_padded),
                    'ffn_w_up': ('w_up', lambda a: a), 'ffn_w_down': ('w_down', lambda a: a)}


def _matmul_weights(full):
    return {_MATMUL_OPERANDS[k][0]: _MATMUL_OPERANDS[k][1](a) for k, a in full.items()}


def _layer_weights(full, small, l):
    w = _matmul_weights(full)
    for k in ('attn_norm', 'ssd_conv_w', 'ssd_conv_b', 'ssd_norm', 'pool_w', 'pool_scale', 'mla_q_norm', 'mla_kv_norm',
              'ffn_norm', 'ffn_conv_w', 'ffn_conv_b'):
        w[k] = small[k][l]
    w['dtb'] = _pad_lanes(small['ssd_dt_bias'][l])
    w['alog'] = _pad_lanes(small['ssd_a_log'][l])
    w['dchan'] = jnp.repeat(small['ssd_d'][l], SSD_HEAD_DIM).reshape(1, SSD_WIDTH)
    w['ssd_norm'] = w['ssd_norm'].reshape(1, SSD_WIDTH)
    return w


def _layer_fwd(x, pos, invf, w, S, l, hosted=None, scan_hosted=None, late_weights=None):
    n = lambda s: f"{s}_l{l}"
    h1 = rmsnorm_fwd(x, w['attn_norm'], name=n("attn_norm"))
    proj = matmul(h1, w['w_in_p'], name=n("w_in"))
    xc = ssd_conv_fwd(proj, w['ssd_conv_w'], w['ssd_conv_b'], S, name=n("ssd_conv"))
    ypre, yssd, hin, *arrived = ssd_fwd(proj, xc, w['dtb'], w['alog'], w['dchan'], w['ssd_norm'], S, name=n("ssd_scan"),
                                        hosted=scan_hosted)
    if late_weights:
        w = {**w, **late_weights(arrived)}
    ypool, pooled = pool_fwd(proj, w['pool_w'], w['pool_scale'], S, name=n("pool"))
    q, k, v, cqn, ckvn = mla_prep_fwd(proj, pos, invf, w['mla_q_norm'], w['w_uq_p'], w['mla_kv_norm'], w['w_ukv_p'],
                                      name=n("mla_prep"))
    o, lse, *exchanged = flash_fwd(q, k, v, S, name=n("attention"), hosted=hosted)
    mix = jnp.concatenate([yssd, ypool, o.astype(MXU_DTYPE)], axis=1)
    x2 = matmul(mix, w['w_out_p'], res=x, name=n("w_out"))
    h2 = rmsnorm_fwd(x2, w['ffn_norm'], name=n("ffn_norm"))
    up = matmul(h2, w['w_up'], name=n("ffn_up"))
    act = ffn_conv_gate_fwd(up, w['ffn_conv_w'], w['ffn_conv_b'], S, name=n("ffn_conv_gate"))
    x3 = matmul(act, w['w_down'], res=x2, name=n("ffn_down"))
    saved = dict(x=x, h1=h1, proj=proj, xc=xc, ypre=ypre, hin=hin, pooled=pooled, q=q, k=k, v=v, cqn=cqn, ckvn=ckvn,
                 o=o, lse=lse, mix=mix, x2=x2, h2=h2, up=up, act=act)
    return x3, saved, w, exchanged


def _layer_bwd(dx3, pos, invf, w, s, S, l, host=None):
    n = lambda t: f"{t}_l{l}"
    g = {}
    dact = matmul(dx3, w['w_down'], nt=True, name=n("d_ffn_down"))
    g['ffn_w_down'] = matmul_tn(s['act'], dx3, name=n("g_ffn_down"))
    dup_g, dup_v, st = ffn_conv_gate_bwd(s['up'], w['ffn_conv_w'], w['ffn_conv_b'], dact, S, name=n("d_ffn_conv_gate"))
    g['ffn_conv_w'], g['ffn_conv_b'] = st[:FFN_CONV], st[FFN_CONV]
    dh2 = matmul(dup_g, w['w_up'], nt=True, kblock=0, name=n("d_ffn_up_g"))
    dh2 = matmul(dup_v, w['w_up'], nt=True, kblock=1, res=dh2, name=n("d_ffn_up_v"))
    g['ffn_w_up'] = jnp.concatenate([matmul_tn(s['h2'], dup_g, name=n("g_ffn_up_g")),
                                     matmul_tn(s['h2'], dup_v, name=n("g_ffn_up_v"))], axis=1)
    dx2, gn = rmsnorm_bwd(s['x2'], w['ffn_norm'], dh2, dx3, name=n("d_ffn_norm"))
    g['ffn_norm'] = gn[0]
    dmix = matmul(dx2, w['w_out_p'], nt=True, name=n("d_w_out"))
    g['w_out'] = _w_out_from_padded(matmul_tn(s['mix'], dx2, name=n("g_w_out")))
    dxc, ddt, dz, sm, gsn = ssd_bwd(s['proj'], s['xc'], s['ypre'], s['hin'], dmix, w['dtb'], w['alog'], w['dchan'],
                                    w['ssd_norm'], S, name=n("d_ssd_scan"))
    g['ssd_a_log'], g['ssd_dt_bias'], g['ssd_d'] = sm[0, :SSD_HEADS], sm[1, :SSD_HEADS], sm[2, :SSD_HEADS]
    g['ssd_norm'] = gsn[0]
    dpre, st = ssd_conv_bwd_pre(s['proj'], w['ssd_conv_w'], w['ssd_conv_b'], dxc, S, name=n("d_ssd_conv_act"))
    g['ssd_conv_w'], g['ssd_conv_b'] = st[:SSD_CONV], st[SSD_CONV]
    dxbc = conv_bwd_x(dpre, w['ssd_conv_w'], S, SSD_CONV, name=n("d_ssd_conv"))
    du, g['pool_w'], gps = pool_bwd(dmix, s['pooled'], w['pool_w'], w['pool_scale'], S, name=n("d_pool"))
    g['pool_scale'] = gps[0]
    dq, dk, dv, *exchanged = flash_bwd(s['q'], s['k'], s['v'], s['o'], s['lse'], dmix, S, name=n("d_attention"),
                                       hosted=host(g) if host else None)
    dqp, dkvp, dcq, dckv, dkpe, gqn, gkn = mla_prep_bwd(s['proj'], pos, invf, w['mla_q_norm'], w['w_uq_p'],
                                                        w['mla_kv_norm'], w['w_ukv_p'], dq, dk, dv, name=n("d_mla_prep"))
    g['mla_q_norm'], g['mla_kv_norm'] = gqn[0], gkn[0]
    g['mla_w_uq'] = _w_uq_from_padded(matmul_tn(s['cqn'], dqp, name=n("g_w_uq")))
    g['mla_w_ukv'] = _w_ukv_from_padded(matmul_tn(s['ckvn'], dkvp, name=n("g_w_ukv")))
    dproj = jnp.concatenate([dz, dxbc, du, dcq, ddt, dckv, dkpe, jnp.zeros_like(dkpe)], axis=1)
    dh1 = matmul(dproj, w['w_in_p'], nt=True, name=n("d_w_in"))
    g['w_in'] = _w_in_from_padded(matmul_tn(s['h1'], dproj, name=n("g_w_in")))
    dx, gn = rmsnorm_bwd(s['x'], w['attn_norm'], dh1, dx2, name=n("d_attn_norm"))
    g['attn_norm'] = gn[0]
    return dx, g, exchanged


def _rope_inputs(positions):
    pos = positions.reshape(-1, 1).astype(F32)
    inv_freq = ROPE_THETA ** (-jnp.arange(0, MLA_ROPE, 2, dtype=F32) / MLA_ROPE)
    invf = jnp.concatenate([jnp.zeros((ROPE0,), F32), inv_freq, inv_freq,
                            jnp.zeros((HEAD_W - ROPE0 - MLA_ROPE,), F32)]).reshape(1, HEAD_W)
    return pos, invf


EARLY_GRADS = ('w_out', 'ffn_w_up', 'ffn_w_down')


def kernel(x, positions, attn_norm, w_in, ssd_conv_w, ssd_conv_b, ssd_dt_bias, ssd_a_log, ssd_d, ssd_norm, pool_w, pool_scale, mla_q_norm, mla_w_uq, mla_kv_norm, mla_w_ukv, w_out, ffn_norm, ffn_w_up, ffn_conv_w, ffn_conv_b, ffn_w_down, final_norm, loss_target, m_attn_norm, m_w_in, m_ssd_conv_w, m_ssd_conv_b, m_ssd_dt_bias, m_ssd_a_log, m_ssd_d, m_ssd_norm, m_pool_w, m_pool_scale, m_mla_q_norm, m_mla_w_uq, m_mla_kv_norm, m_mla_w_ukv, m_w_out, m_ffn_norm, m_ffn_w_up, m_ffn_conv_w, m_ffn_conv_b, m_ffn_w_down, m_final_norm, v_attn_norm, v_w_in, v_ssd_conv_w, v_ssd_conv_b, v_ssd_dt_bias, v_ssd_a_log, v_ssd_d, v_ssd_norm, v_pool_w, v_pool_scale, v_mla_q_norm, v_mla_w_uq, v_mla_kv_norm, v_mla_w_ukv, v_w_out, v_ffn_norm, v_ffn_w_up, v_ffn_conv_w, v_ffn_conv_b, v_ffn_w_down, v_final_norm):
    wv = dict(zip(WEIGHTS, (attn_norm, w_in, ssd_conv_w, ssd_conv_b, ssd_dt_bias, ssd_a_log, ssd_d, ssd_norm, pool_w,
                            pool_scale, mla_q_norm, mla_w_uq, mla_kv_norm, mla_w_ukv, w_out, ffn_norm, ffn_w_up,
                            ffn_conv_w, ffn_conv_b, ffn_w_down, final_norm)))
    mv = dict(zip(WEIGHTS, (m_attn_norm, m_w_in, m_ssd_conv_w, m_ssd_conv_b, m_ssd_dt_bias, m_ssd_a_log, m_ssd_d,
                            m_ssd_norm, m_pool_w, m_pool_scale, m_mla_q_norm, m_mla_w_uq, m_mla_kv_norm, m_mla_w_ukv,
                            m_w_out, m_ffn_norm, m_ffn_w_up, m_ffn_conv_w, m_ffn_conv_b, m_ffn_w_down, m_final_norm)))
    vv = dict(zip(WEIGHTS, (v_attn_norm, v_w_in, v_ssd_conv_w, v_ssd_conv_b, v_ssd_dt_bias, v_ssd_a_log, v_ssd_d,
                            v_ssd_norm, v_pool_w, v_pool_scale, v_mla_q_norm, v_mla_w_uq, v_mla_kv_norm, v_mla_w_ukv,
                            v_w_out, v_ffn_norm, v_ffn_w_up, v_ffn_conv_w, v_ffn_conv_b, v_ffn_w_down, v_final_norm)))
    Bl, S, D = x.shape
    chip = 2 * lax.axis_index("x") + lax.axis_index("y")
    core = lax.axis_index("c")

    big_names = list(BIG)
    first_names = ['w_in']
    rest_names = [k for k in big_names if k not in first_names]

    def shards(l, names):
        return [wv[k][l].astype(MXU_DTYPE) for k in names]

    def whole_weights(names, own, others):
        rel = [relation_of(j, chip) for j in range(N_CHIPS)]
        return {k: jnp.concatenate(
            [jnp.where(r < 0, mine, jnp.where(r == 0, theirs[0], jnp.where(r == 1, theirs[1], theirs[2]))) for r in rel],
            axis=BIG[k] - 1) for k, mine, theirs in zip(names, own, others)}

    first_others = run_exchange(gather_exchange(shards(0, first_names)), name="gather_w_in_l0")
    placed = []
    for k in CONV_SHARDED:
        sh = wv[k]
        whole = jnp.zeros(sh.shape[:-1] + (sh.shape[-1] * N_CHIPS,), F32)
        whole = lax.dynamic_update_slice_in_dim(whole, sh, chip * sh.shape[-1], axis=sh.ndim - 1)
        placed.append(jnp.where(core == 1, whole, 0.0))
    conv_full = _unpack_rows(all_sum_small(_pack_rows(placed, LANE, F32), name="gather_conv_weights"),
                             [p.shape for p in placed])
    small = {k: wv[k] for k in WEIGHTS if k not in BIG}
    small.update(dict(zip(CONV_SHARDED, conv_full)))

    T = Bl * S
    pos, invf = _rope_inputs(positions)
    group_a = [(k, 1) for k in big_names] + [(k, 0) for k in EARLY_GRADS]
    group_b = [(k, 0) for k in big_names if k not in EARLY_GRADS]

    def scatter_of(group, layer_grads):
        send = [_split_for_chips(layer_grads[l][k], BIG[k] - 1) for k, l in group]
        return send, scatter_exchange(send)

    layer_grads = [None] * DEPTH
    sent = {}
    w0 = _layer_weights(whole_weights(first_names, shards(0, first_names), first_others), small, 0)
    h, saved0, w0, others1 = _layer_fwd(
        x.reshape(T, D), pos, invf, w0, S, 0, hosted=gather_exchange(shards(1, big_names)),
        scan_hosted=gather_exchange(shards(0, rest_names)),
        late_weights=lambda got: _matmul_weights(whole_weights(rest_names, shards(0, rest_names), got)))
    w1 = _layer_weights(whole_weights(big_names, shards(1, big_names), others1), small, 1)
    h, saved1, w1, _ = _layer_fwd(h, pos, invf, w1, S, 1)
    loss, dh, g_final_norm = final_loss(h, small['final_norm'], loss_target.reshape(T, D))
    dh, layer_grads[1], _ = _layer_bwd(dh, pos, invf, w1, saved1, S, 1)

    def host_a(early):
        layer_grads[0] = early
        sent['a'], ex = scatter_of(group_a, layer_grads)
        return ex

    dx, layer_grads[0], others_a = _layer_bwd(dh, pos, invf, w0, saved0, S, 0, host=host_a)
    sent['b'], ex_b = scatter_of(group_b, layer_grads)
    others_b = run_exchange(ex_b, name="scatter_grads_b")
    small_names = [k for k in WEIGHTS if k not in BIG]
    grads = {k: jnp.stack([layer_grads[l][k] for l in range(DEPTH)]) for k in small_names if k != 'final_norm'}
    grads['final_norm'] = g_final_norm[0]

    pieces = [{}, {}]
    for tag, group, others in (('a', group_a, others_a), ('b', group_b, others_b)):
        mine = [sum_chips(lax.dynamic_index_in_dim(s, chip, 0, keepdims=False), o, name=f"sum_chips_{k}_l{l}")
                for s, o, (k, l) in zip(sent[tag], others, group)]
        theirs = sibling_swap(mine, name=f"swap_core_sums_{tag}")
        pieces[0].update(dict(zip(group, mine)))
        pieces[1].update(dict(zip(group, theirs)))
    small_sum = all_sum_small(_pack_rows([grads[k] for k in small_names] + [loss[0, :1]], LANE, F32), name="sum_small_grads")
    summed = _unpack_rows(small_sum, [grads[k].shape for k in small_names] + [(1,)])
    loss_total = summed[-1].reshape(())
    g_small = dict(zip(small_names, summed[:-1]))
    for k in CONV_SHARDED:
        n = wv[k].shape[-1]
        g_small[k] = lax.dynamic_slice_in_dim(g_small[k], chip * n, n, axis=g_small[k].ndim - 1)

    out_g, out_d, out_m, out_v = {}, {}, {}, {}
    for k in big_names:
        g_layers = [[pieces[0][(k, l)], pieces[1][(k, l)]] for l in range(DEPTH)]
        out_g[k], out_d[k], out_m[k], out_v[k] = adamw_layers(wv[k], g_layers, mv[k], vv[k], name=f"adamw_{k}")
    at_least_2d = lambda a: a.reshape(1, -1) if a.ndim == 1 else a
    res = adamw_small(*[[at_least_2d(d[k]) for k in small_names] for d in (wv, g_small, mv, vv)], name="adamw_small")
    out_g.update(g_small)
    for dst, r in zip((out_d, out_m, out_v), res):
        dst.update({k: a.reshape(wv[k].shape) for k, a in zip(small_names, r)})
    return (loss_total, dx.reshape(Bl, S, D), *[out_g[k] for k in WEIGHTS], *[out_d[k] for k in WEIGHTS],
            *[out_m[k] for k in WEIGHTS], *[out_v[k] for k in WEIGHTS])
```

```python
import functools
import math
from typing import Callable, NamedTuple

import jax
import jax.numpy as jnp
from jax import lax
from jax.experimental import pallas as pl
from jax.experimental.pallas import tpu as pltpu

F32 = jnp.float32
MXU_DTYPE = jnp.bfloat16
HI = lax.Precision.HIGHEST

D_MODEL = 1024
DEPTH = 2
EPS = 1e-6
SSD_HEADS = 16
SSD_HEAD_DIM = 64
SSD_WIDTH = 1024
SSD_GROUPS = 2
SSD_STATE = 128
SSD_CONV = 4
SSD_CHUNK = 128
SSD_CONV_CH = 1536
POOL_GROUPS = 4
POOL_GROUP_DIM = 128
POOL_WIDTH = 512
POOL_WINDOWS = (2, 4, 8, 16)
MLA_HEADS = 8
MLA_Q_RANK = 384
MLA_KV_RANK = 256
MLA_NOPE = 64
MLA_ROPE = 32
MLA_V = 64
MLA_QK = 96
MLA_WIDTH = 512
ROPE_THETA = 10000.0
MIX_WIDTH = 2048
IN_COLS = 3760
D_FF = 2816
FFN_CONV = 3
ADAM_LR = 0.001
ADAM_B1 = 0.9
ADAM_B2 = 0.999
ADAM_EPS = 1e-08
ADAM_WD = 0.01
ADAM_STEP = 10

LANE = 128
HALO = 8
POOL_HALO = 16
PZ0 = 0
PXBC0 = 1024
PU0 = 2560
PCQ0 = 3072
PDT0 = 3456
PCKV0 = 3584
PKPE0 = 3840
PROJ_W = 4096
HEAD_W = 128
MLA_PAD = MLA_HEADS * HEAD_W
MIXP = SSD_WIDTH + POOL_WIDTH + MLA_PAD
N_CHIPS = 4
N_DEV = 8
VMEM_LIMIT = 56 * 1024 * 1024


def _cparams(dims, vmem=None):
    return pltpu.CompilerParams(dimension_semantics=dims, vmem_limit_bytes=vmem or VMEM_LIMIT)


def _sds(shape, dtype):
    return jax.ShapeDtypeStruct(tuple(shape), dtype)


def _mx(v):
    return v.astype(MXU_DTYPE)


def _dot(a, b):
    return jnp.dot(_mx(a), _mx(b), preferred_element_type=F32)


def _dot_nt(a, b):
    return lax.dot_general(_mx(a), _mx(b), (((1,), (1,)), ((), ())), preferred_element_type=F32)


def _dot_tn(a, b):
    return lax.dot_general(_mx(a), _mx(b), (((0,), (0,)), ((), ())), preferred_element_type=F32)


def _dot_hi(a, b):
    return jnp.dot(a, b, preferred_element_type=F32, precision=HI)


def _sigmoid(v):
    return 1.0 / (1.0 + jnp.exp(-v))


def _pick(n, prefs):
    for p in prefs:
        if n % p == 0:
            return p
    return n


def matmul(a, b, *, res=None, out_dtype=F32, name, nt=False, kblock=0, tm=None, tn=None, hosted=None):
    M, K = a.shape
    N = b.shape[0] if nt else b.shape[1]
    assert (b.shape[1] % K == 0) if nt else (K == b.shape[0] and kblock == 0)
    tm = tm or _pick(M, (1024, 512, 256, 128))
    tn = tn or _pick(N, (1024, 1408, 1280, 512, 256, 128))
    grid = (M // tm, N // tn)
    wrap, h_in, h_ospecs, h_oshapes, h_scratch = _hosting(hosted, grid, 2 if res is None else 3, 1, 0)

    def body(*refs):
        a_ref, b_ref = refs[:2]
        o_ref = refs[-1]
        out = (_dot_nt if nt else _dot)(a_ref[...], b_ref[...])
        if res is not None:
            out = out + refs[2][...]
        o_ref[...] = out.astype(out_dtype)

    b_spec = pl.BlockSpec((tn, K), lambda i, j: (j, kblock)) if nt else pl.BlockSpec((K, tn), lambda i, j: (0, j))
    in_specs = [pl.BlockSpec((tm, K), lambda i, j: (i, 0)), b_spec]
    args = [a, b]
    if res is not None:
        in_specs.append(pl.BlockSpec((tm, tn), lambda i, j: (i, j)))
        args.append(res)
    out = pl.pallas_call(
        wrap(body), name=name, grid=grid, in_specs=in_specs + [pl.BlockSpec(memory_space=pl.ANY)] * len(h_in),
        out_specs=[pl.BlockSpec((tm, tn), lambda i, j: (i, j))] + h_ospecs, out_shape=[_sds((M, N), out_dtype)] + h_oshapes,
        scratch_shapes=h_scratch,
        compiler_params=_cparams(("arbitrary", "arbitrary") if hosted else ("parallel", "parallel")),
    )(*args, *h_in)
    return out if hosted else out[0]


def matmul_tn(a, g, *, name, tm=None, tn=None, tk=None):
    T, M = a.shape
    T2, N = g.shape
    assert T == T2
    tm = tm or _pick(M, (1408, 1280, 1024, 512, 384, 256, 128))
    tn = tn or _pick(N, (1024, 1408, 512, 256, 128))
    tk = tk or _pick(T, (1024, 512, 256, 128))
    nk = T // tk

    def body(a_ref, g_ref, o_ref, acc):
        k = pl.program_id(2)
        part = _dot_tn(a_ref[...], g_ref[...])

        @pl.when(k == 0)
        def _():
            acc[...] = part

        @pl.when(k > 0)
        def _():
            acc[...] += part

        @pl.when(k == nk - 1)
        def _():
            o_ref[...] = acc[...].astype(o_ref.dtype)

    return pl.pallas_call(
        body, name=name, grid=(M // tm, N // tn, nk),
        in_specs=[pl.BlockSpec((tk, tm), lambda i, j, k: (k, i)), pl.BlockSpec((tk, tn), lambda i, j, k: (k, j))],
        out_specs=pl.BlockSpec((tm, tn), lambda i, j, k: (i, j)), out_shape=_sds((M, N), MXU_DTYPE),
        scratch_shapes=[pltpu.VMEM((tm, tn), F32)],
        compiler_params=_cparams(("parallel", "parallel", "arbitrary")),
    )(a, g)


def rmsnorm_fwd(x, gamma, *, name, tm=512):
    T, D = x.shape
    tm = _pick(T, (tm, 256, 128))

    def body(x_ref, g_ref, o_ref):
        xv = x_ref[...]
        r = lax.rsqrt(jnp.mean(xv * xv, axis=-1, keepdims=True) + EPS)
        o_ref[...] = ((xv * r) * g_ref[...]).astype(MXU_DTYPE)

    return pl.pallas_call(
        body, name=name, grid=(T // tm,),
        in_specs=[pl.BlockSpec((tm, D), lambda i: (i, 0)), pl.BlockSpec((1, D), lambda i: (0, 0))],
        out_specs=pl.BlockSpec((tm, D), lambda i: (i, 0)), out_shape=_sds((T, D), MXU_DTYPE),
        compiler_params=_cparams(("parallel",)),
    )(x, gamma.reshape(1, D))


def _rms_bwd_tile(xv, gamma, dh):
    r = lax.rsqrt(jnp.mean(xv * xv, axis=-1, keepdims=True) + EPS)
    xh = xv * r
    dg = jnp.sum(dh * xh, axis=0, keepdims=True)
    dn = dh * gamma
    dx = r * (dn - xh * jnp.mean(dn * xh, axis=-1, keepdims=True))
    return dx, dg


def rmsnorm_bwd(x, gamma, dh, dres, *, name, tm=256):
    T, D = x.shape
    tm = _pick(T, (tm, 128))

    def body(x_ref, g_ref, dh_ref, dr_ref, dx_ref, dg_ref):
        dx, dg = _rms_bwd_tile(x_ref[...], g_ref[...], dh_ref[...].astype(F32))
        dx_ref[...] = dx + dr_ref[...]

        @pl.when(pl.program_id(0) == 0)
        def _():
            dg_ref[...] = dg

        @pl.when(pl.program_id(0) > 0)
        def _():
            dg_ref[...] += dg

    row = pl.BlockSpec((tm, D), lambda i: (i, 0))
    vec = pl.BlockSpec((1, D), lambda i: (0, 0))
    return pl.pallas_call(
        body, name=name, grid=(T // tm,), in_specs=[row, vec, row, row], out_specs=[row, vec],
        out_shape=[_sds((T, D), F32), _sds((1, D), F32)], compiler_params=_cparams(("arbitrary",)),
    )(x, gamma.reshape(1, D), dh, dres)


def final_loss(x, gamma, target, *, name="final_loss", tm=256):
    T, D = x.shape
    tm = _pick(T, (tm, 128))

    def body(x_ref, g_ref, t_ref, l_ref, dx_ref, dg_ref):
        xv = x_ref[...]
        gam = g_ref[...]
        r = lax.rsqrt(jnp.mean(xv * xv, axis=-1, keepdims=True) + EPS)
        y = (xv * r) * gam
        err = y - t_ref[...]
        part = 0.5 * jnp.sum(jnp.sum(err * err, axis=-1, keepdims=True) / D, axis=0, keepdims=True)
        dx, dg = _rms_bwd_tile(xv, gam, err / D)
        dx_ref[...] = dx

        @pl.when(pl.program_id(0) == 0)
        def _():
            dg_ref[...] = dg
            l_ref[...] = jnp.broadcast_to(part, l_ref.shape)

        @pl.when(pl.program_id(0) > 0)
        def _():
            dg_ref[...] += dg
            l_ref[...] += jnp.broadcast_to(part, l_ref.shape)

    row = pl.BlockSpec((tm, D), lambda i: (i, 0))
    vec = pl.BlockSpec((1, D), lambda i: (0, 0))
    return pl.pallas_call(
        body, name=name, grid=(T // tm,), in_specs=[row, vec, row],
        out_specs=[pl.BlockSpec((1, LANE), lambda i: (0, 0)), row, vec],
        out_shape=[_sds((1, LANE), F32), _sds((T, D), F32), _sds((1, D), F32)],
        compiler_params=_cparams(("arbitrary",)),
    )(x, gamma.reshape(1, D), target)


def _halo_prev(ts):
    return lambda i, j, off=0: (jnp.maximum(i * (ts // HALO) - 1, 0), j + off)


def _cat_prev(cur, halo, first):
    return jnp.concatenate([jnp.where(first, 0.0, halo), cur], axis=0)


def _cat_next(cur, halo, last):
    return jnp.concatenate([cur, jnp.where(last, 0.0, halo)], axis=0)


def _delayed(cat, r):
    if r == 0:
        return cat[HALO:]
    return pltpu.roll(cat, r, axis=0)[HALO:]


def _advanced(cat, r):
    n = cat.shape[0]
    if r == 0:
        return cat[:n - HALO]
    return pltpu.roll(cat, n - r, axis=0)[:n - HALO]


def _conv_pre(cat, w, b, K):
    acc = _delayed(cat, K - 1) * w[0:1, :] + b
    for k in range(1, K):
        acc = acc + _delayed(cat, K - 1 - k) * w[k:k + 1, :]
    return acc


def _pad_rows8(w):
    return jnp.pad(w, ((0, 8 - w.shape[0]), (0, 0)))


def ssd_conv_fwd(proj, w, b, S, *, name, ts=1024, tc=512):
    T = proj.shape[0]
    C, K = SSD_CONV_CH, SSD_CONV
    ts = _pick(S, (ts, 256, 128))
    off = PXBC0 // tc
    ns = S // ts

    def body(x_ref, h_ref, w_ref, b_ref, o_ref):
        first = (pl.program_id(0) % ns) == 0
        pre = _conv_pre(_cat_prev(x_ref[...], h_ref[...], first), w_ref[...], b_ref[...], K)
        o_ref[...] = pre * _sigmoid(pre)

    return pl.pallas_call(
        body, name=name, grid=(T // ts, C // tc),
        in_specs=[pl.BlockSpec((ts, tc), lambda i, j: (i, j + off)),
                  pl.BlockSpec((HALO, tc), functools.partial(_halo_prev(ts), off=off)),
                  pl.BlockSpec((8, tc), lambda i, j: (0, j)), pl.BlockSpec((1, tc), lambda i, j: (0, j))],
        out_specs=pl.BlockSpec((ts, tc), lambda i, j: (i, j)), out_shape=_sds((T, C), F32),
        compiler_params=_cparams(("parallel", "parallel")),
    )(proj, proj, _pad_rows8(w), b.reshape(1, C))


def _conv_stats(dpre, cat, K, ts):
    rows = [jnp.sum(dpre[:ts] * _delayed(cat, K - 1 - k)[:ts], axis=0, keepdims=True) for k in range(K)]
    rows.append(jnp.sum(dpre[:ts], axis=0, keepdims=True))
    rows.append(jnp.zeros((8 - len(rows), dpre.shape[1]), F32))
    return jnp.concatenate(rows, axis=0)


def _conv_transposed(dpre, wv, K):
    acc = _advanced(dpre, K - 1) * wv[0:1, :]
    for k in range(1, K):
        acc = acc + _advanced(dpre, K - 1 - k) * wv[k:k + 1, :]
    return acc


def ssd_conv_bwd(proj, w, b, dxc, S, *, name, ts=512, tc=512):
    T = proj.shape[0]
    C, K = SSD_CONV_CH, SSD_CONV
    ts = _pick(S, (ts, 256, 128))
    off = PXBC0 // tc
    ns = S // ts
    nblk = T // HALO

    def body(x_ref, xp_ref, xn_ref, w_ref, b_ref, d_ref, dn_ref, o_ref, acc_ref):
        i = pl.program_id(1)
        first = (i % ns) == 0
        last = (i % ns) == ns - 1
        cat = jnp.concatenate([jnp.where(first, 0.0, xp_ref[...]), x_ref[...], xn_ref[...]], axis=0)
        wv = w_ref[...]
        pre = _conv_pre(cat, wv, b_ref[...], K)
        sg = _sigmoid(pre)
        dpre = _cat_next(d_ref[...], dn_ref[...], last) * (sg * (1.0 + pre * (1.0 - sg)))
        o_ref[...] = _conv_transposed(dpre, wv, K).astype(o_ref.dtype)
        part = _conv_stats(dpre, cat, K, ts)

        @pl.when(i == 0)
        def _():
            acc_ref[...] = part

        @pl.when(i > 0)
        def _():
            acc_ref[...] += part

    hp = _halo_prev(ts)
    hn = lambda i: jnp.minimum((i + 1) * (ts // HALO), nblk - 1)
    return pl.pallas_call(
        body, name=name, grid=(C // tc, T // ts),
        in_specs=[pl.BlockSpec((ts, tc), lambda j, i: (i, j + off)),
                  pl.BlockSpec((HALO, tc), lambda j, i: hp(i, j, off)),
                  pl.BlockSpec((HALO, tc), lambda j, i: (hn(i), j + off)),
                  pl.BlockSpec((8, tc), lambda j, i: (0, j)), pl.BlockSpec((1, tc), lambda j, i: (0, j)),
                  pl.BlockSpec((ts, tc), lambda j, i: (i, j)), pl.BlockSpec((HALO, tc), lambda j, i: (hn(i), j))],
        out_specs=[pl.BlockSpec((ts, tc), lambda j, i: (i, j)), pl.BlockSpec((8, tc), lambda j, i: (0, j))],
        out_shape=[_sds((T, C), MXU_DTYPE), _sds((8, C), F32)],
        compiler_params=_cparams(("parallel", "arbitrary")),
    )(proj, proj, proj, _pad_rows8(w), b.reshape(1, C), dxc, dxc)


def ffn_conv_gate_fwd(up, w, b, S, *, name, ts=512, tc=1408):
    T, C2 = up.shape
    C, K = C2 // 2, FFN_CONV
    ts = _pick(S, (ts, 256, 128))
    nj = C // tc
    ns = S // ts
    w8 = _pad_rows8(w)
    b2 = b.reshape(1, C2)

    def body(g_ref, gh_ref, v_ref, vh_ref, wg_ref, wv_ref, bg_ref, bv_ref, o_ref):
        first = (pl.program_id(0) % ns) == 0
        g = _conv_pre(_cat_prev(g_ref[...], gh_ref[...], first), wg_ref[...], bg_ref[...], K)
        v = _conv_pre(_cat_prev(v_ref[...], vh_ref[...], first), wv_ref[...], bv_ref[...], K)
        o_ref[...] = (g * _sigmoid(g) * v).astype(o_ref.dtype)

    hp = _halo_prev(ts)
    return pl.pallas_call(
        body, name=name, grid=(T // ts, nj),
        in_specs=[pl.BlockSpec((ts, tc), lambda i, j: (i, j)), pl.BlockSpec((HALO, tc), lambda i, j: hp(i, j)),
                  pl.BlockSpec((ts, tc), lambda i, j: (i, j + nj)), pl.BlockSpec((HALO, tc), lambda i, j: hp(i, j, nj)),
                  pl.BlockSpec((8, tc), lambda i, j: (0, j)), pl.BlockSpec((8, tc), lambda i, j: (0, j + nj)),
                  pl.BlockSpec((1, tc), lambda i, j: (0, j)), pl.BlockSpec((1, tc), lambda i, j: (0, j + nj))],
        out_specs=pl.BlockSpec((ts, tc), lambda i, j: (i, j)), out_shape=_sds((T, C), MXU_DTYPE),
        compiler_params=_cparams(("parallel", "parallel")),
    )(up, up, up, up, w8, w8, b2, b2)


def ffn_conv_gate_bwd(up, w, b, dact, S, *, name, ts=256, tc=1408):
    T, C2 = up.shape
    C, K = C2 // 2, FFN_CONV
    ts = _pick(S, (ts, 256, 128))
    nj = C // tc
    ns = S // ts
    nblk = T // HALO
    w8 = _pad_rows8(w)
    b2 = b.reshape(1, C2)

    def body(g_ref, gp_ref, gn_ref, v_ref, vp_ref, vn_ref, wg_ref, wv_ref, bg_ref, bv_ref, d_ref, dn_ref,
             dug_ref, duv_ref, ag_ref, av_ref):
        i = pl.program_id(1)
        first = (i % ns) == 0
        last = (i % ns) == ns - 1
        gcat = jnp.concatenate([jnp.where(first, 0.0, gp_ref[...]), g_ref[...], gn_ref[...]], axis=0)
        vcat = jnp.concatenate([jnp.where(first, 0.0, vp_ref[...]), v_ref[...], vn_ref[...]], axis=0)
        wg, wv = wg_ref[...], wv_ref[...]
        g = _conv_pre(gcat, wg, bg_ref[...], K)
        v = _conv_pre(vcat, wv, bv_ref[...], K)
        d = _cat_next(d_ref[...], dn_ref[...], last)
        sg = _sigmoid(g)
        dg = d * v * (sg * (1.0 + g * (1.0 - sg)))
        dv = d * (g * sg)
        dug_ref[...] = _conv_transposed(dg, wg, K).astype(dug_ref.dtype)
        duv_ref[...] = _conv_transposed(dv, wv, K).astype(duv_ref.dtype)
        sgp, svp = _conv_stats(dg, gcat, K, ts), _conv_stats(dv, vcat, K, ts)

        @pl.when(i == 0)
        def _():
            ag_ref[...] = sgp
            av_ref[...] = svp

        @pl.when(i > 0)
        def _():
            ag_ref[...] += sgp
            av_ref[...] += svp

    hp = _halo_prev(ts)
    hn = lambda i: jnp.minimum((i + 1) * (ts // HALO), nblk - 1)
    cur = lambda off: pl.BlockSpec((ts, tc), lambda j, i: (i, j + off))
    prv = lambda off: pl.BlockSpec((HALO, tc), lambda j, i: hp(i, j, off))
    nxt = lambda off: pl.BlockSpec((HALO, tc), lambda j, i: (hn(i), j + off))
    row = lambda r, off: pl.BlockSpec((r, tc), lambda j, i: (0, j + off))
    dug, duv, ag, av = pl.pallas_call(
        body, name=name, grid=(nj, T // ts),
        in_specs=[cur(0), prv(0), nxt(0), cur(nj), prv(nj), nxt(nj), row(8, 0), row(8, nj), row(1, 0), row(1, nj),
                  cur(0), nxt(0)],
        out_specs=[cur(0), cur(0), row(8, 0), row(8, 0)],
        out_shape=[_sds((T, C), MXU_DTYPE), _sds((T, C), MXU_DTYPE), _sds((8, C), F32), _sds((8, C), F32)],
        compiler_params=_cparams(("parallel", "arbitrary")),
    )(up, up, up, up, up, up, w8, w8, b2, b2, dact, dact)
    return dug, duv, jnp.concatenate([ag, av], axis=1)


def _pool_counts(pos, w):
    return jnp.minimum(pos + 1.0, float(w))


def pool_fwd(proj, pool_w, pool_scale, S, *, name, ts=512):
    T = proj.shape[0]
    C, G, GD, H = POOL_WIDTH, POOL_GROUPS, POOL_GROUP_DIM, POOL_HALO
    ts = _pick(S, (ts, 256, 128))
    ns = S // ts
    off = PU0 // C

    def body(u_ref, h_ref, w_ref, s_ref, y_ref, p_ref):
        i = pl.program_id(0)
        first = (i % ns) == 0
        cat = jnp.concatenate([jnp.where(first, 0.0, h_ref[...]), u_ref[...]], axis=0)
        pos = ((i % ns) * ts + lax.broadcasted_iota(jnp.int32, (ts, 1), 0)).astype(F32)
        sums = cat
        win = 1
        for g, wlen in enumerate(POOL_WINDOWS):
            while win < wlen:
                sums = sums + pltpu.roll(sums, win, axis=0)
                win *= 2
            sl = slice(g * GD, (g + 1) * GD)
            pooled = sums[H:, sl] / _pool_counts(pos, wlen) - cat[H:, sl]
            p_ref[:, sl] = pooled.astype(p_ref.dtype)
            y_ref[:, sl] = (_dot(pooled, w_ref[g]) * s_ref[:, sl]).astype(y_ref.dtype)

    return pl.pallas_call(
        body, name=name, grid=(T // ts,),
        in_specs=[pl.BlockSpec((ts, C), lambda i: (i, off)),
                  pl.BlockSpec((H, C), lambda i: (jnp.maximum(i * (ts // H) - 1, 0), off)),
                  pl.BlockSpec((G, GD, GD), lambda i: (0, 0, 0)), pl.BlockSpec((1, C), lambda i: (0, 0))],
        out_specs=[pl.BlockSpec((ts, C), lambda i: (i, 0)), pl.BlockSpec((ts, C), lambda i: (i, 0))],
        out_shape=[_sds((T, C), MXU_DTYPE), _sds((T, C), MXU_DTYPE)],
        compiler_params=_cparams(("parallel",)),
    )(proj, proj, _mx(pool_w), pool_scale.reshape(1, C))


def pool_bwd(dmix, pooled, pool_w, pool_scale, S, *, name, ts=512):
    T = dmix.shape[0]
    C, G, GD, H = POOL_WIDTH, POOL_GROUPS, POOL_GROUP_DIM, POOL_HALO
    ts = _pick(S, (ts, 256, 128))
    ns = S // ts
    off = SSD_WIDTH // C
    nblk = T // H

    def body(d_ref, dh_ref, p_ref, w_ref, s_ref, du_ref, dw_ref, ds_ref):
        i = pl.program_id(0)
        last = (i % ns) == ns - 1
        dcat = jnp.concatenate([d_ref[...], jnp.where(last, 0.0, dh_ref[...])], axis=0)
        n = ts + H
        pos = ((i % ns) * ts + lax.broadcasted_iota(jnp.int32, (n, 1), 0)).astype(F32)
        dws, dss = [], []
        for g, wlen in enumerate(POOL_WINDOWS):
            sl = slice(g * GD, (g + 1) * GD)
            wg = w_ref[g]
            pg = p_ref[:, sl]
            dys = dcat[:, sl] * s_ref[:, sl]
            dss.append(jnp.sum(dcat[:ts, sl] * _dot(pg, wg), axis=0, keepdims=True))
            dws.append(_dot_tn(pg, dys[:ts]))
            dp = _dot_nt(dys, wg)
            q = dp / _pool_counts(pos, wlen)
            win = 1
            while win < wlen:
                q = q + pltpu.roll(q, n - win, axis=0)
                win *= 2
            du_ref[:, sl] = (q[:ts] - dp[:ts]).astype(du_ref.dtype)
        dsp = jnp.concatenate(dss, axis=1)

        @pl.when(i == 0)
        def _():
            for g in range(G):
                dw_ref[g] = dws[g]
            ds_ref[...] = dsp

        @pl.when(i > 0)
        def _():
            for g in range(G):
                dw_ref[g] += dws[g]
            ds_ref[...] += dsp

    return pl.pallas_call(
        body, name=name, grid=(T // ts,),
        in_specs=[pl.BlockSpec((ts, C), lambda i: (i, off)),
                  pl.BlockSpec((H, C), lambda i: (jnp.minimum((i + 1) * (ts // H), nblk - 1), off)),
                  pl.BlockSpec((ts, C), lambda i: (i, 0)),
                  pl.BlockSpec((G, GD, GD), lambda i: (0, 0, 0)), pl.BlockSpec((1, C), lambda i: (0, 0))],
        out_specs=[pl.BlockSpec((ts, C), lambda i: (i, 0)), pl.BlockSpec((G, GD, GD), lambda i: (0, 0, 0)),
                   pl.BlockSpec((1, C), lambda i: (0, 0))],
        out_shape=[_sds((T, C), MXU_DTYPE), _sds((G, GD, GD), F32), _sds((1, C), F32)],
        compiler_params=_cparams(("arbitrary",)),
    )(dmix, dmix, pooled, _mx(pool_w), pool_scale.reshape(1, C))


ROPE0 = MLA_NOPE
ROPE_HALF = MLA_ROPE // 2


def _rope_tables(pos, invf):
    lane = lax.broadcasted_iota(jnp.int32, (1, HEAD_W), 1)
    ang = pos * invf
    cs, sn = jnp.cos(ang), jnp.sin(ang)
    in_a = (lane >= ROPE0) & (lane < ROPE0 + ROPE_HALF)
    in_b = (lane >= ROPE0 + ROPE_HALF) & (lane < ROPE0 + MLA_ROPE)
    return jnp.where(in_a | in_b, cs, 1.0), jnp.where(in_a, -sn, 0.0), jnp.where(in_b, sn, 0.0), in_a | in_b


def _rope(v, cosf, sin_a, sin_b):
    return (v * cosf + pltpu.roll(v, HEAD_W - ROPE_HALF, axis=1) * sin_a + pltpu.roll(v, ROPE_HALF, axis=1) * sin_b)


def _unrope(d, cosf, sin_a, sin_b):
    return (d * cosf + pltpu.roll(d * sin_a, ROPE_HALF, axis=1) + pltpu.roll(d * sin_b, HEAD_W - ROPE_HALF, axis=1))


def _rms_tile(xv, gamma):
    return (xv * lax.rsqrt(jnp.mean(xv * xv, axis=-1, keepdims=True) + EPS)) * gamma


def mla_prep_fwd(proj, pos, invf, q_norm, w_uq_p, kv_norm, w_ukv_p, *, name, tm=256):
    T = proj.shape[0]
    tm = _pick(T, (tm, 128))
    QR, KR, P = MLA_Q_RANK, MLA_KV_RANK, MLA_PAD

    def body(cq_ref, ckv_ref, kpe_ref, pos_ref, invf_ref, qn_ref, wq_ref, kn_ref, wkv_ref,
             q_ref, k_ref, v_ref, cqn_ref, ckvn_ref):
        cosf, sin_a, sin_b, _ = _rope_tables(pos_ref[...], invf_ref[...])
        cqn = _rms_tile(cq_ref[...], qn_ref[...]).astype(MXU_DTYPE)
        ckvn = _rms_tile(ckv_ref[...], kn_ref[...]).astype(MXU_DTYPE)
        cqn_ref[...] = cqn
        ckvn_ref[...] = ckvn
        qp = _dot(cqn, wq_ref[...])
        kvp = _dot(ckvn, wkv_ref[...])
        kpe = _rope(kpe_ref[...], cosf, sin_a, sin_b)
        for h in range(MLA_HEADS):
            sl = slice(h * HEAD_W, (h + 1) * HEAD_W)
            q_ref[:, sl] = (_rope(qp[:, sl], cosf, sin_a, sin_b) * ATTN_SCALE).astype(q_ref.dtype)
            k_ref[:, sl] = (kvp[:, sl] + kpe).astype(k_ref.dtype)
            v_ref[:, sl] = kvp[:, P + h * HEAD_W:P + (h + 1) * HEAD_W].astype(v_ref.dtype)

    row = lambda w: pl.BlockSpec((tm, w), lambda i: (i, 0))
    full = lambda a, b: pl.BlockSpec((a, b), lambda i: (0, 0))
    return pl.pallas_call(
        body, name=name, grid=(T // tm,),
        in_specs=[pl.BlockSpec((tm, QR), lambda i: (i, PCQ0 // QR)), pl.BlockSpec((tm, KR), lambda i: (i, PCKV0 // KR)),
                  pl.BlockSpec((tm, LANE), lambda i: (i, PKPE0 // LANE)), row(1), full(1, LANE),
                  full(1, QR), full(QR, P), full(1, KR), full(KR, 2 * P)],
        out_specs=[row(P), row(P), row(P), row(QR), row(KR)],
        out_shape=[_sds((T, P), MXU_DTYPE)] * 3 + [_sds((T, QR), MXU_DTYPE), _sds((T, KR), MXU_DTYPE)],
        compiler_params=_cparams(("parallel",)),
    )(proj, proj, proj, pos, invf, q_norm.reshape(1, QR), w_uq_p, kv_norm.reshape(1, KR), w_ukv_p)


def mla_prep_bwd(proj, pos, invf, q_norm, w_uq_p, kv_norm, w_ukv_p, dq, dk, dv, *, name, tm=256):
    T = proj.shape[0]
    tm = _pick(T, (tm, 128))
    QR, KR, P = MLA_Q_RANK, MLA_KV_RANK, MLA_PAD

    def body(cq_ref, ckv_ref, pos_ref, invf_ref, qn_ref, wq_ref, kn_ref, wkv_ref, dq_ref, dk_ref, dv_ref,
             dqp_ref, dkvp_ref, dcq_ref, dckv_ref, dkpe_ref, dqn_ref, dkn_ref):
        cosf, sin_a, sin_b, rot = _rope_tables(pos_ref[...], invf_ref[...])
        dkpe = jnp.zeros((tm, HEAD_W), F32)
        for h in range(MLA_HEADS):
            sl = slice(h * HEAD_W, (h + 1) * HEAD_W)
            dqp_ref[:, sl] = _unrope(dq_ref[:, sl] * ATTN_SCALE, cosf, sin_a, sin_b).astype(dqp_ref.dtype)
            dkh = dk_ref[:, sl]
            dkpe = dkpe + dkh
            dkvp_ref[:, sl] = dkh.astype(dkvp_ref.dtype)
            dkvp_ref[:, P + h * HEAD_W:P + (h + 1) * HEAD_W] = dv_ref[:, sl].astype(dkvp_ref.dtype)
        dkpe_ref[...] = jnp.where(rot, _unrope(dkpe, cosf, sin_a, sin_b), 0.0).astype(dkpe_ref.dtype)
        dcq, dqn = _rms_bwd_tile(cq_ref[...], qn_ref[...], _dot_nt(dqp_ref[...], wq_ref[...]))
        dckv, dkn = _rms_bwd_tile(ckv_ref[...], kn_ref[...], _dot_nt(dkvp_ref[...], wkv_ref[...]))
        dcq_ref[...] = dcq.astype(dcq_ref.dtype)
        dckv_ref[...] = dckv.astype(dckv_ref.dtype)

        @pl.when(pl.program_id(0) == 0)
        def _():
            dqn_ref[...] = dqn
            dkn_ref[...] = dkn

        @pl.when(pl.program_id(0) > 0)
        def _():
            dqn_ref[...] += dqn
            dkn_ref[...] += dkn

    row = lambda w: pl.BlockSpec((tm, w), lambda i: (i, 0))
    full = lambda a, b: pl.BlockSpec((a, b), lambda i: (0, 0))
    return pl.pallas_call(
        body, name=name, grid=(T // tm,),
        in_specs=[pl.BlockSpec((tm, QR), lambda i: (i, PCQ0 // QR)), pl.BlockSpec((tm, KR), lambda i: (i, PCKV0 // KR)),
                  row(1), full(1, LANE), full(1, QR), full(QR, P), full(1, KR), full(KR, 2 * P), row(P), row(P), row(P)],
        out_specs=[row(P), row(2 * P), row(QR), row(KR), row(LANE), full(1, QR), full(1, KR)],
        out_shape=[_sds((T, P), MXU_DTYPE), _sds((T, 2 * P), MXU_DTYPE), _sds((T, QR), MXU_DTYPE),
                   _sds((T, KR), MXU_DTYPE), _sds((T, LANE), MXU_DTYPE), _sds((1, QR), F32), _sds((1, KR), F32)],
        compiler_params=_cparams(("arbitrary",)),
    )(proj, proj, pos, invf, q_norm.reshape(1, QR), w_uq_p, kv_norm.reshape(1, KR), w_ukv_p, dq, dk, dv)


ATTN_SCALE = 1.0 / math.sqrt(MLA_QK)


def _causal_mask(i, j, blk):
    row = lax.broadcasted_iota(jnp.int32, (blk, blk), 0)
    col = lax.broadcasted_iota(jnp.int32, (blk, blk), 1)
    return col <= row + (i - j) * blk


def _hosting(hosted, grid, n_in, n_out, n_scratch):
    if hosted is None:
        return (lambda body: body), (), [], [], []
    hi, ho = len(hosted.inputs), len(hosted.out_shapes)

    def wrap(body):
        def full(*refs):
            ins, rest = refs[:n_in + hi], refs[n_in + hi:]
            outs, scr = rest[:n_out + ho], rest[n_out + ho:]
            parts = ins[n_in:], outs[n_out:], scr[n_scratch:]
            ids = [pl.program_id(d) for d in range(len(grid))]
            step = ids[0]
            for d in range(1, len(grid)):
                step = step * grid[d] + ids[d]
            total = math.prod(grid)

            @pl.when(step == 0)
            def _():
                hosted.start(*parts)

            body(*ins[:n_in], *outs[:n_out], *scr[:n_scratch])

            @pl.when(step == total // 2)
            def _():
                hosted.relay(*parts)

            @pl.when(step == total - 1)
            def _():
                hosted.finish(*parts)

        return full

    hbm = pl.BlockSpec(memory_space=pl.ANY)
    return wrap, tuple(hosted.inputs), [hbm] * ho, list(hosted.out_shapes), list(hosted.sems)


def flash_fwd(q, k, v, S, *, name, blk=512, hosted=None):
    T, P = q.shape
    blk = _pick(S, (blk, 256, 128))
    B, nq, H, W = T // S, S // blk, MLA_HEADS, HEAD_W
    grid = (B, H, nq)
    wrap, h_in, h_ospecs, h_oshapes, h_scratch = _hosting(hosted, grid, 3, 2, 0)

    def body(q_ref, k_ref, v_ref, o_ref, lse_ref):
        i = pl.program_id(2)
        qv = q_ref[...]

        def online(j, carry, masked):
            m_prev, l_prev, acc = carry
            rows = pl.ds(pl.multiple_of(j * blk, blk), blk)
            s = _dot_nt(qv, k_ref[rows, :])
            if masked:
                s = jnp.where(_causal_mask(0, 0, blk), s, -jnp.inf)
            m_new = jnp.maximum(m_prev, jnp.max(s, axis=1, keepdims=True))
            p = jnp.exp(s - m_new)
            alpha = jnp.exp(m_prev - m_new)
            return (m_new, alpha * l_prev + jnp.sum(p, axis=1, keepdims=True), alpha * acc + _dot(p, v_ref[rows, :]))

        init = (jnp.full((blk, 1), -jnp.inf, F32), jnp.zeros((blk, 1), F32), jnp.zeros((blk, W), F32))
        carry = lax.fori_loop(0, i, lambda j, c: online(j, c, False), init)
        m, l, acc = online(i, carry, True)
        o_ref[...] = acc / l
        lse_ref[...] = jnp.broadcast_to(m + jnp.log(l), (blk, W))

    qmap = lambda b, h, i: (b * nq + i, h)
    kmap = lambda b, h, i: (b, h)
    hbm = pl.BlockSpec(memory_space=pl.ANY)
    return pl.pallas_call(
        wrap(body), name=name, grid=grid,
        in_specs=[pl.BlockSpec((blk, W), qmap), pl.BlockSpec((S, W), kmap), pl.BlockSpec((S, W), kmap)] + [hbm] * len(h_in),
        out_specs=[pl.BlockSpec((blk, W), qmap), pl.BlockSpec((blk, W), qmap)] + h_ospecs,
        out_shape=[_sds((T, P), F32), _sds((T, P), F32)] + h_oshapes,
        scratch_shapes=h_scratch,
        compiler_params=_cparams(("arbitrary",) * 3 if hosted else ("parallel", "parallel", "arbitrary")),
    )(q, k, v, *h_in)


def flash_bwd(q, k, v, o, lse, dmix, S, *, name, blk=512, hosted=None):
    T, P = q.shape
    blk = _pick(S, (blk, 256, 128))
    B, nq, H, W = T // S, S // blk, MLA_HEADS, HEAD_W
    off = (SSD_WIDTH + POOL_WIDTH) // W
    grid = (B, H, nq)
    wrap, h_in, h_ospecs, h_oshapes, h_scratch = _hosting(hosted, grid, 6, 3, 1)

    def body(q_ref, k_ref, v_ref, o_ref, lse_ref, do_ref, dq_ref, dk_ref, dv_ref, delta_s):
        j = pl.program_id(2)

        @pl.when(j == 0)
        def _():
            for i in range(nq):
                rows = slice(i * blk, (i + 1) * blk)
                delta_s[rows, :] = jnp.sum(do_ref[rows, :] * o_ref[rows, :], axis=1, keepdims=True)
                dq_ref[rows, :] = jnp.zeros((blk, W), F32)

        kv, vv = k_ref[...], v_ref[...]

        def step(i, carry, masked):
            dk, dv = carry
            rows = pl.ds(pl.multiple_of(i * blk, blk), blk)
            qv, do = q_ref[rows, :], do_ref[rows, :]
            p = jnp.exp(_dot_nt(qv, kv) - lse_ref[rows, 0:1])
            if masked:
                p = jnp.where(_causal_mask(0, 0, blk), p, 0.0)
            ds = p * (_dot_nt(do, vv) - delta_s[rows, :])
            dq_ref[rows, :] += _dot(ds, kv)
            return dk + _dot_tn(ds, qv), dv + _dot_tn(p, do)

        zero = jnp.zeros((blk, W), F32)
        carry = step(j, (zero, zero), True)
        dk, dv = lax.fori_loop(j + 1, nq, lambda i, c: step(i, c, False), carry)
        dk_ref[...] = dk
        dv_ref[...] = dv

    full = lambda b, h, j: (b, h)
    kmap = lambda b, h, j: (b * nq + j, h)
    hbm = pl.BlockSpec(memory_space=pl.ANY)
    return pl.pallas_call(
        wrap(body), name=name, grid=grid,
        in_specs=[pl.BlockSpec((S, W), full), pl.BlockSpec((blk, W), kmap), pl.BlockSpec((blk, W), kmap),
                  pl.BlockSpec((S, W), full), pl.BlockSpec((S, W), full),
                  pl.BlockSpec((S, W), lambda b, h, j: (b, off + h))] + [hbm] * len(h_in),
        out_specs=[pl.BlockSpec((S, W), full), pl.BlockSpec((blk, W), kmap), pl.BlockSpec((blk, W), kmap)] + h_ospecs,
        out_shape=[_sds((T, P), F32)] * 3 + h_oshapes,
        scratch_shapes=[pltpu.VMEM((S, 1), F32)] + h_scratch,
        compiler_params=_cparams(("arbitrary",) * 3 if hosted else ("parallel", "parallel", "arbitrary")),
    )(q, k, v, o, lse, dmix, *h_in)


SSD_PAIRS = SSD_HEADS // 2
PAIRS_PER_GROUP = SSD_PAIRS // SSD_GROUPS
GN = SSD_GROUPS * SSD_STATE


def _log1p_small(e):
    return jnp.where(e < 1e-3, e * (1.0 - e * (0.5 - e / 3.0)), jnp.log(1.0 + e))


def _softplus(v):
    return jnp.maximum(v, 0.0) + _log1p_small(jnp.exp(-jnp.abs(v)))


def _ssd_decay(dt_raw, dtb, alog):
    L = dt_raw.shape[0]
    pre = dt_raw + dtb
    dt = _softplus(pre)
    a = -jnp.exp(alog)
    row = lax.broadcasted_iota(jnp.int32, (L, L), 0)
    col = lax.broadcasted_iota(jnp.int32, (L, L), 1)
    tri = row >= col
    cum = _dot_hi(tri.astype(F32), dt * a)
    return pre, dt, a, tri, cum, cum.T


def _col(m, h):
    return m[:, h:h + 1]


def _pair_sel(m, k, lo):
    return jnp.where(lo, _col(m, 2 * k), _col(m, 2 * k + 1))


def _ssd_specs(S):
    L = SSD_CHUNK
    nc = S // L
    return L, nc


def ssd_fwd(proj, xc, dtb, alog, dchan, normw, S, *, name, hosted=None):
    T = proj.shape[0]
    L, nc = _ssd_specs(S)
    B, W, N = T // S, SSD_WIDTH, SSD_STATE
    wrap, h_in, h_ospecs, h_oshapes, h_scratch = _hosting(hosted, (B, nc), 9, 3, 1)

    def body(xs_ref, bs_ref, cs_ref, dt_ref, z_ref, dtb_ref, alog_ref, dch_ref, nw_ref, y_ref, ys_ref, hin_ref, st):
        @pl.when(pl.program_id(1) == 0)
        def _():
            st[...] = jnp.zeros(st.shape, F32)

        hin_ref[...] = st[...]
        _, dt, a, tri, cum, cum_t = _ssd_decay(dt_ref[...], dtb_ref[...], alog_ref[...])
        last = cum[L - 1:L, :]
        lo = lax.broadcasted_iota(jnp.int32, (1, LANE), 1) < SSD_HEAD_DIM
        for g in range(SSD_GROUPS):
            bm = bs_ref[:, g * N:(g + 1) * N]
            cm = cs_ref[:, g * N:(g + 1) * N]
            bm_t = bm.T
            gmat = _dot_nt(cm, bm)
            for kk in range(PAIRS_PER_GROUP):
                k = g * PAIRS_PER_GROUP + kk
                sl = slice(k * LANE, (k + 1) * LANE)
                xv = xs_ref[:, sl]
                xdt = xv * _pair_sel(dt, k, lo)
                cum_cols = [jnp.broadcast_to(_col(cum, h), (L, LANE)) for h in (2 * k, 2 * k + 1)]
                cum_sel = jnp.where(lo, cum_cols[0], cum_cols[1])
                last_sel = _pair_sel(last, k, lo)
                yd = []
                for j, h in enumerate((2 * k, 2 * k + 1)):
                    gam = jnp.exp(jnp.where(tri, cum_cols[j] - cum_t[h:h + 1, :], -jnp.inf))
                    yd.append(_dot(gmat * gam, xdt))
                hp = st[:, sl]
                y_off = _dot(cm, hp) * jnp.exp(cum_sel)
                y_ref[:, sl] = jnp.where(lo, yd[0], yd[1]) + y_off + xv * dch_ref[:, sl]
                zmat = xdt * jnp.exp(last_sel - cum_sel)
                st[:, sl] = hp * jnp.exp(last_sel) + _dot(bm_t, zmat)
        y = y_ref[...]
        z = z_ref[...]
        yz = y * (z * _sigmoid(z))
        ys_ref[...] = _rms_tile(yz, nw_ref[...]).astype(ys_ref.dtype)

    r = lambda b, c: b * nc + c
    vec = lambda w: pl.BlockSpec((1, w), lambda b, c: (0, 0))
    hbm = pl.BlockSpec(memory_space=pl.ANY)
    return pl.pallas_call(
        wrap(body), name=name, grid=(B, nc),
        in_specs=[pl.BlockSpec((L, W), lambda b, c: (r(b, c), 0)),
                  pl.BlockSpec((L, GN), lambda b, c: (r(b, c), W // GN)),
                  pl.BlockSpec((L, GN), lambda b, c: (r(b, c), W // GN + 1)),
                  pl.BlockSpec((L, LANE), lambda b, c: (r(b, c), PDT0 // LANE)),
                  pl.BlockSpec((L, W), lambda b, c: (r(b, c), PZ0 // W)),
                  vec(LANE), vec(LANE), vec(W), vec(W)] + [hbm] * len(h_in),
        out_specs=[pl.BlockSpec((L, W), lambda b, c: (r(b, c), 0)), pl.BlockSpec((L, W), lambda b, c: (r(b, c), 0)),
                   pl.BlockSpec((N, W), lambda b, c: (r(b, c), 0))] + h_ospecs,
        out_shape=[_sds((T, W), F32), _sds((T, W), MXU_DTYPE), _sds((T // L * N, W), F32)] + h_oshapes,
        scratch_shapes=[pltpu.VMEM((N, W), F32)] + h_scratch,
        compiler_params=_cparams(("arbitrary", "arbitrary") if hosted else ("parallel", "arbitrary")),
    )(xc, xc, xc, proj, proj, dtb, alog, dchan, normw, *h_in)


def ssd_bwd(proj, xc, ypre, hin, dmix, dtb, alog, dchan, normw, S, *, name):
    T = proj.shape[0]
    L, nc = _ssd_specs(S)
    B, W, N = T // S, SSD_WIDTH, SSD_STATE

    def body(xs_ref, bs_ref, cs_ref, dt_ref, z_ref, y_ref, hin_ref, dys_ref, dtb_ref, alog_ref, dch_ref, nw_ref,
             dxc_ref, ddt_ref, dz_ref, sm_ref, dnw_ref, dst):
        step = pl.program_id(0) * nc + pl.program_id(1)

        @pl.when(pl.program_id(1) == 0)
        def _():
            dst[...] = jnp.zeros(dst.shape, F32)

        pre, dt, a, tri, cum, cum_t = _ssd_decay(dt_ref[...], dtb_ref[...], alog_ref[...])
        last = cum[L - 1:L, :]
        e_last = jnp.exp(last)
        lane = lax.broadcasted_iota(jnp.int32, (1, LANE), 1)
        sub = lax.broadcasted_iota(jnp.int32, (LANE, 1), 0)
        lo = lane < SSD_HEAD_DIM
        is_last_row = sub == L - 1
        tri_t = (lax.broadcasted_iota(jnp.int32, (L, L), 0) <= lax.broadcasted_iota(jnp.int32, (L, L), 1))

        y, z, nw = y_ref[...], z_ref[...], nw_ref[...]
        sg = _sigmoid(z)
        gate = z * sg
        dyz, dnw = _rms_bwd_tile(y * gate, nw, dys_ref[...])
        dy_all = dyz * gate
        dz_ref[...] = (dyz * y * (sg * (1.0 + z * (1.0 - sg)))).astype(dz_ref.dtype)

        d_cum = jnp.zeros((L, LANE), F32)
        d_cum_t = jnp.zeros((LANE, L), F32)
        d_dt = jnp.zeros((L, LANE), F32)
        d_dskip = jnp.zeros((1, LANE), F32)
        for g in range(SSD_GROUPS):
            bm = bs_ref[:, g * N:(g + 1) * N]
            cm = cs_ref[:, g * N:(g + 1) * N]
            cm_t = cm.T
            gmat = _dot_nt(cm, bm)
            gmat_t = _dot_nt(bm, cm)
            d_g = jnp.zeros((L, L), F32)
            d_bm = jnp.zeros((L, N), F32)
            d_cm = jnp.zeros((L, N), F32)
            for kk in range(PAIRS_PER_GROUP):
                k = g * PAIRS_PER_GROUP + kk
                sl = slice(k * LANE, (k + 1) * LANE)
                xv = xs_ref[:, sl]
                dyv = dy_all[:, sl]
                dt_sel = _pair_sel(dt, k, lo)
                xdt = xv * dt_sel
                hp = hin_ref[:, sl]
                dh_out = dst[:, sl]
                cum_cols = [jnp.broadcast_to(_col(cum, h), (L, LANE)) for h in (2 * k, 2 * k + 1)]
                cum_sel = jnp.where(lo, cum_cols[0], cum_cols[1])
                last_sel = _pair_sel(last, k, lo)
                e_sel = jnp.exp(cum_sel)
                w_sel = jnp.exp(last_sel - cum_sel)
                e_lane = jnp.exp(last_sel)
                y_off = _dot(cm, hp) * e_sel
                zmat = xdt * w_sel
                d_z = _dot(bm, dh_out)
                d_bm = d_bm + _dot_nt(zmat, dh_out)
                d_xdt = d_z * w_sel
                dw_full = d_z * zmat
                hh = dh_out * hp
                d_r = dyv * e_sel
                d_cm = d_cm + _dot_nt(d_r, hp)
                dst[:, sl] = dh_out * e_lane + _dot(cm_t, d_r)
                dyoff_full = dyv * y_off
                for j, h in enumerate((2 * k, 2 * k + 1)):
                    mine = lo if j == 0 else jnp.logical_not(lo)
                    hot = lane == h
                    dyh = jnp.where(mine, dyv, 0.0)
                    gam = jnp.exp(jnp.where(tri, cum_cols[j] - cum_t[h:h + 1, :], -jnp.inf))
                    gam_t = jnp.exp(jnp.where(tri_t, cum_t[h:h + 1, :] - cum_cols[j], -jnp.inf))
                    mx = gmat * gam
                    d_xdt = d_xdt + _dot(gmat_t * gam_t, dyh)
                    d_mx = jnp.where(tri, _dot_nt(dyh, xdt), 0.0)
                    d_g = d_g + d_mx * gam
                    d_seg = d_mx * mx
                    row_l = (jnp.sum(d_seg, axis=1, keepdims=True)
                             + jnp.sum(jnp.where(mine, dyoff_full - dw_full, 0.0), axis=1, keepdims=True))
                    at_end = (jnp.sum(jnp.where(mine, dw_full, 0.0), keepdims=True)
                              + jnp.sum(jnp.where(mine, hh, 0.0), keepdims=True) * _col(e_last, h))
                    d_cum = d_cum + jnp.where(hot, row_l + jnp.where(is_last_row, at_end, 0.0), 0.0)
                    d_cum_t = d_cum_t - jnp.where(sub == h, jnp.sum(d_seg, axis=0, keepdims=True), 0.0)
                    d_dskip = d_dskip + jnp.where(hot, jnp.sum(jnp.where(mine, dyv * xv, 0.0), keepdims=True), 0.0)
                for j, h in enumerate((2 * k, 2 * k + 1)):
                    mine = lo if j == 0 else jnp.logical_not(lo)
                    d_dt = d_dt + jnp.where(lane == h, jnp.sum(jnp.where(mine, d_xdt * xv, 0.0), axis=1, keepdims=True), 0.0)
                dxc_ref[:, sl] = d_xdt * dt_sel + dyv * dch_ref[:, sl]
            dxc_ref[:, W + g * N:W + (g + 1) * N] = d_bm + _dot_tn(d_g, cm)
            dxc_ref[:, W + GN + g * N:W + GN + (g + 1) * N] = d_cm + _dot(d_g, bm)

        d_cum = d_cum + d_cum_t.T
        d_da = _dot_hi(jnp.logical_not(tri).astype(F32) + (lax.broadcasted_iota(jnp.int32, (L, L), 0)
                                                              == lax.broadcasted_iota(jnp.int32, (L, L), 1)).astype(F32), d_cum)
        d_dt = d_dt + d_da * a
        heads = lane < SSD_HEADS
        d_pre = jnp.where(heads, d_dt * _sigmoid(pre), 0.0)
        ddt_ref[...] = d_pre.astype(ddt_ref.dtype)
        d_alog = jnp.sum(d_da * dt, axis=0, keepdims=True) * a
        part = jnp.concatenate([jnp.where(heads, d_alog, 0.0), jnp.sum(d_pre, axis=0, keepdims=True), d_dskip,
                                jnp.zeros((5, LANE), F32)], axis=0)

        @pl.when(step == 0)
        def _():
            sm_ref[...] = part
            dnw_ref[...] = dnw

        @pl.when(step > 0)
        def _():
            sm_ref[...] += part
            dnw_ref[...] += dnw

    r = lambda b, c: b * nc + (nc - 1 - c)
    vec = lambda w: pl.BlockSpec((1, w), lambda b, c: (0, 0))
    blk = lambda w, j: pl.BlockSpec((L, w), lambda b, c: (r(b, c), j))
    return pl.pallas_call(
        body, name=name, grid=(B, nc),
        in_specs=[blk(W, 0), blk(GN, W // GN), blk(GN, W // GN + 1), blk(LANE, PDT0 // LANE), blk(W, PZ0 // W),
                  blk(W, 0), pl.BlockSpec((N, W), lambda b, c: (r(b, c), 0)), blk(W, 0),
                  vec(LANE), vec(LANE), vec(W), vec(W)],
        out_specs=[blk(SSD_CONV_CH, 0), blk(LANE, 0), blk(W, 0), pl.BlockSpec((8, LANE), lambda b, c: (0, 0)), vec(W)],
        out_shape=[_sds((T, SSD_CONV_CH), F32), _sds((T, LANE), MXU_DTYPE), _sds((T, W), MXU_DTYPE),
                   _sds((8, LANE), F32), _sds((1, W), F32)],
        scratch_shapes=[pltpu.VMEM((N, W), F32)],
        compiler_params=_cparams(("arbitrary", "arbitrary")),
    )(xc, xc, xc, proj, proj, ypre, hin, dmix, dtb, alog, dchan, normw)


def _adamw_math(w, g, m, v):
    m = ADAM_B1 * m + (1.0 - ADAM_B1) * g
    v = ADAM_B2 * v + (1.0 - ADAM_B2) * (g * g)
    m_hat = m / (1.0 - ADAM_B1 ** ADAM_STEP)
    v_hat = v / (1.0 - ADAM_B2 ** ADAM_STEP)
    delta = -ADAM_LR * (m_hat / (jnp.sqrt(v_hat) + ADAM_EPS) + ADAM_WD * w)
    return delta, m, v


def adamw_layers(w, g_layers, m, v, *, name, tr=256):
    L, A, B = w.shape
    tr = _pick(A, (tr, 192, 176, 128, 64, 32, 16, 8))
    na = A // tr
    n = len(g_layers[0])

    def body(*refs):
        w_ref, m_ref, v_ref = refs[0], refs[1 + L * n], refs[2 + L * n]
        g_ref, d_ref, nm_ref, nv_ref = refs[3 + L * n:]
        layer = pl.program_id(0)
        g = None
        for l in range(L):
            parts = refs[1 + l * n:1 + (l + 1) * n]
            gl = parts[0][...]
            for p in parts[1:]:
                gl = gl + p[...]
            g = gl if g is None else jnp.where(layer == l, gl, g)
        g_ref[...] = g
        d_ref[...], nm_ref[...], nv_ref[...] = _adamw_math(w_ref[...], g, m_ref[...], v_ref[...])

    def g_spec(l):
        return pl.BlockSpec((tr, B), lambda layer, i: (jnp.where(layer == l, i, jnp.where(layer < l, 0, na - 1)), 0))

    spec = pl.BlockSpec((None, tr, B), lambda layer, i: (layer, i, 0))
    return pl.pallas_call(
        body, name=name, grid=(L, na), in_specs=[spec] + [g_spec(l) for l in range(L) for _ in range(n)] + [spec] * 2,
        out_specs=[spec] * 4, out_shape=[_sds((L, A, B), F32)] * 4, compiler_params=_cparams(("arbitrary", "arbitrary")),
    )(w, *[p for parts in g_layers for p in parts], m, v)


def adamw_small(ws, gs, ms, vs, *, name):
    n = len(ws)

    def body(*refs):
        w_refs, g_refs, m_refs, v_refs = (refs[i * n:(i + 1) * n] for i in range(4))
        d_refs, nm_refs, nv_refs = (refs[(4 + i) * n:(5 + i) * n] for i in range(3))
        for a in range(n):
            d_refs[a][...], nm_refs[a][...], nv_refs[a][...] = _adamw_math(
                w_refs[a][...], g_refs[a][...], m_refs[a][...], v_refs[a][...])

    vm = pl.BlockSpec(memory_space=pltpu.VMEM)
    out = pl.pallas_call(
        body, name=name, in_specs=[vm] * (4 * n), out_specs=[vm] * (3 * n),
        out_shape=[_sds(w.shape, F32) for w in ws] * 3, compiler_params=pltpu.CompilerParams(vmem_limit_bytes=VMEM_LIMIT),
    )(*ws, *gs, *ms, *vs)
    return out[:n], out[n:2 * n], out[2 * n:]


def _my_place():
    return lax.axis_index("x"), lax.axis_index("y"), lax.axis_index("c")


def _other_chips(x, y):
    return [(1 - x, y), (x, 1 - y), (1 - x, 1 - y)]


def relation_of(chip, me):
    d = chip ^ me
    return jnp.where(d == 2, 0, jnp.where(d == 1, 1, jnp.where(d == 3, 2, -1)))


class Exchange(NamedTuple):
    inputs: tuple
    out_shapes: tuple
    sems: tuple
    start: Callable
    relay: Callable
    finish: Callable


def scatter_exchange(srcs):
    n = len(srcs)

    def copies(ins, outs, sems):
        x, y, c = _my_place()
        out = []
        for k, (px, py) in enumerate(_other_chips(x, y)):
            for a in range(n):
                out.append(pltpu.make_async_remote_copy(
                    src_ref=ins[a].at[2 * px + py], dst_ref=outs[a].at[k], send_sem=sems[0].at[k, a],
                    recv_sem=sems[1].at[k, a], device_id=(px, py, c), device_id_type=pl.DeviceIdType.MESH))
        return out

    def start(ins, outs, sems):
        for cp in copies(ins, outs, sems):
            cp.start()

    def finish(ins, outs, sems):
        cps = copies(ins, outs, sems)
        for cp in cps:
            cp.wait_recv()
        for cp in cps:
            cp.wait_send()

    return Exchange(tuple(srcs), tuple(_sds((3,) + s.shape[1:], s.dtype) for s in srcs),
                    (pltpu.SemaphoreType.DMA((3, n)),) * 2, start, lambda *a: None, finish)


def run_exchange(ex, *, name):
    n_in, n_out = len(ex.inputs), len(ex.out_shapes)

    def body(*refs):
        parts = refs[:n_in], refs[n_in:n_in + n_out], refs[n_in + n_out:]
        ex.start(*parts)
        ex.relay(*parts)
        ex.finish(*parts)

    hbm = pl.BlockSpec(memory_space=pl.ANY)
    return pl.pallas_call(
        body, name=name, in_specs=[hbm] * n_in, out_specs=[hbm] * n_out, out_shape=list(ex.out_shapes),
        scratch_shapes=list(ex.sems), compiler_params=pltpu.CompilerParams(has_side_effects=True),
    )(*ex.inputs)


def sibling_swap(srcs, *, name):
    n = len(srcs)

    def body(*refs):
        src_refs, out_refs, (send_sems, recv_sems) = refs[:n], refs[n:2 * n], refs[2 * n:]
        x, y, c = _my_place()
        copies = [pltpu.make_async_remote_copy(
            src_ref=src_refs[a], dst_ref=out_refs[a], send_sem=send_sems.at[a], recv_sem=recv_sems.at[a],
            device_id=(x, y, 1 - c), device_id_type=pl.DeviceIdType.MESH) for a in range(n)]
        for cp in copies:
            cp.start()
        for cp in copies:
            cp.wait_recv()
        for cp in copies:
            cp.wait_send()

    hbm = pl.BlockSpec(memory_space=pl.ANY)
    return pl.pallas_call(
        body, name=name, in_specs=[hbm] * n, out_specs=[hbm] * n, out_shape=[_sds(s.shape, s.dtype) for s in srcs],
        scratch_shapes=[pltpu.SemaphoreType.DMA((n,)), pltpu.SemaphoreType.DMA((n,))],
        compiler_params=pltpu.CompilerParams(has_side_effects=True),
    )(*srcs)


def gather_exchange(srcs):
    nch = len(srcs)
    halves = [s.shape[0] // 2 for s in srcs]
    assert all(2 * h == s.shape[0] and h % 16 == 0 for h, s in zip(halves, srcs))
    pieces = [(k, q) for k in range(3) for q in range(nch)]

    def makers(ins, outs, sems):
        ici_send, ici_recv, d2d_send, d2d_recv = sems
        x, y, c = _my_place()
        peers = _other_chips(x, y)

        def rows(core, q):
            return pl.ds(core * halves[q], halves[q])

        def ici(k, q):
            px, py = peers[k]
            return pltpu.make_async_remote_copy(
                src_ref=ins[q].at[rows(c, q)], dst_ref=outs[q].at[k, rows(c, q)], send_sem=ici_send.at[k, q],
                recv_sem=ici_recv.at[k, q], device_id=(px, py, c), device_id_type=pl.DeviceIdType.MESH)

        def d2d(k, q, core):
            return pltpu.make_async_remote_copy(
                src_ref=outs[q].at[k, rows(core, q)], dst_ref=outs[q].at[k, rows(core, q)],
                send_sem=d2d_send.at[k, q], recv_sem=d2d_recv.at[k, q], device_id=(x, y, 1 - c),
                device_id_type=pl.DeviceIdType.MESH)

        return ici, d2d, c

    def start(*refs):
        ici, _, _ = makers(*refs)
        for k, q in pieces:
            ici(k, q).start()

    def relay(*refs):
        ici, d2d, c = makers(*refs)
        for k, q in pieces:
            ici(k, q).wait_recv()
            d2d(k, q, c).start()

    def finish(*refs):
        ici, d2d, c = makers(*refs)
        for k, q in pieces:
            d2d(k, q, 1 - c).wait_recv()
        for k, q in pieces:
            ici(k, q).wait_send()
            d2d(k, q, c).wait_send()

    return Exchange(tuple(srcs), tuple(_sds((3,) + s.shape, s.dtype) for s in srcs),
                    (pltpu.SemaphoreType.DMA((3, nch)),) * 4, start, relay, finish)


def all_sum_small(vec, *, name):
    R, C = vec.shape

    def body(v_ref, out_ref, buf, send_sems, recv_sems):
        x, y, c = _my_place()
        me = 4 * x + 2 * y + c
        buf[me] = v_ref[...]
        copies = []
        for k in range(1, N_DEV):
            px, py, pc = x ^ (k >> 2), y ^ ((k >> 1) & 1), c ^ (k & 1)
            copies.append(pltpu.make_async_remote_copy(
                src_ref=v_ref, dst_ref=buf.at[me], send_sem=send_sems.at[k - 1], recv_sem=recv_sems.at[k - 1],
                device_id=(px, py, pc), device_id_type=pl.DeviceIdType.MESH))
        for cp in copies:
            cp.start()
        for k in range(1, N_DEV):
            px, py, pc = x ^ (k >> 2), y ^ ((k >> 1) & 1), c ^ (k & 1)
            pltpu.make_async_remote_copy(
                src_ref=v_ref, dst_ref=buf.at[4 * px + 2 * py + pc], send_sem=send_sems.at[k - 1],
                recv_sem=recv_sems.at[k - 1], device_id=(px, py, pc), device_id_type=pl.DeviceIdType.MESH).wait_recv()
        for cp in copies:
            cp.wait_send()
        acc = buf[0]
        for d in range(1, N_DEV):
            acc = acc + buf[d]
        out_ref[...] = acc

    return pl.pallas_call(
        body, name=name, in_specs=[pl.BlockSpec(memory_space=pltpu.VMEM)], out_specs=pl.BlockSpec(memory_space=pltpu.VMEM),
        out_shape=_sds((R, C), F32),
        scratch_shapes=[pltpu.VMEM((N_DEV, R, C), F32), pltpu.SemaphoreType.DMA((N_DEV - 1,)),
                        pltpu.SemaphoreType.DMA((N_DEV - 1,))],
        compiler_params=pltpu.CompilerParams(has_side_effects=True, vmem_limit_bytes=VMEM_LIMIT),
    )(vec)


def sum_chips(own, others, *, name, tr=512):
    R, C = own.shape
    tr = _pick(R, (tr, 384, 352, 256, 128, 64, 32, 16))

    def body(o_ref, p_ref, s_ref):
        acc = o_ref[...].astype(F32)
        for k in range(3):
            acc = acc + p_ref[k].astype(F32)
        s_ref[...] = acc

    return pl.pallas_call(
        body, name=name, grid=(R // tr,),
        in_specs=[pl.BlockSpec((tr, C), lambda i: (i, 0)), pl.BlockSpec((3, tr, C), lambda i: (0, i, 0))],
        out_specs=pl.BlockSpec((tr, C), lambda i: (i, 0)), out_shape=_sds((R, C), F32),
        compiler_params=_cparams(("parallel",)),
    )(own, others)


WEIGHTS = ['attn_norm', 'w_in', 'ssd_conv_w', 'ssd_conv_b', 'ssd_dt_bias', 'ssd_a_log', 'ssd_d', 'ssd_norm', 'pool_w',
           'pool_scale', 'mla_q_norm', 'mla_w_uq', 'mla_kv_norm', 'mla_w_ukv', 'w_out', 'ffn_norm', 'ffn_w_up',
           'ffn_conv_w', 'ffn_conv_b', 'ffn_w_down', 'final_norm']
BIG = {'w_in': 2, 'mla_w_uq': 2, 'mla_w_ukv': 2, 'w_out': 1, 'ffn_w_up': 2, 'ffn_w_down': 1}
CONV_SHARDED = ('ssd_conv_w', 'ffn_conv_w')


def _zeros_cols(w, n):
    return jnp.zeros((w.shape[0], n), w.dtype)


def _w_in_to_padded(w):
    return jnp.concatenate([w[:, 0:2560], w[:, 2576:3088], w[:, 3088:3472], w[:, 2560:2576], _zeros_cols(w, 112),
                            w[:, 3472:3728], _zeros_cols(w, 64), w[:, 3728:3760], _zeros_cols(w, 32 + 128)], axis=1)


def _w_in_from_padded(g):
    return jnp.concatenate([g[:, 0:2560], g[:, PDT0:PDT0 + SSD_HEADS], g[:, PU0:PU0 + POOL_WIDTH],
                            g[:, PCQ0:PCQ0 + MLA_Q_RANK], g[:, PCKV0:PCKV0 + MLA_KV_RANK],
                            g[:, PKPE0 + ROPE0:PKPE0 + ROPE0 + MLA_ROPE]], axis=1)


def _w_uq_to_padded(w):
    r = w.reshape(MLA_Q_RANK, MLA_HEADS, MLA_QK)
    return jnp.pad(r, ((0, 0), (0, 0), (0, HEAD_W - MLA_QK))).reshape(MLA_Q_RANK, MLA_PAD)


def _w_uq_from_padded(g):
    return g.reshape(MLA_Q_RANK, MLA_HEADS, HEAD_W)[:, :, :MLA_QK].reshape(MLA_Q_RANK, MLA_HEADS * MLA_QK)


def _w_ukv_to_padded(w):
    r = w.reshape(MLA_KV_RANK, MLA_HEADS, MLA_NOPE + MLA_V)
    pad = lambda t: jnp.pad(t, ((0, 0), (0, 0), (0, HEAD_W - t.shape[2]))).reshape(MLA_KV_RANK, MLA_PAD)
    return jnp.concatenate([pad(r[:, :, :MLA_NOPE]), pad(r[:, :, MLA_NOPE:])], axis=1)


def _w_ukv_from_padded(g):
    kk = g[:, :MLA_PAD].reshape(MLA_KV_RANK, MLA_HEADS, HEAD_W)[:, :, :MLA_NOPE]
    vv = g[:, MLA_PAD:].reshape(MLA_KV_RANK, MLA_HEADS, HEAD_W)[:, :, :MLA_V]
    return jnp.concatenate([kk, vv], axis=2).reshape(MLA_KV_RANK, MLA_HEADS * (MLA_NOPE + MLA_V))


def _w_out_to_padded(w):
    att = w[SSD_WIDTH + POOL_WIDTH:].reshape(MLA_HEADS, MLA_V, D_MODEL)
    att = jnp.pad(att, ((0, 0), (0, HEAD_W - MLA_V), (0, 0))).reshape(MLA_PAD, D_MODEL)
    return jnp.concatenate([w[:SSD_WIDTH + POOL_WIDTH], att], axis=0)


def _w_out_from_padded(g):
    att = g[SSD_WIDTH + POOL_WIDTH:].reshape(MLA_HEADS, HEAD_W, D_MODEL)[:, :MLA_V].reshape(MLA_WIDTH, D_MODEL)
    return jnp.concatenate([g[:SSD_WIDTH + POOL_WIDTH], att], axis=0)


def _pad_lanes(v, n=LANE):
    return jnp.pad(v.reshape(1, -1), ((0, 0), (0, n - v.size)))


def _pack_rows(parts, cols, dtype, row_multiple=16):
    flat = jnp.concatenate([p.astype(dtype).reshape(-1) for p in parts])
    rows = -(-flat.size // (cols * row_multiple)) * row_multiple
    return jnp.pad(flat, (0, rows * cols - flat.size)).reshape(rows, cols)


def _unpack_rows(packed, shapes):
    flat = packed.reshape(-1)
    out, at = [], 0
    for s in shapes:
        n = math.prod(s)
        out.append(flat[at:at + n].reshape(s))
        at += n
    return out


def _split_for_chips(g, axis):
    a, b = g.shape
    if axis == 0:
        return g.reshape(N_CHIPS, a // N_CHIPS, b)
    return g.reshape(a, N_CHIPS, b // N_CHIPS).transpose(1, 0, 2)


_MATMUL_OPERANDS = {'w_in': ('w_in_p', _w_in_to_padded), 'mla_w_uq': ('w_uq_p', _w_uq_to_padded),
                    'mla_w_ukv': ('w_ukv_p', _w_ukv_to_padded), 'w_out': ('w_out_p', _w_out_to_padded),
                    'ffn_w_up': ('w_up', lambda a: a), 'ffn_w_down': ('w_down', lambda a: a)}


def _matmul_weights(full):
    return {_MATMUL_OPERANDS[k][0]: _MATMUL_OPERANDS[k][1](a) for k, a in full.items()}


def _layer_weights(full, small, l):
    w = _matmul_weights(full)
    for k in ('attn_norm', 'ssd_conv_w', 'ssd_conv_b', 'ssd_norm', 'pool_w', 'pool_scale', 'mla_q_norm', 'mla_kv_norm',
              'ffn_norm', 'ffn_conv_w', 'ffn_conv_b'):
        w[k] = small[k][l]
    w['dtb'] = _pad_lanes(small['ssd_dt_bias'][l])
    w['alog'] = _pad_lanes(small['ssd_a_log'][l])
    w['dchan'] = jnp.repeat(small['ssd_d'][l], SSD_HEAD_DIM).reshape(1, SSD_WIDTH)
    w['ssd_norm'] = w['ssd_norm'].reshape(1, SSD_WIDTH)
    return w


def _layer_fwd(x, pos, invf, w, S, l, hosted=None, scan_hosted=None, late_weights=None):
    n = lambda s: f"{s}_l{l}"
    h1 = rmsnorm_fwd(x, w['attn_norm'], name=n("attn_norm"))
    proj = matmul(h1, w['w_in_p'], name=n("w_in"))
    xc = ssd_conv_fwd(proj, w['ssd_conv_w'], w['ssd_conv_b'], S, name=n("ssd_conv"))
    ypre, yssd, hin, *arrived = ssd_fwd(proj, xc, w['dtb'], w['alog'], w['dchan'], w['ssd_norm'], S, name=n("ssd_scan"),
                                        hosted=scan_hosted)
    if late_weights:
        w = {**w, **late_weights(arrived)}
    ypool, pooled = pool_fwd(proj, w['pool_w'], w['pool_scale'], S, name=n("pool"))
    q, k, v, cqn, ckvn = mla_prep_fwd(proj, pos, invf, w['mla_q_norm'], w['w_uq_p'], w['mla_kv_norm'], w['w_ukv_p'],
                                      name=n("mla_prep"))
    o, lse, *exchanged = flash_fwd(q, k, v, S, name=n("attention"), hosted=hosted)
    mix = jnp.concatenate([yssd, ypool, o.astype(MXU_DTYPE)], axis=1)
    x2 = matmul(mix, w['w_out_p'], res=x, name=n("w_out"))
    h2 = rmsnorm_fwd(x2, w['ffn_norm'], name=n("ffn_norm"))
    up = matmul(h2, w['w_up'], name=n("ffn_up"))
    act = ffn_conv_gate_fwd(up, w['ffn_conv_w'], w['ffn_conv_b'], S, name=n("ffn_conv_gate"))
    x3 = matmul(act, w['w_down'], res=x2, name=n("ffn_down"))
    saved = dict(x=x, h1=h1, proj=proj, xc=xc, ypre=ypre, hin=hin, pooled=pooled, q=q, k=k, v=v, cqn=cqn, ckvn=ckvn,
                 o=o, lse=lse, mix=mix, x2=x2, h2=h2, up=up, act=act)
    return x3, saved, w, exchanged


def _layer_bwd(dx3, pos, invf, w, s, S, l, host=None, late_host=None):
    n = lambda t: f"{t}_l{l}"
    g = {}
    dact = matmul(dx3, w['w_down'], nt=True, name=n("d_ffn_down"))
    g['ffn_w_down'] = matmul_tn(s['act'], dx3, name=n("g_ffn_down"))
    dup_g, dup_v, st = ffn_conv_gate_bwd(s['up'], w['ffn_conv_w'], w['ffn_conv_b'], dact, S, name=n("d_ffn_conv_gate"))
    g['ffn_conv_w'], g['ffn_conv_b'] = st[:FFN_CONV], st[FFN_CONV]
    dh2 = matmul(dup_g, w['w_up'], nt=True, kblock=0, name=n("d_ffn_up_g"))
    dh2 = matmul(dup_v, w['w_up'], nt=True, kblock=1, res=dh2, name=n("d_ffn_up_v"))
    g['ffn_w_up'] = jnp.concatenate([matmul_tn(s['h2'], dup_g, name=n("g_ffn_up_g")),
                                     matmul_tn(s['h2'], dup_v, name=n("g_ffn_up_v"))], axis=1)
    dx2, gn = rmsnorm_bwd(s['x2'], w['ffn_norm'], dh2, dx3, name=n("d_ffn_norm"))
    g['ffn_norm'] = gn[0]
    dmix = matmul(dx2, w['w_out_p'], nt=True, name=n("d_w_out"))
    g['w_out'] = _w_out_from_padded(matmul_tn(s['mix'], dx2, name=n("g_w_out")))
    dxc, ddt, dz, sm, gsn = ssd_bwd(s['proj'], s['xc'], s['ypre'], s['hin'], dmix, w['dtb'], w['alog'], w['dchan'],
                                    w['ssd_norm'], S, name=n("d_ssd_scan"))
    g['ssd_a_log'], g['ssd_dt_bias'], g['ssd_d'] = sm[0, :SSD_HEADS], sm[1, :SSD_HEADS], sm[2, :SSD_HEADS]
    g['ssd_norm'] = gsn[0]
    dxbc, st = ssd_conv_bwd(s['proj'], w['ssd_conv_w'], w['ssd_conv_b'], dxc, S, name=n("d_ssd_conv"))
    g['ssd_conv_w'], g['ssd_conv_b'] = st[:SSD_CONV], st[SSD_CONV]
    du, g['pool_w'], gps = pool_bwd(dmix, s['pooled'], w['pool_w'], w['pool_scale'], S, name=n("d_pool"))
    g['pool_scale'] = gps[0]
    dq, dk, dv, *exchanged = flash_bwd(s['q'], s['k'], s['v'], s['o'], s['lse'], dmix, S, name=n("d_attention"),
                                       hosted=host(g) if host else None)
    dqp, dkvp, dcq, dckv, dkpe, gqn, gkn = mla_prep_bwd(s['proj'], pos, invf, w['mla_q_norm'], w['w_uq_p'],
                                                        w['mla_kv_norm'], w['w_ukv_p'], dq, dk, dv, name=n("d_mla_prep"))
    g['mla_q_norm'], g['mla_kv_norm'] = gqn[0], gkn[0]
    g['mla_w_uq'] = _w_uq_from_padded(matmul_tn(s['cqn'], dqp, name=n("g_w_uq")))
    g['mla_w_ukv'] = _w_ukv_from_padded(matmul_tn(s['ckvn'], dkvp, name=n("g_w_ukv")))
    dproj = jnp.concatenate([dz, dxbc, du, dcq, ddt, dckv, dkpe, jnp.zeros_like(dkpe)], axis=1)
    g['w_in'] = _w_in_from_padded(matmul_tn(s['h1'], dproj, name=n("g_w_in")))
    dh1 = matmul(dproj, w['w_in_p'], nt=True, name=n("d_w_in"), hosted=late_host(g) if late_host else None)
    dh1, late_exchanged = (dh1[0], dh1[1:]) if late_host else (dh1, [])
    dx, gn = rmsnorm_bwd(s['x'], w['attn_norm'], dh1, dx2, name=n("d_attn_norm"))
    g['attn_norm'] = gn[0]
    return dx, g, exchanged, late_exchanged


def _rope_inputs(positions):
    pos = positions.reshape(-1, 1).astype(F32)
    inv_freq = ROPE_THETA ** (-jnp.arange(0, MLA_ROPE, 2, dtype=F32) / MLA_ROPE)
    invf = jnp.concatenate([jnp.zeros((ROPE0,), F32), inv_freq, inv_freq,
                            jnp.zeros((HEAD_W - ROPE0 - MLA_ROPE,), F32)]).reshape(1, HEAD_W)
    return pos, invf


EARLY_GRADS = ('w_out', 'ffn_w_up', 'ffn_w_down')


def kernel(x, positions, attn_norm, w_in, ssd_conv_w, ssd_conv_b, ssd_dt_bias, ssd_a_log, ssd_d, ssd_norm, pool_w, pool_scale, mla_q_norm, mla_w_uq, mla_kv_norm, mla_w_ukv, w_out, ffn_norm, ffn_w_up, ffn_conv_w, ffn_conv_b, ffn_w_down, final_norm, loss_target, m_attn_norm, m_w_in, m_ssd_conv_w, m_ssd_conv_b, m_ssd_dt_bias, m_ssd_a_log, m_ssd_d, m_ssd_norm, m_pool_w, m_pool_scale, m_mla_q_norm, m_mla_w_uq, m_mla_kv_norm, m_mla_w_ukv, m_w_out, m_ffn_norm, m_ffn_w_up, m_ffn_conv_w, m_ffn_conv_b, m_ffn_w_down, m_final_norm, v_attn_norm, v_w_in, v_ssd_conv_w, v_ssd_conv_b, v_ssd_dt_bias, v_ssd_a_log, v_ssd_d, v_ssd_norm, v_pool_w, v_pool_scale, v_mla_q_norm, v_mla_w_uq, v_mla_kv_norm, v_mla_w_ukv, v_w_out, v_ffn_norm, v_ffn_w_up, v_ffn_conv_w, v_ffn_conv_b, v_ffn_w_down, v_final_norm):
    wv = dict(zip(WEIGHTS, (attn_norm, w_in, ssd_conv_w, ssd_conv_b, ssd_dt_bias, ssd_a_log, ssd_d, ssd_norm, pool_w,
                            pool_scale, mla_q_norm, mla_w_uq, mla_kv_norm, mla_w_ukv, w_out, ffn_norm, ffn_w_up,
                            ffn_conv_w, ffn_conv_b, ffn_w_down, final_norm)))
    mv = dict(zip(WEIGHTS, (m_attn_norm, m_w_in, m_ssd_conv_w, m_ssd_conv_b, m_ssd_dt_bias, m_ssd_a_log, m_ssd_d,
                            m_ssd_norm, m_pool_w, m_pool_scale, m_mla_q_norm, m_mla_w_uq, m_mla_kv_norm, m_mla_w_ukv,
                            m_w_out, m_ffn_norm, m_ffn_w_up, m_ffn_conv_w, m_ffn_conv_b, m_ffn_w_down, m_final_norm)))
    vv = dict(zip(WEIGHTS, (v_attn_norm, v_w_in, v_ssd_conv_w, v_ssd_conv_b, v_ssd_dt_bias, v_ssd_a_log, v_ssd_d,
                            v_ssd_norm, v_pool_w, v_pool_scale, v_mla_q_norm, v_mla_w_uq, v_mla_kv_norm, v_mla_w_ukv,
                            v_w_out, v_ffn_norm, v_ffn_w_up, v_ffn_conv_w, v_ffn_conv_b, v_ffn_w_down, v_final_norm)))
    Bl, S, D = x.shape
    chip = 2 * lax.axis_index("x") + lax.axis_index("y")
    core = lax.axis_index("c")

    big_names = list(BIG)
    first_names = ['w_in']
    rest_names = [k for k in big_names if k not in first_names]

    def shards(l, names):
        return [wv[k][l].astype(MXU_DTYPE) for k in names]

    def whole_weights(names, own, others):
        rel = [relation_of(j, chip) for j in range(N_CHIPS)]
        return {k: jnp.concatenate(
            [jnp.where(r < 0, mine, jnp.where(r == 0, theirs[0], jnp.where(r == 1, theirs[1], theirs[2]))) for r in rel],
            axis=BIG[k] - 1) for k, mine, theirs in zip(names, own, others)}

    first_others = run_exchange(gather_exchange(shards(0, first_names)), name="gather_w_in_l0")
    placed = []
    for k in CONV_SHARDED:
        sh = wv[k]
        whole = jnp.zeros(sh.shape[:-1] + (sh.shape[-1] * N_CHIPS,), F32)
        whole = lax.dynamic_update_slice_in_dim(whole, sh, chip * sh.shape[-1], axis=sh.ndim - 1)
        placed.append(jnp.where(core == 1, whole, 0.0))
    conv_full = _unpack_rows(all_sum_small(_pack_rows(placed, LANE, F32), name="gather_conv_weights"),
                             [p.shape for p in placed])
    small = {k: wv[k] for k in WEIGHTS if k not in BIG}
    small.update(dict(zip(CONV_SHARDED, conv_full)))

    T = Bl * S
    pos, invf = _rope_inputs(positions)
    group_a = [(k, 1) for k in big_names] + [(k, 0) for k in EARLY_GRADS]
    group_b = [(k, 0) for k in big_names if k not in EARLY_GRADS]

    def scatter_of(group, layer_grads):
        send = [_split_for_chips(layer_grads[l][k], BIG[k] - 1) for k, l in group]
        return send, scatter_exchange(send)

    layer_grads = [None] * DEPTH
    sent = {}
    w0 = _layer_weights(whole_weights(first_names, shards(0, first_names), first_others), small, 0)
    h, saved0, w0, others1 = _layer_fwd(
        x.reshape(T, D), pos, invf, w0, S, 0, hosted=gather_exchange(shards(1, big_names)),
        scan_hosted=gather_exchange(shards(0, rest_names)),
        late_weights=lambda got: _matmul_weights(whole_weights(rest_names, shards(0, rest_names), got)))
    w1 = _layer_weights(whole_weights(big_names, shards(1, big_names), others1), small, 1)
    h, saved1, w1, _ = _layer_fwd(h, pos, invf, w1, S, 1)
    loss, dh, g_final_norm = final_loss(h, small['final_norm'], loss_target.reshape(T, D))
    dh, layer_grads[1], _, _ = _layer_bwd(dh, pos, invf, w1, saved1, S, 1)

    def host_a(early):
        layer_grads[0] = early
        sent['a'], ex = scatter_of(group_a, layer_grads)
        return ex

    def host_b(_):
        sent['b'], ex = scatter_of(group_b, layer_grads)
        return ex

    dx, layer_grads[0], others_a, others_b = _layer_bwd(dh, pos, invf, w0, saved0, S, 0, host=host_a, late_host=host_b)
    small_names = [k for k in WEIGHTS if k not in BIG]
    grads = {k: jnp.stack([layer_grads[l][k] for l in range(DEPTH)]) for k in small_names if k != 'final_norm'}
    grads['final_norm'] = g_final_norm[0]

    pieces = [{}, {}]
    for tag, group, others in (('a', group_a, others_a), ('b', group_b, others_b)):
        mine = [sum_chips(lax.dynamic_index_in_dim(s, chip, 0, keepdims=False), o, name=f"sum_chips_{k}_l{l}")
                for s, o, (k, l) in zip(sent[tag], others, group)]
        theirs = sibling_swap(mine, name=f"swap_core_sums_{tag}")
        pieces[0].update(dict(zip(group, mine)))
        pieces[1].update(dict(zip(group, theirs)))
    small_sum = all_sum_small(_pack_rows([grads[k] for k in small_names] + [loss[0, :1]], LANE, F32), name="sum_small_grads")
    summed = _unpack_rows(small_sum, [grads[k].shape for k in small_names] + [(1,)])
    loss_total = summed[-1].reshape(())
    g_small = dict(zip(small_names, summed[:-1]))
    for k in CONV_SHARDED:
        n = wv[k].shape[-1]
        g_small[k] = lax.dynamic_slice_in_dim(g_small[k], chip * n, n, axis=g_small[k].ndim - 1)

    out_g, out_d, out_m, out_v = {}, {}, {}, {}
    for k in big_names:
        g_layers = [[pieces[0][(k, l)], pieces[1][(k, l)]] for l in range(DEPTH)]
        out_g[k], out_d[k], out_m[k], out_v[k] = adamw_layers(wv[k], g_layers, mv[k], vv[k], name=f"adamw_{k}")
    at_least_2d = lambda a: a.reshape(1, -1) if a.ndim == 1 else a
    res = adamw_small(*[[at_least_2d(d[k]) for k in small_names] for d in (wv, g_small, mv, vv)], name="adamw_small")
    out_g.update(g_small)
    for dst, r in zip((out_d, out_m, out_v), res):
        dst.update({k: a.reshape(wv[k].shape) for k, a in zip(small_names, r)})
    return (loss_total, dx.reshape(Bl, S, D), *[out_g[k] for k in WEIGHTS], *[out_d[k] for k in WEIGHTS],
            *[out_m[k] for k in WEIGHTS], *[out_v[k] for k in WEIGHTS])
```

```python
import functools
import math
from typing import Callable, NamedTuple

import jax
import jax.numpy as jnp
from jax import lax
from jax.experimental import pallas as pl
from jax.experimental.pallas import tpu as pltpu

F32 = jnp.float32
MXU_DTYPE = jnp.bfloat16
HI = lax.Precision.HIGHEST

D_MODEL = 1024
DEPTH = 2
EPS = 1e-6
SSD_HEADS = 16
SSD_HEAD_DIM = 64
SSD_WIDTH = 1024
SSD_GROUPS = 2
SSD_STATE = 128
SSD_CONV = 4
SSD_CHUNK = 128
SSD_CONV_CH = 1536
POOL_GROUPS = 4
POOL_GROUP_DIM = 128
POOL_WIDTH = 512
POOL_WINDOWS = (2, 4, 8, 16)
MLA_HEADS = 8
MLA_Q_RANK = 384
MLA_KV_RANK = 256
MLA_NOPE = 64
MLA_ROPE = 32
MLA_V = 64
MLA_QK = 96
MLA_WIDTH = 512
ROPE_THETA = 10000.0
MIX_WIDTH = 2048
IN_COLS = 3760
D_FF = 2816
FFN_CONV = 3
ADAM_LR = 0.001
ADAM_B1 = 0.9
ADAM_B2 = 0.999
ADAM_EPS = 1e-08
ADAM_WD = 0.01
ADAM_STEP = 10

LANE = 128
HALO = 8
POOL_HALO = 16
PZ0 = 0
PXBC0 = 1024
PU0 = 2560
PCQ0 = 3072
PDT0 = 3456
PCKV0 = 3584
PKPE0 = 3840
PROJ_W = 4096
HEAD_W = 128
MLA_PAD = MLA_HEADS * HEAD_W
MIXP = SSD_WIDTH + POOL_WIDTH + MLA_PAD
N_CHIPS = 4
N_DEV = 8
VMEM_LIMIT = 56 * 1024 * 1024


def _cparams(dims, vmem=None):
    return pltpu.CompilerParams(dimension_semantics=dims, vmem_limit_bytes=vmem or VMEM_LIMIT)


def _sds(shape, dtype):
    return jax.ShapeDtypeStruct(tuple(shape), dtype)


def _mx(v):
    return v.astype(MXU_DTYPE)


def _dot(a, b):
    return jnp.dot(_mx(a), _mx(b), preferred_element_type=F32)


def _dot_nt(a, b):
    return lax.dot_general(_mx(a), _mx(b), (((1,), (1,)), ((), ())), preferred_element_type=F32)


def _dot_tn(a, b):
    return lax.dot_general(_mx(a), _mx(b), (((0,), (0,)), ((), ())), preferred_element_type=F32)


def _dot_hi(a, b):
    return jnp.dot(a, b, preferred_element_type=F32, precision=HI)


def _sigmoid(v):
    return 1.0 / (1.0 + jnp.exp(-v))


def _pick(n, prefs):
    for p in prefs:
        if n % p == 0:
            return p
    return n


def matmul(a, b, *, res=None, out_dtype=F32, name, nt=False, kblock=0, tm=None, tn=None, hosted=None):
    M, K = a.shape
    N = b.shape[0] if nt else b.shape[1]
    assert (b.shape[1] % K == 0) if nt else (K == b.shape[0] and kblock == 0)
    tm = tm or _pick(M, (1024, 512, 256, 128))
    tn = tn or _pick(N, (1024, 1408, 1280, 512, 256, 128))
    grid = (M // tm, N // tn)
    wrap, h_in, h_ospecs, h_oshapes, h_scratch = _hosting(hosted, grid, 2 if res is None else 3, 1, 0)

    def body(*refs):
        a_ref, b_ref = refs[:2]
        o_ref = refs[-1]
        out = (_dot_nt if nt else _dot)(a_ref[...], b_ref[...])
        if res is not None:
            out = out + refs[2][...]
        o_ref[...] = out.astype(out_dtype)

    b_spec = pl.BlockSpec((tn, K), lambda i, j: (j, kblock)) if nt else pl.BlockSpec((K, tn), lambda i, j: (0, j))
    in_specs = [pl.BlockSpec((tm, K), lambda i, j: (i, 0)), b_spec]
    args = [a, b]
    if res is not None:
        in_specs.append(pl.BlockSpec((tm, tn), lambda i, j: (i, j)))
        args.append(res)
    out = pl.pallas_call(
        wrap(body), name=name, grid=grid, in_specs=in_specs + [pl.BlockSpec(memory_space=pl.ANY)] * len(h_in),
        out_specs=[pl.BlockSpec((tm, tn), lambda i, j: (i, j))] + h_ospecs, out_shape=[_sds((M, N), out_dtype)] + h_oshapes,
        scratch_shapes=h_scratch,
        compiler_params=_cparams(("arbitrary", "arbitrary") if hosted else ("parallel", "parallel")),
    )(*args, *h_in)
    return out if hosted else out[0]


def matmul_tn(a, g, *, name, tm=None, tn=None, tk=None):
    T, M = a.shape
    T2, N = g.shape
    assert T == T2
    tm = tm or _pick(M, (1408, 1280, 1024, 512, 384, 256, 128))
    tn = tn or _pick(N, (1024, 1408, 512, 256, 128))
    tk = tk or _pick(T, (1024, 512, 256, 128))
    nk = T // tk

    def body(a_ref, g_ref, o_ref, acc):
        k = pl.program_id(2)
        part = _dot_tn(a_ref[...], g_ref[...])

        @pl.when(k == 0)
        def _():
            acc[...] = part

        @pl.when(k > 0)
        def _():
            acc[...] += part

        @pl.when(k == nk - 1)
        def _():
            o_ref[...] = acc[...].astype(o_ref.dtype)

    return pl.pallas_call(
        body, name=name, grid=(M // tm, N // tn, nk),
        in_specs=[pl.BlockSpec((tk, tm), lambda i, j, k: (k, i)), pl.BlockSpec((tk, tn), lambda i, j, k: (k, j))],
        out_specs=pl.BlockSpec((tm, tn), lambda i, j, k: (i, j)), out_shape=_sds((M, N), MXU_DTYPE),
        scratch_shapes=[pltpu.VMEM((tm, tn), F32)],
        compiler_params=_cparams(("parallel", "parallel", "arbitrary")),
    )(a, g)


def rmsnorm_fwd(x, gamma, *, name, tm=512):
    T, D = x.shape
    tm = _pick(T, (tm, 256, 128))

    def body(x_ref, g_ref, o_ref):
        xv = x_ref[...]
        r = lax.rsqrt(jnp.mean(xv * xv, axis=-1, keepdims=True) + EPS)
        o_ref[...] = ((xv * r) * g_ref[...]).astype(MXU_DTYPE)

    return pl.pallas_call(
        body, name=name, grid=(T // tm,),
        in_specs=[pl.BlockSpec((tm, D), lambda i: (i, 0)), pl.BlockSpec((1, D), lambda i: (0, 0))],
        out_specs=pl.BlockSpec((tm, D), lambda i: (i, 0)), out_shape=_sds((T, D), MXU_DTYPE),
        compiler_params=_cparams(("parallel",)),
    )(x, gamma.reshape(1, D))


def _rms_bwd_tile(xv, gamma, dh):
    r = lax.rsqrt(jnp.mean(xv * xv, axis=-1, keepdims=True) + EPS)
    xh = xv * r
    dg = jnp.sum(dh * xh, axis=0, keepdims=True)
    dn = dh * gamma
    dx = r * (dn - xh * jnp.mean(dn * xh, axis=-1, keepdims=True))
    return dx, dg


def rmsnorm_bwd(x, gamma, dh, dres, *, name, tm=256):
    T, D = x.shape
    tm = _pick(T, (tm, 128))

    def body(x_ref, g_ref, dh_ref, dr_ref, dx_ref, dg_ref):
        dx, dg = _rms_bwd_tile(x_ref[...], g_ref[...], dh_ref[...].astype(F32))
        dx_ref[...] = dx + dr_ref[...]

        @pl.when(pl.program_id(0) == 0)
        def _():
            dg_ref[...] = dg

        @pl.when(pl.program_id(0) > 0)
        def _():
            dg_ref[...] += dg

    row = pl.BlockSpec((tm, D), lambda i: (i, 0))
    vec = pl.BlockSpec((1, D), lambda i: (0, 0))
    return pl.pallas_call(
        body, name=name, grid=(T // tm,), in_specs=[row, vec, row, row], out_specs=[row, vec],
        out_shape=[_sds((T, D), F32), _sds((1, D), F32)], compiler_params=_cparams(("arbitrary",)),
    )(x, gamma.reshape(1, D), dh, dres)


def final_loss(x, gamma, target, *, name="final_loss", tm=256):
    T, D = x.shape
    tm = _pick(T, (tm, 128))

    def body(x_ref, g_ref, t_ref, l_ref, dx_ref, dg_ref):
        xv = x_ref[...]
        gam = g_ref[...]
        r = lax.rsqrt(jnp.mean(xv * xv, axis=-1, keepdims=True) + EPS)
        y = (xv * r) * gam
        err = y - t_ref[...]
        part = 0.5 * jnp.sum(jnp.sum(err * err, axis=-1, keepdims=True) / D, axis=0, keepdims=True)
        dx, dg = _rms_bwd_tile(xv, gam, err / D)
        dx_ref[...] = dx

        @pl.when(pl.program_id(0) == 0)
        def _():
            dg_ref[...] = dg
            l_ref[...] = jnp.broadcast_to(part, l_ref.shape)

        @pl.when(pl.program_id(0) > 0)
        def _():
            dg_ref[...] += dg
            l_ref[...] += jnp.broadcast_to(part, l_ref.shape)

    row = pl.BlockSpec((tm, D), lambda i: (i, 0))
    vec = pl.BlockSpec((1, D), lambda i: (0, 0))
    return pl.pallas_call(
        body, name=name, grid=(T // tm,), in_specs=[row, vec, row],
        out_specs=[pl.BlockSpec((1, LANE), lambda i: (0, 0)), row, vec],
        out_shape=[_sds((1, LANE), F32), _sds((T, D), F32), _sds((1, D), F32)],
        compiler_params=_cparams(("arbitrary",)),
    )(x, gamma.reshape(1, D), target)


def _halo_prev(ts):
    return lambda i, j, off=0: (jnp.maximum(i * (ts // HALO) - 1, 0), j + off)


def _cat_prev(cur, halo, first):
    return jnp.concatenate([jnp.where(first, 0.0, halo), cur], axis=0)


def _cat_next(cur, halo, last):
    return jnp.concatenate([cur, jnp.where(last, 0.0, halo)], axis=0)


def _delayed(cat, r):
    if r == 0:
        return cat[HALO:]
    return pltpu.roll(cat, r, axis=0)[HALO:]


def _advanced(cat, r):
    n = cat.shape[0]
    if r == 0:
        return cat[:n - HALO]
    return pltpu.roll(cat, n - r, axis=0)[:n - HALO]


def _conv_pre(cat, w, b, K):
    acc = _delayed(cat, K - 1) * w[0:1, :] + b
    for k in range(1, K):
        acc = acc + _delayed(cat, K - 1 - k) * w[k:k + 1, :]
    return acc


def _pad_rows8(w):
    return jnp.pad(w, ((0, 8 - w.shape[0]), (0, 0)))


def ssd_conv_fwd(proj, w, b, S, *, name, ts=1024, tc=512):
    T = proj.shape[0]
    C, K = SSD_CONV_CH, SSD_CONV
    ts = _pick(S, (ts, 256, 128))
    off = PXBC0 // tc
    ns = S // ts

    def body(x_ref, h_ref, w_ref, b_ref, o_ref):
        first = (pl.program_id(0) % ns) == 0
        pre = _conv_pre(_cat_prev(x_ref[...], h_ref[...], first), w_ref[...], b_ref[...], K)
        o_ref[...] = pre * _sigmoid(pre)

    return pl.pallas_call(
        body, name=name, grid=(T // ts, C // tc),
        in_specs=[pl.BlockSpec((ts, tc), lambda i, j: (i, j + off)),
                  pl.BlockSpec((HALO, tc), functools.partial(_halo_prev(ts), off=off)),
                  pl.BlockSpec((8, tc), lambda i, j: (0, j)), pl.BlockSpec((1, tc), lambda i, j: (0, j))],
        out_specs=pl.BlockSpec((ts, tc), lambda i, j: (i, j)), out_shape=_sds((T, C), F32),
        compiler_params=_cparams(("parallel", "parallel")),
    )(proj, proj, _pad_rows8(w), b.reshape(1, C))


def _conv_stats(dpre, cat, K, ts):
    rows = [jnp.sum(dpre[:ts] * _delayed(cat, K - 1 - k)[:ts], axis=0, keepdims=True) for k in range(K)]
    rows.append(jnp.sum(dpre[:ts], axis=0, keepdims=True))
    rows.append(jnp.zeros((8 - len(rows), dpre.shape[1]), F32))
    return jnp.concatenate(rows, axis=0)


def _conv_transposed(dpre, wv, K):
    acc = _advanced(dpre, K - 1) * wv[0:1, :]
    for k in range(1, K):
        acc = acc + _advanced(dpre, K - 1 - k) * wv[k:k + 1, :]
    return acc


def ssd_conv_bwd(proj, w, b, dxc, S, *, name, ts=512, tc=512):
    T = proj.shape[0]
    C, K = SSD_CONV_CH, SSD_CONV
    ts = _pick(S, (ts, 256, 128))
    off = PXBC0 // tc
    ns = S // ts
    nblk = T // HALO

    def body(x_ref, xp_ref, xn_ref, w_ref, b_ref, d_ref, dn_ref, o_ref, acc_ref):
        i = pl.program_id(1)
        first = (i % ns) == 0
        last = (i % ns) == ns - 1
        cat = jnp.concatenate([jnp.where(first, 0.0, xp_ref[...]), x_ref[...], xn_ref[...]], axis=0)
        wv = w_ref[...]
        pre = _conv_pre(cat, wv, b_ref[...], K)
        sg = _sigmoid(pre)
        dpre = _cat_next(d_ref[...], dn_ref[...], last) * (sg * (1.0 + pre * (1.0 - sg)))
        o_ref[...] = _conv_transposed(dpre, wv, K).astype(o_ref.dtype)
        part = _conv_stats(dpre, cat, K, ts)

        @pl.when(i == 0)
        def _():
            acc_ref[...] = part

        @pl.when(i > 0)
        def _():
            acc_ref[...] += part

    hp = _halo_prev(ts)
    hn = lambda i: jnp.minimum((i + 1) * (ts // HALO), nblk - 1)
    return pl.pallas_call(
        body, name=name, grid=(C // tc, T // ts),
        in_specs=[pl.BlockSpec((ts, tc), lambda j, i: (i, j + off)),
                  pl.BlockSpec((HALO, tc), lambda j, i: hp(i, j, off)),
                  pl.BlockSpec((HALO, tc), lambda j, i: (hn(i), j + off)),
                  pl.BlockSpec((8, tc), lambda j, i: (0, j)), pl.BlockSpec((1, tc), lambda j, i: (0, j)),
                  pl.BlockSpec((ts, tc), lambda j, i: (i, j)), pl.BlockSpec((HALO, tc), lambda j, i: (hn(i), j))],
        out_specs=[pl.BlockSpec((ts, tc), lambda j, i: (i, j)), pl.BlockSpec((8, tc), lambda j, i: (0, j))],
        out_shape=[_sds((T, C), MXU_DTYPE), _sds((8, C), F32)],
        compiler_params=_cparams(("parallel", "arbitrary")),
    )(proj, proj, proj, _pad_rows8(w), b.reshape(1, C), dxc, dxc)


def ffn_conv_gate_fwd(up, w, b, S, *, name, ts=512, tc=1408):
    T, C2 = up.shape
    C, K = C2 // 2, FFN_CONV
    ts = _pick(S, (ts, 256, 128))
    nj = C // tc
    ns = S // ts
    w8 = _pad_rows8(w)
    b2 = b.reshape(1, C2)

    def body(g_ref, gh_ref, v_ref, vh_ref, wg_ref, wv_ref, bg_ref, bv_ref, o_ref):
        first = (pl.program_id(0) % ns) == 0
        g = _conv_pre(_cat_prev(g_ref[...], gh_ref[...], first), wg_ref[...], bg_ref[...], K)
        v = _conv_pre(_cat_prev(v_ref[...], vh_ref[...], first), wv_ref[...], bv_ref[...], K)
        o_ref[...] = (g * _sigmoid(g) * v).astype(o_ref.dtype)

    hp = _halo_prev(ts)
    return pl.pallas_call(
        body, name=name, grid=(T // ts, nj),
        in_specs=[pl.BlockSpec((ts, tc), lambda i, j: (i, j)), pl.BlockSpec((HALO, tc), lambda i, j: hp(i, j)),
                  pl.BlockSpec((ts, tc), lambda i, j: (i, j + nj)), pl.BlockSpec((HALO, tc), lambda i, j: hp(i, j, nj)),
                  pl.BlockSpec((8, tc), lambda i, j: (0, j)), pl.BlockSpec((8, tc), lambda i, j: (0, j + nj)),
                  pl.BlockSpec((1, tc), lambda i, j: (0, j)), pl.BlockSpec((1, tc), lambda i, j: (0, j + nj))],
        out_specs=pl.BlockSpec((ts, tc), lambda i, j: (i, j)), out_shape=_sds((T, C), MXU_DTYPE),
        compiler_params=_cparams(("parallel", "parallel")),
    )(up, up, up, up, w8, w8, b2, b2)


def ffn_conv_gate_bwd(up, w, b, dact, S, *, name, ts=256, tc=1408):
    T, C2 = up.shape
    C, K = C2 // 2, FFN_CONV
    ts = _pick(S, (ts, 256, 128))
    nj = C // tc
    ns = S // ts
    nblk = T // HALO
    w8 = _pad_rows8(w)
    b2 = b.reshape(1, C2)

    def body(g_ref, gp_ref, gn_ref, v_ref, vp_ref, vn_ref, wg_ref, wv_ref, bg_ref, bv_ref, d_ref, dn_ref,
             dug_ref, duv_ref, ag_ref, av_ref):
        i = pl.program_id(1)
        first = (i % ns) == 0
        last = (i % ns) == ns - 1
        gcat = jnp.concatenate([jnp.where(first, 0.0, gp_ref[...]), g_ref[...], gn_ref[...]], axis=0)
        vcat = jnp.concatenate([jnp.where(first, 0.0, vp_ref[...]), v_ref[...], vn_ref[...]], axis=0)
        wg, wv = wg_ref[...], wv_ref[...]
        g = _conv_pre(gcat, wg, bg_ref[...], K)
        v = _conv_pre(vcat, wv, bv_ref[...], K)
        d = _cat_next(d_ref[...], dn_ref[...], last)
        sg = _sigmoid(g)
        dg = d * v * (sg * (1.0 + g * (1.0 - sg)))
        dv = d * (g * sg)
        dug_ref[...] = _conv_transposed(dg, wg, K).astype(dug_ref.dtype)
        duv_ref[...] = _conv_transposed(dv, wv, K).astype(duv_ref.dtype)
        sgp, svp = _conv_stats(dg, gcat, K, ts), _conv_stats(dv, vcat, K, ts)

        @pl.when(i == 0)
        def _():
            ag_ref[...] = sgp
            av_ref[...] = svp

        @pl.when(i > 0)
        def _():
            ag_ref[...] += sgp
            av_ref[...] += svp

    hp = _halo_prev(ts)
    hn = lambda i: jnp.minimum((i + 1) * (ts // HALO), nblk - 1)
    cur = lambda off: pl.BlockSpec((ts, tc), lambda j, i: (i, j + off))
    prv = lambda off: pl.BlockSpec((HALO, tc), lambda j, i: hp(i, j, off))
    nxt = lambda off: pl.BlockSpec((HALO, tc), lambda j, i: (hn(i), j + off))
    row = lambda r, off: pl.BlockSpec((r, tc), lambda j, i: (0, j + off))
    dug, duv, ag, av = pl.pallas_call(
        body, name=name, grid=(nj, T // ts),
        in_specs=[cur(0), prv(0), nxt(0), cur(nj), prv(nj), nxt(nj), row(8, 0), row(8, nj), row(1, 0), row(1, nj),
                  cur(0), nxt(0)],
        out_specs=[cur(0), cur(0), row(8, 0), row(8, 0)],
        out_shape=[_sds((T, C), MXU_DTYPE), _sds((T, C), MXU_DTYPE), _sds((8, C), F32), _sds((8, C), F32)],
        compiler_params=_cparams(("parallel", "arbitrary")),
    )(up, up, up, up, up, up, w8, w8, b2, b2, dact, dact)
    return dug, duv, jnp.concatenate([ag, av], axis=1)


def _pool_counts(pos, w):
    return jnp.minimum(pos + 1.0, float(w))


def pool_fwd(proj, pool_w, pool_scale, S, *, name, ts=512):
    T = proj.shape[0]
    C, G, GD, H = POOL_WIDTH, POOL_GROUPS, POOL_GROUP_DIM, POOL_HALO
    ts = _pick(S, (ts, 256, 128))
    ns = S // ts
    off = PU0 // C

    def body(u_ref, h_ref, w_ref, s_ref, y_ref, p_ref):
        i = pl.program_id(0)
        first = (i % ns) == 0
        cat = jnp.concatenate([jnp.where(first, 0.0, h_ref[...]), u_ref[...]], axis=0)
        pos = ((i % ns) * ts + lax.broadcasted_iota(jnp.int32, (ts, 1), 0)).astype(F32)
        sums = cat
        win = 1
        for g, wlen in enumerate(POOL_WINDOWS):
            while win < wlen:
                sums = sums + pltpu.roll(sums, win, axis=0)
                win *= 2
            sl = slice(g * GD, (g + 1) * GD)
            pooled = sums[H:, sl] / _pool_counts(pos, wlen) - cat[H:, sl]
            p_ref[:, sl] = pooled.astype(p_ref.dtype)
            y_ref[:, sl] = (_dot(pooled, w_ref[g]) * s_ref[:, sl]).astype(y_ref.dtype)

    return pl.pallas_call(
        body, name=name, grid=(T // ts,),
        in_specs=[pl.BlockSpec((ts, C), lambda i: (i, off)),
                  pl.BlockSpec((H, C), lambda i: (jnp.maximum(i * (ts // H) - 1, 0), off)),
                  pl.BlockSpec((G, GD, GD), lambda i: (0, 0, 0)), pl.BlockSpec((1, C), lambda i: (0, 0))],
        out_specs=[pl.BlockSpec((ts, C), lambda i: (i, 0)), pl.BlockSpec((ts, C), lambda i: (i, 0))],
        out_shape=[_sds((T, C), MXU_DTYPE), _sds((T, C), MXU_DTYPE)],
        compiler_params=_cparams(("parallel",)),
    )(proj, proj, _mx(pool_w), pool_scale.reshape(1, C))


def pool_bwd(dmix, pooled, pool_w, pool_scale, S, *, name, ts=512):
    T = dmix.shape[0]
    C, G, GD, H = POOL_WIDTH, POOL_GROUPS, POOL_GROUP_DIM, POOL_HALO
    ts = _pick(S, (ts, 256, 128))
    ns = S // ts
    off = SSD_WIDTH // C
    nblk = T // H

    def body(d_ref, dh_ref, p_ref, w_ref, s_ref, du_ref, dw_ref, ds_ref):
        i = pl.program_id(0)
        last = (i % ns) == ns - 1
        dcat = jnp.concatenate([d_ref[...], jnp.where(last, 0.0, dh_ref[...])], axis=0)
        n = ts + H
        pos = ((i % ns) * ts + lax.broadcasted_iota(jnp.int32, (n, 1), 0)).astype(F32)
        dws, dss = [], []
        for g, wlen in enumerate(POOL_WINDOWS):
            sl = slice(g * GD, (g + 1) * GD)
            wg = w_ref[g]
            pg = p_ref[:, sl]
            dys = dcat[:, sl] * s_ref[:, sl]
            dss.append(jnp.sum(dcat[:ts, sl] * _dot(pg, wg), axis=0, keepdims=True))
            dws.append(_dot_tn(pg, dys[:ts]))
            dp = _dot_nt(dys, wg)
            q = dp / _pool_counts(pos, wlen)
            win = 1
            while win < wlen:
                q = q + pltpu.roll(q, n - win, axis=0)
                win *= 2
            du_ref[:, sl] = (q[:ts] - dp[:ts]).astype(du_ref.dtype)
        dsp = jnp.concatenate(dss, axis=1)

        @pl.when(i == 0)
        def _():
            for g in range(G):
                dw_ref[g] = dws[g]
            ds_ref[...] = dsp

        @pl.when(i > 0)
        def _():
            for g in range(G):
                dw_ref[g] += dws[g]
            ds_ref[...] += dsp

    return pl.pallas_call(
        body, name=name, grid=(T // ts,),
        in_specs=[pl.BlockSpec((ts, C), lambda i: (i, off)),
                  pl.BlockSpec((H, C), lambda i: (jnp.minimum((i + 1) * (ts // H), nblk - 1), off)),
                  pl.BlockSpec((ts, C), lambda i: (i, 0)),
                  pl.BlockSpec((G, GD, GD), lambda i: (0, 0, 0)), pl.BlockSpec((1, C), lambda i: (0, 0))],
        out_specs=[pl.BlockSpec((ts, C), lambda i: (i, 0)), pl.BlockSpec((G, GD, GD), lambda i: (0, 0, 0)),
                   pl.BlockSpec((1, C), lambda i: (0, 0))],
        out_shape=[_sds((T, C), MXU_DTYPE), _sds((G, GD, GD), F32), _sds((1, C), F32)],
        compiler_params=_cparams(("arbitrary",)),
    )(dmix, dmix, pooled, _mx(pool_w), pool_scale.reshape(1, C))


ROPE0 = MLA_NOPE
ROPE_HALF = MLA_ROPE // 2


def _rope_tables(pos, invf):
    lane = lax.broadcasted_iota(jnp.int32, (1, HEAD_W), 1)
    ang = pos * invf
    cs, sn = jnp.cos(ang), jnp.sin(ang)
    in_a = (lane >= ROPE0) & (lane < ROPE0 + ROPE_HALF)
    in_b = (lane >= ROPE0 + ROPE_HALF) & (lane < ROPE0 + MLA_ROPE)
    return jnp.where(in_a | in_b, cs, 1.0), jnp.where(in_a, -sn, 0.0), jnp.where(in_b, sn, 0.0), in_a | in_b


def _rope(v, cosf, sin_a, sin_b):
    return (v * cosf + pltpu.roll(v, HEAD_W - ROPE_HALF, axis=1) * sin_a + pltpu.roll(v, ROPE_HALF, axis=1) * sin_b)


def _unrope(d, cosf, sin_a, sin_b):
    return (d * cosf + pltpu.roll(d * sin_a, ROPE_HALF, axis=1) + pltpu.roll(d * sin_b, HEAD_W - ROPE_HALF, axis=1))


def _rms_tile(xv, gamma):
    return (xv * lax.rsqrt(jnp.mean(xv * xv, axis=-1, keepdims=True) + EPS)) * gamma


def mla_prep_fwd(proj, pos, invf, q_norm, w_uq_p, kv_norm, w_ukv_p, *, name, tm=256):
    T = proj.shape[0]
    tm = _pick(T, (tm, 128))
    QR, KR, P = MLA_Q_RANK, MLA_KV_RANK, MLA_PAD

    def body(cq_ref, ckv_ref, kpe_ref, pos_ref, invf_ref, qn_ref, wq_ref, kn_ref, wkv_ref,
             q_ref, k_ref, v_ref, cqn_ref, ckvn_ref):
        cosf, sin_a, sin_b, _ = _rope_tables(pos_ref[...], invf_ref[...])
        cqn = _rms_tile(cq_ref[...], qn_ref[...]).astype(MXU_DTYPE)
        ckvn = _rms_tile(ckv_ref[...], kn_ref[...]).astype(MXU_DTYPE)
        cqn_ref[...] = cqn
        ckvn_ref[...] = ckvn
        qp = _dot(cqn, wq_ref[...])
        kvp = _dot(ckvn, wkv_ref[...])
        kpe = _rope(kpe_ref[...], cosf, sin_a, sin_b)
        for h in range(MLA_HEADS):
            sl = slice(h * HEAD_W, (h + 1) * HEAD_W)
            q_ref[:, sl] = (_rope(qp[:, sl], cosf, sin_a, sin_b) * ATTN_SCALE).astype(q_ref.dtype)
            k_ref[:, sl] = (kvp[:, sl] + kpe).astype(k_ref.dtype)
            v_ref[:, sl] = kvp[:, P + h * HEAD_W:P + (h + 1) * HEAD_W].astype(v_ref.dtype)

    row = lambda w: pl.BlockSpec((tm, w), lambda i: (i, 0))
    full = lambda a, b: pl.BlockSpec((a, b), lambda i: (0, 0))
    return pl.pallas_call(
        body, name=name, grid=(T // tm,),
        in_specs=[pl.BlockSpec((tm, QR), lambda i: (i, PCQ0 // QR)), pl.BlockSpec((tm, KR), lambda i: (i, PCKV0 // KR)),
                  pl.BlockSpec((tm, LANE), lambda i: (i, PKPE0 // LANE)), row(1), full(1, LANE),
                  full(1, QR), full(QR, P), full(1, KR), full(KR, 2 * P)],
        out_specs=[row(P), row(P), row(P), row(QR), row(KR)],
        out_shape=[_sds((T, P), MXU_DTYPE)] * 3 + [_sds((T, QR), MXU_DTYPE), _sds((T, KR), MXU_DTYPE)],
        compiler_params=_cparams(("parallel",)),
    )(proj, proj, proj, pos, invf, q_norm.reshape(1, QR), w_uq_p, kv_norm.reshape(1, KR), w_ukv_p)


def mla_prep_bwd(proj, pos, invf, q_norm, w_uq_p, kv_norm, w_ukv_p, dq, dk, dv, *, name, tm=256):
    T = proj.shape[0]
    tm = _pick(T, (tm, 128))
    QR, KR, P = MLA_Q_RANK, MLA_KV_RANK, MLA_PAD

    def body(cq_ref, ckv_ref, pos_ref, invf_ref, qn_ref, wq_ref, kn_ref, wkv_ref, dq_ref, dk_ref, dv_ref,
             dqp_ref, dkvp_ref, dcq_ref, dckv_ref, dkpe_ref, dqn_ref, dkn_ref):
        cosf, sin_a, sin_b, rot = _rope_tables(pos_ref[...], invf_ref[...])
        dkpe = jnp.zeros((tm, HEAD_W), F32)
        for h in range(MLA_HEADS):
            sl = slice(h * HEAD_W, (h + 1) * HEAD_W)
            dqp_ref[:, sl] = _unrope(dq_ref[:, sl] * ATTN_SCALE, cosf, sin_a, sin_b).astype(dqp_ref.dtype)
            dkh = dk_ref[:, sl]
            dkpe = dkpe + dkh
            dkvp_ref[:, sl] = dkh.astype(dkvp_ref.dtype)
            dkvp_ref[:, P + h * HEAD_W:P + (h + 1) * HEAD_W] = dv_ref[:, sl].astype(dkvp_ref.dtype)
        dkpe_ref[...] = jnp.where(rot, _unrope(dkpe, cosf, sin_a, sin_b), 0.0).astype(dkpe_ref.dtype)
        dcq, dqn = _rms_bwd_tile(cq_ref[...], qn_ref[...], _dot_nt(dqp_ref[...], wq_ref[...]))
        dckv, dkn = _rms_bwd_tile(ckv_ref[...], kn_ref[...], _dot_nt(dkvp_ref[...], wkv_ref[...]))
        dcq_ref[...] = dcq.astype(dcq_ref.dtype)
        dckv_ref[...] = dckv.astype(dckv_ref.dtype)

        @pl.when(pl.program_id(0) == 0)
        def _():
            dqn_ref[...] = dqn
            dkn_ref[...] = dkn

        @pl.when(pl.program_id(0) > 0)
        def _():
            dqn_ref[...] += dqn
            dkn_ref[...] += dkn

    row = lambda w: pl.BlockSpec((tm, w), lambda i: (i, 0))
    full = lambda a, b: pl.BlockSpec((a, b), lambda i: (0, 0))
    return pl.pallas_call(
        body, name=name, grid=(T // tm,),
        in_specs=[pl.BlockSpec((tm, QR), lambda i: (i, PCQ0 // QR)), pl.BlockSpec((tm, KR), lambda i: (i, PCKV0 // KR)),
                  row(1), full(1, LANE), full(1, QR), full(QR, P), full(1, KR), full(KR, 2 * P), row(P), row(P), row(P)],
        out_specs=[row(P), row(2 * P), row(QR), row(KR), row(LANE), full(1, QR), full(1, KR)],
        out_shape=[_sds((T, P), MXU_DTYPE), _sds((T, 2 * P), MXU_DTYPE), _sds((T, QR), MXU_DTYPE),
                   _sds((T, KR), MXU_DTYPE), _sds((T, LANE), MXU_DTYPE), _sds((1, QR), F32), _sds((1, KR), F32)],
        compiler_params=_cparams(("arbitrary",)),
    )(proj, proj, pos, invf, q_norm.reshape(1, QR), w_uq_p, kv_norm.reshape(1, KR), w_ukv_p, dq, dk, dv)


ATTN_SCALE = 1.0 / math.sqrt(MLA_QK)


def _causal_mask(i, j, blk):
    row = lax.broadcasted_iota(jnp.int32, (blk, blk), 0)
    col = lax.broadcasted_iota(jnp.int32, (blk, blk), 1)
    return col <= row + (i - j) * blk


def _hosting(hosted, grid, n_in, n_out, n_scratch):
    if hosted is None:
        return (lambda body: body), (), [], [], []
    hi, ho = len(hosted.inputs), len(hosted.out_shapes)

    def wrap(body):
        def full(*refs):
            ins, rest = refs[:n_in + hi], refs[n_in + hi:]
            outs, scr = rest[:n_out + ho], rest[n_out + ho:]
            parts = ins[n_in:], outs[n_out:], scr[n_scratch:]
            ids = [pl.program_id(d) for d in range(len(grid))]
            step = ids[0]
            for d in range(1, len(grid)):
                step = step * grid[d] + ids[d]
            total = math.prod(grid)

            @pl.when(step == 0)
            def _():
                hosted.start(*parts)

            body(*ins[:n_in], *outs[:n_out], *scr[:n_scratch])

            @pl.when(step == total // 2)
            def _():
                hosted.relay(*parts)

            @pl.when(step == total - 1)
            def _():
                hosted.finish(*parts)

        return full

    hbm = pl.BlockSpec(memory_space=pl.ANY)
    return wrap, tuple(hosted.inputs), [hbm] * ho, list(hosted.out_shapes), list(hosted.sems)


def flash_fwd(q, k, v, S, *, name, blk=512, hosted=None):
    T, P = q.shape
    blk = _pick(S // 2, (blk, 256, 128))
    B, nq, H, W = T // S, S // (2 * blk), MLA_HEADS, HEAD_W
    grid = (B, H, nq)
    wrap, h_in, h_ospecs, h_oshapes, h_scratch = _hosting(hosted, grid, 3, 2, 0)

    def body(q_ref, k_ref, v_ref, o_ref, lse_ref):
        i = pl.program_id(2)
        q_up, q_lo = q_ref[:blk, :], q_ref[blk:, :]

        def online(qv, kv, vv, carry, masked):
            m_prev, l_prev, acc = carry
            s = _dot_nt(qv, kv)
            if masked:
                s = jnp.where(_causal_mask(0, 0, blk), s, -jnp.inf)
            m_new = jnp.maximum(m_prev, jnp.max(s, axis=1, keepdims=True))
            p = jnp.exp(s - m_new)
            alpha = jnp.exp(m_prev - m_new)
            return (m_new, alpha * l_prev + jnp.sum(p, axis=1, keepdims=True), alpha * acc + _dot(p, vv))

        def keys(j):
            rows = pl.ds(pl.multiple_of(j * blk, blk), blk)
            return k_ref[rows, :], v_ref[rows, :]

        def both(j, carry):
            kv, vv = keys(j)
            return online(q_up, kv, vv, carry[0], False), online(q_lo, kv, vv, carry[1], False)

        init = (jnp.full((blk, 1), -jnp.inf, F32), jnp.zeros((blk, 1), F32), jnp.zeros((blk, W), F32))
        up, lo = lax.fori_loop(0, 2 * i, both, (init, init))
        kv, vv = keys(2 * i)
        up = online(q_up, kv, vv, up, True)
        lo = online(q_lo, kv, vv, lo, False)
        kv, vv = keys(2 * i + 1)
        lo = online(q_lo, kv, vv, lo, True)
        for rows, (m, l, acc) in ((slice(0, blk), up), (slice(blk, 2 * blk), lo)):
            o_ref[rows, :] = acc / l
            lse_ref[rows, :] = jnp.broadcast_to(m + jnp.log(l), (blk, W))

    qmap = lambda b, h, i: (b * nq + i, h)
    kmap = lambda b, h, i: (b, h)
    hbm = pl.BlockSpec(memory_space=pl.ANY)
    return pl.pallas_call(
        wrap(body), name=name, grid=grid,
        in_specs=[pl.BlockSpec((2 * blk, W), qmap), pl.BlockSpec((S, W), kmap), pl.BlockSpec((S, W), kmap)] + [hbm] * len(h_in),
        out_specs=[pl.BlockSpec((2 * blk, W), qmap), pl.BlockSpec((2 * blk, W), qmap)] + h_ospecs,
        out_shape=[_sds((T, P), F32), _sds((T, P), F32)] + h_oshapes,
        scratch_shapes=h_scratch,
        compiler_params=_cparams(("arbitrary",) * 3 if hosted else ("parallel", "parallel", "arbitrary")),
    )(q, k, v, *h_in)


def flash_bwd(q, k, v, o, lse, dmix, S, *, name, blk=512, hosted=None):
    T, P = q.shape
    blk = _pick(S, (blk, 256, 128))
    B, nq, H, W = T // S, S // blk, MLA_HEADS, HEAD_W
    off = (SSD_WIDTH + POOL_WIDTH) // W
    grid = (B, H, nq)
    wrap, h_in, h_ospecs, h_oshapes, h_scratch = _hosting(hosted, grid, 6, 3, 1)

    def body(q_ref, k_ref, v_ref, o_ref, lse_ref, do_ref, dq_ref, dk_ref, dv_ref, delta_s):
        j = pl.program_id(2)

        @pl.when(j == 0)
        def _():
            for i in range(nq):
                rows = slice(i * blk, (i + 1) * blk)
                delta_s[rows, :] = jnp.sum(do_ref[rows, :] * o_ref[rows, :], axis=1, keepdims=True)
                dq_ref[rows, :] = jnp.zeros((blk, W), F32)

        kv, vv = k_ref[...], v_ref[...]

        def step(i, carry, masked):
            dk, dv = carry
            rows = pl.ds(pl.multiple_of(i * blk, blk), blk)
            qv, do = q_ref[rows, :], do_ref[rows, :]
            p = jnp.exp(_dot_nt(qv, kv) - lse_ref[rows, 0:1])
            if masked:
                p = jnp.where(_causal_mask(0, 0, blk), p, 0.0)
            ds = p * (_dot_nt(do, vv) - delta_s[rows, :])
            dq_ref[rows, :] += _dot(ds, kv)
            return dk + _dot_tn(ds, qv), dv + _dot_tn(p, do)

        zero = jnp.zeros((blk, W), F32)
        carry = step(j, (zero, zero), True)
        dk, dv = lax.fori_loop(j + 1, nq, lambda i, c: step(i, c, False), carry)
        dk_ref[...] = dk
        dv_ref[...] = dv

    full = lambda b, h, j: (b, h)
    kmap = lambda b, h, j: (b * nq + j, h)
    hbm = pl.BlockSpec(memory_space=pl.ANY)
    return pl.pallas_call(
        wrap(body), name=name, grid=grid,
        in_specs=[pl.BlockSpec((S, W), full), pl.BlockSpec((blk, W), kmap), pl.BlockSpec((blk, W), kmap),
                  pl.BlockSpec((S, W), full), pl.BlockSpec((S, W), full),
                  pl.BlockSpec((S, W), lambda b, h, j: (b, off + h))] + [hbm] * len(h_in),
        out_specs=[pl.BlockSpec((S, W), full), pl.BlockSpec((blk, W), kmap), pl.BlockSpec((blk, W), kmap)] + h_ospecs,
        out_shape=[_sds((T, P), F32)] * 3 + h_oshapes,
        scratch_shapes=[pltpu.VMEM((S, 1), F32)] + h_scratch,
        compiler_params=_cparams(("arbitrary",) * 3 if hosted else ("parallel", "parallel", "arbitrary")),
    )(q, k, v, o, lse, dmix, *h_in)


SSD_PAIRS = SSD_HEADS // 2
PAIRS_PER_GROUP = SSD_PAIRS // SSD_GROUPS
GN = SSD_GROUPS * SSD_STATE


def _log1p_small(e):
    return jnp.where(e < 1e-3, e * (1.0 - e * (0.5 - e / 3.0)), jnp.log(1.0 + e))


def _softplus(v):
    return jnp.maximum(v, 0.0) + _log1p_small(jnp.exp(-jnp.abs(v)))


def _ssd_decay(dt_raw, dtb, alog):
    L = dt_raw.shape[0]
    pre = dt_raw + dtb
    dt = _softplus(pre)
    a = -jnp.exp(alog)
    row = lax.broadcasted_iota(jnp.int32, (L, L), 0)
    col = lax.broadcasted_iota(jnp.int32, (L, L), 1)
    tri = row >= col
    cum = _dot_hi(tri.astype(F32), dt * a)
    return pre, dt, a, tri, cum, cum.T


def _col(m, h):
    return m[:, h:h + 1]


def _pair_sel(m, k, lo):
    return jnp.where(lo, _col(m, 2 * k), _col(m, 2 * k + 1))


def _ssd_specs(S):
    L = SSD_CHUNK
    nc = S // L
    return L, nc


def ssd_fwd(proj, xc, dtb, alog, dchan, normw, S, *, name, hosted=None):
    T = proj.shape[0]
    L, nc = _ssd_specs(S)
    B, W, N = T // S, SSD_WIDTH, SSD_STATE
    wrap, h_in, h_ospecs, h_oshapes, h_scratch = _hosting(hosted, (B, nc), 9, 3, 1)

    def body(xs_ref, bs_ref, cs_ref, dt_ref, z_ref, dtb_ref, alog_ref, dch_ref, nw_ref, y_ref, ys_ref, hin_ref, st):
        @pl.when(pl.program_id(1) == 0)
        def _():
            st[...] = jnp.zeros(st.shape, F32)

        hin_ref[...] = st[...]
        _, dt, a, tri, cum, cum_t = _ssd_decay(dt_ref[...], dtb_ref[...], alog_ref[...])
        last = cum[L - 1:L, :]
        lo = lax.broadcasted_iota(jnp.int32, (1, LANE), 1) < SSD_HEAD_DIM
        for g in range(SSD_GROUPS):
            bm = bs_ref[:, g * N:(g + 1) * N]
            cm = cs_ref[:, g * N:(g + 1) * N]
            bm_t = bm.T
            gmat = _dot_nt(cm, bm)
            for kk in range(PAIRS_PER_GROUP):
                k = g * PAIRS_PER_GROUP + kk
                sl = slice(k * LANE, (k + 1) * LANE)
                xv = xs_ref[:, sl]
                xdt = xv * _pair_sel(dt, k, lo)
                cum_cols = [jnp.broadcast_to(_col(cum, h), (L, LANE)) for h in (2 * k, 2 * k + 1)]
                cum_sel = jnp.where(lo, cum_cols[0], cum_cols[1])
                last_sel = _pair_sel(last, k, lo)
                yd = []
                for j, h in enumerate((2 * k, 2 * k + 1)):
                    gam = jnp.exp(jnp.where(tri, cum_cols[j] - cum_t[h:h + 1, :], -jnp.inf))
                    yd.append(_dot(gmat * gam, xdt))
                hp = st[:, sl]
                y_off = _dot(cm, hp) * jnp.exp(cum_sel)
                y_ref[:, sl] = jnp.where(lo, yd[0], yd[1]) + y_off + xv * dch_ref[:, sl]
                zmat = xdt * jnp.exp(last_sel - cum_sel)
                st[:, sl] = hp * jnp.exp(last_sel) + _dot(bm_t, zmat)
        y = y_ref[...]
        z = z_ref[...]
        yz = y * (z * _sigmoid(z))
        ys_ref[...] = _rms_tile(yz, nw_ref[...]).astype(ys_ref.dtype)

    r = lambda b, c: b * nc + c
    vec = lambda w: pl.BlockSpec((1, w), lambda b, c: (0, 0))
    hbm = pl.BlockSpec(memory_space=pl.ANY)
    return pl.pallas_call(
        wrap(body), name=name, grid=(B, nc),
        in_specs=[pl.BlockSpec((L, W), lambda b, c: (r(b, c), 0)),
                  pl.BlockSpec((L, GN), lambda b, c: (r(b, c), W // GN)),
                  pl.BlockSpec((L, GN), lambda b, c: (r(b, c), W // GN + 1)),
                  pl.BlockSpec((L, LANE), lambda b, c: (r(b, c), PDT0 // LANE)),
                  pl.BlockSpec((L, W), lambda b, c: (r(b, c), PZ0 // W)),
                  vec(LANE), vec(LANE), vec(W), vec(W)] + [hbm] * len(h_in),
        out_specs=[pl.BlockSpec((L, W), lambda b, c: (r(b, c), 0)), pl.BlockSpec((L, W), lambda b, c: (r(b, c), 0)),
                   pl.BlockSpec((N, W), lambda b, c: (r(b, c), 0))] + h_ospecs,
        out_shape=[_sds((T, W), F32), _sds((T, W), MXU_DTYPE), _sds((T // L * N, W), F32)] + h_oshapes,
        scratch_shapes=[pltpu.VMEM((N, W), F32)] + h_scratch,
        compiler_params=_cparams(("arbitrary", "arbitrary") if hosted else ("parallel", "arbitrary")),
    )(xc, xc, xc, proj, proj, dtb, alog, dchan, normw, *h_in)


def ssd_bwd(proj, xc, ypre, hin, dmix, dtb, alog, dchan, normw, S, *, name):
    T = proj.shape[0]
    L, nc = _ssd_specs(S)
    B, W, N = T // S, SSD_WIDTH, SSD_STATE

    def body(xs_ref, bs_ref, cs_ref, dt_ref, z_ref, y_ref, hin_ref, dys_ref, dtb_ref, alog_ref, dch_ref, nw_ref,
             dxc_ref, ddt_ref, dz_ref, sm_ref, dnw_ref, dst):
        step = pl.program_id(0) * nc + pl.program_id(1)

        @pl.when(pl.program_id(1) == 0)
        def _():
            dst[...] = jnp.zeros(dst.shape, F32)

        pre, dt, a, tri, cum, cum_t = _ssd_decay(dt_ref[...], dtb_ref[...], alog_ref[...])
        last = cum[L - 1:L, :]
        e_last = jnp.exp(last)
        lane = lax.broadcasted_iota(jnp.int32, (1, LANE), 1)
        sub = lax.broadcasted_iota(jnp.int32, (LANE, 1), 0)
        lo = lane < SSD_HEAD_DIM
        is_last_row = sub == L - 1
        tri_t = (lax.broadcasted_iota(jnp.int32, (L, L), 0) <= lax.broadcasted_iota(jnp.int32, (L, L), 1))

        y, z, nw = y_ref[...], z_ref[...], nw_ref[...]
        sg = _sigmoid(z)
        gate = z * sg
        dyz, dnw = _rms_bwd_tile(y * gate, nw, dys_ref[...])
        dy_all = dyz * gate
        dz_ref[...] = (dyz * y * (sg * (1.0 + z * (1.0 - sg)))).astype(dz_ref.dtype)

        d_cum = jnp.zeros((L, LANE), F32)
        d_cum_t = jnp.zeros((LANE, L), F32)
        d_dt = jnp.zeros((L, LANE), F32)
        d_dskip = jnp.zeros((1, LANE), F32)
        for g in range(SSD_GROUPS):
            bm = bs_ref[:, g * N:(g + 1) * N]
            cm = cs_ref[:, g * N:(g + 1) * N]
            cm_t = cm.T
            gmat = _dot_nt(cm, bm)
            gmat_t = _dot_nt(bm, cm)
            d_g = jnp.zeros((L, L), F32)
            d_bm = jnp.zeros((L, N), F32)
            d_cm = jnp.zeros((L, N), F32)
            for kk in range(PAIRS_PER_GROUP):
                k = g * PAIRS_PER_GROUP + kk
                sl = slice(k * LANE, (k + 1) * LANE)
                xv = xs_ref[:, sl]
                dyv = dy_all[:, sl]
                dt_sel = _pair_sel(dt, k, lo)
                xdt = xv * dt_sel
                hp = hin_ref[:, sl]
                dh_out = dst[:, sl]
                cum_cols = [jnp.broadcast_to(_col(cum, h), (L, LANE)) for h in (2 * k, 2 * k + 1)]
                cum_sel = jnp.where(lo, cum_cols[0], cum_cols[1])
                last_sel = _pair_sel(last, k, lo)
                e_sel = jnp.exp(cum_sel)
                w_sel = jnp.exp(last_sel - cum_sel)
                e_lane = jnp.exp(last_sel)
                y_off = _dot(cm, hp) * e_sel
                zmat = xdt * w_sel
                d_z = _dot(bm, dh_out)
                d_bm = d_bm + _dot_nt(zmat, dh_out)
                d_xdt = d_z * w_sel
                dw_full = d_z * zmat
                hh = dh_out * hp
                d_r = dyv * e_sel
                d_cm = d_cm + _dot_nt(d_r, hp)
                dst[:, sl] = dh_out * e_lane + _dot(cm_t, d_r)
                dyoff_full = dyv * y_off
                for j, h in enumerate((2 * k, 2 * k + 1)):
                    mine = lo if j == 0 else jnp.logical_not(lo)
                    hot = lane == h
                    dyh = jnp.where(mine, dyv, 0.0)
                    gam = jnp.exp(jnp.where(tri, cum_cols[j] - cum_t[h:h + 1, :], -jnp.inf))
                    gam_t = jnp.exp(jnp.where(tri_t, cum_t[h:h + 1, :] - cum_cols[j], -jnp.inf))
                    mx = gmat * gam
                    d_xdt = d_xdt + _dot(gmat_t * gam_t, dyh)
                    d_mx = jnp.where(tri, _dot_nt(dyh, xdt), 0.0)
                    d_g = d_g + d_mx * gam
                    d_seg = d_mx * mx
                    row_l = (jnp.sum(d_seg, axis=1, keepdims=True)
                             + jnp.sum(jnp.where(mine, dyoff_full - dw_full, 0.0), axis=1, keepdims=True))
                    at_end = (jnp.sum(jnp.where(mine, dw_full, 0.0), keepdims=True)
                              + jnp.sum(jnp.where(mine, hh, 0.0), keepdims=True) * _col(e_last, h))
                    d_cum = d_cum + jnp.where(hot, row_l + jnp.where(is_last_row, at_end, 0.0), 0.0)
                    d_cum_t = d_cum_t - jnp.where(sub == h, jnp.sum(d_seg, axis=0, keepdims=True), 0.0)
                    d_dskip = d_dskip + jnp.where(hot, jnp.sum(jnp.where(mine, dyv * xv, 0.0), keepdims=True), 0.0)
                for j, h in enumerate((2 * k, 2 * k + 1)):
                    mine = lo if j == 0 else jnp.logical_not(lo)
                    d_dt = d_dt + jnp.where(lane == h, jnp.sum(jnp.where(mine, d_xdt * xv, 0.0), axis=1, keepdims=True), 0.0)
                dxc_ref[:, sl] = d_xdt * dt_sel + dyv * dch_ref[:, sl]
            dxc_ref[:, W + g * N:W + (g + 1) * N] = d_bm + _dot_tn(d_g, cm)
            dxc_ref[:, W + GN + g * N:W + GN + (g + 1) * N] = d_cm + _dot(d_g, bm)

        d_cum = d_cum + d_cum_t.T
        d_da = _dot_hi(jnp.logical_not(tri).astype(F32) + (lax.broadcasted_iota(jnp.int32, (L, L), 0)
                                                              == lax.broadcasted_iota(jnp.int32, (L, L), 1)).astype(F32), d_cum)
        d_dt = d_dt + d_da * a
        heads = lane < SSD_HEADS
        d_pre = jnp.where(heads, d_dt * _sigmoid(pre), 0.0)
        ddt_ref[...] = d_pre.astype(ddt_ref.dtype)
        d_alog = jnp.sum(d_da * dt, axis=0, keepdims=True) * a
        part = jnp.concatenate([jnp.where(heads, d_alog, 0.0), jnp.sum(d_pre, axis=0, keepdims=True), d_dskip,
                                jnp.zeros((5, LANE), F32)], axis=0)

        @pl.when(step == 0)
        def _():
            sm_ref[...] = part
            dnw_ref[...] = dnw

        @pl.when(step > 0)
        def _():
            sm_ref[...] += part
            dnw_ref[...] += dnw

    r = lambda b, c: b * nc + (nc - 1 - c)
    vec = lambda w: pl.BlockSpec((1, w), lambda b, c: (0, 0))
    blk = lambda w, j: pl.BlockSpec((L, w), lambda b, c: (r(b, c), j))
    return pl.pallas_call(
        body, name=name, grid=(B, nc),
        in_specs=[blk(W, 0), blk(GN, W // GN), blk(GN, W // GN + 1), blk(LANE, PDT0 // LANE), blk(W, PZ0 // W),
                  blk(W, 0), pl.BlockSpec((N, W), lambda b, c: (r(b, c), 0)), blk(W, 0),
                  vec(LANE), vec(LANE), vec(W), vec(W)],
        out_specs=[blk(SSD_CONV_CH, 0), blk(LANE, 0), blk(W, 0), pl.BlockSpec((8, LANE), lambda b, c: (0, 0)), vec(W)],
        out_shape=[_sds((T, SSD_CONV_CH), F32), _sds((T, LANE), MXU_DTYPE), _sds((T, W), MXU_DTYPE),
                   _sds((8, LANE), F32), _sds((1, W), F32)],
        scratch_shapes=[pltpu.VMEM((N, W), F32)],
        compiler_params=_cparams(("arbitrary", "arbitrary")),
    )(xc, xc, xc, proj, proj, ypre, hin, dmix, dtb, alog, dchan, normw)


def _adamw_math(w, g, m, v):
    m = ADAM_B1 * m + (1.0 - ADAM_B1) * g
    v = ADAM_B2 * v + (1.0 - ADAM_B2) * (g * g)
    m_hat = m / (1.0 - ADAM_B1 ** ADAM_STEP)
    v_hat = v / (1.0 - ADAM_B2 ** ADAM_STEP)
    delta = -ADAM_LR * (m_hat / (jnp.sqrt(v_hat) + ADAM_EPS) + ADAM_WD * w)
    return delta, m, v


def adamw_layers(w, g_layers, m, v, *, name, tr=256):
    L, A, B = w.shape
    tr = _pick(A, (tr, 192, 176, 128, 64, 32, 16, 8))
    na = A // tr
    n = len(g_layers[0])

    def body(*refs):
        w_ref, m_ref, v_ref = refs[0], refs[1 + L * n], refs[2 + L * n]
        g_ref, d_ref, nm_ref, nv_ref = refs[3 + L * n:]
        layer = pl.program_id(0)
        g = None
        for l in range(L):
            parts = refs[1 + l * n:1 + (l + 1) * n]
            gl = parts[0][...]
            for p in parts[1:]:
                gl = gl + p[...]
            g = gl if g is None else jnp.where(layer == l, gl, g)
        g_ref[...] = g
        d_ref[...], nm_ref[...], nv_ref[...] = _adamw_math(w_ref[...], g, m_ref[...], v_ref[...])

    def g_spec(l):
        return pl.BlockSpec((tr, B), lambda layer, i: (jnp.where(layer == l, i, jnp.where(layer < l, 0, na - 1)), 0))

    spec = pl.BlockSpec((None, tr, B), lambda layer, i: (layer, i, 0))
    return pl.pallas_call(
        body, name=name, grid=(L, na), in_specs=[spec] + [g_spec(l) for l in range(L) for _ in range(n)] + [spec] * 2,
        out_specs=[spec] * 4, out_shape=[_sds((L, A, B), F32)] * 4, compiler_params=_cparams(("arbitrary", "arbitrary")),
    )(w, *[p for parts in g_layers for p in parts], m, v)


def adamw_small(ws, gs, ms, vs, *, name):
    n = len(ws)

    def body(*refs):
        w_refs, g_refs, m_refs, v_refs = (refs[i * n:(i + 1) * n] for i in range(4))
        d_refs, nm_refs, nv_refs = (refs[(4 + i) * n:(5 + i) * n] for i in range(3))
        for a in range(n):
            d_refs[a][...], nm_refs[a][...], nv_refs[a][...] = _adamw_math(
                w_refs[a][...], g_refs[a][...], m_refs[a][...], v_refs[a][...])

    vm = pl.BlockSpec(memory_space=pltpu.VMEM)
    out = pl.pallas_call(
        body, name=name, in_specs=[vm] * (4 * n), out_specs=[vm] * (3 * n),
        out_shape=[_sds(w.shape, F32) for w in ws] * 3, compiler_params=pltpu.CompilerParams(vmem_limit_bytes=VMEM_LIMIT),
    )(*ws, *gs, *ms, *vs)
    return out[:n], out[n:2 * n], out[2 * n:]


def _my_place():
    return lax.axis_index("x"), lax.axis_index("y"), lax.axis_index("c")


def _other_chips(x, y):
    return [(1 - x, y), (x, 1 - y), (1 - x, 1 - y)]


def relation_of(chip, me):
    d = chip ^ me
    return jnp.where(d == 2, 0, jnp.where(d == 1, 1, jnp.where(d == 3, 2, -1)))


class Exchange(NamedTuple):
    inputs: tuple
    out_shapes: tuple
    sems: tuple
    start: Callable
    relay: Callable
    finish: Callable


def scatter_exchange(srcs):
    n = len(srcs)

    def copies(ins, outs, sems):
        x, y, c = _my_place()
        out = []
        for k, (px, py) in enumerate(_other_chips(x, y)):
            for a in range(n):
                out.append(pltpu.make_async_remote_copy(
                    src_ref=ins[a].at[2 * px + py], dst_ref=outs[a].at[k], send_sem=sems[0].at[k, a],
                    recv_sem=sems[1].at[k, a], device_id=(px, py, c), device_id_type=pl.DeviceIdType.MESH))
        return out

    def start(ins, outs, sems):
        for cp in copies(ins, outs, sems):
            cp.start()

    def finish(ins, outs, sems):
        cps = copies(ins, outs, sems)
        for cp in cps:
            cp.wait_recv()
        for cp in cps:
            cp.wait_send()

    return Exchange(tuple(srcs), tuple(_sds((3,) + s.shape[1:], s.dtype) for s in srcs),
                    (pltpu.SemaphoreType.DMA((3, n)),) * 2, start, lambda *a: None, finish)


def run_exchange(ex, *, name):
    n_in, n_out = len(ex.inputs), len(ex.out_shapes)

    def body(*refs):
        parts = refs[:n_in], refs[n_in:n_in + n_out], refs[n_in + n_out:]
        ex.start(*parts)
        ex.relay(*parts)
        ex.finish(*parts)

    hbm = pl.BlockSpec(memory_space=pl.ANY)
    return pl.pallas_call(
        body, name=name, in_specs=[hbm] * n_in, out_specs=[hbm] * n_out, out_shape=list(ex.out_shapes),
        scratch_shapes=list(ex.sems), compiler_params=pltpu.CompilerParams(has_side_effects=True),
    )(*ex.inputs)


def sibling_swap(srcs, *, name):
    n = len(srcs)

    def body(*refs):
        src_refs, out_refs, (send_sems, recv_sems) = refs[:n], refs[n:2 * n], refs[2 * n:]
        x, y, c = _my_place()
        copies = [pltpu.make_async_remote_copy(
            src_ref=src_refs[a], dst_ref=out_refs[a], send_sem=send_sems.at[a], recv_sem=recv_sems.at[a],
            device_id=(x, y, 1 - c), device_id_type=pl.DeviceIdType.MESH) for a in range(n)]
        for cp in copies:
            cp.start()
        for cp in copies:
            cp.wait_recv()
        for cp in copies:
            cp.wait_send()

    hbm = pl.BlockSpec(memory_space=pl.ANY)
    return pl.pallas_call(
        body, name=name, in_specs=[hbm] * n, out_specs=[hbm] * n, out_shape=[_sds(s.shape, s.dtype) for s in srcs],
        scratch_shapes=[pltpu.SemaphoreType.DMA((n,)), pltpu.SemaphoreType.DMA((n,))],
        compiler_params=pltpu.CompilerParams(has_side_effects=True),
    )(*srcs)


def gather_exchange(srcs):
    nch = len(srcs)
    halves = [s.shape[0] // 2 for s in srcs]
    assert all(2 * h == s.shape[0] and h % 16 == 0 for h, s in zip(halves, srcs))
    pieces = [(k, q) for k in range(3) for q in range(nch)]

    def makers(ins, outs, sems):
        ici_send, ici_recv, d2d_send, d2d_recv = sems
        x, y, c = _my_place()
        peers = _other_chips(x, y)

        def rows(core, q):
            return pl.ds(core * halves[q], halves[q])

        def ici(k, q):
            px, py = peers[k]
            return pltpu.make_async_remote_copy(
                src_ref=ins[q].at[rows(c, q)], dst_ref=outs[q].at[k, rows(c, q)], send_sem=ici_send.at[k, q],
                recv_sem=ici_recv.at[k, q], device_id=(px, py, c), device_id_type=pl.DeviceIdType.MESH)

        def d2d(k, q, core):
            return pltpu.make_async_remote_copy(
                src_ref=outs[q].at[k, rows(core, q)], dst_ref=outs[q].at[k, rows(core, q)],
                send_sem=d2d_send.at[k, q], recv_sem=d2d_recv.at[k, q], device_id=(x, y, 1 - c),
                device_id_type=pl.DeviceIdType.MESH)

        return ici, d2d, c

    def start(*refs):
        ici, _, _ = makers(*refs)
        for k, q in pieces:
            ici(k, q).start()

    def relay(*refs):
        ici, d2d, c = makers(*refs)
        for k, q in pieces:
            ici(k, q).wait_recv()
            d2d(k, q, c).start()

    def finish(*refs):
        ici, d2d, c = makers(*refs)
        for k, q in pieces:
            d2d(k, q, 1 - c).wait_recv()
        for k, q in pieces:
            ici(k, q).wait_send()
            d2d(k, q, c).wait_send()

    return Exchange(tuple(srcs), tuple(_sds((3,) + s.shape, s.dtype) for s in srcs),
                    (pltpu.SemaphoreType.DMA((3, nch)),) * 4, start, relay, finish)


def all_sum_small(vec, *, name):
    R, C = vec.shape

    def body(v_ref, out_ref, buf, send_sems, recv_sems):
        x, y, c = _my_place()
        me = 4 * x + 2 * y + c
        buf[me] = v_ref[...]
        copies = []
        for k in range(1, N_DEV):
            px, py, pc = x ^ (k >> 2), y ^ ((k >> 1) & 1), c ^ (k & 1)
            copies.append(pltpu.make_async_remote_copy(
                src_ref=v_ref, dst_ref=buf.at[me], send_sem=send_sems.at[k - 1], recv_sem=recv_sems.at[k - 1],
                device_id=(px, py, pc), device_id_type=pl.DeviceIdType.MESH))
        for cp in copies:
            cp.start()
        for k in range(1, N_DEV):
            px, py, pc = x ^ (k >> 2), y ^ ((k >> 1) & 1), c ^ (k & 1)
            pltpu.make_async_remote_copy(
                src_ref=v_ref, dst_ref=buf.at[4 * px + 2 * py + pc], send_sem=send_sems.at[k - 1],
                recv_sem=recv_sems.at[k - 1], device_id=(px, py, pc), device_id_type=pl.DeviceIdType.MESH).wait_recv()
        for cp in copies:
            cp.wait_send()
        acc = buf[0]
        for d in range(1, N_DEV):
            acc = acc + buf[d]
        out_ref[...] = acc

    return pl.pallas_call(
        body, name=name, in_specs=[pl.BlockSpec(memory_space=pltpu.VMEM)], out_specs=pl.BlockSpec(memory_space=pltpu.VMEM),
        out_shape=_sds((R, C), F32),
        scratch_shapes=[pltpu.VMEM((N_DEV, R, C), F32), pltpu.SemaphoreType.DMA((N_DEV - 1,)),
                        pltpu.SemaphoreType.DMA((N_DEV - 1,))],
        compiler_params=pltpu.CompilerParams(has_side_effects=True, vmem_limit_bytes=VMEM_LIMIT),
    )(vec)


def sum_chips(own, others, *, name, tr=512):
    R, C = own.shape
    tr = _pick(R, (tr, 384, 352, 256, 128, 64, 32, 16))

    def body(o_ref, p_ref, s_ref):
        acc = o_ref[...].astype(F32)
        for k in range(3):
            acc = acc + p_ref[k].astype(F32)
        s_ref[...] = acc

    return pl.pallas_call(
        body, name=name, grid=(R // tr,),
        in_specs=[pl.BlockSpec((tr, C), lambda i: (i, 0)), pl.BlockSpec((3, tr, C), lambda i: (0, i, 0))],
        out_specs=pl.BlockSpec((tr, C), lambda i: (i, 0)), out_shape=_sds((R, C), F32),
        compiler_params=_cparams(("parallel",)),
    )(own, others)


WEIGHTS = ['attn_norm', 'w_in', 'ssd_conv_w', 'ssd_conv_b', 'ssd_dt_bias', 'ssd_a_log', 'ssd_d', 'ssd_norm', 'pool_w',
           'pool_scale', 'mla_q_norm', 'mla_w_uq', 'mla_kv_norm', 'mla_w_ukv', 'w_out', 'ffn_norm', 'ffn_w_up',
           'ffn_conv_w', 'ffn_conv_b', 'ffn_w_down', 'final_norm']
BIG = {'w_in': 2, 'mla_w_uq': 2, 'mla_w_ukv': 2, 'w_out': 1, 'ffn_w_up': 2, 'ffn_w_down': 1}
CONV_SHARDED = ('ssd_conv_w', 'ffn_conv_w')


def _zeros_cols(w, n):
    return jnp.zeros((w.shape[0], n), w.dtype)


def _w_in_to_padded(w):
    return jnp.concatenate([w[:, 0:2560], w[:, 2576:3088], w[:, 3088:3472], w[:, 2560:2576], _zeros_cols(w, 112),
                            w[:, 3472:3728], _zeros_cols(w, 64), w[:, 3728:3760], _zeros_cols(w, 32 + 128)], axis=1)


def _w_in_from_padded(g):
    return jnp.concatenate([g[:, 0:2560], g[:, PDT0:PDT0 + SSD_HEADS], g[:, PU0:PU0 + POOL_WIDTH],
                            g[:, PCQ0:PCQ0 + MLA_Q_RANK], g[:, PCKV0:PCKV0 + MLA_KV_RANK],
                            g[:, PKPE0 + ROPE0:PKPE0 + ROPE0 + MLA_ROPE]], axis=1)


def _w_uq_to_padded(w):
    r = w.reshape(MLA_Q_RANK, MLA_HEADS, MLA_QK)
    return jnp.pad(r, ((0, 0), (0, 0), (0, HEAD_W - MLA_QK))).reshape(MLA_Q_RANK, MLA_PAD)


def _w_uq_from_padded(g):
    return g.reshape(MLA_Q_RANK, MLA_HEADS, HEAD_W)[:, :, :MLA_QK].reshape(MLA_Q_RANK, MLA_HEADS * MLA_QK)


def _w_ukv_to_padded(w):
    r = w.reshape(MLA_KV_RANK, MLA_HEADS, MLA_NOPE + MLA_V)
    pad = lambda t: jnp.pad(t, ((0, 0), (0, 0), (0, HEAD_W - t.shape[2]))).reshape(MLA_KV_RANK, MLA_PAD)
    return jnp.concatenate([pad(r[:, :, :MLA_NOPE]), pad(r[:, :, MLA_NOPE:])], axis=1)


def _w_ukv_from_padded(g):
    kk = g[:, :MLA_PAD].reshape(MLA_KV_RANK, MLA_HEADS, HEAD_W)[:, :, :MLA_NOPE]
    vv = g[:, MLA_PAD:].reshape(MLA_KV_RANK, MLA_HEADS, HEAD_W)[:, :, :MLA_V]
    return jnp.concatenate([kk, vv], axis=2).reshape(MLA_KV_RANK, MLA_HEADS * (MLA_NOPE + MLA_V))


def _w_out_to_padded(w):
    att = w[SSD_WIDTH + POOL_WIDTH:].reshape(MLA_HEADS, MLA_V, D_MODEL)
    att = jnp.pad(att, ((0, 0), (0, HEAD_W - MLA_V), (0, 0))).reshape(MLA_PAD, D_MODEL)
    return jnp.concatenate([w[:SSD_WIDTH + POOL_WIDTH], att], axis=0)


def _w_out_from_padded(g):
    att = g[SSD_WIDTH + POOL_WIDTH:].reshape(MLA_HEADS, HEAD_W, D_MODEL)[:, :MLA_V].reshape(MLA_WIDTH, D_MODEL)
    return jnp.concatenate([g[:SSD_WIDTH + POOL_WIDTH], att], axis=0)


def _pad_lanes(v, n=LANE):
    return jnp.pad(v.reshape(1, -1), ((0, 0), (0, n - v.size)))


def _pack_rows(parts, cols, dtype, row_multiple=16):
    flat = jnp.concatenate([p.astype(dtype).reshape(-1) for p in parts])
    rows = -(-flat.size // (cols * row_multiple)) * row_multiple
    return jnp.pad(flat, (0, rows * cols - flat.size)).reshape(rows, cols)


def _unpack_rows(packed, shapes):
    flat = packed.reshape(-1)
    out, at = [], 0
    for s in shapes:
        n = math.prod(s)
        out.append(flat[at:at + n].reshape(s))
        at += n
    return out


def _split_for_chips(g, axis):
    a, b = g.shape
    if axis == 0:
        return g.reshape(N_CHIPS, a // N_CHIPS, b)
    return g.reshape(a, N_CHIPS, b // N_CHIPS).transpose(1, 0, 2)


_MATMUL_OPERANDS = {'w_in': ('w_in_p', _w_in_to_padded), 'mla_w_uq': ('w_uq_p', _w_uq_to_padded),
                    'mla_w_ukv': ('w_ukv_p', _w_ukv_to_padded), 'w_out': ('w_out_p', _w_out_to_padded),
                    'ffn_w_up': ('w_up', lambda a: a), 'ffn_w_down': ('w_down', lambda a: a)}


def _matmul_weights(full):
    return {_MATMUL_OPERANDS[k][0]: _MATMUL_OPERANDS[k][1](a) for k, a in full.items()}


def _layer_weights(full, small, l):
    w = _matmul_weights(full)
    for k in ('attn_norm', 'ssd_conv_w', 'ssd_conv_b', 'ssd_norm', 'pool_w', 'pool_scale', 'mla_q_norm', 'mla_kv_norm',
              'ffn_norm', 'ffn_conv_w', 'ffn_conv_b'):
        w[k] = small[k][l]
    w['dtb'] = _pad_lanes(small['ssd_dt_bias'][l])
    w['alog'] = _pad_lanes(small['ssd_a_log'][l])
    w['dchan'] = jnp.repeat(small['ssd_d'][l], SSD_HEAD_DIM).reshape(1, SSD_WIDTH)
    w['ssd_norm'] = w['ssd_norm'].reshape(1, SSD_WIDTH)
    return w


def _layer_fwd(x, pos, invf, w, S, l, hosted=None, scan_hosted=None, late_weights=None):
    n = lambda s: f"{s}_l{l}"
    h1 = rmsnorm_fwd(x, w['attn_norm'], name=n("attn_norm"))
    proj = matmul(h1, w['w_in_p'], name=n("w_in"))
    xc = ssd_conv_fwd(proj, w['ssd_conv_w'], w['ssd_conv_b'], S, name=n("ssd_conv"))
    ypre, yssd, hin, *arrived = ssd_fwd(proj, xc, w['dtb'], w['alog'], w['dchan'], w['ssd_norm'], S, name=n("ssd_scan"),
                                        hosted=scan_hosted)
    if late_weights:
        w = {**w, **late_weights(arrived)}
    ypool, pooled = pool_fwd(proj, w['pool_w'], w['pool_scale'], S, name=n("pool"))
    q, k, v, cqn, ckvn = mla_prep_fwd(proj, pos, invf, w['mla_q_norm'], w['w_uq_p'], w['mla_kv_norm'], w['w_ukv_p'],
                                      name=n("mla_prep"))
    o, lse, *exchanged = flash_fwd(q, k, v, S, name=n("attention"), hosted=hosted)
    mix = jnp.concatenate([yssd, ypool, o.astype(MXU_DTYPE)], axis=1)
    x2 = matmul(mix, w['w_out_p'], res=x, name=n("w_out"))
    h2 = rmsnorm_fwd(x2, w['ffn_norm'], name=n("ffn_norm"))
    up = matmul(h2, w['w_up'], name=n("ffn_up"))
    act = ffn_conv_gate_fwd(up, w['ffn_conv_w'], w['ffn_conv_b'], S, name=n("ffn_conv_gate"))
    x3 = matmul(act, w['w_down'], res=x2, name=n("ffn_down"))
    saved = dict(x=x, h1=h1, proj=proj, xc=xc, ypre=ypre, hin=hin, pooled=pooled, q=q, k=k, v=v, cqn=cqn, ckvn=ckvn,
                 o=o, lse=lse, mix=mix, x2=x2, h2=h2, up=up, act=act)
    return x3, saved, w, exchanged


def _layer_bwd(dx3, pos, invf, w, s, S, l, host=None, late_host=None):
    n = lambda t: f"{t}_l{l}"
    g = {}
    dact = matmul(dx3, w['w_down'], nt=True, name=n("d_ffn_down"))
    g['ffn_w_down'] = matmul_tn(s['act'], dx3, name=n("g_ffn_down"))
    dup_g, dup_v, st = ffn_conv_gate_bwd(s['up'], w['ffn_conv_w'], w['ffn_conv_b'], dact, S, name=n("d_ffn_conv_gate"))
    g['ffn_conv_w'], g['ffn_conv_b'] = st[:FFN_CONV], st[FFN_CONV]
    dh2 = matmul(dup_g, w['w_up'], nt=True, kblock=0, name=n("d_ffn_up_g"))
    dh2 = matmul(dup_v, w['w_up'], nt=True, kblock=1, res=dh2, name=n("d_ffn_up_v"))
    g['ffn_w_up'] = jnp.concatenate([matmul_tn(s['h2'], dup_g, name=n("g_ffn_up_g")),
                                     matmul_tn(s['h2'], dup_v, name=n("g_ffn_up_v"))], axis=1)
    dx2, gn = rmsnorm_bwd(s['x2'], w['ffn_norm'], dh2, dx3, name=n("d_ffn_norm"))
    g['ffn_norm'] = gn[0]
    dmix = matmul(dx2, w['w_out_p'], nt=True, name=n("d_w_out"))
    g['w_out'] = _w_out_from_padded(matmul_tn(s['mix'], dx2, name=n("g_w_out")))
    dxc, ddt, dz, sm, gsn = ssd_bwd(s['proj'], s['xc'], s['ypre'], s['hin'], dmix, w['dtb'], w['alog'], w['dchan'],
                                    w['ssd_norm'], S, name=n("d_ssd_scan"))
    g['ssd_a_log'], g['ssd_dt_bias'], g['ssd_d'] = sm[0, :SSD_HEADS], sm[1, :SSD_HEADS], sm[2, :SSD_HEADS]
    g['ssd_norm'] = gsn[0]
    dxbc, st = ssd_conv_bwd(s['proj'], w['ssd_conv_w'], w['ssd_conv_b'], dxc, S, name=n("d_ssd_conv"))
    g['ssd_conv_w'], g['ssd_conv_b'] = st[:SSD_CONV], st[SSD_CONV]
    du, g['pool_w'], gps = pool_bwd(dmix, s['pooled'], w['pool_w'], w['pool_scale'], S, name=n("d_pool"))
    g['pool_scale'] = gps[0]
    dq, dk, dv, *exchanged = flash_bwd(s['q'], s['k'], s['v'], s['o'], s['lse'], dmix, S, name=n("d_attention"),
                                       hosted=host(g) if host else None)
    dqp, dkvp, dcq, dckv, dkpe, gqn, gkn = mla_prep_bwd(s['proj'], pos, invf, w['mla_q_norm'], w['w_uq_p'],
                                                        w['mla_kv_norm'], w['w_ukv_p'], dq, dk, dv, name=n("d_mla_prep"))
    g['mla_q_norm'], g['mla_kv_norm'] = gqn[0], gkn[0]
    g['mla_w_uq'] = _w_uq_from_padded(matmul_tn(s['cqn'], dqp, name=n("g_w_uq")))
    g['mla_w_ukv'] = _w_ukv_from_padded(matmul_tn(s['ckvn'], dkvp, name=n("g_w_ukv")))
    dproj = jnp.concatenate([dz, dxbc, du, dcq, ddt, dckv, dkpe, jnp.zeros_like(dkpe)], axis=1)
    g['w_in'] = _w_in_from_padded(matmul_tn(s['h1'], dproj, name=n("g_w_in")))
    dh1 = matmul(dproj, w['w_in_p'], nt=True, name=n("d_w_in"), hosted=late_host(g) if late_host else None)
    dh1, late_exchanged = (dh1[0], dh1[1:]) if late_host else (dh1, [])
    dx, gn = rmsnorm_bwd(s['x'], w['attn_norm'], dh1, dx2, name=n("d_attn_norm"))
    g['attn_norm'] = gn[0]
    return dx, g, exchanged, late_exchanged


def _rope_inputs(positions):
    pos = positions.reshape(-1, 1).astype(F32)
    inv_freq = ROPE_THETA ** (-jnp.arange(0, MLA_ROPE, 2, dtype=F32) / MLA_ROPE)
    invf = jnp.concatenate([jnp.zeros((ROPE0,), F32), inv_freq, inv_freq,
                            jnp.zeros((HEAD_W - ROPE0 - MLA_ROPE,), F32)]).reshape(1, HEAD_W)
    return pos, invf


EARLY_GRADS = ('w_out', 'ffn_w_up', 'ffn_w_down')


def kernel(x, positions, attn_norm, w_in, ssd_conv_w, ssd_conv_b, ssd_dt_bias, ssd_a_log, ssd_d, ssd_norm, pool_w, pool_scale, mla_q_norm, mla_w_uq, mla_kv_norm, mla_w_ukv, w_out, ffn_norm, ffn_w_up, ffn_conv_w, ffn_conv_b, ffn_w_down, final_norm, loss_target, m_attn_norm, m_w_in, m_ssd_conv_w, m_ssd_conv_b, m_ssd_dt_bias, m_ssd_a_log, m_ssd_d, m_ssd_norm, m_pool_w, m_pool_scale, m_mla_q_norm, m_mla_w_uq, m_mla_kv_norm, m_mla_w_ukv, m_w_out, m_ffn_norm, m_ffn_w_up, m_ffn_conv_w, m_ffn_conv_b, m_ffn_w_down, m_final_norm, v_attn_norm, v_w_in, v_ssd_conv_w, v_ssd_conv_b, v_ssd_dt_bias, v_ssd_a_log, v_ssd_d, v_ssd_norm, v_pool_w, v_pool_scale, v_mla_q_norm, v_mla_w_uq, v_mla_kv_norm, v_mla_w_ukv, v_w_out, v_ffn_norm, v_ffn_w_up, v_ffn_conv_w, v_ffn_conv_b, v_ffn_w_down, v_final_norm):
    wv = dict(zip(WEIGHTS, (attn_norm, w_in, ssd_conv_w, ssd_conv_b, ssd_dt_bias, ssd_a_log, ssd_d, ssd_norm, pool_w,
                            pool_scale, mla_q_norm, mla_w_uq, mla_kv_norm, mla_w_ukv, w_out, ffn_norm, ffn_w_up,
                            ffn_conv_w, ffn_conv_b, ffn_w_down, final_norm)))
    mv = dict(zip(WEIGHTS, (m_attn_norm, m_w_in, m_ssd_conv_w, m_ssd_conv_b, m_ssd_dt_bias, m_ssd_a_log, m_ssd_d,
                            m_ssd_norm, m_pool_w, m_pool_scale, m_mla_q_norm, m_mla_w_uq, m_mla_kv_norm, m_mla_w_ukv,
                            m_w_out, m_ffn_norm, m_ffn_w_up, m_ffn_conv_w, m_ffn_conv_b, m_ffn_w_down, m_final_norm)))
    vv = dict(zip(WEIGHTS, (v_attn_norm, v_w_in, v_ssd_conv_w, v_ssd_conv_b, v_ssd_dt_bias, v_ssd_a_log, v_ssd_d,
                            v_ssd_norm, v_pool_w, v_pool_scale, v_mla_q_norm, v_mla_w_uq, v_mla_kv_norm, v_mla_w_ukv,
                            v_w_out, v_ffn_norm, v_ffn_w_up, v_ffn_conv_w, v_ffn_conv_b, v_ffn_w_down, v_final_norm)))
    Bl, S, D = x.shape
    chip = 2 * lax.axis_index("x") + lax.axis_index("y")
    core = lax.axis_index("c")

    big_names = list(BIG)
    first_names = ['w_in']
    rest_names = [k for k in big_names if k not in first_names]

    def shards(l, names):
        return [wv[k][l].astype(MXU_DTYPE) for k in names]

    def whole_weights(names, own, others):
        rel = [relation_of(j, chip) for j in range(N_CHIPS)]
        return {k: jnp.concatenate(
            [jnp.where(r < 0, mine, jnp.where(r == 0, theirs[0], jnp.where(r == 1, theirs[1], theirs[2]))) for r in rel],
            axis=BIG[k] - 1) for k, mine, theirs in zip(names, own, others)}

    first_others = run_exchange(gather_exchange(shards(0, first_names)), name="gather_w_in_l0")
    placed = []
    for k in CONV_SHARDED:
        sh = wv[k]
        whole = jnp.zeros(sh.shape[:-1] + (sh.shape[-1] * N_CHIPS,), F32)
        whole = lax.dynamic_update_slice_in_dim(whole, sh, chip * sh.shape[-1], axis=sh.ndim - 1)
        placed.append(jnp.where(core == 1, whole, 0.0))
    conv_full = _unpack_rows(all_sum_small(_pack_rows(placed, LANE, F32), name="gather_conv_weights"),
                             [p.shape for p in placed])
    small = {k: wv[k] for k in WEIGHTS if k not in BIG}
    small.update(dict(zip(CONV_SHARDED, conv_full)))

    T = Bl * S
    pos, invf = _rope_inputs(positions)
    group_a = [(k, 1) for k in big_names] + [(k, 0) for k in EARLY_GRADS]
    group_b = [(k, 0) for k in big_names if k not in EARLY_GRADS]

    def scatter_of(group, layer_grads):
        send = [_split_for_chips(layer_grads[l][k], BIG[k] - 1) for k, l in group]
        return send, scatter_exchange(send)

    layer_grads = [None] * DEPTH
    sent = {}
    w0 = _layer_weights(whole_weights(first_names, shards(0, first_names), first_others), small, 0)
    h, saved0, w0, others1 = _layer_fwd(
        x.reshape(T, D), pos, invf, w0, S, 0, hosted=gather_exchange(shards(1, big_names)),
        scan_hosted=gather_exchange(shards(0, rest_names)),
        late_weights=lambda got: _matmul_weights(whole_weights(rest_names, shards(0, rest_names), got)))
    w1 = _layer_weights(whole_weights(big_names, shards(1, big_names), others1), small, 1)
    h, saved1, w1, _ = _layer_fwd(h, pos, invf, w1, S, 1)
    loss, dh, g_final_norm = final_loss(h, small['final_norm'], loss_target.reshape(T, D))
    dh, layer_grads[1], _, _ = _layer_bwd(dh, pos, invf, w1, saved1, S, 1)

    def host_a(early):
        layer_grads[0] = early
        sent['a'], ex = scatter_of(group_a, layer_grads)
        return ex

    def host_b(_):
        sent['b'], ex = scatter_of(group_b, layer_grads)
        return ex

    dx, layer_grads[0], others_a, others_b = _layer_bwd(dh, pos, invf, w0, saved0, S, 0, host=host_a, late_host=host_b)
    small_names = [k for k in WEIGHTS if k not in BIG]
    grads = {k: jnp.stack([layer_grads[l][k] for l in range(DEPTH)]) for k in small_names if k != 'final_norm'}
    grads['final_norm'] = g_final_norm[0]

    pieces = [{}, {}]
    for tag, group, others in (('a', group_a, others_a), ('b', group_b, others_b)):
        mine = [sum_chips(lax.dynamic_index_in_dim(s, chip, 0, keepdims=False), o, name=f"sum_chips_{k}_l{l}")
                for s, o, (k, l) in zip(sent[tag], others, group)]
        theirs = sibling_swap(mine, name=f"swap_core_sums_{tag}")
        pieces[0].update(dict(zip(group, mine)))
        pieces[1].update(dict(zip(group, theirs)))
    small_sum = all_sum_small(_pack_rows([grads[k] for k in small_names] + [loss[0, :1]], LANE, F32), name="sum_small_grads")
    summed = _unpack_rows(small_sum, [grads[k].shape for k in small_names] + [(1,)])
    loss_total = summed[-1].reshape(())
    g_small = dict(zip(small_names, summed[:-1]))
    for k in CONV_SHARDED:
        n = wv[k].shape[-1]
        g_small[k] = lax.dynamic_slice_in_dim(g_small[k], chip * n, n, axis=g_small[k].ndim - 1)

    out_g, out_d, out_m, out_v = {}, {}, {}, {}
    for k in big_names:
        g_layers = [[pieces[0][(k, l)], pieces[1][(k, l)]] for l in range(DEPTH)]
        out_g[k], out_d[k], out_m[k], out_v[k] = adamw_layers(wv[k], g_layers, mv[k], vv[k], name=f"adamw_{k}")
    at_least_2d = lambda a: a.reshape(1, -1) if a.ndim == 1 else a
    res = adamw_small(*[[at_least_2d(d[k]) for k in small_names] for d in (wv, g_small, mv, vv)], name="adamw_small")
    out_g.update(g_small)
    for dst, r in zip((out_d, out_m, out_v), res):
        dst.update({k: a.reshape(wv[k].shape) for k, a in zip(small_names, r)})
    return (loss_total, dx.reshape(Bl, S, D), *[out_g[k] for k in WEIGHTS], *[out_d[k] for k in WEIGHTS],
            *[out_m[k] for k in WEIGHTS], *[out_v[k] for k in WEIGHTS])
```

```python
import functools
import math
from typing import Callable, NamedTuple

import jax
import jax.numpy as jnp
from jax import lax
from jax.experimental import pallas as pl
from jax.experimental.pallas import tpu as pltpu

F32 = jnp.float32
MXU_DTYPE = jnp.bfloat16
HI = lax.Precision.HIGHEST

D_MODEL = 1024
DEPTH = 2
EPS = 1e-6
SSD_HEADS = 16
SSD_HEAD_DIM = 64
SSD_WIDTH = 1024
SSD_GROUPS = 2
SSD_STATE = 128
SSD_CONV = 4
SSD_CHUNK = 128
SSD_CONV_CH = 1536
POOL_GROUPS = 4
POOL_GROUP_DIM = 128
POOL_WIDTH = 512
POOL_WINDOWS = (2, 4, 8, 16)
MLA_HEADS = 8
MLA_Q_RANK = 384
MLA_KV_RANK = 256
MLA_NOPE = 64
MLA_ROPE = 32
MLA_V = 64
MLA_QK = 96
MLA_WIDTH = 512
ROPE_THETA = 10000.0
MIX_WIDTH = 2048
IN_COLS = 3760
D_FF = 2816
FFN_CONV = 3
ADAM_LR = 0.001
ADAM_B1 = 0.9
ADAM_B2 = 0.999
ADAM_EPS = 1e-08
ADAM_WD = 0.01
ADAM_STEP = 10

LANE = 128
HALO = 8
POOL_HALO = 16
PZ0 = 0
PXBC0 = 1024
PU0 = 2560
PCQ0 = 3072
PDT0 = 3456
PCKV0 = 3584
PKPE0 = 3840
PROJ_W = 4096
HEAD_W = 128
MLA_PAD = MLA_HEADS * HEAD_W
MIXP = SSD_WIDTH + POOL_WIDTH + MLA_PAD
N_CHIPS = 4
N_DEV = 8
VMEM_LIMIT = 56 * 1024 * 1024


def _cparams(dims, vmem=None):
    return pltpu.CompilerParams(dimension_semantics=dims, vmem_limit_bytes=vmem or VMEM_LIMIT)


def _sds(shape, dtype):
    return jax.ShapeDtypeStruct(tuple(shape), dtype)


def _mx(v):
    return v.astype(MXU_DTYPE)


def _dot(a, b):
    return jnp.dot(_mx(a), _mx(b), preferred_element_type=F32)


def _dot_nt(a, b):
    return lax.dot_general(_mx(a), _mx(b), (((1,), (1,)), ((), ())), preferred_element_type=F32)


def _dot_tn(a, b):
    return lax.dot_general(_mx(a), _mx(b), (((0,), (0,)), ((), ())), preferred_element_type=F32)


def _dot_hi(a, b):
    return jnp.dot(a, b, preferred_element_type=F32, precision=HI)


def _sigmoid(v):
    return 1.0 / (1.0 + jnp.exp(-v))


def _pick(n, prefs):
    for p in prefs:
        if n % p == 0:
            return p
    return n


def matmul(a, b, *, res=None, out_dtype=F32, name, nt=False, kblock=0, tm=None, tn=None, hosted=None):
    M, K = a.shape
    N = b.shape[0] if nt else b.shape[1]
    assert (b.shape[1] % K == 0) if nt else (K == b.shape[0] and kblock == 0)
    tm = tm or _pick(M, (1024, 512, 256, 128))
    tn = tn or _pick(N, (1024, 1408, 1280, 512, 256, 128))
    grid = (M // tm, N // tn)
    wrap, h_in, h_ospecs, h_oshapes, h_scratch = _hosting(hosted, grid, 2 if res is None else 3, 1, 0)

    def body(*refs):
        a_ref, b_ref = refs[:2]
        o_ref = refs[-1]
        out = (_dot_nt if nt else _dot)(a_ref[...], b_ref[...])
        if res is not None:
            out = out + refs[2][...]
        o_ref[...] = out.astype(out_dtype)

    b_spec = pl.BlockSpec((tn, K), lambda i, j: (j, kblock)) if nt else pl.BlockSpec((K, tn), lambda i, j: (0, j))
    in_specs = [pl.BlockSpec((tm, K), lambda i, j: (i, 0)), b_spec]
    args = [a, b]
    if res is not None:
        in_specs.append(pl.BlockSpec((tm, tn), lambda i, j: (i, j)))
        args.append(res)
    out = pl.pallas_call(
        wrap(body), name=name, grid=grid, in_specs=in_specs + [pl.BlockSpec(memory_space=pl.ANY)] * len(h_in),
        out_specs=[pl.BlockSpec((tm, tn), lambda i, j: (i, j))] + h_ospecs, out_shape=[_sds((M, N), out_dtype)] + h_oshapes,
        scratch_shapes=h_scratch,
        compiler_params=_cparams(("arbitrary", "arbitrary") if hosted else ("parallel", "parallel")),
    )(*args, *h_in)
    return out if hosted else out[0]


def matmul_tn(a, g, *, name, tm=None, tn=None, tk=None):
    T, M = a.shape
    T2, N = g.shape
    assert T == T2
    tm = tm or _pick(M, (1408, 1280, 1024, 512, 384, 256, 128))
    tn = tn or _pick(N, (1024, 1408, 512, 256, 128))
    tk = tk or _pick(T, (1024, 512, 256, 128))
    nk = T // tk

    def body(a_ref, g_ref, o_ref, acc):
        k = pl.program_id(2)
        part = _dot_tn(a_ref[...], g_ref[...])

        @pl.when(k == 0)
        def _():
            acc[...] = part

        @pl.when(k > 0)
        def _():
            acc[...] += part

        @pl.when(k == nk - 1)
        def _():
            o_ref[...] = acc[...].astype(o_ref.dtype)

    return pl.pallas_call(
        body, name=name, grid=(M // tm, N // tn, nk),
        in_specs=[pl.BlockSpec((tk, tm), lambda i, j, k: (k, i)), pl.BlockSpec((tk, tn), lambda i, j, k: (k, j))],
        out_specs=pl.BlockSpec((tm, tn), lambda i, j, k: (i, j)), out_shape=_sds((M, N), MXU_DTYPE),
        scratch_shapes=[pltpu.VMEM((tm, tn), F32)],
        compiler_params=_cparams(("parallel", "parallel", "arbitrary")),
    )(a, g)


def rmsnorm_fwd(x, gamma, *, name, tm=512):
    T, D = x.shape
    tm = _pick(T, (tm, 256, 128))

    def body(x_ref, g_ref, o_ref):
        xv = x_ref[...]
        r = lax.rsqrt(jnp.mean(xv * xv, axis=-1, keepdims=True) + EPS)
        o_ref[...] = ((xv * r) * g_ref[...]).astype(MXU_DTYPE)

    return pl.pallas_call(
        body, name=name, grid=(T // tm,),
        in_specs=[pl.BlockSpec((tm, D), lambda i: (i, 0)), pl.BlockSpec((1, D), lambda i: (0, 0))],
        out_specs=pl.BlockSpec((tm, D), lambda i: (i, 0)), out_shape=_sds((T, D), MXU_DTYPE),
        compiler_params=_cparams(("parallel",)),
    )(x, gamma.reshape(1, D))


def _rms_bwd_tile(xv, gamma, dh):
    r = lax.rsqrt(jnp.mean(xv * xv, axis=-1, keepdims=True) + EPS)
    xh = xv * r
    dg = jnp.sum(dh * xh, axis=0, keepdims=True)
    dn = dh * gamma
    dx = r * (dn - xh * jnp.mean(dn * xh, axis=-1, keepdims=True))
    return dx, dg


def rmsnorm_bwd(x, gamma, dh, dres, *, name, tm=256):
    T, D = x.shape
    tm = _pick(T, (tm, 128))

    def body(x_ref, g_ref, dh_ref, dr_ref, dx_ref, dg_ref):
        dx, dg = _rms_bwd_tile(x_ref[...], g_ref[...], dh_ref[...].astype(F32))
        dx_ref[...] = dx + dr_ref[...]

        @pl.when(pl.program_id(0) == 0)
        def _():
            dg_ref[...] = dg

        @pl.when(pl.program_id(0) > 0)
        def _():
            dg_ref[...] += dg

    row = pl.BlockSpec((tm, D), lambda i: (i, 0))
    vec = pl.BlockSpec((1, D), lambda i: (0, 0))
    return pl.pallas_call(
        body, name=name, grid=(T // tm,), in_specs=[row, vec, row, row], out_specs=[row, vec],
        out_shape=[_sds((T, D), F32), _sds((1, D), F32)], compiler_params=_cparams(("arbitrary",)),
    )(x, gamma.reshape(1, D), dh, dres)


def final_loss(x, gamma, target, *, name="final_loss", tm=256):
    T, D = x.shape
    tm = _pick(T, (tm, 128))

    def body(x_ref, g_ref, t_ref, l_ref, dx_ref, dg_ref):
        xv = x_ref[...]
        gam = g_ref[...]
        r = lax.rsqrt(jnp.mean(xv * xv, axis=-1, keepdims=True) + EPS)
        y = (xv * r) * gam
        err = y - t_ref[...]
        part = 0.5 * jnp.sum(jnp.sum(err * err, axis=-1, keepdims=True) / D, axis=0, keepdims=True)
        dx, dg = _rms_bwd_tile(xv, gam, err / D)
        dx_ref[...] = dx

        @pl.when(pl.program_id(0) == 0)
        def _():
            dg_ref[...] = dg
            l_ref[...] = jnp.broadcast_to(part, l_ref.shape)

        @pl.when(pl.program_id(0) > 0)
        def _():
            dg_ref[...] += dg
            l_ref[...] += jnp.broadcast_to(part, l_ref.shape)

    row = pl.BlockSpec((tm, D), lambda i: (i, 0))
    vec = pl.BlockSpec((1, D), lambda i: (0, 0))
    return pl.pallas_call(
        body, name=name, grid=(T // tm,), in_specs=[row, vec, row],
        out_specs=[pl.BlockSpec((1, LANE), lambda i: (0, 0)), row, vec],
        out_shape=[_sds((1, LANE), F32), _sds((T, D), F32), _sds((1, D), F32)],
        compiler_params=_cparams(("arbitrary",)),
    )(x, gamma.reshape(1, D), target)


def _halo_prev(ts):
    return lambda i, j, off=0: (jnp.maximum(i * (ts // HALO) - 1, 0), j + off)


def _cat_prev(cur, halo, first):
    return jnp.concatenate([jnp.where(first, 0.0, halo), cur], axis=0)


def _cat_next(cur, halo, last):
    return jnp.concatenate([cur, jnp.where(last, 0.0, halo)], axis=0)


def _delayed(cat, r):
    if r == 0:
        return cat[HALO:]
    return pltpu.roll(cat, r, axis=0)[HALO:]


def _advanced(cat, r):
    n = cat.shape[0]
    if r == 0:
        return cat[:n - HALO]
    return pltpu.roll(cat, n - r, axis=0)[:n - HALO]


def _conv_pre(cat, w, b, K):
    acc = _delayed(cat, K - 1) * w[0:1, :] + b
    for k in range(1, K):
        acc = acc + _delayed(cat, K - 1 - k) * w[k:k + 1, :]
    return acc


def _pad_rows8(w):
    return jnp.pad(w, ((0, 8 - w.shape[0]), (0, 0)))


def ssd_conv_fwd(proj, w, b, S, *, name, ts=1024, tc=512):
    T = proj.shape[0]
    C, K = SSD_CONV_CH, SSD_CONV
    ts = _pick(S, (ts, 256, 128))
    off = PXBC0 // tc
    ns = S // ts

    def body(x_ref, h_ref, w_ref, b_ref, o_ref):
        first = (pl.program_id(0) % ns) == 0
        pre = _conv_pre(_cat_prev(x_ref[...], h_ref[...], first), w_ref[...], b_ref[...], K)
        o_ref[...] = pre * _sigmoid(pre)

    return pl.pallas_call(
        body, name=name, grid=(T // ts, C // tc),
        in_specs=[pl.BlockSpec((ts, tc), lambda i, j: (i, j + off)),
                  pl.BlockSpec((HALO, tc), functools.partial(_halo_prev(ts), off=off)),
                  pl.BlockSpec((8, tc), lambda i, j: (0, j)), pl.BlockSpec((1, tc), lambda i, j: (0, j))],
        out_specs=pl.BlockSpec((ts, tc), lambda i, j: (i, j)), out_shape=_sds((T, C), F32),
        compiler_params=_cparams(("parallel", "parallel")),
    )(proj, proj, _pad_rows8(w), b.reshape(1, C))


def _conv_stats(dpre, cat, K, ts):
    rows = [jnp.sum(dpre[:ts] * _delayed(cat, K - 1 - k)[:ts], axis=0, keepdims=True) for k in range(K)]
    rows.append(jnp.sum(dpre[:ts], axis=0, keepdims=True))
    rows.append(jnp.zeros((8 - len(rows), dpre.shape[1]), F32))
    return jnp.concatenate(rows, axis=0)


def _conv_transposed(dpre, wv, K):
    acc = _advanced(dpre, K - 1) * wv[0:1, :]
    for k in range(1, K):
        acc = acc + _advanced(dpre, K - 1 - k) * wv[k:k + 1, :]
    return acc


def ssd_conv_bwd(proj, w, b, dxc, S, *, name, ts=512, tc=512):
    T = proj.shape[0]
    C, K = SSD_CONV_CH, SSD_CONV
    ts = _pick(S, (ts, 256, 128))
    off = PXBC0 // tc
    ns = S // ts
    nblk = T // HALO

    def body(x_ref, xp_ref, xn_ref, w_ref, b_ref, d_ref, dn_ref, o_ref, acc_ref):
        i = pl.program_id(1)
        first = (i % ns) == 0
        last = (i % ns) == ns - 1
        cat = jnp.concatenate([jnp.where(first, 0.0, xp_ref[...]), x_ref[...], xn_ref[...]], axis=0)
        wv = w_ref[...]
        pre = _conv_pre(cat, wv, b_ref[...], K)
        sg = _sigmoid(pre)
        dpre = _cat_next(d_ref[...], dn_ref[...], last) * (sg * (1.0 + pre * (1.0 - sg)))
        o_ref[...] = _conv_transposed(dpre, wv, K).astype(o_ref.dtype)
        part = _conv_stats(dpre, cat, K, ts)

        @pl.when(i == 0)
        def _():
            acc_ref[...] = part

        @pl.when(i > 0)
        def _():
            acc_ref[...] += part

    hp = _halo_prev(ts)
    hn = lambda i: jnp.minimum((i + 1) * (ts // HALO), nblk - 1)
    return pl.pallas_call(
        body, name=name, grid=(C // tc, T // ts),
        in_specs=[pl.BlockSpec((ts, tc), lambda j, i: (i, j + off)),
                  pl.BlockSpec((HALO, tc), lambda j, i: hp(i, j, off)),
                  pl.BlockSpec((HALO, tc), lambda j, i: (hn(i), j + off)),
                  pl.BlockSpec((8, tc), lambda j, i: (0, j)), pl.BlockSpec((1, tc), lambda j, i: (0, j)),
                  pl.BlockSpec((ts, tc), lambda j, i: (i, j)), pl.BlockSpec((HALO, tc), lambda j, i: (hn(i), j))],
        out_specs=[pl.BlockSpec((ts, tc), lambda j, i: (i, j)), pl.BlockSpec((8, tc), lambda j, i: (0, j))],
        out_shape=[_sds((T, C), MXU_DTYPE), _sds((8, C), F32)],
        compiler_params=_cparams(("parallel", "arbitrary")),
    )(proj, proj, proj, _pad_rows8(w), b.reshape(1, C), dxc, dxc)


def ffn_conv_gate_fwd(up, w, b, S, *, name, ts=512, tc=1408):
    T, C2 = up.shape
    C, K = C2 // 2, FFN_CONV
    ts = _pick(S, (ts, 256, 128))
    nj = C // tc
    ns = S // ts
    w8 = _pad_rows8(w)
    b2 = b.reshape(1, C2)

    def body(g_ref, gh_ref, v_ref, vh_ref, wg_ref, wv_ref, bg_ref, bv_ref, o_ref):
        first = (pl.program_id(0) % ns) == 0
        g = _conv_pre(_cat_prev(g_ref[...], gh_ref[...], first), wg_ref[...], bg_ref[...], K)
        v = _conv_pre(_cat_prev(v_ref[...], vh_ref[...], first), wv_ref[...], bv_ref[...], K)
        o_ref[...] = (g * _sigmoid(g) * v).astype(o_ref.dtype)

    hp = _halo_prev(ts)
    return pl.pallas_call(
        body, name=name, grid=(T // ts, nj),
        in_specs=[pl.BlockSpec((ts, tc), lambda i, j: (i, j)), pl.BlockSpec((HALO, tc), lambda i, j: hp(i, j)),
                  pl.BlockSpec((ts, tc), lambda i, j: (i, j + nj)), pl.BlockSpec((HALO, tc), lambda i, j: hp(i, j, nj)),
                  pl.BlockSpec((8, tc), lambda i, j: (0, j)), pl.BlockSpec((8, tc), lambda i, j: (0, j + nj)),
                  pl.BlockSpec((1, tc), lambda i, j: (0, j)), pl.BlockSpec((1, tc), lambda i, j: (0, j + nj))],
        out_specs=pl.BlockSpec((ts, tc), lambda i, j: (i, j)), out_shape=_sds((T, C), MXU_DTYPE),
        compiler_params=_cparams(("parallel", "parallel")),
    )(up, up, up, up, w8, w8, b2, b2)


def ffn_conv_gate_bwd(up, w, b, dact, S, *, name, ts=256, tc=1408):
    T, C2 = up.shape
    C, K = C2 // 2, FFN_CONV
    ts = _pick(S, (ts, 256, 128))
    nj = C // tc
    ns = S // ts
    nblk = T // HALO
    w8 = _pad_rows8(w)
    b2 = b.reshape(1, C2)

    def body(g_ref, gp_ref, gn_ref, v_ref, vp_ref, vn_ref, wg_ref, wv_ref, bg_ref, bv_ref, d_ref, dn_ref,
             dug_ref, duv_ref, ag_ref, av_ref):
        i = pl.program_id(1)
        first = (i % ns) == 0
        last = (i % ns) == ns - 1
        gcat = jnp.concatenate([jnp.where(first, 0.0, gp_ref[...]), g_ref[...], gn_ref[...]], axis=0)
        vcat = jnp.concatenate([jnp.where(first, 0.0, vp_ref[...]), v_ref[...], vn_ref[...]], axis=0)
        wg, wv = wg_ref[...], wv_ref[...]
        g = _conv_pre(gcat, wg, bg_ref[...], K)
        v = _conv_pre(vcat, wv, bv_ref[...], K)
        d = _cat_next(d_ref[...], dn_ref[...], last)
        sg = _sigmoid(g)
        dg = d * v * (sg * (1.0 + g * (1.0 - sg)))
        dv = d * (g * sg)
        dug_ref[...] = _conv_transposed(dg, wg, K).astype(dug_ref.dtype)
        duv_ref[...] = _conv_transposed(dv, wv, K).astype(duv_ref.dtype)
        sgp, svp = _conv_stats(dg, gcat, K, ts), _conv_stats(dv, vcat, K, ts)

        @pl.when(i == 0)
        def _():
            ag_ref[...] = sgp
            av_ref[...] = svp

        @pl.when(i > 0)
        def _():
            ag_ref[...] += sgp
            av_ref[...] += svp

    hp = _halo_prev(ts)
    hn = lambda i: jnp.minimum((i + 1) * (ts // HALO), nblk - 1)
    cur = lambda off: pl.BlockSpec((ts, tc), lambda j, i: (i, j + off))
    prv = lambda off: pl.BlockSpec((HALO, tc), lambda j, i: hp(i, j, off))
    nxt = lambda off: pl.BlockSpec((HALO, tc), lambda j, i: (hn(i), j + off))
    row = lambda r, off: pl.BlockSpec((r, tc), lambda j, i: (0, j + off))
    dug, duv, ag, av = pl.pallas_call(
        body, name=name, grid=(nj, T // ts),
        in_specs=[cur(0), prv(0), nxt(0), cur(nj), prv(nj), nxt(nj), row(8, 0), row(8, nj), row(1, 0), row(1, nj),
                  cur(0), nxt(0)],
        out_specs=[cur(0), cur(0), row(8, 0), row(8, 0)],
        out_shape=[_sds((T, C), MXU_DTYPE), _sds((T, C), MXU_DTYPE), _sds((8, C), F32), _sds((8, C), F32)],
        compiler_params=_cparams(("parallel", "arbitrary")),
    )(up, up, up, up, up, up, w8, w8, b2, b2, dact, dact)
    return dug, duv, jnp.concatenate([ag, av], axis=1)


def _pool_counts(pos, w):
    return jnp.minimum(pos + 1.0, float(w))


def pool_fwd(proj, pool_w, pool_scale, S, *, name, ts=512):
    T = proj.shape[0]
    C, G, GD, H = POOL_WIDTH, POOL_GROUPS, POOL_GROUP_DIM, POOL_HALO
    ts = _pick(S, (ts, 256, 128))
    ns = S // ts
    off = PU0 // C

    def body(u_ref, h_ref, w_ref, s_ref, y_ref, p_ref):
        i = pl.program_id(0)
        first = (i % ns) == 0
        cat = jnp.concatenate([jnp.where(first, 0.0, h_ref[...]), u_ref[...]], axis=0)
        pos = ((i % ns) * ts + lax.broadcasted_iota(jnp.int32, (ts, 1), 0)).astype(F32)
        sums = cat
        win = 1
        for g, wlen in enumerate(POOL_WINDOWS):
            while win < wlen:
                sums = sums + pltpu.roll(sums, win, axis=0)
                win *= 2
            sl = slice(g * GD, (g + 1) * GD)
            pooled = sums[H:, sl] / _pool_counts(pos, wlen) - cat[H:, sl]
            p_ref[:, sl] = pooled.astype(p_ref.dtype)
            y_ref[:, sl] = (_dot(pooled, w_ref[g]) * s_ref[:, sl]).astype(y_ref.dtype)

    return pl.pallas_call(
        body, name=name, grid=(T // ts,),
        in_specs=[pl.BlockSpec((ts, C), lambda i: (i, off)),
                  pl.BlockSpec((H, C), lambda i: (jnp.maximum(i * (ts // H) - 1, 0), off)),
                  pl.BlockSpec((G, GD, GD), lambda i: (0, 0, 0)), pl.BlockSpec((1, C), lambda i: (0, 0))],
        out_specs=[pl.BlockSpec((ts, C), lambda i: (i, 0)), pl.BlockSpec((ts, C), lambda i: (i, 0))],
        out_shape=[_sds((T, C), MXU_DTYPE), _sds((T, C), MXU_DTYPE)],
        compiler_params=_cparams(("parallel",)),
    )(proj, proj, _mx(pool_w), pool_scale.reshape(1, C))


def pool_bwd(dmix, pooled, pool_w, pool_scale, S, *, name, ts=512):
    T = dmix.shape[0]
    C, G, GD, H = POOL_WIDTH, POOL_GROUPS, POOL_GROUP_DIM, POOL_HALO
    ts = _pick(S, (ts, 256, 128))
    ns = S // ts
    off = SSD_WIDTH // C
    nblk = T // H

    def body(d_ref, dh_ref, p_ref, w_ref, s_ref, du_ref, dw_ref, ds_ref):
        i = pl.program_id(0)
        last = (i % ns) == ns - 1
        dcat = jnp.concatenate([d_ref[...], jnp.where(last, 0.0, dh_ref[...])], axis=0)
        n = ts + H
        pos = ((i % ns) * ts + lax.broadcasted_iota(jnp.int32, (n, 1), 0)).astype(F32)
        dws, dss = [], []
        for g, wlen in enumerate(POOL_WINDOWS):
            sl = slice(g * GD, (g + 1) * GD)
            wg = w_ref[g]
            pg = p_ref[:, sl]
            dys = dcat[:, sl] * s_ref[:, sl]
            dss.append(jnp.sum(dcat[:ts, sl] * _dot(pg, wg), axis=0, keepdims=True))
            dws.append(_dot_tn(pg, dys[:ts]))
            dp = _dot_nt(dys, wg)
            q = dp / _pool_counts(pos, wlen)
            win = 1
            while win < wlen:
                q = q + pltpu.roll(q, n - win, axis=0)
                win *= 2
            du_ref[:, sl] = (q[:ts] - dp[:ts]).astype(du_ref.dtype)
        dsp = jnp.concatenate(dss, axis=1)

        @pl.when(i == 0)
        def _():
            for g in range(G):
                dw_ref[g] = dws[g]
            ds_ref[...] = dsp

        @pl.when(i > 0)
        def _():
            for g in range(G):
                dw_ref[g] += dws[g]
            ds_ref[...] += dsp

    return pl.pallas_call(
        body, name=name, grid=(T // ts,),
        in_specs=[pl.BlockSpec((ts, C), lambda i: (i, off)),
                  pl.BlockSpec((H, C), lambda i: (jnp.minimum((i + 1) * (ts // H), nblk - 1), off)),
                  pl.BlockSpec((ts, C), lambda i: (i, 0)),
                  pl.BlockSpec((G, GD, GD), lambda i: (0, 0, 0)), pl.BlockSpec((1, C), lambda i: (0, 0))],
        out_specs=[pl.BlockSpec((ts, C), lambda i: (i, 0)), pl.BlockSpec((G, GD, GD), lambda i: (0, 0, 0)),
                   pl.BlockSpec((1, C), lambda i: (0, 0))],
        out_shape=[_sds((T, C), MXU_DTYPE), _sds((G, GD, GD), F32), _sds((1, C), F32)],
        compiler_params=_cparams(("arbitrary",)),
    )(dmix, dmix, pooled, _mx(pool_w), pool_scale.reshape(1, C))


ROPE0 = MLA_NOPE
ROPE_HALF = MLA_ROPE // 2


def _rope_tables(pos, invf):
    lane = lax.broadcasted_iota(jnp.int32, (1, HEAD_W), 1)
    ang = pos * invf
    cs, sn = jnp.cos(ang), jnp.sin(ang)
    in_a = (lane >= ROPE0) & (lane < ROPE0 + ROPE_HALF)
    in_b = (lane >= ROPE0 + ROPE_HALF) & (lane < ROPE0 + MLA_ROPE)
    return jnp.where(in_a | in_b, cs, 1.0), jnp.where(in_a, -sn, 0.0), jnp.where(in_b, sn, 0.0), in_a | in_b


def _rope(v, cosf, sin_a, sin_b):
    return (v * cosf + pltpu.roll(v, HEAD_W - ROPE_HALF, axis=1) * sin_a + pltpu.roll(v, ROPE_HALF, axis=1) * sin_b)


def _unrope(d, cosf, sin_a, sin_b):
    return (d * cosf + pltpu.roll(d * sin_a, ROPE_HALF, axis=1) + pltpu.roll(d * sin_b, HEAD_W - ROPE_HALF, axis=1))


def _rms_tile(xv, gamma):
    return (xv * lax.rsqrt(jnp.mean(xv * xv, axis=-1, keepdims=True) + EPS)) * gamma


def mla_prep_fwd(proj, pos, invf, q_norm, w_uq_p, kv_norm, w_ukv_p, *, name, tm=256):
    T = proj.shape[0]
    tm = _pick(T, (tm, 128))
    QR, KR, P = MLA_Q_RANK, MLA_KV_RANK, MLA_PAD

    def body(cq_ref, ckv_ref, kpe_ref, pos_ref, invf_ref, qn_ref, wq_ref, kn_ref, wkv_ref,
             q_ref, k_ref, v_ref, cqn_ref, ckvn_ref):
        cosf, sin_a, sin_b, _ = _rope_tables(pos_ref[...], invf_ref[...])
        cqn = _rms_tile(cq_ref[...], qn_ref[...]).astype(MXU_DTYPE)
        ckvn = _rms_tile(ckv_ref[...], kn_ref[...]).astype(MXU_DTYPE)
        cqn_ref[...] = cqn
        ckvn_ref[...] = ckvn
        qp = _dot(cqn, wq_ref[...])
        kvp = _dot(ckvn, wkv_ref[...])
        kpe = _rope(kpe_ref[...], cosf, sin_a, sin_b)
        for h in range(MLA_HEADS):
            sl = slice(h * HEAD_W, (h + 1) * HEAD_W)
            q_ref[:, sl] = (_rope(qp[:, sl], cosf, sin_a, sin_b) * ATTN_SCALE).astype(q_ref.dtype)
            k_ref[:, sl] = (kvp[:, sl] + kpe).astype(k_ref.dtype)
            v_ref[:, sl] = kvp[:, P + h * HEAD_W:P + (h + 1) * HEAD_W].astype(v_ref.dtype)

    row = lambda w: pl.BlockSpec((tm, w), lambda i: (i, 0))
    full = lambda a, b: pl.BlockSpec((a, b), lambda i: (0, 0))
    return pl.pallas_call(
        body, name=name, grid=(T // tm,),
        in_specs=[pl.BlockSpec((tm, QR), lambda i: (i, PCQ0 // QR)), pl.BlockSpec((tm, KR), lambda i: (i, PCKV0 // KR)),
                  pl.BlockSpec((tm, LANE), lambda i: (i, PKPE0 // LANE)), row(1), full(1, LANE),
                  full(1, QR), full(QR, P), full(1, KR), full(KR, 2 * P)],
        out_specs=[row(P), row(P), row(P), row(QR), row(KR)],
        out_shape=[_sds((T, P), MXU_DTYPE)] * 3 + [_sds((T, QR), MXU_DTYPE), _sds((T, KR), MXU_DTYPE)],
        compiler_params=_cparams(("parallel",)),
    )(proj, proj, proj, pos, invf, q_norm.reshape(1, QR), w_uq_p, kv_norm.reshape(1, KR), w_ukv_p)


def mla_prep_bwd(proj, pos, invf, q_norm, w_uq_p, kv_norm, w_ukv_p, dq, dk, dv, *, name, tm=256):
    T = proj.shape[0]
    tm = _pick(T, (tm, 128))
    QR, KR, P = MLA_Q_RANK, MLA_KV_RANK, MLA_PAD

    def body(cq_ref, ckv_ref, pos_ref, invf_ref, qn_ref, wq_ref, kn_ref, wkv_ref, dq_ref, dk_ref, dv_ref,
             dqp_ref, dkvp_ref, dcq_ref, dckv_ref, dkpe_ref, dqn_ref, dkn_ref):
        cosf, sin_a, sin_b, rot = _rope_tables(pos_ref[...], invf_ref[...])
        dkpe = jnp.zeros((tm, HEAD_W), F32)
        for h in range(MLA_HEADS):
            sl = slice(h * HEAD_W, (h + 1) * HEAD_W)
            dqp_ref[:, sl] = _unrope(dq_ref[:, sl] * ATTN_SCALE, cosf, sin_a, sin_b).astype(dqp_ref.dtype)
            dkh = dk_ref[:, sl]
            dkpe = dkpe + dkh
            dkvp_ref[:, sl] = dkh.astype(dkvp_ref.dtype)
            dkvp_ref[:, P + h * HEAD_W:P + (h + 1) * HEAD_W] = dv_ref[:, sl].astype(dkvp_ref.dtype)
        dkpe_ref[...] = jnp.where(rot, _unrope(dkpe, cosf, sin_a, sin_b), 0.0).astype(dkpe_ref.dtype)
        dcq, dqn = _rms_bwd_tile(cq_ref[...], qn_ref[...], _dot_nt(dqp_ref[...], wq_ref[...]))
        dckv, dkn = _rms_bwd_tile(ckv_ref[...], kn_ref[...], _dot_nt(dkvp_ref[...], wkv_ref[...]))
        dcq_ref[...] = dcq.astype(dcq_ref.dtype)
        dckv_ref[...] = dckv.astype(dckv_ref.dtype)

        @pl.when(pl.program_id(0) == 0)
        def _():
            dqn_ref[...] = dqn
            dkn_ref[...] = dkn

        @pl.when(pl.program_id(0) > 0)
        def _():
            dqn_ref[...] += dqn
            dkn_ref[...] += dkn

    row = lambda w: pl.BlockSpec((tm, w), lambda i: (i, 0))
    full = lambda a, b: pl.BlockSpec((a, b), lambda i: (0, 0))
    return pl.pallas_call(
        body, name=name, grid=(T // tm,),
        in_specs=[pl.BlockSpec((tm, QR), lambda i: (i, PCQ0 // QR)), pl.BlockSpec((tm, KR), lambda i: (i, PCKV0 // KR)),
                  row(1), full(1, LANE), full(1, QR), full(QR, P), full(1, KR), full(KR, 2 * P), row(P), row(P), row(P)],
        out_specs=[row(P), row(2 * P), row(QR), row(KR), row(LANE), full(1, QR), full(1, KR)],
        out_shape=[_sds((T, P), MXU_DTYPE), _sds((T, 2 * P), MXU_DTYPE), _sds((T, QR), MXU_DTYPE),
                   _sds((T, KR), MXU_DTYPE), _sds((T, LANE), MXU_DTYPE), _sds((1, QR), F32), _sds((1, KR), F32)],
        compiler_params=_cparams(("arbitrary",)),
    )(proj, proj, pos, invf, q_norm.reshape(1, QR), w_uq_p, kv_norm.reshape(1, KR), w_ukv_p, dq, dk, dv)


ATTN_SCALE = 1.0 / math.sqrt(MLA_QK)


def _causal_mask(i, j, blk):
    row = lax.broadcasted_iota(jnp.int32, (blk, blk), 0)
    col = lax.broadcasted_iota(jnp.int32, (blk, blk), 1)
    return col <= row + (i - j) * blk


def _hosting(hosted, grid, n_in, n_out, n_scratch):
    if hosted is None:
        return (lambda body: body), (), [], [], []
    hi, ho = len(hosted.inputs), len(hosted.out_shapes)

    def wrap(body):
        def full(*refs):
            ins, rest = refs[:n_in + hi], refs[n_in + hi:]
            outs, scr = rest[:n_out + ho], rest[n_out + ho:]
            parts = ins[n_in:], outs[n_out:], scr[n_scratch:]
            ids = [pl.program_id(d) for d in range(len(grid))]
            step = ids[0]
            for d in range(1, len(grid)):
                step = step * grid[d] + ids[d]
            total = math.prod(grid)

            @pl.when(step == 0)
            def _():
                hosted.start(*parts)

            body(*ins[:n_in], *outs[:n_out], *scr[:n_scratch])

            @pl.when(step == total // 2)
            def _():
                hosted.relay(*parts)

            @pl.when(step == total - 1)
            def _():
                hosted.finish(*parts)

        return full

    hbm = pl.BlockSpec(memory_space=pl.ANY)
    return wrap, tuple(hosted.inputs), [hbm] * ho, list(hosted.out_shapes), list(hosted.sems)


def flash_fwd(q, k, v, S, *, name, blk=1024, hosted=None):
    T, P = q.shape
    blk = _pick(S // 2, (blk, 256, 128))
    B, nq, H, W = T // S, S // (2 * blk), MLA_HEADS, HEAD_W
    grid = (B, H, nq)
    wrap, h_in, h_ospecs, h_oshapes, h_scratch = _hosting(hosted, grid, 3, 2, 0)

    def body(q_ref, k_ref, v_ref, o_ref, lse_ref):
        i = pl.program_id(2)
        q_up, q_lo = q_ref[:blk, :], q_ref[blk:, :]

        def online(qv, kv, vv, carry, masked):
            m_prev, l_prev, acc = carry
            s = _dot_nt(qv, kv)
            if masked:
                s = jnp.where(_causal_mask(0, 0, blk), s, -jnp.inf)
            m_new = jnp.maximum(m_prev, jnp.max(s, axis=1, keepdims=True))
            p = jnp.exp(s - m_new)
            alpha = jnp.exp(m_prev - m_new)
            return (m_new, alpha * l_prev + jnp.sum(p, axis=1, keepdims=True), alpha * acc + _dot(p, vv))

        def keys(j):
            rows = pl.ds(pl.multiple_of(j * blk, blk), blk)
            return k_ref[rows, :], v_ref[rows, :]

        def both(j, carry):
            kv, vv = keys(j)
            return online(q_up, kv, vv, carry[0], False), online(q_lo, kv, vv, carry[1], False)

        init = (jnp.full((blk, 1), -jnp.inf, F32), jnp.zeros((blk, 1), F32), jnp.zeros((blk, W), F32))
        up, lo = lax.fori_loop(0, 2 * i, both, (init, init))
        kv, vv = keys(2 * i)
        up = online(q_up, kv, vv, up, True)
        lo = online(q_lo, kv, vv, lo, False)
        kv, vv = keys(2 * i + 1)
        lo = online(q_lo, kv, vv, lo, True)
        for rows, (m, l, acc) in ((slice(0, blk), up), (slice(blk, 2 * blk), lo)):
            o_ref[rows, :] = acc / l
            lse_ref[rows, :] = jnp.broadcast_to(m + jnp.log(l), (blk, W))

    qmap = lambda b, h, i: (b * nq + i, h)
    kmap = lambda b, h, i: (b, h)
    hbm = pl.BlockSpec(memory_space=pl.ANY)
    return pl.pallas_call(
        wrap(body), name=name, grid=grid,
        in_specs=[pl.BlockSpec((2 * blk, W), qmap), pl.BlockSpec((S, W), kmap), pl.BlockSpec((S, W), kmap)] + [hbm] * len(h_in),
        out_specs=[pl.BlockSpec((2 * blk, W), qmap), pl.BlockSpec((2 * blk, W), qmap)] + h_ospecs,
        out_shape=[_sds((T, P), F32), _sds((T, P), F32)] + h_oshapes,
        scratch_shapes=h_scratch,
        compiler_params=_cparams(("arbitrary",) * 3 if hosted else ("parallel", "parallel", "arbitrary")),
    )(q, k, v, *h_in)


def flash_bwd(q, k, v, o, lse, dmix, S, *, name, blk=512, hosted=None):
    T, P = q.shape
    blk = _pick(S, (blk, 256, 128))
    B, nq, H, W = T // S, S // blk, MLA_HEADS, HEAD_W
    off = (SSD_WIDTH + POOL_WIDTH) // W
    grid = (B, H, nq)
    wrap, h_in, h_ospecs, h_oshapes, h_scratch = _hosting(hosted, grid, 6, 3, 1)

    def body(q_ref, k_ref, v_ref, o_ref, lse_ref, do_ref, dq_ref, dk_ref, dv_ref, delta_s):
        j = pl.program_id(2)

        @pl.when(j == 0)
        def _():
            for i in range(nq):
                rows = slice(i * blk, (i + 1) * blk)
                delta_s[rows, :] = jnp.sum(do_ref[rows, :] * o_ref[rows, :], axis=1, keepdims=True)
                dq_ref[rows, :] = jnp.zeros((blk, W), F32)

        kv, vv = k_ref[...], v_ref[...]

        def step(i, carry, masked):
            dk, dv = carry
            rows = pl.ds(pl.multiple_of(i * blk, blk), blk)
            qv, do = q_ref[rows, :], do_ref[rows, :]
            p = jnp.exp(_dot_nt(qv, kv) - lse_ref[rows, 0:1])
            if masked:
                p = jnp.where(_causal_mask(0, 0, blk), p, 0.0)
            ds = p * (_dot_nt(do, vv) - delta_s[rows, :])
            dq_ref[rows, :] += _dot(ds, kv)
            return dk + _dot_tn(ds, qv), dv + _dot_tn(p, do)

        zero = jnp.zeros((blk, W), F32)
        carry = step(j, (zero, zero), True)
        dk, dv = lax.fori_loop(j + 1, nq, lambda i, c: step(i, c, False), carry)
        dk_ref[...] = dk
        dv_ref[...] = dv

    full = lambda b, h, j: (b, h)
    kmap = lambda b, h, j: (b * nq + j, h)
    hbm = pl.BlockSpec(memory_space=pl.ANY)
    return pl.pallas_call(
        wrap(body), name=name, grid=grid,
        in_specs=[pl.BlockSpec((S, W), full), pl.BlockSpec((blk, W), kmap), pl.BlockSpec((blk, W), kmap),
                  pl.BlockSpec((S, W), full), pl.BlockSpec((S, W), full),
                  pl.BlockSpec((S, W), lambda b, h, j: (b, off + h))] + [hbm] * len(h_in),
        out_specs=[pl.BlockSpec((S, W), full), pl.BlockSpec((blk, W), kmap), pl.BlockSpec((blk, W), kmap)] + h_ospecs,
        out_shape=[_sds((T, P), F32)] * 3 + h_oshapes,
        scratch_shapes=[pltpu.VMEM((S, 1), F32)] + h_scratch,
        compiler_params=_cparams(("arbitrary",) * 3 if hosted else ("parallel", "parallel", "arbitrary")),
    )(q, k, v, o, lse, dmix, *h_in)


SSD_PAIRS = SSD_HEADS // 2
PAIRS_PER_GROUP = SSD_PAIRS // SSD_GROUPS
GN = SSD_GROUPS * SSD_STATE


def _log1p_small(e):
    return jnp.where(e < 1e-3, e * (1.0 - e * (0.5 - e / 3.0)), jnp.log(1.0 + e))


def _softplus(v):
    return jnp.maximum(v, 0.0) + _log1p_small(jnp.exp(-jnp.abs(v)))


def _ssd_decay(dt_raw, dtb, alog):
    L = dt_raw.shape[0]
    pre = dt_raw + dtb
    dt = _softplus(pre)
    a = -jnp.exp(alog)
    row = lax.broadcasted_iota(jnp.int32, (L, L), 0)
    col = lax.broadcasted_iota(jnp.int32, (L, L), 1)
    tri = row >= col
    cum = _dot_hi(tri.astype(F32), dt * a)
    return pre, dt, a, tri, cum, cum.T


def _col(m, h):
    return m[:, h:h + 1]


def _pair_sel(m, k, lo):
    return jnp.where(lo, _col(m, 2 * k), _col(m, 2 * k + 1))


def _ssd_specs(S):
    L = SSD_CHUNK
    nc = S // L
    return L, nc


def ssd_fwd(proj, xc, dtb, alog, dchan, normw, S, *, name, hosted=None):
    T = proj.shape[0]
    L, nc = _ssd_specs(S)
    B, W, N = T // S, SSD_WIDTH, SSD_STATE
    wrap, h_in, h_ospecs, h_oshapes, h_scratch = _hosting(hosted, (B, nc), 9, 3, 1)

    def body(xs_ref, bs_ref, cs_ref, dt_ref, z_ref, dtb_ref, alog_ref, dch_ref, nw_ref, y_ref, ys_ref, hin_ref, st):
        @pl.when(pl.program_id(1) == 0)
        def _():
            st[...] = jnp.zeros(st.shape, F32)

        hin_ref[...] = st[...]
        _, dt, a, tri, cum, cum_t = _ssd_decay(dt_ref[...], dtb_ref[...], alog_ref[...])
        last = cum[L - 1:L, :]
        lo = lax.broadcasted_iota(jnp.int32, (1, LANE), 1) < SSD_HEAD_DIM
        for g in range(SSD_GROUPS):
            bm = bs_ref[:, g * N:(g + 1) * N]
            cm = cs_ref[:, g * N:(g + 1) * N]
            bm_t = bm.T
            gmat = _dot_nt(cm, bm)
            for kk in range(PAIRS_PER_GROUP):
                k = g * PAIRS_PER_GROUP + kk
                sl = slice(k * LANE, (k + 1) * LANE)
                xv = xs_ref[:, sl]
                xdt = xv * _pair_sel(dt, k, lo)
                cum_cols = [jnp.broadcast_to(_col(cum, h), (L, LANE)) for h in (2 * k, 2 * k + 1)]
                cum_sel = jnp.where(lo, cum_cols[0], cum_cols[1])
                last_sel = _pair_sel(last, k, lo)
                yd = []
                for j, h in enumerate((2 * k, 2 * k + 1)):
                    gam = jnp.exp(jnp.where(tri, cum_cols[j] - cum_t[h:h + 1, :], -jnp.inf))
                    yd.append(_dot(gmat * gam, xdt))
                hp = st[:, sl]
                y_off = _dot(cm, hp) * jnp.exp(cum_sel)
                y_ref[:, sl] = jnp.where(lo, yd[0], yd[1]) + y_off + xv * dch_ref[:, sl]
                zmat = xdt * jnp.exp(last_sel - cum_sel)
                st[:, sl] = hp * jnp.exp(last_sel) + _dot(bm_t, zmat)
        y = y_ref[...]
        z = z_ref[...]
        yz = y * (z * _sigmoid(z))
        ys_ref[...] = _rms_tile(yz, nw_ref[...]).astype(ys_ref.dtype)

    r = lambda b, c: b * nc + c
    vec = lambda w: pl.BlockSpec((1, w), lambda b, c: (0, 0))
    hbm = pl.BlockSpec(memory_space=pl.ANY)
    return pl.pallas_call(
        wrap(body), name=name, grid=(B, nc),
        in_specs=[pl.BlockSpec((L, W), lambda b, c: (r(b, c), 0)),
                  pl.BlockSpec((L, GN), lambda b, c: (r(b, c), W // GN)),
                  pl.BlockSpec((L, GN), lambda b, c: (r(b, c), W // GN + 1)),
                  pl.BlockSpec((L, LANE), lambda b, c: (r(b, c), PDT0 // LANE)),
                  pl.BlockSpec((L, W), lambda b, c: (r(b, c), PZ0 // W)),
                  vec(LANE), vec(LANE), vec(W), vec(W)] + [hbm] * len(h_in),
        out_specs=[pl.BlockSpec((L, W), lambda b, c: (r(b, c), 0)), pl.BlockSpec((L, W), lambda b, c: (r(b, c), 0)),
                   pl.BlockSpec((N, W), lambda b, c: (r(b, c), 0))] + h_ospecs,
        out_shape=[_sds((T, W), F32), _sds((T, W), MXU_DTYPE), _sds((T // L * N, W), F32)] + h_oshapes,
        scratch_shapes=[pltpu.VMEM((N, W), F32)] + h_scratch,
        compiler_params=_cparams(("arbitrary", "arbitrary") if hosted else ("parallel", "arbitrary")),
    )(xc, xc, xc, proj, proj, dtb, alog, dchan, normw, *h_in)


def ssd_bwd(proj, xc, ypre, hin, dmix, dtb, alog, dchan, normw, S, *, name):
    T = proj.shape[0]
    L, nc = _ssd_specs(S)
    B, W, N = T // S, SSD_WIDTH, SSD_STATE

    def body(xs_ref, bs_ref, cs_ref, dt_ref, z_ref, y_ref, hin_ref, dys_ref, dtb_ref, alog_ref, dch_ref, nw_ref,
             dxc_ref, ddt_ref, dz_ref, sm_ref, dnw_ref, dst):
        step = pl.program_id(0) * nc + pl.program_id(1)

        @pl.when(pl.program_id(1) == 0)
        def _():
            dst[...] = jnp.zeros(dst.shape, F32)

        pre, dt, a, tri, cum, cum_t = _ssd_decay(dt_ref[...], dtb_ref[...], alog_ref[...])
        last = cum[L - 1:L, :]
        e_last = jnp.exp(last)
        lane = lax.broadcasted_iota(jnp.int32, (1, LANE), 1)
        sub = lax.broadcasted_iota(jnp.int32, (LANE, 1), 0)
        lo = lane < SSD_HEAD_DIM
        is_last_row = sub == L - 1
        tri_t = (lax.broadcasted_iota(jnp.int32, (L, L), 0) <= lax.broadcasted_iota(jnp.int32, (L, L), 1))

        y, z, nw = y_ref[...], z_ref[...], nw_ref[...]
        sg = _sigmoid(z)
        gate = z * sg
        dyz, dnw = _rms_bwd_tile(y * gate, nw, dys_ref[...])
        dy_all = dyz * gate
        dz_ref[...] = (dyz * y * (sg * (1.0 + z * (1.0 - sg)))).astype(dz_ref.dtype)

        d_cum = jnp.zeros((L, LANE), F32)
        d_cum_t = jnp.zeros((LANE, L), F32)
        d_dt = jnp.zeros((L, LANE), F32)
        d_dskip = jnp.zeros((1, LANE), F32)
        for g in range(SSD_GROUPS):
            bm = bs_ref[:, g * N:(g + 1) * N]
            cm = cs_ref[:, g * N:(g + 1) * N]
            cm_t = cm.T
            gmat = _dot_nt(cm, bm)
            gmat_t = _dot_nt(bm, cm)
            d_g = jnp.zeros((L, L), F32)
            d_bm = jnp.zeros((L, N), F32)
            d_cm = jnp.zeros((L, N), F32)
            for kk in range(PAIRS_PER_GROUP):
                k = g * PAIRS_PER_GROUP + kk
                sl = slice(k * LANE, (k + 1) * LANE)
                xv = xs_ref[:, sl]
                dyv = dy_all[:, sl]
                dt_sel = _pair_sel(dt, k, lo)
                xdt = xv * dt_sel
                hp = hin_ref[:, sl]
                dh_out = dst[:, sl]
                cum_cols = [jnp.broadcast_to(_col(cum, h), (L, LANE)) for h in (2 * k, 2 * k + 1)]
                cum_sel = jnp.where(lo, cum_cols[0], cum_cols[1])
                last_sel = _pair_sel(last, k, lo)
                e_sel = jnp.exp(cum_sel)
                w_sel = jnp.exp(last_sel - cum_sel)
                e_lane = jnp.exp(last_sel)
                y_off = _dot(cm, hp) * e_sel
                zmat = xdt * w_sel
                d_z = _dot(bm, dh_out)
                d_bm = d_bm + _dot_nt(zmat, dh_out)
                d_xdt = d_z * w_sel
                dw_full = d_z * zmat
                hh = dh_out * hp
                d_r = dyv * e_sel
                d_cm = d_cm + _dot_nt(d_r, hp)
                dst[:, sl] = dh_out * e_lane + _dot(cm_t, d_r)
                dyoff_full = dyv * y_off
                for j, h in enumerate((2 * k, 2 * k + 1)):
                    mine = lo if j == 0 else jnp.logical_not(lo)
                    hot = lane == h
                    dyh = jnp.where(mine, dyv, 0.0)
                    gam = jnp.exp(jnp.where(tri, cum_cols[j] - cum_t[h:h + 1, :], -jnp.inf))
                    gam_t = jnp.exp(jnp.where(tri_t, cum_t[h:h + 1, :] - cum_cols[j], -jnp.inf))
                    mx = gmat * gam
                    d_xdt = d_xdt + _dot(gmat_t * gam_t, dyh)
                    d_mx = jnp.where(tri, _dot_nt(dyh, xdt), 0.0)
                    d_g = d_g + d_mx * gam
                    d_seg = d_mx * mx
                    row_l = jnp.sum(d_seg + jnp.where(mine, dyoff_full - dw_full, 0.0), axis=1, keepdims=True)
                    at_end = (jnp.sum(jnp.where(mine, dw_full, 0.0), keepdims=True)
                              + jnp.sum(jnp.where(mine, hh, 0.0), keepdims=True) * _col(e_last, h))
                    d_cum = d_cum + jnp.where(hot, row_l + jnp.where(is_last_row, at_end, 0.0), 0.0)
                    d_cum_t = d_cum_t - jnp.where(sub == h, jnp.sum(d_seg, axis=0, keepdims=True), 0.0)
                    d_dskip = d_dskip + jnp.where(hot, jnp.sum(jnp.where(mine, dyv * xv, 0.0), keepdims=True), 0.0)
                for j, h in enumerate((2 * k, 2 * k + 1)):
                    mine = lo if j == 0 else jnp.logical_not(lo)
                    d_dt = d_dt + jnp.where(lane == h, jnp.sum(jnp.where(mine, d_xdt * xv, 0.0), axis=1, keepdims=True), 0.0)
                dxc_ref[:, sl] = d_xdt * dt_sel + dyv * dch_ref[:, sl]
            dxc_ref[:, W + g * N:W + (g + 1) * N] = d_bm + _dot_tn(d_g, cm)
            dxc_ref[:, W + GN + g * N:W + GN + (g + 1) * N] = d_cm + _dot(d_g, bm)

        d_cum = d_cum + d_cum_t.T
        d_da = _dot_hi(jnp.logical_not(tri).astype(F32) + (lax.broadcasted_iota(jnp.int32, (L, L), 0)
                                                              == lax.broadcasted_iota(jnp.int32, (L, L), 1)).astype(F32), d_cum)
        d_dt = d_dt + d_da * a
        heads = lane < SSD_HEADS
        d_pre = jnp.where(heads, d_dt * _sigmoid(pre), 0.0)
        ddt_ref[...] = d_pre.astype(ddt_ref.dtype)
        d_alog = jnp.sum(d_da * dt, axis=0, keepdims=True) * a
        part = jnp.concatenate([jnp.where(heads, d_alog, 0.0), jnp.sum(d_pre, axis=0, keepdims=True), d_dskip,
                                jnp.zeros((5, LANE), F32)], axis=0)

        @pl.when(step == 0)
        def _():
            sm_ref[...] = part
            dnw_ref[...] = dnw

        @pl.when(step > 0)
        def _():
            sm_ref[...] += part
            dnw_ref[...] += dnw

    r = lambda b, c: b * nc + (nc - 1 - c)
    vec = lambda w: pl.BlockSpec((1, w), lambda b, c: (0, 0))
    blk = lambda w, j: pl.BlockSpec((L, w), lambda b, c: (r(b, c), j))
    return pl.pallas_call(
        body, name=name, grid=(B, nc),
        in_specs=[blk(W, 0), blk(GN, W // GN), blk(GN, W // GN + 1), blk(LANE, PDT0 // LANE), blk(W, PZ0 // W),
                  blk(W, 0), pl.BlockSpec((N, W), lambda b, c: (r(b, c), 0)), blk(W, 0),
                  vec(LANE), vec(LANE), vec(W), vec(W)],
        out_specs=[blk(SSD_CONV_CH, 0), blk(LANE, 0), blk(W, 0), pl.BlockSpec((8, LANE), lambda b, c: (0, 0)), vec(W)],
        out_shape=[_sds((T, SSD_CONV_CH), F32), _sds((T, LANE), MXU_DTYPE), _sds((T, W), MXU_DTYPE),
                   _sds((8, LANE), F32), _sds((1, W), F32)],
        scratch_shapes=[pltpu.VMEM((N, W), F32)],
        compiler_params=_cparams(("arbitrary", "arbitrary")),
    )(xc, xc, xc, proj, proj, ypre, hin, dmix, dtb, alog, dchan, normw)


def _adamw_math(w, g, m, v):
    m = ADAM_B1 * m + (1.0 - ADAM_B1) * g
    v = ADAM_B2 * v + (1.0 - ADAM_B2) * (g * g)
    m_hat = m / (1.0 - ADAM_B1 ** ADAM_STEP)
    v_hat = v / (1.0 - ADAM_B2 ** ADAM_STEP)
    delta = -ADAM_LR * (m_hat / (jnp.sqrt(v_hat) + ADAM_EPS) + ADAM_WD * w)
    return delta, m, v


def adamw_layers(w, g_layers, m, v, *, name, tr=256):
    L, A, B = w.shape
    tr = _pick(A, (tr, 192, 176, 128, 64, 32, 16, 8))
    na = A // tr
    n = len(g_layers[0])

    def body(*refs):
        w_ref, m_ref, v_ref = refs[0], refs[1 + L * n], refs[2 + L * n]
        g_ref, d_ref, nm_ref, nv_ref = refs[3 + L * n:]
        layer = pl.program_id(0)
        g = None
        for l in range(L):
            parts = refs[1 + l * n:1 + (l + 1) * n]
            gl = parts[0][...]
            for p in parts[1:]:
                gl = gl + p[...]
            g = gl if g is None else jnp.where(layer == l, gl, g)
        g_ref[...] = g
        d_ref[...], nm_ref[...], nv_ref[...] = _adamw_math(w_ref[...], g, m_ref[...], v_ref[...])

    def g_spec(l):
        return pl.BlockSpec((tr, B), lambda layer, i: (jnp.where(layer == l, i, jnp.where(layer < l, 0, na - 1)), 0))

    spec = pl.BlockSpec((None, tr, B), lambda layer, i: (layer, i, 0))
    return pl.pallas_call(
        body, name=name, grid=(L, na), in_specs=[spec] + [g_spec(l) for l in range(L) for _ in range(n)] + [spec] * 2,
        out_specs=[spec] * 4, out_shape=[_sds((L, A, B), F32)] * 4, compiler_params=_cparams(("arbitrary", "arbitrary")),
    )(w, *[p for parts in g_layers for p in parts], m, v)


def adamw_small(ws, gs, ms, vs, *, name):
    n = len(ws)

    def body(*refs):
        w_refs, g_refs, m_refs, v_refs = (refs[i * n:(i + 1) * n] for i in range(4))
        d_refs, nm_refs, nv_refs = (refs[(4 + i) * n:(5 + i) * n] for i in range(3))
        for a in range(n):
            d_refs[a][...], nm_refs[a][...], nv_refs[a][...] = _adamw_math(
                w_refs[a][...], g_refs[a][...], m_refs[a][...], v_refs[a][...])

    vm = pl.BlockSpec(memory_space=pltpu.VMEM)
    out = pl.pallas_call(
        body, name=name, in_specs=[vm] * (4 * n), out_specs=[vm] * (3 * n),
        out_shape=[_sds(w.shape, F32) for w in ws] * 3, compiler_params=pltpu.CompilerParams(vmem_limit_bytes=VMEM_LIMIT),
    )(*ws, *gs, *ms, *vs)
    return out[:n], out[n:2 * n], out[2 * n:]


def _my_place():
    return lax.axis_index("x"), lax.axis_index("y"), lax.axis_index("c")


def _other_chips(x, y):
    return [(1 - x, y), (x, 1 - y), (1 - x, 1 - y)]


def relation_of(chip, me):
    d = chip ^ me
    return jnp.where(d == 2, 0, jnp.where(d == 1, 1, jnp.where(d == 3, 2, -1)))


class Exchange(NamedTuple):
    inputs: tuple
    out_shapes: tuple
    sems: tuple
    start: Callable
    relay: Callable
    finish: Callable


def scatter_exchange(srcs):
    n = len(srcs)

    def copies(ins, outs, sems):
        x, y, c = _my_place()
        out = []
        for k, (px, py) in enumerate(_other_chips(x, y)):
            for a in range(n):
                out.append(pltpu.make_async_remote_copy(
                    src_ref=ins[a].at[2 * px + py], dst_ref=outs[a].at[k], send_sem=sems[0].at[k, a],
                    recv_sem=sems[1].at[k, a], device_id=(px, py, c), device_id_type=pl.DeviceIdType.MESH))
        return out

    def start(ins, outs, sems):
        for cp in copies(ins, outs, sems):
            cp.start()

    def finish(ins, outs, sems):
        cps = copies(ins, outs, sems)
        for cp in cps:
            cp.wait_recv()
        for cp in cps:
            cp.wait_send()

    return Exchange(tuple(srcs), tuple(_sds((3,) + s.shape[1:], s.dtype) for s in srcs),
                    (pltpu.SemaphoreType.DMA((3, n)),) * 2, start, lambda *a: None, finish)


def run_exchange(ex, *, name):
    n_in, n_out = len(ex.inputs), len(ex.out_shapes)

    def body(*refs):
        parts = refs[:n_in], refs[n_in:n_in + n_out], refs[n_in + n_out:]
        ex.start(*parts)
        ex.relay(*parts)
        ex.finish(*parts)

    hbm = pl.BlockSpec(memory_space=pl.ANY)
    return pl.pallas_call(
        body, name=name, in_specs=[hbm] * n_in, out_specs=[hbm] * n_out, out_shape=list(ex.out_shapes),
        scratch_shapes=list(ex.sems), compiler_params=pltpu.CompilerParams(has_side_effects=True),
    )(*ex.inputs)


def sibling_swap(srcs, *, name):
    n = len(srcs)

    def body(*refs):
        src_refs, out_refs, (send_sems, recv_sems) = refs[:n], refs[n:2 * n], refs[2 * n:]
        x, y, c = _my_place()
        copies = [pltpu.make_async_remote_copy(
            src_ref=src_refs[a], dst_ref=out_refs[a], send_sem=send_sems.at[a], recv_sem=recv_sems.at[a],
            device_id=(x, y, 1 - c), device_id_type=pl.DeviceIdType.MESH) for a in range(n)]
        for cp in copies:
            cp.start()
        for cp in copies:
            cp.wait_recv()
        for cp in copies:
            cp.wait_send()

    hbm = pl.BlockSpec(memory_space=pl.ANY)
    return pl.pallas_call(
        body, name=name, in_specs=[hbm] * n, out_specs=[hbm] * n, out_shape=[_sds(s.shape, s.dtype) for s in srcs],
        scratch_shapes=[pltpu.SemaphoreType.DMA((n,)), pltpu.SemaphoreType.DMA((n,))],
        compiler_params=pltpu.CompilerParams(has_side_effects=True),
    )(*srcs)


def gather_exchange(srcs):
    nch = len(srcs)
    halves = [s.shape[0] // 2 for s in srcs]
    assert all(2 * h == s.shape[0] and h % 16 == 0 for h, s in zip(halves, srcs))
    pieces = [(k, q) for k in range(3) for q in range(nch)]

    def makers(ins, outs, sems):
        ici_send, ici_recv, d2d_send, d2d_recv = sems
        x, y, c = _my_place()
        peers = _other_chips(x, y)

        def rows(core, q):
            return pl.ds(core * halves[q], halves[q])

        def ici(k, q):
            px, py = peers[k]
            return pltpu.make_async_remote_copy(
                src_ref=ins[q].at[rows(c, q)], dst_ref=outs[q].at[k, rows(c, q)], send_sem=ici_send.at[k, q],
                recv_sem=ici_recv.at[k, q], device_id=(px, py, c), device_id_type=pl.DeviceIdType.MESH)

        def d2d(k, q, core):
            return pltpu.make_async_remote_copy(
                src_ref=outs[q].at[k, rows(core, q)], dst_ref=outs[q].at[k, rows(core, q)],
                send_sem=d2d_send.at[k, q], recv_sem=d2d_recv.at[k, q], device_id=(x, y, 1 - c),
                device_id_type=pl.DeviceIdType.MESH)

        return ici, d2d, c

    def start(*refs):
        ici, _, _ = makers(*refs)
        for k, q in pieces:
            ici(k, q).start()

    def relay(*refs):
        ici, d2d, c = makers(*refs)
        for k, q in pieces:
            ici(k, q).wait_recv()
            d2d(k, q, c).start()

    def finish(*refs):
        ici, d2d, c = makers(*refs)
        for k, q in pieces:
            d2d(k, q, 1 - c).wait_recv()
        for k, q in pieces:
            ici(k, q).wait_send()
            d2d(k, q, c).wait_send()

    return Exchange(tuple(srcs), tuple(_sds((3,) + s.shape, s.dtype) for s in srcs),
                    (pltpu.SemaphoreType.DMA((3, nch)),) * 4, start, relay, finish)


def all_sum_small(vec, *, name):
    R, C = vec.shape

    def body(v_ref, out_ref, buf, send_sems, recv_sems):
        x, y, c = _my_place()
        me = 4 * x + 2 * y + c
        buf[me] = v_ref[...]
        copies = []
        for k in range(1, N_DEV):
            px, py, pc = x ^ (k >> 2), y ^ ((k >> 1) & 1), c ^ (k & 1)
            copies.append(pltpu.make_async_remote_copy(
                src_ref=v_ref, dst_ref=buf.at[me], send_sem=send_sems.at[k - 1], recv_sem=recv_sems.at[k - 1],
                device_id=(px, py, pc), device_id_type=pl.DeviceIdType.MESH))
        for cp in copies:
            cp.start()
        for k in range(1, N_DEV):
            px, py, pc = x ^ (k >> 2), y ^ ((k >> 1) & 1), c ^ (k & 1)
            pltpu.make_async_remote_copy(
                src_ref=v_ref, dst_ref=buf.at[4 * px + 2 * py + pc], send_sem=send_sems.at[k - 1],
                recv_sem=recv_sems.at[k - 1], device_id=(px, py, pc), device_id_type=pl.DeviceIdType.MESH).wait_recv()
        for cp in copies:
            cp.wait_send()
        acc = buf[0]
        for d in range(1, N_DEV):
            acc = acc + buf[d]
        out_ref[...] = acc

    return pl.pallas_call(
        body, name=name, in_specs=[pl.BlockSpec(memory_space=pltpu.VMEM)], out_specs=pl.BlockSpec(memory_space=pltpu.VMEM),
        out_shape=_sds((R, C), F32),
        scratch_shapes=[pltpu.VMEM((N_DEV, R, C), F32), pltpu.SemaphoreType.DMA((N_DEV - 1,)),
                        pltpu.SemaphoreType.DMA((N_DEV - 1,))],
        compiler_params=pltpu.CompilerParams(has_side_effects=True, vmem_limit_bytes=VMEM_LIMIT),
    )(vec)


def sum_chips(own, others, *, name, tr=512):
    R, C = own.shape
    tr = _pick(R, (tr, 384, 352, 256, 128, 64, 32, 16))

    def body(o_ref, p_ref, s_ref):
        acc = o_ref[...].astype(F32)
        for k in range(3):
            acc = acc + p_ref[k].astype(F32)
        s_ref[...] = acc

    return pl.pallas_call(
        body, name=name, grid=(R // tr,),
        in_specs=[pl.BlockSpec((tr, C), lambda i: (i, 0)), pl.BlockSpec((3, tr, C), lambda i: (0, i, 0))],
        out_specs=pl.BlockSpec((tr, C), lambda i: (i, 0)), out_shape=_sds((R, C), F32),
        compiler_params=_cparams(("parallel",)),
    )(own, others)


WEIGHTS = ['attn_norm', 'w_in', 'ssd_conv_w', 'ssd_conv_b', 'ssd_dt_bias', 'ssd_a_log', 'ssd_d', 'ssd_norm', 'pool_w',
           'pool_scale', 'mla_q_norm', 'mla_w_uq', 'mla_kv_norm', 'mla_w_ukv', 'w_out', 'ffn_norm', 'ffn_w_up',
           'ffn_conv_w', 'ffn_conv_b', 'ffn_w_down', 'final_norm']
BIG = {'w_in': 2, 'mla_w_uq': 2, 'mla_w_ukv': 2, 'w_out': 1, 'ffn_w_up': 2, 'ffn_w_down': 1}
CONV_SHARDED = ('ssd_conv_w', 'ffn_conv_w')


def _zeros_cols(w, n):
    return jnp.zeros((w.shape[0], n), w.dtype)


def _w_in_to_padded(w):
    return jnp.concatenate([w[:, 0:2560], w[:, 2576:3088], w[:, 3088:3472], w[:, 2560:2576], _zeros_cols(w, 112),
                            w[:, 3472:3728], _zeros_cols(w, 64), w[:, 3728:3760], _zeros_cols(w, 32 + 128)], axis=1)


def _w_in_from_padded(g):
    return jnp.concatenate([g[:, 0:2560], g[:, PDT0:PDT0 + SSD_HEADS], g[:, PU0:PU0 + POOL_WIDTH],
                            g[:, PCQ0:PCQ0 + MLA_Q_RANK], g[:, PCKV0:PCKV0 + MLA_KV_RANK],
                            g[:, PKPE0 + ROPE0:PKPE0 + ROPE0 + MLA_ROPE]], axis=1)


def _w_uq_to_padded(w):
    r = w.reshape(MLA_Q_RANK, MLA_HEADS, MLA_QK)
    return jnp.pad(r, ((0, 0), (0, 0), (0, HEAD_W - MLA_QK))).reshape(MLA_Q_RANK, MLA_PAD)


def _w_uq_from_padded(g):
    return g.reshape(MLA_Q_RANK, MLA_HEADS, HEAD_W)[:, :, :MLA_QK].reshape(MLA_Q_RANK, MLA_HEADS * MLA_QK)


def _w_ukv_to_padded(w):
    r = w.reshape(MLA_KV_RANK, MLA_HEADS, MLA_NOPE + MLA_V)
    pad = lambda t: jnp.pad(t, ((0, 0), (0, 0), (0, HEAD_W - t.shape[2]))).reshape(MLA_KV_RANK, MLA_PAD)
    return jnp.concatenate([pad(r[:, :, :MLA_NOPE]), pad(r[:, :, MLA_NOPE:])], axis=1)


def _w_ukv_from_padded(g):
    kk = g[:, :MLA_PAD].reshape(MLA_KV_RANK, MLA_HEADS, HEAD_W)[:, :, :MLA_NOPE]
    vv = g[:, MLA_PAD:].reshape(MLA_KV_RANK, MLA_HEADS, HEAD_W)[:, :, :MLA_V]
    return jnp.concatenate([kk, vv], axis=2).reshape(MLA_KV_RANK, MLA_HEADS * (MLA_NOPE + MLA_V))


def _w_out_to_padded(w):
    att = w[SSD_WIDTH + POOL_WIDTH:].reshape(MLA_HEADS, MLA_V, D_MODEL)
    att = jnp.pad(att, ((0, 0), (0, HEAD_W - MLA_V), (0, 0))).reshape(MLA_PAD, D_MODEL)
    return jnp.concatenate([w[:SSD_WIDTH + POOL_WIDTH], att], axis=0)


def _w_out_from_padded(g):
    att = g[SSD_WIDTH + POOL_WIDTH:].reshape(MLA_HEADS, HEAD_W, D_MODEL)[:, :MLA_V].reshape(MLA_WIDTH, D_MODEL)
    return jnp.concatenate([g[:SSD_WIDTH + POOL_WIDTH], att], axis=0)


def _pad_lanes(v, n=LANE):
    return jnp.pad(v.reshape(1, -1), ((0, 0), (0, n - v.size)))


def _pack_rows(parts, cols, dtype, row_multiple=16):
    flat = jnp.concatenate([p.astype(dtype).reshape(-1) for p in parts])
    rows = -(-flat.size // (cols * row_multiple)) * row_multiple
    return jnp.pad(flat, (0, rows * cols - flat.size)).reshape(rows, cols)


def _unpack_rows(packed, shapes):
    flat = packed.reshape(-1)
    out, at = [], 0
    for s in shapes:
        n = math.prod(s)
        out.append(flat[at:at + n].reshape(s))
        at += n
    return out


def _split_for_chips(g, axis):
    a, b = g.shape
    if axis == 0:
        return g.reshape(N_CHIPS, a // N_CHIPS, b)
    return g.reshape(a, N_CHIPS, b // N_CHIPS).transpose(1, 0, 2)


_MATMUL_OPERANDS = {'w_in': ('w_in_p', _w_in_to_padded), 'mla_w_uq': ('w_uq_p', _w_uq_to_padded),
                    'mla_w_ukv': ('w_ukv_p', _w_ukv_to_padded), 'w_out': ('w_out_p', _w_out_to_padded),
                    'ffn_w_up': ('w_up', lambda a: a), 'ffn_w_down': ('w_down', lambda a: a)}


def _matmul_weights(full):
    return {_MATMUL_OPERANDS[k][0]: _MATMUL_OPERANDS[k][1](a) for k, a in full.items()}


def _layer_weights(full, small, l):
    w = _matmul_weights(full)
    for k in ('attn_norm', 'ssd_conv_w', 'ssd_conv_b', 'ssd_norm', 'pool_w', 'pool_scale', 'mla_q_norm', 'mla_kv_norm',
              'ffn_norm', 'ffn_conv_w', 'ffn_conv_b'):
        w[k] = small[k][l]
    w['dtb'] = _pad_lanes(small['ssd_dt_bias'][l])
    w['alog'] = _pad_lanes(small['ssd_a_log'][l])
    w['dchan'] = jnp.repeat(small['ssd_d'][l], SSD_HEAD_DIM).reshape(1, SSD_WIDTH)
    w['ssd_norm'] = w['ssd_norm'].reshape(1, SSD_WIDTH)
    return w


def _layer_fwd(x, pos, invf, w, S, l, hosted=None, scan_hosted=None, late_weights=None, post_weights=None):
    n = lambda s: f"{s}_l{l}"
    h1 = rmsnorm_fwd(x, w['attn_norm'], name=n("attn_norm"))
    proj = matmul(h1, w['w_in_p'], name=n("w_in"))
    xc = ssd_conv_fwd(proj, w['ssd_conv_w'], w['ssd_conv_b'], S, name=n("ssd_conv"))
    ypre, yssd, hin, *arrived = ssd_fwd(proj, xc, w['dtb'], w['alog'], w['dchan'], w['ssd_norm'], S, name=n("ssd_scan"),
                                        hosted=scan_hosted)
    if late_weights:
        w = {**w, **late_weights(arrived)}
    ypool, pooled = pool_fwd(proj, w['pool_w'], w['pool_scale'], S, name=n("pool"))
    q, k, v, cqn, ckvn = mla_prep_fwd(proj, pos, invf, w['mla_q_norm'], w['w_uq_p'], w['mla_kv_norm'], w['w_ukv_p'],
                                      name=n("mla_prep"))
    o, lse, *exchanged = flash_fwd(q, k, v, S, name=n("attention"), hosted=hosted)
    if post_weights:
        w = {**w, **post_weights(exchanged)}
    mix = jnp.concatenate([yssd, ypool, o.astype(MXU_DTYPE)], axis=1)
    x2 = matmul(mix, w['w_out_p'], res=x, name=n("w_out"))
    h2 = rmsnorm_fwd(x2, w['ffn_norm'], name=n("ffn_norm"))
    up = matmul(h2, w['w_up'], name=n("ffn_up"))
    act = ffn_conv_gate_fwd(up, w['ffn_conv_w'], w['ffn_conv_b'], S, name=n("ffn_conv_gate"))
    x3 = matmul(act, w['w_down'], res=x2, name=n("ffn_down"))
    saved = dict(x=x, h1=h1, proj=proj, xc=xc, ypre=ypre, hin=hin, pooled=pooled, q=q, k=k, v=v, cqn=cqn, ckvn=ckvn,
                 o=o, lse=lse, mix=mix, x2=x2, h2=h2, up=up, act=act)
    return x3, saved, w, exchanged


def _layer_bwd(dx3, pos, invf, w, s, S, l, host=None, late_host=None):
    n = lambda t: f"{t}_l{l}"
    g = {}
    dact = matmul(dx3, w['w_down'], nt=True, name=n("d_ffn_down"))
    g['ffn_w_down'] = matmul_tn(s['act'], dx3, name=n("g_ffn_down"))
    dup_g, dup_v, st = ffn_conv_gate_bwd(s['up'], w['ffn_conv_w'], w['ffn_conv_b'], dact, S, name=n("d_ffn_conv_gate"))
    g['ffn_conv_w'], g['ffn_conv_b'] = st[:FFN_CONV], st[FFN_CONV]
    dh2 = matmul(dup_g, w['w_up'], nt=True, kblock=0, name=n("d_ffn_up_g"))
    dh2 = matmul(dup_v, w['w_up'], nt=True, kblock=1, res=dh2, name=n("d_ffn_up_v"))
    g['ffn_w_up'] = jnp.concatenate([matmul_tn(s['h2'], dup_g, name=n("g_ffn_up_g")),
                                     matmul_tn(s['h2'], dup_v, name=n("g_ffn_up_v"))], axis=1)
    dx2, gn = rmsnorm_bwd(s['x2'], w['ffn_norm'], dh2, dx3, name=n("d_ffn_norm"))
    g['ffn_norm'] = gn[0]
    dmix = matmul(dx2, w['w_out_p'], nt=True, name=n("d_w_out"))
    g['w_out'] = _w_out_from_padded(matmul_tn(s['mix'], dx2, name=n("g_w_out")))
    dxc, ddt, dz, sm, gsn = ssd_bwd(s['proj'], s['xc'], s['ypre'], s['hin'], dmix, w['dtb'], w['alog'], w['dchan'],
                                    w['ssd_norm'], S, name=n("d_ssd_scan"))
    g['ssd_a_log'], g['ssd_dt_bias'], g['ssd_d'] = sm[0, :SSD_HEADS], sm[1, :SSD_HEADS], sm[2, :SSD_HEADS]
    g['ssd_norm'] = gsn[0]
    dxbc, st = ssd_conv_bwd(s['proj'], w['ssd_conv_w'], w['ssd_conv_b'], dxc, S, name=n("d_ssd_conv"))
    g['ssd_conv_w'], g['ssd_conv_b'] = st[:SSD_CONV], st[SSD_CONV]
    du, g['pool_w'], gps = pool_bwd(dmix, s['pooled'], w['pool_w'], w['pool_scale'], S, name=n("d_pool"))
    g['pool_scale'] = gps[0]
    dq, dk, dv, *exchanged = flash_bwd(s['q'], s['k'], s['v'], s['o'], s['lse'], dmix, S, name=n("d_attention"),
                                       hosted=host(g) if host else None)
    dqp, dkvp, dcq, dckv, dkpe, gqn, gkn = mla_prep_bwd(s['proj'], pos, invf, w['mla_q_norm'], w['w_uq_p'],
                                                        w['mla_kv_norm'], w['w_ukv_p'], dq, dk, dv, name=n("d_mla_prep"))
    g['mla_q_norm'], g['mla_kv_norm'] = gqn[0], gkn[0]
    g['mla_w_uq'] = _w_uq_from_padded(matmul_tn(s['cqn'], dqp, name=n("g_w_uq")))
    g['mla_w_ukv'] = _w_ukv_from_padded(matmul_tn(s['ckvn'], dkvp, name=n("g_w_ukv")))
    dproj = jnp.concatenate([dz, dxbc, du, dcq, ddt, dckv, dkpe, jnp.zeros_like(dkpe)], axis=1)
    g['w_in'] = _w_in_from_padded(matmul_tn(s['h1'], dproj, name=n("g_w_in")))
    dh1 = matmul(dproj, w['w_in_p'], nt=True, name=n("d_w_in"), hosted=late_host(g) if late_host else None)
    dh1, late_exchanged = (dh1[0], dh1[1:]) if late_host else (dh1, [])
    dx, gn = rmsnorm_bwd(s['x'], w['attn_norm'], dh1, dx2, name=n("d_attn_norm"))
    g['attn_norm'] = gn[0]
    return dx, g, exchanged, late_exchanged


def _rope_inputs(positions):
    pos = positions.reshape(-1, 1).astype(F32)
    inv_freq = ROPE_THETA ** (-jnp.arange(0, MLA_ROPE, 2, dtype=F32) / MLA_ROPE)
    invf = jnp.concatenate([jnp.zeros((ROPE0,), F32), inv_freq, inv_freq,
                            jnp.zeros((HEAD_W - ROPE0 - MLA_ROPE,), F32)]).reshape(1, HEAD_W)
    return pos, invf


EARLY_GRADS = ('w_out', 'ffn_w_up', 'ffn_w_down')


def kernel(x, positions, attn_norm, w_in, ssd_conv_w, ssd_conv_b, ssd_dt_bias, ssd_a_log, ssd_d, ssd_norm, pool_w, pool_scale, mla_q_norm, mla_w_uq, mla_kv_norm, mla_w_ukv, w_out, ffn_norm, ffn_w_up, ffn_conv_w, ffn_conv_b, ffn_w_down, final_norm, loss_target, m_attn_norm, m_w_in, m_ssd_conv_w, m_ssd_conv_b, m_ssd_dt_bias, m_ssd_a_log, m_ssd_d, m_ssd_norm, m_pool_w, m_pool_scale, m_mla_q_norm, m_mla_w_uq, m_mla_kv_norm, m_mla_w_ukv, m_w_out, m_ffn_norm, m_ffn_w_up, m_ffn_conv_w, m_ffn_conv_b, m_ffn_w_down, m_final_norm, v_attn_norm, v_w_in, v_ssd_conv_w, v_ssd_conv_b, v_ssd_dt_bias, v_ssd_a_log, v_ssd_d, v_ssd_norm, v_pool_w, v_pool_scale, v_mla_q_norm, v_mla_w_uq, v_mla_kv_norm, v_mla_w_ukv, v_w_out, v_ffn_norm, v_ffn_w_up, v_ffn_conv_w, v_ffn_conv_b, v_ffn_w_down, v_final_norm):
    wv = dict(zip(WEIGHTS, (attn_norm, w_in, ssd_conv_w, ssd_conv_b, ssd_dt_bias, ssd_a_log, ssd_d, ssd_norm, pool_w,
                            pool_scale, mla_q_norm, mla_w_uq, mla_kv_norm, mla_w_ukv, w_out, ffn_norm, ffn_w_up,
                            ffn_conv_w, ffn_conv_b, ffn_w_down, final_norm)))
    mv = dict(zip(WEIGHTS, (m_attn_norm, m_w_in, m_ssd_conv_w, m_ssd_conv_b, m_ssd_dt_bias, m_ssd_a_log, m_ssd_d,
                            m_ssd_norm, m_pool_w, m_pool_scale, m_mla_q_norm, m_mla_w_uq, m_mla_kv_norm, m_mla_w_ukv,
                            m_w_out, m_ffn_norm, m_ffn_w_up, m_ffn_conv_w, m_ffn_conv_b, m_ffn_w_down, m_final_norm)))
    vv = dict(zip(WEIGHTS, (v_attn_norm, v_w_in, v_ssd_conv_w, v_ssd_conv_b, v_ssd_dt_bias, v_ssd_a_log, v_ssd_d,
                            v_ssd_norm, v_pool_w, v_pool_scale, v_mla_q_norm, v_mla_w_uq, v_mla_kv_norm, v_mla_w_ukv,
                            v_w_out, v_ffn_norm, v_ffn_w_up, v_ffn_conv_w, v_ffn_conv_b, v_ffn_w_down, v_final_norm)))
    Bl, S, D = x.shape
    chip = 2 * lax.axis_index("x") + lax.axis_index("y")
    core = lax.axis_index("c")

    big_names = list(BIG)
    first_names = ['w_in']
    scan_names = ['mla_w_uq', 'mla_w_ukv']
    post_names = [k for k in big_names if k not in first_names + scan_names]

    def shards(l, names):
        return [wv[k][l].astype(MXU_DTYPE) for k in names]

    def whole_weights(names, own, others):
        rel = [relation_of(j, chip) for j in range(N_CHIPS)]
        return {k: jnp.concatenate(
            [jnp.where(r < 0, mine, jnp.where(r == 0, theirs[0], jnp.where(r == 1, theirs[1], theirs[2]))) for r in rel],
            axis=BIG[k] - 1) for k, mine, theirs in zip(names, own, others)}

    first_others = run_exchange(gather_exchange(shards(0, first_names)), name="gather_w_in_l0")
    placed = []
    for k in CONV_SHARDED:
        sh = wv[k]
        whole = jnp.zeros(sh.shape[:-1] + (sh.shape[-1] * N_CHIPS,), F32)
        whole = lax.dynamic_update_slice_in_dim(whole, sh, chip * sh.shape[-1], axis=sh.ndim - 1)
        placed.append(jnp.where(core == 1, whole, 0.0))
    conv_full = _unpack_rows(all_sum_small(_pack_rows(placed, LANE, F32), name="gather_conv_weights"),
                             [p.shape for p in placed])
    small = {k: wv[k] for k in WEIGHTS if k not in BIG}
    small.update(dict(zip(CONV_SHARDED, conv_full)))

    T = Bl * S
    pos, invf = _rope_inputs(positions)
    group_a = [(k, 1) for k in big_names] + [(k, 0) for k in EARLY_GRADS]
    group_b = [(k, 0) for k in big_names if k not in EARLY_GRADS]

    def scatter_of(group, layer_grads):
        send = [_split_for_chips(layer_grads[l][k], BIG[k] - 1) for k, l in group]
        return send, scatter_exchange(send)

    layer_grads = [None] * DEPTH
    sent = {}
    w0 = _layer_weights(whole_weights(first_names, shards(0, first_names), first_others), small, 0)
    h, saved0, w0, arrived = _layer_fwd(
        x.reshape(T, D), pos, invf, w0, S, 0,
        scan_hosted=gather_exchange(shards(0, scan_names)),
        late_weights=lambda got: _matmul_weights(whole_weights(scan_names, shards(0, scan_names), got)),
        hosted=gather_exchange(shards(0, post_names) + shards(1, big_names)),
        post_weights=lambda got: _matmul_weights(whole_weights(post_names, shards(0, post_names), got[:len(post_names)])))
    others1 = arrived[len(post_names):]
    w1 = _layer_weights(whole_weights(big_names, shards(1, big_names), others1), small, 1)
    h, saved1, w1, _ = _layer_fwd(h, pos, invf, w1, S, 1)
    loss, dh, g_final_norm = final_loss(h, small['final_norm'], loss_target.reshape(T, D))
    dh, layer_grads[1], _, _ = _layer_bwd(dh, pos, invf, w1, saved1, S, 1)

    def host_a(early):
        layer_grads[0] = early
        sent['a'], ex = scatter_of(group_a, layer_grads)
        return ex

    def host_b(_):
        sent['b'], ex = scatter_of(group_b, layer_grads)
        return ex

    dx, layer_grads[0], others_a, others_b = _layer_bwd(dh, pos, invf, w0, saved0, S, 0, host=host_a, late_host=host_b)
    small_names = [k for k in WEIGHTS if k not in BIG]
    grads = {k: jnp.stack([layer_grads[l][k] for l in range(DEPTH)]) for k in small_names if k != 'final_norm'}
    grads['final_norm'] = g_final_norm[0]

    pieces = [{}, {}]
    for tag, group, others in (('a', group_a, others_a), ('b', group_b, others_b)):
        mine = [sum_chips(lax.dynamic_index_in_dim(s, chip, 0, keepdims=False), o, name=f"sum_chips_{k}_l{l}")
                for s, o, (k, l) in zip(sent[tag], others, group)]
        theirs = sibling_swap(mine, name=f"swap_core_sums_{tag}")
        pieces[0].update(dict(zip(group, mine)))
        pieces[1].update(dict(zip(group, theirs)))
    small_sum = all_sum_small(_pack_rows([grads[k] for k in small_names] + [loss[0, :1]], LANE, F32), name="sum_small_grads")
    summed = _unpack_rows(small_sum, [grads[k].shape for k in small_names] + [(1,)])
    loss_total = summed[-1].reshape(())
    g_small = dict(zip(small_names, summed[:-1]))
    for k in CONV_SHARDED:
        n = wv[k].shape[-1]
        g_small[k] = lax.dynamic_slice_in_dim(g_small[k], chip * n, n, axis=g_small[k].ndim - 1)

    out_g, out_d, out_m, out_v = {}, {}, {}, {}
    for k in big_names:
        g_layers = [[pieces[0][(k, l)], pieces[1][(k, l)]] for l in range(DEPTH)]
        out_g[k], out_d[k], out_m[k], out_v[k] = adamw_layers(wv[k], g_layers, mv[k], vv[k], name=f"adamw_{k}")
    at_least_2d = lambda a: a.reshape(1, -1) if a.ndim == 1 else a
    res = adamw_small(*[[at_least_2d(d[k]) for k in small_names] for d in (wv, g_small, mv, vv)], name="adamw_small")
    out_g.update(g_small)
    for dst, r in zip((out_d, out_m, out_v), res):
        dst.update({k: a.reshape(wv[k].shape) for k, a in zip(small_names, r)})
    return (loss_total, dx.reshape(Bl, S, D), *[out_g[k] for k in WEIGHTS], *[out_d[k] for k in WEIGHTS],
            *[out_m[k] for k in WEIGHTS], *[out_v[k] for k in WEIGHTS])
```

```python
import functools
import math
from typing import Callable, NamedTuple

import jax
import jax.numpy as jnp
from jax import lax
from jax.experimental import pallas as pl
from jax.experimental.pallas import tpu as pltpu

F32 = jnp.float32
MXU_DTYPE = jnp.bfloat16
HI = lax.Precision.HIGHEST

D_MODEL = 1024
DEPTH = 2
EPS = 1e-6
SSD_HEADS = 16
SSD_HEAD_DIM = 64
SSD_WIDTH = 1024
SSD_GROUPS = 2
SSD_STATE = 128
SSD_CONV = 4
SSD_CHUNK = 128
SSD_CONV_CH = 1536
POOL_GROUPS = 4
POOL_GROUP_DIM = 128
POOL_WIDTH = 512
POOL_WINDOWS = (2, 4, 8, 16)
MLA_HEADS = 8
MLA_Q_RANK = 384
MLA_KV_RANK = 256
MLA_NOPE = 64
MLA_ROPE = 32
MLA_V = 64
MLA_QK = 96
MLA_WIDTH = 512
ROPE_THETA = 10000.0
MIX_WIDTH = 2048
IN_COLS = 3760
D_FF = 2816
FFN_CONV = 3
ADAM_LR = 0.001
ADAM_B1 = 0.9
ADAM_B2 = 0.999
ADAM_EPS = 1e-08
ADAM_WD = 0.01
ADAM_STEP = 10

LANE = 128
HALO = 8
POOL_HALO = 16
PZ0 = 0
PXBC0 = 1024
PU0 = 2560
PCQ0 = 3072
PDT0 = 3456
PCKV0 = 3584
PKPE0 = 3840
PROJ_W = 4096
HEAD_W = 128
MLA_PAD = MLA_HEADS * HEAD_W
MIXP = SSD_WIDTH + POOL_WIDTH + MLA_PAD
N_CHIPS = 4
N_DEV = 8
VMEM_LIMIT = 56 * 1024 * 1024


def _cparams(dims, vmem=None):
    return pltpu.CompilerParams(dimension_semantics=dims, vmem_limit_bytes=vmem or VMEM_LIMIT)


def _sds(shape, dtype):
    return jax.ShapeDtypeStruct(tuple(shape), dtype)


def _mx(v):
    return v.astype(MXU_DTYPE)


def _dot(a, b):
    return jnp.dot(_mx(a), _mx(b), preferred_element_type=F32)


def _dot_nt(a, b):
    return lax.dot_general(_mx(a), _mx(b), (((1,), (1,)), ((), ())), preferred_element_type=F32)


def _dot_tn(a, b):
    return lax.dot_general(_mx(a), _mx(b), (((0,), (0,)), ((), ())), preferred_element_type=F32)


def _dot_hi(a, b):
    return jnp.dot(a, b, preferred_element_type=F32, precision=HI)


def _sigmoid(v):
    return 1.0 / (1.0 + jnp.exp(-v))


def _pick(n, prefs):
    for p in prefs:
        if n % p == 0:
            return p
    return n


def matmul(a, b, *, res=None, out_dtype=F32, name, nt=False, kblock=0, tm=None, tn=None, hosted=None):
    M, K = a.shape
    N = b.shape[0] if nt else b.shape[1]
    assert (b.shape[1] % K == 0) if nt else (K == b.shape[0] and kblock == 0)
    tm = tm or _pick(M, (1024, 512, 256, 128))
    tn = tn or _pick(N, (1024, 1408, 1280, 512, 256, 128))
    grid = (M // tm, N // tn)
    wrap, h_in, h_ospecs, h_oshapes, h_scratch = _hosting(hosted, grid, 2 if res is None else 3, 1, 0)

    def body(*refs):
        a_ref, b_ref = refs[:2]
        o_ref = refs[-1]
        out = (_dot_nt if nt else _dot)(a_ref[...], b_ref[...])
        if res is not None:
            out = out + refs[2][...]
        o_ref[...] = out.astype(out_dtype)

    b_spec = pl.BlockSpec((tn, K), lambda i, j: (j, kblock)) if nt else pl.BlockSpec((K, tn), lambda i, j: (0, j))
    in_specs = [pl.BlockSpec((tm, K), lambda i, j: (i, 0)), b_spec]
    args = [a, b]
    if res is not None:
        in_specs.append(pl.BlockSpec((tm, tn), lambda i, j: (i, j)))
        args.append(res)
    out = pl.pallas_call(
        wrap(body), name=name, grid=grid, in_specs=in_specs + [pl.BlockSpec(memory_space=pl.ANY)] * len(h_in),
        out_specs=[pl.BlockSpec((tm, tn), lambda i, j: (i, j))] + h_ospecs, out_shape=[_sds((M, N), out_dtype)] + h_oshapes,
        scratch_shapes=h_scratch,
        compiler_params=_cparams(("arbitrary", "arbitrary") if hosted else ("parallel", "parallel")),
    )(*args, *h_in)
    return out if hosted else out[0]


def matmul_tn(a, g, *, name, tm=None, tn=None, tk=None):
    T, M = a.shape
    T2, N = g.shape
    assert T == T2
    tm = tm or _pick(M, (1408, 1280, 1024, 512, 384, 256, 128))
    tn = tn or _pick(N, (1024, 1408, 512, 256, 128))
    tk = tk or _pick(T, (1024, 512, 256, 128))
    nk = T // tk

    def body(a_ref, g_ref, o_ref, acc):
        k = pl.program_id(2)
        part = _dot_tn(a_ref[...], g_ref[...])

        @pl.when(k == 0)
        def _():
            acc[...] = part

        @pl.when(k > 0)
        def _():
            acc[...] += part

        @pl.when(k == nk - 1)
        def _():
            o_ref[...] = acc[...].astype(o_ref.dtype)

    return pl.pallas_call(
        body, name=name, grid=(M // tm, N // tn, nk),
        in_specs=[pl.BlockSpec((tk, tm), lambda i, j, k: (k, i)), pl.BlockSpec((tk, tn), lambda i, j, k: (k, j))],
        out_specs=pl.BlockSpec((tm, tn), lambda i, j, k: (i, j)), out_shape=_sds((M, N), MXU_DTYPE),
        scratch_shapes=[pltpu.VMEM((tm, tn), F32)],
        compiler_params=_cparams(("parallel", "parallel", "arbitrary")),
    )(a, g)


def rmsnorm_fwd(x, gamma, *, name, tm=512):
    T, D = x.shape
    tm = _pick(T, (tm, 256, 128))

    def body(x_ref, g_ref, o_ref):
        xv = x_ref[...]
        r = lax.rsqrt(jnp.mean(xv * xv, axis=-1, keepdims=True) + EPS)
        o_ref[...] = ((xv * r) * g_ref[...]).astype(MXU_DTYPE)

    return pl.pallas_call(
        body, name=name, grid=(T // tm,),
        in_specs=[pl.BlockSpec((tm, D), lambda i: (i, 0)), pl.BlockSpec((1, D), lambda i: (0, 0))],
        out_specs=pl.BlockSpec((tm, D), lambda i: (i, 0)), out_shape=_sds((T, D), MXU_DTYPE),
        compiler_params=_cparams(("parallel",)),
    )(x, gamma.reshape(1, D))


def _rms_bwd_tile(xv, gamma, dh):
    r = lax.rsqrt(jnp.mean(xv * xv, axis=-1, keepdims=True) + EPS)
    xh = xv * r
    dg = jnp.sum(dh * xh, axis=0, keepdims=True)
    dn = dh * gamma
    dx = r * (dn - xh * jnp.mean(dn * xh, axis=-1, keepdims=True))
    return dx, dg


def rmsnorm_bwd(x, gamma, dh, dres, *, name, tm=256):
    T, D = x.shape
    tm = _pick(T, (tm, 128))

    def body(x_ref, g_ref, dh_ref, dr_ref, dx_ref, dg_ref):
        dx, dg = _rms_bwd_tile(x_ref[...], g_ref[...], dh_ref[...].astype(F32))
        dx_ref[...] = dx + dr_ref[...]

        @pl.when(pl.program_id(0) == 0)
        def _():
            dg_ref[...] = dg

        @pl.when(pl.program_id(0) > 0)
        def _():
            dg_ref[...] += dg

    row = pl.BlockSpec((tm, D), lambda i: (i, 0))
    vec = pl.BlockSpec((1, D), lambda i: (0, 0))
    return pl.pallas_call(
        body, name=name, grid=(T // tm,), in_specs=[row, vec, row, row], out_specs=[row, vec],
        out_shape=[_sds((T, D), F32), _sds((1, D), F32)], compiler_params=_cparams(("arbitrary",)),
    )(x, gamma.reshape(1, D), dh, dres)


def final_loss(x, gamma, target, *, name="final_loss", tm=256):
    T, D = x.shape
    tm = _pick(T, (tm, 128))

    def body(x_ref, g_ref, t_ref, l_ref, dx_ref, dg_ref):
        xv = x_ref[...]
        gam = g_ref[...]
        r = lax.rsqrt(jnp.mean(xv * xv, axis=-1, keepdims=True) + EPS)
        y = (xv * r) * gam
        err = y - t_ref[...]
        part = 0.5 * jnp.sum(jnp.sum(err * err, axis=-1, keepdims=True) / D, axis=0, keepdims=True)
        dx, dg = _rms_bwd_tile(xv, gam, err / D)
        dx_ref[...] = dx

        @pl.when(pl.program_id(0) == 0)
        def _():
            dg_ref[...] = dg
            l_ref[...] = jnp.broadcast_to(part, l_ref.shape)

        @pl.when(pl.program_id(0) > 0)
        def _():
            dg_ref[...] += dg
            l_ref[...] += jnp.broadcast_to(part, l_ref.shape)

    row = pl.BlockSpec((tm, D), lambda i: (i, 0))
    vec = pl.BlockSpec((1, D), lambda i: (0, 0))
    return pl.pallas_call(
        body, name=name, grid=(T // tm,), in_specs=[row, vec, row],
        out_specs=[pl.BlockSpec((1, LANE), lambda i: (0, 0)), row, vec],
        out_shape=[_sds((1, LANE), F32), _sds((T, D), F32), _sds((1, D), F32)],
        compiler_params=_cparams(("arbitrary",)),
    )(x, gamma.reshape(1, D), target)


def _halo_prev(ts):
    return lambda i, j, off=0: (jnp.maximum(i * (ts // HALO) - 1, 0), j + off)


def _cat_prev(cur, halo, first):
    return jnp.concatenate([jnp.where(first, 0.0, halo), cur], axis=0)


def _cat_next(cur, halo, last):
    return jnp.concatenate([cur, jnp.where(last, 0.0, halo)], axis=0)


def _delayed(cat, r):
    if r == 0:
        return cat[HALO:]
    return pltpu.roll(cat, r, axis=0)[HALO:]


def _advanced(cat, r):
    n = cat.shape[0]
    if r == 0:
        return cat[:n - HALO]
    return pltpu.roll(cat, n - r, axis=0)[:n - HALO]


def _conv_pre(cat, w, b, K):
    acc = _delayed(cat, K - 1) * w[0:1, :] + b
    for k in range(1, K):
        acc = acc + _delayed(cat, K - 1 - k) * w[k:k + 1, :]
    return acc


def _pad_rows8(w):
    return jnp.pad(w, ((0, 8 - w.shape[0]), (0, 0)))


def ssd_conv_fwd(proj, w, b, S, *, name, ts=1024, tc=512):
    T = proj.shape[0]
    C, K = SSD_CONV_CH, SSD_CONV
    ts = _pick(S, (ts, 256, 128))
    off = PXBC0 // tc
    ns = S // ts

    def body(x_ref, h_ref, w_ref, b_ref, o_ref):
        first = (pl.program_id(0) % ns) == 0
        pre = _conv_pre(_cat_prev(x_ref[...], h_ref[...], first), w_ref[...], b_ref[...], K)
        o_ref[...] = pre * _sigmoid(pre)

    return pl.pallas_call(
        body, name=name, grid=(T // ts, C // tc),
        in_specs=[pl.BlockSpec((ts, tc), lambda i, j: (i, j + off)),
                  pl.BlockSpec((HALO, tc), functools.partial(_halo_prev(ts), off=off)),
                  pl.BlockSpec((8, tc), lambda i, j: (0, j)), pl.BlockSpec((1, tc), lambda i, j: (0, j))],
        out_specs=pl.BlockSpec((ts, tc), lambda i, j: (i, j)), out_shape=_sds((T, C), F32),
        compiler_params=_cparams(("parallel", "parallel")),
    )(proj, proj, _pad_rows8(w), b.reshape(1, C))


def _conv_stats(dpre, cat, K, ts):
    rows = [jnp.sum(dpre[:ts] * _delayed(cat, K - 1 - k)[:ts], axis=0, keepdims=True) for k in range(K)]
    rows.append(jnp.sum(dpre[:ts], axis=0, keepdims=True))
    rows.append(jnp.zeros((8 - len(rows), dpre.shape[1]), F32))
    return jnp.concatenate(rows, axis=0)


def _conv_transposed(dpre, wv, K):
    acc = _advanced(dpre, K - 1) * wv[0:1, :]
    for k in range(1, K):
        acc = acc + _advanced(dpre, K - 1 - k) * wv[k:k + 1, :]
    return acc


def ssd_conv_bwd(proj, w, b, dxc, S, *, name, ts=512, tc=512):
    T = proj.shape[0]
    C, K = SSD_CONV_CH, SSD_CONV
    ts = _pick(S, (ts, 256, 128))
    off = PXBC0 // tc
    ns = S // ts
    nblk = T // HALO

    def body(x_ref, xp_ref, xn_ref, w_ref, b_ref, d_ref, dn_ref, o_ref, acc_ref):
        i = pl.program_id(1)
        first = (i % ns) == 0
        last = (i % ns) == ns - 1
        cat = jnp.concatenate([jnp.where(first, 0.0, xp_ref[...]), x_ref[...], xn_ref[...]], axis=0)
        wv = w_ref[...]
        pre = _conv_pre(cat, wv, b_ref[...], K)
        sg = _sigmoid(pre)
        dpre = _cat_next(d_ref[...], dn_ref[...], last) * (sg * (1.0 + pre * (1.0 - sg)))
        o_ref[...] = _conv_transposed(dpre, wv, K).astype(o_ref.dtype)
        part = _conv_stats(dpre, cat, K, ts)

        @pl.when(i == 0)
        def _():
            acc_ref[...] = part

        @pl.when(i > 0)
        def _():
            acc_ref[...] += part

    hp = _halo_prev(ts)
    hn = lambda i: jnp.minimum((i + 1) * (ts // HALO), nblk - 1)
    return pl.pallas_call(
        body, name=name, grid=(C // tc, T // ts),
        in_specs=[pl.BlockSpec((ts, tc), lambda j, i: (i, j + off)),
                  pl.BlockSpec((HALO, tc), lambda j, i: hp(i, j, off)),
                  pl.BlockSpec((HALO, tc), lambda j, i: (hn(i), j + off)),
                  pl.BlockSpec((8, tc), lambda j, i: (0, j)), pl.BlockSpec((1, tc), lambda j, i: (0, j)),
                  pl.BlockSpec((ts, tc), lambda j, i: (i, j)), pl.BlockSpec((HALO, tc), lambda j, i: (hn(i), j))],
        out_specs=[pl.BlockSpec((ts, tc), lambda j, i: (i, j)), pl.BlockSpec((8, tc), lambda j, i: (0, j))],
        out_shape=[_sds((T, C), MXU_DTYPE), _sds((8, C), F32)],
        compiler_params=_cparams(("parallel", "arbitrary")),
    )(proj, proj, proj, _pad_rows8(w), b.reshape(1, C), dxc, dxc)


def ffn_conv_gate_fwd(up, w, b, S, *, name, ts=512, tc=1408):
    T, C2 = up.shape
    C, K = C2 // 2, FFN_CONV
    ts = _pick(S, (ts, 256, 128))
    nj = C // tc
    ns = S // ts
    w8 = _pad_rows8(w)
    b2 = b.reshape(1, C2)

    def body(g_ref, gh_ref, v_ref, vh_ref, wg_ref, wv_ref, bg_ref, bv_ref, o_ref):
        first = (pl.program_id(0) % ns) == 0
        g = _conv_pre(_cat_prev(g_ref[...], gh_ref[...], first), wg_ref[...], bg_ref[...], K)
        v = _conv_pre(_cat_prev(v_ref[...], vh_ref[...], first), wv_ref[...], bv_ref[...], K)
        o_ref[...] = (g * _sigmoid(g) * v).astype(o_ref.dtype)

    hp = _halo_prev(ts)
    return pl.pallas_call(
        body, name=name, grid=(T // ts, nj),
        in_specs=[pl.BlockSpec((ts, tc), lambda i, j: (i, j)), pl.BlockSpec((HALO, tc), lambda i, j: hp(i, j)),
                  pl.BlockSpec((ts, tc), lambda i, j: (i, j + nj)), pl.BlockSpec((HALO, tc), lambda i, j: hp(i, j, nj)),
                  pl.BlockSpec((8, tc), lambda i, j: (0, j)), pl.BlockSpec((8, tc), lambda i, j: (0, j + nj)),
                  pl.BlockSpec((1, tc), lambda i, j: (0, j)), pl.BlockSpec((1, tc), lambda i, j: (0, j + nj))],
        out_specs=pl.BlockSpec((ts, tc), lambda i, j: (i, j)), out_shape=_sds((T, C), MXU_DTYPE),
        compiler_params=_cparams(("parallel", "parallel")),
    )(up, up, up, up, w8, w8, b2, b2)


def ffn_conv_gate_bwd(up, w, b, dact, S, *, name, ts=256, tc=1408):
    T, C2 = up.shape
    C, K = C2 // 2, FFN_CONV
    ts = _pick(S, (ts, 256, 128))
    nj = C // tc
    ns = S // ts
    nblk = T // HALO
    w8 = _pad_rows8(w)
    b2 = b.reshape(1, C2)

    def body(g_ref, gp_ref, gn_ref, v_ref, vp_ref, vn_ref, wg_ref, wv_ref, bg_ref, bv_ref, d_ref, dn_ref,
             dug_ref, duv_ref, ag_ref, av_ref):
        i = pl.program_id(1)
        first = (i % ns) == 0
        last = (i % ns) == ns - 1
        gcat = jnp.concatenate([jnp.where(first, 0.0, gp_ref[...]), g_ref[...], gn_ref[...]], axis=0)
        vcat = jnp.concatenate([jnp.where(first, 0.0, vp_ref[...]), v_ref[...], vn_ref[...]], axis=0)
        wg, wv = wg_ref[...], wv_ref[...]
        g = _conv_pre(gcat, wg, bg_ref[...], K)
        v = _conv_pre(vcat, wv, bv_ref[...], K)
        d = _cat_next(d_ref[...], dn_ref[...], last)
        sg = _sigmoid(g)
        dg = d * v * (sg * (1.0 + g * (1.0 - sg)))
        dv = d * (g * sg)
        dug_ref[...] = _conv_transposed(dg, wg, K).astype(dug_ref.dtype)
        duv_ref[...] = _conv_transposed(dv, wv, K).astype(duv_ref.dtype)
        sgp, svp = _conv_stats(dg, gcat, K, ts), _conv_stats(dv, vcat, K, ts)

        @pl.when(i == 0)
        def _():
            ag_ref[...] = sgp
            av_ref[...] = svp

        @pl.when(i > 0)
        def _():
            ag_ref[...] += sgp
            av_ref[...] += svp

    hp = _halo_prev(ts)
    hn = lambda i: jnp.minimum((i + 1) * (ts // HALO), nblk - 1)
    cur = lambda off: pl.BlockSpec((ts, tc), lambda j, i: (i, j + off))
    prv = lambda off: pl.BlockSpec((HALO, tc), lambda j, i: hp(i, j, off))
    nxt = lambda off: pl.BlockSpec((HALO, tc), lambda j, i: (hn(i), j + off))
    row = lambda r, off: pl.BlockSpec((r, tc), lambda j, i: (0, j + off))
    dug, duv, ag, av = pl.pallas_call(
        body, name=name, grid=(nj, T // ts),
        in_specs=[cur(0), prv(0), nxt(0), cur(nj), prv(nj), nxt(nj), row(8, 0), row(8, nj), row(1, 0), row(1, nj),
                  cur(0), nxt(0)],
        out_specs=[cur(0), cur(0), row(8, 0), row(8, 0)],
        out_shape=[_sds((T, C), MXU_DTYPE), _sds((T, C), MXU_DTYPE), _sds((8, C), F32), _sds((8, C), F32)],
        compiler_params=_cparams(("parallel", "arbitrary")),
    )(up, up, up, up, up, up, w8, w8, b2, b2, dact, dact)
    return dug, duv, jnp.concatenate([ag, av], axis=1)


def _pool_counts(pos, w):
    return jnp.minimum(pos + 1.0, float(w))


def pool_fwd(proj, pool_w, pool_scale, S, *, name, ts=512):
    T = proj.shape[0]
    C, G, GD, H = POOL_WIDTH, POOL_GROUPS, POOL_GROUP_DIM, POOL_HALO
    ts = _pick(S, (ts, 256, 128))
    ns = S // ts
    off = PU0 // C

    def body(u_ref, h_ref, w_ref, s_ref, y_ref, p_ref):
        i = pl.program_id(0)
        first = (i % ns) == 0
        cat = jnp.concatenate([jnp.where(first, 0.0, h_ref[...]), u_ref[...]], axis=0)
        pos = ((i % ns) * ts + lax.broadcasted_iota(jnp.int32, (ts, 1), 0)).astype(F32)
        sums = cat
        win = 1
        for g, wlen in enumerate(POOL_WINDOWS):
            while win < wlen:
                sums = sums + pltpu.roll(sums, win, axis=0)
                win *= 2
            sl = slice(g * GD, (g + 1) * GD)
            pooled = sums[H:, sl] / _pool_counts(pos, wlen) - cat[H:, sl]
            p_ref[:, sl] = pooled.astype(p_ref.dtype)
            y_ref[:, sl] = (_dot(pooled, w_ref[g]) * s_ref[:, sl]).astype(y_ref.dtype)

    return pl.pallas_call(
        body, name=name, grid=(T // ts,),
        in_specs=[pl.BlockSpec((ts, C), lambda i: (i, off)),
                  pl.BlockSpec((H, C), lambda i: (jnp.maximum(i * (ts // H) - 1, 0), off)),
                  pl.BlockSpec((G, GD, GD), lambda i: (0, 0, 0)), pl.BlockSpec((1, C), lambda i: (0, 0))],
        out_specs=[pl.BlockSpec((ts, C), lambda i: (i, 0)), pl.BlockSpec((ts, C), lambda i: (i, 0))],
        out_shape=[_sds((T, C), MXU_DTYPE), _sds((T, C), MXU_DTYPE)],
        compiler_params=_cparams(("parallel",)),
    )(proj, proj, _mx(pool_w), pool_scale.reshape(1, C))


def pool_bwd(dmix, pooled, pool_w, pool_scale, S, *, name, ts=512):
    T = dmix.shape[0]
    C, G, GD, H = POOL_WIDTH, POOL_GROUPS, POOL_GROUP_DIM, POOL_HALO
    ts = _pick(S, (ts, 256, 128))
    ns = S // ts
    off = SSD_WIDTH // C
    nblk = T // H

    def body(d_ref, dh_ref, p_ref, w_ref, s_ref, du_ref, dw_ref, ds_ref):
        i = pl.program_id(0)
        last = (i % ns) == ns - 1
        dcat = jnp.concatenate([d_ref[...], jnp.where(last, 0.0, dh_ref[...])], axis=0)
        n = ts + H
        pos = ((i % ns) * ts + lax.broadcasted_iota(jnp.int32, (n, 1), 0)).astype(F32)
        dws, dss = [], []
        for g, wlen in enumerate(POOL_WINDOWS):
            sl = slice(g * GD, (g + 1) * GD)
            wg = w_ref[g]
            pg = p_ref[:, sl]
            dys = dcat[:, sl] * s_ref[:, sl]
            dss.append(jnp.sum(dcat[:ts, sl] * _dot(pg, wg), axis=0, keepdims=True))
            dws.append(_dot_tn(pg, dys[:ts]))
            dp = _dot_nt(dys, wg)
            q = dp / _pool_counts(pos, wlen)
            win = 1
            while win < wlen:
                q = q + pltpu.roll(q, n - win, axis=0)
                win *= 2
            du_ref[:, sl] = (q[:ts] - dp[:ts]).astype(du_ref.dtype)
        dsp = jnp.concatenate(dss, axis=1)

        @pl.when(i == 0)
        def _():
            for g in range(G):
                dw_ref[g] = dws[g]
            ds_ref[...] = dsp

        @pl.when(i > 0)
        def _():
            for g in range(G):
                dw_ref[g] += dws[g]
            ds_ref[...] += dsp

    return pl.pallas_call(
        body, name=name, grid=(T // ts,),
        in_specs=[pl.BlockSpec((ts, C), lambda i: (i, off)),
                  pl.BlockSpec((H, C), lambda i: (jnp.minimum((i + 1) * (ts // H), nblk - 1), off)),
                  pl.BlockSpec((ts, C), lambda i: (i, 0)),
                  pl.BlockSpec((G, GD, GD), lambda i: (0, 0, 0)), pl.BlockSpec((1, C), lambda i: (0, 0))],
        out_specs=[pl.BlockSpec((ts, C), lambda i: (i, 0)), pl.BlockSpec((G, GD, GD), lambda i: (0, 0, 0)),
                   pl.BlockSpec((1, C), lambda i: (0, 0))],
        out_shape=[_sds((T, C), MXU_DTYPE), _sds((G, GD, GD), F32), _sds((1, C), F32)],
        compiler_params=_cparams(("arbitrary",)),
    )(dmix, dmix, pooled, _mx(pool_w), pool_scale.reshape(1, C))


ROPE0 = MLA_NOPE
ROPE_HALF = MLA_ROPE // 2


def _rope_tables(pos, invf):
    lane = lax.broadcasted_iota(jnp.int32, (1, HEAD_W), 1)
    ang = pos * invf
    cs, sn = jnp.cos(ang), jnp.sin(ang)
    in_a = (lane >= ROPE0) & (lane < ROPE0 + ROPE_HALF)
    in_b = (lane >= ROPE0 + ROPE_HALF) & (lane < ROPE0 + MLA_ROPE)
    return jnp.where(in_a | in_b, cs, 1.0), jnp.where(in_a, -sn, 0.0), jnp.where(in_b, sn, 0.0), in_a | in_b


def _rope(v, cosf, sin_a, sin_b):
    return (v * cosf + pltpu.roll(v, HEAD_W - ROPE_HALF, axis=1) * sin_a + pltpu.roll(v, ROPE_HALF, axis=1) * sin_b)


def _unrope(d, cosf, sin_a, sin_b):
    return (d * cosf + pltpu.roll(d * sin_a, ROPE_HALF, axis=1) + pltpu.roll(d * sin_b, HEAD_W - ROPE_HALF, axis=1))


def _rms_tile(xv, gamma):
    return (xv * lax.rsqrt(jnp.mean(xv * xv, axis=-1, keepdims=True) + EPS)) * gamma


def mla_prep_fwd(proj, pos, invf, q_norm, w_uq_p, kv_norm, w_ukv_p, *, name, tm=256):
    T = proj.shape[0]
    tm = _pick(T, (tm, 128))
    QR, KR, P = MLA_Q_RANK, MLA_KV_RANK, MLA_PAD

    def body(cq_ref, ckv_ref, kpe_ref, pos_ref, invf_ref, qn_ref, wq_ref, kn_ref, wkv_ref,
             q_ref, k_ref, v_ref, cqn_ref, ckvn_ref):
        cosf, sin_a, sin_b, _ = _rope_tables(pos_ref[...], invf_ref[...])
        cqn = _rms_tile(cq_ref[...], qn_ref[...]).astype(MXU_DTYPE)
        ckvn = _rms_tile(ckv_ref[...], kn_ref[...]).astype(MXU_DTYPE)
        cqn_ref[...] = cqn
        ckvn_ref[...] = ckvn
        qp = _dot(cqn, wq_ref[...])
        kvp = _dot(ckvn, wkv_ref[...])
        kpe = _rope(kpe_ref[...], cosf, sin_a, sin_b)
        for h in range(MLA_HEADS):
            sl = slice(h * HEAD_W, (h + 1) * HEAD_W)
            q_ref[:, sl] = (_rope(qp[:, sl], cosf, sin_a, sin_b) * ATTN_SCALE).astype(q_ref.dtype)
            k_ref[:, sl] = (kvp[:, sl] + kpe).astype(k_ref.dtype)
            v_ref[:, sl] = kvp[:, P + h * HEAD_W:P + (h + 1) * HEAD_W].astype(v_ref.dtype)

    row = lambda w: pl.BlockSpec((tm, w), lambda i: (i, 0))
    full = lambda a, b: pl.BlockSpec((a, b), lambda i: (0, 0))
    return pl.pallas_call(
        body, name=name, grid=(T // tm,),
        in_specs=[pl.BlockSpec((tm, QR), lambda i: (i, PCQ0 // QR)), pl.BlockSpec((tm, KR), lambda i: (i, PCKV0 // KR)),
                  pl.BlockSpec((tm, LANE), lambda i: (i, PKPE0 // LANE)), row(1), full(1, LANE),
                  full(1, QR), full(QR, P), full(1, KR), full(KR, 2 * P)],
        out_specs=[row(P), row(P), row(P), row(QR), row(KR)],
        out_shape=[_sds((T, P), MXU_DTYPE)] * 3 + [_sds((T, QR), MXU_DTYPE), _sds((T, KR), MXU_DTYPE)],
        compiler_params=_cparams(("parallel",)),
    )(proj, proj, proj, pos, invf, q_norm.reshape(1, QR), w_uq_p, kv_norm.reshape(1, KR), w_ukv_p)


def mla_prep_bwd(proj, pos, invf, q_norm, w_uq_p, kv_norm, w_ukv_p, dq, dk, dv, *, name, tm=256):
    T = proj.shape[0]
    tm = _pick(T, (tm, 128))
    QR, KR, P = MLA_Q_RANK, MLA_KV_RANK, MLA_PAD

    def body(cq_ref, ckv_ref, pos_ref, invf_ref, qn_ref, wq_ref, kn_ref, wkv_ref, dq_ref, dk_ref, dv_ref,
             dqp_ref, dkvp_ref, dcq_ref, dckv_ref, dkpe_ref, dqn_ref, dkn_ref):
        cosf, sin_a, sin_b, rot = _rope_tables(pos_ref[...], invf_ref[...])
        dkpe = jnp.zeros((tm, HEAD_W), F32)
        for h in range(MLA_HEADS):
            sl = slice(h * HEAD_W, (h + 1) * HEAD_W)
            dqp_ref[:, sl] = _unrope(dq_ref[:, sl] * ATTN_SCALE, cosf, sin_a, sin_b).astype(dqp_ref.dtype)
            dkh = dk_ref[:, sl]
            dkpe = dkpe + dkh
            dkvp_ref[:, sl] = dkh.astype(dkvp_ref.dtype)
            dkvp_ref[:, P + h * HEAD_W:P + (h + 1) * HEAD_W] = dv_ref[:, sl].astype(dkvp_ref.dtype)
        dkpe_ref[...] = jnp.where(rot, _unrope(dkpe, cosf, sin_a, sin_b), 0.0).astype(dkpe_ref.dtype)
        dcq, dqn = _rms_bwd_tile(cq_ref[...], qn_ref[...], _dot_nt(dqp_ref[...], wq_ref[...]))
        dckv, dkn = _rms_bwd_tile(ckv_ref[...], kn_ref[...], _dot_nt(dkvp_ref[...], wkv_ref[...]))
        dcq_ref[...] = dcq.astype(dcq_ref.dtype)
        dckv_ref[...] = dckv.astype(dckv_ref.dtype)

        @pl.when(pl.program_id(0) == 0)
        def _():
            dqn_ref[...] = dqn
            dkn_ref[...] = dkn

        @pl.when(pl.program_id(0) > 0)
        def _():
            dqn_ref[...] += dqn
            dkn_ref[...] += dkn

    row = lambda w: pl.BlockSpec((tm, w), lambda i: (i, 0))
    full = lambda a, b: pl.BlockSpec((a, b), lambda i: (0, 0))
    return pl.pallas_call(
        body, name=name, grid=(T // tm,),
        in_specs=[pl.BlockSpec((tm, QR), lambda i: (i, PCQ0 // QR)), pl.BlockSpec((tm, KR), lambda i: (i, PCKV0 // KR)),
                  row(1), full(1, LANE), full(1, QR), full(QR, P), full(1, KR), full(KR, 2 * P), row(P), row(P), row(P)],
        out_specs=[row(P), row(2 * P), row(QR), row(KR), row(LANE), full(1, QR), full(1, KR)],
        out_shape=[_sds((T, P), MXU_DTYPE), _sds((T, 2 * P), MXU_DTYPE), _sds((T, QR), MXU_DTYPE),
                   _sds((T, KR), MXU_DTYPE), _sds((T, LANE), MXU_DTYPE), _sds((1, QR), F32), _sds((1, KR), F32)],
        compiler_params=_cparams(("arbitrary",)),
    )(proj, proj, pos, invf, q_norm.reshape(1, QR), w_uq_p, kv_norm.reshape(1, KR), w_ukv_p, dq, dk, dv)


ATTN_SCALE = 1.0 / math.sqrt(MLA_QK)


def _causal_mask(i, j, blk):
    row = lax.broadcasted_iota(jnp.int32, (blk, blk), 0)
    col = lax.broadcasted_iota(jnp.int32, (blk, blk), 1)
    return col <= row + (i - j) * blk


def _hosting(hosted, grid, n_in, n_out, n_scratch):
    if hosted is None:
        return (lambda body: body), (), [], [], []
    hi, ho = len(hosted.inputs), len(hosted.out_shapes)

    def wrap(body):
        def full(*refs):
            ins, rest = refs[:n_in + hi], refs[n_in + hi:]
            outs, scr = rest[:n_out + ho], rest[n_out + ho:]
            parts = ins[n_in:], outs[n_out:], scr[n_scratch:]
            ids = [pl.program_id(d) for d in range(len(grid))]
            step = ids[0]
            for d in range(1, len(grid)):
                step = step * grid[d] + ids[d]
            total = math.prod(grid)

            @pl.when(step == 0)
            def _():
                hosted.start(*parts)

            body(*ins[:n_in], *outs[:n_out], *scr[:n_scratch])

            @pl.when(step == total // 2)
            def _():
                hosted.relay(*parts)

            @pl.when(step == total - 1)
            def _():
                hosted.finish(*parts)

        return full

    hbm = pl.BlockSpec(memory_space=pl.ANY)
    return wrap, tuple(hosted.inputs), [hbm] * ho, list(hosted.out_shapes), list(hosted.sems)


def flash_fwd(q, k, v, S, *, name, blk=1024, hosted=None):
    T, P = q.shape
    blk = _pick(S // 2, (blk, 256, 128))
    B, nq, H, W = T // S, S // (2 * blk), MLA_HEADS, HEAD_W
    grid = (B, H, nq)
    wrap, h_in, h_ospecs, h_oshapes, h_scratch = _hosting(hosted, grid, 3, 2, 0)

    def body(q_ref, k_ref, v_ref, o_ref, lse_ref):
        i = pl.program_id(2)
        q_up, q_lo = q_ref[:blk, :], q_ref[blk:, :]

        def online(qv, kv, vv, carry, masked):
            m_prev, l_prev, acc = carry
            s = _dot_nt(qv, kv)
            if masked:
                s = jnp.where(_causal_mask(0, 0, blk), s, -jnp.inf)
            m_new = jnp.maximum(m_prev, jnp.max(s, axis=1, keepdims=True))
            p = jnp.exp(s - m_new)
            alpha = jnp.exp(m_prev - m_new)
            return (m_new, alpha * l_prev + jnp.sum(p, axis=1, keepdims=True), alpha * acc + _dot(p, vv))

        def keys(j):
            rows = pl.ds(pl.multiple_of(j * blk, blk), blk)
            return k_ref[rows, :], v_ref[rows, :]

        def both(j, carry):
            kv, vv = keys(j)
            return online(q_up, kv, vv, carry[0], False), online(q_lo, kv, vv, carry[1], False)

        init = (jnp.full((blk, 1), -jnp.inf, F32), jnp.zeros((blk, 1), F32), jnp.zeros((blk, W), F32))
        up, lo = lax.fori_loop(0, 2 * i, both, (init, init))
        kv, vv = keys(2 * i)
        up = online(q_up, kv, vv, up, True)
        lo = online(q_lo, kv, vv, lo, False)
        kv, vv = keys(2 * i + 1)
        lo = online(q_lo, kv, vv, lo, True)
        for rows, (m, l, acc) in ((slice(0, blk), up), (slice(blk, 2 * blk), lo)):
            o_ref[rows, :] = acc / l
            lse_ref[rows, :] = jnp.broadcast_to(m + jnp.log(l), (blk, W))

    qmap = lambda b, h, i: (b * nq + i, h)
    kmap = lambda b, h, i: (b, h)
    hbm = pl.BlockSpec(memory_space=pl.ANY)
    return pl.pallas_call(
        wrap(body), name=name, grid=grid,
        in_specs=[pl.BlockSpec((2 * blk, W), qmap), pl.BlockSpec((S, W), kmap), pl.BlockSpec((S, W), kmap)] + [hbm] * len(h_in),
        out_specs=[pl.BlockSpec((2 * blk, W), qmap), pl.BlockSpec((2 * blk, W), qmap)] + h_ospecs,
        out_shape=[_sds((T, P), F32), _sds((T, P), F32)] + h_oshapes,
        scratch_shapes=h_scratch,
        compiler_params=_cparams(("arbitrary",) * 3 if hosted else ("parallel", "parallel", "arbitrary")),
    )(q, k, v, *h_in)


def flash_bwd(q, k, v, o, lse, dmix, S, *, name, blk=1024, hosted=None):
    T, P = q.shape
    blk = _pick(S, (blk, 256, 128))
    B, nq, H, W = T // S, S // blk, MLA_HEADS, HEAD_W
    off = (SSD_WIDTH + POOL_WIDTH) // W
    grid = (B, H, nq)
    wrap, h_in, h_ospecs, h_oshapes, h_scratch = _hosting(hosted, grid, 6, 3, 1)

    def body(q_ref, k_ref, v_ref, o_ref, lse_ref, do_ref, dq_ref, dk_ref, dv_ref, delta_s):
        j = pl.program_id(2)

        @pl.when(j == 0)
        def _():
            for i in range(nq):
                rows = slice(i * blk, (i + 1) * blk)
                delta_s[rows, :] = jnp.sum(do_ref[rows, :] * o_ref[rows, :], axis=1, keepdims=True)
                dq_ref[rows, :] = jnp.zeros((blk, W), F32)

        kv, vv = k_ref[...], v_ref[...]

        def step(i, carry, masked):
            dk, dv = carry
            rows = pl.ds(pl.multiple_of(i * blk, blk), blk)
            qv, do = q_ref[rows, :], do_ref[rows, :]
            p = jnp.exp(_dot_nt(qv, kv) - lse_ref[rows, 0:1])
            if masked:
                p = jnp.where(_causal_mask(0, 0, blk), p, 0.0)
            ds = p * (_dot_nt(do, vv) - delta_s[rows, :])
            dq_ref[rows, :] += _dot(ds, kv)
            return dk + _dot_tn(ds, qv), dv + _dot_tn(p, do)

        zero = jnp.zeros((blk, W), F32)
        carry = step(j, (zero, zero), True)
        dk, dv = lax.fori_loop(j + 1, nq, lambda i, c: step(i, c, False), carry)
        dk_ref[...] = dk
        dv_ref[...] = dv

    full = lambda b, h, j: (b, h)
    kmap = lambda b, h, j: (b * nq + j, h)
    hbm = pl.BlockSpec(memory_space=pl.ANY)
    return pl.pallas_call(
        wrap(body), name=name, grid=grid,
        in_specs=[pl.BlockSpec((S, W), full), pl.BlockSpec((blk, W), kmap), pl.BlockSpec((blk, W), kmap),
                  pl.BlockSpec((S, W), full), pl.BlockSpec((S, W), full),
                  pl.BlockSpec((S, W), lambda b, h, j: (b, off + h))] + [hbm] * len(h_in),
        out_specs=[pl.BlockSpec((S, W), full), pl.BlockSpec((blk, W), kmap), pl.BlockSpec((blk, W), kmap)] + h_ospecs,
        out_shape=[_sds((T, P), F32)] * 3 + h_oshapes,
        scratch_shapes=[pltpu.VMEM((S, 1), F32)] + h_scratch,
        compiler_params=_cparams(("arbitrary",) * 3 if hosted else ("parallel", "parallel", "arbitrary")),
    )(q, k, v, o, lse, dmix, *h_in)


SSD_PAIRS = SSD_HEADS // 2
PAIRS_PER_GROUP = SSD_PAIRS // SSD_GROUPS
GN = SSD_GROUPS * SSD_STATE


def _log1p_small(e):
    return jnp.where(e < 1e-3, e * (1.0 - e * (0.5 - e / 3.0)), jnp.log(1.0 + e))


def _softplus(v):
    return jnp.maximum(v, 0.0) + _log1p_small(jnp.exp(-jnp.abs(v)))


def _ssd_decay(dt_raw, dtb, alog):
    L = dt_raw.shape[0]
    pre = dt_raw + dtb
    dt = _softplus(pre)
    a = -jnp.exp(alog)
    row = lax.broadcasted_iota(jnp.int32, (L, L), 0)
    col = lax.broadcasted_iota(jnp.int32, (L, L), 1)
    tri = row >= col
    cum = _dot_hi(tri.astype(F32), dt * a)
    return pre, dt, a, tri, cum, cum.T


def _col(m, h):
    return m[:, h:h + 1]


def _pair_sel(m, k, lo):
    return jnp.where(lo, _col(m, 2 * k), _col(m, 2 * k + 1))


def _ssd_specs(S):
    L = SSD_CHUNK
    nc = S // L
    return L, nc


def ssd_fwd(proj, xc, dtb, alog, dchan, normw, S, *, name, hosted=None):
    T = proj.shape[0]
    L, nc = _ssd_specs(S)
    B, W, N = T // S, SSD_WIDTH, SSD_STATE
    wrap, h_in, h_ospecs, h_oshapes, h_scratch = _hosting(hosted, (B, nc), 9, 3, 1)

    def body(xs_ref, bs_ref, cs_ref, dt_ref, z_ref, dtb_ref, alog_ref, dch_ref, nw_ref, y_ref, ys_ref, hin_ref, st):
        @pl.when(pl.program_id(1) == 0)
        def _():
            st[...] = jnp.zeros(st.shape, F32)

        hin_ref[...] = st[...]
        _, dt, a, tri, cum, cum_t = _ssd_decay(dt_ref[...], dtb_ref[...], alog_ref[...])
        last = cum[L - 1:L, :]
        lo = lax.broadcasted_iota(jnp.int32, (1, LANE), 1) < SSD_HEAD_DIM
        for g in range(SSD_GROUPS):
            bm = bs_ref[:, g * N:(g + 1) * N]
            cm = cs_ref[:, g * N:(g + 1) * N]
            bm_t = bm.T
            gmat = _dot_nt(cm, bm)
            for kk in range(PAIRS_PER_GROUP):
                k = g * PAIRS_PER_GROUP + kk
                sl = slice(k * LANE, (k + 1) * LANE)
                xv = xs_ref[:, sl]
                xdt = xv * _pair_sel(dt, k, lo)
                cum_cols = [jnp.broadcast_to(_col(cum, h), (L, LANE)) for h in (2 * k, 2 * k + 1)]
                cum_sel = jnp.where(lo, cum_cols[0], cum_cols[1])
                last_sel = _pair_sel(last, k, lo)
                yd = []
                for j, h in enumerate((2 * k, 2 * k + 1)):
                    gam = jnp.exp(jnp.where(tri, cum_cols[j] - cum_t[h:h + 1, :], -jnp.inf))
                    yd.append(_dot(gmat * gam, xdt))
                hp = st[:, sl]
                y_off = _dot(cm, hp) * jnp.exp(cum_sel)
                y_ref[:, sl] = jnp.where(lo, yd[0], yd[1]) + y_off + xv * dch_ref[:, sl]
                zmat = xdt * jnp.exp(last_sel - cum_sel)
                st[:, sl] = hp * jnp.exp(last_sel) + _dot(bm_t, zmat)
        y = y_ref[...]
        z = z_ref[...]
        yz = y * (z * _sigmoid(z))
        ys_ref[...] = _rms_tile(yz, nw_ref[...]).astype(ys_ref.dtype)

    r = lambda b, c: b * nc + c
    vec = lambda w: pl.BlockSpec((1, w), lambda b, c: (0, 0))
    hbm = pl.BlockSpec(memory_space=pl.ANY)
    return pl.pallas_call(
        wrap(body), name=name, grid=(B, nc),
        in_specs=[pl.BlockSpec((L, W), lambda b, c: (r(b, c), 0)),
                  pl.BlockSpec((L, GN), lambda b, c: (r(b, c), W // GN)),
                  pl.BlockSpec((L, GN), lambda b, c: (r(b, c), W // GN + 1)),
                  pl.BlockSpec((L, LANE), lambda b, c: (r(b, c), PDT0 // LANE)),
                  pl.BlockSpec((L, W), lambda b, c: (r(b, c), PZ0 // W)),
                  vec(LANE), vec(LANE), vec(W), vec(W)] + [hbm] * len(h_in),
        out_specs=[pl.BlockSpec((L, W), lambda b, c: (r(b, c), 0)), pl.BlockSpec((L, W), lambda b, c: (r(b, c), 0)),
                   pl.BlockSpec((N, W), lambda b, c: (r(b, c), 0))] + h_ospecs,
        out_shape=[_sds((T, W), F32), _sds((T, W), MXU_DTYPE), _sds((T // L * N, W), F32)] + h_oshapes,
        scratch_shapes=[pltpu.VMEM((N, W), F32)] + h_scratch,
        compiler_params=_cparams(("arbitrary", "arbitrary") if hosted else ("parallel", "arbitrary")),
    )(xc, xc, xc, proj, proj, dtb, alog, dchan, normw, *h_in)


def ssd_bwd(proj, xc, ypre, hin, dmix, dtb, alog, dchan, normw, S, *, name):
    T = proj.shape[0]
    L, nc = _ssd_specs(S)
    B, W, N = T // S, SSD_WIDTH, SSD_STATE

    def body(xs_ref, bs_ref, cs_ref, dt_ref, z_ref, y_ref, hin_ref, dys_ref, dtb_ref, alog_ref, dch_ref, nw_ref,
             dxc_ref, ddt_ref, dz_ref, sm_ref, dnw_ref, dst):
        step = pl.program_id(0) * nc + pl.program_id(1)

        @pl.when(pl.program_id(1) == 0)
        def _():
            dst[...] = jnp.zeros(dst.shape, F32)

        pre, dt, a, tri, cum, cum_t = _ssd_decay(dt_ref[...], dtb_ref[...], alog_ref[...])
        last = cum[L - 1:L, :]
        e_last = jnp.exp(last)
        lane = lax.broadcasted_iota(jnp.int32, (1, LANE), 1)
        sub = lax.broadcasted_iota(jnp.int32, (LANE, 1), 0)
        lo = lane < SSD_HEAD_DIM
        is_last_row = sub == L - 1
        tri_t = (lax.broadcasted_iota(jnp.int32, (L, L), 0) <= lax.broadcasted_iota(jnp.int32, (L, L), 1))

        y, z, nw = y_ref[...], z_ref[...], nw_ref[...]
        sg = _sigmoid(z)
        gate = z * sg
        dyz, dnw = _rms_bwd_tile(y * gate, nw, dys_ref[...])
        dy_all = dyz * gate
        dz_ref[...] = (dyz * y * (sg * (1.0 + z * (1.0 - sg)))).astype(dz_ref.dtype)

        d_cum = jnp.zeros((L, LANE), F32)
        d_cum_t = jnp.zeros((LANE, L), F32)
        d_dt = jnp.zeros((L, LANE), F32)
        d_dskip = jnp.zeros((1, LANE), F32)
        for g in range(SSD_GROUPS):
            bm = bs_ref[:, g * N:(g + 1) * N]
            cm = cs_ref[:, g * N:(g + 1) * N]
            cm_t = cm.T
            gmat = _dot_nt(cm, bm)
            gmat_t = _dot_nt(bm, cm)
            d_g = jnp.zeros((L, L), F32)
            d_bm = jnp.zeros((L, N), F32)
            d_cm = jnp.zeros((L, N), F32)
            for kk in range(PAIRS_PER_GROUP):
                k = g * PAIRS_PER_GROUP + kk
                sl = slice(k * LANE, (k + 1) * LANE)
                xv = xs_ref[:, sl]
                dyv = dy_all[:, sl]
                dt_sel = _pair_sel(dt, k, lo)
                xdt = xv * dt_sel
                hp = hin_ref[:, sl]
                dh_out = dst[:, sl]
                cum_cols = [jnp.broadcast_to(_col(cum, h), (L, LANE)) for h in (2 * k, 2 * k + 1)]
                cum_sel = jnp.where(lo, cum_cols[0], cum_cols[1])
                last_sel = _pair_sel(last, k, lo)
                e_sel = jnp.exp(cum_sel)
                w_sel = jnp.exp(last_sel - cum_sel)
                e_lane = jnp.exp(last_sel)
                y_off = _dot(cm, hp) * e_sel
                zmat = xdt * w_sel
                d_z = _dot(bm, dh_out)
                d_bm = d_bm + _dot_nt(zmat, dh_out)
                d_xdt = d_z * w_sel
                dw_full = d_z * zmat
                hh = dh_out * hp
                d_r = dyv * e_sel
                d_cm = d_cm + _dot_nt(d_r, hp)
                dst[:, sl] = dh_out * e_lane + _dot(cm_t, d_r)
                dyoff_full = dyv * y_off
                for j, h in enumerate((2 * k, 2 * k + 1)):
                    mine = lo if j == 0 else jnp.logical_not(lo)
                    hot = lane == h
                    dyh = jnp.where(mine, dyv, 0.0)
                    gam = jnp.exp(jnp.where(tri, cum_cols[j] - cum_t[h:h + 1, :], -jnp.inf))
                    gam_t = jnp.exp(jnp.where(tri_t, cum_t[h:h + 1, :] - cum_cols[j], -jnp.inf))
                    mx = gmat * gam
                    d_xdt = d_xdt + _dot(gmat_t * gam_t, dyh)
                    d_mx = jnp.where(tri, _dot_nt(dyh, xdt), 0.0)
                    d_g = d_g + d_mx * gam
                    d_seg = d_mx * mx
                    row_l = jnp.sum(d_seg + jnp.where(mine, dyoff_full - dw_full, 0.0), axis=1, keepdims=True)
                    at_end = (jnp.sum(jnp.where(mine, dw_full, 0.0), keepdims=True)
                              + jnp.sum(jnp.where(mine, hh, 0.0), keepdims=True) * _col(e_last, h))
                    d_cum = d_cum + jnp.where(hot, row_l + jnp.where(is_last_row, at_end, 0.0), 0.0)
                    d_cum_t = d_cum_t - jnp.where(sub == h, jnp.sum(d_seg, axis=0, keepdims=True), 0.0)
                    d_dskip = d_dskip + jnp.where(hot, jnp.sum(jnp.where(mine, dyv * xv, 0.0), keepdims=True), 0.0)
                for j, h in enumerate((2 * k, 2 * k + 1)):
                    mine = lo if j == 0 else jnp.logical_not(lo)
                    d_dt = d_dt + jnp.where(lane == h, jnp.sum(jnp.where(mine, d_xdt * xv, 0.0), axis=1, keepdims=True), 0.0)
                dxc_ref[:, sl] = d_xdt * dt_sel + dyv * dch_ref[:, sl]
            dxc_ref[:, W + g * N:W + (g + 1) * N] = d_bm + _dot_tn(d_g, cm)
            dxc_ref[:, W + GN + g * N:W + GN + (g + 1) * N] = d_cm + _dot(d_g, bm)

        d_cum = d_cum + d_cum_t.T
        d_da = _dot_hi(jnp.logical_not(tri).astype(F32) + (lax.broadcasted_iota(jnp.int32, (L, L), 0)
                                                              == lax.broadcasted_iota(jnp.int32, (L, L), 1)).astype(F32), d_cum)
        d_dt = d_dt + d_da * a
        heads = lane < SSD_HEADS
        d_pre = jnp.where(heads, d_dt * _sigmoid(pre), 0.0)
        ddt_ref[...] = d_pre.astype(ddt_ref.dtype)
        d_alog = jnp.sum(d_da * dt, axis=0, keepdims=True) * a
        part = jnp.concatenate([jnp.where(heads, d_alog, 0.0), jnp.sum(d_pre, axis=0, keepdims=True), d_dskip,
                                jnp.zeros((5, LANE), F32)], axis=0)

        @pl.when(step == 0)
        def _():
            sm_ref[...] = part
            dnw_ref[...] = dnw

        @pl.when(step > 0)
        def _():
            sm_ref[...] += part
            dnw_ref[...] += dnw

    r = lambda b, c: b * nc + (nc - 1 - c)
    vec = lambda w: pl.BlockSpec((1, w), lambda b, c: (0, 0))
    blk = lambda w, j: pl.BlockSpec((L, w), lambda b, c: (r(b, c), j))
    return pl.pallas_call(
        body, name=name, grid=(B, nc),
        in_specs=[blk(W, 0), blk(GN, W // GN), blk(GN, W // GN + 1), blk(LANE, PDT0 // LANE), blk(W, PZ0 // W),
                  blk(W, 0), pl.BlockSpec((N, W), lambda b, c: (r(b, c), 0)), blk(W, 0),
                  vec(LANE), vec(LANE), vec(W), vec(W)],
        out_specs=[blk(SSD_CONV_CH, 0), blk(LANE, 0), blk(W, 0), pl.BlockSpec((8, LANE), lambda b, c: (0, 0)), vec(W)],
        out_shape=[_sds((T, SSD_CONV_CH), F32), _sds((T, LANE), MXU_DTYPE), _sds((T, W), MXU_DTYPE),
                   _sds((8, LANE), F32), _sds((1, W), F32)],
        scratch_shapes=[pltpu.VMEM((N, W), F32)],
        compiler_params=_cparams(("arbitrary", "arbitrary")),
    )(xc, xc, xc, proj, proj, ypre, hin, dmix, dtb, alog, dchan, normw)


def _adamw_math(w, g, m, v):
    m = ADAM_B1 * m + (1.0 - ADAM_B1) * g
    v = ADAM_B2 * v + (1.0 - ADAM_B2) * (g * g)
    m_hat = m / (1.0 - ADAM_B1 ** ADAM_STEP)
    v_hat = v / (1.0 - ADAM_B2 ** ADAM_STEP)
    delta = -ADAM_LR * (m_hat / (jnp.sqrt(v_hat) + ADAM_EPS) + ADAM_WD * w)
    return delta, m, v


def adamw_layers(w, g_layers, m, v, *, name, tr=256):
    L, A, B = w.shape
    tr = _pick(A, (tr, 192, 176, 128, 64, 32, 16, 8))
    na = A // tr
    n = len(g_layers[0])

    def body(*refs):
        w_ref, m_ref, v_ref = refs[0], refs[1 + L * n], refs[2 + L * n]
        g_ref, d_ref, nm_ref, nv_ref = refs[3 + L * n:]
        layer = pl.program_id(0)
        g = None
        for l in range(L):
            parts = refs[1 + l * n:1 + (l + 1) * n]
            gl = parts[0][...]
            for p in parts[1:]:
                gl = gl + p[...]
            g = gl if g is None else jnp.where(layer == l, gl, g)
        g_ref[...] = g
        d_ref[...], nm_ref[...], nv_ref[...] = _adamw_math(w_ref[...], g, m_ref[...], v_ref[...])

    def g_spec(l):
        return pl.BlockSpec((tr, B), lambda layer, i: (jnp.where(layer == l, i, jnp.where(layer < l, 0, na - 1)), 0))

    spec = pl.BlockSpec((None, tr, B), lambda layer, i: (layer, i, 0))
    return pl.pallas_call(
        body, name=name, grid=(L, na), in_specs=[spec] + [g_spec(l) for l in range(L) for _ in range(n)] + [spec] * 2,
        out_specs=[spec] * 4, out_shape=[_sds((L, A, B), F32)] * 4, compiler_params=_cparams(("arbitrary", "arbitrary")),
    )(w, *[p for parts in g_layers for p in parts], m, v)


def adamw_small(ws, gs, ms, vs, *, name):
    n = len(ws)

    def body(*refs):
        w_refs, g_refs, m_refs, v_refs = (refs[i * n:(i + 1) * n] for i in range(4))
        d_refs, nm_refs, nv_refs = (refs[(4 + i) * n:(5 + i) * n] for i in range(3))
        for a in range(n):
            d_refs[a][...], nm_refs[a][...], nv_refs[a][...] = _adamw_math(
                w_refs[a][...], g_refs[a][...], m_refs[a][...], v_refs[a][...])

    vm = pl.BlockSpec(memory_space=pltpu.VMEM)
    out = pl.pallas_call(
        body, name=name, in_specs=[vm] * (4 * n), out_specs=[vm] * (3 * n),
        out_shape=[_sds(w.shape, F32) for w in ws] * 3, compiler_params=pltpu.CompilerParams(vmem_limit_bytes=VMEM_LIMIT),
    )(*ws, *gs, *ms, *vs)
    return out[:n], out[n:2 * n], out[2 * n:]


def _my_place():
    return lax.axis_index("x"), lax.axis_index("y"), lax.axis_index("c")


def _other_chips(x, y):
    return [(1 - x, y), (x, 1 - y), (1 - x, 1 - y)]


def relation_of(chip, me):
    d = chip ^ me
    return jnp.where(d == 2, 0, jnp.where(d == 1, 1, jnp.where(d == 3, 2, -1)))


class Exchange(NamedTuple):
    inputs: tuple
    out_shapes: tuple
    sems: tuple
    start: Callable
    relay: Callable
    finish: Callable


def scatter_exchange(srcs):
    n = len(srcs)

    def copies(ins, outs, sems):
        x, y, c = _my_place()
        out = []
        for k, (px, py) in enumerate(_other_chips(x, y)):
            for a in range(n):
                out.append(pltpu.make_async_remote_copy(
                    src_ref=ins[a].at[2 * px + py], dst_ref=outs[a].at[k], send_sem=sems[0].at[k, a],
                    recv_sem=sems[1].at[k, a], device_id=(px, py, c), device_id_type=pl.DeviceIdType.MESH))
        return out

    def start(ins, outs, sems):
        for cp in copies(ins, outs, sems):
            cp.start()

    def finish(ins, outs, sems):
        cps = copies(ins, outs, sems)
        for cp in cps:
            cp.wait_recv()
        for cp in cps:
            cp.wait_send()

    return Exchange(tuple(srcs), tuple(_sds((3,) + s.shape[1:], s.dtype) for s in srcs),
                    (pltpu.SemaphoreType.DMA((3, n)),) * 2, start, lambda *a: None, finish)


def run_exchange(ex, *, name):
    n_in, n_out = len(ex.inputs), len(ex.out_shapes)

    def body(*refs):
        parts = refs[:n_in], refs[n_in:n_in + n_out], refs[n_in + n_out:]
        ex.start(*parts)
        ex.relay(*parts)
        ex.finish(*parts)

    hbm = pl.BlockSpec(memory_space=pl.ANY)
    return pl.pallas_call(
        body, name=name, in_specs=[hbm] * n_in, out_specs=[hbm] * n_out, out_shape=list(ex.out_shapes),
        scratch_shapes=list(ex.sems), compiler_params=pltpu.CompilerParams(has_side_effects=True),
    )(*ex.inputs)


def sibling_swap(srcs, *, name):
    n = len(srcs)

    def body(*refs):
        src_refs, out_refs, (send_sems, recv_sems) = refs[:n], refs[n:2 * n], refs[2 * n:]
        x, y, c = _my_place()
        copies = [pltpu.make_async_remote_copy(
            src_ref=src_refs[a], dst_ref=out_refs[a], send_sem=send_sems.at[a], recv_sem=recv_sems.at[a],
            device_id=(x, y, 1 - c), device_id_type=pl.DeviceIdType.MESH) for a in range(n)]
        for cp in copies:
            cp.start()
        for cp in copies:
            cp.wait_recv()
        for cp in copies:
            cp.wait_send()

    hbm = pl.BlockSpec(memory_space=pl.ANY)
    return pl.pallas_call(
        body, name=name, in_specs=[hbm] * n, out_specs=[hbm] * n, out_shape=[_sds(s.shape, s.dtype) for s in srcs],
        scratch_shapes=[pltpu.SemaphoreType.DMA((n,)), pltpu.SemaphoreType.DMA((n,))],
        compiler_params=pltpu.CompilerParams(has_side_effects=True),
    )(*srcs)


def gather_exchange(srcs):
    nch = len(srcs)
    halves = [s.shape[0] // 2 for s in srcs]
    assert all(2 * h == s.shape[0] and h % 16 == 0 for h, s in zip(halves, srcs))
    pieces = [(k, q) for k in range(3) for q in range(nch)]

    def makers(ins, outs, sems):
        ici_send, ici_recv, d2d_send, d2d_recv = sems
        x, y, c = _my_place()
        peers = _other_chips(x, y)

        def rows(core, q):
            return pl.ds(core * halves[q], halves[q])

        def ici(k, q):
            px, py = peers[k]
            return pltpu.make_async_remote_copy(
                src_ref=ins[q].at[rows(c, q)], dst_ref=outs[q].at[k, rows(c, q)], send_sem=ici_send.at[k, q],
                recv_sem=ici_recv.at[k, q], device_id=(px, py, c), device_id_type=pl.DeviceIdType.MESH)

        def d2d(k, q, core):
            return pltpu.make_async_remote_copy(
                src_ref=outs[q].at[k, rows(core, q)], dst_ref=outs[q].at[k, rows(core, q)],
                send_sem=d2d_send.at[k, q], recv_sem=d2d_recv.at[k, q], device_id=(x, y, 1 - c),
                device_id_type=pl.DeviceIdType.MESH)

        return ici, d2d, c

    def start(*refs):
        ici, _, _ = makers(*refs)
        for k, q in pieces:
            ici(k, q).start()

    def relay(*refs):
        ici, d2d, c = makers(*refs)
        for k, q in pieces:
            ici(k, q).wait_recv()
            d2d(k, q, c).start()

    def finish(*refs):
        ici, d2d, c = makers(*refs)
        for k, q in pieces:
            d2d(k, q, 1 - c).wait_recv()
        for k, q in pieces:
            ici(k, q).wait_send()
            d2d(k, q, c).wait_send()

    return Exchange(tuple(srcs), tuple(_sds((3,) + s.shape, s.dtype) for s in srcs),
                    (pltpu.SemaphoreType.DMA((3, nch)),) * 4, start, relay, finish)


def all_sum_small(vec, *, name):
    R, C = vec.shape

    def body(v_ref, out_ref, buf, send_sems, recv_sems):
        x, y, c = _my_place()
        me = 4 * x + 2 * y + c
        buf[me] = v_ref[...]
        copies = []
        for k in range(1, N_DEV):
            px, py, pc = x ^ (k >> 2), y ^ ((k >> 1) & 1), c ^ (k & 1)
            copies.append(pltpu.make_async_remote_copy(
                src_ref=v_ref, dst_ref=buf.at[me], send_sem=send_sems.at[k - 1], recv_sem=recv_sems.at[k - 1],
                device_id=(px, py, pc), device_id_type=pl.DeviceIdType.MESH))
        for cp in copies:
            cp.start()
        for k in range(1, N_DEV):
            px, py, pc = x ^ (k >> 2), y ^ ((k >> 1) & 1), c ^ (k & 1)
            pltpu.make_async_remote_copy(
                src_ref=v_ref, dst_ref=buf.at[4 * px + 2 * py + pc], send_sem=send_sems.at[k - 1],
                recv_sem=recv_sems.at[k - 1], device_id=(px, py, pc), device_id_type=pl.DeviceIdType.MESH).wait_recv()
        for cp in copies:
            cp.wait_send()
        acc = buf[0]
        for d in range(1, N_DEV):
            acc = acc + buf[d]
        out_ref[...] = acc

    return pl.pallas_call(
        body, name=name, in_specs=[pl.BlockSpec(memory_space=pltpu.VMEM)], out_specs=pl.BlockSpec(memory_space=pltpu.VMEM),
        out_shape=_sds((R, C), F32),
        scratch_shapes=[pltpu.VMEM((N_DEV, R, C), F32), pltpu.SemaphoreType.DMA((N_DEV - 1,)),
                        pltpu.SemaphoreType.DMA((N_DEV - 1,))],
        compiler_params=pltpu.CompilerParams(has_side_effects=True, vmem_limit_bytes=VMEM_LIMIT),
    )(vec)


def sum_chips(own, others, *, name, tr=512):
    R, C = own.shape
    tr = _pick(R, (tr, 384, 352, 256, 128, 64, 32, 16))

    def body(o_ref, p_ref, s_ref):
        acc = o_ref[...].astype(F32)
        for k in range(3):
            acc = acc + p_ref[k].astype(F32)
        s_ref[...] = acc

    return pl.pallas_call(
        body, name=name, grid=(R // tr,),
        in_specs=[pl.BlockSpec((tr, C), lambda i: (i, 0)), pl.BlockSpec((3, tr, C), lambda i: (0, i, 0))],
        out_specs=pl.BlockSpec((tr, C), lambda i: (i, 0)), out_shape=_sds((R, C), F32),
        compiler_params=_cparams(("parallel",)),
    )(own, others)


WEIGHTS = ['attn_norm', 'w_in', 'ssd_conv_w', 'ssd_conv_b', 'ssd_dt_bias', 'ssd_a_log', 'ssd_d', 'ssd_norm', 'pool_w',
           'pool_scale', 'mla_q_norm', 'mla_w_uq', 'mla_kv_norm', 'mla_w_ukv', 'w_out', 'ffn_norm', 'ffn_w_up',
           'ffn_conv_w', 'ffn_conv_b', 'ffn_w_down', 'final_norm']
BIG = {'w_in': 2, 'mla_w_uq': 2, 'mla_w_ukv': 2, 'w_out': 1, 'ffn_w_up': 2, 'ffn_w_down': 1}
CONV_SHARDED = ('ssd_conv_w', 'ffn_conv_w')


def _zeros_cols(w, n):
    return jnp.zeros((w.shape[0], n), w.dtype)


def _w_in_to_padded(w):
    return jnp.concatenate([w[:, 0:2560], w[:, 2576:3088], w[:, 3088:3472], w[:, 2560:2576], _zeros_cols(w, 112),
                            w[:, 3472:3728], _zeros_cols(w, 64), w[:, 3728:3760], _zeros_cols(w, 32 + 128)], axis=1)


def _w_in_from_padded(g):
    return jnp.concatenate([g[:, 0:2560], g[:, PDT0:PDT0 + SSD_HEADS], g[:, PU0:PU0 + POOL_WIDTH],
                            g[:, PCQ0:PCQ0 + MLA_Q_RANK], g[:, PCKV0:PCKV0 + MLA_KV_RANK],
                            g[:, PKPE0 + ROPE0:PKPE0 + ROPE0 + MLA_ROPE]], axis=1)


def _w_uq_to_padded(w):
    r = w.reshape(MLA_Q_RANK, MLA_HEADS, MLA_QK)
    return jnp.pad(r, ((0, 0), (0, 0), (0, HEAD_W - MLA_QK))).reshape(MLA_Q_RANK, MLA_PAD)


def _w_uq_from_padded(g):
    return g.reshape(MLA_Q_RANK, MLA_HEADS, HEAD_W)[:, :, :MLA_QK].reshape(MLA_Q_RANK, MLA_HEADS * MLA_QK)


def _w_ukv_to_padded(w):
    r = w.reshape(MLA_KV_RANK, MLA_HEADS, MLA_NOPE + MLA_V)
    pad = lambda t: jnp.pad(t, ((0, 0), (0, 0), (0, HEAD_W - t.shape[2]))).reshape(MLA_KV_RANK, MLA_PAD)
    return jnp.concatenate([pad(r[:, :, :MLA_NOPE]), pad(r[:, :, MLA_NOPE:])], axis=1)


def _w_ukv_from_padded(g):
    kk = g[:, :MLA_PAD].reshape(MLA_KV_RANK, MLA_HEADS, HEAD_W)[:, :, :MLA_NOPE]
    vv = g[:, MLA_PAD:].reshape(MLA_KV_RANK, MLA_HEADS, HEAD_W)[:, :, :MLA_V]
    return jnp.concatenate([kk, vv], axis=2).reshape(MLA_KV_RANK, MLA_HEADS * (MLA_NOPE + MLA_V))


def _w_out_to_padded(w):
    att = w[SSD_WIDTH + POOL_WIDTH:].reshape(MLA_HEADS, MLA_V, D_MODEL)
    att = jnp.pad(att, ((0, 0), (0, HEAD_W - MLA_V), (0, 0))).reshape(MLA_PAD, D_MODEL)
    return jnp.concatenate([w[:SSD_WIDTH + POOL_WIDTH], att], axis=0)


def _w_out_from_padded(g):
    att = g[SSD_WIDTH + POOL_WIDTH:].reshape(MLA_HEADS, HEAD_W, D_MODEL)[:, :MLA_V].reshape(MLA_WIDTH, D_MODEL)
    return jnp.concatenate([g[:SSD_WIDTH + POOL_WIDTH], att], axis=0)


def _pad_lanes(v, n=LANE):
    return jnp.pad(v.reshape(1, -1), ((0, 0), (0, n - v.size)))


def _pack_rows(parts, cols, dtype, row_multiple=16):
    flat = jnp.concatenate([p.astype(dtype).reshape(-1) for p in parts])
    rows = -(-flat.size // (cols * row_multiple)) * row_multiple
    return jnp.pad(flat, (0, rows * cols - flat.size)).reshape(rows, cols)


def _unpack_rows(packed, shapes):
    flat = packed.reshape(-1)
    out, at = [], 0
    for s in shapes:
        n = math.prod(s)
        out.append(flat[at:at + n].reshape(s))
        at += n
    return out


def _split_for_chips(g, axis):
    a, b = g.shape
    if axis == 0:
        return g.reshape(N_CHIPS, a // N_CHIPS, b)
    return g.reshape(a, N_CHIPS, b // N_CHIPS).transpose(1, 0, 2)


_MATMUL_OPERANDS = {'w_in': ('w_in_p', _w_in_to_padded), 'mla_w_uq': ('w_uq_p', _w_uq_to_padded),
                    'mla_w_ukv': ('w_ukv_p', _w_ukv_to_padded), 'w_out': ('w_out_p', _w_out_to_padded),
                    'ffn_w_up': ('w_up', lambda a: a), 'ffn_w_down': ('w_down', lambda a: a)}


def _matmul_weights(full):
    return {_MATMUL_OPERANDS[k][0]: _MATMUL_OPERANDS[k][1](a) for k, a in full.items()}


def _layer_weights(full, small, l):
    w = _matmul_weights(full)
    for k in ('attn_norm', 'ssd_conv_w', 'ssd_conv_b', 'ssd_norm', 'pool_w', 'pool_scale', 'mla_q_norm', 'mla_kv_norm',
              'ffn_norm', 'ffn_conv_w', 'ffn_conv_b'):
        w[k] = small[k][l]
    w['dtb'] = _pad_lanes(small['ssd_dt_bias'][l])
    w['alog'] = _pad_lanes(small['ssd_a_log'][l])
    w['dchan'] = jnp.repeat(small['ssd_d'][l], SSD_HEAD_DIM).reshape(1, SSD_WIDTH)
    w['ssd_norm'] = w['ssd_norm'].reshape(1, SSD_WIDTH)
    return w


def _layer_fwd(x, pos, invf, w, S, l, hosted=None, scan_hosted=None, late_weights=None, post_weights=None):
    n = lambda s: f"{s}_l{l}"
    h1 = rmsnorm_fwd(x, w['attn_norm'], name=n("attn_norm"))
    proj = matmul(h1, w['w_in_p'], name=n("w_in"))
    xc = ssd_conv_fwd(proj, w['ssd_conv_w'], w['ssd_conv_b'], S, name=n("ssd_conv"))
    ypre, yssd, hin, *arrived = ssd_fwd(proj, xc, w['dtb'], w['alog'], w['dchan'], w['ssd_norm'], S, name=n("ssd_scan"),
                                        hosted=scan_hosted)
    if late_weights:
        w = {**w, **late_weights(arrived)}
    ypool, pooled = pool_fwd(proj, w['pool_w'], w['pool_scale'], S, name=n("pool"))
    q, k, v, cqn, ckvn = mla_prep_fwd(proj, pos, invf, w['mla_q_norm'], w['w_uq_p'], w['mla_kv_norm'], w['w_ukv_p'],
                                      name=n("mla_prep"))
    o, lse, *exchanged = flash_fwd(q, k, v, S, name=n("attention"), hosted=hosted)
    if post_weights:
        w = {**w, **post_weights(exchanged)}
    mix = jnp.concatenate([yssd, ypool, o.astype(MXU_DTYPE)], axis=1)
    x2 = matmul(mix, w['w_out_p'], res=x, name=n("w_out"))
    h2 = rmsnorm_fwd(x2, w['ffn_norm'], name=n("ffn_norm"))
    up = matmul(h2, w['w_up'], name=n("ffn_up"))
    act = ffn_conv_gate_fwd(up, w['ffn_conv_w'], w['ffn_conv_b'], S, name=n("ffn_conv_gate"))
    x3 = matmul(act, w['w_down'], res=x2, name=n("ffn_down"))
    saved = dict(x=x, h1=h1, proj=proj, xc=xc, ypre=ypre, hin=hin, pooled=pooled, q=q, k=k, v=v, cqn=cqn, ckvn=ckvn,
                 o=o, lse=lse, mix=mix, x2=x2, h2=h2, up=up, act=act)
    return x3, saved, w, exchanged


def _layer_bwd(dx3, pos, invf, w, s, S, l, host=None, late_host=None):
    n = lambda t: f"{t}_l{l}"
    g = {}
    dact = matmul(dx3, w['w_down'], nt=True, name=n("d_ffn_down"))
    g['ffn_w_down'] = matmul_tn(s['act'], dx3, name=n("g_ffn_down"))
    dup_g, dup_v, st = ffn_conv_gate_bwd(s['up'], w['ffn_conv_w'], w['ffn_conv_b'], dact, S, name=n("d_ffn_conv_gate"))
    g['ffn_conv_w'], g['ffn_conv_b'] = st[:FFN_CONV], st[FFN_CONV]
    dh2 = matmul(dup_g, w['w_up'], nt=True, kblock=0, name=n("d_ffn_up_g"))
    dh2 = matmul(dup_v, w['w_up'], nt=True, kblock=1, res=dh2, name=n("d_ffn_up_v"))
    g['ffn_w_up'] = jnp.concatenate([matmul_tn(s['h2'], dup_g, name=n("g_ffn_up_g")),
                                     matmul_tn(s['h2'], dup_v, name=n("g_ffn_up_v"))], axis=1)
    dx2, gn = rmsnorm_bwd(s['x2'], w['ffn_norm'], dh2, dx3, name=n("d_ffn_norm"))
    g['ffn_norm'] = gn[0]
    dmix = matmul(dx2, w['w_out_p'], nt=True, name=n("d_w_out"))
    g['w_out'] = _w_out_from_padded(matmul_tn(s['mix'], dx2, name=n("g_w_out")))
    dxc, ddt, dz, sm, gsn = ssd_bwd(s['proj'], s['xc'], s['ypre'], s['hin'], dmix, w['dtb'], w['alog'], w['dchan'],
                                    w['ssd_norm'], S, name=n("d_ssd_scan"))
    g['ssd_a_log'], g['ssd_dt_bias'], g['ssd_d'] = sm[0, :SSD_HEADS], sm[1, :SSD_HEADS], sm[2, :SSD_HEADS]
    g['ssd_norm'] = gsn[0]
    dxbc, st = ssd_conv_bwd(s['proj'], w['ssd_conv_w'], w['ssd_conv_b'], dxc, S, name=n("d_ssd_conv"))
    g['ssd_conv_w'], g['ssd_conv_b'] = st[:SSD_CONV], st[SSD_CONV]
    du, g['pool_w'], gps = pool_bwd(dmix, s['pooled'], w['pool_w'], w['pool_scale'], S, name=n("d_pool"))
    g['pool_scale'] = gps[0]
    dq, dk, dv, *exchanged = flash_bwd(s['q'], s['k'], s['v'], s['o'], s['lse'], dmix, S, name=n("d_attention"),
                                       hosted=host(g) if host else None)
    dqp, dkvp, dcq, dckv, dkpe, gqn, gkn = mla_prep_bwd(s['proj'], pos, invf, w['mla_q_norm'], w['w_uq_p'],
                                                        w['mla_kv_norm'], w['w_ukv_p'], dq, dk, dv, name=n("d_mla_prep"))
    g['mla_q_norm'], g['mla_kv_norm'] = gqn[0], gkn[0]
    g['mla_w_uq'] = _w_uq_from_padded(matmul_tn(s['cqn'], dqp, name=n("g_w_uq")))
    g['mla_w_ukv'] = _w_ukv_from_padded(matmul_tn(s['ckvn'], dkvp, name=n("g_w_ukv")))
    dproj = jnp.concatenate([dz, dxbc, du, dcq, ddt, dckv, dkpe, jnp.zeros_like(dkpe)], axis=1)
    g['w_in'] = _w_in_from_padded(matmul_tn(s['h1'], dproj, name=n("g_w_in")))
    dh1 = matmul(dproj, w['w_in_p'], nt=True, name=n("d_w_in"), hosted=late_host(g) if late_host else None)
    dh1, late_exchanged = (dh1[0], dh1[1:]) if late_host else (dh1, [])
    dx, gn = rmsnorm_bwd(s['x'], w['attn_norm'], dh1, dx2, name=n("d_attn_norm"))
    g['attn_norm'] = gn[0]
    return dx, g, exchanged, late_exchanged


def _rope_inputs(positions):
    pos = positions.reshape(-1, 1).astype(F32)
    inv_freq = ROPE_THETA ** (-jnp.arange(0, MLA_ROPE, 2, dtype=F32) / MLA_ROPE)
    invf = jnp.concatenate([jnp.zeros((ROPE0,), F32), inv_freq, inv_freq,
                            jnp.zeros((HEAD_W - ROPE0 - MLA_ROPE,), F32)]).reshape(1, HEAD_W)
    return pos, invf


EARLY_GRADS = ('w_out', 'ffn_w_up', 'ffn_w_down')


def kernel(x, positions, attn_norm, w_in, ssd_conv_w, ssd_conv_b, ssd_dt_bias, ssd_a_log, ssd_d, ssd_norm, pool_w, pool_scale, mla_q_norm, mla_w_uq, mla_kv_norm, mla_w_ukv, w_out, ffn_norm, ffn_w_up, ffn_conv_w, ffn_conv_b, ffn_w_down, final_norm, loss_target, m_attn_norm, m_w_in, m_ssd_conv_w, m_ssd_conv_b, m_ssd_dt_bias, m_ssd_a_log, m_ssd_d, m_ssd_norm, m_pool_w, m_pool_scale, m_mla_q_norm, m_mla_w_uq, m_mla_kv_norm, m_mla_w_ukv, m_w_out, m_ffn_norm, m_ffn_w_up, m_ffn_conv_w, m_ffn_conv_b, m_ffn_w_down, m_final_norm, v_attn_norm, v_w_in, v_ssd_conv_w, v_ssd_conv_b, v_ssd_dt_bias, v_ssd_a_log, v_ssd_d, v_ssd_norm, v_pool_w, v_pool_scale, v_mla_q_norm, v_mla_w_uq, v_mla_kv_norm, v_mla_w_ukv, v_w_out, v_ffn_norm, v_ffn_w_up, v_ffn_conv_w, v_ffn_conv_b, v_ffn_w_down, v_final_norm):
    wv = dict(zip(WEIGHTS, (attn_norm, w_in, ssd_conv_w, ssd_conv_b, ssd_dt_bias, ssd_a_log, ssd_d, ssd_norm, pool_w,
                            pool_scale, mla_q_norm, mla_w_uq, mla_kv_norm, mla_w_ukv, w_out, ffn_norm, ffn_w_up,
                            ffn_conv_w, ffn_conv_b, ffn_w_down, final_norm)))
    mv = dict(zip(WEIGHTS, (m_attn_norm, m_w_in, m_ssd_conv_w, m_ssd_conv_b, m_ssd_dt_bias, m_ssd_a_log, m_ssd_d,
                            m_ssd_norm, m_pool_w, m_pool_scale, m_mla_q_norm, m_mla_w_uq, m_mla_kv_norm, m_mla_w_ukv,
                            m_w_out, m_ffn_norm, m_ffn_w_up, m_ffn_conv_w, m_ffn_conv_b, m_ffn_w_down, m_final_norm)))
    vv = dict(zip(WEIGHTS, (v_attn_norm, v_w_in, v_ssd_conv_w, v_ssd_conv_b, v_ssd_dt_bias, v_ssd_a_log, v_ssd_d,
                            v_ssd_norm, v_pool_w, v_pool_scale, v_mla_q_norm, v_mla_w_uq, v_mla_kv_norm, v_mla_w_ukv,
                            v_w_out, v_ffn_norm, v_ffn_w_up, v_ffn_conv_w, v_ffn_conv_b, v_ffn_w_down, v_final_norm)))
    Bl, S, D = x.shape
    chip = 2 * lax.axis_index("x") + lax.axis_index("y")
    core = lax.axis_index("c")

    big_names = list(BIG)
    first_names = ['w_in']
    scan_names = ['mla_w_uq', 'mla_w_ukv']
    post_names = [k for k in big_names if k not in first_names + scan_names]

    def shards(l, names):
        return [wv[k][l].astype(MXU_DTYPE) for k in names]

    def whole_weights(names, own, others):
        rel = [relation_of(j, chip) for j in range(N_CHIPS)]
        return {k: jnp.concatenate(
            [jnp.where(r < 0, mine, jnp.where(r == 0, theirs[0], jnp.where(r == 1, theirs[1], theirs[2]))) for r in rel],
            axis=BIG[k] - 1) for k, mine, theirs in zip(names, own, others)}

    first_others = run_exchange(gather_exchange(shards(0, first_names)), name="gather_w_in_l0")
    placed = []
    for k in CONV_SHARDED:
        sh = wv[k]
        whole = jnp.zeros(sh.shape[:-1] + (sh.shape[-1] * N_CHIPS,), F32)
        whole = lax.dynamic_update_slice_in_dim(whole, sh, chip * sh.shape[-1], axis=sh.ndim - 1)
        placed.append(jnp.where(core == 1, whole, 0.0))
    conv_full = _unpack_rows(all_sum_small(_pack_rows(placed, LANE, F32), name="gather_conv_weights"),
                             [p.shape for p in placed])
    small = {k: wv[k] for k in WEIGHTS if k not in BIG}
    small.update(dict(zip(CONV_SHARDED, conv_full)))

    T = Bl * S
    pos, invf = _rope_inputs(positions)
    group_a = [(k, 1) for k in big_names] + [(k, 0) for k in EARLY_GRADS]
    group_b = [(k, 0) for k in big_names if k not in EARLY_GRADS]

    def scatter_of(group, layer_grads):
        send = [_split_for_chips(layer_grads[l][k], BIG[k] - 1) for k, l in group]
        return send, scatter_exchange(send)

    layer_grads = [None] * DEPTH
    sent = {}
    w0 = _layer_weights(whole_weights(first_names, shards(0, first_names), first_others), small, 0)
    h, saved0, w0, arrived = _layer_fwd(
        x.reshape(T, D), pos, invf, w0, S, 0,
        scan_hosted=gather_exchange(shards(0, scan_names)),
        late_weights=lambda got: _matmul_weights(whole_weights(scan_names, shards(0, scan_names), got)),
        hosted=gather_exchange(shards(0, post_names) + shards(1, big_names)),
        post_weights=lambda got: _matmul_weights(whole_weights(post_names, shards(0, post_names), got[:len(post_names)])))
    others1 = arrived[len(post_names):]
    w1 = _layer_weights(whole_weights(big_names, shards(1, big_names), others1), small, 1)
    h, saved1, w1, _ = _layer_fwd(h, pos, invf, w1, S, 1)
    loss, dh, g_final_norm = final_loss(h, small['final_norm'], loss_target.reshape(T, D))
    dh, layer_grads[1], _, _ = _layer_bwd(dh, pos, invf, w1, saved1, S, 1)

    def host_a(early):
        layer_grads[0] = early
        sent['a'], ex = scatter_of(group_a, layer_grads)
        return ex

    def host_b(_):
        sent['b'], ex = scatter_of(group_b, layer_grads)
        return ex

    dx, layer_grads[0], others_a, others_b = _layer_bwd(dh, pos, invf, w0, saved0, S, 0, host=host_a, late_host=host_b)
    small_names = [k for k in WEIGHTS if k not in BIG]
    grads = {k: jnp.stack([layer_grads[l][k] for l in range(DEPTH)]) for k in small_names if k != 'final_norm'}
    grads['final_norm'] = g_final_norm[0]

    pieces = [{}, {}]
    for tag, group, others in (('a', group_a, others_a), ('b', group_b, others_b)):
        mine = [sum_chips(lax.dynamic_index_in_dim(s, chip, 0, keepdims=False), o, name=f"sum_chips_{k}_l{l}")
                for s, o, (k, l) in zip(sent[tag], others, group)]
        theirs = sibling_swap(mine, name=f"swap_core_sums_{tag}")
        pieces[0].update(dict(zip(group, mine)))
        pieces[1].update(dict(zip(group, theirs)))
    small_sum = all_sum_small(_pack_rows([grads[k] for k in small_names] + [loss[0, :1]], LANE, F32), name="sum_small_grads")
    summed = _unpack_rows(small_sum, [grads[k].shape for k in small_names] + [(1,)])
    loss_total = summed[-1].reshape(())
    g_small = dict(zip(small_names, summed[:-1]))
    for k in CONV_SHARDED:
        n = wv[k].shape[-1]
        g_small[k] = lax.dynamic_slice_in_dim(g_small[k], chip * n, n, axis=g_small[k].ndim - 1)

    out_g, out_d, out_m, out_v = {}, {}, {}, {}
    for k in big_names:
        g_layers = [[pieces[0][(k, l)], pieces[1][(k, l)]] for l in range(DEPTH)]
        out_g[k], out_d[k], out_m[k], out_v[k] = adamw_layers(wv[k], g_layers, mv[k], vv[k], name=f"adamw_{k}")
    at_least_2d = lambda a: a.reshape(1, -1) if a.ndim == 1 else a
    res = adamw_small(*[[at_least_2d(d[k]) for k in small_names] for d in (wv, g_small, mv, vv)], name="adamw_small")
    out_g.update(g_small)
    for dst, r in zip((out_d, out_m, out_v), res):
        dst.update({k: a.reshape(wv[k].shape) for k, a in zip(small_names, r)})
    return (loss_total, dx.reshape(Bl, S, D), *[out_g[k] for k in WEIGHTS], *[out_d[k] for k in WEIGHTS],
            *[out_m[k] for k in WEIGHTS], *[out_v[k] for k in WEIGHTS])
```

```python
import functools
import math
from typing import Callable, NamedTuple

import jax
import jax.numpy as jnp
from jax import lax
from jax.experimental import pallas as pl
from jax.experimental.pallas import tpu as pltpu

F32 = jnp.float32
MXU_DTYPE = jnp.bfloat16
HI = lax.Precision.HIGHEST

D_MODEL = 1024
DEPTH = 2
EPS = 1e-6
SSD_HEADS = 16
SSD_HEAD_DIM = 64
SSD_WIDTH = 1024
SSD_GROUPS = 2
SSD_STATE = 128
SSD_CONV = 4
SSD_CHUNK = 128
SSD_CONV_CH = 1536
POOL_GROUPS = 4
POOL_GROUP_DIM = 128
POOL_WIDTH = 512
POOL_WINDOWS = (2, 4, 8, 16)
MLA_HEADS = 8
MLA_Q_RANK = 384
MLA_KV_RANK = 256
MLA_NOPE = 64
MLA_ROPE = 32
MLA_V = 64
MLA_QK = 96
MLA_WIDTH = 512
ROPE_THETA = 10000.0
MIX_WIDTH = 2048
IN_COLS = 3760
D_FF = 2816
FFN_CONV = 3
ADAM_LR = 0.001
ADAM_B1 = 0.9
ADAM_B2 = 0.999
ADAM_EPS = 1e-08
ADAM_WD = 0.01
ADAM_STEP = 10

LANE = 128
HALO = 8
POOL_HALO = 16
PZ0 = 0
PXBC0 = 1024
PU0 = 2560
PCQ0 = 3072
PDT0 = 3456
PCKV0 = 3584
PKPE0 = 3840
PROJ_W = 4096
HEAD_W = 128
MLA_PAD = MLA_HEADS * HEAD_W
MIXP = SSD_WIDTH + POOL_WIDTH + MLA_PAD
N_CHIPS = 4
N_DEV = 8
VMEM_LIMIT = 56 * 1024 * 1024
MATMUL_VMEM_BUDGET = 40 * 1024 * 1024


def _cparams(dims, vmem=None):
    return pltpu.CompilerParams(dimension_semantics=dims, vmem_limit_bytes=vmem or VMEM_LIMIT)


def _sds(shape, dtype):
    return jax.ShapeDtypeStruct(tuple(shape), dtype)


def _mx(v):
    return v.astype(MXU_DTYPE)


def _dot(a, b):
    return jnp.dot(_mx(a), _mx(b), preferred_element_type=F32)


def _dot_nt(a, b):
    return lax.dot_general(_mx(a), _mx(b), (((1,), (1,)), ((), ())), preferred_element_type=F32)


def _dot_tn(a, b):
    return lax.dot_general(_mx(a), _mx(b), (((0,), (0,)), ((), ())), preferred_element_type=F32)


def _dot_hi(a, b):
    return jnp.dot(a, b, preferred_element_type=F32, precision=HI)


def _sigmoid(v):
    return 1.0 / (1.0 + jnp.exp(-v))


def _pick(n, prefs):
    for p in prefs:
        if n % p == 0:
            return p
    return n


def matmul(a, b, *, res=None, out_dtype=F32, name, nt=False, kblock=0, tm=None, tn=None, hosted=None):
    M, K = a.shape
    N = b.shape[0] if nt else b.shape[1]
    assert (b.shape[1] % K == 0) if nt else (K == b.shape[0] and kblock == 0)
    tn = tn or _pick(N, (1024, 1408, 1280, 512, 256, 128))
    a_bytes, o_bytes = jnp.dtype(a.dtype).itemsize, jnp.dtype(out_dtype).itemsize
    fits = lambda t: 2 * (t * K * a_bytes + K * tn * 2 + t * tn * (o_bytes + (4 if res is not None else 0))) <= MATMUL_VMEM_BUDGET
    tm = tm or next(t for t in (2048, 1024, 512, 256, 128) if M % t == 0 and (fits(t) or t == 128))
    grid = (M // tm, N // tn)
    wrap, h_in, h_ospecs, h_oshapes, h_scratch = _hosting(hosted, grid, 2 if res is None else 3, 1, 0)

    def body(*refs):
        a_ref, b_ref = refs[:2]
        o_ref = refs[-1]
        out = (_dot_nt if nt else _dot)(a_ref[...], b_ref[...])
        if res is not None:
            out = out + refs[2][...]
        o_ref[...] = out.astype(out_dtype)

    b_spec = pl.BlockSpec((tn, K), lambda i, j: (j, kblock)) if nt else pl.BlockSpec((K, tn), lambda i, j: (0, j))
    in_specs = [pl.BlockSpec((tm, K), lambda i, j: (i, 0)), b_spec]
    args = [a, b]
    if res is not None:
        in_specs.append(pl.BlockSpec((tm, tn), lambda i, j: (i, j)))
        args.append(res)
    out = pl.pallas_call(
        wrap(body), name=name, grid=grid, in_specs=in_specs + [pl.BlockSpec(memory_space=pl.ANY)] * len(h_in),
        out_specs=[pl.BlockSpec((tm, tn), lambda i, j: (i, j))] + h_ospecs, out_shape=[_sds((M, N), out_dtype)] + h_oshapes,
        scratch_shapes=h_scratch,
        compiler_params=_cparams(("arbitrary", "arbitrary") if hosted else ("parallel", "parallel")),
    )(*args, *h_in)
    return out if hosted else out[0]


def matmul_tn(a, g, *, name, tm=None, tn=None, tk=None):
    T, M = a.shape
    T2, N = g.shape
    assert T == T2
    tm = tm or _pick(M, (1408, 1280, 1024, 512, 384, 256, 128))
    tn = tn or _pick(N, (1024, 1408, 512, 256, 128))
    tk = tk or _pick(T, (1024, 512, 256, 128))
    nk = T // tk

    def body(a_ref, g_ref, o_ref, acc):
        k = pl.program_id(2)
        part = _dot_tn(a_ref[...], g_ref[...])

        @pl.when(k == 0)
        def _():
            acc[...] = part

        @pl.when(k > 0)
        def _():
            acc[...] += part

        @pl.when(k == nk - 1)
        def _():
            o_ref[...] = acc[...].astype(o_ref.dtype)

    return pl.pallas_call(
        body, name=name, grid=(M // tm, N // tn, nk),
        in_specs=[pl.BlockSpec((tk, tm), lambda i, j, k: (k, i)), pl.BlockSpec((tk, tn), lambda i, j, k: (k, j))],
        out_specs=pl.BlockSpec((tm, tn), lambda i, j, k: (i, j)), out_shape=_sds((M, N), MXU_DTYPE),
        scratch_shapes=[pltpu.VMEM((tm, tn), F32)],
        compiler_params=_cparams(("parallel", "parallel", "arbitrary")),
    )(a, g)


def rmsnorm_fwd(x, gamma, *, name, tm=512):
    T, D = x.shape
    tm = _pick(T, (tm, 256, 128))

    def body(x_ref, g_ref, o_ref):
        xv = x_ref[...]
        r = lax.rsqrt(jnp.mean(xv * xv, axis=-1, keepdims=True) + EPS)
        o_ref[...] = ((xv * r) * g_ref[...]).astype(MXU_DTYPE)

    return pl.pallas_call(
        body, name=name, grid=(T // tm,),
        in_specs=[pl.BlockSpec((tm, D), lambda i: (i, 0)), pl.BlockSpec((1, D), lambda i: (0, 0))],
        out_specs=pl.BlockSpec((tm, D), lambda i: (i, 0)), out_shape=_sds((T, D), MXU_DTYPE),
        compiler_params=_cparams(("parallel",)),
    )(x, gamma.reshape(1, D))


def _rms_bwd_tile(xv, gamma, dh):
    r = lax.rsqrt(jnp.mean(xv * xv, axis=-1, keepdims=True) + EPS)
    xh = xv * r
    dg = jnp.sum(dh * xh, axis=0, keepdims=True)
    dn = dh * gamma
    dx = r * (dn - xh * jnp.mean(dn * xh, axis=-1, keepdims=True))
    return dx, dg


def rmsnorm_bwd(x, gamma, dh, dres, *, name, tm=256):
    T, D = x.shape
    tm = _pick(T, (tm, 128))

    def body(x_ref, g_ref, dh_ref, dr_ref, dx_ref, dg_ref):
        dx, dg = _rms_bwd_tile(x_ref[...], g_ref[...], dh_ref[...].astype(F32))
        dx_ref[...] = dx + dr_ref[...]

        @pl.when(pl.program_id(0) == 0)
        def _():
            dg_ref[...] = dg

        @pl.when(pl.program_id(0) > 0)
        def _():
            dg_ref[...] += dg

    row = pl.BlockSpec((tm, D), lambda i: (i, 0))
    vec = pl.BlockSpec((1, D), lambda i: (0, 0))
    return pl.pallas_call(
        body, name=name, grid=(T // tm,), in_specs=[row, vec, row, row], out_specs=[row, vec],
        out_shape=[_sds((T, D), F32), _sds((1, D), F32)], compiler_params=_cparams(("arbitrary",)),
    )(x, gamma.reshape(1, D), dh, dres)


def final_loss(x, gamma, target, *, name="final_loss", tm=256):
    T, D = x.shape
    tm = _pick(T, (tm, 128))

    def body(x_ref, g_ref, t_ref, l_ref, dx_ref, dg_ref):
        xv = x_ref[...]
        gam = g_ref[...]
        r = lax.rsqrt(jnp.mean(xv * xv, axis=-1, keepdims=True) + EPS)
        y = (xv * r) * gam
        err = y - t_ref[...]
        part = 0.5 * jnp.sum(jnp.sum(err * err, axis=-1, keepdims=True) / D, axis=0, keepdims=True)
        dx, dg = _rms_bwd_tile(xv, gam, err / D)
        dx_ref[...] = dx

        @pl.when(pl.program_id(0) == 0)
        def _():
            dg_ref[...] = dg
            l_ref[...] = jnp.broadcast_to(part, l_ref.shape)

        @pl.when(pl.program_id(0) > 0)
        def _():
            dg_ref[...] += dg
            l_ref[...] += jnp.broadcast_to(part, l_ref.shape)

    row = pl.BlockSpec((tm, D), lambda i: (i, 0))
    vec = pl.BlockSpec((1, D), lambda i: (0, 0))
    return pl.pallas_call(
        body, name=name, grid=(T // tm,), in_specs=[row, vec, row],
        out_specs=[pl.BlockSpec((1, LANE), lambda i: (0, 0)), row, vec],
        out_shape=[_sds((1, LANE), F32), _sds((T, D), F32), _sds((1, D), F32)],
        compiler_params=_cparams(("arbitrary",)),
    )(x, gamma.reshape(1, D), target)


def _halo_prev(ts):
    return lambda i, j, off=0: (jnp.maximum(i * (ts // HALO) - 1, 0), j + off)


def _cat_prev(cur, halo, first):
    return jnp.concatenate([jnp.where(first, 0.0, halo), cur], axis=0)


def _cat_next(cur, halo, last):
    return jnp.concatenate([cur, jnp.where(last, 0.0, halo)], axis=0)


def _delayed(cat, r):
    if r == 0:
        return cat[HALO:]
    return pltpu.roll(cat, r, axis=0)[HALO:]


def _advanced(cat, r):
    n = cat.shape[0]
    if r == 0:
        return cat[:n - HALO]
    return pltpu.roll(cat, n - r, axis=0)[:n - HALO]


def _conv_pre(cat, w, b, K):
    acc = _delayed(cat, K - 1) * w[0:1, :] + b
    for k in range(1, K):
        acc = acc + _delayed(cat, K - 1 - k) * w[k:k + 1, :]
    return acc


def _pad_rows8(w):
    return jnp.pad(w, ((0, 8 - w.shape[0]), (0, 0)))


def ssd_conv_fwd(proj, w, b, S, *, name, ts=1024, tc=512):
    T = proj.shape[0]
    C, K = SSD_CONV_CH, SSD_CONV
    ts = _pick(S, (ts, 256, 128))
    off = PXBC0 // tc
    ns = S // ts

    def body(x_ref, h_ref, w_ref, b_ref, o_ref):
        first = (pl.program_id(0) % ns) == 0
        pre = _conv_pre(_cat_prev(x_ref[...], h_ref[...], first), w_ref[...], b_ref[...], K)
        o_ref[...] = pre * _sigmoid(pre)

    return pl.pallas_call(
        body, name=name, grid=(T // ts, C // tc),
        in_specs=[pl.BlockSpec((ts, tc), lambda i, j: (i, j + off)),
                  pl.BlockSpec((HALO, tc), functools.partial(_halo_prev(ts), off=off)),
                  pl.BlockSpec((8, tc), lambda i, j: (0, j)), pl.BlockSpec((1, tc), lambda i, j: (0, j))],
        out_specs=pl.BlockSpec((ts, tc), lambda i, j: (i, j)), out_shape=_sds((T, C), F32),
        compiler_params=_cparams(("parallel", "parallel")),
    )(proj, proj, _pad_rows8(w), b.reshape(1, C))


def _conv_stats(dpre, cat, K, ts):
    rows = [jnp.sum(dpre[:ts] * _delayed(cat, K - 1 - k)[:ts], axis=0, keepdims=True) for k in range(K)]
    rows.append(jnp.sum(dpre[:ts], axis=0, keepdims=True))
    rows.append(jnp.zeros((8 - len(rows), dpre.shape[1]), F32))
    return jnp.concatenate(rows, axis=0)


def _conv_transposed(dpre, wv, K):
    acc = _advanced(dpre, K - 1) * wv[0:1, :]
    for k in range(1, K):
        acc = acc + _advanced(dpre, K - 1 - k) * wv[k:k + 1, :]
    return acc


def ssd_conv_bwd(proj, w, b, dxc, S, *, name, ts=512, tc=512):
    T = proj.shape[0]
    C, K = SSD_CONV_CH, SSD_CONV
    ts = _pick(S, (ts, 256, 128))
    off = PXBC0 // tc
    ns = S // ts
    nblk = T // HALO

    def body(x_ref, xp_ref, xn_ref, w_ref, b_ref, d_ref, dn_ref, o_ref, acc_ref):
        i = pl.program_id(1)
        first = (i % ns) == 0
        last = (i % ns) == ns - 1
        cat = jnp.concatenate([jnp.where(first, 0.0, xp_ref[...]), x_ref[...], xn_ref[...]], axis=0)
        wv = w_ref[...]
        pre = _conv_pre(cat, wv, b_ref[...], K)
        sg = _sigmoid(pre)
        dpre = _cat_next(d_ref[...], dn_ref[...], last) * (sg * (1.0 + pre * (1.0 - sg)))
        o_ref[...] = _conv_transposed(dpre, wv, K).astype(o_ref.dtype)
        part = _conv_stats(dpre, cat, K, ts)

        @pl.when(i == 0)
        def _():
            acc_ref[...] = part

        @pl.when(i > 0)
        def _():
            acc_ref[...] += part

    hp = _halo_prev(ts)
    hn = lambda i: jnp.minimum((i + 1) * (ts // HALO), nblk - 1)
    return pl.pallas_call(
        body, name=name, grid=(C // tc, T // ts),
        in_specs=[pl.BlockSpec((ts, tc), lambda j, i: (i, j + off)),
                  pl.BlockSpec((HALO, tc), lambda j, i: hp(i, j, off)),
                  pl.BlockSpec((HALO, tc), lambda j, i: (hn(i), j + off)),
                  pl.BlockSpec((8, tc), lambda j, i: (0, j)), pl.BlockSpec((1, tc), lambda j, i: (0, j)),
                  pl.BlockSpec((ts, tc), lambda j, i: (i, j)), pl.BlockSpec((HALO, tc), lambda j, i: (hn(i), j))],
        out_specs=[pl.BlockSpec((ts, tc), lambda j, i: (i, j)), pl.BlockSpec((8, tc), lambda j, i: (0, j))],
        out_shape=[_sds((T, C), MXU_DTYPE), _sds((8, C), F32)],
        compiler_params=_cparams(("parallel", "arbitrary")),
    )(proj, proj, proj, _pad_rows8(w), b.reshape(1, C), dxc, dxc)


def ffn_conv_gate_fwd(up, w, b, S, *, name, ts=512, tc=1408):
    T, C2 = up.shape
    C, K = C2 // 2, FFN_CONV
    ts = _pick(S, (ts, 256, 128))
    nj = C // tc
    ns = S // ts
    w8 = _pad_rows8(w)
    b2 = b.reshape(1, C2)

    def body(g_ref, gh_ref, v_ref, vh_ref, wg_ref, wv_ref, bg_ref, bv_ref, o_ref):
        first = (pl.program_id(0) % ns) == 0
        g = _conv_pre(_cat_prev(g_ref[...], gh_ref[...], first), wg_ref[...], bg_ref[...], K)
        v = _conv_pre(_cat_prev(v_ref[...], vh_ref[...], first), wv_ref[...], bv_ref[...], K)
        o_ref[...] = (g * _sigmoid(g) * v).astype(o_ref.dtype)

    hp = _halo_prev(ts)
    return pl.pallas_call(
        body, name=name, grid=(T // ts, nj),
        in_specs=[pl.BlockSpec((ts, tc), lambda i, j: (i, j)), pl.BlockSpec((HALO, tc), lambda i, j: hp(i, j)),
                  pl.BlockSpec((ts, tc), lambda i, j: (i, j + nj)), pl.BlockSpec((HALO, tc), lambda i, j: hp(i, j, nj)),
                  pl.BlockSpec((8, tc), lambda i, j: (0, j)), pl.BlockSpec((8, tc), lambda i, j: (0, j + nj)),
                  pl.BlockSpec((1, tc), lambda i, j: (0, j)), pl.BlockSpec((1, tc), lambda i, j: (0, j + nj))],
        out_specs=pl.BlockSpec((ts, tc), lambda i, j: (i, j)), out_shape=_sds((T, C), MXU_DTYPE),
        compiler_params=_cparams(("parallel", "parallel")),
    )(up, up, up, up, w8, w8, b2, b2)


def ffn_conv_gate_bwd(up, w, b, dact, S, *, name, ts=256, tc=1408):
    T, C2 = up.shape
    C, K = C2 // 2, FFN_CONV
    ts = _pick(S, (ts, 256, 128))
    nj = C // tc
    ns = S // ts
    nblk = T // HALO
    w8 = _pad_rows8(w)
    b2 = b.reshape(1, C2)

    def body(g_ref, gp_ref, gn_ref, v_ref, vp_ref, vn_ref, wg_ref, wv_ref, bg_ref, bv_ref, d_ref, dn_ref,
             dug_ref, duv_ref, ag_ref, av_ref):
        i = pl.program_id(1)
        first = (i % ns) == 0
        last = (i % ns) == ns - 1
        gcat = jnp.concatenate([jnp.where(first, 0.0, gp_ref[...]), g_ref[...], gn_ref[...]], axis=0)
        vcat = jnp.concatenate([jnp.where(first, 0.0, vp_ref[...]), v_ref[...], vn_ref[...]], axis=0)
        wg, wv = wg_ref[...], wv_ref[...]
        g = _conv_pre(gcat, wg, bg_ref[...], K)
        v = _conv_pre(vcat, wv, bv_ref[...], K)
        d = _cat_next(d_ref[...], dn_ref[...], last)
        sg = _sigmoid(g)
        dg = d * v * (sg * (1.0 + g * (1.0 - sg)))
        dv = d * (g * sg)
        dug_ref[...] = _conv_transposed(dg, wg, K).astype(dug_ref.dtype)
        duv_ref[...] = _conv_transposed(dv, wv, K).astype(duv_ref.dtype)
        sgp, svp = _conv_stats(dg, gcat, K, ts), _conv_stats(dv, vcat, K, ts)

        @pl.when(i == 0)
        def _():
            ag_ref[...] = sgp
            av_ref[...] = svp

        @pl.when(i > 0)
        def _():
            ag_ref[...] += sgp
            av_ref[...] += svp

    hp = _halo_prev(ts)
    hn = lambda i: jnp.minimum((i + 1) * (ts // HALO), nblk - 1)
    cur = lambda off: pl.BlockSpec((ts, tc), lambda j, i: (i, j + off))
    prv = lambda off: pl.BlockSpec((HALO, tc), lambda j, i: hp(i, j, off))
    nxt = lambda off: pl.BlockSpec((HALO, tc), lambda j, i: (hn(i), j + off))
    row = lambda r, off: pl.BlockSpec((r, tc), lambda j, i: (0, j + off))
    dug, duv, ag, av = pl.pallas_call(
        body, name=name, grid=(nj, T // ts),
        in_specs=[cur(0), prv(0), nxt(0), cur(nj), prv(nj), nxt(nj), row(8, 0), row(8, nj), row(1, 0), row(1, nj),
                  cur(0), nxt(0)],
        out_specs=[cur(0), cur(0), row(8, 0), row(8, 0)],
        out_shape=[_sds((T, C), MXU_DTYPE), _sds((T, C), MXU_DTYPE), _sds((8, C), F32), _sds((8, C), F32)],
        compiler_params=_cparams(("parallel", "arbitrary")),
    )(up, up, up, up, up, up, w8, w8, b2, b2, dact, dact)
    return dug, duv, jnp.concatenate([ag, av], axis=1)


def _pool_counts(pos, w):
    return jnp.minimum(pos + 1.0, float(w))


def pool_fwd(proj, pool_w, pool_scale, S, *, name, ts=512):
    T = proj.shape[0]
    C, G, GD, H = POOL_WIDTH, POOL_GROUPS, POOL_GROUP_DIM, POOL_HALO
    ts = _pick(S, (ts, 256, 128))
    ns = S // ts
    off = PU0 // C

    def body(u_ref, h_ref, w_ref, s_ref, y_ref, p_ref):
        i = pl.program_id(0)
        first = (i % ns) == 0
        cat = jnp.concatenate([jnp.where(first, 0.0, h_ref[...]), u_ref[...]], axis=0)
        pos = ((i % ns) * ts + lax.broadcasted_iota(jnp.int32, (ts, 1), 0)).astype(F32)
        sums = cat
        win = 1
        for g, wlen in enumerate(POOL_WINDOWS):
            while win < wlen:
                sums = sums + pltpu.roll(sums, win, axis=0)
                win *= 2
            sl = slice(g * GD, (g + 1) * GD)
            pooled = sums[H:, sl] / _pool_counts(pos, wlen) - cat[H:, sl]
            p_ref[:, sl] = pooled.astype(p_ref.dtype)
            y_ref[:, sl] = (_dot(pooled, w_ref[g]) * s_ref[:, sl]).astype(y_ref.dtype)

    return pl.pallas_call(
        body, name=name, grid=(T // ts,),
        in_specs=[pl.BlockSpec((ts, C), lambda i: (i, off)),
                  pl.BlockSpec((H, C), lambda i: (jnp.maximum(i * (ts // H) - 1, 0), off)),
                  pl.BlockSpec((G, GD, GD), lambda i: (0, 0, 0)), pl.BlockSpec((1, C), lambda i: (0, 0))],
        out_specs=[pl.BlockSpec((ts, C), lambda i: (i, 0)), pl.BlockSpec((ts, C), lambda i: (i, 0))],
        out_shape=[_sds((T, C), MXU_DTYPE), _sds((T, C), MXU_DTYPE)],
        compiler_params=_cparams(("parallel",)),
    )(proj, proj, _mx(pool_w), pool_scale.reshape(1, C))


def pool_bwd(dmix, pooled, pool_w, pool_scale, S, *, name, ts=512):
    T = dmix.shape[0]
    C, G, GD, H = POOL_WIDTH, POOL_GROUPS, POOL_GROUP_DIM, POOL_HALO
    ts = _pick(S, (ts, 256, 128))
    ns = S // ts
    off = SSD_WIDTH // C
    nblk = T // H

    def body(d_ref, dh_ref, p_ref, w_ref, s_ref, du_ref, dw_ref, ds_ref):
        i = pl.program_id(0)
        last = (i % ns) == ns - 1
        dcat = jnp.concatenate([d_ref[...], jnp.where(last, 0.0, dh_ref[...])], axis=0)
        n = ts + H
        pos = ((i % ns) * ts + lax.broadcasted_iota(jnp.int32, (n, 1), 0)).astype(F32)
        dws, dss = [], []
        for g, wlen in enumerate(POOL_WINDOWS):
            sl = slice(g * GD, (g + 1) * GD)
            wg = w_ref[g]
            pg = p_ref[:, sl]
            dys = dcat[:, sl] * s_ref[:, sl]
            dss.append(jnp.sum(dcat[:ts, sl] * _dot(pg, wg), axis=0, keepdims=True))
            dws.append(_dot_tn(pg, dys[:ts]))
            dp = _dot_nt(dys, wg)
            q = dp / _pool_counts(pos, wlen)
            win = 1
            while win < wlen:
                q = q + pltpu.roll(q, n - win, axis=0)
                win *= 2
            du_ref[:, sl] = (q[:ts] - dp[:ts]).astype(du_ref.dtype)
        dsp = jnp.concatenate(dss, axis=1)

        @pl.when(i == 0)
        def _():
            for g in range(G):
                dw_ref[g] = dws[g]
            ds_ref[...] = dsp

        @pl.when(i > 0)
        def _():
            for g in range(G):
                dw_ref[g] += dws[g]
            ds_ref[...] += dsp

    return pl.pallas_call(
        body, name=name, grid=(T // ts,),
        in_specs=[pl.BlockSpec((ts, C), lambda i: (i, off)),
                  pl.BlockSpec((H, C), lambda i: (jnp.minimum((i + 1) * (ts // H), nblk - 1), off)),
                  pl.BlockSpec((ts, C), lambda i: (i, 0)),
                  pl.BlockSpec((G, GD, GD), lambda i: (0, 0, 0)), pl.BlockSpec((1, C), lambda i: (0, 0))],
        out_specs=[pl.BlockSpec((ts, C), lambda i: (i, 0)), pl.BlockSpec((G, GD, GD), lambda i: (0, 0, 0)),
                   pl.BlockSpec((1, C), lambda i: (0, 0))],
        out_shape=[_sds((T, C), MXU_DTYPE), _sds((G, GD, GD), F32), _sds((1, C), F32)],
        compiler_params=_cparams(("arbitrary",)),
    )(dmix, dmix, pooled, _mx(pool_w), pool_scale.reshape(1, C))


ROPE0 = MLA_NOPE
ROPE_HALF = MLA_ROPE // 2


def _rope_tables(pos, invf):
    lane = lax.broadcasted_iota(jnp.int32, (1, HEAD_W), 1)
    ang = pos * invf
    cs, sn = jnp.cos(ang), jnp.sin(ang)
    in_a = (lane >= ROPE0) & (lane < ROPE0 + ROPE_HALF)
    in_b = (lane >= ROPE0 + ROPE_HALF) & (lane < ROPE0 + MLA_ROPE)
    return jnp.where(in_a | in_b, cs, 1.0), jnp.where(in_a, -sn, 0.0), jnp.where(in_b, sn, 0.0), in_a | in_b


def _rope(v, cosf, sin_a, sin_b):
    return (v * cosf + pltpu.roll(v, HEAD_W - ROPE_HALF, axis=1) * sin_a + pltpu.roll(v, ROPE_HALF, axis=1) * sin_b)


def _unrope(d, cosf, sin_a, sin_b):
    return (d * cosf + pltpu.roll(d * sin_a, ROPE_HALF, axis=1) + pltpu.roll(d * sin_b, HEAD_W - ROPE_HALF, axis=1))


def _rms_tile(xv, gamma):
    return (xv * lax.rsqrt(jnp.mean(xv * xv, axis=-1, keepdims=True) + EPS)) * gamma


def mla_prep_fwd(proj, pos, invf, q_norm, w_uq_p, kv_norm, w_ukv_p, *, name, tm=256):
    T = proj.shape[0]
    tm = _pick(T, (tm, 128))
    QR, KR, P = MLA_Q_RANK, MLA_KV_RANK, MLA_PAD

    def body(cq_ref, ckv_ref, kpe_ref, pos_ref, invf_ref, qn_ref, wq_ref, kn_ref, wkv_ref,
             q_ref, k_ref, v_ref, cqn_ref, ckvn_ref):
        cosf, sin_a, sin_b, _ = _rope_tables(pos_ref[...], invf_ref[...])
        cqn = _rms_tile(cq_ref[...], qn_ref[...]).astype(MXU_DTYPE)
        ckvn = _rms_tile(ckv_ref[...], kn_ref[...]).astype(MXU_DTYPE)
        cqn_ref[...] = cqn
        ckvn_ref[...] = ckvn
        qp = _dot(cqn, wq_ref[...])
        kvp = _dot(ckvn, wkv_ref[...])
        kpe = _rope(kpe_ref[...], cosf, sin_a, sin_b)
        for h in range(MLA_HEADS):
            sl = slice(h * HEAD_W, (h + 1) * HEAD_W)
            q_ref[:, sl] = (_rope(qp[:, sl], cosf, sin_a, sin_b) * ATTN_SCALE).astype(q_ref.dtype)
            k_ref[:, sl] = (kvp[:, sl] + kpe).astype(k_ref.dtype)
            v_ref[:, sl] = kvp[:, P + h * HEAD_W:P + (h + 1) * HEAD_W].astype(v_ref.dtype)

    row = lambda w: pl.BlockSpec((tm, w), lambda i: (i, 0))
    full = lambda a, b: pl.BlockSpec((a, b), lambda i: (0, 0))
    return pl.pallas_call(
        body, name=name, grid=(T // tm,),
        in_specs=[pl.BlockSpec((tm, QR), lambda i: (i, PCQ0 // QR)), pl.BlockSpec((tm, KR), lambda i: (i, PCKV0 // KR)),
                  pl.BlockSpec((tm, LANE), lambda i: (i, PKPE0 // LANE)), row(1), full(1, LANE),
                  full(1, QR), full(QR, P), full(1, KR), full(KR, 2 * P)],
        out_specs=[row(P), row(P), row(P), row(QR), row(KR)],
        out_shape=[_sds((T, P), MXU_DTYPE)] * 3 + [_sds((T, QR), MXU_DTYPE), _sds((T, KR), MXU_DTYPE)],
        compiler_params=_cparams(("parallel",)),
    )(proj, proj, proj, pos, invf, q_norm.reshape(1, QR), w_uq_p, kv_norm.reshape(1, KR), w_ukv_p)


def mla_prep_bwd(proj, pos, invf, q_norm, w_uq_p, kv_norm, w_ukv_p, dq, dk, dv, *, name, tm=256):
    T = proj.shape[0]
    tm = _pick(T, (tm, 128))
    QR, KR, P = MLA_Q_RANK, MLA_KV_RANK, MLA_PAD

    def body(cq_ref, ckv_ref, pos_ref, invf_ref, qn_ref, wq_ref, kn_ref, wkv_ref, dq_ref, dk_ref, dv_ref,
             dqp_ref, dkvp_ref, dcq_ref, dckv_ref, dkpe_ref, dqn_ref, dkn_ref):
        cosf, sin_a, sin_b, rot = _rope_tables(pos_ref[...], invf_ref[...])
        dkpe = jnp.zeros((tm, HEAD_W), F32)
        for h in range(MLA_HEADS):
            sl = slice(h * HEAD_W, (h + 1) * HEAD_W)
            dqp_ref[:, sl] = _unrope(dq_ref[:, sl] * ATTN_SCALE, cosf, sin_a, sin_b).astype(dqp_ref.dtype)
            dkh = dk_ref[:, sl]
            dkpe = dkpe + dkh
            dkvp_ref[:, sl] = dkh.astype(dkvp_ref.dtype)
            dkvp_ref[:, P + h * HEAD_W:P + (h + 1) * HEAD_W] = dv_ref[:, sl].astype(dkvp_ref.dtype)
        dkpe_ref[...] = jnp.where(rot, _unrope(dkpe, cosf, sin_a, sin_b), 0.0).astype(dkpe_ref.dtype)
        dcq, dqn = _rms_bwd_tile(cq_ref[...], qn_ref[...], _dot_nt(dqp_ref[...], wq_ref[...]))
        dckv, dkn = _rms_bwd_tile(ckv_ref[...], kn_ref[...], _dot_nt(dkvp_ref[...], wkv_ref[...]))
        dcq_ref[...] = dcq.astype(dcq_ref.dtype)
        dckv_ref[...] = dckv.astype(dckv_ref.dtype)

        @pl.when(pl.program_id(0) == 0)
        def _():
            dqn_ref[...] = dqn
            dkn_ref[...] = dkn

        @pl.when(pl.program_id(0) > 0)
        def _():
            dqn_ref[...] += dqn
            dkn_ref[...] += dkn

    row = lambda w: pl.BlockSpec((tm, w), lambda i: (i, 0))
    full = lambda a, b: pl.BlockSpec((a, b), lambda i: (0, 0))
    return pl.pallas_call(
        body, name=name, grid=(T // tm,),
        in_specs=[pl.BlockSpec((tm, QR), lambda i: (i, PCQ0 // QR)), pl.BlockSpec((tm, KR), lambda i: (i, PCKV0 // KR)),
                  row(1), full(1, LANE), full(1, QR), full(QR, P), full(1, KR), full(KR, 2 * P), row(P), row(P), row(P)],
        out_specs=[row(P), row(2 * P), row(QR), row(KR), row(LANE), full(1, QR), full(1, KR)],
        out_shape=[_sds((T, P), MXU_DTYPE), _sds((T, 2 * P), MXU_DTYPE), _sds((T, QR), MXU_DTYPE),
                   _sds((T, KR), MXU_DTYPE), _sds((T, LANE), MXU_DTYPE), _sds((1, QR), F32), _sds((1, KR), F32)],
        compiler_params=_cparams(("arbitrary",)),
    )(proj, proj, pos, invf, q_norm.reshape(1, QR), w_uq_p, kv_norm.reshape(1, KR), w_ukv_p, dq, dk, dv)


ATTN_SCALE = 1.0 / math.sqrt(MLA_QK)


def _causal_mask(i, j, blk):
    row = lax.broadcasted_iota(jnp.int32, (blk, blk), 0)
    col = lax.broadcasted_iota(jnp.int32, (blk, blk), 1)
    return col <= row + (i - j) * blk


def _hosting(hosted, grid, n_in, n_out, n_scratch):
    if hosted is None:
        return (lambda body: body), (), [], [], []
    hi, ho = len(hosted.inputs), len(hosted.out_shapes)

    def wrap(body):
        def full(*refs):
            ins, rest = refs[:n_in + hi], refs[n_in + hi:]
            outs, scr = rest[:n_out + ho], rest[n_out + ho:]
            parts = ins[n_in:], outs[n_out:], scr[n_scratch:]
            ids = [pl.program_id(d) for d in range(len(grid))]
            step = ids[0]
            for d in range(1, len(grid)):
                step = step * grid[d] + ids[d]
            total = math.prod(grid)

            @pl.when(step == 0)
            def _():
                hosted.start(*parts)

            body(*ins[:n_in], *outs[:n_out], *scr[:n_scratch])

            @pl.when(step == total // 2)
            def _():
                hosted.relay(*parts)

            @pl.when(step == total - 1)
            def _():
                hosted.finish(*parts)

        return full

    hbm = pl.BlockSpec(memory_space=pl.ANY)
    return wrap, tuple(hosted.inputs), [hbm] * ho, list(hosted.out_shapes), list(hosted.sems)


def flash_fwd(q, k, v, S, *, name, blk=1024, hosted=None):
    T, P = q.shape
    blk = _pick(S // 2, (blk, 256, 128))
    B, nq, H, W = T // S, S // (2 * blk), MLA_HEADS, HEAD_W
    grid = (B, H, nq)
    wrap, h_in, h_ospecs, h_oshapes, h_scratch = _hosting(hosted, grid, 3, 2, 0)

    def body(q_ref, k_ref, v_ref, o_ref, lse_ref):
        i = pl.program_id(2)
        q_up, q_lo = q_ref[:blk, :], q_ref[blk:, :]

        def online(qv, kv, vv, carry, masked):
            m_prev, l_prev, acc = carry
            s = _dot_nt(qv, kv)
            if masked:
                s = jnp.where(_causal_mask(0, 0, blk), s, -jnp.inf)
            m_new = jnp.maximum(m_prev, jnp.max(s, axis=1, keepdims=True))
            p = jnp.exp(s - m_new)
            alpha = jnp.exp(m_prev - m_new)
            return (m_new, alpha * l_prev + jnp.sum(p, axis=1, keepdims=True), alpha * acc + _dot(p, vv))

        def keys(j):
            rows = pl.ds(pl.multiple_of(j * blk, blk), blk)
            return k_ref[rows, :], v_ref[rows, :]

        def both(j, carry):
            kv, vv = keys(j)
            return online(q_up, kv, vv, carry[0], False), online(q_lo, kv, vv, carry[1], False)

        init = (jnp.full((blk, 1), -jnp.inf, F32), jnp.zeros((blk, 1), F32), jnp.zeros((blk, W), F32))
        up, lo = lax.fori_loop(0, 2 * i, both, (init, init))
        kv, vv = keys(2 * i)
        up = online(q_up, kv, vv, up, True)
        lo = online(q_lo, kv, vv, lo, False)
        kv, vv = keys(2 * i + 1)
        lo = online(q_lo, kv, vv, lo, True)
        for rows, (m, l, acc) in ((slice(0, blk), up), (slice(blk, 2 * blk), lo)):
            o_ref[rows, :] = acc / l
            lse_ref[rows, :] = jnp.broadcast_to(m + jnp.log(l), (blk, W))

    qmap = lambda b, h, i: (b * nq + i, h)
    kmap = lambda b, h, i: (b, h)
    hbm = pl.BlockSpec(memory_space=pl.ANY)
    return pl.pallas_call(
        wrap(body), name=name, grid=grid,
        in_specs=[pl.BlockSpec((2 * blk, W), qmap), pl.BlockSpec((S, W), kmap), pl.BlockSpec((S, W), kmap)] + [hbm] * len(h_in),
        out_specs=[pl.BlockSpec((2 * blk, W), qmap), pl.BlockSpec((2 * blk, W), qmap)] + h_ospecs,
        out_shape=[_sds((T, P), F32), _sds((T, P), F32)] + h_oshapes,
        scratch_shapes=h_scratch,
        compiler_params=_cparams(("arbitrary",) * 3 if hosted else ("parallel", "parallel", "arbitrary")),
    )(q, k, v, *h_in)


def flash_bwd(q, k, v, o, lse, dmix, S, *, name, blk=1024, hosted=None):
    T, P = q.shape
    blk = _pick(S, (blk, 256, 128))
    B, nq, H, W = T // S, S // blk, MLA_HEADS, HEAD_W
    off = (SSD_WIDTH + POOL_WIDTH) // W
    grid = (B, H, nq)
    wrap, h_in, h_ospecs, h_oshapes, h_scratch = _hosting(hosted, grid, 6, 3, 1)

    def body(q_ref, k_ref, v_ref, o_ref, lse_ref, do_ref, dq_ref, dk_ref, dv_ref, delta_s):
        j = pl.program_id(2)

        @pl.when(j == 0)
        def _():
            for i in range(nq):
                rows = slice(i * blk, (i + 1) * blk)
                delta_s[rows, :] = jnp.sum(do_ref[rows, :] * o_ref[rows, :], axis=1, keepdims=True)
                dq_ref[rows, :] = jnp.zeros((blk, W), F32)

        kv, vv = k_ref[...], v_ref[...]

        def step(i, carry, masked):
            dk, dv = carry
            rows = pl.ds(pl.multiple_of(i * blk, blk), blk)
            qv, do = q_ref[rows, :], do_ref[rows, :]
            p = jnp.exp(_dot_nt(qv, kv) - lse_ref[rows, 0:1])
            if masked:
                p = jnp.where(_causal_mask(0, 0, blk), p, 0.0)
            ds = p * (_dot_nt(do, vv) - delta_s[rows, :])
            dq_ref[rows, :] += _dot(ds, kv)
            return dk + _dot_tn(ds, qv), dv + _dot_tn(p, do)

        zero = jnp.zeros((blk, W), F32)
        carry = step(j, (zero, zero), True)
        dk, dv = lax.fori_loop(j + 1, nq, lambda i, c: step(i, c, False), carry)
        dk_ref[...] = dk
        dv_ref[...] = dv

    full = lambda b, h, j: (b, h)
    kmap = lambda b, h, j: (b * nq + j, h)
    hbm = pl.BlockSpec(memory_space=pl.ANY)
    return pl.pallas_call(
        wrap(body), name=name, grid=grid,
        in_specs=[pl.BlockSpec((S, W), full), pl.BlockSpec((blk, W), kmap), pl.BlockSpec((blk, W), kmap),
                  pl.BlockSpec((S, W), full), pl.BlockSpec((S, W), full),
                  pl.BlockSpec((S, W), lambda b, h, j: (b, off + h))] + [hbm] * len(h_in),
        out_specs=[pl.BlockSpec((S, W), full), pl.BlockSpec((blk, W), kmap), pl.BlockSpec((blk, W), kmap)] + h_ospecs,
        out_shape=[_sds((T, P), F32)] * 3 + h_oshapes,
        scratch_shapes=[pltpu.VMEM((S, 1), F32)] + h_scratch,
        compiler_params=_cparams(("arbitrary",) * 3 if hosted else ("parallel", "parallel", "arbitrary")),
    )(q, k, v, o, lse, dmix, *h_in)


SSD_PAIRS = SSD_HEADS // 2
PAIRS_PER_GROUP = SSD_PAIRS // SSD_GROUPS
GN = SSD_GROUPS * SSD_STATE


def _log1p_small(e):
    return jnp.where(e < 1e-3, e * (1.0 - e * (0.5 - e / 3.0)), jnp.log(1.0 + e))


def _softplus(v):
    return jnp.maximum(v, 0.0) + _log1p_small(jnp.exp(-jnp.abs(v)))


def _ssd_decay(dt_raw, dtb, alog):
    L = dt_raw.shape[0]
    pre = dt_raw + dtb
    dt = _softplus(pre)
    a = -jnp.exp(alog)
    row = lax.broadcasted_iota(jnp.int32, (L, L), 0)
    col = lax.broadcasted_iota(jnp.int32, (L, L), 1)
    tri = row >= col
    cum = _dot_hi(tri.astype(F32), dt * a)
    return pre, dt, a, tri, cum, cum.T


def _col(m, h):
    return m[:, h:h + 1]


def _pair_sel(m, k, lo):
    return jnp.where(lo, _col(m, 2 * k), _col(m, 2 * k + 1))


def _ssd_specs(S):
    L = SSD_CHUNK
    nc = S // L
    return L, nc


def ssd_fwd(proj, xc, dtb, alog, dchan, normw, S, *, name, hosted=None):
    T = proj.shape[0]
    L, nc = _ssd_specs(S)
    B, W, N = T // S, SSD_WIDTH, SSD_STATE
    wrap, h_in, h_ospecs, h_oshapes, h_scratch = _hosting(hosted, (B, nc), 9, 3, 1)

    def body(xs_ref, bs_ref, cs_ref, dt_ref, z_ref, dtb_ref, alog_ref, dch_ref, nw_ref, y_ref, ys_ref, hin_ref, st):
        @pl.when(pl.program_id(1) == 0)
        def _():
            st[...] = jnp.zeros(st.shape, F32)

        hin_ref[...] = st[...]
        _, dt, a, tri, cum, cum_t = _ssd_decay(dt_ref[...], dtb_ref[...], alog_ref[...])
        last = cum[L - 1:L, :]
        lo = lax.broadcasted_iota(jnp.int32, (1, LANE), 1) < SSD_HEAD_DIM
        for g in range(SSD_GROUPS):
            bm = bs_ref[:, g * N:(g + 1) * N]
            cm = cs_ref[:, g * N:(g + 1) * N]
            bm_t = bm.T
            gmat = _dot_nt(cm, bm)
            for kk in range(PAIRS_PER_GROUP):
                k = g * PAIRS_PER_GROUP + kk
                sl = slice(k * LANE, (k + 1) * LANE)
                xv = xs_ref[:, sl]
                xdt = xv * _pair_sel(dt, k, lo)
                cum_cols = [jnp.broadcast_to(_col(cum, h), (L, LANE)) for h in (2 * k, 2 * k + 1)]
                cum_sel = jnp.where(lo, cum_cols[0], cum_cols[1])
                last_sel = _pair_sel(last, k, lo)
                yd = []
                for j, h in enumerate((2 * k, 2 * k + 1)):
                    gam = jnp.exp(jnp.where(tri, cum_cols[j] - cum_t[h:h + 1, :], -jnp.inf))
                    yd.append(_dot(gmat * gam, xdt))
                hp = st[:, sl]
                y_off = _dot(cm, hp) * jnp.exp(cum_sel)
                y_ref[:, sl] = jnp.where(lo, yd[0], yd[1]) + y_off + xv * dch_ref[:, sl]
                zmat = xdt * jnp.exp(last_sel - cum_sel)
                st[:, sl] = hp * jnp.exp(last_sel) + _dot(bm_t, zmat)
        y = y_ref[...]
        z = z_ref[...]
        yz = y * (z * _sigmoid(z))
        ys_ref[...] = _rms_tile(yz, nw_ref[...]).astype(ys_ref.dtype)

    r = lambda b, c: b * nc + c
    vec = lambda w: pl.BlockSpec((1, w), lambda b, c: (0, 0))
    hbm = pl.BlockSpec(memory_space=pl.ANY)
    return pl.pallas_call(
        wrap(body), name=name, grid=(B, nc),
        in_specs=[pl.BlockSpec((L, W), lambda b, c: (r(b, c), 0)),
                  pl.BlockSpec((L, GN), lambda b, c: (r(b, c), W // GN)),
                  pl.BlockSpec((L, GN), lambda b, c: (r(b, c), W // GN + 1)),
                  pl.BlockSpec((L, LANE), lambda b, c: (r(b, c), PDT0 // LANE)),
                  pl.BlockSpec((L, W), lambda b, c: (r(b, c), PZ0 // W)),
                  vec(LANE), vec(LANE), vec(W), vec(W)] + [hbm] * len(h_in),
        out_specs=[pl.BlockSpec((L, W), lambda b, c: (r(b, c), 0)), pl.BlockSpec((L, W), lambda b, c: (r(b, c), 0)),
                   pl.BlockSpec((N, W), lambda b, c: (r(b, c), 0))] + h_ospecs,
        out_shape=[_sds((T, W), F32), _sds((T, W), MXU_DTYPE), _sds((T // L * N, W), F32)] + h_oshapes,
        scratch_shapes=[pltpu.VMEM((N, W), F32)] + h_scratch,
        compiler_params=_cparams(("arbitrary", "arbitrary") if hosted else ("parallel", "arbitrary")),
    )(xc, xc, xc, proj, proj, dtb, alog, dchan, normw, *h_in)


def ssd_bwd(proj, xc, ypre, hin, dmix, dtb, alog, dchan, normw, S, *, name):
    T = proj.shape[0]
    L, nc = _ssd_specs(S)
    B, W, N = T // S, SSD_WIDTH, SSD_STATE

    def body(xs_ref, bs_ref, cs_ref, dt_ref, z_ref, y_ref, hin_ref, dys_ref, dtb_ref, alog_ref, dch_ref, nw_ref,
             dxc_ref, ddt_ref, dz_ref, sm_ref, dnw_ref, dst):
        step = pl.program_id(0) * nc + pl.program_id(1)

        @pl.when(pl.program_id(1) == 0)
        def _():
            dst[...] = jnp.zeros(dst.shape, F32)

        pre, dt, a, tri, cum, cum_t = _ssd_decay(dt_ref[...], dtb_ref[...], alog_ref[...])
        last = cum[L - 1:L, :]
        e_last = jnp.exp(last)
        lane = lax.broadcasted_iota(jnp.int32, (1, LANE), 1)
        sub = lax.broadcasted_iota(jnp.int32, (LANE, 1), 0)
        lo = lane < SSD_HEAD_DIM
        is_last_row = sub == L - 1
        tri_t = (lax.broadcasted_iota(jnp.int32, (L, L), 0) <= lax.broadcasted_iota(jnp.int32, (L, L), 1))

        y, z, nw = y_ref[...], z_ref[...], nw_ref[...]
        sg = _sigmoid(z)
        gate = z * sg
        dyz, dnw = _rms_bwd_tile(y * gate, nw, dys_ref[...])
        dy_all = dyz * gate
        dz_ref[...] = (dyz * y * (sg * (1.0 + z * (1.0 - sg)))).astype(dz_ref.dtype)

        d_cum = jnp.zeros((L, LANE), F32)
        d_cum_t = jnp.zeros((LANE, L), F32)
        d_dt = jnp.zeros((L, LANE), F32)
        d_dskip = jnp.zeros((1, LANE), F32)
        for g in range(SSD_GROUPS):
            bm = bs_ref[:, g * N:(g + 1) * N]
            cm = cs_ref[:, g * N:(g + 1) * N]
            cm_t = cm.T
            gmat = _dot_nt(cm, bm)
            gmat_t = _dot_nt(bm, cm)
            d_g = jnp.zeros((L, L), F32)
            d_bm = jnp.zeros((L, N), F32)
            d_cm = jnp.zeros((L, N), F32)
            for kk in range(PAIRS_PER_GROUP):
                k = g * PAIRS_PER_GROUP + kk
                sl = slice(k * LANE, (k + 1) * LANE)
                xv = xs_ref[:, sl]
                dyv = dy_all[:, sl]
                dt_sel = _pair_sel(dt, k, lo)
                xdt = xv * dt_sel
                hp = hin_ref[:, sl]
                dh_out = dst[:, sl]
                cum_cols = [jnp.broadcast_to(_col(cum, h), (L, LANE)) for h in (2 * k, 2 * k + 1)]
                cum_sel = jnp.where(lo, cum_cols[0], cum_cols[1])
                last_sel = _pair_sel(last, k, lo)
                e_sel = jnp.exp(cum_sel)
                w_sel = jnp.exp(last_sel - cum_sel)
                e_lane = jnp.exp(last_sel)
                y_off = _dot(cm, hp) * e_sel
                zmat = xdt * w_sel
                d_z = _dot(bm, dh_out)
                d_bm = d_bm + _dot_nt(zmat, dh_out)
                d_xdt = d_z * w_sel
                dw_full = d_z * zmat
                hh = dh_out * hp
                d_r = dyv * e_sel
                d_cm = d_cm + _dot_nt(d_r, hp)
                dst[:, sl] = dh_out * e_lane + _dot(cm_t, d_r)
                dyoff_full = dyv * y_off
                for j, h in enumerate((2 * k, 2 * k + 1)):
                    mine = lo if j == 0 else jnp.logical_not(lo)
                    hot = lane == h
                    dyh = jnp.where(mine, dyv, 0.0)
                    gam = jnp.exp(jnp.where(tri, cum_cols[j] - cum_t[h:h + 1, :], -jnp.inf))
                    gam_t = jnp.exp(jnp.where(tri_t, cum_t[h:h + 1, :] - cum_cols[j], -jnp.inf))
                    mx = gmat * gam
                    d_xdt = d_xdt + _dot(gmat_t * gam_t, dyh)
                    d_mx = jnp.where(tri, _dot_nt(dyh, xdt), 0.0)
                    d_g = d_g + d_mx * gam
                    d_seg = d_mx * mx
                    row_l = jnp.sum(d_seg + jnp.where(mine, dyoff_full - dw_full, 0.0), axis=1, keepdims=True)
                    at_end = (jnp.sum(jnp.where(mine, dw_full, 0.0), keepdims=True)
                              + jnp.sum(jnp.where(mine, hh, 0.0), keepdims=True) * _col(e_last, h))
                    d_cum = d_cum + jnp.where(hot, row_l + jnp.where(is_last_row, at_end, 0.0), 0.0)
                    d_cum_t = d_cum_t - jnp.where(sub == h, jnp.sum(d_seg, axis=0, keepdims=True), 0.0)
                    d_dskip = d_dskip + jnp.where(hot, jnp.sum(jnp.where(mine, dyv * xv, 0.0), keepdims=True), 0.0)
                for j, h in enumerate((2 * k, 2 * k + 1)):
                    mine = lo if j == 0 else jnp.logical_not(lo)
                    d_dt = d_dt + jnp.where(lane == h, jnp.sum(jnp.where(mine, d_xdt * xv, 0.0), axis=1, keepdims=True), 0.0)
                dxc_ref[:, sl] = d_xdt * dt_sel + dyv * dch_ref[:, sl]
            dxc_ref[:, W + g * N:W + (g + 1) * N] = d_bm + _dot_tn(d_g, cm)
            dxc_ref[:, W + GN + g * N:W + GN + (g + 1) * N] = d_cm + _dot(d_g, bm)

        d_cum = d_cum + d_cum_t.T
        d_da = _dot_hi(jnp.logical_not(tri).astype(F32) + (lax.broadcasted_iota(jnp.int32, (L, L), 0)
                                                              == lax.broadcasted_iota(jnp.int32, (L, L), 1)).astype(F32), d_cum)
        d_dt = d_dt + d_da * a
        heads = lane < SSD_HEADS
        d_pre = jnp.where(heads, d_dt * _sigmoid(pre), 0.0)
        ddt_ref[...] = d_pre.astype(ddt_ref.dtype)
        d_alog = jnp.sum(d_da * dt, axis=0, keepdims=True) * a
        part = jnp.concatenate([jnp.where(heads, d_alog, 0.0), jnp.sum(d_pre, axis=0, keepdims=True), d_dskip,
                                jnp.zeros((5, LANE), F32)], axis=0)

        @pl.when(step == 0)
        def _():
            sm_ref[...] = part
            dnw_ref[...] = dnw

        @pl.when(step > 0)
        def _():
            sm_ref[...] += part
            dnw_ref[...] += dnw

    r = lambda b, c: b * nc + (nc - 1 - c)
    vec = lambda w: pl.BlockSpec((1, w), lambda b, c: (0, 0))
    blk = lambda w, j: pl.BlockSpec((L, w), lambda b, c: (r(b, c), j))
    return pl.pallas_call(
        body, name=name, grid=(B, nc),
        in_specs=[blk(W, 0), blk(GN, W // GN), blk(GN, W // GN + 1), blk(LANE, PDT0 // LANE), blk(W, PZ0 // W),
                  blk(W, 0), pl.BlockSpec((N, W), lambda b, c: (r(b, c), 0)), blk(W, 0),
                  vec(LANE), vec(LANE), vec(W), vec(W)],
        out_specs=[blk(SSD_CONV_CH, 0), blk(LANE, 0), blk(W, 0), pl.BlockSpec((8, LANE), lambda b, c: (0, 0)), vec(W)],
        out_shape=[_sds((T, SSD_CONV_CH), F32), _sds((T, LANE), MXU_DTYPE), _sds((T, W), MXU_DTYPE),
                   _sds((8, LANE), F32), _sds((1, W), F32)],
        scratch_shapes=[pltpu.VMEM((N, W), F32)],
        compiler_params=_cparams(("arbitrary", "arbitrary")),
    )(xc, xc, xc, proj, proj, ypre, hin, dmix, dtb, alog, dchan, normw)


def _adamw_math(w, g, m, v):
    m = ADAM_B1 * m + (1.0 - ADAM_B1) * g
    v = ADAM_B2 * v + (1.0 - ADAM_B2) * (g * g)
    m_hat = m / (1.0 - ADAM_B1 ** ADAM_STEP)
    v_hat = v / (1.0 - ADAM_B2 ** ADAM_STEP)
    delta = -ADAM_LR * (m_hat / (jnp.sqrt(v_hat) + ADAM_EPS) + ADAM_WD * w)
    return delta, m, v


def adamw_layers(w, g_layers, m, v, *, name, tr=256):
    L, A, B = w.shape
    tr = _pick(A, (tr, 192, 176, 128, 64, 32, 16, 8))
    na = A // tr
    n = len(g_layers[0])

    def body(*refs):
        w_ref, m_ref, v_ref = refs[0], refs[1 + L * n], refs[2 + L * n]
        g_ref, d_ref, nm_ref, nv_ref = refs[3 + L * n:]
        layer = pl.program_id(0)
        g = None
        for l in range(L):
            parts = refs[1 + l * n:1 + (l + 1) * n]
            gl = parts[0][...]
            for p in parts[1:]:
                gl = gl + p[...]
            g = gl if g is None else jnp.where(layer == l, gl, g)
        g_ref[...] = g
        d_ref[...], nm_ref[...], nv_ref[...] = _adamw_math(w_ref[...], g, m_ref[...], v_ref[...])

    def g_spec(l):
        return pl.BlockSpec((tr, B), lambda layer, i: (jnp.where(layer == l, i, jnp.where(layer < l, 0, na - 1)), 0))

    spec = pl.BlockSpec((None, tr, B), lambda layer, i: (layer, i, 0))
    return pl.pallas_call(
        body, name=name, grid=(L, na), in_specs=[spec] + [g_spec(l) for l in range(L) for _ in range(n)] + [spec] * 2,
        out_specs=[spec] * 4, out_shape=[_sds((L, A, B), F32)] * 4, compiler_params=_cparams(("arbitrary", "arbitrary")),
    )(w, *[p for parts in g_layers for p in parts], m, v)


def adamw_small(ws, gs, ms, vs, *, name):
    n = len(ws)

    def body(*refs):
        w_refs, g_refs, m_refs, v_refs = (refs[i * n:(i + 1) * n] for i in range(4))
        d_refs, nm_refs, nv_refs = (refs[(4 + i) * n:(5 + i) * n] for i in range(3))
        for a in range(n):
            d_refs[a][...], nm_refs[a][...], nv_refs[a][...] = _adamw_math(
                w_refs[a][...], g_refs[a][...], m_refs[a][...], v_refs[a][...])

    vm = pl.BlockSpec(memory_space=pltpu.VMEM)
    out = pl.pallas_call(
        body, name=name, in_specs=[vm] * (4 * n), out_specs=[vm] * (3 * n),
        out_shape=[_sds(w.shape, F32) for w in ws] * 3, compiler_params=pltpu.CompilerParams(vmem_limit_bytes=VMEM_LIMIT),
    )(*ws, *gs, *ms, *vs)
    return out[:n], out[n:2 * n], out[2 * n:]


def _my_place():
    return lax.axis_index("x"), lax.axis_index("y"), lax.axis_index("c")


def _other_chips(x, y):
    return [(1 - x, y), (x, 1 - y), (1 - x, 1 - y)]


def relation_of(chip, me):
    d = chip ^ me
    return jnp.where(d == 2, 0, jnp.where(d == 1, 1, jnp.where(d == 3, 2, -1)))


class Exchange(NamedTuple):
    inputs: tuple
    out_shapes: tuple
    sems: tuple
    start: Callable
    relay: Callable
    finish: Callable


def scatter_exchange(srcs):
    n = len(srcs)

    def copies(ins, outs, sems):
        x, y, c = _my_place()
        out = []
        for k, (px, py) in enumerate(_other_chips(x, y)):
            for a in range(n):
                out.append(pltpu.make_async_remote_copy(
                    src_ref=ins[a].at[2 * px + py], dst_ref=outs[a].at[k], send_sem=sems[0].at[k, a],
                    recv_sem=sems[1].at[k, a], device_id=(px, py, c), device_id_type=pl.DeviceIdType.MESH))
        return out

    def start(ins, outs, sems):
        for cp in copies(ins, outs, sems):
            cp.start()

    def finish(ins, outs, sems):
        cps = copies(ins, outs, sems)
        for cp in cps:
            cp.wait_recv()
        for cp in cps:
            cp.wait_send()

    return Exchange(tuple(srcs), tuple(_sds((3,) + s.shape[1:], s.dtype) for s in srcs),
                    (pltpu.SemaphoreType.DMA((3, n)),) * 2, start, lambda *a: None, finish)


def run_exchange(ex, *, name):
    n_in, n_out = len(ex.inputs), len(ex.out_shapes)

    def body(*refs):
        parts = refs[:n_in], refs[n_in:n_in + n_out], refs[n_in + n_out:]
        ex.start(*parts)
        ex.relay(*parts)
        ex.finish(*parts)

    hbm = pl.BlockSpec(memory_space=pl.ANY)
    return pl.pallas_call(
        body, name=name, in_specs=[hbm] * n_in, out_specs=[hbm] * n_out, out_shape=list(ex.out_shapes),
        scratch_shapes=list(ex.sems), compiler_params=pltpu.CompilerParams(has_side_effects=True),
    )(*ex.inputs)


def sibling_swap(srcs, *, name):
    n = len(srcs)

    def body(*refs):
        src_refs, out_refs, (send_sems, recv_sems) = refs[:n], refs[n:2 * n], refs[2 * n:]
        x, y, c = _my_place()
        copies = [pltpu.make_async_remote_copy(
            src_ref=src_refs[a], dst_ref=out_refs[a], send_sem=send_sems.at[a], recv_sem=recv_sems.at[a],
            device_id=(x, y, 1 - c), device_id_type=pl.DeviceIdType.MESH) for a in range(n)]
        for cp in copies:
            cp.start()
        for cp in copies:
            cp.wait_recv()
        for cp in copies:
            cp.wait_send()

    hbm = pl.BlockSpec(memory_space=pl.ANY)
    return pl.pallas_call(
        body, name=name, in_specs=[hbm] * n, out_specs=[hbm] * n, out_shape=[_sds(s.shape, s.dtype) for s in srcs],
        scratch_shapes=[pltpu.SemaphoreType.DMA((n,)), pltpu.SemaphoreType.DMA((n,))],
        compiler_params=pltpu.CompilerParams(has_side_effects=True),
    )(*srcs)


def gather_exchange(srcs):
    nch = len(srcs)
    halves = [s.shape[0] // 2 for s in srcs]
    assert all(2 * h == s.shape[0] and h % 16 == 0 for h, s in zip(halves, srcs))
    pieces = [(k, q) for k in range(3) for q in range(nch)]

    def makers(ins, outs, sems):
        ici_send, ici_recv, d2d_send, d2d_recv = sems
        x, y, c = _my_place()
        peers = _other_chips(x, y)

        def rows(core, q):
            return pl.ds(core * halves[q], halves[q])

        def ici(k, q):
            px, py = peers[k]
            return pltpu.make_async_remote_copy(
                src_ref=ins[q].at[rows(c, q)], dst_ref=outs[q].at[k, rows(c, q)], send_sem=ici_send.at[k, q],
                recv_sem=ici_recv.at[k, q], device_id=(px, py, c), device_id_type=pl.DeviceIdType.MESH)

        def d2d(k, q, core):
            return pltpu.make_async_remote_copy(
                src_ref=outs[q].at[k, rows(core, q)], dst_ref=outs[q].at[k, rows(core, q)],
                send_sem=d2d_send.at[k, q], recv_sem=d2d_recv.at[k, q], device_id=(x, y, 1 - c),
                device_id_type=pl.DeviceIdType.MESH)

        return ici, d2d, c

    def start(*refs):
        ici, _, _ = makers(*refs)
        for k, q in pieces:
            ici(k, q).start()

    def relay(*refs):
        ici, d2d, c = makers(*refs)
        for k, q in pieces:
            ici(k, q).wait_recv()
            d2d(k, q, c).start()

    def finish(*refs):
        ici, d2d, c = makers(*refs)
        for k, q in pieces:
            d2d(k, q, 1 - c).wait_recv()
        for k, q in pieces:
            ici(k, q).wait_send()
            d2d(k, q, c).wait_send()

    return Exchange(tuple(srcs), tuple(_sds((3,) + s.shape, s.dtype) for s in srcs),
                    (pltpu.SemaphoreType.DMA((3, nch)),) * 4, start, relay, finish)


def all_sum_small(vec, *, name):
    R, C = vec.shape

    def body(v_ref, out_ref, buf, send_sems, recv_sems):
        x, y, c = _my_place()
        me = 4 * x + 2 * y + c
        buf[me] = v_ref[...]
        copies = []
        for k in range(1, N_DEV):
            px, py, pc = x ^ (k >> 2), y ^ ((k >> 1) & 1), c ^ (k & 1)
            copies.append(pltpu.make_async_remote_copy(
                src_ref=v_ref, dst_ref=buf.at[me], send_sem=send_sems.at[k - 1], recv_sem=recv_sems.at[k - 1],
                device_id=(px, py, pc), device_id_type=pl.DeviceIdType.MESH))
        for cp in copies:
            cp.start()
        for k in range(1, N_DEV):
            px, py, pc = x ^ (k >> 2), y ^ ((k >> 1) & 1), c ^ (k & 1)
            pltpu.make_async_remote_copy(
                src_ref=v_ref, dst_ref=buf.at[4 * px + 2 * py + pc], send_sem=send_sems.at[k - 1],
                recv_sem=recv_sems.at[k - 1], device_id=(px, py, pc), device_id_type=pl.DeviceIdType.MESH).wait_recv()
        for cp in copies:
            cp.wait_send()
        acc = buf[0]
        for d in range(1, N_DEV):
            acc = acc + buf[d]
        out_ref[...] = acc

    return pl.pallas_call(
        body, name=name, in_specs=[pl.BlockSpec(memory_space=pltpu.VMEM)], out_specs=pl.BlockSpec(memory_space=pltpu.VMEM),
        out_shape=_sds((R, C), F32),
        scratch_shapes=[pltpu.VMEM((N_DEV, R, C), F32), pltpu.SemaphoreType.DMA((N_DEV - 1,)),
                        pltpu.SemaphoreType.DMA((N_DEV - 1,))],
        compiler_params=pltpu.CompilerParams(has_side_effects=True, vmem_limit_bytes=VMEM_LIMIT),
    )(vec)


def sum_chips(own, others, *, name, tr=512):
    R, C = own.shape
    tr = _pick(R, (tr, 384, 352, 256, 128, 64, 32, 16))

    def body(o_ref, p_ref, s_ref):
        acc = o_ref[...].astype(F32)
        for k in range(3):
            acc = acc + p_ref[k].astype(F32)
        s_ref[...] = acc

    return pl.pallas_call(
        body, name=name, grid=(R // tr,),
        in_specs=[pl.BlockSpec((tr, C), lambda i: (i, 0)), pl.BlockSpec((3, tr, C), lambda i: (0, i, 0))],
        out_specs=pl.BlockSpec((tr, C), lambda i: (i, 0)), out_shape=_sds((R, C), F32),
        compiler_params=_cparams(("parallel",)),
    )(own, others)


WEIGHTS = ['attn_norm', 'w_in', 'ssd_conv_w', 'ssd_conv_b', 'ssd_dt_bias', 'ssd_a_log', 'ssd_d', 'ssd_norm', 'pool_w',
           'pool_scale', 'mla_q_norm', 'mla_w_uq', 'mla_kv_norm', 'mla_w_ukv', 'w_out', 'ffn_norm', 'ffn_w_up',
           'ffn_conv_w', 'ffn_conv_b', 'ffn_w_down', 'final_norm']
BIG = {'w_in': 2, 'mla_w_uq': 2, 'mla_w_ukv': 2, 'w_out': 1, 'ffn_w_up': 2, 'ffn_w_down': 1}
CONV_SHARDED = ('ssd_conv_w', 'ffn_conv_w')


def _zeros_cols(w, n):
    return jnp.zeros((w.shape[0], n), w.dtype)


def _w_in_to_padded(w):
    return jnp.concatenate([w[:, 0:2560], w[:, 2576:3088], w[:, 3088:3472], w[:, 2560:2576], _zeros_cols(w, 112),
                            w[:, 3472:3728], _zeros_cols(w, 64), w[:, 3728:3760], _zeros_cols(w, 32 + 128)], axis=1)


def _w_in_from_padded(g):
    return jnp.concatenate([g[:, 0:2560], g[:, PDT0:PDT0 + SSD_HEADS], g[:, PU0:PU0 + POOL_WIDTH],
                            g[:, PCQ0:PCQ0 + MLA_Q_RANK], g[:, PCKV0:PCKV0 + MLA_KV_RANK],
                            g[:, PKPE0 + ROPE0:PKPE0 + ROPE0 + MLA_ROPE]], axis=1)


def _w_uq_to_padded(w):
    r = w.reshape(MLA_Q_RANK, MLA_HEADS, MLA_QK)
    return jnp.pad(r, ((0, 0), (0, 0), (0, HEAD_W - MLA_QK))).reshape(MLA_Q_RANK, MLA_PAD)


def _w_uq_from_padded(g):
    return g.reshape(MLA_Q_RANK, MLA_HEADS, HEAD_W)[:, :, :MLA_QK].reshape(MLA_Q_RANK, MLA_HEADS * MLA_QK)


def _w_ukv_to_padded(w):
    r = w.reshape(MLA_KV_RANK, MLA_HEADS, MLA_NOPE + MLA_V)
    pad = lambda t: jnp.pad(t, ((0, 0), (0, 0), (0, HEAD_W - t.shape[2]))).reshape(MLA_KV_RANK, MLA_PAD)
    return jnp.concatenate([pad(r[:, :, :MLA_NOPE]), pad(r[:, :, MLA_NOPE:])], axis=1)


def _w_ukv_from_padded(g):
    kk = g[:, :MLA_PAD].reshape(MLA_KV_RANK, MLA_HEADS, HEAD_W)[:, :, :MLA_NOPE]
    vv = g[:, MLA_PAD:].reshape(MLA_KV_RANK, MLA_HEADS, HEAD_W)[:, :, :MLA_V]
    return jnp.concatenate([kk, vv], axis=2).reshape(MLA_KV_RANK, MLA_HEADS * (MLA_NOPE + MLA_V))


def _w_out_to_padded(w):
    att = w[SSD_WIDTH + POOL_WIDTH:].reshape(MLA_HEADS, MLA_V, D_MODEL)
    att = jnp.pad(att, ((0, 0), (0, HEAD_W - MLA_V), (0, 0))).reshape(MLA_PAD, D_MODEL)
    return jnp.concatenate([w[:SSD_WIDTH + POOL_WIDTH], att], axis=0)


def _w_out_from_padded(g):
    att = g[SSD_WIDTH + POOL_WIDTH:].reshape(MLA_HEADS, HEAD_W, D_MODEL)[:, :MLA_V].reshape(MLA_WIDTH, D_MODEL)
    return jnp.concatenate([g[:SSD_WIDTH + POOL_WIDTH], att], axis=0)


def _pad_lanes(v, n=LANE):
    return jnp.pad(v.reshape(1, -1), ((0, 0), (0, n - v.size)))


def _pack_rows(parts, cols, dtype, row_multiple=16):
    flat = jnp.concatenate([p.astype(dtype).reshape(-1) for p in parts])
    rows = -(-flat.size // (cols * row_multiple)) * row_multiple
    return jnp.pad(flat, (0, rows * cols - flat.size)).reshape(rows, cols)


def _unpack_rows(packed, shapes):
    flat = packed.reshape(-1)
    out, at = [], 0
    for s in shapes:
        n = math.prod(s)
        out.append(flat[at:at + n].reshape(s))
        at += n
    return out


def _split_for_chips(g, axis):
    a, b = g.shape
    if axis == 0:
        return g.reshape(N_CHIPS, a // N_CHIPS, b)
    return g.reshape(a, N_CHIPS, b // N_CHIPS).transpose(1, 0, 2)


_MATMUL_OPERANDS = {'w_in': ('w_in_p', _w_in_to_padded), 'mla_w_uq': ('w_uq_p', _w_uq_to_padded),
                    'mla_w_ukv': ('w_ukv_p', _w_ukv_to_padded), 'w_out': ('w_out_p', _w_out_to_padded),
                    'ffn_w_up': ('w_up', lambda a: a), 'ffn_w_down': ('w_down', lambda a: a)}


def _matmul_weights(full):
    return {_MATMUL_OPERANDS[k][0]: _MATMUL_OPERANDS[k][1](a) for k, a in full.items()}


def _layer_weights(full, small, l):
    w = _matmul_weights(full)
    for k in ('attn_norm', 'ssd_conv_w', 'ssd_conv_b', 'ssd_norm', 'pool_w', 'pool_scale', 'mla_q_norm', 'mla_kv_norm',
              'ffn_norm', 'ffn_conv_w', 'ffn_conv_b'):
        w[k] = small[k][l]
    w['dtb'] = _pad_lanes(small['ssd_dt_bias'][l])
    w['alog'] = _pad_lanes(small['ssd_a_log'][l])
    w['dchan'] = jnp.repeat(small['ssd_d'][l], SSD_HEAD_DIM).reshape(1, SSD_WIDTH)
    w['ssd_norm'] = w['ssd_norm'].reshape(1, SSD_WIDTH)
    return w


def _layer_fwd(x, pos, invf, w, S, l, hosted=None, scan_hosted=None, late_weights=None, post_weights=None):
    n = lambda s: f"{s}_l{l}"
    h1 = rmsnorm_fwd(x, w['attn_norm'], name=n("attn_norm"))
    proj = matmul(h1, w['w_in_p'], name=n("w_in"))
    xc = ssd_conv_fwd(proj, w['ssd_conv_w'], w['ssd_conv_b'], S, name=n("ssd_conv"))
    ypre, yssd, hin, *arrived = ssd_fwd(proj, xc, w['dtb'], w['alog'], w['dchan'], w['ssd_norm'], S, name=n("ssd_scan"),
                                        hosted=scan_hosted)
    if late_weights:
        w = {**w, **late_weights(arrived)}
    ypool, pooled = pool_fwd(proj, w['pool_w'], w['pool_scale'], S, name=n("pool"))
    q, k, v, cqn, ckvn = mla_prep_fwd(proj, pos, invf, w['mla_q_norm'], w['w_uq_p'], w['mla_kv_norm'], w['w_ukv_p'],
                                      name=n("mla_prep"))
    o, lse, *exchanged = flash_fwd(q, k, v, S, name=n("attention"), hosted=hosted)
    if post_weights:
        w = {**w, **post_weights(exchanged)}
    mix = jnp.concatenate([yssd, ypool, o.astype(MXU_DTYPE)], axis=1)
    x2 = matmul(mix, w['w_out_p'], res=x, name=n("w_out"))
    h2 = rmsnorm_fwd(x2, w['ffn_norm'], name=n("ffn_norm"))
    up = matmul(h2, w['w_up'], name=n("ffn_up"))
    act = ffn_conv_gate_fwd(up, w['ffn_conv_w'], w['ffn_conv_b'], S, name=n("ffn_conv_gate"))
    x3 = matmul(act, w['w_down'], res=x2, name=n("ffn_down"))
    saved = dict(x=x, h1=h1, proj=proj, xc=xc, ypre=ypre, hin=hin, pooled=pooled, q=q, k=k, v=v, cqn=cqn, ckvn=ckvn,
                 o=o, lse=lse, mix=mix, x2=x2, h2=h2, up=up, act=act)
    return x3, saved, w, exchanged


def _layer_bwd(dx3, pos, invf, w, s, S, l, host=None, late_host=None):
    n = lambda t: f"{t}_l{l}"
    g = {}
    dact = matmul(dx3, w['w_down'], nt=True, name=n("d_ffn_down"))
    g['ffn_w_down'] = matmul_tn(s['act'], dx3, name=n("g_ffn_down"))
    dup_g, dup_v, st = ffn_conv_gate_bwd(s['up'], w['ffn_conv_w'], w['ffn_conv_b'], dact, S, name=n("d_ffn_conv_gate"))
    g['ffn_conv_w'], g['ffn_conv_b'] = st[:FFN_CONV], st[FFN_CONV]
    dh2 = matmul(dup_g, w['w_up'], nt=True, kblock=0, name=n("d_ffn_up_g"))
    dh2 = matmul(dup_v, w['w_up'], nt=True, kblock=1, res=dh2, name=n("d_ffn_up_v"))
    g['ffn_w_up'] = jnp.concatenate([matmul_tn(s['h2'], dup_g, name=n("g_ffn_up_g")),
                                     matmul_tn(s['h2'], dup_v, name=n("g_ffn_up_v"))], axis=1)
    dx2, gn = rmsnorm_bwd(s['x2'], w['ffn_norm'], dh2, dx3, name=n("d_ffn_norm"))
    g['ffn_norm'] = gn[0]
    dmix = matmul(dx2, w['w_out_p'], nt=True, name=n("d_w_out"))
    g['w_out'] = _w_out_from_padded(matmul_tn(s['mix'], dx2, name=n("g_w_out")))
    dxc, ddt, dz, sm, gsn = ssd_bwd(s['proj'], s['xc'], s['ypre'], s['hin'], dmix, w['dtb'], w['alog'], w['dchan'],
                                    w['ssd_norm'], S, name=n("d_ssd_scan"))
    g['ssd_a_log'], g['ssd_dt_bias'], g['ssd_d'] = sm[0, :SSD_HEADS], sm[1, :SSD_HEADS], sm[2, :SSD_HEADS]
    g['ssd_norm'] = gsn[0]
    dxbc, st = ssd_conv_bwd(s['proj'], w['ssd_conv_w'], w['ssd_conv_b'], dxc, S, name=n("d_ssd_conv"))
    g['ssd_conv_w'], g['ssd_conv_b'] = st[:SSD_CONV], st[SSD_CONV]
    du, g['pool_w'], gps = pool_bwd(dmix, s['pooled'], w['pool_w'], w['pool_scale'], S, name=n("d_pool"))
    g['pool_scale'] = gps[0]
    dq, dk, dv, *exchanged = flash_bwd(s['q'], s['k'], s['v'], s['o'], s['lse'], dmix, S, name=n("d_attention"),
                                       hosted=host(g) if host else None)
    dqp, dkvp, dcq, dckv, dkpe, gqn, gkn = mla_prep_bwd(s['proj'], pos, invf, w['mla_q_norm'], w['w_uq_p'],
                                                        w['mla_kv_norm'], w['w_ukv_p'], dq, dk, dv, name=n("d_mla_prep"))
    g['mla_q_norm'], g['mla_kv_norm'] = gqn[0], gkn[0]
    g['mla_w_uq'] = _w_uq_from_padded(matmul_tn(s['cqn'], dqp, name=n("g_w_uq")))
    g['mla_w_ukv'] = _w_ukv_from_padded(matmul_tn(s['ckvn'], dkvp, name=n("g_w_ukv")))
    dproj = jnp.concatenate([dz, dxbc, du, dcq, ddt, dckv, dkpe, jnp.zeros_like(dkpe)], axis=1)
    g['w_in'] = _w_in_from_padded(matmul_tn(s['h1'], dproj, name=n("g_w_in")))
    dh1 = matmul(dproj, w['w_in_p'], nt=True, name=n("d_w_in"), hosted=late_host(g) if late_host else None)
    dh1, late_exchanged = (dh1[0], dh1[1:]) if late_host else (dh1, [])
    dx, gn = rmsnorm_bwd(s['x'], w['attn_norm'], dh1, dx2, name=n("d_attn_norm"))
    g['attn_norm'] = gn[0]
    return dx, g, exchanged, late_exchanged


def _rope_inputs(positions):
    pos = positions.reshape(-1, 1).astype(F32)
    inv_freq = ROPE_THETA ** (-jnp.arange(0, MLA_ROPE, 2, dtype=F32) / MLA_ROPE)
    invf = jnp.concatenate([jnp.zeros((ROPE0,), F32), inv_freq, inv_freq,
                            jnp.zeros((HEAD_W - ROPE0 - MLA_ROPE,), F32)]).reshape(1, HEAD_W)
    return pos, invf


EARLY_GRADS = ('w_out', 'ffn_w_up', 'ffn_w_down')


def kernel(x, positions, attn_norm, w_in, ssd_conv_w, ssd_conv_b, ssd_dt_bias, ssd_a_log, ssd_d, ssd_norm, pool_w, pool_scale, mla_q_norm, mla_w_uq, mla_kv_norm, mla_w_ukv, w_out, ffn_norm, ffn_w_up, ffn_conv_w, ffn_conv_b, ffn_w_down, final_norm, loss_target, m_attn_norm, m_w_in, m_ssd_conv_w, m_ssd_conv_b, m_ssd_dt_bias, m_ssd_a_log, m_ssd_d, m_ssd_norm, m_pool_w, m_pool_scale, m_mla_q_norm, m_mla_w_uq, m_mla_kv_norm, m_mla_w_ukv, m_w_out, m_ffn_norm, m_ffn_w_up, m_ffn_conv_w, m_ffn_conv_b, m_ffn_w_down, m_final_norm, v_attn_norm, v_w_in, v_ssd_conv_w, v_ssd_conv_b, v_ssd_dt_bias, v_ssd_a_log, v_ssd_d, v_ssd_norm, v_pool_w, v_pool_scale, v_mla_q_norm, v_mla_w_uq, v_mla_kv_norm, v_mla_w_ukv, v_w_out, v_ffn_norm, v_ffn_w_up, v_ffn_conv_w, v_ffn_conv_b, v_ffn_w_down, v_final_norm):
    wv = dict(zip(WEIGHTS, (attn_norm, w_in, ssd_conv_w, ssd_conv_b, ssd_dt_bias, ssd_a_log, ssd_d, ssd_norm, pool_w,
                            pool_scale, mla_q_norm, mla_w_uq, mla_kv_norm, mla_w_ukv, w_out, ffn_norm, ffn_w_up,
                            ffn_conv_w, ffn_conv_b, ffn_w_down, final_norm)))
    mv = dict(zip(WEIGHTS, (m_attn_norm, m_w_in, m_ssd_conv_w, m_ssd_conv_b, m_ssd_dt_bias, m_ssd_a_log, m_ssd_d,
                            m_ssd_norm, m_pool_w, m_pool_scale, m_mla_q_norm, m_mla_w_uq, m_mla_kv_norm, m_mla_w_ukv,
                            m_w_out, m_ffn_norm, m_ffn_w_up, m_ffn_conv_w, m_ffn_conv_b, m_ffn_w_down, m_final_norm)))
    vv = dict(zip(WEIGHTS, (v_attn_norm, v_w_in, v_ssd_conv_w, v_ssd_conv_b, v_ssd_dt_bias, v_ssd_a_log, v_ssd_d,
                            v_ssd_norm, v_pool_w, v_pool_scale, v_mla_q_norm, v_mla_w_uq, v_mla_kv_norm, v_mla_w_ukv,
                            v_w_out, v_ffn_norm, v_ffn_w_up, v_ffn_conv_w, v_ffn_conv_b, v_ffn_w_down, v_final_norm)))
    Bl, S, D = x.shape
    chip = 2 * lax.axis_index("x") + lax.axis_index("y")
    core = lax.axis_index("c")

    big_names = list(BIG)
    first_names = ['w_in']
    scan_names = ['mla_w_uq', 'mla_w_ukv']
    post_names = [k for k in big_names if k not in first_names + scan_names]

    def shards(l, names):
        return [wv[k][l].astype(MXU_DTYPE) for k in names]

    def whole_weights(names, own, others):
        rel = [relation_of(j, chip) for j in range(N_CHIPS)]
        return {k: jnp.concatenate(
            [jnp.where(r < 0, mine, jnp.where(r == 0, theirs[0], jnp.where(r == 1, theirs[1], theirs[2]))) for r in rel],
            axis=BIG[k] - 1) for k, mine, theirs in zip(names, own, others)}

    first_others = run_exchange(gather_exchange(shards(0, first_names)), name="gather_w_in_l0")
    placed = []
    for k in CONV_SHARDED:
        sh = wv[k]
        whole = jnp.zeros(sh.shape[:-1] + (sh.shape[-1] * N_CHIPS,), F32)
        whole = lax.dynamic_update_slice_in_dim(whole, sh, chip * sh.shape[-1], axis=sh.ndim - 1)
        placed.append(jnp.where(core == 1, whole, 0.0))
    conv_full = _unpack_rows(all_sum_small(_pack_rows(placed, LANE, F32), name="gather_conv_weights"),
                             [p.shape for p in placed])
    small = {k: wv[k] for k in WEIGHTS if k not in BIG}
    small.update(dict(zip(CONV_SHARDED, conv_full)))

    T = Bl * S
    pos, invf = _rope_inputs(positions)
    group_a = [(k, 1) for k in big_names] + [(k, 0) for k in EARLY_GRADS]
    group_b = [(k, 0) for k in big_names if k not in EARLY_GRADS]

    def scatter_of(group, layer_grads):
        send = [_split_for_chips(layer_grads[l][k], BIG[k] - 1) for k, l in group]
        return send, scatter_exchange(send)

    layer_grads = [None] * DEPTH
    sent = {}
    w0 = _layer_weights(whole_weights(first_names, shards(0, first_names), first_others), small, 0)
    h, saved0, w0, arrived = _layer_fwd(
        x.reshape(T, D), pos, invf, w0, S, 0,
        scan_hosted=gather_exchange(shards(0, scan_names)),
        late_weights=lambda got: _matmul_weights(whole_weights(scan_names, shards(0, scan_names), got)),
        hosted=gather_exchange(shards(0, post_names) + shards(1, big_names)),
        post_weights=lambda got: _matmul_weights(whole_weights(post_names, shards(0, post_names), got[:len(post_names)])))
    others1 = arrived[len(post_names):]
    w1 = _layer_weights(whole_weights(big_names, shards(1, big_names), others1), small, 1)
    h, saved1, w1, _ = _layer_fwd(h, pos, invf, w1, S, 1)
    loss, dh, g_final_norm = final_loss(h, small['final_norm'], loss_target.reshape(T, D))
    dh, layer_grads[1], _, _ = _layer_bwd(dh, pos, invf, w1, saved1, S, 1)

    def host_a(early):
        layer_grads[0] = early
        sent['a'], ex = scatter_of(group_a, layer_grads)
        return ex

    def host_b(_):
        sent['b'], ex = scatter_of(group_b, layer_grads)
        return ex

    dx, layer_grads[0], others_a, others_b = _layer_bwd(dh, pos, invf, w0, saved0, S, 0, host=host_a, late_host=host_b)
    small_names = [k for k in WEIGHTS if k not in BIG]
    grads = {k: jnp.stack([layer_grads[l][k] for l in range(DEPTH)]) for k in small_names if k != 'final_norm'}
    grads['final_norm'] = g_final_norm[0]

    pieces = [{}, {}]
    for tag, group, others in (('a', group_a, others_a), ('b', group_b, others_b)):
        mine = [sum_chips(lax.dynamic_index_in_dim(s, chip, 0, keepdims=False), o, name=f"sum_chips_{k}_l{l}")
                for s, o, (k, l) in zip(sent[tag], others, group)]
        theirs = sibling_swap(mine, name=f"swap_core_sums_{tag}")
        pieces[0].update(dict(zip(group, mine)))
        pieces[1].update(dict(zip(group, theirs)))
    small_sum = all_sum_small(_pack_rows([grads[k] for k in small_names] + [loss[0, :1]], LANE, F32), name="sum_small_grads")
    summed = _unpack_rows(small_sum, [grads[k].shape for k in small_names] + [(1,)])
    loss_total = summed[-1].reshape(())
    g_small = dict(zip(small_names, summed[:-1]))
    for k in CONV_SHARDED:
        n = wv[k].shape[-1]
        g_small[k] = lax.dynamic_slice_in_dim(g_small[k], chip * n, n, axis=g_small[k].ndim - 1)

    out_g, out_d, out_m, out_v = {}, {}, {}, {}
    for k in big_names:
        g_layers = [[pieces[0][(k, l)], pieces[1][(k, l)]] for l in range(DEPTH)]
        out_g[k], out_d[k], out_m[k], out_v[k] = adamw_layers(wv[k], g_layers, mv[k], vv[k], name=f"adamw_{k}")
    at_least_2d = lambda a: a.reshape(1, -1) if a.ndim == 1 else a
    res = adamw_small(*[[at_least_2d(d[k]) for k in small_names] for d in (wv, g_small, mv, vv)], name="adamw_small")
    out_g.update(g_small)
    for dst, r in zip((out_d, out_m, out_v), res):
        dst.update({k: a.reshape(wv[k].shape) for k, a in zip(small_names, r)})
    return (loss_total, dx.reshape(Bl, S, D), *[out_g[k] for k in WEIGHTS], *[out_d[k] for k in WEIGHTS],
            *[out_m[k] for k in WEIGHTS], *[out_v[k] for k in WEIGHTS])
```

```python
import functools
import math
from typing import Callable, NamedTuple

import jax
import jax.numpy as jnp
from jax import lax
from jax.experimental import pallas as pl
from jax.experimental.pallas import tpu as pltpu

F32 = jnp.float32
MXU_DTYPE = jnp.bfloat16
HI = lax.Precision.HIGHEST

D_MODEL = 1024
DEPTH = 2
EPS = 1e-6
SSD_HEADS = 16
SSD_HEAD_DIM = 64
SSD_WIDTH = 1024
SSD_GROUPS = 2
SSD_STATE = 128
SSD_CONV = 4
SSD_CHUNK = 128
SSD_CONV_CH = 1536
POOL_GROUPS = 4
POOL_GROUP_DIM = 128
POOL_WIDTH = 512
POOL_WINDOWS = (2, 4, 8, 16)
MLA_HEADS = 8
MLA_Q_RANK = 384
MLA_KV_RANK = 256
MLA_NOPE = 64
MLA_ROPE = 32
MLA_V = 64
MLA_QK = 96
MLA_WIDTH = 512
ROPE_THETA = 10000.0
MIX_WIDTH = 2048
IN_COLS = 3760
D_FF = 2816
FFN_CONV = 3
ADAM_LR = 0.001
ADAM_B1 = 0.9
ADAM_B2 = 0.999
ADAM_EPS = 1e-08
ADAM_WD = 0.01
ADAM_STEP = 10

LANE = 128
HALO = 8
POOL_HALO = 16
PZ0 = 0
PXBC0 = 1024
PU0 = 2560
PCQ0 = 3072
PDT0 = 3456
PCKV0 = 3584
PKPE0 = 3840
PROJ_W = 4096
HEAD_W = 128
MLA_PAD = MLA_HEADS * HEAD_W
MIXP = SSD_WIDTH + POOL_WIDTH + MLA_PAD
N_CHIPS = 4
N_DEV = 8
VMEM_LIMIT = 56 * 1024 * 1024
MATMUL_VMEM_BUDGET = 40 * 1024 * 1024


def _cparams(dims, vmem=None):
    return pltpu.CompilerParams(dimension_semantics=dims, vmem_limit_bytes=vmem or VMEM_LIMIT)


def _sds(shape, dtype):
    return jax.ShapeDtypeStruct(tuple(shape), dtype)


def _mx(v):
    return v.astype(MXU_DTYPE)


def _dot(a, b):
    return jnp.dot(_mx(a), _mx(b), preferred_element_type=F32)


def _dot_nt(a, b):
    return lax.dot_general(_mx(a), _mx(b), (((1,), (1,)), ((), ())), preferred_element_type=F32)


def _dot_tn(a, b):
    return lax.dot_general(_mx(a), _mx(b), (((0,), (0,)), ((), ())), preferred_element_type=F32)


def _dot_hi(a, b):
    return jnp.dot(a, b, preferred_element_type=F32, precision=HI)


def _sigmoid(v):
    return 1.0 / (1.0 + jnp.exp(-v))


def _pick(n, prefs):
    for p in prefs:
        if n % p == 0:
            return p
    return n


def matmul(a, b, *, res=None, out_dtype=F32, name, nt=False, kblock=0, tm=None, tn=None, hosted=None):
    M, K = a.shape
    N = b.shape[0] if nt else b.shape[1]
    assert (b.shape[1] % K == 0) if nt else (K == b.shape[0] and kblock == 0)
    tn = tn or _pick(N, (1024, 1408, 1280, 512, 256, 128))
    a_bytes, o_bytes = jnp.dtype(a.dtype).itemsize, jnp.dtype(out_dtype).itemsize
    fits = lambda t: 2 * (t * K * a_bytes + K * tn * 2 + t * tn * (o_bytes + (4 if res is not None else 0))) <= MATMUL_VMEM_BUDGET
    tm = tm or next(t for t in (2048, 1024, 512, 256, 128) if M % t == 0 and (fits(t) or t == 128))
    grid = (M // tm, N // tn)
    wrap, h_in, h_ospecs, h_oshapes, h_scratch = _hosting(hosted, grid, 2 if res is None else 3, 1, 0)

    def body(*refs):
        a_ref, b_ref = refs[:2]
        o_ref = refs[-1]
        out = (_dot_nt if nt else _dot)(a_ref[...], b_ref[...])
        if res is not None:
            out = out + refs[2][...]
        o_ref[...] = out.astype(out_dtype)

    b_spec = pl.BlockSpec((tn, K), lambda i, j: (j, kblock)) if nt else pl.BlockSpec((K, tn), lambda i, j: (0, j))
    in_specs = [pl.BlockSpec((tm, K), lambda i, j: (i, 0)), b_spec]
    args = [a, b]
    if res is not None:
        in_specs.append(pl.BlockSpec((tm, tn), lambda i, j: (i, j)))
        args.append(res)
    out = pl.pallas_call(
        wrap(body), name=name, grid=grid, in_specs=in_specs + [pl.BlockSpec(memory_space=pl.ANY)] * len(h_in),
        out_specs=[pl.BlockSpec((tm, tn), lambda i, j: (i, j))] + h_ospecs, out_shape=[_sds((M, N), out_dtype)] + h_oshapes,
        scratch_shapes=h_scratch,
        compiler_params=_cparams(("arbitrary", "arbitrary") if hosted else ("parallel", "parallel")),
    )(*args, *h_in)
    return out if hosted else out[0]


def matmul_tn(a, g, *, name, tm=None, tn=None, tk=None):
    T, M = a.shape
    T2, N = g.shape
    assert T == T2
    tm = tm or _pick(M, (1408, 1280, 1024, 512, 384, 256, 128))
    tn = tn or _pick(N, (1024, 1408, 512, 256, 128))
    tk = tk or _pick(T, (1024, 512, 256, 128))
    nk = T // tk

    def body(a_ref, g_ref, o_ref, acc):
        k = pl.program_id(2)
        part = _dot_tn(a_ref[...], g_ref[...])

        @pl.when(k == 0)
        def _():
            acc[...] = part

        @pl.when(k > 0)
        def _():
            acc[...] += part

        @pl.when(k == nk - 1)
        def _():
            o_ref[...] = acc[...].astype(o_ref.dtype)

    return pl.pallas_call(
        body, name=name, grid=(M // tm, N // tn, nk),
        in_specs=[pl.BlockSpec((tk, tm), lambda i, j, k: (k, i)), pl.BlockSpec((tk, tn), lambda i, j, k: (k, j))],
        out_specs=pl.BlockSpec((tm, tn), lambda i, j, k: (i, j)), out_shape=_sds((M, N), MXU_DTYPE),
        scratch_shapes=[pltpu.VMEM((tm, tn), F32)],
        compiler_params=_cparams(("parallel", "parallel", "arbitrary")),
    )(a, g)


def rmsnorm_fwd(x, gamma, *, name, tm=512):
    T, D = x.shape
    tm = _pick(T, (tm, 256, 128))

    def body(x_ref, g_ref, o_ref):
        xv = x_ref[...]
        r = lax.rsqrt(jnp.mean(xv * xv, axis=-1, keepdims=True) + EPS)
        o_ref[...] = ((xv * r) * g_ref[...]).astype(MXU_DTYPE)

    return pl.pallas_call(
        body, name=name, grid=(T // tm,),
        in_specs=[pl.BlockSpec((tm, D), lambda i: (i, 0)), pl.BlockSpec((1, D), lambda i: (0, 0))],
        out_specs=pl.BlockSpec((tm, D), lambda i: (i, 0)), out_shape=_sds((T, D), MXU_DTYPE),
        compiler_params=_cparams(("parallel",)),
    )(x, gamma.reshape(1, D))


def _rms_bwd_tile(xv, gamma, dh):
    r = lax.rsqrt(jnp.mean(xv * xv, axis=-1, keepdims=True) + EPS)
    xh = xv * r
    dg = jnp.sum(dh * xh, axis=0, keepdims=True)
    dn = dh * gamma
    dx = r * (dn - xh * jnp.mean(dn * xh, axis=-1, keepdims=True))
    return dx, dg


def rmsnorm_bwd(x, gamma, dh, dres, *, name, tm=512):
    T, D = x.shape
    tm = _pick(T, (tm, 256, 128))

    def body(x_ref, g_ref, dh_ref, dr_ref, dx_ref, dg_ref):
        dx, dg = _rms_bwd_tile(x_ref[...], g_ref[...], dh_ref[...].astype(F32))
        dx_ref[...] = dx + dr_ref[...]

        @pl.when(pl.program_id(0) == 0)
        def _():
            dg_ref[...] = dg

        @pl.when(pl.program_id(0) > 0)
        def _():
            dg_ref[...] += dg

    row = pl.BlockSpec((tm, D), lambda i: (i, 0))
    vec = pl.BlockSpec((1, D), lambda i: (0, 0))
    return pl.pallas_call(
        body, name=name, grid=(T // tm,), in_specs=[row, vec, row, row], out_specs=[row, vec],
        out_shape=[_sds((T, D), F32), _sds((1, D), F32)], compiler_params=_cparams(("arbitrary",)),
    )(x, gamma.reshape(1, D), dh, dres)


def final_loss(x, gamma, target, *, name="final_loss", tm=512):
    T, D = x.shape
    tm = _pick(T, (tm, 256, 128))

    def body(x_ref, g_ref, t_ref, l_ref, dx_ref, dg_ref):
        xv = x_ref[...]
        gam = g_ref[...]
        r = lax.rsqrt(jnp.mean(xv * xv, axis=-1, keepdims=True) + EPS)
        y = (xv * r) * gam
        err = y - t_ref[...]
        part = 0.5 * jnp.sum(jnp.sum(err * err, axis=-1, keepdims=True) / D, axis=0, keepdims=True)
        dx, dg = _rms_bwd_tile(xv, gam, err / D)
        dx_ref[...] = dx

        @pl.when(pl.program_id(0) == 0)
        def _():
            dg_ref[...] = dg
            l_ref[...] = jnp.broadcast_to(part, l_ref.shape)

        @pl.when(pl.program_id(0) > 0)
        def _():
            dg_ref[...] += dg
            l_ref[...] += jnp.broadcast_to(part, l_ref.shape)

    row = pl.BlockSpec((tm, D), lambda i: (i, 0))
    vec = pl.BlockSpec((1, D), lambda i: (0, 0))
    return pl.pallas_call(
        body, name=name, grid=(T // tm,), in_specs=[row, vec, row],
        out_specs=[pl.BlockSpec((1, LANE), lambda i: (0, 0)), row, vec],
        out_shape=[_sds((1, LANE), F32), _sds((T, D), F32), _sds((1, D), F32)],
        compiler_params=_cparams(("arbitrary",)),
    )(x, gamma.reshape(1, D), target)


def _halo_prev(ts):
    return lambda i, j, off=0: (jnp.maximum(i * (ts // HALO) - 1, 0), j + off)


def _cat_prev(cur, halo, first):
    return jnp.concatenate([jnp.where(first, 0.0, halo), cur], axis=0)


def _cat_next(cur, halo, last):
    return jnp.concatenate([cur, jnp.where(last, 0.0, halo)], axis=0)


def _delayed(cat, r):
    if r == 0:
        return cat[HALO:]
    return pltpu.roll(cat, r, axis=0)[HALO:]


def _advanced(cat, r):
    n = cat.shape[0]
    if r == 0:
        return cat[:n - HALO]
    return pltpu.roll(cat, n - r, axis=0)[:n - HALO]


def _conv_pre(cat, w, b, K):
    acc = _delayed(cat, K - 1) * w[0:1, :] + b
    for k in range(1, K):
        acc = acc + _delayed(cat, K - 1 - k) * w[k:k + 1, :]
    return acc


def _pad_rows8(w):
    return jnp.pad(w, ((0, 8 - w.shape[0]), (0, 0)))


def ssd_conv_fwd(proj, w, b, S, *, name, ts=1024, tc=512):
    T = proj.shape[0]
    C, K = SSD_CONV_CH, SSD_CONV
    ts = _pick(S, (ts, 256, 128))
    off = PXBC0 // tc
    ns = S // ts

    def body(x_ref, h_ref, w_ref, b_ref, o_ref):
        first = (pl.program_id(0) % ns) == 0
        pre = _conv_pre(_cat_prev(x_ref[...], h_ref[...], first), w_ref[...], b_ref[...], K)
        o_ref[...] = pre * _sigmoid(pre)

    return pl.pallas_call(
        body, name=name, grid=(T // ts, C // tc),
        in_specs=[pl.BlockSpec((ts, tc), lambda i, j: (i, j + off)),
                  pl.BlockSpec((HALO, tc), functools.partial(_halo_prev(ts), off=off)),
                  pl.BlockSpec((8, tc), lambda i, j: (0, j)), pl.BlockSpec((1, tc), lambda i, j: (0, j))],
        out_specs=pl.BlockSpec((ts, tc), lambda i, j: (i, j)), out_shape=_sds((T, C), F32),
        compiler_params=_cparams(("parallel", "parallel")),
    )(proj, proj, _pad_rows8(w), b.reshape(1, C))


def _conv_stats(dpre, cat, K, ts):
    rows = [jnp.sum(dpre[:ts] * _delayed(cat, K - 1 - k)[:ts], axis=0, keepdims=True) for k in range(K)]
    rows.append(jnp.sum(dpre[:ts], axis=0, keepdims=True))
    rows.append(jnp.zeros((8 - len(rows), dpre.shape[1]), F32))
    return jnp.concatenate(rows, axis=0)


def _conv_transposed(dpre, wv, K):
    acc = _advanced(dpre, K - 1) * wv[0:1, :]
    for k in range(1, K):
        acc = acc + _advanced(dpre, K - 1 - k) * wv[k:k + 1, :]
    return acc


def ssd_conv_bwd(proj, w, b, dxc, S, *, name, ts=512, tc=512):
    T = proj.shape[0]
    C, K = SSD_CONV_CH, SSD_CONV
    ts = _pick(S, (ts, 256, 128))
    off = PXBC0 // tc
    ns = S // ts
    nblk = T // HALO

    def body(x_ref, xp_ref, xn_ref, w_ref, b_ref, d_ref, dn_ref, o_ref, acc_ref):
        i = pl.program_id(1)
        first = (i % ns) == 0
        last = (i % ns) == ns - 1
        cat = jnp.concatenate([jnp.where(first, 0.0, xp_ref[...]), x_ref[...], xn_ref[...]], axis=0)
        wv = w_ref[...]
        pre = _conv_pre(cat, wv, b_ref[...], K)
        sg = _sigmoid(pre)
        dpre = _cat_next(d_ref[...], dn_ref[...], last) * (sg * (1.0 + pre * (1.0 - sg)))
        o_ref[...] = _conv_transposed(dpre, wv, K).astype(o_ref.dtype)
        part = _conv_stats(dpre, cat, K, ts)

        @pl.when(i == 0)
        def _():
            acc_ref[...] = part

        @pl.when(i > 0)
        def _():
            acc_ref[...] += part

    hp = _halo_prev(ts)
    hn = lambda i: jnp.minimum((i + 1) * (ts // HALO), nblk - 1)
    return pl.pallas_call(
        body, name=name, grid=(C // tc, T // ts),
        in_specs=[pl.BlockSpec((ts, tc), lambda j, i: (i, j + off)),
                  pl.BlockSpec((HALO, tc), lambda j, i: hp(i, j, off)),
                  pl.BlockSpec((HALO, tc), lambda j, i: (hn(i), j + off)),
                  pl.BlockSpec((8, tc), lambda j, i: (0, j)), pl.BlockSpec((1, tc), lambda j, i: (0, j)),
                  pl.BlockSpec((ts, tc), lambda j, i: (i, j)), pl.BlockSpec((HALO, tc), lambda j, i: (hn(i), j))],
        out_specs=[pl.BlockSpec((ts, tc), lambda j, i: (i, j)), pl.BlockSpec((8, tc), lambda j, i: (0, j))],
        out_shape=[_sds((T, C), MXU_DTYPE), _sds((8, C), F32)],
        compiler_params=_cparams(("parallel", "arbitrary")),
    )(proj, proj, proj, _pad_rows8(w), b.reshape(1, C), dxc, dxc)


def ffn_conv_gate_fwd(up, w, b, S, *, name, ts=512, tc=1408):
    T, C2 = up.shape
    C, K = C2 // 2, FFN_CONV
    ts = _pick(S, (ts, 256, 128))
    nj = C // tc
    ns = S // ts
    w8 = _pad_rows8(w)
    b2 = b.reshape(1, C2)

    def body(g_ref, gh_ref, v_ref, vh_ref, wg_ref, wv_ref, bg_ref, bv_ref, o_ref):
        first = (pl.program_id(0) % ns) == 0
        g = _conv_pre(_cat_prev(g_ref[...], gh_ref[...], first), wg_ref[...], bg_ref[...], K)
        v = _conv_pre(_cat_prev(v_ref[...], vh_ref[...], first), wv_ref[...], bv_ref[...], K)
        o_ref[...] = (g * _sigmoid(g) * v).astype(o_ref.dtype)

    hp = _halo_prev(ts)
    return pl.pallas_call(
        body, name=name, grid=(T // ts, nj),
        in_specs=[pl.BlockSpec((ts, tc), lambda i, j: (i, j)), pl.BlockSpec((HALO, tc), lambda i, j: hp(i, j)),
                  pl.BlockSpec((ts, tc), lambda i, j: (i, j + nj)), pl.BlockSpec((HALO, tc), lambda i, j: hp(i, j, nj)),
                  pl.BlockSpec((8, tc), lambda i, j: (0, j)), pl.BlockSpec((8, tc), lambda i, j: (0, j + nj)),
                  pl.BlockSpec((1, tc), lambda i, j: (0, j)), pl.BlockSpec((1, tc), lambda i, j: (0, j + nj))],
        out_specs=pl.BlockSpec((ts, tc), lambda i, j: (i, j)), out_shape=_sds((T, C), MXU_DTYPE),
        compiler_params=_cparams(("parallel", "parallel")),
    )(up, up, up, up, w8, w8, b2, b2)


def ffn_conv_gate_bwd(up, w, b, dact, S, *, name, ts=256, tc=1408):
    T, C2 = up.shape
    C, K = C2 // 2, FFN_CONV
    ts = _pick(S, (ts, 256, 128))
    nj = C // tc
    ns = S // ts
    nblk = T // HALO
    w8 = _pad_rows8(w)
    b2 = b.reshape(1, C2)

    def body(g_ref, gp_ref, gn_ref, v_ref, vp_ref, vn_ref, wg_ref, wv_ref, bg_ref, bv_ref, d_ref, dn_ref,
             dug_ref, duv_ref, ag_ref, av_ref):
        i = pl.program_id(1)
        first = (i % ns) == 0
        last = (i % ns) == ns - 1
        gcat = jnp.concatenate([jnp.where(first, 0.0, gp_ref[...]), g_ref[...], gn_ref[...]], axis=0)
        vcat = jnp.concatenate([jnp.where(first, 0.0, vp_ref[...]), v_ref[...], vn_ref[...]], axis=0)
        wg, wv = wg_ref[...], wv_ref[...]
        g = _conv_pre(gcat, wg, bg_ref[...], K)
        v = _conv_pre(vcat, wv, bv_ref[...], K)
        d = _cat_next(d_ref[...], dn_ref[...], last)
        sg = _sigmoid(g)
        dg = d * v * (sg * (1.0 + g * (1.0 - sg)))
        dv = d * (g * sg)
        dug_ref[...] = _conv_transposed(dg, wg, K).astype(dug_ref.dtype)
        duv_ref[...] = _conv_transposed(dv, wv, K).astype(duv_ref.dtype)
        sgp, svp = _conv_stats(dg, gcat, K, ts), _conv_stats(dv, vcat, K, ts)

        @pl.when(i == 0)
        def _():
            ag_ref[...] = sgp
            av_ref[...] = svp

        @pl.when(i > 0)
        def _():
            ag_ref[...] += sgp
            av_ref[...] += svp

    hp = _halo_prev(ts)
    hn = lambda i: jnp.minimum((i + 1) * (ts // HALO), nblk - 1)
    cur = lambda off: pl.BlockSpec((ts, tc), lambda j, i: (i, j + off))
    prv = lambda off: pl.BlockSpec((HALO, tc), lambda j, i: hp(i, j, off))
    nxt = lambda off: pl.BlockSpec((HALO, tc), lambda j, i: (hn(i), j + off))
    row = lambda r, off: pl.BlockSpec((r, tc), lambda j, i: (0, j + off))
    dug, duv, ag, av = pl.pallas_call(
        body, name=name, grid=(nj, T // ts),
        in_specs=[cur(0), prv(0), nxt(0), cur(nj), prv(nj), nxt(nj), row(8, 0), row(8, nj), row(1, 0), row(1, nj),
                  cur(0), nxt(0)],
        out_specs=[cur(0), cur(0), row(8, 0), row(8, 0)],
        out_shape=[_sds((T, C), MXU_DTYPE), _sds((T, C), MXU_DTYPE), _sds((8, C), F32), _sds((8, C), F32)],
        compiler_params=_cparams(("parallel", "arbitrary")),
    )(up, up, up, up, up, up, w8, w8, b2, b2, dact, dact)
    return dug, duv, jnp.concatenate([ag, av], axis=1)


def _pool_counts(pos, w):
    return jnp.minimum(pos + 1.0, float(w))


def pool_fwd(proj, pool_w, pool_scale, S, *, name, ts=512):
    T = proj.shape[0]
    C, G, GD, H = POOL_WIDTH, POOL_GROUPS, POOL_GROUP_DIM, POOL_HALO
    ts = _pick(S, (ts, 256, 128))
    ns = S // ts
    off = PU0 // C

    def body(u_ref, h_ref, w_ref, s_ref, y_ref, p_ref):
        i = pl.program_id(0)
        first = (i % ns) == 0
        cat = jnp.concatenate([jnp.where(first, 0.0, h_ref[...]), u_ref[...]], axis=0)
        pos = ((i % ns) * ts + lax.broadcasted_iota(jnp.int32, (ts, 1), 0)).astype(F32)
        sums = cat
        win = 1
        for g, wlen in enumerate(POOL_WINDOWS):
            while win < wlen:
                sums = sums + pltpu.roll(sums, win, axis=0)
                win *= 2
            sl = slice(g * GD, (g + 1) * GD)
            pooled = sums[H:, sl] / _pool_counts(pos, wlen) - cat[H:, sl]
            p_ref[:, sl] = pooled.astype(p_ref.dtype)
            y_ref[:, sl] = (_dot(pooled, w_ref[g]) * s_ref[:, sl]).astype(y_ref.dtype)

    return pl.pallas_call(
        body, name=name, grid=(T // ts,),
        in_specs=[pl.BlockSpec((ts, C), lambda i: (i, off)),
                  pl.BlockSpec((H, C), lambda i: (jnp.maximum(i * (ts // H) - 1, 0), off)),
                  pl.BlockSpec((G, GD, GD), lambda i: (0, 0, 0)), pl.BlockSpec((1, C), lambda i: (0, 0))],
        out_specs=[pl.BlockSpec((ts, C), lambda i: (i, 0)), pl.BlockSpec((ts, C), lambda i: (i, 0))],
        out_shape=[_sds((T, C), MXU_DTYPE), _sds((T, C), MXU_DTYPE)],
        compiler_params=_cparams(("parallel",)),
    )(proj, proj, _mx(pool_w), pool_scale.reshape(1, C))


def pool_bwd(dmix, pooled, pool_w, pool_scale, S, *, name, ts=512):
    T = dmix.shape[0]
    C, G, GD, H = POOL_WIDTH, POOL_GROUPS, POOL_GROUP_DIM, POOL_HALO
    ts = _pick(S, (ts, 256, 128))
    ns = S // ts
    off = SSD_WIDTH // C
    nblk = T // H

    def body(d_ref, dh_ref, p_ref, w_ref, s_ref, du_ref, dw_ref, ds_ref):
        i = pl.program_id(0)
        last = (i % ns) == ns - 1
        dcat = jnp.concatenate([d_ref[...], jnp.where(last, 0.0, dh_ref[...])], axis=0)
        n = ts + H
        pos = ((i % ns) * ts + lax.broadcasted_iota(jnp.int32, (n, 1), 0)).astype(F32)
        dws, dss = [], []
        for g, wlen in enumerate(POOL_WINDOWS):
            sl = slice(g * GD, (g + 1) * GD)
            wg = w_ref[g]
            pg = p_ref[:, sl]
            dys = dcat[:, sl] * s_ref[:, sl]
            dss.append(jnp.sum(dcat[:ts, sl] * _dot(pg, wg), axis=0, keepdims=True))
            dws.append(_dot_tn(pg, dys[:ts]))
            dp = _dot_nt(dys, wg)
            q = dp / _pool_counts(pos, wlen)
            win = 1
            while win < wlen:
                q = q + pltpu.roll(q, n - win, axis=0)
                win *= 2
            du_ref[:, sl] = (q[:ts] - dp[:ts]).astype(du_ref.dtype)
        dsp = jnp.concatenate(dss, axis=1)

        @pl.when(i == 0)
        def _():
            for g in range(G):
                dw_ref[g] = dws[g]
            ds_ref[...] = dsp

        @pl.when(i > 0)
        def _():
            for g in range(G):
                dw_ref[g] += dws[g]
            ds_ref[...] += dsp

    return pl.pallas_call(
        body, name=name, grid=(T // ts,),
        in_specs=[pl.BlockSpec((ts, C), lambda i: (i, off)),
                  pl.BlockSpec((H, C), lambda i: (jnp.minimum((i + 1) * (ts // H), nblk - 1), off)),
                  pl.BlockSpec((ts, C), lambda i: (i, 0)),
                  pl.BlockSpec((G, GD, GD), lambda i: (0, 0, 0)), pl.BlockSpec((1, C), lambda i: (0, 0))],
        out_specs=[pl.BlockSpec((ts, C), lambda i: (i, 0)), pl.BlockSpec((G, GD, GD), lambda i: (0, 0, 0)),
                   pl.BlockSpec((1, C), lambda i: (0, 0))],
        out_shape=[_sds((T, C), MXU_DTYPE), _sds((G, GD, GD), F32), _sds((1, C), F32)],
        compiler_params=_cparams(("arbitrary",)),
    )(dmix, dmix, pooled, _mx(pool_w), pool_scale.reshape(1, C))


ROPE0 = MLA_NOPE
ROPE_HALF = MLA_ROPE // 2


def _rope_tables(pos, invf):
    lane = lax.broadcasted_iota(jnp.int32, (1, HEAD_W), 1)
    ang = pos * invf
    cs, sn = jnp.cos(ang), jnp.sin(ang)
    in_a = (lane >= ROPE0) & (lane < ROPE0 + ROPE_HALF)
    in_b = (lane >= ROPE0 + ROPE_HALF) & (lane < ROPE0 + MLA_ROPE)
    return jnp.where(in_a | in_b, cs, 1.0), jnp.where(in_a, -sn, 0.0), jnp.where(in_b, sn, 0.0), in_a | in_b


def _rope(v, cosf, sin_a, sin_b):
    return (v * cosf + pltpu.roll(v, HEAD_W - ROPE_HALF, axis=1) * sin_a + pltpu.roll(v, ROPE_HALF, axis=1) * sin_b)


def _unrope(d, cosf, sin_a, sin_b):
    return (d * cosf + pltpu.roll(d * sin_a, ROPE_HALF, axis=1) + pltpu.roll(d * sin_b, HEAD_W - ROPE_HALF, axis=1))


def _rms_tile(xv, gamma):
    return (xv * lax.rsqrt(jnp.mean(xv * xv, axis=-1, keepdims=True) + EPS)) * gamma


def mla_prep_fwd(proj, pos, invf, q_norm, w_uq_p, kv_norm, w_ukv_p, *, name, tm=512):
    T = proj.shape[0]
    tm = _pick(T, (tm, 256, 128))
    QR, KR, P = MLA_Q_RANK, MLA_KV_RANK, MLA_PAD

    def body(cq_ref, ckv_ref, kpe_ref, pos_ref, invf_ref, qn_ref, wq_ref, kn_ref, wkv_ref,
             q_ref, k_ref, v_ref, cqn_ref, ckvn_ref):
        cosf, sin_a, sin_b, _ = _rope_tables(pos_ref[...], invf_ref[...])
        cqn = _rms_tile(cq_ref[...], qn_ref[...]).astype(MXU_DTYPE)
        ckvn = _rms_tile(ckv_ref[...], kn_ref[...]).astype(MXU_DTYPE)
        cqn_ref[...] = cqn
        ckvn_ref[...] = ckvn
        qp = _dot(cqn, wq_ref[...])
        kvp = _dot(ckvn, wkv_ref[...])
        kpe = _rope(kpe_ref[...], cosf, sin_a, sin_b)
        for h in range(MLA_HEADS):
            sl = slice(h * HEAD_W, (h + 1) * HEAD_W)
            q_ref[:, sl] = (_rope(qp[:, sl], cosf, sin_a, sin_b) * ATTN_SCALE).astype(q_ref.dtype)
            k_ref[:, sl] = (kvp[:, sl] + kpe).astype(k_ref.dtype)
            v_ref[:, sl] = kvp[:, P + h * HEAD_W:P + (h + 1) * HEAD_W].astype(v_ref.dtype)

    row = lambda w: pl.BlockSpec((tm, w), lambda i: (i, 0))
    full = lambda a, b: pl.BlockSpec((a, b), lambda i: (0, 0))
    return pl.pallas_call(
        body, name=name, grid=(T // tm,),
        in_specs=[pl.BlockSpec((tm, QR), lambda i: (i, PCQ0 // QR)), pl.BlockSpec((tm, KR), lambda i: (i, PCKV0 // KR)),
                  pl.BlockSpec((tm, LANE), lambda i: (i, PKPE0 // LANE)), row(1), full(1, LANE),
                  full(1, QR), full(QR, P), full(1, KR), full(KR, 2 * P)],
        out_specs=[row(P), row(P), row(P), row(QR), row(KR)],
        out_shape=[_sds((T, P), MXU_DTYPE)] * 3 + [_sds((T, QR), MXU_DTYPE), _sds((T, KR), MXU_DTYPE)],
        compiler_params=_cparams(("parallel",)),
    )(proj, proj, proj, pos, invf, q_norm.reshape(1, QR), w_uq_p, kv_norm.reshape(1, KR), w_ukv_p)


def mla_prep_bwd(proj, pos, invf, q_norm, w_uq_p, kv_norm, w_ukv_p, dq, dk, dv, *, name, tm=512):
    T = proj.shape[0]
    tm = _pick(T, (tm, 256, 128))
    QR, KR, P = MLA_Q_RANK, MLA_KV_RANK, MLA_PAD

    def body(cq_ref, ckv_ref, pos_ref, invf_ref, qn_ref, wq_ref, kn_ref, wkv_ref, dq_ref, dk_ref, dv_ref,
             dqp_ref, dkvp_ref, dcq_ref, dckv_ref, dkpe_ref, dqn_ref, dkn_ref):
        cosf, sin_a, sin_b, rot = _rope_tables(pos_ref[...], invf_ref[...])
        dkpe = jnp.zeros((tm, HEAD_W), F32)
        for h in range(MLA_HEADS):
            sl = slice(h * HEAD_W, (h + 1) * HEAD_W)
            dqp_ref[:, sl] = _unrope(dq_ref[:, sl] * ATTN_SCALE, cosf, sin_a, sin_b).astype(dqp_ref.dtype)
            dkh = dk_ref[:, sl]
            dkpe = dkpe + dkh
            dkvp_ref[:, sl] = dkh.astype(dkvp_ref.dtype)
            dkvp_ref[:, P + h * HEAD_W:P + (h + 1) * HEAD_W] = dv_ref[:, sl].astype(dkvp_ref.dtype)
        dkpe_ref[...] = jnp.where(rot, _unrope(dkpe, cosf, sin_a, sin_b), 0.0).astype(dkpe_ref.dtype)
        dcq, dqn = _rms_bwd_tile(cq_ref[...], qn_ref[...], _dot_nt(dqp_ref[...], wq_ref[...]))
        dckv, dkn = _rms_bwd_tile(ckv_ref[...], kn_ref[...], _dot_nt(dkvp_ref[...], wkv_ref[...]))
        dcq_ref[...] = dcq.astype(dcq_ref.dtype)
        dckv_ref[...] = dckv.astype(dckv_ref.dtype)

        @pl.when(pl.program_id(0) == 0)
        def _():
            dqn_ref[...] = dqn
            dkn_ref[...] = dkn

        @pl.when(pl.program_id(0) > 0)
        def _():
            dqn_ref[...] += dqn
            dkn_ref[...] += dkn

    row = lambda w: pl.BlockSpec((tm, w), lambda i: (i, 0))
    full = lambda a, b: pl.BlockSpec((a, b), lambda i: (0, 0))
    return pl.pallas_call(
        body, name=name, grid=(T // tm,),
        in_specs=[pl.BlockSpec((tm, QR), lambda i: (i, PCQ0 // QR)), pl.BlockSpec((tm, KR), lambda i: (i, PCKV0 // KR)),
                  row(1), full(1, LANE), full(1, QR), full(QR, P), full(1, KR), full(KR, 2 * P), row(P), row(P), row(P)],
        out_specs=[row(P), row(2 * P), row(QR), row(KR), row(LANE), full(1, QR), full(1, KR)],
        out_shape=[_sds((T, P), MXU_DTYPE), _sds((T, 2 * P), MXU_DTYPE), _sds((T, QR), MXU_DTYPE),
                   _sds((T, KR), MXU_DTYPE), _sds((T, LANE), MXU_DTYPE), _sds((1, QR), F32), _sds((1, KR), F32)],
        compiler_params=_cparams(("arbitrary",)),
    )(proj, proj, pos, invf, q_norm.reshape(1, QR), w_uq_p, kv_norm.reshape(1, KR), w_ukv_p, dq, dk, dv)


ATTN_SCALE = 1.0 / math.sqrt(MLA_QK)


def _causal_mask(i, j, blk):
    row = lax.broadcasted_iota(jnp.int32, (blk, blk), 0)
    col = lax.broadcasted_iota(jnp.int32, (blk, blk), 1)
    return col <= row + (i - j) * blk


def _hosting(hosted, grid, n_in, n_out, n_scratch):
    if hosted is None:
        return (lambda body: body), (), [], [], []
    hi, ho = len(hosted.inputs), len(hosted.out_shapes)

    def wrap(body):
        def full(*refs):
            ins, rest = refs[:n_in + hi], refs[n_in + hi:]
            outs, scr = rest[:n_out + ho], rest[n_out + ho:]
            parts = ins[n_in:], outs[n_out:], scr[n_scratch:]
            ids = [pl.program_id(d) for d in range(len(grid))]
            step = ids[0]
            for d in range(1, len(grid)):
                step = step * grid[d] + ids[d]
            total = math.prod(grid)

            @pl.when(step == 0)
            def _():
                hosted.start(*parts)

            body(*ins[:n_in], *outs[:n_out], *scr[:n_scratch])

            @pl.when(step == total // 2)
            def _():
                hosted.relay(*parts)

            @pl.when(step == total - 1)
            def _():
                hosted.finish(*parts)

        return full

    hbm = pl.BlockSpec(memory_space=pl.ANY)
    return wrap, tuple(hosted.inputs), [hbm] * ho, list(hosted.out_shapes), list(hosted.sems)


def flash_fwd(q, k, v, S, *, name, blk=1024, hosted=None):
    T, P = q.shape
    blk = _pick(S // 2, (blk, 256, 128))
    B, nq, H, W = T // S, S // (2 * blk), MLA_HEADS, HEAD_W
    grid = (B, H, nq)
    wrap, h_in, h_ospecs, h_oshapes, h_scratch = _hosting(hosted, grid, 3, 2, 0)

    def body(q_ref, k_ref, v_ref, o_ref, lse_ref):
        i = pl.program_id(2)
        q_up, q_lo = q_ref[:blk, :], q_ref[blk:, :]

        def online(qv, kv, vv, carry, masked):
            m_prev, l_prev, acc = carry
            s = _dot_nt(qv, kv)
            if masked:
                s = jnp.where(_causal_mask(0, 0, blk), s, -jnp.inf)
            m_new = jnp.maximum(m_prev, jnp.max(s, axis=1, keepdims=True))
            p = jnp.exp(s - m_new)
            alpha = jnp.exp(m_prev - m_new)
            return (m_new, alpha * l_prev + jnp.sum(p, axis=1, keepdims=True), alpha * acc + _dot(p, vv))

        def keys(j):
            rows = pl.ds(pl.multiple_of(j * blk, blk), blk)
            return k_ref[rows, :], v_ref[rows, :]

        def both(j, carry):
            kv, vv = keys(j)
            return online(q_up, kv, vv, carry[0], False), online(q_lo, kv, vv, carry[1], False)

        init = (jnp.full((blk, 1), -jnp.inf, F32), jnp.zeros((blk, 1), F32), jnp.zeros((blk, W), F32))
        up, lo = lax.fori_loop(0, 2 * i, both, (init, init))
        kv, vv = keys(2 * i)
        up = online(q_up, kv, vv, up, True)
        lo = online(q_lo, kv, vv, lo, False)
        kv, vv = keys(2 * i + 1)
        lo = online(q_lo, kv, vv, lo, True)
        for rows, (m, l, acc) in ((slice(0, blk), up), (slice(blk, 2 * blk), lo)):
            o_ref[rows, :] = acc / l
            lse_ref[rows, :] = jnp.broadcast_to(m + jnp.log(l), (blk, W))

    qmap = lambda b, h, i: (b * nq + i, h)
    kmap = lambda b, h, i: (b, h)
    hbm = pl.BlockSpec(memory_space=pl.ANY)
    return pl.pallas_call(
        wrap(body), name=name, grid=grid,
        in_specs=[pl.BlockSpec((2 * blk, W), qmap), pl.BlockSpec((S, W), kmap), pl.BlockSpec((S, W), kmap)] + [hbm] * len(h_in),
        out_specs=[pl.BlockSpec((2 * blk, W), qmap), pl.BlockSpec((2 * blk, W), qmap)] + h_ospecs,
        out_shape=[_sds((T, P), F32), _sds((T, P), F32)] + h_oshapes,
        scratch_shapes=h_scratch,
        compiler_params=_cparams(("arbitrary",) * 3 if hosted else ("parallel", "parallel", "arbitrary")),
    )(q, k, v, *h_in)


def flash_bwd(q, k, v, o, lse, dmix, S, *, name, blk=1024, hosted=None):
    T, P = q.shape
    blk = _pick(S, (blk, 256, 128))
    B, nq, H, W = T // S, S // blk, MLA_HEADS, HEAD_W
    off = (SSD_WIDTH + POOL_WIDTH) // W
    grid = (B, H, nq)
    wrap, h_in, h_ospecs, h_oshapes, h_scratch = _hosting(hosted, grid, 6, 3, 1)

    def body(q_ref, k_ref, v_ref, o_ref, lse_ref, do_ref, dq_ref, dk_ref, dv_ref, delta_s):
        j = pl.program_id(2)

        @pl.when(j == 0)
        def _():
            for i in range(nq):
                rows = slice(i * blk, (i + 1) * blk)
                delta_s[rows, :] = jnp.sum(do_ref[rows, :] * o_ref[rows, :], axis=1, keepdims=True)
                dq_ref[rows, :] = jnp.zeros((blk, W), F32)

        kv, vv = k_ref[...], v_ref[...]

        def step(i, carry, masked):
            dk, dv = carry
            rows = pl.ds(pl.multiple_of(i * blk, blk), blk)
            qv, do = q_ref[rows, :], do_ref[rows, :]
            p = jnp.exp(_dot_nt(qv, kv) - lse_ref[rows, 0:1])
            if masked:
                p = jnp.where(_causal_mask(0, 0, blk), p, 0.0)
            ds = p * (_dot_nt(do, vv) - delta_s[rows, :])
            dq_ref[rows, :] += _dot(ds, kv)
            return dk + _dot_tn(ds, qv), dv + _dot_tn(p, do)

        zero = jnp.zeros((blk, W), F32)
        carry = step(j, (zero, zero), True)
        dk, dv = lax.fori_loop(j + 1, nq, lambda i, c: step(i, c, False), carry)
        dk_ref[...] = dk
        dv_ref[...] = dv

    full = lambda b, h, j: (b, h)
    kmap = lambda b, h, j: (b * nq + j, h)
    hbm = pl.BlockSpec(memory_space=pl.ANY)
    return pl.pallas_call(
        wrap(body), name=name, grid=grid,
        in_specs=[pl.BlockSpec((S, W), full), pl.BlockSpec((blk, W), kmap), pl.BlockSpec((blk, W), kmap),
                  pl.BlockSpec((S, W), full), pl.BlockSpec((S, W), full),
                  pl.BlockSpec((S, W), lambda b, h, j: (b, off + h))] + [hbm] * len(h_in),
        out_specs=[pl.BlockSpec((S, W), full), pl.BlockSpec((blk, W), kmap), pl.BlockSpec((blk, W), kmap)] + h_ospecs,
        out_shape=[_sds((T, P), F32)] * 3 + h_oshapes,
        scratch_shapes=[pltpu.VMEM((S, 1), F32)] + h_scratch,
        compiler_params=_cparams(("arbitrary",) * 3 if hosted else ("parallel", "parallel", "arbitrary")),
    )(q, k, v, o, lse, dmix, *h_in)


SSD_PAIRS = SSD_HEADS // 2
PAIRS_PER_GROUP = SSD_PAIRS // SSD_GROUPS
GN = SSD_GROUPS * SSD_STATE


def _log1p_small(e):
    return jnp.where(e < 1e-3, e * (1.0 - e * (0.5 - e / 3.0)), jnp.log(1.0 + e))


def _softplus(v):
    return jnp.maximum(v, 0.0) + _log1p_small(jnp.exp(-jnp.abs(v)))


def _ssd_decay(dt_raw, dtb, alog):
    L = dt_raw.shape[0]
    pre = dt_raw + dtb
    dt = _softplus(pre)
    a = -jnp.exp(alog)
    row = lax.broadcasted_iota(jnp.int32, (L, L), 0)
    col = lax.broadcasted_iota(jnp.int32, (L, L), 1)
    tri = row >= col
    cum = _dot_hi(tri.astype(F32), dt * a)
    return pre, dt, a, tri, cum, cum.T


def _col(m, h):
    return m[:, h:h + 1]


def _pair_sel(m, k, lo):
    return jnp.where(lo, _col(m, 2 * k), _col(m, 2 * k + 1))


def _ssd_specs(S):
    L = SSD_CHUNK
    nc = S // L
    return L, nc


def ssd_fwd(proj, xc, dtb, alog, dchan, normw, S, *, name, hosted=None):
    T = proj.shape[0]
    L, nc = _ssd_specs(S)
    B, W, N = T // S, SSD_WIDTH, SSD_STATE
    wrap, h_in, h_ospecs, h_oshapes, h_scratch = _hosting(hosted, (B, nc), 9, 3, 1)

    def body(xs_ref, bs_ref, cs_ref, dt_ref, z_ref, dtb_ref, alog_ref, dch_ref, nw_ref, y_ref, ys_ref, hin_ref, st):
        @pl.when(pl.program_id(1) == 0)
        def _():
            st[...] = jnp.zeros(st.shape, F32)

        hin_ref[...] = st[...]
        _, dt, a, tri, cum, cum_t = _ssd_decay(dt_ref[...], dtb_ref[...], alog_ref[...])
        last = cum[L - 1:L, :]
        lo = lax.broadcasted_iota(jnp.int32, (1, LANE), 1) < SSD_HEAD_DIM
        for g in range(SSD_GROUPS):
            bm = bs_ref[:, g * N:(g + 1) * N]
            cm = cs_ref[:, g * N:(g + 1) * N]
            bm_t = bm.T
            gmat = _dot_nt(cm, bm)
            for kk in range(PAIRS_PER_GROUP):
                k = g * PAIRS_PER_GROUP + kk
                sl = slice(k * LANE, (k + 1) * LANE)
                xv = xs_ref[:, sl]
                xdt = xv * _pair_sel(dt, k, lo)
                cum_cols = [jnp.broadcast_to(_col(cum, h), (L, LANE)) for h in (2 * k, 2 * k + 1)]
                cum_sel = jnp.where(lo, cum_cols[0], cum_cols[1])
                last_sel = _pair_sel(last, k, lo)
                yd = []
                for j, h in enumerate((2 * k, 2 * k + 1)):
                    gam = jnp.exp(jnp.where(tri, cum_cols[j] - cum_t[h:h + 1, :], -jnp.inf))
                    yd.append(_dot(gmat * gam, xdt))
                hp = st[:, sl]
                y_off = _dot(cm, hp) * jnp.exp(cum_sel)
                y_ref[:, sl] = jnp.where(lo, yd[0], yd[1]) + y_off + xv * dch_ref[:, sl]
                zmat = xdt * jnp.exp(last_sel - cum_sel)
                st[:, sl] = hp * jnp.exp(last_sel) + _dot(bm_t, zmat)
        y = y_ref[...]
        z = z_ref[...]
        yz = y * (z * _sigmoid(z))
        ys_ref[...] = _rms_tile(yz, nw_ref[...]).astype(ys_ref.dtype)

    r = lambda b, c: b * nc + c
    vec = lambda w: pl.BlockSpec((1, w), lambda b, c: (0, 0))
    hbm = pl.BlockSpec(memory_space=pl.ANY)
    return pl.pallas_call(
        wrap(body), name=name, grid=(B, nc),
        in_specs=[pl.BlockSpec((L, W), lambda b, c: (r(b, c), 0)),
                  pl.BlockSpec((L, GN), lambda b, c: (r(b, c), W // GN)),
                  pl.BlockSpec((L, GN), lambda b, c: (r(b, c), W // GN + 1)),
                  pl.BlockSpec((L, LANE), lambda b, c: (r(b, c), PDT0 // LANE)),
                  pl.BlockSpec((L, W), lambda b, c: (r(b, c), PZ0 // W)),
                  vec(LANE), vec(LANE), vec(W), vec(W)] + [hbm] * len(h_in),
        out_specs=[pl.BlockSpec((L, W), lambda b, c: (r(b, c), 0)), pl.BlockSpec((L, W), lambda b, c: (r(b, c), 0)),
                   pl.BlockSpec((N, W), lambda b, c: (r(b, c), 0))] + h_ospecs,
        out_shape=[_sds((T, W), F32), _sds((T, W), MXU_DTYPE), _sds((T // L * N, W), F32)] + h_oshapes,
        scratch_shapes=[pltpu.VMEM((N, W), F32)] + h_scratch,
        compiler_params=_cparams(("arbitrary", "arbitrary") if hosted else ("parallel", "arbitrary")),
    )(xc, xc, xc, proj, proj, dtb, alog, dchan, normw, *h_in)


def ssd_bwd(proj, xc, ypre, hin, dmix, dtb, alog, dchan, normw, S, *, name):
    T = proj.shape[0]
    L, nc = _ssd_specs(S)
    B, W, N = T // S, SSD_WIDTH, SSD_STATE

    def body(xs_ref, bs_ref, cs_ref, dt_ref, z_ref, y_ref, hin_ref, dys_ref, dtb_ref, alog_ref, dch_ref, nw_ref,
             dxc_ref, ddt_ref, dz_ref, sm_ref, dnw_ref, dst):
        step = pl.program_id(0) * nc + pl.program_id(1)

        @pl.when(pl.program_id(1) == 0)
        def _():
            dst[...] = jnp.zeros(dst.shape, F32)

        pre, dt, a, tri, cum, cum_t = _ssd_decay(dt_ref[...], dtb_ref[...], alog_ref[...])
        last = cum[L - 1:L, :]
        e_last = jnp.exp(last)
        lane = lax.broadcasted_iota(jnp.int32, (1, LANE), 1)
        sub = lax.broadcasted_iota(jnp.int32, (LANE, 1), 0)
        lo = lane < SSD_HEAD_DIM
        is_last_row = sub == L - 1
        tri_t = (lax.broadcasted_iota(jnp.int32, (L, L), 0) <= lax.broadcasted_iota(jnp.int32, (L, L), 1))

        y, z, nw = y_ref[...], z_ref[...], nw_ref[...]
        sg = _sigmoid(z)
        gate = z * sg
        dyz, dnw = _rms_bwd_tile(y * gate, nw, dys_ref[...])
        dy_all = dyz * gate
        dz_ref[...] = (dyz * y * (sg * (1.0 + z * (1.0 - sg)))).astype(dz_ref.dtype)

        d_cum = jnp.zeros((L, LANE), F32)
        d_cum_t = jnp.zeros((LANE, L), F32)
        d_dt = jnp.zeros((L, LANE), F32)
        d_dskip = jnp.zeros((1, LANE), F32)
        for g in range(SSD_GROUPS):
            bm = bs_ref[:, g * N:(g + 1) * N]
            cm = cs_ref[:, g * N:(g + 1) * N]
            cm_t = cm.T
            gmat = _dot_nt(cm, bm)
            gmat_t = _dot_nt(bm, cm)
            d_g = jnp.zeros((L, L), F32)
            d_bm = jnp.zeros((L, N), F32)
            d_cm = jnp.zeros((L, N), F32)
            for kk in range(PAIRS_PER_GROUP):
                k = g * PAIRS_PER_GROUP + kk
                sl = slice(k * LANE, (k + 1) * LANE)
                xv = xs_ref[:, sl]
                dyv = dy_all[:, sl]
                dt_sel = _pair_sel(dt, k, lo)
                xdt = xv * dt_sel
                hp = hin_ref[:, sl]
                dh_out = dst[:, sl]
                cum_cols = [jnp.broadcast_to(_col(cum, h), (L, LANE)) for h in (2 * k, 2 * k + 1)]
                cum_sel = jnp.where(lo, cum_cols[0], cum_cols[1])
                last_sel = _pair_sel(last, k, lo)
                e_sel = jnp.exp(cum_sel)
                w_sel = jnp.exp(last_sel - cum_sel)
                e_lane = jnp.exp(last_sel)
                y_off = _dot(cm, hp) * e_sel
                zmat = xdt * w_sel
                d_z = _dot(bm, dh_out)
                d_bm = d_bm + _dot_nt(zmat, dh_out)
                d_xdt = d_z * w_sel
                dw_full = d_z * zmat
                hh = dh_out * hp
                d_r = dyv * e_sel
                d_cm = d_cm + _dot_nt(d_r, hp)
                dst[:, sl] = dh_out * e_lane + _dot(cm_t, d_r)
                dyoff_full = dyv * y_off
                for j, h in enumerate((2 * k, 2 * k + 1)):
                    mine = lo if j == 0 else jnp.logical_not(lo)
                    hot = lane == h
                    dyh = jnp.where(mine, dyv, 0.0)
                    gam = jnp.exp(jnp.where(tri, cum_cols[j] - cum_t[h:h + 1, :], -jnp.inf))
                    gam_t = jnp.exp(jnp.where(tri_t, cum_t[h:h + 1, :] - cum_cols[j], -jnp.inf))
                    mx = gmat * gam
                    d_xdt = d_xdt + _dot(gmat_t * gam_t, dyh)
                    d_mx = jnp.where(tri, _dot_nt(dyh, xdt), 0.0)
                    d_g = d_g + d_mx * gam
                    d_seg = d_mx * mx
                    row_l = jnp.sum(d_seg + jnp.where(mine, dyoff_full - dw_full, 0.0), axis=1, keepdims=True)
                    at_end = (jnp.sum(jnp.where(mine, dw_full, 0.0), keepdims=True)
                              + jnp.sum(jnp.where(mine, hh, 0.0), keepdims=True) * _col(e_last, h))
                    d_cum = d_cum + jnp.where(hot, row_l + jnp.where(is_last_row, at_end, 0.0), 0.0)
                    d_cum_t = d_cum_t - jnp.where(sub == h, jnp.sum(d_seg, axis=0, keepdims=True), 0.0)
                    d_dskip = d_dskip + jnp.where(hot, jnp.sum(jnp.where(mine, dyv * xv, 0.0), keepdims=True), 0.0)
                for j, h in enumerate((2 * k, 2 * k + 1)):
                    mine = lo if j == 0 else jnp.logical_not(lo)
                    d_dt = d_dt + jnp.where(lane == h, jnp.sum(jnp.where(mine, d_xdt * xv, 0.0), axis=1, keepdims=True), 0.0)
                dxc_ref[:, sl] = d_xdt * dt_sel + dyv * dch_ref[:, sl]
            dxc_ref[:, W + g * N:W + (g + 1) * N] = d_bm + _dot_tn(d_g, cm)
            dxc_ref[:, W + GN + g * N:W + GN + (g + 1) * N] = d_cm + _dot(d_g, bm)

        d_cum = d_cum + d_cum_t.T
        d_da = _dot_hi(jnp.logical_not(tri).astype(F32) + (lax.broadcasted_iota(jnp.int32, (L, L), 0)
                                                              == lax.broadcasted_iota(jnp.int32, (L, L), 1)).astype(F32), d_cum)
        d_dt = d_dt + d_da * a
        heads = lane < SSD_HEADS
        d_pre = jnp.where(heads, d_dt * _sigmoid(pre), 0.0)
        ddt_ref[...] = d_pre.astype(ddt_ref.dtype)
        d_alog = jnp.sum(d_da * dt, axis=0, keepdims=True) * a
        part = jnp.concatenate([jnp.where(heads, d_alog, 0.0), jnp.sum(d_pre, axis=0, keepdims=True), d_dskip,
                                jnp.zeros((5, LANE), F32)], axis=0)

        @pl.when(step == 0)
        def _():
            sm_ref[...] = part
            dnw_ref[...] = dnw

        @pl.when(step > 0)
        def _():
            sm_ref[...] += part
            dnw_ref[...] += dnw

    r = lambda b, c: b * nc + (nc - 1 - c)
    vec = lambda w: pl.BlockSpec((1, w), lambda b, c: (0, 0))
    blk = lambda w, j: pl.BlockSpec((L, w), lambda b, c: (r(b, c), j))
    return pl.pallas_call(
        body, name=name, grid=(B, nc),
        in_specs=[blk(W, 0), blk(GN, W // GN), blk(GN, W // GN + 1), blk(LANE, PDT0 // LANE), blk(W, PZ0 // W),
                  blk(W, 0), pl.BlockSpec((N, W), lambda b, c: (r(b, c), 0)), blk(W, 0),
                  vec(LANE), vec(LANE), vec(W), vec(W)],
        out_specs=[blk(SSD_CONV_CH, 0), blk(LANE, 0), blk(W, 0), pl.BlockSpec((8, LANE), lambda b, c: (0, 0)), vec(W)],
        out_shape=[_sds((T, SSD_CONV_CH), F32), _sds((T, LANE), MXU_DTYPE), _sds((T, W), MXU_DTYPE),
                   _sds((8, LANE), F32), _sds((1, W), F32)],
        scratch_shapes=[pltpu.VMEM((N, W), F32)],
        compiler_params=_cparams(("arbitrary", "arbitrary")),
    )(xc, xc, xc, proj, proj, ypre, hin, dmix, dtb, alog, dchan, normw)


def _adamw_math(w, g, m, v):
    m = ADAM_B1 * m + (1.0 - ADAM_B1) * g
    v = ADAM_B2 * v + (1.0 - ADAM_B2) * (g * g)
    m_hat = m / (1.0 - ADAM_B1 ** ADAM_STEP)
    v_hat = v / (1.0 - ADAM_B2 ** ADAM_STEP)
    delta = -ADAM_LR * (m_hat / (jnp.sqrt(v_hat) + ADAM_EPS) + ADAM_WD * w)
    return delta, m, v


def adamw_layers(w, g_layers, m, v, *, name, tr=256):
    L, A, B = w.shape
    tr = _pick(A, (tr, 192, 176, 128, 64, 32, 16, 8))
    na = A // tr
    n = len(g_layers[0])

    def body(*refs):
        w_ref, m_ref, v_ref = refs[0], refs[1 + L * n], refs[2 + L * n]
        g_ref, d_ref, nm_ref, nv_ref = refs[3 + L * n:]
        layer = pl.program_id(0)
        g = None
        for l in range(L):
            parts = refs[1 + l * n:1 + (l + 1) * n]
            gl = parts[0][...]
            for p in parts[1:]:
                gl = gl + p[...]
            g = gl if g is None else jnp.where(layer == l, gl, g)
        g_ref[...] = g
        d_ref[...], nm_ref[...], nv_ref[...] = _adamw_math(w_ref[...], g, m_ref[...], v_ref[...])

    def g_spec(l):
        return pl.BlockSpec((tr, B), lambda layer, i: (jnp.where(layer == l, i, jnp.where(layer < l, 0, na - 1)), 0))

    spec = pl.BlockSpec((None, tr, B), lambda layer, i: (layer, i, 0))
    return pl.pallas_call(
        body, name=name, grid=(L, na), in_specs=[spec] + [g_spec(l) for l in range(L) for _ in range(n)] + [spec] * 2,
        out_specs=[spec] * 4, out_shape=[_sds((L, A, B), F32)] * 4, compiler_params=_cparams(("arbitrary", "arbitrary")),
    )(w, *[p for parts in g_layers for p in parts], m, v)


def adamw_small(ws, gs, ms, vs, *, name):
    n = len(ws)

    def body(*refs):
        w_refs, g_refs, m_refs, v_refs = (refs[i * n:(i + 1) * n] for i in range(4))
        d_refs, nm_refs, nv_refs = (refs[(4 + i) * n:(5 + i) * n] for i in range(3))
        for a in range(n):
            d_refs[a][...], nm_refs[a][...], nv_refs[a][...] = _adamw_math(
                w_refs[a][...], g_refs[a][...], m_refs[a][...], v_refs[a][...])

    vm = pl.BlockSpec(memory_space=pltpu.VMEM)
    out = pl.pallas_call(
        body, name=name, in_specs=[vm] * (4 * n), out_specs=[vm] * (3 * n),
        out_shape=[_sds(w.shape, F32) for w in ws] * 3, compiler_params=pltpu.CompilerParams(vmem_limit_bytes=VMEM_LIMIT),
    )(*ws, *gs, *ms, *vs)
    return out[:n], out[n:2 * n], out[2 * n:]


def _my_place():
    return lax.axis_index("x"), lax.axis_index("y"), lax.axis_index("c")


def _other_chips(x, y):
    return [(1 - x, y), (x, 1 - y), (1 - x, 1 - y)]


def relation_of(chip, me):
    d = chip ^ me
    return jnp.where(d == 2, 0, jnp.where(d == 1, 1, jnp.where(d == 3, 2, -1)))


class Exchange(NamedTuple):
    inputs: tuple
    out_shapes: tuple
    sems: tuple
    start: Callable
    relay: Callable
    finish: Callable


def scatter_exchange(srcs):
    n = len(srcs)

    def copies(ins, outs, sems):
        x, y, c = _my_place()
        out = []
        for k, (px, py) in enumerate(_other_chips(x, y)):
            for a in range(n):
                out.append(pltpu.make_async_remote_copy(
                    src_ref=ins[a].at[2 * px + py], dst_ref=outs[a].at[k], send_sem=sems[0].at[k, a],
                    recv_sem=sems[1].at[k, a], device_id=(px, py, c), device_id_type=pl.DeviceIdType.MESH))
        return out

    def start(ins, outs, sems):
        for cp in copies(ins, outs, sems):
            cp.start()

    def finish(ins, outs, sems):
        cps = copies(ins, outs, sems)
        for cp in cps:
            cp.wait_recv()
        for cp in cps:
            cp.wait_send()

    return Exchange(tuple(srcs), tuple(_sds((3,) + s.shape[1:], s.dtype) for s in srcs),
                    (pltpu.SemaphoreType.DMA((3, n)),) * 2, start, lambda *a: None, finish)


def run_exchange(ex, *, name):
    n_in, n_out = len(ex.inputs), len(ex.out_shapes)

    def body(*refs):
        parts = refs[:n_in], refs[n_in:n_in + n_out], refs[n_in + n_out:]
        ex.start(*parts)
        ex.relay(*parts)
        ex.finish(*parts)

    hbm = pl.BlockSpec(memory_space=pl.ANY)
    return pl.pallas_call(
        body, name=name, in_specs=[hbm] * n_in, out_specs=[hbm] * n_out, out_shape=list(ex.out_shapes),
        scratch_shapes=list(ex.sems), compiler_params=pltpu.CompilerParams(has_side_effects=True),
    )(*ex.inputs)


def sibling_swap(srcs, *, name):
    n = len(srcs)

    def body(*refs):
        src_refs, out_refs, (send_sems, recv_sems) = refs[:n], refs[n:2 * n], refs[2 * n:]
        x, y, c = _my_place()
        copies = [pltpu.make_async_remote_copy(
            src_ref=src_refs[a], dst_ref=out_refs[a], send_sem=send_sems.at[a], recv_sem=recv_sems.at[a],
            device_id=(x, y, 1 - c), device_id_type=pl.DeviceIdType.MESH) for a in range(n)]
        for cp in copies:
            cp.start()
        for cp in copies:
            cp.wait_recv()
        for cp in copies:
            cp.wait_send()

    hbm = pl.BlockSpec(memory_space=pl.ANY)
    return pl.pallas_call(
        body, name=name, in_specs=[hbm] * n, out_specs=[hbm] * n, out_shape=[_sds(s.shape, s.dtype) for s in srcs],
        scratch_shapes=[pltpu.SemaphoreType.DMA((n,)), pltpu.SemaphoreType.DMA((n,))],
        compiler_params=pltpu.CompilerParams(has_side_effects=True),
    )(*srcs)


def gather_exchange(srcs):
    nch = len(srcs)
    halves = [s.shape[0] // 2 for s in srcs]
    assert all(2 * h == s.shape[0] and h % 16 == 0 for h, s in zip(halves, srcs))
    pieces = [(k, q) for k in range(3) for q in range(nch)]

    def makers(ins, outs, sems):
        ici_send, ici_recv, d2d_send, d2d_recv = sems
        x, y, c = _my_place()
        peers = _other_chips(x, y)

        def rows(core, q):
            return pl.ds(core * halves[q], halves[q])

        def ici(k, q):
            px, py = peers[k]
            return pltpu.make_async_remote_copy(
                src_ref=ins[q].at[rows(c, q)], dst_ref=outs[q].at[k, rows(c, q)], send_sem=ici_send.at[k, q],
                recv_sem=ici_recv.at[k, q], device_id=(px, py, c), device_id_type=pl.DeviceIdType.MESH)

        def d2d(k, q, core):
            return pltpu.make_async_remote_copy(
                src_ref=outs[q].at[k, rows(core, q)], dst_ref=outs[q].at[k, rows(core, q)],
                send_sem=d2d_send.at[k, q], recv_sem=d2d_recv.at[k, q], device_id=(x, y, 1 - c),
                device_id_type=pl.DeviceIdType.MESH)

        return ici, d2d, c

    def start(*refs):
        ici, _, _ = makers(*refs)
        for k, q in pieces:
            ici(k, q).start()

    def relay(*refs):
        ici, d2d, c = makers(*refs)
        for k, q in pieces:
            ici(k, q).wait_recv()
            d2d(k, q, c).start()

    def finish(*refs):
        ici, d2d, c = makers(*refs)
        for k, q in pieces:
            d2d(k, q, 1 - c).wait_recv()
        for k, q in pieces:
            ici(k, q).wait_send()
            d2d(k, q, c).wait_send()

    return Exchange(tuple(srcs), tuple(_sds((3,) + s.shape, s.dtype) for s in srcs),
                    (pltpu.SemaphoreType.DMA((3, nch)),) * 4, start, relay, finish)


def all_sum_small(vec, *, name):
    R, C = vec.shape

    def body(v_ref, out_ref, buf, send_sems, recv_sems):
        x, y, c = _my_place()
        me = 4 * x + 2 * y + c
        buf[me] = v_ref[...]
        copies = []
        for k in range(1, N_DEV):
            px, py, pc = x ^ (k >> 2), y ^ ((k >> 1) & 1), c ^ (k & 1)
            copies.append(pltpu.make_async_remote_copy(
                src_ref=v_ref, dst_ref=buf.at[me], send_sem=send_sems.at[k - 1], recv_sem=recv_sems.at[k - 1],
                device_id=(px, py, pc), device_id_type=pl.DeviceIdType.MESH))
        for cp in copies:
            cp.start()
        for k in range(1, N_DEV):
            px, py, pc = x ^ (k >> 2), y ^ ((k >> 1) & 1), c ^ (k & 1)
            pltpu.make_async_remote_copy(
                src_ref=v_ref, dst_ref=buf.at[4 * px + 2 * py + pc], send_sem=send_sems.at[k - 1],
                recv_sem=recv_sems.at[k - 1], device_id=(px, py, pc), device_id_type=pl.DeviceIdType.MESH).wait_recv()
        for cp in copies:
            cp.wait_send()
        acc = buf[0]
        for d in range(1, N_DEV):
            acc = acc + buf[d]
        out_ref[...] = acc

    return pl.pallas_call(
        body, name=name, in_specs=[pl.BlockSpec(memory_space=pltpu.VMEM)], out_specs=pl.BlockSpec(memory_space=pltpu.VMEM),
        out_shape=_sds((R, C), F32),
        scratch_shapes=[pltpu.VMEM((N_DEV, R, C), F32), pltpu.SemaphoreType.DMA((N_DEV - 1,)),
                        pltpu.SemaphoreType.DMA((N_DEV - 1,))],
        compiler_params=pltpu.CompilerParams(has_side_effects=True, vmem_limit_bytes=VMEM_LIMIT),
    )(vec)


def sum_chips(own, others, *, name, tr=512):
    R, C = own.shape
    tr = _pick(R, (tr, 384, 352, 256, 128, 64, 32, 16))

    def body(o_ref, p_ref, s_ref):
        acc = o_ref[...].astype(F32)
        for k in range(3):
            acc = acc + p_ref[k].astype(F32)
        s_ref[...] = acc

    return pl.pallas_call(
        body, name=name, grid=(R // tr,),
        in_specs=[pl.BlockSpec((tr, C), lambda i: (i, 0)), pl.BlockSpec((3, tr, C), lambda i: (0, i, 0))],
        out_specs=pl.BlockSpec((tr, C), lambda i: (i, 0)), out_shape=_sds((R, C), F32),
        compiler_params=_cparams(("parallel",)),
    )(own, others)


WEIGHTS = ['attn_norm', 'w_in', 'ssd_conv_w', 'ssd_conv_b', 'ssd_dt_bias', 'ssd_a_log', 'ssd_d', 'ssd_norm', 'pool_w',
           'pool_scale', 'mla_q_norm', 'mla_w_uq', 'mla_kv_norm', 'mla_w_ukv', 'w_out', 'ffn_norm', 'ffn_w_up',
           'ffn_conv_w', 'ffn_conv_b', 'ffn_w_down', 'final_norm']
BIG = {'w_in': 2, 'mla_w_uq': 2, 'mla_w_ukv': 2, 'w_out': 1, 'ffn_w_up': 2, 'ffn_w_down': 1}
CONV_SHARDED = ('ssd_conv_w', 'ffn_conv_w')


def _zeros_cols(w, n):
    return jnp.zeros((w.shape[0], n), w.dtype)


def _w_in_to_padded(w):
    return jnp.concatenate([w[:, 0:2560], w[:, 2576:3088], w[:, 3088:3472], w[:, 2560:2576], _zeros_cols(w, 112),
                            w[:, 3472:3728], _zeros_cols(w, 64), w[:, 3728:3760], _zeros_cols(w, 32 + 128)], axis=1)


def _w_in_from_padded(g):
    return jnp.concatenate([g[:, 0:2560], g[:, PDT0:PDT0 + SSD_HEADS], g[:, PU0:PU0 + POOL_WIDTH],
                            g[:, PCQ0:PCQ0 + MLA_Q_RANK], g[:, PCKV0:PCKV0 + MLA_KV_RANK],
                            g[:, PKPE0 + ROPE0:PKPE0 + ROPE0 + MLA_ROPE]], axis=1)


def _w_uq_to_padded(w):
    r = w.reshape(MLA_Q_RANK, MLA_HEADS, MLA_QK)
    return jnp.pad(r, ((0, 0), (0, 0), (0, HEAD_W - MLA_QK))).reshape(MLA_Q_RANK, MLA_PAD)


def _w_uq_from_padded(g):
    return g.reshape(MLA_Q_RANK, MLA_HEADS, HEAD_W)[:, :, :MLA_QK].reshape(MLA_Q_RANK, MLA_HEADS * MLA_QK)


def _w_ukv_to_padded(w):
    r = w.reshape(MLA_KV_RANK, MLA_HEADS, MLA_NOPE + MLA_V)
    pad = lambda t: jnp.pad(t, ((0, 0), (0, 0), (0, HEAD_W - t.shape[2]))).reshape(MLA_KV_RANK, MLA_PAD)
    return jnp.concatenate([pad(r[:, :, :MLA_NOPE]), pad(r[:, :, MLA_NOPE:])], axis=1)


def _w_ukv_from_padded(g):
    kk = g[:, :MLA_PAD].reshape(MLA_KV_RANK, MLA_HEADS, HEAD_W)[:, :, :MLA_NOPE]
    vv = g[:, MLA_PAD:].reshape(MLA_KV_RANK, MLA_HEADS, HEAD_W)[:, :, :MLA_V]
    return jnp.concatenate([kk, vv], axis=2).reshape(MLA_KV_RANK, MLA_HEADS * (MLA_NOPE + MLA_V))


def _w_out_to_padded(w):
    att = w[SSD_WIDTH + POOL_WIDTH:].reshape(MLA_HEADS, MLA_V, D_MODEL)
    att = jnp.pad(att, ((0, 0), (0, HEAD_W - MLA_V), (0, 0))).reshape(MLA_PAD, D_MODEL)
    return jnp.concatenate([w[:SSD_WIDTH + POOL_WIDTH], att], axis=0)


def _w_out_from_padded(g):
    att = g[SSD_WIDTH + POOL_WIDTH:].reshape(MLA_HEADS, HEAD_W, D_MODEL)[:, :MLA_V].reshape(MLA_WIDTH, D_MODEL)
    return jnp.concatenate([g[:SSD_WIDTH + POOL_WIDTH], att], axis=0)


def _pad_lanes(v, n=LANE):
    return jnp.pad(v.reshape(1, -1), ((0, 0), (0, n - v.size)))


def _pack_rows(parts, cols, dtype, row_multiple=16):
    flat = jnp.concatenate([p.astype(dtype).reshape(-1) for p in parts])
    rows = -(-flat.size // (cols * row_multiple)) * row_multiple
    return jnp.pad(flat, (0, rows * cols - flat.size)).reshape(rows, cols)


def _unpack_rows(packed, shapes):
    flat = packed.reshape(-1)
    out, at = [], 0
    for s in shapes:
        n = math.prod(s)
        out.append(flat[at:at + n].reshape(s))
        at += n
    return out


def _split_for_chips(g, axis):
    a, b = g.shape
    if axis == 0:
        return g.reshape(N_CHIPS, a // N_CHIPS, b)
    return g.reshape(a, N_CHIPS, b // N_CHIPS).transpose(1, 0, 2)


_MATMUL_OPERANDS = {'w_in': ('w_in_p', _w_in_to_padded), 'mla_w_uq': ('w_uq_p', _w_uq_to_padded),
                    'mla_w_ukv': ('w_ukv_p', _w_ukv_to_padded), 'w_out': ('w_out_p', _w_out_to_padded),
                    'ffn_w_up': ('w_up', lambda a: a), 'ffn_w_down': ('w_down', lambda a: a)}


def _matmul_weights(full):
    return {_MATMUL_OPERANDS[k][0]: _MATMUL_OPERANDS[k][1](a) for k, a in full.items()}


def _layer_weights(full, small, l):
    w = _matmul_weights(full)
    for k in ('attn_norm', 'ssd_conv_w', 'ssd_conv_b', 'ssd_norm', 'pool_w', 'pool_scale', 'mla_q_norm', 'mla_kv_norm',
              'ffn_norm', 'ffn_conv_w', 'ffn_conv_b'):
        w[k] = small[k][l]
    w['dtb'] = _pad_lanes(small['ssd_dt_bias'][l])
    w['alog'] = _pad_lanes(small['ssd_a_log'][l])
    w['dchan'] = jnp.repeat(small['ssd_d'][l], SSD_HEAD_DIM).reshape(1, SSD_WIDTH)
    w['ssd_norm'] = w['ssd_norm'].reshape(1, SSD_WIDTH)
    return w


def _layer_fwd(x, pos, invf, w, S, l, hosted=None, scan_hosted=None, late_weights=None, post_weights=None):
    n = lambda s: f"{s}_l{l}"
    h1 = rmsnorm_fwd(x, w['attn_norm'], name=n("attn_norm"))
    proj = matmul(h1, w['w_in_p'], name=n("w_in"))
    xc = ssd_conv_fwd(proj, w['ssd_conv_w'], w['ssd_conv_b'], S, name=n("ssd_conv"))
    ypre, yssd, hin, *arrived = ssd_fwd(proj, xc, w['dtb'], w['alog'], w['dchan'], w['ssd_norm'], S, name=n("ssd_scan"),
                                        hosted=scan_hosted)
    if late_weights:
        w = {**w, **late_weights(arrived)}
    ypool, pooled = pool_fwd(proj, w['pool_w'], w['pool_scale'], S, name=n("pool"))
    q, k, v, cqn, ckvn = mla_prep_fwd(proj, pos, invf, w['mla_q_norm'], w['w_uq_p'], w['mla_kv_norm'], w['w_ukv_p'],
                                      name=n("mla_prep"))
    o, lse, *exchanged = flash_fwd(q, k, v, S, name=n("attention"), hosted=hosted)
    if post_weights:
        w = {**w, **post_weights(exchanged)}
    mix = jnp.concatenate([yssd, ypool, o.astype(MXU_DTYPE)], axis=1)
    x2 = matmul(mix, w['w_out_p'], res=x, name=n("w_out"))
    h2 = rmsnorm_fwd(x2, w['ffn_norm'], name=n("ffn_norm"))
    up = matmul(h2, w['w_up'], name=n("ffn_up"))
    act = ffn_conv_gate_fwd(up, w['ffn_conv_w'], w['ffn_conv_b'], S, name=n("ffn_conv_gate"))
    x3 = matmul(act, w['w_down'], res=x2, name=n("ffn_down"))
    saved = dict(x=x, h1=h1, proj=proj, xc=xc, ypre=ypre, hin=hin, pooled=pooled, q=q, k=k, v=v, cqn=cqn, ckvn=ckvn,
                 o=o, lse=lse, mix=mix, x2=x2, h2=h2, up=up, act=act)
    return x3, saved, w, exchanged


def _layer_bwd(dx3, pos, invf, w, s, S, l, host=None, late_host=None):
    n = lambda t: f"{t}_l{l}"
    g = {}
    dact = matmul(dx3, w['w_down'], nt=True, name=n("d_ffn_down"))
    g['ffn_w_down'] = matmul_tn(s['act'], dx3, name=n("g_ffn_down"))
    dup_g, dup_v, st = ffn_conv_gate_bwd(s['up'], w['ffn_conv_w'], w['ffn_conv_b'], dact, S, name=n("d_ffn_conv_gate"))
    g['ffn_conv_w'], g['ffn_conv_b'] = st[:FFN_CONV], st[FFN_CONV]
    dh2 = matmul(dup_g, w['w_up'], nt=True, kblock=0, name=n("d_ffn_up_g"))
    dh2 = matmul(dup_v, w['w_up'], nt=True, kblock=1, res=dh2, name=n("d_ffn_up_v"))
    g['ffn_w_up'] = jnp.concatenate([matmul_tn(s['h2'], dup_g, name=n("g_ffn_up_g")),
                                     matmul_tn(s['h2'], dup_v, name=n("g_ffn_up_v"))], axis=1)
    dx2, gn = rmsnorm_bwd(s['x2'], w['ffn_norm'], dh2, dx3, name=n("d_ffn_norm"))
    g['ffn_norm'] = gn[0]
    dmix = matmul(dx2, w['w_out_p'], nt=True, name=n("d_w_out"))
    g['w_out'] = _w_out_from_padded(matmul_tn(s['mix'], dx2, name=n("g_w_out")))
    dxc, ddt, dz, sm, gsn = ssd_bwd(s['proj'], s['xc'], s['ypre'], s['hin'], dmix, w['dtb'], w['alog'], w['dchan'],
                                    w['ssd_norm'], S, name=n("d_ssd_scan"))
    g['ssd_a_log'], g['ssd_dt_bias'], g['ssd_d'] = sm[0, :SSD_HEADS], sm[1, :SSD_HEADS], sm[2, :SSD_HEADS]
    g['ssd_norm'] = gsn[0]
    dxbc, st = ssd_conv_bwd(s['proj'], w['ssd_conv_w'], w['ssd_conv_b'], dxc, S, name=n("d_ssd_conv"))
    g['ssd_conv_w'], g['ssd_conv_b'] = st[:SSD_CONV], st[SSD_CONV]
    du, g['pool_w'], gps = pool_bwd(dmix, s['pooled'], w['pool_w'], w['pool_scale'], S, name=n("d_pool"))
    g['pool_scale'] = gps[0]
    dq, dk, dv, *exchanged = flash_bwd(s['q'], s['k'], s['v'], s['o'], s['lse'], dmix, S, name=n("d_attention"),
                                       hosted=host(g) if host else None)
    dqp, dkvp, dcq, dckv, dkpe, gqn, gkn = mla_prep_bwd(s['proj'], pos, invf, w['mla_q_norm'], w['w_uq_p'],
                                                        w['mla_kv_norm'], w['w_ukv_p'], dq, dk, dv, name=n("d_mla_prep"))
    g['mla_q_norm'], g['mla_kv_norm'] = gqn[0], gkn[0]
    g['mla_w_uq'] = _w_uq_from_padded(matmul_tn(s['cqn'], dqp, name=n("g_w_uq")))
    g['mla_w_ukv'] = _w_ukv_from_padded(matmul_tn(s['ckvn'], dkvp, name=n("g_w_ukv")))
    dproj = jnp.concatenate([dz, dxbc, du, dcq, ddt, dckv, dkpe, jnp.zeros_like(dkpe)], axis=1)
    g['w_in'] = _w_in_from_padded(matmul_tn(s['h1'], dproj, name=n("g_w_in")))
    dh1 = matmul(dproj, w['w_in_p'], nt=True, name=n("d_w_in"), hosted=late_host(g) if late_host else None)
    dh1, late_exchanged = (dh1[0], dh1[1:]) if late_host else (dh1, [])
    dx, gn = rmsnorm_bwd(s['x'], w['attn_norm'], dh1, dx2, name=n("d_attn_norm"))
    g['attn_norm'] = gn[0]
    return dx, g, exchanged, late_exchanged


def _rope_inputs(positions):
    pos = positions.reshape(-1, 1).astype(F32)
    inv_freq = ROPE_THETA ** (-jnp.arange(0, MLA_ROPE, 2, dtype=F32) / MLA_ROPE)
    invf = jnp.concatenate([jnp.zeros((ROPE0,), F32), inv_freq, inv_freq,
                            jnp.zeros((HEAD_W - ROPE0 - MLA_ROPE,), F32)]).reshape(1, HEAD_W)
    return pos, invf


EARLY_GRADS = ('w_out', 'ffn_w_up', 'ffn_w_down')


def kernel(x, positions, attn_norm, w_in, ssd_conv_w, ssd_conv_b, ssd_dt_bias, ssd_a_log, ssd_d, ssd_norm, pool_w, pool_scale, mla_q_norm, mla_w_uq, mla_kv_norm, mla_w_ukv, w_out, ffn_norm, ffn_w_up, ffn_conv_w, ffn_conv_b, ffn_w_down, final_norm, loss_target, m_attn_norm, m_w_in, m_ssd_conv_w, m_ssd_conv_b, m_ssd_dt_bias, m_ssd_a_log, m_ssd_d, m_ssd_norm, m_pool_w, m_pool_scale, m_mla_q_norm, m_mla_w_uq, m_mla_kv_norm, m_mla_w_ukv, m_w_out, m_ffn_norm, m_ffn_w_up, m_ffn_conv_w, m_ffn_conv_b, m_ffn_w_down, m_final_norm, v_attn_norm, v_w_in, v_ssd_conv_w, v_ssd_conv_b, v_ssd_dt_bias, v_ssd_a_log, v_ssd_d, v_ssd_norm, v_pool_w, v_pool_scale, v_mla_q_norm, v_mla_w_uq, v_mla_kv_norm, v_mla_w_ukv, v_w_out, v_ffn_norm, v_ffn_w_up, v_ffn_conv_w, v_ffn_conv_b, v_ffn_w_down, v_final_norm):
    wv = dict(zip(WEIGHTS, (attn_norm, w_in, ssd_conv_w, ssd_conv_b, ssd_dt_bias, ssd_a_log, ssd_d, ssd_norm, pool_w,
                            pool_scale, mla_q_norm, mla_w_uq, mla_kv_norm, mla_w_ukv, w_out, ffn_norm, ffn_w_up,
                            ffn_conv_w, ffn_conv_b, ffn_w_down, final_norm)))
    mv = dict(zip(WEIGHTS, (m_attn_norm, m_w_in, m_ssd_conv_w, m_ssd_conv_b, m_ssd_dt_bias, m_ssd_a_log, m_ssd_d,
                            m_ssd_norm, m_pool_w, m_pool_scale, m_mla_q_norm, m_mla_w_uq, m_mla_kv_norm, m_mla_w_ukv,
                            m_w_out, m_ffn_norm, m_ffn_w_up, m_ffn_conv_w, m_ffn_conv_b, m_ffn_w_down, m_final_norm)))
    vv = dict(zip(WEIGHTS, (v_attn_norm, v_w_in, v_ssd_conv_w, v_ssd_conv_b, v_ssd_dt_bias, v_ssd_a_log, v_ssd_d,
                            v_ssd_norm, v_pool_w, v_pool_scale, v_mla_q_norm, v_mla_w_uq, v_mla_kv_norm, v_mla_w_ukv,
                            v_w_out, v_ffn_norm, v_ffn_w_up, v_ffn_conv_w, v_ffn_conv_b, v_ffn_w_down, v_final_norm)))
    Bl, S, D = x.shape
    chip = 2 * lax.axis_index("x") + lax.axis_index("y")
    core = lax.axis_index("c")

    big_names = list(BIG)
    first_names = ['w_in']
    scan_names = ['mla_w_uq', 'mla_w_ukv']
    post_names = [k for k in big_names if k not in first_names + scan_names]

    def shards(l, names):
        return [wv[k][l].astype(MXU_DTYPE) for k in names]

    def whole_weights(names, own, others):
        rel = [relation_of(j, chip) for j in range(N_CHIPS)]
        return {k: jnp.concatenate(
            [jnp.where(r < 0, mine, jnp.where(r == 0, theirs[0], jnp.where(r == 1, theirs[1], theirs[2]))) for r in rel],
            axis=BIG[k] - 1) for k, mine, theirs in zip(names, own, others)}

    first_others = run_exchange(gather_exchange(shards(0, first_names)), name="gather_w_in_l0")
    placed = []
    for k in CONV_SHARDED:
        sh = wv[k]
        whole = jnp.zeros(sh.shape[:-1] + (sh.shape[-1] * N_CHIPS,), F32)
        whole = lax.dynamic_update_slice_in_dim(whole, sh, chip * sh.shape[-1], axis=sh.ndim - 1)
        placed.append(jnp.where(core == 1, whole, 0.0))
    conv_full = _unpack_rows(all_sum_small(_pack_rows(placed, LANE, F32), name="gather_conv_weights"),
                             [p.shape for p in placed])
    small = {k: wv[k] for k in WEIGHTS if k not in BIG}
    small.update(dict(zip(CONV_SHARDED, conv_full)))

    T = Bl * S
    pos, invf = _rope_inputs(positions)
    group_a = [(k, 1) for k in big_names] + [(k, 0) for k in EARLY_GRADS]
    group_b = [(k, 0) for k in big_names if k not in EARLY_GRADS]

    def scatter_of(group, layer_grads):
        send = [_split_for_chips(layer_grads[l][k], BIG[k] - 1) for k, l in group]
        return send, scatter_exchange(send)

    layer_grads = [None] * DEPTH
    sent = {}
    w0 = _layer_weights(whole_weights(first_names, shards(0, first_names), first_others), small, 0)
    h, saved0, w0, arrived = _layer_fwd(
        x.reshape(T, D), pos, invf, w0, S, 0,
        scan_hosted=gather_exchange(shards(0, scan_names)),
        late_weights=lambda got: _matmul_weights(whole_weights(scan_names, shards(0, scan_names), got)),
        hosted=gather_exchange(shards(0, post_names) + shards(1, big_names)),
        post_weights=lambda got: _matmul_weights(whole_weights(post_names, shards(0, post_names), got[:len(post_names)])))
    others1 = arrived[len(post_names):]
    w1 = _layer_weights(whole_weights(big_names, shards(1, big_names), others1), small, 1)
    h, saved1, w1, _ = _layer_fwd(h, pos, invf, w1, S, 1)
    loss, dh, g_final_norm = final_loss(h, small['final_norm'], loss_target.reshape(T, D))
    dh, layer_grads[1], _, _ = _layer_bwd(dh, pos, invf, w1, saved1, S, 1)

    def host_a(early):
        layer_grads[0] = early
        sent['a'], ex = scatter_of(group_a, layer_grads)
        return ex

    def host_b(_):
        sent['b'], ex = scatter_of(group_b, layer_grads)
        return ex

    dx, layer_grads[0], others_a, others_b = _layer_bwd(dh, pos, invf, w0, saved0, S, 0, host=host_a, late_host=host_b)
    small_names = [k for k in WEIGHTS if k not in BIG]
    grads = {k: jnp.stack([layer_grads[l][k] for l in range(DEPTH)]) for k in small_names if k != 'final_norm'}
    grads['final_norm'] = g_final_norm[0]

    pieces = [{}, {}]
    for tag, group, others in (('a', group_a, others_a), ('b', group_b, others_b)):
        mine = [sum_chips(lax.dynamic_index_in_dim(s, chip, 0, keepdims=False), o, name=f"sum_chips_{k}_l{l}")
                for s, o, (k, l) in zip(sent[tag], others, group)]
        theirs = sibling_swap(mine, name=f"swap_core_sums_{tag}")
        pieces[0].update(dict(zip(group, mine)))
        pieces[1].update(dict(zip(group, theirs)))
    small_sum = all_sum_small(_pack_rows([grads[k] for k in small_names] + [loss[0, :1]], LANE, F32), name="sum_small_grads")
    summed = _unpack_rows(small_sum, [grads[k].shape for k in small_names] + [(1,)])
    loss_total = summed[-1].reshape(())
    g_small = dict(zip(small_names, summed[:-1]))
    for k in CONV_SHARDED:
        n = wv[k].shape[-1]
        g_small[k] = lax.dynamic_slice_in_dim(g_small[k], chip * n, n, axis=g_small[k].ndim - 1)

    out_g, out_d, out_m, out_v = {}, {}, {}, {}
    for k in big_names:
        g_layers = [[pieces[0][(k, l)], pieces[1][(k, l)]] for l in range(DEPTH)]
        out_g[k], out_d[k], out_m[k], out_v[k] = adamw_layers(wv[k], g_layers, mv[k], vv[k], name=f"adamw_{k}")
    at_least_2d = lambda a: a.reshape(1, -1) if a.ndim == 1 else a
    res = adamw_small(*[[at_least_2d(d[k]) for k in small_names] for d in (wv, g_small, mv, vv)], name="adamw_small")
    out_g.update(g_small)
    for dst, r in zip((out_d, out_m, out_v), res):
        dst.update({k: a.reshape(wv[k].shape) for k, a in zip(small_names, r)})
    return (loss_total, dx.reshape(Bl, S, D), *[out_g[k] for k in WEIGHTS], *[out_d[k] for k in WEIGHTS],
            *[out_m[k] for k in WEIGHTS], *[out_v[k] for k in WEIGHTS])
```

```python
import functools
import math
from typing import Callable, NamedTuple

import jax
import jax.numpy as jnp
from jax import lax
from jax.experimental import pallas as pl
from jax.experimental.pallas import tpu as pltpu

F32 = jnp.float32
MXU_DTYPE = jnp.bfloat16
HI = lax.Precision.HIGHEST

D_MODEL = 1024
DEPTH = 2
EPS = 1e-6
SSD_HEADS = 16
SSD_HEAD_DIM = 64
SSD_WIDTH = 1024
SSD_GROUPS = 2
SSD_STATE = 128
SSD_CONV = 4
SSD_CHUNK = 128
SSD_CONV_CH = 1536
POOL_GROUPS = 4
POOL_GROUP_DIM = 128
POOL_WIDTH = 512
POOL_WINDOWS = (2, 4, 8, 16)
MLA_HEADS = 8
MLA_Q_RANK = 384
MLA_KV_RANK = 256
MLA_NOPE = 64
MLA_ROPE = 32
MLA_V = 64
MLA_QK = 96
MLA_WIDTH = 512
ROPE_THETA = 10000.0
MIX_WIDTH = 2048
IN_COLS = 3760
D_FF = 2816
FFN_CONV = 3
ADAM_LR = 0.001
ADAM_B1 = 0.9
ADAM_B2 = 0.999
ADAM_EPS = 1e-08
ADAM_WD = 0.01
ADAM_STEP = 10

LANE = 128
HALO = 8
POOL_HALO = 16
PZ0 = 0
PXBC0 = 1024
PU0 = 2560
PCQ0 = 3072
PDT0 = 3456
PCKV0 = 3584
PKPE0 = 3840
PROJ_W = 4096
HEAD_W = 128
MLA_PAD = MLA_HEADS * HEAD_W
MIXP = SSD_WIDTH + POOL_WIDTH + MLA_PAD
N_CHIPS = 4
N_DEV = 8
VMEM_LIMIT = 56 * 1024 * 1024
MATMUL_VMEM_BUDGET = 40 * 1024 * 1024


def _cparams(dims, vmem=None):
    return pltpu.CompilerParams(dimension_semantics=dims, vmem_limit_bytes=vmem or VMEM_LIMIT)


def _sds(shape, dtype):
    return jax.ShapeDtypeStruct(tuple(shape), dtype)


def _mx(v):
    return v.astype(MXU_DTYPE)


def _dot(a, b):
    return jnp.dot(_mx(a), _mx(b), preferred_element_type=F32)


def _dot_nt(a, b):
    return lax.dot_general(_mx(a), _mx(b), (((1,), (1,)), ((), ())), preferred_element_type=F32)


def _dot_tn(a, b):
    return lax.dot_general(_mx(a), _mx(b), (((0,), (0,)), ((), ())), preferred_element_type=F32)


def _dot_hi(a, b):
    return jnp.dot(a, b, preferred_element_type=F32, precision=HI)


def _sigmoid(v):
    return 1.0 / (1.0 + jnp.exp(-v))


def _pick(n, prefs):
    for p in prefs:
        if n % p == 0:
            return p
    return n


def matmul(a, b, *, res=None, out_dtype=F32, name, nt=False, kblock=0, tm=None, tn=None, hosted=None):
    M, K = a.shape
    N = b.shape[0] if nt else b.shape[1]
    assert (b.shape[1] % K == 0) if nt else (K == b.shape[0] and kblock == 0)
    tn = tn or _pick(N, (1024, 1408, 1280, 512, 256, 128))
    a_bytes, o_bytes = jnp.dtype(a.dtype).itemsize, jnp.dtype(out_dtype).itemsize
    fits = lambda t: 2 * (t * K * a_bytes + K * tn * 2 + t * tn * (o_bytes + (4 if res is not None else 0))) <= MATMUL_VMEM_BUDGET
    tm = tm or next(t for t in (2048, 1024, 512, 256, 128) if M % t == 0 and (fits(t) or t == 128))
    grid = (M // tm, N // tn)
    wrap, h_in, h_ospecs, h_oshapes, h_scratch = _hosting(hosted, grid, 2 if res is None else 3, 1, 0)

    def body(*refs):
        a_ref, b_ref = refs[:2]
        o_ref = refs[-1]
        out = (_dot_nt if nt else _dot)(a_ref[...], b_ref[...])
        if res is not None:
            out = out + refs[2][...]
        o_ref[...] = out.astype(out_dtype)

    b_spec = pl.BlockSpec((tn, K), lambda i, j: (j, kblock)) if nt else pl.BlockSpec((K, tn), lambda i, j: (0, j))
    in_specs = [pl.BlockSpec((tm, K), lambda i, j: (i, 0)), b_spec]
    args = [a, b]
    if res is not None:
        in_specs.append(pl.BlockSpec((tm, tn), lambda i, j: (i, j)))
        args.append(res)
    out = pl.pallas_call(
        wrap(body), name=name, grid=grid, in_specs=in_specs + [pl.BlockSpec(memory_space=pl.ANY)] * len(h_in),
        out_specs=[pl.BlockSpec((tm, tn), lambda i, j: (i, j))] + h_ospecs, out_shape=[_sds((M, N), out_dtype)] + h_oshapes,
        scratch_shapes=h_scratch,
        compiler_params=_cparams(("arbitrary", "arbitrary") if hosted else ("parallel", "parallel")),
    )(*args, *h_in)
    return out if hosted else out[0]


def matmul_tn(a, g, *, name, tm=None, tn=None, tk=None):
    T, M = a.shape
    T2, N = g.shape
    assert T == T2
    tm = tm or _pick(M, (1408, 1280, 1024, 512, 384, 256, 128))
    tn = tn or _pick(N, (1024, 1408, 512, 256, 128))
    tk = tk or _pick(T, (1024, 512, 256, 128))
    nk = T // tk

    def body(a_ref, g_ref, o_ref, acc):
        k = pl.program_id(2)
        part = _dot_tn(a_ref[...], g_ref[...])

        @pl.when(k == 0)
        def _():
            acc[...] = part

        @pl.when(k > 0)
        def _():
            acc[...] += part

        @pl.when(k == nk - 1)
        def _():
            o_ref[...] = acc[...].astype(o_ref.dtype)

    return pl.pallas_call(
        body, name=name, grid=(M // tm, N // tn, nk),
        in_specs=[pl.BlockSpec((tk, tm), lambda i, j, k: (k, i)), pl.BlockSpec((tk, tn), lambda i, j, k: (k, j))],
        out_specs=pl.BlockSpec((tm, tn), lambda i, j, k: (i, j)), out_shape=_sds((M, N), MXU_DTYPE),
        scratch_shapes=[pltpu.VMEM((tm, tn), F32)],
        compiler_params=_cparams(("parallel", "parallel", "arbitrary")),
    )(a, g)


def rmsnorm_fwd(x, gamma, *, name, tm=1024):
    T, D = x.shape
    tm = _pick(T, (tm, 256, 128))

    def body(x_ref, g_ref, o_ref):
        xv = x_ref[...]
        r = lax.rsqrt(jnp.mean(xv * xv, axis=-1, keepdims=True) + EPS)
        o_ref[...] = ((xv * r) * g_ref[...]).astype(MXU_DTYPE)

    return pl.pallas_call(
        body, name=name, grid=(T // tm,),
        in_specs=[pl.BlockSpec((tm, D), lambda i: (i, 0)), pl.BlockSpec((1, D), lambda i: (0, 0))],
        out_specs=pl.BlockSpec((tm, D), lambda i: (i, 0)), out_shape=_sds((T, D), MXU_DTYPE),
        compiler_params=_cparams(("parallel",)),
    )(x, gamma.reshape(1, D))


def _rms_bwd_tile(xv, gamma, dh):
    r = lax.rsqrt(jnp.mean(xv * xv, axis=-1, keepdims=True) + EPS)
    xh = xv * r
    dg = jnp.sum(dh * xh, axis=0, keepdims=True)
    dn = dh * gamma
    dx = r * (dn - xh * jnp.mean(dn * xh, axis=-1, keepdims=True))
    return dx, dg


def rmsnorm_bwd(x, gamma, dh, dres, *, name, tm=512):
    T, D = x.shape
    tm = _pick(T, (tm, 256, 128))

    def body(x_ref, g_ref, dh_ref, dr_ref, dx_ref, dg_ref):
        dx, dg = _rms_bwd_tile(x_ref[...], g_ref[...], dh_ref[...].astype(F32))
        dx_ref[...] = dx + dr_ref[...]

        @pl.when(pl.program_id(0) == 0)
        def _():
            dg_ref[...] = dg

        @pl.when(pl.program_id(0) > 0)
        def _():
            dg_ref[...] += dg

    row = pl.BlockSpec((tm, D), lambda i: (i, 0))
    vec = pl.BlockSpec((1, D), lambda i: (0, 0))
    return pl.pallas_call(
        body, name=name, grid=(T // tm,), in_specs=[row, vec, row, row], out_specs=[row, vec],
        out_shape=[_sds((T, D), F32), _sds((1, D), F32)], compiler_params=_cparams(("arbitrary",)),
    )(x, gamma.reshape(1, D), dh, dres)


def final_loss(x, gamma, target, *, name="final_loss", tm=512):
    T, D = x.shape
    tm = _pick(T, (tm, 256, 128))

    def body(x_ref, g_ref, t_ref, l_ref, dx_ref, dg_ref):
        xv = x_ref[...]
        gam = g_ref[...]
        r = lax.rsqrt(jnp.mean(xv * xv, axis=-1, keepdims=True) + EPS)
        y = (xv * r) * gam
        err = y - t_ref[...]
        part = 0.5 * jnp.sum(jnp.sum(err * err, axis=-1, keepdims=True) / D, axis=0, keepdims=True)
        dx, dg = _rms_bwd_tile(xv, gam, err / D)
        dx_ref[...] = dx

        @pl.when(pl.program_id(0) == 0)
        def _():
            dg_ref[...] = dg
            l_ref[...] = jnp.broadcast_to(part, l_ref.shape)

        @pl.when(pl.program_id(0) > 0)
        def _():
            dg_ref[...] += dg
            l_ref[...] += jnp.broadcast_to(part, l_ref.shape)

    row = pl.BlockSpec((tm, D), lambda i: (i, 0))
    vec = pl.BlockSpec((1, D), lambda i: (0, 0))
    return pl.pallas_call(
        body, name=name, grid=(T // tm,), in_specs=[row, vec, row],
        out_specs=[pl.BlockSpec((1, LANE), lambda i: (0, 0)), row, vec],
        out_shape=[_sds((1, LANE), F32), _sds((T, D), F32), _sds((1, D), F32)],
        compiler_params=_cparams(("arbitrary",)),
    )(x, gamma.reshape(1, D), target)


def _halo_prev(ts):
    return lambda i, j, off=0: (jnp.maximum(i * (ts // HALO) - 1, 0), j + off)


def _cat_prev(cur, halo, first):
    return jnp.concatenate([jnp.where(first, 0.0, halo), cur], axis=0)


def _cat_next(cur, halo, last):
    return jnp.concatenate([cur, jnp.where(last, 0.0, halo)], axis=0)


def _delayed(cat, r):
    if r == 0:
        return cat[HALO:]
    return pltpu.roll(cat, r, axis=0)[HALO:]


def _advanced(cat, r):
    n = cat.shape[0]
    if r == 0:
        return cat[:n - HALO]
    return pltpu.roll(cat, n - r, axis=0)[:n - HALO]


def _conv_pre(cat, w, b, K):
    acc = _delayed(cat, K - 1) * w[0:1, :] + b
    for k in range(1, K):
        acc = acc + _delayed(cat, K - 1 - k) * w[k:k + 1, :]
    return acc


def _pad_rows8(w):
    return jnp.pad(w, ((0, 8 - w.shape[0]), (0, 0)))


def ssd_conv_fwd(proj, w, b, S, *, name, ts=1024, tc=512):
    T = proj.shape[0]
    C, K = SSD_CONV_CH, SSD_CONV
    ts = _pick(S, (ts, 256, 128))
    off = PXBC0 // tc
    ns = S // ts

    def body(x_ref, h_ref, w_ref, b_ref, o_ref):
        first = (pl.program_id(0) % ns) == 0
        pre = _conv_pre(_cat_prev(x_ref[...], h_ref[...], first), w_ref[...], b_ref[...], K)
        o_ref[...] = pre * _sigmoid(pre)

    return pl.pallas_call(
        body, name=name, grid=(T // ts, C // tc),
        in_specs=[pl.BlockSpec((ts, tc), lambda i, j: (i, j + off)),
                  pl.BlockSpec((HALO, tc), functools.partial(_halo_prev(ts), off=off)),
                  pl.BlockSpec((8, tc), lambda i, j: (0, j)), pl.BlockSpec((1, tc), lambda i, j: (0, j))],
        out_specs=pl.BlockSpec((ts, tc), lambda i, j: (i, j)), out_shape=_sds((T, C), F32),
        compiler_params=_cparams(("parallel", "parallel")),
    )(proj, proj, _pad_rows8(w), b.reshape(1, C))


def _conv_stats(dpre, cat, K, ts):
    rows = [jnp.sum(dpre[:ts] * _delayed(cat, K - 1 - k)[:ts], axis=0, keepdims=True) for k in range(K)]
    rows.append(jnp.sum(dpre[:ts], axis=0, keepdims=True))
    rows.append(jnp.zeros((8 - len(rows), dpre.shape[1]), F32))
    return jnp.concatenate(rows, axis=0)


def _conv_transposed(dpre, wv, K):
    acc = _advanced(dpre, K - 1) * wv[0:1, :]
    for k in range(1, K):
        acc = acc + _advanced(dpre, K - 1 - k) * wv[k:k + 1, :]
    return acc


def ssd_conv_bwd(proj, w, b, dxc, S, *, name, ts=512, tc=512):
    T = proj.shape[0]
    C, K = SSD_CONV_CH, SSD_CONV
    ts = _pick(S, (ts, 256, 128))
    off = PXBC0 // tc
    ns = S // ts
    nblk = T // HALO

    def body(x_ref, xp_ref, xn_ref, w_ref, b_ref, d_ref, dn_ref, o_ref, acc_ref):
        i = pl.program_id(1)
        first = (i % ns) == 0
        last = (i % ns) == ns - 1
        cat = jnp.concatenate([jnp.where(first, 0.0, xp_ref[...]), x_ref[...], xn_ref[...]], axis=0)
        wv = w_ref[...]
        pre = _conv_pre(cat, wv, b_ref[...], K)
        sg = _sigmoid(pre)
        dpre = _cat_next(d_ref[...], dn_ref[...], last) * (sg * (1.0 + pre * (1.0 - sg)))
        o_ref[...] = _conv_transposed(dpre, wv, K).astype(o_ref.dtype)
        part = _conv_stats(dpre, cat, K, ts)

        @pl.when(i == 0)
        def _():
            acc_ref[...] = part

        @pl.when(i > 0)
        def _():
            acc_ref[...] += part

    hp = _halo_prev(ts)
    hn = lambda i: jnp.minimum((i + 1) * (ts // HALO), nblk - 1)
    return pl.pallas_call(
        body, name=name, grid=(C // tc, T // ts),
        in_specs=[pl.BlockSpec((ts, tc), lambda j, i: (i, j + off)),
                  pl.BlockSpec((HALO, tc), lambda j, i: hp(i, j, off)),
                  pl.BlockSpec((HALO, tc), lambda j, i: (hn(i), j + off)),
                  pl.BlockSpec((8, tc), lambda j, i: (0, j)), pl.BlockSpec((1, tc), lambda j, i: (0, j)),
                  pl.BlockSpec((ts, tc), lambda j, i: (i, j)), pl.BlockSpec((HALO, tc), lambda j, i: (hn(i), j))],
        out_specs=[pl.BlockSpec((ts, tc), lambda j, i: (i, j)), pl.BlockSpec((8, tc), lambda j, i: (0, j))],
        out_shape=[_sds((T, C), MXU_DTYPE), _sds((8, C), F32)],
        compiler_params=_cparams(("parallel", "arbitrary")),
    )(proj, proj, proj, _pad_rows8(w), b.reshape(1, C), dxc, dxc)


def ffn_conv_gate_fwd(up, w, b, S, *, name, ts=512, tc=1408):
    T, C2 = up.shape
    C, K = C2 // 2, FFN_CONV
    ts = _pick(S, (ts, 256, 128))
    nj = C // tc
    ns = S // ts
    w8 = _pad_rows8(w)
    b2 = b.reshape(1, C2)

    def body(g_ref, gh_ref, v_ref, vh_ref, wg_ref, wv_ref, bg_ref, bv_ref, o_ref):
        first = (pl.program_id(0) % ns) == 0
        g = _conv_pre(_cat_prev(g_ref[...], gh_ref[...], first), wg_ref[...], bg_ref[...], K)
        v = _conv_pre(_cat_prev(v_ref[...], vh_ref[...], first), wv_ref[...], bv_ref[...], K)
        o_ref[...] = (g * _sigmoid(g) * v).astype(o_ref.dtype)

    hp = _halo_prev(ts)
    return pl.pallas_call(
        body, name=name, grid=(T // ts, nj),
        in_specs=[pl.BlockSpec((ts, tc), lambda i, j: (i, j)), pl.BlockSpec((HALO, tc), lambda i, j: hp(i, j)),
                  pl.BlockSpec((ts, tc), lambda i, j: (i, j + nj)), pl.BlockSpec((HALO, tc), lambda i, j: hp(i, j, nj)),
                  pl.BlockSpec((8, tc), lambda i, j: (0, j)), pl.BlockSpec((8, tc), lambda i, j: (0, j + nj)),
                  pl.BlockSpec((1, tc), lambda i, j: (0, j)), pl.BlockSpec((1, tc), lambda i, j: (0, j + nj))],
        out_specs=pl.BlockSpec((ts, tc), lambda i, j: (i, j)), out_shape=_sds((T, C), MXU_DTYPE),
        compiler_params=_cparams(("parallel", "parallel")),
    )(up, up, up, up, w8, w8, b2, b2)


def ffn_conv_gate_bwd(up, w, b, dact, S, *, name, ts=256, tc=1408):
    T, C2 = up.shape
    C, K = C2 // 2, FFN_CONV
    ts = _pick(S, (ts, 256, 128))
    nj = C // tc
    ns = S // ts
    nblk = T // HALO
    w8 = _pad_rows8(w)
    b2 = b.reshape(1, C2)

    def body(g_ref, gp_ref, gn_ref, v_ref, vp_ref, vn_ref, wg_ref, wv_ref, bg_ref, bv_ref, d_ref, dn_ref,
             dug_ref, duv_ref, ag_ref, av_ref):
        i = pl.program_id(1)
        first = (i % ns) == 0
        last = (i % ns) == ns - 1
        gcat = jnp.concatenate([jnp.where(first, 0.0, gp_ref[...]), g_ref[...], gn_ref[...]], axis=0)
        vcat = jnp.concatenate([jnp.where(first, 0.0, vp_ref[...]), v_ref[...], vn_ref[...]], axis=0)
        wg, wv = wg_ref[...], wv_ref[...]
        g = _conv_pre(gcat, wg, bg_ref[...], K)
        v = _conv_pre(vcat, wv, bv_ref[...], K)
        d = _cat_next(d_ref[...], dn_ref[...], last)
        sg = _sigmoid(g)
        dg = d * v * (sg * (1.0 + g * (1.0 - sg)))
        dv = d * (g * sg)
        dug_ref[...] = _conv_transposed(dg, wg, K).astype(dug_ref.dtype)
        duv_ref[...] = _conv_transposed(dv, wv, K).astype(duv_ref.dtype)
        sgp, svp = _conv_stats(dg, gcat, K, ts), _conv_stats(dv, vcat, K, ts)

        @pl.when(i == 0)
        def _():
            ag_ref[...] = sgp
            av_ref[...] = svp

        @pl.when(i > 0)
        def _():
            ag_ref[...] += sgp
            av_ref[...] += svp

    hp = _halo_prev(ts)
    hn = lambda i: jnp.minimum((i + 1) * (ts // HALO), nblk - 1)
    cur = lambda off: pl.BlockSpec((ts, tc), lambda j, i: (i, j + off))
    prv = lambda off: pl.BlockSpec((HALO, tc), lambda j, i: hp(i, j, off))
    nxt = lambda off: pl.BlockSpec((HALO, tc), lambda j, i: (hn(i), j + off))
    row = lambda r, off: pl.BlockSpec((r, tc), lambda j, i: (0, j + off))
    dug, duv, ag, av = pl.pallas_call(
        body, name=name, grid=(nj, T // ts),
        in_specs=[cur(0), prv(0), nxt(0), cur(nj), prv(nj), nxt(nj), row(8, 0), row(8, nj), row(1, 0), row(1, nj),
                  cur(0), nxt(0)],
        out_specs=[cur(0), cur(0), row(8, 0), row(8, 0)],
        out_shape=[_sds((T, C), MXU_DTYPE), _sds((T, C), MXU_DTYPE), _sds((8, C), F32), _sds((8, C), F32)],
        compiler_params=_cparams(("parallel", "arbitrary")),
    )(up, up, up, up, up, up, w8, w8, b2, b2, dact, dact)
    return dug, duv, jnp.concatenate([ag, av], axis=1)


def _pool_counts(pos, w):
    return jnp.minimum(pos + 1.0, float(w))


def pool_fwd(proj, pool_w, pool_scale, S, *, name, ts=1024):
    T = proj.shape[0]
    C, G, GD, H = POOL_WIDTH, POOL_GROUPS, POOL_GROUP_DIM, POOL_HALO
    ts = _pick(S, (ts, 256, 128))
    ns = S // ts
    off = PU0 // C

    def body(u_ref, h_ref, w_ref, s_ref, y_ref, p_ref):
        i = pl.program_id(0)
        first = (i % ns) == 0
        cat = jnp.concatenate([jnp.where(first, 0.0, h_ref[...]), u_ref[...]], axis=0)
        pos = ((i % ns) * ts + lax.broadcasted_iota(jnp.int32, (ts, 1), 0)).astype(F32)
        sums = cat
        win = 1
        for g, wlen in enumerate(POOL_WINDOWS):
            while win < wlen:
                sums = sums + pltpu.roll(sums, win, axis=0)
                win *= 2
            sl = slice(g * GD, (g + 1) * GD)
            pooled = sums[H:, sl] / _pool_counts(pos, wlen) - cat[H:, sl]
            p_ref[:, sl] = pooled.astype(p_ref.dtype)
            y_ref[:, sl] = (_dot(pooled, w_ref[g]) * s_ref[:, sl]).astype(y_ref.dtype)

    return pl.pallas_call(
        body, name=name, grid=(T // ts,),
        in_specs=[pl.BlockSpec((ts, C), lambda i: (i, off)),
                  pl.BlockSpec((H, C), lambda i: (jnp.maximum(i * (ts // H) - 1, 0), off)),
                  pl.BlockSpec((G, GD, GD), lambda i: (0, 0, 0)), pl.BlockSpec((1, C), lambda i: (0, 0))],
        out_specs=[pl.BlockSpec((ts, C), lambda i: (i, 0)), pl.BlockSpec((ts, C), lambda i: (i, 0))],
        out_shape=[_sds((T, C), MXU_DTYPE), _sds((T, C), MXU_DTYPE)],
        compiler_params=_cparams(("parallel",)),
    )(proj, proj, _mx(pool_w), pool_scale.reshape(1, C))


def pool_bwd(dmix, pooled, pool_w, pool_scale, S, *, name, ts=1024):
    T = dmix.shape[0]
    C, G, GD, H = POOL_WIDTH, POOL_GROUPS, POOL_GROUP_DIM, POOL_HALO
    ts = _pick(S, (ts, 256, 128))
    ns = S // ts
    off = SSD_WIDTH // C
    nblk = T // H

    def body(d_ref, dh_ref, p_ref, w_ref, s_ref, du_ref, dw_ref, ds_ref):
        i = pl.program_id(0)
        last = (i % ns) == ns - 1
        dcat = jnp.concatenate([d_ref[...], jnp.where(last, 0.0, dh_ref[...])], axis=0)
        n = ts + H
        pos = ((i % ns) * ts + lax.broadcasted_iota(jnp.int32, (n, 1), 0)).astype(F32)
        dws, dss = [], []
        for g, wlen in enumerate(POOL_WINDOWS):
            sl = slice(g * GD, (g + 1) * GD)
            wg = w_ref[g]
            pg = p_ref[:, sl]
            dys = dcat[:, sl] * s_ref[:, sl]
            dss.append(jnp.sum(dcat[:ts, sl] * _dot(pg, wg), axis=0, keepdims=True))
            dws.append(_dot_tn(pg, dys[:ts]))
            dp = _dot_nt(dys, wg)
            q = dp / _pool_counts(pos, wlen)
            win = 1
            while win < wlen:
                q = q + pltpu.roll(q, n - win, axis=0)
                win *= 2
            du_ref[:, sl] = (q[:ts] - dp[:ts]).astype(du_ref.dtype)
        dsp = jnp.concatenate(dss, axis=1)

        @pl.when(i == 0)
        def _():
            for g in range(G):
                dw_ref[g] = dws[g]
            ds_ref[...] = dsp

        @pl.when(i > 0)
        def _():
            for g in range(G):
                dw_ref[g] += dws[g]
            ds_ref[...] += dsp

    return pl.pallas_call(
        body, name=name, grid=(T // ts,),
        in_specs=[pl.BlockSpec((ts, C), lambda i: (i, off)),
                  pl.BlockSpec((H, C), lambda i: (jnp.minimum((i + 1) * (ts // H), nblk - 1), off)),
                  pl.BlockSpec((ts, C), lambda i: (i, 0)),
                  pl.BlockSpec((G, GD, GD), lambda i: (0, 0, 0)), pl.BlockSpec((1, C), lambda i: (0, 0))],
        out_specs=[pl.BlockSpec((ts, C), lambda i: (i, 0)), pl.BlockSpec((G, GD, GD), lambda i: (0, 0, 0)),
                   pl.BlockSpec((1, C), lambda i: (0, 0))],
        out_shape=[_sds((T, C), MXU_DTYPE), _sds((G, GD, GD), F32), _sds((1, C), F32)],
        compiler_params=_cparams(("arbitrary",)),
    )(dmix, dmix, pooled, _mx(pool_w), pool_scale.reshape(1, C))


ROPE0 = MLA_NOPE
ROPE_HALF = MLA_ROPE // 2


def _rope_tables(pos, invf):
    lane = lax.broadcasted_iota(jnp.int32, (1, HEAD_W), 1)
    ang = pos * invf
    cs, sn = jnp.cos(ang), jnp.sin(ang)
    in_a = (lane >= ROPE0) & (lane < ROPE0 + ROPE_HALF)
    in_b = (lane >= ROPE0 + ROPE_HALF) & (lane < ROPE0 + MLA_ROPE)
    return jnp.where(in_a | in_b, cs, 1.0), jnp.where(in_a, -sn, 0.0), jnp.where(in_b, sn, 0.0), in_a | in_b


def _rope(v, cosf, sin_a, sin_b):
    return (v * cosf + pltpu.roll(v, HEAD_W - ROPE_HALF, axis=1) * sin_a + pltpu.roll(v, ROPE_HALF, axis=1) * sin_b)


def _unrope(d, cosf, sin_a, sin_b):
    return (d * cosf + pltpu.roll(d * sin_a, ROPE_HALF, axis=1) + pltpu.roll(d * sin_b, HEAD_W - ROPE_HALF, axis=1))


def _rms_tile(xv, gamma):
    return (xv * lax.rsqrt(jnp.mean(xv * xv, axis=-1, keepdims=True) + EPS)) * gamma


def mla_prep_fwd(proj, pos, invf, q_norm, w_uq_p, kv_norm, w_ukv_p, *, name, tm=512):
    T = proj.shape[0]
    tm = _pick(T, (tm, 256, 128))
    QR, KR, P = MLA_Q_RANK, MLA_KV_RANK, MLA_PAD

    def body(cq_ref, ckv_ref, kpe_ref, pos_ref, invf_ref, qn_ref, wq_ref, kn_ref, wkv_ref,
             q_ref, k_ref, v_ref, cqn_ref, ckvn_ref):
        cosf, sin_a, sin_b, _ = _rope_tables(pos_ref[...], invf_ref[...])
        cqn = _rms_tile(cq_ref[...], qn_ref[...]).astype(MXU_DTYPE)
        ckvn = _rms_tile(ckv_ref[...], kn_ref[...]).astype(MXU_DTYPE)
        cqn_ref[...] = cqn
        ckvn_ref[...] = ckvn
        qp = _dot(cqn, wq_ref[...])
        kvp = _dot(ckvn, wkv_ref[...])
        kpe = _rope(kpe_ref[...], cosf, sin_a, sin_b)
        for h in range(MLA_HEADS):
            sl = slice(h * HEAD_W, (h + 1) * HEAD_W)
            q_ref[:, sl] = (_rope(qp[:, sl], cosf, sin_a, sin_b) * ATTN_SCALE).astype(q_ref.dtype)
            k_ref[:, sl] = (kvp[:, sl] + kpe).astype(k_ref.dtype)
            v_ref[:, sl] = kvp[:, P + h * HEAD_W:P + (h + 1) * HEAD_W].astype(v_ref.dtype)

    row = lambda w: pl.BlockSpec((tm, w), lambda i: (i, 0))
    full = lambda a, b: pl.BlockSpec((a, b), lambda i: (0, 0))
    return pl.pallas_call(
        body, name=name, grid=(T // tm,),
        in_specs=[pl.BlockSpec((tm, QR), lambda i: (i, PCQ0 // QR)), pl.BlockSpec((tm, KR), lambda i: (i, PCKV0 // KR)),
                  pl.BlockSpec((tm, LANE), lambda i: (i, PKPE0 // LANE)), row(1), full(1, LANE),
                  full(1, QR), full(QR, P), full(1, KR), full(KR, 2 * P)],
        out_specs=[row(P), row(P), row(P), row(QR), row(KR)],
        out_shape=[_sds((T, P), MXU_DTYPE)] * 3 + [_sds((T, QR), MXU_DTYPE), _sds((T, KR), MXU_DTYPE)],
        compiler_params=_cparams(("parallel",)),
    )(proj, proj, proj, pos, invf, q_norm.reshape(1, QR), w_uq_p, kv_norm.reshape(1, KR), w_ukv_p)


def mla_prep_bwd(proj, pos, invf, q_norm, w_uq_p, kv_norm, w_ukv_p, dq, dk, dv, *, name, tm=512):
    T = proj.shape[0]
    tm = _pick(T, (tm, 256, 128))
    QR, KR, P = MLA_Q_RANK, MLA_KV_RANK, MLA_PAD

    def body(cq_ref, ckv_ref, pos_ref, invf_ref, qn_ref, wq_ref, kn_ref, wkv_ref, dq_ref, dk_ref, dv_ref,
             dqp_ref, dkvp_ref, dcq_ref, dckv_ref, dkpe_ref, dqn_ref, dkn_ref):
        cosf, sin_a, sin_b, rot = _rope_tables(pos_ref[...], invf_ref[...])
        dkpe = jnp.zeros((tm, HEAD_W), F32)
        for h in range(MLA_HEADS):
            sl = slice(h * HEAD_W, (h + 1) * HEAD_W)
            dqp_ref[:, sl] = _unrope(dq_ref[:, sl] * ATTN_SCALE, cosf, sin_a, sin_b).astype(dqp_ref.dtype)
            dkh = dk_ref[:, sl]
            dkpe = dkpe + dkh
            dkvp_ref[:, sl] = dkh.astype(dkvp_ref.dtype)
            dkvp_ref[:, P + h * HEAD_W:P + (h + 1) * HEAD_W] = dv_ref[:, sl].astype(dkvp_ref.dtype)
        dkpe_ref[...] = jnp.where(rot, _unrope(dkpe, cosf, sin_a, sin_b), 0.0).astype(dkpe_ref.dtype)
        dcq, dqn = _rms_bwd_tile(cq_ref[...], qn_ref[...], _dot_nt(dqp_ref[...], wq_ref[...]))
        dckv, dkn = _rms_bwd_tile(ckv_ref[...], kn_ref[...], _dot_nt(dkvp_ref[...], wkv_ref[...]))
        dcq_ref[...] = dcq.astype(dcq_ref.dtype)
        dckv_ref[...] = dckv.astype(dckv_ref.dtype)

        @pl.when(pl.program_id(0) == 0)
        def _():
            dqn_ref[...] = dqn
            dkn_ref[...] = dkn

        @pl.when(pl.program_id(0) > 0)
        def _():
            dqn_ref[...] += dqn
            dkn_ref[...] += dkn

    row = lambda w: pl.BlockSpec((tm, w), lambda i: (i, 0))
    full = lambda a, b: pl.BlockSpec((a, b), lambda i: (0, 0))
    return pl.pallas_call(
        body, name=name, grid=(T // tm,),
        in_specs=[pl.BlockSpec((tm, QR), lambda i: (i, PCQ0 // QR)), pl.BlockSpec((tm, KR), lambda i: (i, PCKV0 // KR)),
                  row(1), full(1, LANE), full(1, QR), full(QR, P), full(1, KR), full(KR, 2 * P), row(P), row(P), row(P)],
        out_specs=[row(P), row(2 * P), row(QR), row(KR), row(LANE), full(1, QR), full(1, KR)],
        out_shape=[_sds((T, P), MXU_DTYPE), _sds((T, 2 * P), MXU_DTYPE), _sds((T, QR), MXU_DTYPE),
                   _sds((T, KR), MXU_DTYPE), _sds((T, LANE), MXU_DTYPE), _sds((1, QR), F32), _sds((1, KR), F32)],
        compiler_params=_cparams(("arbitrary",)),
    )(proj, proj, pos, invf, q_norm.reshape(1, QR), w_uq_p, kv_norm.reshape(1, KR), w_ukv_p, dq, dk, dv)


ATTN_SCALE = 1.0 / math.sqrt(MLA_QK)


def _causal_mask(i, j, blk):
    row = lax.broadcasted_iota(jnp.int32, (blk, blk), 0)
    col = lax.broadcasted_iota(jnp.int32, (blk, blk), 1)
    return col <= row + (i - j) * blk


def _hosting(hosted, grid, n_in, n_out, n_scratch):
    if hosted is None:
        return (lambda body: body), (), [], [], []
    hi, ho = len(hosted.inputs), len(hosted.out_shapes)

    def wrap(body):
        def full(*refs):
            ins, rest = refs[:n_in + hi], refs[n_in + hi:]
            outs, scr = rest[:n_out + ho], rest[n_out + ho:]
            parts = ins[n_in:], outs[n_out:], scr[n_scratch:]
            ids = [pl.program_id(d) for d in range(len(grid))]
            step = ids[0]
            for d in range(1, len(grid)):
                step = step * grid[d] + ids[d]
            total = math.prod(grid)

            @pl.when(step == 0)
            def _():
                hosted.start(*parts)

            body(*ins[:n_in], *outs[:n_out], *scr[:n_scratch])

            @pl.when(step == total // 2)
            def _():
                hosted.relay(*parts)

            @pl.when(step == total - 1)
            def _():
                hosted.finish(*parts)

        return full

    hbm = pl.BlockSpec(memory_space=pl.ANY)
    return wrap, tuple(hosted.inputs), [hbm] * ho, list(hosted.out_shapes), list(hosted.sems)


def flash_fwd(q, k, v, S, *, name, blk=1024, hosted=None):
    T, P = q.shape
    blk = _pick(S // 2, (blk, 256, 128))
    B, nq, H, W = T // S, S // (2 * blk), MLA_HEADS, HEAD_W
    grid = (B, H, nq)
    wrap, h_in, h_ospecs, h_oshapes, h_scratch = _hosting(hosted, grid, 3, 2, 0)

    def body(q_ref, k_ref, v_ref, o_ref, lse_ref):
        i = pl.program_id(2)
        q_up, q_lo = q_ref[:blk, :], q_ref[blk:, :]

        def online(qv, kv, vv, carry, masked):
            m_prev, l_prev, acc = carry
            s = _dot_nt(qv, kv)
            if masked:
                s = jnp.where(_causal_mask(0, 0, blk), s, -jnp.inf)
            m_new = jnp.maximum(m_prev, jnp.max(s, axis=1, keepdims=True))
            p = jnp.exp(s - m_new)
            alpha = jnp.exp(m_prev - m_new)
            return (m_new, alpha * l_prev + jnp.sum(p, axis=1, keepdims=True), alpha * acc + _dot(p, vv))

        def keys(j):
            rows = pl.ds(pl.multiple_of(j * blk, blk), blk)
            return k_ref[rows, :], v_ref[rows, :]

        def both(j, carry):
            kv, vv = keys(j)
            return online(q_up, kv, vv, carry[0], False), online(q_lo, kv, vv, carry[1], False)

        init = (jnp.full((blk, 1), -jnp.inf, F32), jnp.zeros((blk, 1), F32), jnp.zeros((blk, W), F32))
        up, lo = lax.fori_loop(0, 2 * i, both, (init, init))
        kv, vv = keys(2 * i)
        up = online(q_up, kv, vv, up, True)
        lo = online(q_lo, kv, vv, lo, False)
        kv, vv = keys(2 * i + 1)
        lo = online(q_lo, kv, vv, lo, True)
        for rows, (m, l, acc) in ((slice(0, blk), up), (slice(blk, 2 * blk), lo)):
            o_ref[rows, :] = acc / l
            lse_ref[rows, :] = jnp.broadcast_to(m + jnp.log(l), (blk, W))

    qmap = lambda b, h, i: (b * nq + i, h)
    kmap = lambda b, h, i: (b, h)
    hbm = pl.BlockSpec(memory_space=pl.ANY)
    return pl.pallas_call(
        wrap(body), name=name, grid=grid,
        in_specs=[pl.BlockSpec((2 * blk, W), qmap), pl.BlockSpec((S, W), kmap), pl.BlockSpec((S, W), kmap)] + [hbm] * len(h_in),
        out_specs=[pl.BlockSpec((2 * blk, W), qmap), pl.BlockSpec((2 * blk, W), qmap)] + h_ospecs,
        out_shape=[_sds((T, P), F32), _sds((T, P), F32)] + h_oshapes,
        scratch_shapes=h_scratch,
        compiler_params=_cparams(("arbitrary",) * 3 if hosted else ("parallel", "parallel", "arbitrary")),
    )(q, k, v, *h_in)


def flash_bwd(q, k, v, o, lse, dmix, S, *, name, blk=1024, hosted=None):
    T, P = q.shape
    blk = _pick(S, (blk, 256, 128))
    B, nq, H, W = T // S, S // blk, MLA_HEADS, HEAD_W
    off = (SSD_WIDTH + POOL_WIDTH) // W
    grid = (B, H, nq)
    wrap, h_in, h_ospecs, h_oshapes, h_scratch = _hosting(hosted, grid, 6, 3, 1)

    def body(q_ref, k_ref, v_ref, o_ref, lse_ref, do_ref, dq_ref, dk_ref, dv_ref, delta_s):
        j = pl.program_id(2)

        @pl.when(j == 0)
        def _():
            for i in range(nq):
                rows = slice(i * blk, (i + 1) * blk)
                delta_s[rows, :] = jnp.sum(do_ref[rows, :] * o_ref[rows, :], axis=1, keepdims=True)
                dq_ref[rows, :] = jnp.zeros((blk, W), F32)

        kv, vv = k_ref[...], v_ref[...]

        def step(i, carry, masked):
            dk, dv = carry
            rows = pl.ds(pl.multiple_of(i * blk, blk), blk)
            qv, do = q_ref[rows, :], do_ref[rows, :]
            p = jnp.exp(_dot_nt(qv, kv) - lse_ref[rows, 0:1])
            if masked:
                p = jnp.where(_causal_mask(0, 0, blk), p, 0.0)
            ds = p * (_dot_nt(do, vv) - delta_s[rows, :])
            dq_ref[rows, :] += _dot(ds, kv)
            return dk + _dot_tn(ds, qv), dv + _dot_tn(p, do)

        zero = jnp.zeros((blk, W), F32)
        carry = step(j, (zero, zero), True)
        dk, dv = lax.fori_loop(j + 1, nq, lambda i, c: step(i, c, False), carry)
        dk_ref[...] = dk
        dv_ref[...] = dv

    full = lambda b, h, j: (b, h)
    kmap = lambda b, h, j: (b * nq + j, h)
    hbm = pl.BlockSpec(memory_space=pl.ANY)
    return pl.pallas_call(
        wrap(body), name=name, grid=grid,
        in_specs=[pl.BlockSpec((S, W), full), pl.BlockSpec((blk, W), kmap), pl.BlockSpec((blk, W), kmap),
                  pl.BlockSpec((S, W), full), pl.BlockSpec((S, W), full),
                  pl.BlockSpec((S, W), lambda b, h, j: (b, off + h))] + [hbm] * len(h_in),
        out_specs=[pl.BlockSpec((S, W), full), pl.BlockSpec((blk, W), kmap), pl.BlockSpec((blk, W), kmap)] + h_ospecs,
        out_shape=[_sds((T, P), F32)] * 3 + h_oshapes,
        scratch_shapes=[pltpu.VMEM((S, 1), F32)] + h_scratch,
        compiler_params=_cparams(("arbitrary",) * 3 if hosted else ("parallel", "parallel", "arbitrary")),
    )(q, k, v, o, lse, dmix, *h_in)


SSD_PAIRS = SSD_HEADS // 2
PAIRS_PER_GROUP = SSD_PAIRS // SSD_GROUPS
GN = SSD_GROUPS * SSD_STATE


def _log1p_small(e):
    return jnp.where(e < 1e-3, e * (1.0 - e * (0.5 - e / 3.0)), jnp.log(1.0 + e))


def _softplus(v):
    return jnp.maximum(v, 0.0) + _log1p_small(jnp.exp(-jnp.abs(v)))


def _ssd_decay(dt_raw, dtb, alog):
    L = dt_raw.shape[0]
    pre = dt_raw + dtb
    dt = _softplus(pre)
    a = -jnp.exp(alog)
    row = lax.broadcasted_iota(jnp.int32, (L, L), 0)
    col = lax.broadcasted_iota(jnp.int32, (L, L), 1)
    tri = row >= col
    cum = _dot_hi(tri.astype(F32), dt * a)
    return pre, dt, a, tri, cum, cum.T


def _col(m, h):
    return m[:, h:h + 1]


def _pair_sel(m, k, lo):
    return jnp.where(lo, _col(m, 2 * k), _col(m, 2 * k + 1))


def _ssd_specs(S):
    L = SSD_CHUNK
    nc = S // L
    return L, nc


def ssd_fwd(proj, xc, dtb, alog, dchan, normw, S, *, name, hosted=None):
    T = proj.shape[0]
    L, nc = _ssd_specs(S)
    B, W, N = T // S, SSD_WIDTH, SSD_STATE
    wrap, h_in, h_ospecs, h_oshapes, h_scratch = _hosting(hosted, (B, nc), 9, 3, 1)

    def body(xs_ref, bs_ref, cs_ref, dt_ref, z_ref, dtb_ref, alog_ref, dch_ref, nw_ref, y_ref, ys_ref, hin_ref, st):
        @pl.when(pl.program_id(1) == 0)
        def _():
            st[...] = jnp.zeros(st.shape, F32)

        hin_ref[...] = st[...]
        _, dt, a, tri, cum, cum_t = _ssd_decay(dt_ref[...], dtb_ref[...], alog_ref[...])
        last = cum[L - 1:L, :]
        lo = lax.broadcasted_iota(jnp.int32, (1, LANE), 1) < SSD_HEAD_DIM
        for g in range(SSD_GROUPS):
            bm = bs_ref[:, g * N:(g + 1) * N]
            cm = cs_ref[:, g * N:(g + 1) * N]
            bm_t = bm.T
            gmat = _dot_nt(cm, bm)
            for kk in range(PAIRS_PER_GROUP):
                k = g * PAIRS_PER_GROUP + kk
                sl = slice(k * LANE, (k + 1) * LANE)
                xv = xs_ref[:, sl]
                xdt = xv * _pair_sel(dt, k, lo)
                cum_cols = [jnp.broadcast_to(_col(cum, h), (L, LANE)) for h in (2 * k, 2 * k + 1)]
                cum_sel = jnp.where(lo, cum_cols[0], cum_cols[1])
                last_sel = _pair_sel(last, k, lo)
                yd = []
                for j, h in enumerate((2 * k, 2 * k + 1)):
                    gam = jnp.exp(jnp.where(tri, cum_cols[j] - cum_t[h:h + 1, :], -jnp.inf))
                    yd.append(_dot(gmat * gam, xdt))
                hp = st[:, sl]
                y_off = _dot(cm, hp) * jnp.exp(cum_sel)
                y_ref[:, sl] = jnp.where(lo, yd[0], yd[1]) + y_off + xv * dch_ref[:, sl]
                zmat = xdt * jnp.exp(last_sel - cum_sel)
                st[:, sl] = hp * jnp.exp(last_sel) + _dot(bm_t, zmat)
        y = y_ref[...]
        z = z_ref[...]
        yz = y * (z * _sigmoid(z))
        ys_ref[...] = _rms_tile(yz, nw_ref[...]).astype(ys_ref.dtype)

    r = lambda b, c: b * nc + c
    vec = lambda w: pl.BlockSpec((1, w), lambda b, c: (0, 0))
    hbm = pl.BlockSpec(memory_space=pl.ANY)
    return pl.pallas_call(
        wrap(body), name=name, grid=(B, nc),
        in_specs=[pl.BlockSpec((L, W), lambda b, c: (r(b, c), 0)),
                  pl.BlockSpec((L, GN), lambda b, c: (r(b, c), W // GN)),
                  pl.BlockSpec((L, GN), lambda b, c: (r(b, c), W // GN + 1)),
                  pl.BlockSpec((L, LANE), lambda b, c: (r(b, c), PDT0 // LANE)),
                  pl.BlockSpec((L, W), lambda b, c: (r(b, c), PZ0 // W)),
                  vec(LANE), vec(LANE), vec(W), vec(W)] + [hbm] * len(h_in),
        out_specs=[pl.BlockSpec((L, W), lambda b, c: (r(b, c), 0)), pl.BlockSpec((L, W), lambda b, c: (r(b, c), 0)),
                   pl.BlockSpec((N, W), lambda b, c: (r(b, c), 0))] + h_ospecs,
        out_shape=[_sds((T, W), F32), _sds((T, W), MXU_DTYPE), _sds((T // L * N, W), F32)] + h_oshapes,
        scratch_shapes=[pltpu.VMEM((N, W), F32)] + h_scratch,
        compiler_params=_cparams(("arbitrary", "arbitrary") if hosted else ("parallel", "arbitrary")),
    )(xc, xc, xc, proj, proj, dtb, alog, dchan, normw, *h_in)


def ssd_bwd(proj, xc, ypre, hin, dmix, dtb, alog, dchan, normw, S, *, name):
    T = proj.shape[0]
    L, nc = _ssd_specs(S)
    B, W, N = T // S, SSD_WIDTH, SSD_STATE

    def body(xs_ref, bs_ref, cs_ref, dt_ref, z_ref, y_ref, hin_ref, dys_ref, dtb_ref, alog_ref, dch_ref, nw_ref,
             dxc_ref, ddt_ref, dz_ref, sm_ref, dnw_ref, dst):
        step = pl.program_id(0) * nc + pl.program_id(1)

        @pl.when(pl.program_id(1) == 0)
        def _():
            dst[...] = jnp.zeros(dst.shape, F32)

        pre, dt, a, tri, cum, cum_t = _ssd_decay(dt_ref[...], dtb_ref[...], alog_ref[...])
        last = cum[L - 1:L, :]
        e_last = jnp.exp(last)
        lane = lax.broadcasted_iota(jnp.int32, (1, LANE), 1)
        sub = lax.broadcasted_iota(jnp.int32, (LANE, 1), 0)
        lo = lane < SSD_HEAD_DIM
        is_last_row = sub == L - 1
        tri_t = (lax.broadcasted_iota(jnp.int32, (L, L), 0) <= lax.broadcasted_iota(jnp.int32, (L, L), 1))

        y, z, nw = y_ref[...], z_ref[...], nw_ref[...]
        sg = _sigmoid(z)
        gate = z * sg
        dyz, dnw = _rms_bwd_tile(y * gate, nw, dys_ref[...])
        dy_all = dyz * gate
        dz_ref[...] = (dyz * y * (sg * (1.0 + z * (1.0 - sg)))).astype(dz_ref.dtype)

        d_cum = jnp.zeros((L, LANE), F32)
        d_cum_t = jnp.zeros((LANE, L), F32)
        d_dt = jnp.zeros((L, LANE), F32)
        d_dskip = jnp.zeros((1, LANE), F32)
        for g in range(SSD_GROUPS):
            bm = bs_ref[:, g * N:(g + 1) * N]
            cm = cs_ref[:, g * N:(g + 1) * N]
            cm_t = cm.T
            gmat = _dot_nt(cm, bm)
            gmat_t = _dot_nt(bm, cm)
            d_g = jnp.zeros((L, L), F32)
            d_bm = jnp.zeros((L, N), F32)
            d_cm = jnp.zeros((L, N), F32)
            for kk in range(PAIRS_PER_GROUP):
                k = g * PAIRS_PER_GROUP + kk
                sl = slice(k * LANE, (k + 1) * LANE)
                xv = xs_ref[:, sl]
                dyv = dy_all[:, sl]
                dt_sel = _pair_sel(dt, k, lo)
                xdt = xv * dt_sel
                hp = hin_ref[:, sl]
                dh_out = dst[:, sl]
                cum_cols = [jnp.broadcast_to(_col(cum, h), (L, LANE)) for h in (2 * k, 2 * k + 1)]
                cum_sel = jnp.where(lo, cum_cols[0], cum_cols[1])
                last_sel = _pair_sel(last, k, lo)
                e_sel = jnp.exp(cum_sel)
                w_sel = jnp.exp(last_sel - cum_sel)
                e_lane = jnp.exp(last_sel)
                y_off = _dot(cm, hp) * e_sel
                zmat = xdt * w_sel
                d_z = _dot(bm, dh_out)
                d_bm = d_bm + _dot_nt(zmat, dh_out)
                d_xdt = d_z * w_sel
                dw_full = d_z * zmat
                hh = dh_out * hp
                d_r = dyv * e_sel
                d_cm = d_cm + _dot_nt(d_r, hp)
                dst[:, sl] = dh_out * e_lane + _dot(cm_t, d_r)
                dyoff_full = dyv * y_off
                for j, h in enumerate((2 * k, 2 * k + 1)):
                    mine = lo if j == 0 else jnp.logical_not(lo)
                    hot = lane == h
                    dyh = jnp.where(mine, dyv, 0.0)
                    gam = jnp.exp(jnp.where(tri, cum_cols[j] - cum_t[h:h + 1, :], -jnp.inf))
                    gam_t = jnp.exp(jnp.where(tri_t, cum_t[h:h + 1, :] - cum_cols[j], -jnp.inf))
                    mx = gmat * gam
                    d_xdt = d_xdt + _dot(gmat_t * gam_t, dyh)
                    d_mx = jnp.where(tri, _dot_nt(dyh, xdt), 0.0)
                    d_g = d_g + d_mx * gam
                    d_seg = d_mx * mx
                    row_l = jnp.sum(d_seg + jnp.where(mine, dyoff_full - dw_full, 0.0), axis=1, keepdims=True)
                    at_end = (jnp.sum(jnp.where(mine, dw_full, 0.0), keepdims=True)
                              + jnp.sum(jnp.where(mine, hh, 0.0), keepdims=True) * _col(e_last, h))
                    d_cum = d_cum + jnp.where(hot, row_l + jnp.where(is_last_row, at_end, 0.0), 0.0)
                    d_cum_t = d_cum_t - jnp.where(sub == h, jnp.sum(d_seg, axis=0, keepdims=True), 0.0)
                    d_dskip = d_dskip + jnp.where(hot, jnp.sum(jnp.where(mine, dyv * xv, 0.0), keepdims=True), 0.0)
                for j, h in enumerate((2 * k, 2 * k + 1)):
                    mine = lo if j == 0 else jnp.logical_not(lo)
                    d_dt = d_dt + jnp.where(lane == h, jnp.sum(jnp.where(mine, d_xdt * xv, 0.0), axis=1, keepdims=True), 0.0)
                dxc_ref[:, sl] = d_xdt * dt_sel + dyv * dch_ref[:, sl]
            dxc_ref[:, W + g * N:W + (g + 1) * N] = d_bm + _dot_tn(d_g, cm)
            dxc_ref[:, W + GN + g * N:W + GN + (g + 1) * N] = d_cm + _dot(d_g, bm)

        d_cum = d_cum + d_cum_t.T
        d_da = _dot_hi(jnp.logical_not(tri).astype(F32) + (lax.broadcasted_iota(jnp.int32, (L, L), 0)
                                                              == lax.broadcasted_iota(jnp.int32, (L, L), 1)).astype(F32), d_cum)
        d_dt = d_dt + d_da * a
        heads = lane < SSD_HEADS
        d_pre = jnp.where(heads, d_dt * _sigmoid(pre), 0.0)
        ddt_ref[...] = d_pre.astype(ddt_ref.dtype)
        d_alog = jnp.sum(d_da * dt, axis=0, keepdims=True) * a
        part = jnp.concatenate([jnp.where(heads, d_alog, 0.0), jnp.sum(d_pre, axis=0, keepdims=True), d_dskip,
                                jnp.zeros((5, LANE), F32)], axis=0)

        @pl.when(step == 0)
        def _():
            sm_ref[...] = part
            dnw_ref[...] = dnw

        @pl.when(step > 0)
        def _():
            sm_ref[...] += part
            dnw_ref[...] += dnw

    r = lambda b, c: b * nc + (nc - 1 - c)
    vec = lambda w: pl.BlockSpec((1, w), lambda b, c: (0, 0))
    blk = lambda w, j: pl.BlockSpec((L, w), lambda b, c: (r(b, c), j))
    return pl.pallas_call(
        body, name=name, grid=(B, nc),
        in_specs=[blk(W, 0), blk(GN, W // GN), blk(GN, W // GN + 1), blk(LANE, PDT0 // LANE), blk(W, PZ0 // W),
                  blk(W, 0), pl.BlockSpec((N, W), lambda b, c: (r(b, c), 0)), blk(W, 0),
                  vec(LANE), vec(LANE), vec(W), vec(W)],
        out_specs=[blk(SSD_CONV_CH, 0), blk(LANE, 0), blk(W, 0), pl.BlockSpec((8, LANE), lambda b, c: (0, 0)), vec(W)],
        out_shape=[_sds((T, SSD_CONV_CH), F32), _sds((T, LANE), MXU_DTYPE), _sds((T, W), MXU_DTYPE),
                   _sds((8, LANE), F32), _sds((1, W), F32)],
        scratch_shapes=[pltpu.VMEM((N, W), F32)],
        compiler_params=_cparams(("arbitrary", "arbitrary")),
    )(xc, xc, xc, proj, proj, ypre, hin, dmix, dtb, alog, dchan, normw)


def _adamw_math(w, g, m, v):
    m = ADAM_B1 * m + (1.0 - ADAM_B1) * g
    v = ADAM_B2 * v + (1.0 - ADAM_B2) * (g * g)
    m_hat = m / (1.0 - ADAM_B1 ** ADAM_STEP)
    v_hat = v / (1.0 - ADAM_B2 ** ADAM_STEP)
    delta = -ADAM_LR * (m_hat / (jnp.sqrt(v_hat) + ADAM_EPS) + ADAM_WD * w)
    return delta, m, v


def adamw_layers(w, g_layers, m, v, *, name, tr=256):
    L, A, B = w.shape
    tr = _pick(A, (tr, 192, 176, 128, 64, 32, 16, 8))
    na = A // tr
    n = len(g_layers[0])

    def body(*refs):
        w_ref, m_ref, v_ref = refs[0], refs[1 + L * n], refs[2 + L * n]
        g_ref, d_ref, nm_ref, nv_ref = refs[3 + L * n:]
        layer = pl.program_id(0)
        g = None
        for l in range(L):
            parts = refs[1 + l * n:1 + (l + 1) * n]
            gl = parts[0][...]
            for p in parts[1:]:
                gl = gl + p[...]
            g = gl if g is None else jnp.where(layer == l, gl, g)
        g_ref[...] = g
        d_ref[...], nm_ref[...], nv_ref[...] = _adamw_math(w_ref[...], g, m_ref[...], v_ref[...])

    def g_spec(l):
        return pl.BlockSpec((tr, B), lambda layer, i: (jnp.where(layer == l, i, jnp.where(layer < l, 0, na - 1)), 0))

    spec = pl.BlockSpec((None, tr, B), lambda layer, i: (layer, i, 0))
    return pl.pallas_call(
        body, name=name, grid=(L, na), in_specs=[spec] + [g_spec(l) for l in range(L) for _ in range(n)] + [spec] * 2,
        out_specs=[spec] * 4, out_shape=[_sds((L, A, B), F32)] * 4, compiler_params=_cparams(("arbitrary", "arbitrary")),
    )(w, *[p for parts in g_layers for p in parts], m, v)


def adamw_small(ws, gs, ms, vs, *, name):
    n = len(ws)

    def body(*refs):
        w_refs, g_refs, m_refs, v_refs = (refs[i * n:(i + 1) * n] for i in range(4))
        d_refs, nm_refs, nv_refs = (refs[(4 + i) * n:(5 + i) * n] for i in range(3))
        for a in range(n):
            d_refs[a][...], nm_refs[a][...], nv_refs[a][...] = _adamw_math(
                w_refs[a][...], g_refs[a][...], m_refs[a][...], v_refs[a][...])

    vm = pl.BlockSpec(memory_space=pltpu.VMEM)
    out = pl.pallas_call(
        body, name=name, in_specs=[vm] * (4 * n), out_specs=[vm] * (3 * n),
        out_shape=[_sds(w.shape, F32) for w in ws] * 3, compiler_params=pltpu.CompilerParams(vmem_limit_bytes=VMEM_LIMIT),
    )(*ws, *gs, *ms, *vs)
    return out[:n], out[n:2 * n], out[2 * n:]


def _my_place():
    return lax.axis_index("x"), lax.axis_index("y"), lax.axis_index("c")


def _other_chips(x, y):
    return [(1 - x, y), (x, 1 - y), (1 - x, 1 - y)]


def relation_of(chip, me):
    d = chip ^ me
    return jnp.where(d == 2, 0, jnp.where(d == 1, 1, jnp.where(d == 3, 2, -1)))


class Exchange(NamedTuple):
    inputs: tuple
    out_shapes: tuple
    sems: tuple
    start: Callable
    relay: Callable
    finish: Callable


def scatter_exchange(srcs):
    n = len(srcs)

    def copies(ins, outs, sems):
        x, y, c = _my_place()
        out = []
        for k, (px, py) in enumerate(_other_chips(x, y)):
            for a in range(n):
                out.append(pltpu.make_async_remote_copy(
                    src_ref=ins[a].at[2 * px + py], dst_ref=outs[a].at[k], send_sem=sems[0].at[k, a],
                    recv_sem=sems[1].at[k, a], device_id=(px, py, c), device_id_type=pl.DeviceIdType.MESH))
        return out

    def start(ins, outs, sems):
        for cp in copies(ins, outs, sems):
            cp.start()

    def finish(ins, outs, sems):
        cps = copies(ins, outs, sems)
        for cp in cps:
            cp.wait_recv()
        for cp in cps:
            cp.wait_send()

    return Exchange(tuple(srcs), tuple(_sds((3,) + s.shape[1:], s.dtype) for s in srcs),
                    (pltpu.SemaphoreType.DMA((3, n)),) * 2, start, lambda *a: None, finish)


def run_exchange(ex, *, name):
    n_in, n_out = len(ex.inputs), len(ex.out_shapes)

    def body(*refs):
        parts = refs[:n_in], refs[n_in:n_in + n_out], refs[n_in + n_out:]
        ex.start(*parts)
        ex.relay(*parts)
        ex.finish(*parts)

    hbm = pl.BlockSpec(memory_space=pl.ANY)
    return pl.pallas_call(
        body, name=name, in_specs=[hbm] * n_in, out_specs=[hbm] * n_out, out_shape=list(ex.out_shapes),
        scratch_shapes=list(ex.sems), compiler_params=pltpu.CompilerParams(has_side_effects=True),
    )(*ex.inputs)


def sibling_swap(srcs, *, name):
    n = len(srcs)

    def body(*refs):
        src_refs, out_refs, (send_sems, recv_sems) = refs[:n], refs[n:2 * n], refs[2 * n:]
        x, y, c = _my_place()
        copies = [pltpu.make_async_remote_copy(
            src_ref=src_refs[a], dst_ref=out_refs[a], send_sem=send_sems.at[a], recv_sem=recv_sems.at[a],
            device_id=(x, y, 1 - c), device_id_type=pl.DeviceIdType.MESH) for a in range(n)]
        for cp in copies:
            cp.start()
        for cp in copies:
            cp.wait_recv()
        for cp in copies:
            cp.wait_send()

    hbm = pl.BlockSpec(memory_space=pl.ANY)
    return pl.pallas_call(
        body, name=name, in_specs=[hbm] * n, out_specs=[hbm] * n, out_shape=[_sds(s.shape, s.dtype) for s in srcs],
        scratch_shapes=[pltpu.SemaphoreType.DMA((n,)), pltpu.SemaphoreType.DMA((n,))],
        compiler_params=pltpu.CompilerParams(has_side_effects=True),
    )(*srcs)


def gather_exchange(srcs):
    nch = len(srcs)
    halves = [s.shape[0] // 2 for s in srcs]
    assert all(2 * h == s.shape[0] and h % 16 == 0 for h, s in zip(halves, srcs))
    pieces = [(k, q) for k in range(3) for q in range(nch)]

    def makers(ins, outs, sems):
        ici_send, ici_recv, d2d_send, d2d_recv = sems
        x, y, c = _my_place()
        peers = _other_chips(x, y)

        def rows(core, q):
            return pl.ds(core * halves[q], halves[q])

        def ici(k, q):
            px, py = peers[k]
            return pltpu.make_async_remote_copy(
                src_ref=ins[q].at[rows(c, q)], dst_ref=outs[q].at[k, rows(c, q)], send_sem=ici_send.at[k, q],
                recv_sem=ici_recv.at[k, q], device_id=(px, py, c), device_id_type=pl.DeviceIdType.MESH)

        def d2d(k, q, core):
            return pltpu.make_async_remote_copy(
                src_ref=outs[q].at[k, rows(core, q)], dst_ref=outs[q].at[k, rows(core, q)],
                send_sem=d2d_send.at[k, q], recv_sem=d2d_recv.at[k, q], device_id=(x, y, 1 - c),
                device_id_type=pl.DeviceIdType.MESH)

        return ici, d2d, c

    def start(*refs):
        ici, _, _ = makers(*refs)
        for k, q in pieces:
            ici(k, q).start()

    def relay(*refs):
        ici, d2d, c = makers(*refs)
        for k, q in pieces:
            ici(k, q).wait_recv()
            d2d(k, q, c).start()

    def finish(*refs):
        ici, d2d, c = makers(*refs)
        for k, q in pieces:
            d2d(k, q, 1 - c).wait_recv()
        for k, q in pieces:
            ici(k, q).wait_send()
            d2d(k, q, c).wait_send()

    return Exchange(tuple(srcs), tuple(_sds((3,) + s.shape, s.dtype) for s in srcs),
                    (pltpu.SemaphoreType.DMA((3, nch)),) * 4, start, relay, finish)


def all_sum_small(vec, *, name):
    R, C = vec.shape

    def body(v_ref, out_ref, buf, send_sems, recv_sems):
        x, y, c = _my_place()
        me = 4 * x + 2 * y + c
        buf[me] = v_ref[...]
        copies = []
        for k in range(1, N_DEV):
            px, py, pc = x ^ (k >> 2), y ^ ((k >> 1) & 1), c ^ (k & 1)
            copies.append(pltpu.make_async_remote_copy(
                src_ref=v_ref, dst_ref=buf.at[me], send_sem=send_sems.at[k - 1], recv_sem=recv_sems.at[k - 1],
                device_id=(px, py, pc), device_id_type=pl.DeviceIdType.MESH))
        for cp in copies:
            cp.start()
        for k in range(1, N_DEV):
            px, py, pc = x ^ (k >> 2), y ^ ((k >> 1) & 1), c ^ (k & 1)
            pltpu.make_async_remote_copy(
                src_ref=v_ref, dst_ref=buf.at[4 * px + 2 * py + pc], send_sem=send_sems.at[k - 1],
                recv_sem=recv_sems.at[k - 1], device_id=(px, py, pc), device_id_type=pl.DeviceIdType.MESH).wait_recv()
        for cp in copies:
            cp.wait_send()
        acc = buf[0]
        for d in range(1, N_DEV):
            acc = acc + buf[d]
        out_ref[...] = acc

    return pl.pallas_call(
        body, name=name, in_specs=[pl.BlockSpec(memory_space=pltpu.VMEM)], out_specs=pl.BlockSpec(memory_space=pltpu.VMEM),
        out_shape=_sds((R, C), F32),
        scratch_shapes=[pltpu.VMEM((N_DEV, R, C), F32), pltpu.SemaphoreType.DMA((N_DEV - 1,)),
                        pltpu.SemaphoreType.DMA((N_DEV - 1,))],
        compiler_params=pltpu.CompilerParams(has_side_effects=True, vmem_limit_bytes=VMEM_LIMIT),
    )(vec)


def sum_chips(own, others, *, name, tr=512):
    R, C = own.shape
    tr = _pick(R, (tr, 384, 352, 256, 128, 64, 32, 16))

    def body(o_ref, p_ref, s_ref):
        acc = o_ref[...].astype(F32)
        for k in range(3):
            acc = acc + p_ref[k].astype(F32)
        s_ref[...] = acc

    return pl.pallas_call(
        body, name=name, grid=(R // tr,),
        in_specs=[pl.BlockSpec((tr, C), lambda i: (i, 0)), pl.BlockSpec((3, tr, C), lambda i: (0, i, 0))],
        out_specs=pl.BlockSpec((tr, C), lambda i: (i, 0)), out_shape=_sds((R, C), F32),
        compiler_params=_cparams(("parallel",)),
    )(own, others)


WEIGHTS = ['attn_norm', 'w_in', 'ssd_conv_w', 'ssd_conv_b', 'ssd_dt_bias', 'ssd_a_log', 'ssd_d', 'ssd_norm', 'pool_w',
           'pool_scale', 'mla_q_norm', 'mla_w_uq', 'mla_kv_norm', 'mla_w_ukv', 'w_out', 'ffn_norm', 'ffn_w_up',
           'ffn_conv_w', 'ffn_conv_b', 'ffn_w_down', 'final_norm']
BIG = {'w_in': 2, 'mla_w_uq': 2, 'mla_w_ukv': 2, 'w_out': 1, 'ffn_w_up': 2, 'ffn_w_down': 1}
CONV_SHARDED = ('ssd_conv_w', 'ffn_conv_w')


def _zeros_cols(w, n):
    return jnp.zeros((w.shape[0], n), w.dtype)


def _w_in_to_padded(w):
    return jnp.concatenate([w[:, 0:2560], w[:, 2576:3088], w[:, 3088:3472], w[:, 2560:2576], _zeros_cols(w, 112),
                            w[:, 3472:3728], _zeros_cols(w, 64), w[:, 3728:3760], _zeros_cols(w, 32 + 128)], axis=1)


def _w_in_from_padded(g):
    return jnp.concatenate([g[:, 0:2560], g[:, PDT0:PDT0 + SSD_HEADS], g[:, PU0:PU0 + POOL_WIDTH],
                            g[:, PCQ0:PCQ0 + MLA_Q_RANK], g[:, PCKV0:PCKV0 + MLA_KV_RANK],
                            g[:, PKPE0 + ROPE0:PKPE0 + ROPE0 + MLA_ROPE]], axis=1)


def _w_uq_to_padded(w):
    r = w.reshape(MLA_Q_RANK, MLA_HEADS, MLA_QK)
    return jnp.pad(r, ((0, 0), (0, 0), (0, HEAD_W - MLA_QK))).reshape(MLA_Q_RANK, MLA_PAD)


def _w_uq_from_padded(g):
    return g.reshape(MLA_Q_RANK, MLA_HEADS, HEAD_W)[:, :, :MLA_QK].reshape(MLA_Q_RANK, MLA_HEADS * MLA_QK)


def _w_ukv_to_padded(w):
    r = w.reshape(MLA_KV_RANK, MLA_HEADS, MLA_NOPE + MLA_V)
    pad = lambda t: jnp.pad(t, ((0, 0), (0, 0), (0, HEAD_W - t.shape[2]))).reshape(MLA_KV_RANK, MLA_PAD)
    return jnp.concatenate([pad(r[:, :, :MLA_NOPE]), pad(r[:, :, MLA_NOPE:])], axis=1)


def _w_ukv_from_padded(g):
    kk = g[:, :MLA_PAD].reshape(MLA_KV_RANK, MLA_HEADS, HEAD_W)[:, :, :MLA_NOPE]
    vv = g[:, MLA_PAD:].reshape(MLA_KV_RANK, MLA_HEADS, HEAD_W)[:, :, :MLA_V]
    return jnp.concatenate([kk, vv], axis=2).reshape(MLA_KV_RANK, MLA_HEADS * (MLA_NOPE + MLA_V))


def _w_out_to_padded(w):
    att = w[SSD_WIDTH + POOL_WIDTH:].reshape(MLA_HEADS, MLA_V, D_MODEL)
    att = jnp.pad(att, ((0, 0), (0, HEAD_W - MLA_V), (0, 0))).reshape(MLA_PAD, D_MODEL)
    return jnp.concatenate([w[:SSD_WIDTH + POOL_WIDTH], att], axis=0)


def _w_out_from_padded(g):
    att = g[SSD_WIDTH + POOL_WIDTH:].reshape(MLA_HEADS, HEAD_W, D_MODEL)[:, :MLA_V].reshape(MLA_WIDTH, D_MODEL)
    return jnp.concatenate([g[:SSD_WIDTH + POOL_WIDTH], att], axis=0)


def _pad_lanes(v, n=LANE):
    return jnp.pad(v.reshape(1, -1), ((0, 0), (0, n - v.size)))


def _pack_rows(parts, cols, dtype, row_multiple=16):
    flat = jnp.concatenate([p.astype(dtype).reshape(-1) for p in parts])
    rows = -(-flat.size // (cols * row_multiple)) * row_multiple
    return jnp.pad(flat, (0, rows * cols - flat.size)).reshape(rows, cols)


def _unpack_rows(packed, shapes):
    flat = packed.reshape(-1)
    out, at = [], 0
    for s in shapes:
        n = math.prod(s)
        out.append(flat[at:at + n].reshape(s))
        at += n
    return out


def _split_for_chips(g, axis):
    a, b = g.shape
    if axis == 0:
        return g.reshape(N_CHIPS, a // N_CHIPS, b)
    return g.reshape(a, N_CHIPS, b // N_CHIPS).transpose(1, 0, 2)


_MATMUL_OPERANDS = {'w_in': ('w_in_p', _w_in_to_padded), 'mla_w_uq': ('w_uq_p', _w_uq_to_padded),
                    'mla_w_ukv': ('w_ukv_p', _w_ukv_to_padded), 'w_out': ('w_out_p', _w_out_to_padded),
                    'ffn_w_up': ('w_up', lambda a: a), 'ffn_w_down': ('w_down', lambda a: a)}


def _matmul_weights(full):
    return {_MATMUL_OPERANDS[k][0]: _MATMUL_OPERANDS[k][1](a) for k, a in full.items()}


def _layer_weights(full, small, l):
    w = _matmul_weights(full)
    for k in ('attn_norm', 'ssd_conv_w', 'ssd_conv_b', 'ssd_norm', 'pool_w', 'pool_scale', 'mla_q_norm', 'mla_kv_norm',
              'ffn_norm', 'ffn_conv_w', 'ffn_conv_b'):
        w[k] = small[k][l]
    w['dtb'] = _pad_lanes(small['ssd_dt_bias'][l])
    w['alog'] = _pad_lanes(small['ssd_a_log'][l])
    w['dchan'] = jnp.repeat(small['ssd_d'][l], SSD_HEAD_DIM).reshape(1, SSD_WIDTH)
    w['ssd_norm'] = w['ssd_norm'].reshape(1, SSD_WIDTH)
    return w


def _layer_fwd(x, pos, invf, w, S, l, hosted=None, scan_hosted=None, late_weights=None, post_weights=None):
    n = lambda s: f"{s}_l{l}"
    h1 = rmsnorm_fwd(x, w['attn_norm'], name=n("attn_norm"))
    proj = matmul(h1, w['w_in_p'], name=n("w_in"))
    xc = ssd_conv_fwd(proj, w['ssd_conv_w'], w['ssd_conv_b'], S, name=n("ssd_conv"))
    ypre, yssd, hin, *arrived = ssd_fwd(proj, xc, w['dtb'], w['alog'], w['dchan'], w['ssd_norm'], S, name=n("ssd_scan"),
                                        hosted=scan_hosted)
    if late_weights:
        w = {**w, **late_weights(arrived)}
    ypool, pooled = pool_fwd(proj, w['pool_w'], w['pool_scale'], S, name=n("pool"))
    q, k, v, cqn, ckvn = mla_prep_fwd(proj, pos, invf, w['mla_q_norm'], w['w_uq_p'], w['mla_kv_norm'], w['w_ukv_p'],
                                      name=n("mla_prep"))
    o, lse, *exchanged = flash_fwd(q, k, v, S, name=n("attention"), hosted=hosted)
    if post_weights:
        w = {**w, **post_weights(exchanged)}
    mix = jnp.concatenate([yssd, ypool, o.astype(MXU_DTYPE)], axis=1)
    x2 = matmul(mix, w['w_out_p'], res=x, name=n("w_out"))
    h2 = rmsnorm_fwd(x2, w['ffn_norm'], name=n("ffn_norm"))
    up = matmul(h2, w['w_up'], name=n("ffn_up"))
    act = ffn_conv_gate_fwd(up, w['ffn_conv_w'], w['ffn_conv_b'], S, name=n("ffn_conv_gate"))
    x3 = matmul(act, w['w_down'], res=x2, name=n("ffn_down"))
    saved = dict(x=x, h1=h1, proj=proj, xc=xc, ypre=ypre, hin=hin, pooled=pooled, q=q, k=k, v=v, cqn=cqn, ckvn=ckvn,
                 o=o, lse=lse, mix=mix, x2=x2, h2=h2, up=up, act=act)
    return x3, saved, w, exchanged


def _layer_bwd(dx3, pos, invf, w, s, S, l, host=None, late_host=None):
    n = lambda t: f"{t}_l{l}"
    g = {}
    dact = matmul(dx3, w['w_down'], nt=True, name=n("d_ffn_down"))
    g['ffn_w_down'] = matmul_tn(s['act'], dx3, name=n("g_ffn_down"))
    dup_g, dup_v, st = ffn_conv_gate_bwd(s['up'], w['ffn_conv_w'], w['ffn_conv_b'], dact, S, name=n("d_ffn_conv_gate"))
    g['ffn_conv_w'], g['ffn_conv_b'] = st[:FFN_CONV], st[FFN_CONV]
    dh2 = matmul(dup_g, w['w_up'], nt=True, kblock=0, name=n("d_ffn_up_g"))
    dh2 = matmul(dup_v, w['w_up'], nt=True, kblock=1, res=dh2, name=n("d_ffn_up_v"))
    g['ffn_w_up'] = jnp.concatenate([matmul_tn(s['h2'], dup_g, name=n("g_ffn_up_g")),
                                     matmul_tn(s['h2'], dup_v, name=n("g_ffn_up_v"))], axis=1)
    dx2, gn = rmsnorm_bwd(s['x2'], w['ffn_norm'], dh2, dx3, name=n("d_ffn_norm"))
    g['ffn_norm'] = gn[0]
    dmix = matmul(dx2, w['w_out_p'], nt=True, name=n("d_w_out"))
    g['w_out'] = _w_out_from_padded(matmul_tn(s['mix'], dx2, name=n("g_w_out")))
    dxc, ddt, dz, sm, gsn = ssd_bwd(s['proj'], s['xc'], s['ypre'], s['hin'], dmix, w['dtb'], w['alog'], w['dchan'],
                                    w['ssd_norm'], S, name=n("d_ssd_scan"))
    g['ssd_a_log'], g['ssd_dt_bias'], g['ssd_d'] = sm[0, :SSD_HEADS], sm[1, :SSD_HEADS], sm[2, :SSD_HEADS]
    g['ssd_norm'] = gsn[0]
    dxbc, st = ssd_conv_bwd(s['proj'], w['ssd_conv_w'], w['ssd_conv_b'], dxc, S, name=n("d_ssd_conv"))
    g['ssd_conv_w'], g['ssd_conv_b'] = st[:SSD_CONV], st[SSD_CONV]
    du, g['pool_w'], gps = pool_bwd(dmix, s['pooled'], w['pool_w'], w['pool_scale'], S, name=n("d_pool"))
    g['pool_scale'] = gps[0]
    dq, dk, dv, *exchanged = flash_bwd(s['q'], s['k'], s['v'], s['o'], s['lse'], dmix, S, name=n("d_attention"),
                                       hosted=host(g) if host else None)
    dqp, dkvp, dcq, dckv, dkpe, gqn, gkn = mla_prep_bwd(s['proj'], pos, invf, w['mla_q_norm'], w['w_uq_p'],
                                                        w['mla_kv_norm'], w['w_ukv_p'], dq, dk, dv, name=n("d_mla_prep"))
    g['mla_q_norm'], g['mla_kv_norm'] = gqn[0], gkn[0]
    g['mla_w_uq'] = _w_uq_from_padded(matmul_tn(s['cqn'], dqp, name=n("g_w_uq")))
    g['mla_w_ukv'] = _w_ukv_from_padded(matmul_tn(s['ckvn'], dkvp, name=n("g_w_ukv")))
    dproj = jnp.concatenate([dz, dxbc, du, dcq, ddt, dckv, dkpe, jnp.zeros_like(dkpe)], axis=1)
    g['w_in'] = _w_in_from_padded(matmul_tn(s['h1'], dproj, name=n("g_w_in")))
    dh1 = matmul(dproj, w['w_in_p'], nt=True, name=n("d_w_in"), hosted=late_host(g) if late_host else None)
    dh1, late_exchanged = (dh1[0], dh1[1:]) if late_host else (dh1, [])
    dx, gn = rmsnorm_bwd(s['x'], w['attn_norm'], dh1, dx2, name=n("d_attn_norm"))
    g['attn_norm'] = gn[0]
    return dx, g, exchanged, late_exchanged


def _rope_inputs(positions):
    pos = positions.reshape(-1, 1).astype(F32)
    inv_freq = ROPE_THETA ** (-jnp.arange(0, MLA_ROPE, 2, dtype=F32) / MLA_ROPE)
    invf = jnp.concatenate([jnp.zeros((ROPE0,), F32), inv_freq, inv_freq,
                            jnp.zeros((HEAD_W - ROPE0 - MLA_ROPE,), F32)]).reshape(1, HEAD_W)
    return pos, invf


EARLY_GRADS = ('w_out', 'ffn_w_up', 'ffn_w_down')


def kernel(x, positions, attn_norm, w_in, ssd_conv_w, ssd_conv_b, ssd_dt_bias, ssd_a_log, ssd_d, ssd_norm, pool_w, pool_scale, mla_q_norm, mla_w_uq, mla_kv_norm, mla_w_ukv, w_out, ffn_norm, ffn_w_up, ffn_conv_w, ffn_conv_b, ffn_w_down, final_norm, loss_target, m_attn_norm, m_w_in, m_ssd_conv_w, m_ssd_conv_b, m_ssd_dt_bias, m_ssd_a_log, m_ssd_d, m_ssd_norm, m_pool_w, m_pool_scale, m_mla_q_norm, m_mla_w_uq, m_mla_kv_norm, m_mla_w_ukv, m_w_out, m_ffn_norm, m_ffn_w_up, m_ffn_conv_w, m_ffn_conv_b, m_ffn_w_down, m_final_norm, v_attn_norm, v_w_in, v_ssd_conv_w, v_ssd_conv_b, v_ssd_dt_bias, v_ssd_a_log, v_ssd_d, v_ssd_norm, v_pool_w, v_pool_scale, v_mla_q_norm, v_mla_w_uq, v_mla_kv_norm, v_mla_w_ukv, v_w_out, v_ffn_norm, v_ffn_w_up, v_ffn_conv_w, v_ffn_conv_b, v_ffn_w_down, v_final_norm):
    wv = dict(zip(WEIGHTS, (attn_norm, w_in, ssd_conv_w, ssd_conv_b, ssd_dt_bias, ssd_a_log, ssd_d, ssd_norm, pool_w,
                            pool_scale, mla_q_norm, mla_w_uq, mla_kv_norm, mla_w_ukv, w_out, ffn_norm, ffn_w_up,
                            ffn_conv_w, ffn_conv_b, ffn_w_down, final_norm)))
    mv = dict(zip(WEIGHTS, (m_attn_norm, m_w_in, m_ssd_conv_w, m_ssd_conv_b, m_ssd_dt_bias, m_ssd_a_log, m_ssd_d,
                            m_ssd_norm, m_pool_w, m_pool_scale, m_mla_q_norm, m_mla_w_uq, m_mla_kv_norm, m_mla_w_ukv,
                            m_w_out, m_ffn_norm, m_ffn_w_up, m_ffn_conv_w, m_ffn_conv_b, m_ffn_w_down, m_final_norm)))
    vv = dict(zip(WEIGHTS, (v_attn_norm, v_w_in, v_ssd_conv_w, v_ssd_conv_b, v_ssd_dt_bias, v_ssd_a_log, v_ssd_d,
                            v_ssd_norm, v_pool_w, v_pool_scale, v_mla_q_norm, v_mla_w_uq, v_mla_kv_norm, v_mla_w_ukv,
                            v_w_out, v_ffn_norm, v_ffn_w_up, v_ffn_conv_w, v_ffn_conv_b, v_ffn_w_down, v_final_norm)))
    Bl, S, D = x.shape
    chip = 2 * lax.axis_index("x") + lax.axis_index("y")
    core = lax.axis_index("c")

    big_names = list(BIG)
    first_names = ['w_in']
    scan_names = ['mla_w_uq', 'mla_w_ukv']
    post_names = [k for k in big_names if k not in first_names + scan_names]

    def shards(l, names):
        return [wv[k][l].astype(MXU_DTYPE) for k in names]

    def whole_weights(names, own, others):
        rel = [relation_of(j, chip) for j in range(N_CHIPS)]
        return {k: jnp.concatenate(
            [jnp.where(r < 0, mine, jnp.where(r == 0, theirs[0], jnp.where(r == 1, theirs[1], theirs[2]))) for r in rel],
            axis=BIG[k] - 1) for k, mine, theirs in zip(names, own, others)}

    first_others = run_exchange(gather_exchange(shards(0, first_names)), name="gather_w_in_l0")
    placed = []
    for k in CONV_SHARDED:
        sh = wv[k]
        whole = jnp.zeros(sh.shape[:-1] + (sh.shape[-1] * N_CHIPS,), F32)
        whole = lax.dynamic_update_slice_in_dim(whole, sh, chip * sh.shape[-1], axis=sh.ndim - 1)
        placed.append(jnp.where(core == 1, whole, 0.0))
    conv_full = _unpack_rows(all_sum_small(_pack_rows(placed, LANE, F32), name="gather_conv_weights"),
                             [p.shape for p in placed])
    small = {k: wv[k] for k in WEIGHTS if k not in BIG}
    small.update(dict(zip(CONV_SHARDED, conv_full)))

    T = Bl * S
    pos, invf = _rope_inputs(positions)
    group_a = [(k, 1) for k in big_names] + [(k, 0) for k in EARLY_GRADS]
    group_b = [(k, 0) for k in big_names if k not in EARLY_GRADS]

    def scatter_of(group, layer_grads):
        send = [_split_for_chips(layer_grads[l][k], BIG[k] - 1) for k, l in group]
        return send, scatter_exchange(send)

    layer_grads = [None] * DEPTH
    sent = {}
    w0 = _layer_weights(whole_weights(first_names, shards(0, first_names), first_others), small, 0)
    h, saved0, w0, arrived = _layer_fwd(
        x.reshape(T, D), pos, invf, w0, S, 0,
        scan_hosted=gather_exchange(shards(0, scan_names)),
        late_weights=lambda got: _matmul_weights(whole_weights(scan_names, shards(0, scan_names), got)),
        hosted=gather_exchange(shards(0, post_names) + shards(1, big_names)),
        post_weights=lambda got: _matmul_weights(whole_weights(post_names, shards(0, post_names), got[:len(post_names)])))
    others1 = arrived[len(post_names):]
    w1 = _layer_weights(whole_weights(big_names, shards(1, big_names), others1), small, 1)
    h, saved1, w1, _ = _layer_fwd(h, pos, invf, w1, S, 1)
    loss, dh, g_final_norm = final_loss(h, small['final_norm'], loss_target.reshape(T, D))
    dh, layer_grads[1], _, _ = _layer_bwd(dh, pos, invf, w1, saved1, S, 1)

    def host_a(early):
        layer_grads[0] = early
        sent['a'], ex = scatter_of(group_a, layer_grads)
        return ex

    def host_b(_):
        sent['b'], ex = scatter_of(group_b, layer_grads)
        return ex

    dx, layer_grads[0], others_a, others_b = _layer_bwd(dh, pos, invf, w0, saved0, S, 0, host=host_a, late_host=host_b)
    small_names = [k for k in WEIGHTS if k not in BIG]
    grads = {k: jnp.stack([layer_grads[l][k] for l in range(DEPTH)]) for k in small_names if k != 'final_norm'}
    grads['final_norm'] = g_final_norm[0]

    pieces = [{}, {}]
    for tag, group, others in (('a', group_a, others_a), ('b', group_b, others_b)):
        mine = [sum_chips(lax.dynamic_index_in_dim(s, chip, 0, keepdims=False), o, name=f"sum_chips_{k}_l{l}")
                for s, o, (k, l) in zip(sent[tag], others, group)]
        theirs = sibling_swap(mine, name=f"swap_core_sums_{tag}")
        pieces[0].update(dict(zip(group, mine)))
        pieces[1].update(dict(zip(group, theirs)))
    small_sum = all_sum_small(_pack_rows([grads[k] for k in small_names] + [loss[0, :1]], LANE, F32), name="sum_small_grads")
    summed = _unpack_rows(small_sum, [grads[k].shape for k in small_names] + [(1,)])
    loss_total = summed[-1].reshape(())
    g_small = dict(zip(small_names, summed[:-1]))
    for k in CONV_SHARDED:
        n = wv[k].shape[-1]
        g_small[k] = lax.dynamic_slice_in_dim(g_small[k], chip * n, n, axis=g_small[k].ndim - 1)

    out_g, out_d, out_m, out_v = {}, {}, {}, {}
    for k in big_names:
        g_layers = [[pieces[0][(k, l)], pieces[1][(k, l)]] for l in range(DEPTH)]
        out_g[k], out_d[k], out_m[k], out_v[k] = adamw_layers(wv[k], g_layers, mv[k], vv[k], name=f"adamw_{k}")
    at_least_2d = lambda a: a.reshape(1, -1) if a.ndim == 1 else a
    res = adamw_small(*[[at_least_2d(d[k]) for k in small_names] for d in (wv, g_small, mv, vv)], name="adamw_small")
    out_g.update(g_small)
    for dst, r in zip((out_d, out_m, out_v), res):
        dst.update({k: a.reshape(wv[k].shape) for k, a in zip(small_names, r)})
    return (loss_total, dx.reshape(Bl, S, D), *[out_g[k] for k in WEIGHTS], *[out_d[k] for k in WEIGHTS],
            *[out_m[k] for k in WEIGHTS], *[out_v[k] for k in WEIGHTS])
```

```python
import functools
import math
from typing import Callable, NamedTuple

import jax
import jax.numpy as jnp
from jax import lax
from jax.experimental import pallas as pl
from jax.experimental.pallas import tpu as pltpu

F32 = jnp.float32
MXU_DTYPE = jnp.bfloat16
HI = lax.Precision.HIGHEST

D_MODEL = 1024
DEPTH = 2
EPS = 1e-6
SSD_HEADS = 16
SSD_HEAD_DIM = 64
SSD_WIDTH = 1024
SSD_GROUPS = 2
SSD_STATE = 128
SSD_CONV = 4
SSD_CHUNK = 128
SSD_CONV_CH = 1536
POOL_GROUPS = 4
POOL_GROUP_DIM = 128
POOL_WIDTH = 512
POOL_WINDOWS = (2, 4, 8, 16)
MLA_HEADS = 8
MLA_Q_RANK = 384
MLA_KV_RANK = 256
MLA_NOPE = 64
MLA_ROPE = 32
MLA_V = 64
MLA_QK = 96
MLA_WIDTH = 512
ROPE_THETA = 10000.0
MIX_WIDTH = 2048
IN_COLS = 3760
D_FF = 2816
FFN_CONV = 3
ADAM_LR = 0.001
ADAM_B1 = 0.9
ADAM_B2 = 0.999
ADAM_EPS = 1e-08
ADAM_WD = 0.01
ADAM_STEP = 10

LANE = 128
HALO = 8
POOL_HALO = 16
PZ0 = 0
PXBC0 = 1024
PU0 = 2560
PCQ0 = 3072
PDT0 = 3456
PCKV0 = 3584
PKPE0 = 3840
PROJ_W = 4096
HEAD_W = 128
MLA_PAD = MLA_HEADS * HEAD_W
MIXP = SSD_WIDTH + POOL_WIDTH + MLA_PAD
N_CHIPS = 4
N_DEV = 8
VMEM_LIMIT = 56 * 1024 * 1024
MATMUL_VMEM_BUDGET = 40 * 1024 * 1024


def _cparams(dims, vmem=None):
    return pltpu.CompilerParams(dimension_semantics=dims, vmem_limit_bytes=vmem or VMEM_LIMIT)


def _sds(shape, dtype):
    return jax.ShapeDtypeStruct(tuple(shape), dtype)


def _mx(v):
    return v.astype(MXU_DTYPE)


def _dot(a, b):
    return jnp.dot(_mx(a), _mx(b), preferred_element_type=F32)


def _dot_nt(a, b):
    return lax.dot_general(_mx(a), _mx(b), (((1,), (1,)), ((), ())), preferred_element_type=F32)


def _dot_tn(a, b):
    return lax.dot_general(_mx(a), _mx(b), (((0,), (0,)), ((), ())), preferred_element_type=F32)


def _dot_hi(a, b):
    return jnp.dot(a, b, preferred_element_type=F32, precision=HI)


def _sigmoid(v):
    return 1.0 / (1.0 + jnp.exp(-v))


def _pick(n, prefs):
    for p in prefs:
        if n % p == 0:
            return p
    return n


def matmul(a, b, *, res=None, out_dtype=F32, name, nt=False, kblock=0, tm=None, tn=None, hosted=None):
    M, K = a.shape
    N = b.shape[0] if nt else b.shape[1]
    assert (b.shape[1] % K == 0) if nt else (K == b.shape[0] and kblock == 0)
    tn = tn or _pick(N, (1024, 1408, 1280, 512, 256, 128))
    a_bytes, o_bytes = jnp.dtype(a.dtype).itemsize, jnp.dtype(out_dtype).itemsize
    fits = lambda t: 2 * (t * K * a_bytes + K * tn * 2 + t * tn * (o_bytes + (4 if res is not None else 0))) <= MATMUL_VMEM_BUDGET
    tm = tm or next(t for t in (2048, 1024, 512, 256, 128) if M % t == 0 and (fits(t) or t == 128))
    grid = (M // tm, N // tn)
    wrap, h_in, h_ospecs, h_oshapes, h_scratch = _hosting(hosted, grid, 2 if res is None else 3, 1, 0)

    def body(*refs):
        a_ref, b_ref = refs[:2]
        o_ref = refs[-1]
        out = (_dot_nt if nt else _dot)(a_ref[...], b_ref[...])
        if res is not None:
            out = out + refs[2][...]
        o_ref[...] = out.astype(out_dtype)

    b_spec = pl.BlockSpec((tn, K), lambda i, j: (j, kblock)) if nt else pl.BlockSpec((K, tn), lambda i, j: (0, j))
    in_specs = [pl.BlockSpec((tm, K), lambda i, j: (i, 0)), b_spec]
    args = [a, b]
    if res is not None:
        in_specs.append(pl.BlockSpec((tm, tn), lambda i, j: (i, j)))
        args.append(res)
    out = pl.pallas_call(
        wrap(body), name=name, grid=grid, in_specs=in_specs + [pl.BlockSpec(memory_space=pl.ANY)] * len(h_in),
        out_specs=[pl.BlockSpec((tm, tn), lambda i, j: (i, j))] + h_ospecs, out_shape=[_sds((M, N), out_dtype)] + h_oshapes,
        scratch_shapes=h_scratch,
        compiler_params=_cparams(("arbitrary", "arbitrary") if hosted else ("parallel", "parallel")),
    )(*args, *h_in)
    return out if hosted else out[0]


def matmul_tn(a, g, *, name, tm=None, tn=None, tk=None):
    T, M = a.shape
    T2, N = g.shape
    assert T == T2
    tm = tm or _pick(M, (1408, 1280, 1024, 512, 384, 256, 128))
    tn = tn or _pick(N, (1024, 1408, 512, 256, 128))
    tk = tk or _pick(T, (1024, 512, 256, 128))
    nk = T // tk

    def body(a_ref, g_ref, o_ref, acc):
        k = pl.program_id(2)
        part = _dot_tn(a_ref[...], g_ref[...])

        @pl.when(k == 0)
        def _():
            acc[...] = part

        @pl.when(k > 0)
        def _():
            acc[...] += part

        @pl.when(k == nk - 1)
        def _():
            o_ref[...] = acc[...].astype(o_ref.dtype)

    return pl.pallas_call(
        body, name=name, grid=(M // tm, N // tn, nk),
        in_specs=[pl.BlockSpec((tk, tm), lambda i, j, k: (k, i)), pl.BlockSpec((tk, tn), lambda i, j, k: (k, j))],
        out_specs=pl.BlockSpec((tm, tn), lambda i, j, k: (i, j)), out_shape=_sds((M, N), MXU_DTYPE),
        scratch_shapes=[pltpu.VMEM((tm, tn), F32)],
        compiler_params=_cparams(("parallel", "parallel", "arbitrary")),
    )(a, g)


def rmsnorm_fwd(x, gamma, *, name, tm=1024, hosted=None):
    T, D = x.shape
    tm = _pick(T, (tm, 256, 128))
    grid = (T // tm,)
    wrap, h_in, h_ospecs, h_oshapes, h_scratch = _hosting(hosted, grid, 2, 1, 0)

    def body(x_ref, g_ref, o_ref):
        xv = x_ref[...]
        r = lax.rsqrt(jnp.mean(xv * xv, axis=-1, keepdims=True) + EPS)
        o_ref[...] = ((xv * r) * g_ref[...]).astype(MXU_DTYPE)

    out = pl.pallas_call(
        wrap(body), name=name, grid=grid,
        in_specs=[pl.BlockSpec((tm, D), lambda i: (i, 0)), pl.BlockSpec((1, D), lambda i: (0, 0))]
        + [pl.BlockSpec(memory_space=pl.ANY)] * len(h_in),
        out_specs=[pl.BlockSpec((tm, D), lambda i: (i, 0))] + h_ospecs, out_shape=[_sds((T, D), MXU_DTYPE)] + h_oshapes,
        scratch_shapes=h_scratch, compiler_params=_cparams(("arbitrary",) if hosted else ("parallel",)),
    )(x, gamma.reshape(1, D), *h_in)
    return out if hosted else out[0]


def _rms_bwd_tile(xv, gamma, dh):
    r = lax.rsqrt(jnp.mean(xv * xv, axis=-1, keepdims=True) + EPS)
    xh = xv * r
    dg = jnp.sum(dh * xh, axis=0, keepdims=True)
    dn = dh * gamma
    dx = r * (dn - xh * jnp.mean(dn * xh, axis=-1, keepdims=True))
    return dx, dg


def rmsnorm_bwd(x, gamma, dh, dres, *, name, tm=512):
    T, D = x.shape
    tm = _pick(T, (tm, 256, 128))

    def body(x_ref, g_ref, dh_ref, dr_ref, dx_ref, dg_ref):
        dx, dg = _rms_bwd_tile(x_ref[...], g_ref[...], dh_ref[...].astype(F32))
        dx_ref[...] = dx + dr_ref[...]

        @pl.when(pl.program_id(0) == 0)
        def _():
            dg_ref[...] = dg

        @pl.when(pl.program_id(0) > 0)
        def _():
            dg_ref[...] += dg

    row = pl.BlockSpec((tm, D), lambda i: (i, 0))
    vec = pl.BlockSpec((1, D), lambda i: (0, 0))
    return pl.pallas_call(
        body, name=name, grid=(T // tm,), in_specs=[row, vec, row, row], out_specs=[row, vec],
        out_shape=[_sds((T, D), F32), _sds((1, D), F32)], compiler_params=_cparams(("arbitrary",)),
    )(x, gamma.reshape(1, D), dh, dres)


def final_loss(x, gamma, target, *, name="final_loss", tm=512):
    T, D = x.shape
    tm = _pick(T, (tm, 256, 128))

    def body(x_ref, g_ref, t_ref, l_ref, dx_ref, dg_ref):
        xv = x_ref[...]
        gam = g_ref[...]
        r = lax.rsqrt(jnp.mean(xv * xv, axis=-1, keepdims=True) + EPS)
        y = (xv * r) * gam
        err = y - t_ref[...]
        part = 0.5 * jnp.sum(jnp.sum(err * err, axis=-1, keepdims=True) / D, axis=0, keepdims=True)
        dx, dg = _rms_bwd_tile(xv, gam, err / D)
        dx_ref[...] = dx

        @pl.when(pl.program_id(0) == 0)
        def _():
            dg_ref[...] = dg
            l_ref[...] = jnp.broadcast_to(part, l_ref.shape)

        @pl.when(pl.program_id(0) > 0)
        def _():
            dg_ref[...] += dg
            l_ref[...] += jnp.broadcast_to(part, l_ref.shape)

    row = pl.BlockSpec((tm, D), lambda i: (i, 0))
    vec = pl.BlockSpec((1, D), lambda i: (0, 0))
    return pl.pallas_call(
        body, name=name, grid=(T // tm,), in_specs=[row, vec, row],
        out_specs=[pl.BlockSpec((1, LANE), lambda i: (0, 0)), row, vec],
        out_shape=[_sds((1, LANE), F32), _sds((T, D), F32), _sds((1, D), F32)],
        compiler_params=_cparams(("arbitrary",)),
    )(x, gamma.reshape(1, D), target)


def _halo_prev(ts):
    return lambda i, j, off=0: (jnp.maximum(i * (ts // HALO) - 1, 0), j + off)


def _cat_prev(cur, halo, first):
    return jnp.concatenate([jnp.where(first, 0.0, halo), cur], axis=0)


def _cat_next(cur, halo, last):
    return jnp.concatenate([cur, jnp.where(last, 0.0, halo)], axis=0)


def _delayed(cat, r):
    if r == 0:
        return cat[HALO:]
    return pltpu.roll(cat, r, axis=0)[HALO:]


def _advanced(cat, r):
    n = cat.shape[0]
    if r == 0:
        return cat[:n - HALO]
    return pltpu.roll(cat, n - r, axis=0)[:n - HALO]


def _conv_pre(cat, w, b, K):
    acc = _delayed(cat, K - 1) * w[0:1, :] + b
    for k in range(1, K):
        acc = acc + _delayed(cat, K - 1 - k) * w[k:k + 1, :]
    return acc


def _pad_rows8(w):
    return jnp.pad(w, ((0, 8 - w.shape[0]), (0, 0)))


def ssd_conv_fwd(proj, w, b, S, *, name, ts=1024, tc=512):
    T = proj.shape[0]
    C, K = SSD_CONV_CH, SSD_CONV
    ts = _pick(S, (ts, 256, 128))
    off = PXBC0 // tc
    ns = S // ts

    def body(x_ref, h_ref, w_ref, b_ref, o_ref):
        first = (pl.program_id(0) % ns) == 0
        pre = _conv_pre(_cat_prev(x_ref[...], h_ref[...], first), w_ref[...], b_ref[...], K)
        o_ref[...] = pre * _sigmoid(pre)

    return pl.pallas_call(
        body, name=name, grid=(T // ts, C // tc),
        in_specs=[pl.BlockSpec((ts, tc), lambda i, j: (i, j + off)),
                  pl.BlockSpec((HALO, tc), functools.partial(_halo_prev(ts), off=off)),
                  pl.BlockSpec((8, tc), lambda i, j: (0, j)), pl.BlockSpec((1, tc), lambda i, j: (0, j))],
        out_specs=pl.BlockSpec((ts, tc), lambda i, j: (i, j)), out_shape=_sds((T, C), F32),
        compiler_params=_cparams(("parallel", "parallel")),
    )(proj, proj, _pad_rows8(w), b.reshape(1, C))


def _conv_stats(dpre, cat, K, ts):
    rows = [jnp.sum(dpre[:ts] * _delayed(cat, K - 1 - k)[:ts], axis=0, keepdims=True) for k in range(K)]
    rows.append(jnp.sum(dpre[:ts], axis=0, keepdims=True))
    rows.append(jnp.zeros((8 - len(rows), dpre.shape[1]), F32))
    return jnp.concatenate(rows, axis=0)


def _conv_transposed(dpre, wv, K):
    acc = _advanced(dpre, K - 1) * wv[0:1, :]
    for k in range(1, K):
        acc = acc + _advanced(dpre, K - 1 - k) * wv[k:k + 1, :]
    return acc


def ssd_conv_bwd(proj, w, b, dxc, S, *, name, ts=512, tc=512):
    T = proj.shape[0]
    C, K = SSD_CONV_CH, SSD_CONV
    ts = _pick(S, (ts, 256, 128))
    off = PXBC0 // tc
    ns = S // ts
    nblk = T // HALO

    def body(x_ref, xp_ref, xn_ref, w_ref, b_ref, d_ref, dn_ref, o_ref, acc_ref):
        i = pl.program_id(1)
        first = (i % ns) == 0
        last = (i % ns) == ns - 1
        cat = jnp.concatenate([jnp.where(first, 0.0, xp_ref[...]), x_ref[...], xn_ref[...]], axis=0)
        wv = w_ref[...]
        pre = _conv_pre(cat, wv, b_ref[...], K)
        sg = _sigmoid(pre)
        dpre = _cat_next(d_ref[...], dn_ref[...], last) * (sg * (1.0 + pre * (1.0 - sg)))
        o_ref[...] = _conv_transposed(dpre, wv, K).astype(o_ref.dtype)
        part = _conv_stats(dpre, cat, K, ts)

        @pl.when(i == 0)
        def _():
            acc_ref[...] = part

        @pl.when(i > 0)
        def _():
            acc_ref[...] += part

    hp = _halo_prev(ts)
    hn = lambda i: jnp.minimum((i + 1) * (ts // HALO), nblk - 1)
    return pl.pallas_call(
        body, name=name, grid=(C // tc, T // ts),
        in_specs=[pl.BlockSpec((ts, tc), lambda j, i: (i, j + off)),
                  pl.BlockSpec((HALO, tc), lambda j, i: hp(i, j, off)),
                  pl.BlockSpec((HALO, tc), lambda j, i: (hn(i), j + off)),
                  pl.BlockSpec((8, tc), lambda j, i: (0, j)), pl.BlockSpec((1, tc), lambda j, i: (0, j)),
                  pl.BlockSpec((ts, tc), lambda j, i: (i, j)), pl.BlockSpec((HALO, tc), lambda j, i: (hn(i), j))],
        out_specs=[pl.BlockSpec((ts, tc), lambda j, i: (i, j)), pl.BlockSpec((8, tc), lambda j, i: (0, j))],
        out_shape=[_sds((T, C), MXU_DTYPE), _sds((8, C), F32)],
        compiler_params=_cparams(("parallel", "arbitrary")),
    )(proj, proj, proj, _pad_rows8(w), b.reshape(1, C), dxc, dxc)


def ffn_conv_gate_fwd(up, w, b, S, *, name, ts=512, tc=1408):
    T, C2 = up.shape
    C, K = C2 // 2, FFN_CONV
    ts = _pick(S, (ts, 256, 128))
    nj = C // tc
    ns = S // ts
    w8 = _pad_rows8(w)
    b2 = b.reshape(1, C2)

    def body(g_ref, gh_ref, v_ref, vh_ref, wg_ref, wv_ref, bg_ref, bv_ref, o_ref):
        first = (pl.program_id(0) % ns) == 0
        g = _conv_pre(_cat_prev(g_ref[...], gh_ref[...], first), wg_ref[...], bg_ref[...], K)
        v = _conv_pre(_cat_prev(v_ref[...], vh_ref[...], first), wv_ref[...], bv_ref[...], K)
        o_ref[...] = (g * _sigmoid(g) * v).astype(o_ref.dtype)

    hp = _halo_prev(ts)
    return pl.pallas_call(
        body, name=name, grid=(T // ts, nj),
        in_specs=[pl.BlockSpec((ts, tc), lambda i, j: (i, j)), pl.BlockSpec((HALO, tc), lambda i, j: hp(i, j)),
                  pl.BlockSpec((ts, tc), lambda i, j: (i, j + nj)), pl.BlockSpec((HALO, tc), lambda i, j: hp(i, j, nj)),
                  pl.BlockSpec((8, tc), lambda i, j: (0, j)), pl.BlockSpec((8, tc), lambda i, j: (0, j + nj)),
                  pl.BlockSpec((1, tc), lambda i, j: (0, j)), pl.BlockSpec((1, tc), lambda i, j: (0, j + nj))],
        out_specs=pl.BlockSpec((ts, tc), lambda i, j: (i, j)), out_shape=_sds((T, C), MXU_DTYPE),
        compiler_params=_cparams(("parallel", "parallel")),
    )(up, up, up, up, w8, w8, b2, b2)


def ffn_conv_gate_bwd(up, w, b, dact, S, *, name, ts=256, tc=1408):
    T, C2 = up.shape
    C, K = C2 // 2, FFN_CONV
    ts = _pick(S, (ts, 256, 128))
    nj = C // tc
    ns = S // ts
    nblk = T // HALO
    w8 = _pad_rows8(w)
    b2 = b.reshape(1, C2)

    def body(g_ref, gp_ref, gn_ref, v_ref, vp_ref, vn_ref, wg_ref, wv_ref, bg_ref, bv_ref, d_ref, dn_ref,
             dug_ref, duv_ref, ag_ref, av_ref):
        i = pl.program_id(1)
        first = (i % ns) == 0
        last = (i % ns) == ns - 1
        gcat = jnp.concatenate([jnp.where(first, 0.0, gp_ref[...]), g_ref[...], gn_ref[...]], axis=0)
        vcat = jnp.concatenate([jnp.where(first, 0.0, vp_ref[...]), v_ref[...], vn_ref[...]], axis=0)
        wg, wv = wg_ref[...], wv_ref[...]
        g = _conv_pre(gcat, wg, bg_ref[...], K)
        v = _conv_pre(vcat, wv, bv_ref[...], K)
        d = _cat_next(d_ref[...], dn_ref[...], last)
        sg = _sigmoid(g)
        dg = d * v * (sg * (1.0 + g * (1.0 - sg)))
        dv = d * (g * sg)
        dug_ref[...] = _conv_transposed(dg, wg, K).astype(dug_ref.dtype)
        duv_ref[...] = _conv_transposed(dv, wv, K).astype(duv_ref.dtype)
        sgp, svp = _conv_stats(dg, gcat, K, ts), _conv_stats(dv, vcat, K, ts)

        @pl.when(i == 0)
        def _():
            ag_ref[...] = sgp
            av_ref[...] = svp

        @pl.when(i > 0)
        def _():
            ag_ref[...] += sgp
            av_ref[...] += svp

    hp = _halo_prev(ts)
    hn = lambda i: jnp.minimum((i + 1) * (ts // HALO), nblk - 1)
    cur = lambda off: pl.BlockSpec((ts, tc), lambda j, i: (i, j + off))
    prv = lambda off: pl.BlockSpec((HALO, tc), lambda j, i: hp(i, j, off))
    nxt = lambda off: pl.BlockSpec((HALO, tc), lambda j, i: (hn(i), j + off))
    row = lambda r, off: pl.BlockSpec((r, tc), lambda j, i: (0, j + off))
    dug, duv, ag, av = pl.pallas_call(
        body, name=name, grid=(nj, T // ts),
        in_specs=[cur(0), prv(0), nxt(0), cur(nj), prv(nj), nxt(nj), row(8, 0), row(8, nj), row(1, 0), row(1, nj),
                  cur(0), nxt(0)],
        out_specs=[cur(0), cur(0), row(8, 0), row(8, 0)],
        out_shape=[_sds((T, C), MXU_DTYPE), _sds((T, C), MXU_DTYPE), _sds((8, C), F32), _sds((8, C), F32)],
        compiler_params=_cparams(("parallel", "arbitrary")),
    )(up, up, up, up, up, up, w8, w8, b2, b2, dact, dact)
    return dug, duv, jnp.concatenate([ag, av], axis=1)


def _pool_counts(pos, w):
    return jnp.minimum(pos + 1.0, float(w))


def pool_fwd(proj, pool_w, pool_scale, S, *, name, ts=1024):
    T = proj.shape[0]
    C, G, GD, H = POOL_WIDTH, POOL_GROUPS, POOL_GROUP_DIM, POOL_HALO
    ts = _pick(S, (ts, 256, 128))
    ns = S // ts
    off = PU0 // C

    def body(u_ref, h_ref, w_ref, s_ref, y_ref, p_ref):
        i = pl.program_id(0)
        first = (i % ns) == 0
        cat = jnp.concatenate([jnp.where(first, 0.0, h_ref[...]), u_ref[...]], axis=0)
        pos = ((i % ns) * ts + lax.broadcasted_iota(jnp.int32, (ts, 1), 0)).astype(F32)
        sums = cat
        win = 1
        for g, wlen in enumerate(POOL_WINDOWS):
            while win < wlen:
                sums = sums + pltpu.roll(sums, win, axis=0)
                win *= 2
            sl = slice(g * GD, (g + 1) * GD)
            pooled = sums[H:, sl] / _pool_counts(pos, wlen) - cat[H:, sl]
            p_ref[:, sl] = pooled.astype(p_ref.dtype)
            y_ref[:, sl] = (_dot(pooled, w_ref[g]) * s_ref[:, sl]).astype(y_ref.dtype)

    return pl.pallas_call(
        body, name=name, grid=(T // ts,),
        in_specs=[pl.BlockSpec((ts, C), lambda i: (i, off)),
                  pl.BlockSpec((H, C), lambda i: (jnp.maximum(i * (ts // H) - 1, 0), off)),
                  pl.BlockSpec((G, GD, GD), lambda i: (0, 0, 0)), pl.BlockSpec((1, C), lambda i: (0, 0))],
        out_specs=[pl.BlockSpec((ts, C), lambda i: (i, 0)), pl.BlockSpec((ts, C), lambda i: (i, 0))],
        out_shape=[_sds((T, C), MXU_DTYPE), _sds((T, C), MXU_DTYPE)],
        compiler_params=_cparams(("parallel",)),
    )(proj, proj, _mx(pool_w), pool_scale.reshape(1, C))


def pool_bwd(dmix, pooled, pool_w, pool_scale, S, *, name, ts=1024):
    T = dmix.shape[0]
    C, G, GD, H = POOL_WIDTH, POOL_GROUPS, POOL_GROUP_DIM, POOL_HALO
    ts = _pick(S, (ts, 256, 128))
    ns = S // ts
    off = SSD_WIDTH // C
    nblk = T // H

    def body(d_ref, dh_ref, p_ref, w_ref, s_ref, du_ref, dw_ref, ds_ref):
        i = pl.program_id(0)
        last = (i % ns) == ns - 1
        dcat = jnp.concatenate([d_ref[...], jnp.where(last, 0.0, dh_ref[...])], axis=0)
        n = ts + H
        pos = ((i % ns) * ts + lax.broadcasted_iota(jnp.int32, (n, 1), 0)).astype(F32)
        dws, dss = [], []
        for g, wlen in enumerate(POOL_WINDOWS):
            sl = slice(g * GD, (g + 1) * GD)
            wg = w_ref[g]
            pg = p_ref[:, sl]
            dys = dcat[:, sl] * s_ref[:, sl]
            dss.append(jnp.sum(dcat[:ts, sl] * _dot(pg, wg), axis=0, keepdims=True))
            dws.append(_dot_tn(pg, dys[:ts]))
            dp = _dot_nt(dys, wg)
            q = dp / _pool_counts(pos, wlen)
            win = 1
            while win < wlen:
                q = q + pltpu.roll(q, n - win, axis=0)
                win *= 2
            du_ref[:, sl] = (q[:ts] - dp[:ts]).astype(du_ref.dtype)
        dsp = jnp.concatenate(dss, axis=1)

        @pl.when(i == 0)
        def _():
            for g in range(G):
                dw_ref[g] = dws[g]
            ds_ref[...] = dsp

        @pl.when(i > 0)
        def _():
            for g in range(G):
                dw_ref[g] += dws[g]
            ds_ref[...] += dsp

    return pl.pallas_call(
        body, name=name, grid=(T // ts,),
        in_specs=[pl.BlockSpec((ts, C), lambda i: (i, off)),
                  pl.BlockSpec((H, C), lambda i: (jnp.minimum((i + 1) * (ts // H), nblk - 1), off)),
                  pl.BlockSpec((ts, C), lambda i: (i, 0)),
                  pl.BlockSpec((G, GD, GD), lambda i: (0, 0, 0)), pl.BlockSpec((1, C), lambda i: (0, 0))],
        out_specs=[pl.BlockSpec((ts, C), lambda i: (i, 0)), pl.BlockSpec((G, GD, GD), lambda i: (0, 0, 0)),
                   pl.BlockSpec((1, C), lambda i: (0, 0))],
        out_shape=[_sds((T, C), MXU_DTYPE), _sds((G, GD, GD), F32), _sds((1, C), F32)],
        compiler_params=_cparams(("arbitrary",)),
    )(dmix, dmix, pooled, _mx(pool_w), pool_scale.reshape(1, C))


ROPE0 = MLA_NOPE
ROPE_HALF = MLA_ROPE // 2


def _rope_tables(pos, invf):
    lane = lax.broadcasted_iota(jnp.int32, (1, HEAD_W), 1)
    ang = pos * invf
    cs, sn = jnp.cos(ang), jnp.sin(ang)
    in_a = (lane >= ROPE0) & (lane < ROPE0 + ROPE_HALF)
    in_b = (lane >= ROPE0 + ROPE_HALF) & (lane < ROPE0 + MLA_ROPE)
    return jnp.where(in_a | in_b, cs, 1.0), jnp.where(in_a, -sn, 0.0), jnp.where(in_b, sn, 0.0), in_a | in_b


def _rope(v, cosf, sin_a, sin_b):
    return (v * cosf + pltpu.roll(v, HEAD_W - ROPE_HALF, axis=1) * sin_a + pltpu.roll(v, ROPE_HALF, axis=1) * sin_b)


def _unrope(d, cosf, sin_a, sin_b):
    return (d * cosf + pltpu.roll(d * sin_a, ROPE_HALF, axis=1) + pltpu.roll(d * sin_b, HEAD_W - ROPE_HALF, axis=1))


def _rms_tile(xv, gamma):
    return (xv * lax.rsqrt(jnp.mean(xv * xv, axis=-1, keepdims=True) + EPS)) * gamma


def mla_prep_fwd(proj, pos, invf, q_norm, w_uq_p, kv_norm, w_ukv_p, *, name, tm=512):
    T = proj.shape[0]
    tm = _pick(T, (tm, 256, 128))
    QR, KR, P = MLA_Q_RANK, MLA_KV_RANK, MLA_PAD

    def body(cq_ref, ckv_ref, kpe_ref, pos_ref, invf_ref, qn_ref, wq_ref, kn_ref, wkv_ref,
             q_ref, k_ref, v_ref, cqn_ref, ckvn_ref):
        cosf, sin_a, sin_b, _ = _rope_tables(pos_ref[...], invf_ref[...])
        cqn = _rms_tile(cq_ref[...], qn_ref[...]).astype(MXU_DTYPE)
        ckvn = _rms_tile(ckv_ref[...], kn_ref[...]).astype(MXU_DTYPE)
        cqn_ref[...] = cqn
        ckvn_ref[...] = ckvn
        qp = _dot(cqn, wq_ref[...])
        kvp = _dot(ckvn, wkv_ref[...])
        kpe = _rope(kpe_ref[...], cosf, sin_a, sin_b)
        for h in range(MLA_HEADS):
            sl = slice(h * HEAD_W, (h + 1) * HEAD_W)
            q_ref[:, sl] = (_rope(qp[:, sl], cosf, sin_a, sin_b) * ATTN_SCALE).astype(q_ref.dtype)
            k_ref[:, sl] = (kvp[:, sl] + kpe).astype(k_ref.dtype)
            v_ref[:, sl] = kvp[:, P + h * HEAD_W:P + (h + 1) * HEAD_W].astype(v_ref.dtype)

    row = lambda w: pl.BlockSpec((tm, w), lambda i: (i, 0))
    full = lambda a, b: pl.BlockSpec((a, b), lambda i: (0, 0))
    return pl.pallas_call(
        body, name=name, grid=(T // tm,),
        in_specs=[pl.BlockSpec((tm, QR), lambda i: (i, PCQ0 // QR)), pl.BlockSpec((tm, KR), lambda i: (i, PCKV0 // KR)),
                  pl.BlockSpec((tm, LANE), lambda i: (i, PKPE0 // LANE)), row(1), full(1, LANE),
                  full(1, QR), full(QR, P), full(1, KR), full(KR, 2 * P)],
        out_specs=[row(P), row(P), row(P), row(QR), row(KR)],
        out_shape=[_sds((T, P), MXU_DTYPE)] * 3 + [_sds((T, QR), MXU_DTYPE), _sds((T, KR), MXU_DTYPE)],
        compiler_params=_cparams(("parallel",)),
    )(proj, proj, proj, pos, invf, q_norm.reshape(1, QR), w_uq_p, kv_norm.reshape(1, KR), w_ukv_p)


def mla_prep_bwd(proj, pos, invf, q_norm, w_uq_p, kv_norm, w_ukv_p, dq, dk, dv, *, name, tm=512):
    T = proj.shape[0]
    tm = _pick(T, (tm, 256, 128))
    QR, KR, P = MLA_Q_RANK, MLA_KV_RANK, MLA_PAD

    def body(cq_ref, ckv_ref, pos_ref, invf_ref, qn_ref, wq_ref, kn_ref, wkv_ref, dq_ref, dk_ref, dv_ref,
             dqp_ref, dkvp_ref, dcq_ref, dckv_ref, dkpe_ref, dqn_ref, dkn_ref):
        cosf, sin_a, sin_b, rot = _rope_tables(pos_ref[...], invf_ref[...])
        dkpe = jnp.zeros((tm, HEAD_W), F32)
        for h in range(MLA_HEADS):
            sl = slice(h * HEAD_W, (h + 1) * HEAD_W)
            dqp_ref[:, sl] = _unrope(dq_ref[:, sl] * ATTN_SCALE, cosf, sin_a, sin_b).astype(dqp_ref.dtype)
            dkh = dk_ref[:, sl]
            dkpe = dkpe + dkh
            dkvp_ref[:, sl] = dkh.astype(dkvp_ref.dtype)
            dkvp_ref[:, P + h * HEAD_W:P + (h + 1) * HEAD_W] = dv_ref[:, sl].astype(dkvp_ref.dtype)
        dkpe_ref[...] = jnp.where(rot, _unrope(dkpe, cosf, sin_a, sin_b), 0.0).astype(dkpe_ref.dtype)
        dcq, dqn = _rms_bwd_tile(cq_ref[...], qn_ref[...], _dot_nt(dqp_ref[...], wq_ref[...]))
        dckv, dkn = _rms_bwd_tile(ckv_ref[...], kn_ref[...], _dot_nt(dkvp_ref[...], wkv_ref[...]))
        dcq_ref[...] = dcq.astype(dcq_ref.dtype)
        dckv_ref[...] = dckv.astype(dckv_ref.dtype)

        @pl.when(pl.program_id(0) == 0)
        def _():
            dqn_ref[...] = dqn
            dkn_ref[...] = dkn

        @pl.when(pl.program_id(0) > 0)
        def _():
            dqn_ref[...] += dqn
            dkn_ref[...] += dkn

    row = lambda w: pl.BlockSpec((tm, w), lambda i: (i, 0))
    full = lambda a, b: pl.BlockSpec((a, b), lambda i: (0, 0))
    return pl.pallas_call(
        body, name=name, grid=(T // tm,),
        in_specs=[pl.BlockSpec((tm, QR), lambda i: (i, PCQ0 // QR)), pl.BlockSpec((tm, KR), lambda i: (i, PCKV0 // KR)),
                  row(1), full(1, LANE), full(1, QR), full(QR, P), full(1, KR), full(KR, 2 * P), row(P), row(P), row(P)],
        out_specs=[row(P), row(2 * P), row(QR), row(KR), row(LANE), full(1, QR), full(1, KR)],
        out_shape=[_sds((T, P), MXU_DTYPE), _sds((T, 2 * P), MXU_DTYPE), _sds((T, QR), MXU_DTYPE),
                   _sds((T, KR), MXU_DTYPE), _sds((T, LANE), MXU_DTYPE), _sds((1, QR), F32), _sds((1, KR), F32)],
        compiler_params=_cparams(("arbitrary",)),
    )(proj, proj, pos, invf, q_norm.reshape(1, QR), w_uq_p, kv_norm.reshape(1, KR), w_ukv_p, dq, dk, dv)


ATTN_SCALE = 1.0 / math.sqrt(MLA_QK)


def _causal_mask(i, j, blk):
    row = lax.broadcasted_iota(jnp.int32, (blk, blk), 0)
    col = lax.broadcasted_iota(jnp.int32, (blk, blk), 1)
    return col <= row + (i - j) * blk


def _hosting(hosted, grid, n_in, n_out, n_scratch):
    if hosted is None:
        return (lambda body: body), (), [], [], []
    hi, ho = len(hosted.inputs), len(hosted.out_shapes)

    def wrap(body):
        def full(*refs):
            ins, rest = refs[:n_in + hi], refs[n_in + hi:]
            outs, scr = rest[:n_out + ho], rest[n_out + ho:]
            parts = ins[n_in:], outs[n_out:], scr[n_scratch:]
            ids = [pl.program_id(d) for d in range(len(grid))]
            step = ids[0]
            for d in range(1, len(grid)):
                step = step * grid[d] + ids[d]
            total = math.prod(grid)

            @pl.when(step == 0)
            def _():
                hosted.start(*parts)

            body(*ins[:n_in], *outs[:n_out], *scr[:n_scratch])

            @pl.when(step == total // 2)
            def _():
                hosted.relay(*parts)

            @pl.when(step == total - 1)
            def _():
                hosted.finish(*parts)

        return full

    hbm = pl.BlockSpec(memory_space=pl.ANY)
    return wrap, tuple(hosted.inputs), [hbm] * ho, list(hosted.out_shapes), list(hosted.sems)


def flash_fwd(q, k, v, S, *, name, blk=1024, hosted=None):
    T, P = q.shape
    blk = _pick(S // 2, (blk, 256, 128))
    B, nq, H, W = T // S, S // (2 * blk), MLA_HEADS, HEAD_W
    grid = (B, H, nq)
    wrap, h_in, h_ospecs, h_oshapes, h_scratch = _hosting(hosted, grid, 3, 2, 0)

    def body(q_ref, k_ref, v_ref, o_ref, lse_ref):
        i = pl.program_id(2)
        q_up, q_lo = q_ref[:blk, :], q_ref[blk:, :]

        def online(qv, kv, vv, carry, masked):
            m_prev, l_prev, acc = carry
            s = _dot_nt(qv, kv)
            if masked:
                s = jnp.where(_causal_mask(0, 0, blk), s, -jnp.inf)
            m_new = jnp.maximum(m_prev, jnp.max(s, axis=1, keepdims=True))
            p = jnp.exp(s - m_new)
            alpha = jnp.exp(m_prev - m_new)
            return (m_new, alpha * l_prev + jnp.sum(p, axis=1, keepdims=True), alpha * acc + _dot(p, vv))

        def keys(j):
            rows = pl.ds(pl.multiple_of(j * blk, blk), blk)
            return k_ref[rows, :], v_ref[rows, :]

        def both(j, carry):
            kv, vv = keys(j)
            return online(q_up, kv, vv, carry[0], False), online(q_lo, kv, vv, carry[1], False)

        init = (jnp.full((blk, 1), -jnp.inf, F32), jnp.zeros((blk, 1), F32), jnp.zeros((blk, W), F32))
        up, lo = lax.fori_loop(0, 2 * i, both, (init, init))
        kv, vv = keys(2 * i)
        up = online(q_up, kv, vv, up, True)
        lo = online(q_lo, kv, vv, lo, False)
        kv, vv = keys(2 * i + 1)
        lo = online(q_lo, kv, vv, lo, True)
        for rows, (m, l, acc) in ((slice(0, blk), up), (slice(blk, 2 * blk), lo)):
            o_ref[rows, :] = acc / l
            lse_ref[rows, :] = jnp.broadcast_to(m + jnp.log(l), (blk, W))

    qmap = lambda b, h, i: (b * nq + i, h)
    kmap = lambda b, h, i: (b, h)
    hbm = pl.BlockSpec(memory_space=pl.ANY)
    return pl.pallas_call(
        wrap(body), name=name, grid=grid,
        in_specs=[pl.BlockSpec((2 * blk, W), qmap), pl.BlockSpec((S, W), kmap), pl.BlockSpec((S, W), kmap)] + [hbm] * len(h_in),
        out_specs=[pl.BlockSpec((2 * blk, W), qmap), pl.BlockSpec((2 * blk, W), qmap)] + h_ospecs,
        out_shape=[_sds((T, P), F32), _sds((T, P), F32)] + h_oshapes,
        scratch_shapes=h_scratch,
        compiler_params=_cparams(("arbitrary",) * 3 if hosted else ("parallel", "parallel", "arbitrary")),
    )(q, k, v, *h_in)


def flash_bwd(q, k, v, o, lse, dmix, S, *, name, blk=1024, hosted=None):
    T, P = q.shape
    blk = _pick(S, (blk, 256, 128))
    B, nq, H, W = T // S, S // blk, MLA_HEADS, HEAD_W
    off = (SSD_WIDTH + POOL_WIDTH) // W
    grid = (B, H, nq)
    wrap, h_in, h_ospecs, h_oshapes, h_scratch = _hosting(hosted, grid, 6, 3, 1)

    def body(q_ref, k_ref, v_ref, o_ref, lse_ref, do_ref, dq_ref, dk_ref, dv_ref, delta_s):
        j = pl.program_id(2)

        @pl.when(j == 0)
        def _():
            for i in range(nq):
                rows = slice(i * blk, (i + 1) * blk)
                delta_s[rows, :] = jnp.sum(do_ref[rows, :] * o_ref[rows, :], axis=1, keepdims=True)
                dq_ref[rows, :] = jnp.zeros((blk, W), F32)

        kv, vv = k_ref[...], v_ref[...]

        def step(i, carry, masked):
            dk, dv = carry
            rows = pl.ds(pl.multiple_of(i * blk, blk), blk)
            qv, do = q_ref[rows, :], do_ref[rows, :]
            p = jnp.exp(_dot_nt(qv, kv) - lse_ref[rows, 0:1])
            if masked:
                p = jnp.where(_causal_mask(0, 0, blk), p, 0.0)
            ds = p * (_dot_nt(do, vv) - delta_s[rows, :])
            dq_ref[rows, :] += _dot(ds, kv)
            return dk + _dot_tn(ds, qv), dv + _dot_tn(p, do)

        zero = jnp.zeros((blk, W), F32)
        carry = step(j, (zero, zero), True)
        dk, dv = lax.fori_loop(j + 1, nq, lambda i, c: step(i, c, False), carry)
        dk_ref[...] = dk
        dv_ref[...] = dv

    full = lambda b, h, j: (b, h)
    kmap = lambda b, h, j: (b * nq + j, h)
    hbm = pl.BlockSpec(memory_space=pl.ANY)
    return pl.pallas_call(
        wrap(body), name=name, grid=grid,
        in_specs=[pl.BlockSpec((S, W), full), pl.BlockSpec((blk, W), kmap), pl.BlockSpec((blk, W), kmap),
                  pl.BlockSpec((S, W), full), pl.BlockSpec((S, W), full),
                  pl.BlockSpec((S, W), lambda b, h, j: (b, off + h))] + [hbm] * len(h_in),
        out_specs=[pl.BlockSpec((S, W), full), pl.BlockSpec((blk, W), kmap), pl.BlockSpec((blk, W), kmap)] + h_ospecs,
        out_shape=[_sds((T, P), F32)] * 3 + h_oshapes,
        scratch_shapes=[pltpu.VMEM((S, 1), F32)] + h_scratch,
        compiler_params=_cparams(("arbitrary",) * 3 if hosted else ("parallel", "parallel", "arbitrary")),
    )(q, k, v, o, lse, dmix, *h_in)


SSD_PAIRS = SSD_HEADS // 2
PAIRS_PER_GROUP = SSD_PAIRS // SSD_GROUPS
GN = SSD_GROUPS * SSD_STATE


def _log1p_small(e):
    return jnp.where(e < 1e-3, e * (1.0 - e * (0.5 - e / 3.0)), jnp.log(1.0 + e))


def _softplus(v):
    return jnp.maximum(v, 0.0) + _log1p_small(jnp.exp(-jnp.abs(v)))


def _ssd_decay(dt_raw, dtb, alog):
    L = dt_raw.shape[0]
    pre = dt_raw + dtb
    dt = _softplus(pre)
    a = -jnp.exp(alog)
    row = lax.broadcasted_iota(jnp.int32, (L, L), 0)
    col = lax.broadcasted_iota(jnp.int32, (L, L), 1)
    tri = row >= col
    cum = _dot_hi(tri.astype(F32), dt * a)
    return pre, dt, a, tri, cum, cum.T


def _col(m, h):
    return m[:, h:h + 1]


def _pair_sel(m, k, lo):
    return jnp.where(lo, _col(m, 2 * k), _col(m, 2 * k + 1))


def _ssd_specs(S):
    L = SSD_CHUNK
    nc = S // L
    return L, nc


def ssd_fwd(proj, xc, dtb, alog, dchan, normw, S, *, name, hosted=None):
    T = proj.shape[0]
    L, nc = _ssd_specs(S)
    B, W, N = T // S, SSD_WIDTH, SSD_STATE
    wrap, h_in, h_ospecs, h_oshapes, h_scratch = _hosting(hosted, (B, nc), 9, 3, 1)

    def body(xs_ref, bs_ref, cs_ref, dt_ref, z_ref, dtb_ref, alog_ref, dch_ref, nw_ref, y_ref, ys_ref, hin_ref, st):
        @pl.when(pl.program_id(1) == 0)
        def _():
            st[...] = jnp.zeros(st.shape, F32)

        hin_ref[...] = st[...]
        _, dt, a, tri, cum, cum_t = _ssd_decay(dt_ref[...], dtb_ref[...], alog_ref[...])
        last = cum[L - 1:L, :]
        lo = lax.broadcasted_iota(jnp.int32, (1, LANE), 1) < SSD_HEAD_DIM
        for g in range(SSD_GROUPS):
            bm = bs_ref[:, g * N:(g + 1) * N]
            cm = cs_ref[:, g * N:(g + 1) * N]
            bm_t = bm.T
            gmat = _dot_nt(cm, bm)
            for kk in range(PAIRS_PER_GROUP):
                k = g * PAIRS_PER_GROUP + kk
                sl = slice(k * LANE, (k + 1) * LANE)
                xv = xs_ref[:, sl]
                xdt = xv * _pair_sel(dt, k, lo)
                cum_cols = [jnp.broadcast_to(_col(cum, h), (L, LANE)) for h in (2 * k, 2 * k + 1)]
                cum_sel = jnp.where(lo, cum_cols[0], cum_cols[1])
                last_sel = _pair_sel(last, k, lo)
                yd = []
                for j, h in enumerate((2 * k, 2 * k + 1)):
                    gam = jnp.exp(jnp.where(tri, cum_cols[j] - cum_t[h:h + 1, :], -jnp.inf))
                    yd.append(_dot(gmat * gam, xdt))
                hp = st[:, sl]
                y_off = _dot(cm, hp) * jnp.exp(cum_sel)
                y_ref[:, sl] = jnp.where(lo, yd[0], yd[1]) + y_off + xv * dch_ref[:, sl]
                zmat = xdt * jnp.exp(last_sel - cum_sel)
                st[:, sl] = hp * jnp.exp(last_sel) + _dot(bm_t, zmat)
        y = y_ref[...]
        z = z_ref[...]
        yz = y * (z * _sigmoid(z))
        ys_ref[...] = _rms_tile(yz, nw_ref[...]).astype(ys_ref.dtype)

    r = lambda b, c: b * nc + c
    vec = lambda w: pl.BlockSpec((1, w), lambda b, c: (0, 0))
    hbm = pl.BlockSpec(memory_space=pl.ANY)
    return pl.pallas_call(
        wrap(body), name=name, grid=(B, nc),
        in_specs=[pl.BlockSpec((L, W), lambda b, c: (r(b, c), 0)),
                  pl.BlockSpec((L, GN), lambda b, c: (r(b, c), W // GN)),
                  pl.BlockSpec((L, GN), lambda b, c: (r(b, c), W // GN + 1)),
                  pl.BlockSpec((L, LANE), lambda b, c: (r(b, c), PDT0 // LANE)),
                  pl.BlockSpec((L, W), lambda b, c: (r(b, c), PZ0 // W)),
                  vec(LANE), vec(LANE), vec(W), vec(W)] + [hbm] * len(h_in),
        out_specs=[pl.BlockSpec((L, W), lambda b, c: (r(b, c), 0)), pl.BlockSpec((L, W), lambda b, c: (r(b, c), 0)),
                   pl.BlockSpec((N, W), lambda b, c: (r(b, c), 0))] + h_ospecs,
        out_shape=[_sds((T, W), F32), _sds((T, W), MXU_DTYPE), _sds((T // L * N, W), F32)] + h_oshapes,
        scratch_shapes=[pltpu.VMEM((N, W), F32)] + h_scratch,
        compiler_params=_cparams(("arbitrary", "arbitrary") if hosted else ("parallel", "arbitrary")),
    )(xc, xc, xc, proj, proj, dtb, alog, dchan, normw, *h_in)


def ssd_bwd(proj, xc, ypre, hin, dmix, dtb, alog, dchan, normw, S, *, name):
    T = proj.shape[0]
    L, nc = _ssd_specs(S)
    B, W, N = T // S, SSD_WIDTH, SSD_STATE

    def body(xs_ref, bs_ref, cs_ref, dt_ref, z_ref, y_ref, hin_ref, dys_ref, dtb_ref, alog_ref, dch_ref, nw_ref,
             dxc_ref, ddt_ref, dz_ref, sm_ref, dnw_ref, dst):
        step = pl.program_id(0) * nc + pl.program_id(1)

        @pl.when(pl.program_id(1) == 0)
        def _():
            dst[...] = jnp.zeros(dst.shape, F32)

        pre, dt, a, tri, cum, cum_t = _ssd_decay(dt_ref[...], dtb_ref[...], alog_ref[...])
        last = cum[L - 1:L, :]
        e_last = jnp.exp(last)
        lane = lax.broadcasted_iota(jnp.int32, (1, LANE), 1)
        sub = lax.broadcasted_iota(jnp.int32, (LANE, 1), 0)
        lo = lane < SSD_HEAD_DIM
        is_last_row = sub == L - 1
        tri_t = (lax.broadcasted_iota(jnp.int32, (L, L), 0) <= lax.broadcasted_iota(jnp.int32, (L, L), 1))

        y, z, nw = y_ref[...], z_ref[...], nw_ref[...]
        sg = _sigmoid(z)
        gate = z * sg
        dyz, dnw = _rms_bwd_tile(y * gate, nw, dys_ref[...])
        dy_all = dyz * gate
        dz_ref[...] = (dyz * y * (sg * (1.0 + z * (1.0 - sg)))).astype(dz_ref.dtype)

        d_cum = jnp.zeros((L, LANE), F32)
        d_cum_t = jnp.zeros((LANE, L), F32)
        d_dt = jnp.zeros((L, LANE), F32)
        d_dskip = jnp.zeros((1, LANE), F32)
        for g in range(SSD_GROUPS):
            bm = bs_ref[:, g * N:(g + 1) * N]
            cm = cs_ref[:, g * N:(g + 1) * N]
            cm_t = cm.T
            gmat = _dot_nt(cm, bm)
            gmat_t = _dot_nt(bm, cm)
            d_g = jnp.zeros((L, L), F32)
            d_bm = jnp.zeros((L, N), F32)
            d_cm = jnp.zeros((L, N), F32)
            for kk in range(PAIRS_PER_GROUP):
                k = g * PAIRS_PER_GROUP + kk
                sl = slice(k * LANE, (k + 1) * LANE)
                xv = xs_ref[:, sl]
                dyv = dy_all[:, sl]
                dt_sel = _pair_sel(dt, k, lo)
                xdt = xv * dt_sel
                hp = hin_ref[:, sl]
                dh_out = dst[:, sl]
                cum_cols = [jnp.broadcast_to(_col(cum, h), (L, LANE)) for h in (2 * k, 2 * k + 1)]
                cum_sel = jnp.where(lo, cum_cols[0], cum_cols[1])
                last_sel = _pair_sel(last, k, lo)
                e_sel = jnp.exp(cum_sel)
                w_sel = jnp.exp(last_sel - cum_sel)
                e_lane = jnp.exp(last_sel)
                y_off = _dot(cm, hp) * e_sel
                zmat = xdt * w_sel
                d_z = _dot(bm, dh_out)
                d_bm = d_bm + _dot_nt(zmat, dh_out)
                d_xdt = d_z * w_sel
                dw_full = d_z * zmat
                hh = dh_out * hp
                d_r = dyv * e_sel
                d_cm = d_cm + _dot_nt(d_r, hp)
                dst[:, sl] = dh_out * e_lane + _dot(cm_t, d_r)
                dyoff_full = dyv * y_off
                for j, h in enumerate((2 * k, 2 * k + 1)):
                    mine = lo if j == 0 else jnp.logical_not(lo)
                    hot = lane == h
                    dyh = jnp.where(mine, dyv, 0.0)
                    gam = jnp.exp(jnp.where(tri, cum_cols[j] - cum_t[h:h + 1, :], -jnp.inf))
                    gam_t = jnp.exp(jnp.where(tri_t, cum_t[h:h + 1, :] - cum_cols[j], -jnp.inf))
                    mx = gmat * gam
                    d_xdt = d_xdt + _dot(gmat_t * gam_t, dyh)
                    d_mx = jnp.where(tri, _dot_nt(dyh, xdt), 0.0)
                    d_g = d_g + d_mx * gam
                    d_seg = d_mx * mx
                    row_l = jnp.sum(d_seg + jnp.where(mine, dyoff_full - dw_full, 0.0), axis=1, keepdims=True)
                    at_end = (jnp.sum(jnp.where(mine, dw_full, 0.0), keepdims=True)
                              + jnp.sum(jnp.where(mine, hh, 0.0), keepdims=True) * _col(e_last, h))
                    d_cum = d_cum + jnp.where(hot, row_l + jnp.where(is_last_row, at_end, 0.0), 0.0)
                    d_cum_t = d_cum_t - jnp.where(sub == h, jnp.sum(d_seg, axis=0, keepdims=True), 0.0)
                    d_dskip = d_dskip + jnp.where(hot, jnp.sum(jnp.where(mine, dyv * xv, 0.0), keepdims=True), 0.0)
                for j, h in enumerate((2 * k, 2 * k + 1)):
                    mine = lo if j == 0 else jnp.logical_not(lo)
                    d_dt = d_dt + jnp.where(lane == h, jnp.sum(jnp.where(mine, d_xdt * xv, 0.0), axis=1, keepdims=True), 0.0)
                dxc_ref[:, sl] = d_xdt * dt_sel + dyv * dch_ref[:, sl]
            dxc_ref[:, W + g * N:W + (g + 1) * N] = d_bm + _dot_tn(d_g, cm)
            dxc_ref[:, W + GN + g * N:W + GN + (g + 1) * N] = d_cm + _dot(d_g, bm)

        d_cum = d_cum + d_cum_t.T
        d_da = _dot_hi(jnp.logical_not(tri).astype(F32) + (lax.broadcasted_iota(jnp.int32, (L, L), 0)
                                                              == lax.broadcasted_iota(jnp.int32, (L, L), 1)).astype(F32), d_cum)
        d_dt = d_dt + d_da * a
        heads = lane < SSD_HEADS
        d_pre = jnp.where(heads, d_dt * _sigmoid(pre), 0.0)
        ddt_ref[...] = d_pre.astype(ddt_ref.dtype)
        d_alog = jnp.sum(d_da * dt, axis=0, keepdims=True) * a
        part = jnp.concatenate([jnp.where(heads, d_alog, 0.0), jnp.sum(d_pre, axis=0, keepdims=True), d_dskip,
                                jnp.zeros((5, LANE), F32)], axis=0)

        @pl.when(step == 0)
        def _():
            sm_ref[...] = part
            dnw_ref[...] = dnw

        @pl.when(step > 0)
        def _():
            sm_ref[...] += part
            dnw_ref[...] += dnw

    r = lambda b, c: b * nc + (nc - 1 - c)
    vec = lambda w: pl.BlockSpec((1, w), lambda b, c: (0, 0))
    blk = lambda w, j: pl.BlockSpec((L, w), lambda b, c: (r(b, c), j))
    return pl.pallas_call(
        body, name=name, grid=(B, nc),
        in_specs=[blk(W, 0), blk(GN, W // GN), blk(GN, W // GN + 1), blk(LANE, PDT0 // LANE), blk(W, PZ0 // W),
                  blk(W, 0), pl.BlockSpec((N, W), lambda b, c: (r(b, c), 0)), blk(W, 0),
                  vec(LANE), vec(LANE), vec(W), vec(W)],
        out_specs=[blk(SSD_CONV_CH, 0), blk(LANE, 0), blk(W, 0), pl.BlockSpec((8, LANE), lambda b, c: (0, 0)), vec(W)],
        out_shape=[_sds((T, SSD_CONV_CH), F32), _sds((T, LANE), MXU_DTYPE), _sds((T, W), MXU_DTYPE),
                   _sds((8, LANE), F32), _sds((1, W), F32)],
        scratch_shapes=[pltpu.VMEM((N, W), F32)],
        compiler_params=_cparams(("arbitrary", "arbitrary")),
    )(xc, xc, xc, proj, proj, ypre, hin, dmix, dtb, alog, dchan, normw)


def _adamw_math(w, g, m, v):
    m = ADAM_B1 * m + (1.0 - ADAM_B1) * g
    v = ADAM_B2 * v + (1.0 - ADAM_B2) * (g * g)
    m_hat = m / (1.0 - ADAM_B1 ** ADAM_STEP)
    v_hat = v / (1.0 - ADAM_B2 ** ADAM_STEP)
    delta = -ADAM_LR * (m_hat / (jnp.sqrt(v_hat) + ADAM_EPS) + ADAM_WD * w)
    return delta, m, v


def adamw_layers(w, g_layers, m, v, *, name, tr=256):
    L, A, B = w.shape
    tr = _pick(A, (tr, 192, 176, 128, 64, 32, 16, 8))
    na = A // tr
    n = len(g_layers[0])

    def body(*refs):
        w_ref, m_ref, v_ref = refs[0], refs[1 + L * n], refs[2 + L * n]
        g_ref, d_ref, nm_ref, nv_ref = refs[3 + L * n:]
        layer = pl.program_id(0)
        g = None
        for l in range(L):
            parts = refs[1 + l * n:1 + (l + 1) * n]
            gl = parts[0][...]
            for p in parts[1:]:
                gl = gl + p[...]
            g = gl if g is None else jnp.where(layer == l, gl, g)
        g_ref[...] = g
        d_ref[...], nm_ref[...], nv_ref[...] = _adamw_math(w_ref[...], g, m_ref[...], v_ref[...])

    def g_spec(l):
        return pl.BlockSpec((tr, B), lambda layer, i: (jnp.where(layer == l, i, jnp.where(layer < l, 0, na - 1)), 0))

    spec = pl.BlockSpec((None, tr, B), lambda layer, i: (layer, i, 0))
    return pl.pallas_call(
        body, name=name, grid=(L, na), in_specs=[spec] + [g_spec(l) for l in range(L) for _ in range(n)] + [spec] * 2,
        out_specs=[spec] * 4, out_shape=[_sds((L, A, B), F32)] * 4, compiler_params=_cparams(("arbitrary", "arbitrary")),
    )(w, *[p for parts in g_layers for p in parts], m, v)


def adamw_small(ws, gs, ms, vs, *, name):
    n = len(ws)

    def body(*refs):
        w_refs, g_refs, m_refs, v_refs = (refs[i * n:(i + 1) * n] for i in range(4))
        d_refs, nm_refs, nv_refs = (refs[(4 + i) * n:(5 + i) * n] for i in range(3))
        for a in range(n):
            d_refs[a][...], nm_refs[a][...], nv_refs[a][...] = _adamw_math(
                w_refs[a][...], g_refs[a][...], m_refs[a][...], v_refs[a][...])

    vm = pl.BlockSpec(memory_space=pltpu.VMEM)
    out = pl.pallas_call(
        body, name=name, in_specs=[vm] * (4 * n), out_specs=[vm] * (3 * n),
        out_shape=[_sds(w.shape, F32) for w in ws] * 3, compiler_params=pltpu.CompilerParams(vmem_limit_bytes=VMEM_LIMIT),
    )(*ws, *gs, *ms, *vs)
    return out[:n], out[n:2 * n], out[2 * n:]


def _my_place():
    return lax.axis_index("x"), lax.axis_index("y"), lax.axis_index("c")


def _other_chips(x, y):
    return [(1 - x, y), (x, 1 - y), (1 - x, 1 - y)]


def relation_of(chip, me):
    d = chip ^ me
    return jnp.where(d == 2, 0, jnp.where(d == 1, 1, jnp.where(d == 3, 2, -1)))


class Exchange(NamedTuple):
    inputs: tuple
    out_shapes: tuple
    sems: tuple
    start: Callable
    relay: Callable
    finish: Callable


def scatter_exchange(srcs):
    n = len(srcs)

    def copies(ins, outs, sems):
        x, y, c = _my_place()
        out = []
        for k, (px, py) in enumerate(_other_chips(x, y)):
            for a in range(n):
                out.append(pltpu.make_async_remote_copy(
                    src_ref=ins[a].at[2 * px + py], dst_ref=outs[a].at[k], send_sem=sems[0].at[k, a],
                    recv_sem=sems[1].at[k, a], device_id=(px, py, c), device_id_type=pl.DeviceIdType.MESH))
        return out

    def start(ins, outs, sems):
        for cp in copies(ins, outs, sems):
            cp.start()

    def finish(ins, outs, sems):
        cps = copies(ins, outs, sems)
        for cp in cps:
            cp.wait_recv()
        for cp in cps:
            cp.wait_send()

    return Exchange(tuple(srcs), tuple(_sds((3,) + s.shape[1:], s.dtype) for s in srcs),
                    (pltpu.SemaphoreType.DMA((3, n)),) * 2, start, lambda *a: None, finish)


def run_exchange(ex, *, name):
    n_in, n_out = len(ex.inputs), len(ex.out_shapes)

    def body(*refs):
        parts = refs[:n_in], refs[n_in:n_in + n_out], refs[n_in + n_out:]
        ex.start(*parts)
        ex.relay(*parts)
        ex.finish(*parts)

    hbm = pl.BlockSpec(memory_space=pl.ANY)
    return pl.pallas_call(
        body, name=name, in_specs=[hbm] * n_in, out_specs=[hbm] * n_out, out_shape=list(ex.out_shapes),
        scratch_shapes=list(ex.sems), compiler_params=pltpu.CompilerParams(has_side_effects=True),
    )(*ex.inputs)


def sibling_swap(srcs, *, name):
    n = len(srcs)

    def body(*refs):
        src_refs, out_refs, (send_sems, recv_sems) = refs[:n], refs[n:2 * n], refs[2 * n:]
        x, y, c = _my_place()
        copies = [pltpu.make_async_remote_copy(
            src_ref=src_refs[a], dst_ref=out_refs[a], send_sem=send_sems.at[a], recv_sem=recv_sems.at[a],
            device_id=(x, y, 1 - c), device_id_type=pl.DeviceIdType.MESH) for a in range(n)]
        for cp in copies:
            cp.start()
        for cp in copies:
            cp.wait_recv()
        for cp in copies:
            cp.wait_send()

    hbm = pl.BlockSpec(memory_space=pl.ANY)
    return pl.pallas_call(
        body, name=name, in_specs=[hbm] * n, out_specs=[hbm] * n, out_shape=[_sds(s.shape, s.dtype) for s in srcs],
        scratch_shapes=[pltpu.SemaphoreType.DMA((n,)), pltpu.SemaphoreType.DMA((n,))],
        compiler_params=pltpu.CompilerParams(has_side_effects=True),
    )(*srcs)


def gather_exchange(srcs):
    nch = len(srcs)
    halves = [s.shape[0] // 2 for s in srcs]
    assert all(2 * h == s.shape[0] and h % 16 == 0 for h, s in zip(halves, srcs))
    pieces = [(k, q) for k in range(3) for q in range(nch)]

    def makers(ins, outs, sems):
        ici_send, ici_recv, d2d_send, d2d_recv = sems
        x, y, c = _my_place()
        peers = _other_chips(x, y)

        def rows(core, q):
            return pl.ds(core * halves[q], halves[q])

        def ici(k, q):
            px, py = peers[k]
            return pltpu.make_async_remote_copy(
                src_ref=ins[q].at[rows(c, q)], dst_ref=outs[q].at[k, rows(c, q)], send_sem=ici_send.at[k, q],
                recv_sem=ici_recv.at[k, q], device_id=(px, py, c), device_id_type=pl.DeviceIdType.MESH)

        def d2d(k, q, core):
            return pltpu.make_async_remote_copy(
                src_ref=outs[q].at[k, rows(core, q)], dst_ref=outs[q].at[k, rows(core, q)],
                send_sem=d2d_send.at[k, q], recv_sem=d2d_recv.at[k, q], device_id=(x, y, 1 - c),
                device_id_type=pl.DeviceIdType.MESH)

        return ici, d2d, c

    def start(*refs):
        ici, _, _ = makers(*refs)
        for k, q in pieces:
            ici(k, q).start()

    def relay(*refs):
        ici, d2d, c = makers(*refs)
        for k, q in pieces:
            ici(k, q).wait_recv()
            d2d(k, q, c).start()

    def finish(*refs):
        ici, d2d, c = makers(*refs)
        for k, q in pieces:
            d2d(k, q, 1 - c).wait_recv()
        for k, q in pieces:
            ici(k, q).wait_send()
            d2d(k, q, c).wait_send()

    return Exchange(tuple(srcs), tuple(_sds((3,) + s.shape, s.dtype) for s in srcs),
                    (pltpu.SemaphoreType.DMA((3, nch)),) * 4, start, relay, finish)


def all_sum_small(vec, *, name):
    R, C = vec.shape

    def body(v_ref, out_ref, buf, send_sems, recv_sems):
        x, y, c = _my_place()
        me = 4 * x + 2 * y + c
        buf[me] = v_ref[...]
        copies = []
        for k in range(1, N_DEV):
            px, py, pc = x ^ (k >> 2), y ^ ((k >> 1) & 1), c ^ (k & 1)
            copies.append(pltpu.make_async_remote_copy(
                src_ref=v_ref, dst_ref=buf.at[me], send_sem=send_sems.at[k - 1], recv_sem=recv_sems.at[k - 1],
                device_id=(px, py, pc), device_id_type=pl.DeviceIdType.MESH))
        for cp in copies:
            cp.start()
        for k in range(1, N_DEV):
            px, py, pc = x ^ (k >> 2), y ^ ((k >> 1) & 1), c ^ (k & 1)
            pltpu.make_async_remote_copy(
                src_ref=v_ref, dst_ref=buf.at[4 * px + 2 * py + pc], send_sem=send_sems.at[k - 1],
                recv_sem=recv_sems.at[k - 1], device_id=(px, py, pc), device_id_type=pl.DeviceIdType.MESH).wait_recv()
        for cp in copies:
            cp.wait_send()
        acc = buf[0]
        for d in range(1, N_DEV):
            acc = acc + buf[d]
        out_ref[...] = acc

    return pl.pallas_call(
        body, name=name, in_specs=[pl.BlockSpec(memory_space=pltpu.VMEM)], out_specs=pl.BlockSpec(memory_space=pltpu.VMEM),
        out_shape=_sds((R, C), F32),
        scratch_shapes=[pltpu.VMEM((N_DEV, R, C), F32), pltpu.SemaphoreType.DMA((N_DEV - 1,)),
                        pltpu.SemaphoreType.DMA((N_DEV - 1,))],
        compiler_params=pltpu.CompilerParams(has_side_effects=True, vmem_limit_bytes=VMEM_LIMIT),
    )(vec)


def sum_chips(own, others, *, name, tr=512):
    R, C = own.shape
    tr = _pick(R, (tr, 384, 352, 256, 128, 64, 32, 16))

    def body(o_ref, p_ref, s_ref):
        acc = o_ref[...].astype(F32)
        for k in range(3):
            acc = acc + p_ref[k].astype(F32)
        s_ref[...] = acc

    return pl.pallas_call(
        body, name=name, grid=(R // tr,),
        in_specs=[pl.BlockSpec((tr, C), lambda i: (i, 0)), pl.BlockSpec((3, tr, C), lambda i: (0, i, 0))],
        out_specs=pl.BlockSpec((tr, C), lambda i: (i, 0)), out_shape=_sds((R, C), F32),
        compiler_params=_cparams(("parallel",)),
    )(own, others)


WEIGHTS = ['attn_norm', 'w_in', 'ssd_conv_w', 'ssd_conv_b', 'ssd_dt_bias', 'ssd_a_log', 'ssd_d', 'ssd_norm', 'pool_w',
           'pool_scale', 'mla_q_norm', 'mla_w_uq', 'mla_kv_norm', 'mla_w_ukv', 'w_out', 'ffn_norm', 'ffn_w_up',
           'ffn_conv_w', 'ffn_conv_b', 'ffn_w_down', 'final_norm']
BIG = {'w_in': 2, 'mla_w_uq': 2, 'mla_w_ukv': 2, 'w_out': 1, 'ffn_w_up': 2, 'ffn_w_down': 1}
CONV_SHARDED = ('ssd_conv_w', 'ffn_conv_w')


def _zeros_cols(w, n):
    return jnp.zeros((w.shape[0], n), w.dtype)


def _w_in_to_padded(w):
    return jnp.concatenate([w[:, 0:2560], w[:, 2576:3088], w[:, 3088:3472], w[:, 2560:2576], _zeros_cols(w, 112),
                            w[:, 3472:3728], _zeros_cols(w, 64), w[:, 3728:3760], _zeros_cols(w, 32 + 128)], axis=1)


def _w_in_from_padded(g):
    return jnp.concatenate([g[:, 0:2560], g[:, PDT0:PDT0 + SSD_HEADS], g[:, PU0:PU0 + POOL_WIDTH],
                            g[:, PCQ0:PCQ0 + MLA_Q_RANK], g[:, PCKV0:PCKV0 + MLA_KV_RANK],
                            g[:, PKPE0 + ROPE0:PKPE0 + ROPE0 + MLA_ROPE]], axis=1)


def _w_uq_to_padded(w):
    r = w.reshape(MLA_Q_RANK, MLA_HEADS, MLA_QK)
    return jnp.pad(r, ((0, 0), (0, 0), (0, HEAD_W - MLA_QK))).reshape(MLA_Q_RANK, MLA_PAD)


def _w_uq_from_padded(g):
    return g.reshape(MLA_Q_RANK, MLA_HEADS, HEAD_W)[:, :, :MLA_QK].reshape(MLA_Q_RANK, MLA_HEADS * MLA_QK)


def _w_ukv_to_padded(w):
    r = w.reshape(MLA_KV_RANK, MLA_HEADS, MLA_NOPE + MLA_V)
    pad = lambda t: jnp.pad(t, ((0, 0), (0, 0), (0, HEAD_W - t.shape[2]))).reshape(MLA_KV_RANK, MLA_PAD)
    return jnp.concatenate([pad(r[:, :, :MLA_NOPE]), pad(r[:, :, MLA_NOPE:])], axis=1)


def _w_ukv_from_padded(g):
    kk = g[:, :MLA_PAD].reshape(MLA_KV_RANK, MLA_HEADS, HEAD_W)[:, :, :MLA_NOPE]
    vv = g[:, MLA_PAD:].reshape(MLA_KV_RANK, MLA_HEADS, HEAD_W)[:, :, :MLA_V]
    return jnp.concatenate([kk, vv], axis=2).reshape(MLA_KV_RANK, MLA_HEADS * (MLA_NOPE + MLA_V))


def _w_out_to_padded(w):
    att = w[SSD_WIDTH + POOL_WIDTH:].reshape(MLA_HEADS, MLA_V, D_MODEL)
    att = jnp.pad(att, ((0, 0), (0, HEAD_W - MLA_V), (0, 0))).reshape(MLA_PAD, D_MODEL)
    return jnp.concatenate([w[:SSD_WIDTH + POOL_WIDTH], att], axis=0)


def _w_out_from_padded(g):
    att = g[SSD_WIDTH + POOL_WIDTH:].reshape(MLA_HEADS, HEAD_W, D_MODEL)[:, :MLA_V].reshape(MLA_WIDTH, D_MODEL)
    return jnp.concatenate([g[:SSD_WIDTH + POOL_WIDTH], att], axis=0)


def _pad_lanes(v, n=LANE):
    return jnp.pad(v.reshape(1, -1), ((0, 0), (0, n - v.size)))


def _pack_rows(parts, cols, dtype, row_multiple=16):
    flat = jnp.concatenate([p.astype(dtype).reshape(-1) for p in parts])
    rows = -(-flat.size // (cols * row_multiple)) * row_multiple
    return jnp.pad(flat, (0, rows * cols - flat.size)).reshape(rows, cols)


def _unpack_rows(packed, shapes):
    flat = packed.reshape(-1)
    out, at = [], 0
    for s in shapes:
        n = math.prod(s)
        out.append(flat[at:at + n].reshape(s))
        at += n
    return out


def _split_for_chips(g, axis):
    a, b = g.shape
    if axis == 0:
        return g.reshape(N_CHIPS, a // N_CHIPS, b)
    return g.reshape(a, N_CHIPS, b // N_CHIPS).transpose(1, 0, 2)


_MATMUL_OPERANDS = {'w_in': ('w_in_p', _w_in_to_padded), 'mla_w_uq': ('w_uq_p', _w_uq_to_padded),
                    'mla_w_ukv': ('w_ukv_p', _w_ukv_to_padded), 'w_out': ('w_out_p', _w_out_to_padded),
                    'ffn_w_up': ('w_up', lambda a: a), 'ffn_w_down': ('w_down', lambda a: a)}


def _matmul_weights(full):
    return {_MATMUL_OPERANDS[k][0]: _MATMUL_OPERANDS[k][1](a) for k, a in full.items()}


def _layer_weights(full, small, l):
    w = _matmul_weights(full)
    for k in ('attn_norm', 'ssd_conv_w', 'ssd_conv_b', 'ssd_norm', 'pool_w', 'pool_scale', 'mla_q_norm', 'mla_kv_norm',
              'ffn_norm', 'ffn_conv_w', 'ffn_conv_b'):
        w[k] = small[k][l]
    w['dtb'] = _pad_lanes(small['ssd_dt_bias'][l])
    w['alog'] = _pad_lanes(small['ssd_a_log'][l])
    w['dchan'] = jnp.repeat(small['ssd_d'][l], SSD_HEAD_DIM).reshape(1, SSD_WIDTH)
    w['ssd_norm'] = w['ssd_norm'].reshape(1, SSD_WIDTH)
    return w


def _layer_fwd(x, pos, invf, w, S, l, hosted=None, scan_hosted=None, late_weights=None, post_weights=None,
               norm_hosted=None, first_weights=None):
    n = lambda s: f"{s}_l{l}"
    h1 = rmsnorm_fwd(x, w['attn_norm'], name=n("attn_norm"), hosted=norm_hosted)
    if norm_hosted is not None:
        w = {**w, **first_weights(h1[1:])}
        h1 = h1[0]
    proj = matmul(h1, w['w_in_p'], name=n("w_in"))
    xc = ssd_conv_fwd(proj, w['ssd_conv_w'], w['ssd_conv_b'], S, name=n("ssd_conv"))
    ypre, yssd, hin, *arrived = ssd_fwd(proj, xc, w['dtb'], w['alog'], w['dchan'], w['ssd_norm'], S, name=n("ssd_scan"),
                                        hosted=scan_hosted)
    if late_weights:
        w = {**w, **late_weights(arrived)}
    ypool, pooled = pool_fwd(proj, w['pool_w'], w['pool_scale'], S, name=n("pool"))
    q, k, v, cqn, ckvn = mla_prep_fwd(proj, pos, invf, w['mla_q_norm'], w['w_uq_p'], w['mla_kv_norm'], w['w_ukv_p'],
                                      name=n("mla_prep"))
    o, lse, *exchanged = flash_fwd(q, k, v, S, name=n("attention"), hosted=hosted)
    if post_weights:
        w = {**w, **post_weights(exchanged)}
    mix = jnp.concatenate([yssd, ypool, o.astype(MXU_DTYPE)], axis=1)
    x2 = matmul(mix, w['w_out_p'], res=x, name=n("w_out"))
    h2 = rmsnorm_fwd(x2, w['ffn_norm'], name=n("ffn_norm"))
    up = matmul(h2, w['w_up'], name=n("ffn_up"))
    act = ffn_conv_gate_fwd(up, w['ffn_conv_w'], w['ffn_conv_b'], S, name=n("ffn_conv_gate"))
    x3 = matmul(act, w['w_down'], res=x2, name=n("ffn_down"))
    saved = dict(x=x, h1=h1, proj=proj, xc=xc, ypre=ypre, hin=hin, pooled=pooled, q=q, k=k, v=v, cqn=cqn, ckvn=ckvn,
                 o=o, lse=lse, mix=mix, x2=x2, h2=h2, up=up, act=act)
    return x3, saved, w, exchanged


def _layer_bwd(dx3, pos, invf, w, s, S, l, host=None, late_host=None):
    n = lambda t: f"{t}_l{l}"
    g = {}
    dact = matmul(dx3, w['w_down'], nt=True, name=n("d_ffn_down"))
    g['ffn_w_down'] = matmul_tn(s['act'], dx3, name=n("g_ffn_down"))
    dup_g, dup_v, st = ffn_conv_gate_bwd(s['up'], w['ffn_conv_w'], w['ffn_conv_b'], dact, S, name=n("d_ffn_conv_gate"))
    g['ffn_conv_w'], g['ffn_conv_b'] = st[:FFN_CONV], st[FFN_CONV]
    dh2 = matmul(dup_g, w['w_up'], nt=True, kblock=0, name=n("d_ffn_up_g"))
    dh2 = matmul(dup_v, w['w_up'], nt=True, kblock=1, res=dh2, name=n("d_ffn_up_v"))
    g['ffn_w_up'] = jnp.concatenate([matmul_tn(s['h2'], dup_g, name=n("g_ffn_up_g")),
                                     matmul_tn(s['h2'], dup_v, name=n("g_ffn_up_v"))], axis=1)
    dx2, gn = rmsnorm_bwd(s['x2'], w['ffn_norm'], dh2, dx3, name=n("d_ffn_norm"))
    g['ffn_norm'] = gn[0]
    dmix = matmul(dx2, w['w_out_p'], nt=True, name=n("d_w_out"))
    g['w_out'] = _w_out_from_padded(matmul_tn(s['mix'], dx2, name=n("g_w_out")))
    dxc, ddt, dz, sm, gsn = ssd_bwd(s['proj'], s['xc'], s['ypre'], s['hin'], dmix, w['dtb'], w['alog'], w['dchan'],
                                    w['ssd_norm'], S, name=n("d_ssd_scan"))
    g['ssd_a_log'], g['ssd_dt_bias'], g['ssd_d'] = sm[0, :SSD_HEADS], sm[1, :SSD_HEADS], sm[2, :SSD_HEADS]
    g['ssd_norm'] = gsn[0]
    dxbc, st = ssd_conv_bwd(s['proj'], w['ssd_conv_w'], w['ssd_conv_b'], dxc, S, name=n("d_ssd_conv"))
    g['ssd_conv_w'], g['ssd_conv_b'] = st[:SSD_CONV], st[SSD_CONV]
    du, g['pool_w'], gps = pool_bwd(dmix, s['pooled'], w['pool_w'], w['pool_scale'], S, name=n("d_pool"))
    g['pool_scale'] = gps[0]
    dq, dk, dv, *exchanged = flash_bwd(s['q'], s['k'], s['v'], s['o'], s['lse'], dmix, S, name=n("d_attention"),
                                       hosted=host(g) if host else None)
    dqp, dkvp, dcq, dckv, dkpe, gqn, gkn = mla_prep_bwd(s['proj'], pos, invf, w['mla_q_norm'], w['w_uq_p'],
                                                        w['mla_kv_norm'], w['w_ukv_p'], dq, dk, dv, name=n("d_mla_prep"))
    g['mla_q_norm'], g['mla_kv_norm'] = gqn[0], gkn[0]
    g['mla_w_uq'] = _w_uq_from_padded(matmul_tn(s['cqn'], dqp, name=n("g_w_uq")))
    g['mla_w_ukv'] = _w_ukv_from_padded(matmul_tn(s['ckvn'], dkvp, name=n("g_w_ukv")))
    dproj = jnp.concatenate([dz, dxbc, du, dcq, ddt, dckv, dkpe, jnp.zeros_like(dkpe)], axis=1)
    g['w_in'] = _w_in_from_padded(matmul_tn(s['h1'], dproj, name=n("g_w_in")))
    dh1 = matmul(dproj, w['w_in_p'], nt=True, name=n("d_w_in"), hosted=late_host(g) if late_host else None)
    dh1, late_exchanged = (dh1[0], dh1[1:]) if late_host else (dh1, [])
    dx, gn = rmsnorm_bwd(s['x'], w['attn_norm'], dh1, dx2, name=n("d_attn_norm"))
    g['attn_norm'] = gn[0]
    return dx, g, exchanged, late_exchanged


def _rope_inputs(positions):
    pos = positions.reshape(-1, 1).astype(F32)
    inv_freq = ROPE_THETA ** (-jnp.arange(0, MLA_ROPE, 2, dtype=F32) / MLA_ROPE)
    invf = jnp.concatenate([jnp.zeros((ROPE0,), F32), inv_freq, inv_freq,
                            jnp.zeros((HEAD_W - ROPE0 - MLA_ROPE,), F32)]).reshape(1, HEAD_W)
    return pos, invf


EARLY_GRADS = ('w_out', 'ffn_w_up', 'ffn_w_down')


def kernel(x, positions, attn_norm, w_in, ssd_conv_w, ssd_conv_b, ssd_dt_bias, ssd_a_log, ssd_d, ssd_norm, pool_w, pool_scale, mla_q_norm, mla_w_uq, mla_kv_norm, mla_w_ukv, w_out, ffn_norm, ffn_w_up, ffn_conv_w, ffn_conv_b, ffn_w_down, final_norm, loss_target, m_attn_norm, m_w_in, m_ssd_conv_w, m_ssd_conv_b, m_ssd_dt_bias, m_ssd_a_log, m_ssd_d, m_ssd_norm, m_pool_w, m_pool_scale, m_mla_q_norm, m_mla_w_uq, m_mla_kv_norm, m_mla_w_ukv, m_w_out, m_ffn_norm, m_ffn_w_up, m_ffn_conv_w, m_ffn_conv_b, m_ffn_w_down, m_final_norm, v_attn_norm, v_w_in, v_ssd_conv_w, v_ssd_conv_b, v_ssd_dt_bias, v_ssd_a_log, v_ssd_d, v_ssd_norm, v_pool_w, v_pool_scale, v_mla_q_norm, v_mla_w_uq, v_mla_kv_norm, v_mla_w_ukv, v_w_out, v_ffn_norm, v_ffn_w_up, v_ffn_conv_w, v_ffn_conv_b, v_ffn_w_down, v_final_norm):
    wv = dict(zip(WEIGHTS, (attn_norm, w_in, ssd_conv_w, ssd_conv_b, ssd_dt_bias, ssd_a_log, ssd_d, ssd_norm, pool_w,
                            pool_scale, mla_q_norm, mla_w_uq, mla_kv_norm, mla_w_ukv, w_out, ffn_norm, ffn_w_up,
                            ffn_conv_w, ffn_conv_b, ffn_w_down, final_norm)))
    mv = dict(zip(WEIGHTS, (m_attn_norm, m_w_in, m_ssd_conv_w, m_ssd_conv_b, m_ssd_dt_bias, m_ssd_a_log, m_ssd_d,
                            m_ssd_norm, m_pool_w, m_pool_scale, m_mla_q_norm, m_mla_w_uq, m_mla_kv_norm, m_mla_w_ukv,
                            m_w_out, m_ffn_norm, m_ffn_w_up, m_ffn_conv_w, m_ffn_conv_b, m_ffn_w_down, m_final_norm)))
    vv = dict(zip(WEIGHTS, (v_attn_norm, v_w_in, v_ssd_conv_w, v_ssd_conv_b, v_ssd_dt_bias, v_ssd_a_log, v_ssd_d,
                            v_ssd_norm, v_pool_w, v_pool_scale, v_mla_q_norm, v_mla_w_uq, v_mla_kv_norm, v_mla_w_ukv,
                            v_w_out, v_ffn_norm, v_ffn_w_up, v_ffn_conv_w, v_ffn_conv_b, v_ffn_w_down, v_final_norm)))
    Bl, S, D = x.shape
    chip = 2 * lax.axis_index("x") + lax.axis_index("y")
    core = lax.axis_index("c")

    big_names = list(BIG)
    first_names = ['w_in']
    scan_names = ['mla_w_uq', 'mla_w_ukv']
    post_names = [k for k in big_names if k not in first_names + scan_names]

    def shards(l, names):
        return [wv[k][l].astype(MXU_DTYPE) for k in names]

    def whole_weights(names, own, others):
        rel = [relation_of(j, chip) for j in range(N_CHIPS)]
        return {k: jnp.concatenate(
            [jnp.where(r < 0, mine, jnp.where(r == 0, theirs[0], jnp.where(r == 1, theirs[1], theirs[2]))) for r in rel],
            axis=BIG[k] - 1) for k, mine, theirs in zip(names, own, others)}

    placed = []
    for k in CONV_SHARDED:
        sh = wv[k]
        whole = jnp.zeros(sh.shape[:-1] + (sh.shape[-1] * N_CHIPS,), F32)
        whole = lax.dynamic_update_slice_in_dim(whole, sh, chip * sh.shape[-1], axis=sh.ndim - 1)
        placed.append(jnp.where(core == 1, whole, 0.0))
    conv_full = _unpack_rows(all_sum_small(_pack_rows(placed, LANE, F32), name="gather_conv_weights"),
                             [p.shape for p in placed])
    small = {k: wv[k] for k in WEIGHTS if k not in BIG}
    small.update(dict(zip(CONV_SHARDED, conv_full)))

    T = Bl * S
    pos, invf = _rope_inputs(positions)
    group_a = [(k, 1) for k in big_names] + [(k, 0) for k in EARLY_GRADS]
    group_b = [(k, 0) for k in big_names if k not in EARLY_GRADS]

    def scatter_of(group, layer_grads):
        send = [_split_for_chips(layer_grads[l][k], BIG[k] - 1) for k, l in group]
        return send, scatter_exchange(send)

    layer_grads = [None] * DEPTH
    sent = {}
    w0 = _layer_weights({}, small, 0)
    h, saved0, w0, arrived = _layer_fwd(
        x.reshape(T, D), pos, invf, w0, S, 0,
        norm_hosted=gather_exchange(shards(0, first_names)),
        first_weights=lambda got: _matmul_weights(whole_weights(first_names, shards(0, first_names), got)),
        scan_hosted=gather_exchange(shards(0, scan_names)),
        late_weights=lambda got: _matmul_weights(whole_weights(scan_names, shards(0, scan_names), got)),
        hosted=gather_exchange(shards(0, post_names) + shards(1, big_names)),
        post_weights=lambda got: _matmul_weights(whole_weights(post_names, shards(0, post_names), got[:len(post_names)])))
    others1 = arrived[len(post_names):]
    w1 = _layer_weights(whole_weights(big_names, shards(1, big_names), others1), small, 1)
    h, saved1, w1, _ = _layer_fwd(h, pos, invf, w1, S, 1)
    loss, dh, g_final_norm = final_loss(h, small['final_norm'], loss_target.reshape(T, D))
    dh, layer_grads[1], _, _ = _layer_bwd(dh, pos, invf, w1, saved1, S, 1)

    def host_a(early):
        layer_grads[0] = early
        sent['a'], ex = scatter_of(group_a, layer_grads)
        return ex

    def host_b(_):
        sent['b'], ex = scatter_of(group_b, layer_grads)
        return ex

    dx, layer_grads[0], others_a, others_b = _layer_bwd(dh, pos, invf, w0, saved0, S, 0, host=host_a, late_host=host_b)
    small_names = [k for k in WEIGHTS if k not in BIG]
    grads = {k: jnp.stack([layer_grads[l][k] for l in range(DEPTH)]) for k in small_names if k != 'final_norm'}
    grads['final_norm'] = g_final_norm[0]

    pieces = [{}, {}]
    for tag, group, others in (('a', group_a, others_a), ('b', group_b, others_b)):
        mine = [sum_chips(lax.dynamic_index_in_dim(s, chip, 0, keepdims=False), o, name=f"sum_chips_{k}_l{l}")
                for s, o, (k, l) in zip(sent[tag], others, group)]
        theirs = sibling_swap(mine, name=f"swap_core_sums_{tag}")
        pieces[0].update(dict(zip(group, mine)))
        pieces[1].update(dict(zip(group, theirs)))
    small_sum = all_sum_small(_pack_rows([grads[k] for k in small_names] + [loss[0, :1]], LANE, F32), name="sum_small_grads")
    summed = _unpack_rows(small_sum, [grads[k].shape for k in small_names] + [(1,)])
    loss_total = summed[-1].reshape(())
    g_small = dict(zip(small_names, summed[:-1]))
    for k in CONV_SHARDED:
        n = wv[k].shape[-1]
        g_small[k] = lax.dynamic_slice_in_dim(g_small[k], chip * n, n, axis=g_small[k].ndim - 1)

    out_g, out_d, out_m, out_v = {}, {}, {}, {}
    for k in big_names:
        g_layers = [[pieces[0][(k, l)], pieces[1][(k, l)]] for l in range(DEPTH)]
        out_g[k], out_d[k], out_m[k], out_v[k] = adamw_layers(wv[k], g_layers, mv[k], vv[k], name=f"adamw_{k}")
    at_least_2d = lambda a: a.reshape(1, -1) if a.ndim == 1 else a
    res = adamw_small(*[[at_least_2d(d[k]) for k in small_names] for d in (wv, g_small, mv, vv)], name="adamw_small")
    out_g.update(g_small)
    for dst, r in zip((out_d, out_m, out_v), res):
        dst.update({k: a.reshape(wv[k].shape) for k, a in zip(small_names, r)})
    return (loss_total, dx.reshape(Bl, S, D), *[out_g[k] for k in WEIGHTS], *[out_d[k] for k in WEIGHTS],
            *[out_m[k] for k in WEIGHTS], *[out_v[k] for k in WEIGHTS])
```
